```python
import math
import jax, jax.numpy as jnp
from jax import lax
import numpy as np

D_MODEL = 2048
BATCH = 32
SEQ = 256
DEPTH = 1
DEC_BATCH = 2
DEC_SEQ = 2048
PAST_LEN = 256

GRID_W = 64
D_MIX = D_MODEL
GLA_WIDTH = D_MIX // 2
GLA_HEADS = 4
GLA_DV = GLA_WIDTH // GLA_HEADS
GLA_DK = GLA_DV // 2
GLA_KDIM = GLA_HEADS * GLA_DK
GLA_RANK = 16
GLA_TAU = 16.0
GLA_CHUNK = 64
HY_WIDTH = D_MIX - GLA_WIDTH
HY_ORDER = 2
HY_SHORT = 3
HY_BANDS = 16
HY_PE_DIM = 1 + 2 * HY_BANDS
HY_FILTER_HIDDEN = 64
HY_TARGET = 1e-2
HY_FAST_PCT = 0.3
HY_SLOW_PCT = 1.5
N_GROUPS = 4
EXP_PER_GROUP = 4
N_EXPERTS = N_GROUPS * EXP_PER_GROUP
TOP_K_IN_GROUP = 2
D_EXPERT = D_MODEL // 4
N_IN_COLS = 2 * GLA_KDIM + 2 * GLA_WIDTH + 2 * GLA_RANK + (HY_ORDER + 1) * HY_WIDTH
EPS = 1e-6

kernel_name = 'hybrid_gla_hyena_hmoe_diffusion_step'


def _rmsnorm(x, g):
    xf = x.astype(jnp.float32)
    y = xf * lax.rsqrt(jnp.mean(xf * xf, axis=-1, keepdims=True) + EPS)
    return (y * g.astype(jnp.float32)).astype(x.dtype)


def _gla_chunked(q, k, v, log_a, s0):
    B, L, H, DK = q.shape
    DV = v.shape[-1]
    C = GLA_CHUNK
    NC = L // C
    q = q.reshape(B, NC, C, H, DK)
    k = k.reshape(B, NC, C, H, DK)
    v = v.reshape(B, NC, C, H, DV)
    b = jnp.cumsum(log_a.reshape(B, NC, C, H, DK), axis=2)
    b_last = b[:, :, -1]
    q_t = q * jnp.exp(b)
    k_t = k * jnp.exp(-b)
    mask = jnp.tril(jnp.ones((C, C), dtype=bool))
    att = jnp.einsum('bnihk,bnjhk->bnhij', q_t, k_t)
    att = jnp.where(mask, att, 0.0)
    o_intra = jnp.einsum('bnhij,bnjhv->bnihv', att, v)
    k_s = k * jnp.exp(b_last[:, :, None] - b)
    upd = jnp.einsum('bnjhk,bnjhv->bnhkv', k_s, v)
    decay = jnp.exp(b_last)

    def step(S, inp):
        d, u = inp
        return d[..., None] * S + u, S

    s_final, s_in = lax.scan(step, s0, (jnp.moveaxis(decay, 1, 0), jnp.moveaxis(upd, 1, 0)))
    s_in = jnp.moveaxis(s_in, 0, 1)
    o_inter = jnp.einsum('bnihk,bnhkv->bnihv', q_t, s_in)
    return (o_intra + o_inter).reshape(B, L, H, DV), s_final


def _short_conv(u, w, b, n_seg):
    B, L, C = u.shape
    seg = L // n_seg
    pad = HY_SHORT // 2
    us = jnp.pad(u.reshape(B, n_seg, seg, C), ((0, 0), (0, 0), (pad, pad), (0, 0)))
    y = sum(us[:, :, j:j + seg] * w[j] for j in range(HY_SHORT)) + b
    return y.reshape(B, L, C)


def _hyena_filters(L, w1, b1, w2, b2, w3, freq):
    f32 = jnp.float32
    t = jnp.arange(L, dtype=f32)
    t01 = t / max(L - 1, 1)
    ang = 2.0 * math.pi * t / L
    bands = jnp.linspace(1e-4, HY_BANDS - 1, HY_BANDS, dtype=f32)
    pe = jnp.concatenate([t01[:, None], jnp.cos(ang[:, None] * bands), -jnp.sin(ang[:, None] * bands)], axis=-1)
    fr = freq.astype(f32)
    hdn = jnp.sin(fr * (pe @ w1.astype(f32) + b1.astype(f32)))
    hdn = jnp.sin(fr * (hdn @ w2.astype(f32) + b2.astype(f32)))
    filt = (hdn @ w3.astype(f32)).reshape(L, 2, HY_ORDER, HY_WIDTH)
    deltas = jnp.abs(jnp.linspace(math.log(HY_TARGET) / HY_SLOW_PCT, math.log(HY_TARGET) / HY_FAST_PCT, HY_WIDTH, dtype=f32))
    filt = filt * jnp.exp(-t01[:, None] * deltas)[:, None, None, :]
    filt = filt / jnp.sum(jnp.abs(filt), axis=(0, 1), keepdims=True)
    full = jnp.concatenate([filt[:, 0], jnp.zeros((1, HY_ORDER, HY_WIDTH), f32), jnp.flip(filt[1:, 1], axis=0)], axis=0)
    return jnp.fft.rfft(full, axis=0)


def _mixer(h, s0_f, s0_b, n_seg, l, p):
    B, L, _ = h.shape
    f32 = jnp.float32
    proj = h @ p['w_in'][l]
    sizes = (GLA_KDIM, GLA_KDIM, GLA_WIDTH, GLA_WIDTH, GLA_RANK, GLA_RANK, (HY_ORDER + 1) * HY_WIDTH)
    q, k, v, g, lr_f, lr_b, u_hy = jnp.split(proj, np.cumsum(sizes)[:-1].tolist(), axis=-1)
    q = q.astype(f32).reshape(B, L, GLA_HEADS, GLA_DK) * (GLA_DK ** -0.5)
    k = k.astype(f32).reshape(B, L, GLA_HEADS, GLA_DK)
    v = v.astype(f32).reshape(B, L, GLA_HEADS, GLA_DV)
    la_f = jax.nn.log_sigmoid((lr_f @ p['w_dec_f'][l] + p['b_dec_f'][l]).astype(f32)).reshape(B, L, GLA_HEADS, GLA_DK) / GLA_TAU
    la_b = jax.nn.log_sigmoid((lr_b @ p['w_dec_b'][l] + p['b_dec_b'][l]).astype(f32)).reshape(B, L, GLA_HEADS, GLA_DK) / GLA_TAU
    o_f, s_f = _gla_chunked(q, k, v, la_f, s0_f.astype(f32))
    o_b, s_b = _gla_chunked(jnp.flip(q, 1), jnp.flip(k, 1), jnp.flip(v, 1), jnp.flip(la_b, 1), s0_b.astype(f32))
    o = _rmsnorm(o_f + jnp.flip(o_b, 1), p['g_gla'][l]).reshape(B, L, GLA_WIDTH)
    y_gla = (o * jax.nn.silu(g.astype(f32))).astype(h.dtype)
    u_hy = _short_conv(u_hy, p['hy_conv_w'][l], p['hy_conv_b'][l], n_seg)
    parts = jnp.split(u_hy.astype(f32), HY_ORDER + 1, axis=-1)
    filt_f = _hyena_filters(L, p['hy_w1'][l], p['hy_b1'][l], p['hy_w2'][l], p['hy_b2'][l], p['hy_w3'][l], p['hy_freq'][l])
    z = parts[0]
    for n in range(HY_ORDER):
        conv = jnp.fft.irfft(jnp.fft.rfft(z, n=2 * L, axis=1) * filt_f[None, :, n], n=2 * L, axis=1)[:, :L]
        z = parts[n + 1] * (conv + p['hy_bias'][l, n].astype(f32) * z)
    y_hy = _rmsnorm(z, p['g_hy'][l]).astype(h.dtype)
    y = jnp.concatenate([y_gla, y_hy], axis=-1) @ p['w_out'][l]
    return y, s_f, s_b


def _moe(h, l, p):
    B, L, D = h.shape
    f32 = jnp.float32
    t = h.reshape(B * L, D)
    lg = (t @ p['w_router_grp'][l] + p['b_router_grp'][l]).astype(f32)
    pg = jax.nn.softmax(lg, axis=-1)
    grp = jnp.argmax(lg, axis=-1)
    p_grp = jnp.take_along_axis(pg, grp[:, None], axis=1)[:, 0]
    le = (t @ p['w_router_exp'][l] + p['b_router_exp'][l]).astype(f32).reshape(-1, N_GROUPS, EXP_PER_GROUP)
    le_sel = jnp.take_along_axis(le, grp[:, None, None], axis=1)[:, 0]
    top_v, top_i = lax.top_k(jax.nn.softmax(le_sel, axis=-1), TOP_K_IN_GROUP)
    top_v = top_v / jnp.sum(top_v, axis=-1, keepdims=True)
    ids = grp[:, None] * EXP_PER_GROUP + top_i
    gates = jnp.sum(jax.nn.one_hot(ids, N_EXPERTS, dtype=f32) * (p_grp[:, None] * top_v)[..., None], axis=1)
    out = jnp.zeros((B * L, D), f32)
    for e in range(N_EXPERTS):
        he = jax.nn.silu(t @ p['w_exp_gate'][l, e]) * (t @ p['w_exp_up'][l, e])
        out = out + gates[:, e:e + 1] * (he @ p['w_exp_down'][l, e]).astype(f32)
    return out.reshape(B, L, D).astype(h.dtype)


def _layer(x, cond, s0_f, s0_b, n_seg, l, p):
    mod = jax.nn.silu(cond) @ p['w_ada'][l] + p['b_ada'][l]
    sh1, sc1, gt1, sh2, sc2, gt2 = jnp.split(mod[:, None, :], 6, axis=-1)
    h = _rmsnorm(x, p['g_pre_mix'][l]) * (1.0 + sc1) + sh1
    y, s_f, s_b = _mixer(h, s0_f, s0_b, n_seg, l, p)
    x = x + gt1 * _rmsnorm(y, p['g_post_mix'][l])
    h = _rmsnorm(x, p['g_pre_ffn'][l]) * (1.0 + sc2) + sh2
    y = _moe(h, l, p)
    x = x + gt2 * _rmsnorm(y, p['g_post_ffn'][l])
    return x, s_f, s_b


def setup_inputs(seed: int = 0) -> dict:
    key = jax.random.key(seed)
    ks = iter(jax.random.split(key, 48))

    def nrm(shape, scale):
        return jax.random.normal(next(ks), shape, jnp.float32) * scale

    def gain(shape):
        return 1.0 + nrm(shape, 0.05)

    D = D_MODEL
    return {
        'x_prompt': nrm((BATCH, SEQ, D), 1.0),
        'x_sample': nrm((DEC_BATCH, DEC_SEQ, D), 1.0),
        'c': nrm((DEC_BATCH, D), 1.0),
        'state_gla_fwd': nrm((DEC_BATCH, DEPTH, GLA_HEADS, GLA_DK, GLA_DV), 1.0),
        'state_gla_bwd': nrm((DEC_BATCH, DEPTH, GLA_HEADS, GLA_DK, GLA_DV), 1.0),
        'c_ctx': nrm((D,), 1.0),
        'w_ada': nrm((DEPTH, D, 6 * D), 0.5 * D ** -0.5),
        'b_ada': nrm((DEPTH, 6 * D), 0.02),
        'g_pre_mix': gain((DEPTH, D)),
        'g_post_mix': gain((DEPTH, D)),
        'g_pre_ffn': gain((DEPTH, D)),
        'g_post_ffn': gain((DEPTH, D)),
        'w_in': nrm((DEPTH, D, N_IN_COLS), D ** -0.5),
        'w_dec_f': nrm((DEPTH, GLA_RANK, GLA_KDIM), GLA_RANK ** -0.5),
        'b_dec_f': nrm((DEPTH, GLA_KDIM), 0.1),
        'w_dec_b': nrm((DEPTH, GLA_RANK, GLA_KDIM), GLA_RANK ** -0.5),
        'b_dec_b': nrm((DEPTH, GLA_KDIM), 0.1),
        'g_gla': gain((DEPTH, GLA_DV)),
        'hy_conv_w': nrm((DEPTH, HY_SHORT, (HY_ORDER + 1) * HY_WIDTH), HY_SHORT ** -0.5),
        'hy_conv_b': nrm((DEPTH, (HY_ORDER + 1) * HY_WIDTH), 0.02),
        'hy_w1': nrm((DEPTH, HY_PE_DIM, HY_FILTER_HIDDEN), HY_PE_DIM ** -0.5),
        'hy_b1': nrm((DEPTH, HY_FILTER_HIDDEN), 0.1),
        'hy_w2': nrm((DEPTH, HY_FILTER_HIDDEN, HY_FILTER_HIDDEN), HY_FILTER_HIDDEN ** -0.5),
        'hy_b2': nrm((DEPTH, HY_FILTER_HIDDEN), 0.1),
        'hy_w3': nrm((DEPTH, HY_FILTER_HIDDEN, 2 * HY_ORDER * HY_WIDTH), HY_FILTER_HIDDEN ** -0.5),
        'hy_freq': gain((DEPTH, HY_FILTER_HIDDEN)),
        'hy_bias': nrm((DEPTH, HY_ORDER, HY_WIDTH), 0.1),
        'g_hy': gain((DEPTH, HY_WIDTH)),
        'w_out': nrm((DEPTH, D_MIX, D), D_MIX ** -0.5),
        'w_router_grp': nrm((DEPTH, D, N_GROUPS), D ** -0.5),
        'b_router_grp': nrm((DEPTH, N_GROUPS), 0.01),
        'w_router_exp': nrm((DEPTH, D, N_EXPERTS), D ** -0.5),
        'b_router_exp': nrm((DEPTH, N_EXPERTS), 0.01),
        'w_exp_gate': nrm((DEPTH, N_EXPERTS, D, D_EXPERT), D ** -0.5),
        'w_exp_up': nrm((DEPTH, N_EXPERTS, D, D_EXPERT), D ** -0.5),
        'w_exp_down': nrm((DEPTH, N_EXPERTS, D_EXPERT, D), D_EXPERT ** -0.5),
    }


def reference(x_prompt, x_sample, c, state_gla_fwd, state_gla_bwd, c_ctx, w_ada, b_ada,
              g_pre_mix, g_post_mix, g_pre_ffn, g_post_ffn, w_in, w_dec_f, b_dec_f, w_dec_b, b_dec_b,
              g_gla, hy_conv_w, hy_conv_b, hy_w1, hy_b1, hy_w2, hy_b2, hy_w3, hy_freq, hy_bias, g_hy,
              w_out, w_router_grp, b_router_grp, w_router_exp, b_router_exp, w_exp_gate, w_exp_up, w_exp_down):
    p = dict(w_ada=w_ada, b_ada=b_ada, g_pre_mix=g_pre_mix, g_post_mix=g_post_mix, g_pre_ffn=g_pre_ffn,
             g_post_ffn=g_post_ffn, w_in=w_in, w_dec_f=w_dec_f, b_dec_f=b_dec_f, w_dec_b=w_dec_b, b_dec_b=b_dec_b,
             g_gla=g_gla, hy_conv_w=hy_conv_w, hy_conv_b=hy_conv_b, hy_w1=hy_w1, hy_b1=hy_b1, hy_w2=hy_w2,
             hy_b2=hy_b2, hy_w3=hy_w3, hy_freq=hy_freq, hy_bias=hy_bias, g_hy=g_hy, w_out=w_out,
             w_router_grp=w_router_grp, b_router_grp=b_router_grp, w_router_exp=w_router_exp,
             b_router_exp=b_router_exp, w_exp_gate=w_exp_gate, w_exp_up=w_exp_up, w_exp_down=w_exp_down)
    rows = x_sample.shape[1] // GRID_W
    zero_state = jnp.zeros((x_prompt.shape[0], GLA_HEADS, GLA_DK, GLA_DV), jnp.float32)
    y_p = x_prompt
    y_s = x_sample
    new_f = []
    new_b = []
    for l in range(DEPTH):
        y_p, s_f, s_b = _layer(y_p, c_ctx[None, :], zero_state, zero_state, 1, l, p)
        new_f.append(s_f.astype(x_prompt.dtype))
        new_b.append(s_b.astype(x_prompt.dtype))
        y_s, _, _ = _layer(y_s, c, state_gla_fwd[:, l], state_gla_bwd[:, l], rows, l, p)
    new_state_gla_fwd = jnp.stack(new_f, axis=1)
    new_state_gla_bwd = jnp.stack(new_b, axis=1)
    return (y_p, y_s, new_state_gla_fwd, new_state_gla_bwd)
```

```python
import functools
import math

import jax
import jax.numpy as jnp
from jax import lax
from jax.experimental import pallas as pl
from jax.experimental.pallas import tpu as pltpu

F32 = jnp.float32
BF16 = jnp.bfloat16

GRID_W = 64
GLA_HEADS = 4
GLA_RANK = 16
GLA_TAU = 16.0
GLA_CHUNK = 64
HY_ORDER = 2
HY_SHORT = 3
HY_BANDS = 16
HY_TARGET = 1e-2
HY_FAST_PCT = 0.3
HY_SLOW_PCT = 1.5
N_GROUPS = 4
EXP_PER_GROUP = 4
N_EXPERTS = N_GROUPS * EXP_PER_GROUP
EPS = 1e-6

LANES = 128
VMEM_LIMIT = 56 << 20


def _cparams(sem):
    return pltpu.CompilerParams(dimension_semantics=sem, vmem_limit_bytes=VMEM_LIMIT)


def _dot(a, b):
    return jnp.dot(a, b, preferred_element_type=F32)


def _dot_nt(a, b):
    return lax.dot_general(a, b, (((1,), (1,)), ((), ())), preferred_element_type=F32)


def _dot_tn(a, b):
    return lax.dot_general(a, b, (((0,), (0,)), ((), ())), preferred_element_type=F32)


def _split2(x):
    hi = x.astype(BF16)
    lo = (x - hi.astype(F32)).astype(BF16)
    return hi, lo


def _dot_hp(a, b):
    ah, al = _split2(a)
    bh, bl = _split2(b)
    return _dot(ah, bh) + (_dot(ah, bl) + _dot(al, bh))


def _dot_exact_lhs(t, x):
    x0 = x.astype(BF16)
    r = x - x0.astype(F32)
    x1 = r.astype(BF16)
    x2 = (r - x1.astype(F32)).astype(BF16)
    return _dot(t, x0) + (_dot(t, x1) + _dot(t, x2))


def _rms(x, g):
    return x * lax.rsqrt(jnp.mean(x * x, axis=-1, keepdims=True) + EPS) * g


def _silu(x):
    return x / (1.0 + jnp.exp(-x))


def _ada_kernel(c_ref, w_ref, b_ref, o_ref):
    s = _silu(c_ref[...]).astype(BF16)
    o_ref[...] = _dot(s, w_ref[...].astype(BF16)) + b_ref[...]


def _ada_mod(cond, w_ada, b_ada):
    rows, d = cond.shape
    n = w_ada.shape[1]
    tn = 1024
    return pl.pallas_call(
        _ada_kernel,
        grid=(n // tn,),
        in_specs=[pl.BlockSpec((rows, d), lambda j: (0, 0)),
                  pl.BlockSpec((d, tn), lambda j: (0, j)),
                  pl.BlockSpec((1, tn), lambda j: (0, j))],
        out_specs=pl.BlockSpec((rows, tn), lambda j: (0, j)),
        out_shape=jax.ShapeDtypeStruct((rows, n), F32),
        compiler_params=_cparams(("arbitrary",)),
        name="ada_mod",
    )(cond, w_ada, b_ada.reshape(1, n))


def _premix_kernel(x_ref, mod_ref, g_ref, w_ref, wlh_ref, wll_ref, o_ref, lr_ref, h_scr):
    @pl.when(pl.program_id(1) == 0)
    def _():
        h = _rms(x_ref[...], g_ref[...]) * (1.0 + mod_ref[0, 1:2, :]) + mod_ref[0, 0:1, :]
        hh, hl = _split2(h)
        h_scr[...] = hh
        lr_ref[...] = _dot(hh, wlh_ref[...]) + (_dot(hh, wll_ref[...]) + _dot(hl, wlh_ref[...]))

    o_ref[...] = _dot(h_scr[...], w_ref[...]).astype(o_ref.dtype)


def _premix_proj(x, mod, mod_row, g, w_main, wlr_hi, wlr_lo, tm, tn):
    n, d = x.shape
    nc = w_main.shape[1]
    return pl.pallas_call(
        _premix_kernel,
        grid=(n // tm, nc // tn),
        in_specs=[pl.BlockSpec((tm, d), lambda i, j: (i, 0)),
                  pl.BlockSpec((1, 6, d), lambda i, j: (mod_row(i * tm), 0, 0)),
                  pl.BlockSpec((1, d), lambda i, j: (0, 0)),
                  pl.BlockSpec((d, tn), lambda i, j: (0, j)),
                  pl.BlockSpec((d, LANES), lambda i, j: (0, 0)),
                  pl.BlockSpec((d, LANES), lambda i, j: (0, 0))],
        out_specs=[pl.BlockSpec((tm, tn), lambda i, j: (i, j)),
                   pl.BlockSpec((tm, LANES), lambda i, j: (i, 0))],
        out_shape=[jax.ShapeDtypeStruct((n, nc), BF16),
                   jax.ShapeDtypeStruct((n, LANES), F32)],
        scratch_shapes=[pltpu.VMEM((tm, d), BF16)],
        compiler_params=_cparams(("arbitrary", "arbitrary")),
        name="premix_proj",
    )(x, mod, g, w_main, wlr_hi, wlr_lo)


def _log_sigmoid(x):
    return jnp.minimum(x, 0.0) - jnp.log1p(jnp.exp(-jnp.abs(x)))


def _gla_kernel(q_ref, k_ref, v_ref, g_ref, lr_ref, wdf_ref, bdf_ref, wdb_ref, bdb_ref, gg_ref,
                s0f_ref, s0b_ref, y_ref, sf_ref, sb_ref,
                laf_scr, lab_scr, qf_scr, qb_scr, o_scr, uf_scr, ub_scr, df_scr, db_scr, *, seq, dk):
    c = GLA_CHUNK
    n_chunks = seq // c
    scale = dk ** -0.5

    lr = lr_ref[0]
    laf_scr[...] = _log_sigmoid(_dot_hp(lr, wdf_ref[0]) + bdf_ref[...]) / GLA_TAU
    lab_scr[...] = _log_sigmoid(_dot_hp(lr, wdb_ref[0]) + bdb_ref[...]) / GLA_TAU

    row = lax.broadcasted_iota(jnp.int32, (c, c), 0)
    col = lax.broadcasted_iota(jnp.int32, (c, c), 1)
    lower = row >= col
    upper = col >= row
    tl = lower.astype(BF16)
    tu = upper.astype(BF16)

    def chunk_local(n, carry):
        sl = pl.ds(pl.multiple_of(n * c, c), c)
        bf = _dot_exact_lhs(tl, laf_scr[sl, :])
        bb = _dot_exact_lhs(tu, lab_scr[sl, :])
        q = q_ref[0, sl, :].astype(F32) * scale
        k = k_ref[0, sl, :].astype(F32)
        v = v_ref[0, sl, :]
        bf_end = bf[c - 1:c, :]
        bb_end = bb[0:1, :]
        qf = (q * jnp.exp(bf)).astype(BF16)
        kf = (k * jnp.exp(-bf)).astype(BF16)
        ksf = (k * jnp.exp(bf_end - bf)).astype(BF16)
        qb = (q * jnp.exp(bb)).astype(BF16)
        kb = (k * jnp.exp(-bb)).astype(BF16)
        ksb = (k * jnp.exp(bb_end - bb)).astype(BF16)
        att = jnp.where(lower, _dot_nt(qf, kf), 0.0) + jnp.where(upper, _dot_nt(qb, kb), 0.0)
        o_scr[sl, :] = _dot(att.astype(BF16), v)
        qf_scr[sl, :] = qf
        qb_scr[sl, :] = qb
        uf_scr[n] = _dot_tn(v, ksf)
        ub_scr[n] = _dot_tn(v, ksb)
        df_scr[n] = jnp.broadcast_to(jnp.exp(bf_end), (8, dk))
        db_scr[n] = jnp.broadcast_to(jnp.exp(bb_end), (8, dk))
        return carry

    lax.fori_loop(0, n_chunks, chunk_local, 0)

    def scan_fwd(n, s):
        sl = pl.ds(pl.multiple_of(n * c, c), c)
        o_scr[sl, :] += _dot_nt(qf_scr[sl, :], s.astype(BF16))
        return s * df_scr[n][0:1, :] + uf_scr[n]

    def scan_bwd(i, s):
        n = n_chunks - 1 - i
        sl = pl.ds(pl.multiple_of(n * c, c), c)
        o_scr[sl, :] += _dot_nt(qb_scr[sl, :], s.astype(BF16))
        return s * db_scr[n][0:1, :] + ub_scr[n]

    s_f = lax.fori_loop(0, n_chunks, scan_fwd, s0f_ref[0, 0].T)
    s_b = lax.fori_loop(0, n_chunks, scan_bwd, s0b_ref[0, 0].T)
    sf_ref[0, 0] = s_f.T
    sb_ref[0, 0] = s_b.T

    o = _rms(o_scr[...], gg_ref[...])
    y_ref[0] = (o * _silu(g_ref[0].astype(F32))).astype(y_ref.dtype)


def _gla(proj, lr, wdf, bdf, wdb, bdb, g_gla, s0f, s0b, batch, seq, dk, dv):
    heads = GLA_HEADS
    proj3 = proj.reshape(batch, seq, proj.shape[-1])
    lr3 = lr.reshape(batch, seq, LANES)
    kdim = heads * dk
    width = heads * dv
    k_blk = kdim // dk
    v_blk = 2 * kdim // dv
    g_blk = (2 * kdim + width) // dv
    n_chunks = seq // GLA_CHUNK

    def s0_map(s0):
        if s0.shape[0] == batch:
            return lambda b, h: (b, h, 0, 0)
        return lambda b, h: (0, 0, 0, 0)

    kern = functools.partial(_gla_kernel, seq=seq, dk=dk)
    return pl.pallas_call(
        kern,
        grid=(batch, heads),
        in_specs=[pl.BlockSpec((1, seq, dk), lambda b, h: (b, 0, h)),
                  pl.BlockSpec((1, seq, dk), lambda b, h: (b, 0, k_blk + h)),
                  pl.BlockSpec((1, seq, dv), lambda b, h: (b, 0, v_blk + h)),
                  pl.BlockSpec((1, seq, dv), lambda b, h: (b, 0, g_blk + h)),
                  pl.BlockSpec((1, seq, LANES), lambda b, h: (b, 0, 0)),
                  pl.BlockSpec((1, LANES, dk), lambda b, h: (h, 0, 0)),
                  pl.BlockSpec((1, dk), lambda b, h: (0, h)),
                  pl.BlockSpec((1, LANES, dk), lambda b, h: (h, 0, 0)),
                  pl.BlockSpec((1, dk), lambda b, h: (0, h)),
                  pl.BlockSpec((1, dv), lambda b, h: (0, 0)),
                  pl.BlockSpec((1, 1, dk, dv), s0_map(s0f)),
                  pl.BlockSpec((1, 1, dk, dv), s0_map(s0b))],
        out_specs=[pl.BlockSpec((1, seq, dv), lambda b, h: (b, 0, h)),
                   pl.BlockSpec((1, 1, dk, dv), lambda b, h: (b, h, 0, 0)),
                   pl.BlockSpec((1, 1, dk, dv), lambda b, h: (b, h, 0, 0))],
        out_shape=[jax.ShapeDtypeStruct((batch, seq, width), BF16),
                   jax.ShapeDtypeStruct((batch, heads, dk, dv), F32),
                   jax.ShapeDtypeStruct((batch, heads, dk, dv), F32)],
        scratch_shapes=[pltpu.VMEM((seq, dk), F32), pltpu.VMEM((seq, dk), F32),
                        pltpu.VMEM((seq, dk), BF16), pltpu.VMEM((seq, dk), BF16),
                        pltpu.VMEM((seq, dv), F32),
                        pltpu.VMEM((n_chunks, dv, dk), F32), pltpu.VMEM((n_chunks, dv, dk), F32),
                        pltpu.VMEM((n_chunks, 8, dk), F32), pltpu.VMEM((n_chunks, 8, dk), F32)],
        compiler_params=_cparams(("arbitrary", "arbitrary")),
        name="gla",
    )(proj3, proj3, proj3, proj3, lr3, wdf, bdf, wdb, bdb, g_gla, s0f, s0b)


def _filter_kernel(pe_ref, w1_ref, b1_ref, w2_ref, b2_ref, fr_ref, w3_ref, dl_ref, ad_ref, hl_ref, *, seq):
    pe = pe_ref[...]
    fr = fr_ref[...]
    h1 = jnp.sin(fr * (_dot_hp(pe, w1_ref[...]) + b1_ref[...]))
    h2 = jnp.sin(fr * (_dot_hp(h1, w2_ref[...]) + b2_ref[...]))
    dec = jnp.exp(-pe[:, 0:1] * dl_ref[...])
    row = lax.broadcasted_iota(jnp.int32, (seq, 1), 0)
    alt = jnp.where(row % 2 == 0, 1.0, -1.0)
    for o in range(HY_ORDER):
        ff = _dot_hp(h2, w3_ref[o]) * dec
        fb = _dot_hp(h2, w3_ref[HY_ORDER + o]) * dec
        nrm = jnp.sum(jnp.abs(ff), axis=0, keepdims=True) + jnp.sum(jnp.abs(fb), axis=0, keepdims=True)
        ff = ff / nrm
        fb = jnp.where(row == 0, 0.0, fb / nrm)
        a = ff + fb
        ad_ref[o, 0] = a.astype(ad_ref.dtype)
        ad_ref[o, 1] = (fb - ff).astype(ad_ref.dtype)
        hl_ref[o] = jnp.sum(alt * a, axis=0, keepdims=True)


def _hy_filter(pe, w1p, b1, w2, b2, freq, w3r, deltas, seq, width, tc):
    hid = w2.shape[0]
    kern = functools.partial(_filter_kernel, seq=seq)
    return pl.pallas_call(
        kern,
        grid=(width // tc,),
        in_specs=[pl.BlockSpec((seq, LANES), lambda c: (0, 0)),
                  pl.BlockSpec((LANES, hid), lambda c: (0, 0)),
                  pl.BlockSpec((1, hid), lambda c: (0, 0)),
                  pl.BlockSpec((hid, hid), lambda c: (0, 0)),
                  pl.BlockSpec((1, hid), lambda c: (0, 0)),
                  pl.BlockSpec((1, hid), lambda c: (0, 0)),
                  pl.BlockSpec((2 * HY_ORDER, hid, tc), lambda c: (0, 0, c)),
                  pl.BlockSpec((1, tc), lambda c: (0, c))],
        out_specs=[pl.BlockSpec((HY_ORDER, 2, seq, tc), lambda c: (0, 0, 0, c)),
                   pl.BlockSpec((HY_ORDER, 1, tc), lambda c: (0, 0, c))],
        out_shape=[jax.ShapeDtypeStruct((HY_ORDER, 2, seq, width), BF16),
                   jax.ShapeDtypeStruct((HY_ORDER, 1, width), F32)],
        compiler_params=_cparams(("arbitrary",)),
        name="hy_filter",
    )(pe, w1p, b1, w2, b2, freq, w3r, deltas)


def _spectrum_kernel(cs_ref, ad_ref, h_ref, *, seq, tr):
    r = pl.program_id(1)
    k = (r * tr) % seq + lax.broadcasted_iota(jnp.int32, (tr, 1), 0)
    wgt = jnp.where(k == 0, 1.0, 2.0) * (0.5 / seq)
    h_ref[0] = wgt * _dot(cs_ref[...], ad_ref[0, 0])


def _hy_spectrum(cs, ad, seq, width, tr, tc):
    kern = functools.partial(_spectrum_kernel, seq=seq, tr=tr)
    per_half = seq // tr
    return pl.pallas_call(
        kern,
        grid=(HY_ORDER, 2 * seq // tr, width // tc),
        in_specs=[pl.BlockSpec((tr, seq), lambda o, r, c: (r, 0)),
                  pl.BlockSpec((1, 1, seq, tc), lambda o, r, c: (o, r // per_half, 0, c))],
        out_specs=pl.BlockSpec((1, tr, tc), lambda o, r, c: (o, r, c)),
        out_shape=jax.ShapeDtypeStruct((HY_ORDER, 2 * seq, width), F32),
        compiler_params=_cparams(("arbitrary", "arbitrary", "arbitrary")),
        name="hy_spectrum",
    )(cs, ad)


def _hyconv_kernel(u0_ref, u1_ref, u2_ref, cw0_ref, cw1_ref, cw2_ref, cb0_ref, cb1_ref, cb2_ref,
                   cs_ref, h_ref, hl_ref, bias_ref, z_ref, y_scr, *, seq, seg):
    row = lax.broadcasted_iota(jnp.int32, (seq, 1), 0)
    pos = row % seg
    alt = jnp.where(row % 2 == 0, 1.0, -1.0)

    def short_conv(u_ref, cw_ref, cb_ref):
        u = u_ref[0].astype(F32)
        prev = jnp.where(pos == 0, 0.0, pltpu.roll(u, 1, 0))
        nxt = jnp.where(pos == seg - 1, 0.0, pltpu.roll(u, seq - 1, 0))
        return prev * cw_ref[0:1, :] + u * cw_ref[1:2, :] + nxt * cw_ref[2:3, :] + cb_ref[...]

    gates = ((u1_ref, cw1_ref, cb1_ref), (u2_ref, cw2_ref, cb2_ref))
    z = short_conv(u0_ref, cw0_ref, cb0_ref)
    kb = min(seq, 512)
    for n in range(HY_ORDER):
        zb = z.astype(BF16)
        for r in range(0, seq, kb):
            xc = _dot(cs_ref[r:r + kb, :], zb)
            xs = _dot(cs_ref[seq + r:seq + r + kb, :], zb)
            hre = h_ref[n, r:r + kb, :]
            him = h_ref[n, seq + r:seq + r + kb, :]
            y_scr[r:r + kb, :] = (xc * hre + xs * him).astype(BF16)
            y_scr[seq + r:seq + r + kb, :] = (xs * hre - xc * him).astype(BF16)
        nyq = jnp.sum(alt * z, axis=0, keepdims=True) * (hl_ref[n] * (0.5 / seq))
        conv = _dot(cs_ref[:seq, :], y_scr[:seq, :]) + _dot(cs_ref[seq:, :], y_scr[seq:, :]) + alt * nyq
        z = short_conv(*gates[n]) * (conv + bias_ref[n] * z)
    z_ref[0] = z.astype(z_ref.dtype)


def _hy_conv(proj, conv_w, conv_b, cs, hspec, hl, bias, batch, seq, seg, col0, width, tc):
    proj3 = proj.reshape(batch, seq, proj.shape[-1])
    kern = functools.partial(_hyconv_kernel, seq=seq, seg=seg)
    blk0 = col0 // tc
    per = width // tc

    def u_spec(p):
        return pl.BlockSpec((1, seq, tc), lambda c, b: (b, 0, blk0 + p * per + c))

    def w_spec(p, rows):
        return pl.BlockSpec((rows, tc), lambda c, b: (0, p * per + c))

    return pl.pallas_call(
        kern,
        grid=(per, batch),
        in_specs=[u_spec(0), u_spec(1), u_spec(2),
                  w_spec(0, HY_SHORT), w_spec(1, HY_SHORT), w_spec(2, HY_SHORT),
                  w_spec(0, 1), w_spec(1, 1), w_spec(2, 1),
                  pl.BlockSpec((2 * seq, seq), lambda c, b: (0, 0), pipeline_mode=pl.Buffered(1)),
                  pl.BlockSpec((HY_ORDER, 2 * seq, tc), lambda c, b: (0, 0, c), pipeline_mode=pl.Buffered(1)),
                  pl.BlockSpec((HY_ORDER, 1, tc), lambda c, b: (0, 0, c)),
                  pl.BlockSpec((HY_ORDER, 1, tc), lambda c, b: (0, 0, c))],
        out_specs=pl.BlockSpec((1, seq, tc), lambda c, b: (b, 0, c)),
        out_shape=jax.ShapeDtypeStruct((batch, seq, width), BF16),
        scratch_shapes=[pltpu.VMEM((2 * seq, tc), BF16)],
        compiler_params=_cparams(("arbitrary", "arbitrary")),
        name="hy_conv",
    )(proj3, proj3, proj3, conv_w, conv_w, conv_w, conv_b, conv_b, conv_b, cs, hspec, hl, bias)


def _mixout_kernel(yg_ref, zh_ref, x_ref, mod_ref, ghy_ref, gpost_ref, gpre_ref, wo_ref,
                   wrh_ref, wrl_ref, br_ref, x1_ref, h2_ref, gates_ref, *, gla_width):
    yh = _rms(zh_ref[...].astype(F32), ghy_ref[...]).astype(BF16)
    y = _dot(yg_ref[...], wo_ref[:gla_width, :]) + _dot(yh, wo_ref[gla_width:, :])
    x1 = x_ref[...] + mod_ref[0, 2:3, :] * _rms(y, gpost_ref[...])
    x1_ref[...] = x1
    h2 = _rms(x1, gpre_ref[...]) * (1.0 + mod_ref[0, 4:5, :]) + mod_ref[0, 3:4, :]
    hh, hl = _split2(h2)
    h2_ref[...] = hh
    logits = _dot(hh, wrh_ref[...]) + (_dot(hh, wrl_ref[...]) + _dot(hl, wrh_ref[...])) + br_ref[...]

    lane = lax.broadcasted_iota(jnp.int32, logits.shape, 1).astype(F32)
    neg = -jnp.inf
    is_grp = (lane >= N_EXPERTS) & (lane < N_EXPERTS + N_GROUPS)
    m = jnp.max(jnp.where(is_grp, logits, neg), axis=1, keepdims=True)
    p_grp = 1.0 / jnp.sum(jnp.where(is_grp, jnp.exp(logits - m), 0.0), axis=1, keepdims=True)
    grp = jnp.min(jnp.where(is_grp & (logits == m), lane - N_EXPERTS, 1e9), axis=1, keepdims=True)
    sel = (lane >= grp * EXP_PER_GROUP) & (lane < (grp + 1.0) * EXP_PER_GROUP)
    me = jnp.max(jnp.where(sel, logits, neg), axis=1, keepdims=True)
    pe = jnp.where(sel, jnp.exp(logits - me), -1.0)
    v1 = jnp.max(pe, axis=1, keepdims=True)
    i1 = jnp.min(jnp.where(pe == v1, lane, 1e9), axis=1, keepdims=True)
    pe2 = jnp.where(lane == i1, -1.0, pe)
    v2 = jnp.max(pe2, axis=1, keepdims=True)
    i2 = jnp.min(jnp.where(pe2 == v2, lane, 1e9), axis=1, keepdims=True)
    den = v1 + v2
    gates_ref[...] = jnp.where(lane == i1, v1 / den, jnp.where(lane == i2, v2 / den, 0.0)) * p_grp


def _mix_out(yg, zh, x, mod, mod_row, g_hy, g_post, g_pre, w_out, wr_hi, wr_lo, b_r, tm):
    n, d = x.shape
    gw = yg.shape[1]
    hw = zh.shape[1]
    kern = functools.partial(_mixout_kernel, gla_width=gw)
    row = lambda i: (i, 0)
    fixed = lambda i: (0, 0)
    return pl.pallas_call(
        kern,
        grid=(n // tm,),
        in_specs=[pl.BlockSpec((tm, gw), row),
                  pl.BlockSpec((tm, hw), row),
                  pl.BlockSpec((tm, d), row),
                  pl.BlockSpec((1, 6, d), lambda i: (mod_row(i * tm), 0, 0)),
                  pl.BlockSpec((1, hw), fixed),
                  pl.BlockSpec((1, d), fixed),
                  pl.BlockSpec((1, d), fixed),
                  pl.BlockSpec((gw + hw, d), fixed),
                  pl.BlockSpec((d, LANES), fixed),
                  pl.BlockSpec((d, LANES), fixed),
                  pl.BlockSpec((1, LANES), fixed)],
        out_specs=[pl.BlockSpec((tm, d), row), pl.BlockSpec((tm, d), row), pl.BlockSpec((tm, LANES), row)],
        out_shape=[jax.ShapeDtypeStruct((n, d), F32),
                   jax.ShapeDtypeStruct((n, d), BF16),
                   jax.ShapeDtypeStruct((n, LANES), F32)],
        compiler_params=_cparams(("arbitrary",)),
        name="mix_out",
    )(yg, zh, x, mod, g_hy, g_post, g_pre, w_out, wr_hi, wr_lo, b_r)


def _moe_kernel(h_ref, gates_ref, wg_ref, wu_ref, wd_ref, x1_ref, mod_ref, gpost_ref, o_ref, acc_scr):
    e = pl.program_id(1)

    @pl.when(e == 0)
    def _():
        acc_scr[...] = jnp.zeros_like(acc_scr)

    h = h_ref[...]
    he = (_silu(_dot(h, wg_ref[0])) * _dot(h, wu_ref[0])).astype(BF16)
    lane = lax.broadcasted_iota(jnp.int32, gates_ref.shape, 1)
    gate = jnp.sum(jnp.where(lane == e, gates_ref[...], 0.0), axis=1, keepdims=True)
    acc_scr[...] += gate * _dot(he, wd_ref[0])

    @pl.when(e == pl.num_programs(1) - 1)
    def _():
        o_ref[...] = x1_ref[...] + mod_ref[0, 5:6, :] * _rms(acc_scr[...], gpost_ref[...])


def _moe(h2, gates, wg, wu, wd, x1, mod, mod_row, g_post, tm):
    n, d = x1.shape
    n_exp, _, de = wg.shape
    return pl.pallas_call(
        _moe_kernel,
        grid=(n // tm, n_exp),
        in_specs=[pl.BlockSpec((tm, d), lambda i, e: (i, 0)),
                  pl.BlockSpec((tm, LANES), lambda i, e: (i, 0)),
                  pl.BlockSpec((1, d, de), lambda i, e: (e, 0, 0)),
                  pl.BlockSpec((1, d, de), lambda i, e: (e, 0, 0)),
                  pl.BlockSpec((1, de, d), lambda i, e: (e, 0, 0)),
                  pl.BlockSpec((tm, d), lambda i, e: (i, 0)),
                  pl.BlockSpec((1, 6, d), lambda i, e: (mod_row(i * tm), 0, 0)),
                  pl.BlockSpec((1, d), lambda i, e: (0, 0))],
        out_specs=pl.BlockSpec((tm, d), lambda i, e: (i, 0)),
        out_shape=jax.ShapeDtypeStruct((n, d), F32),
        scratch_shapes=[pltpu.VMEM((tm, d), F32)],
        compiler_params=_cparams(("arbitrary", "arbitrary")),
        name="moe",
    )(h2, gates, wg, wu, wd, x1, mod, g_post)


def _dft_cos_sin(seq):
    k = jnp.arange(seq, dtype=jnp.int32)
    ang = ((k[:, None] * k[None, :]) % (2 * seq)).astype(F32) * (math.pi / seq)
    return jnp.concatenate([jnp.cos(ang), jnp.sin(ang)], axis=0).astype(BF16)


def _positional_features(seq):
    t = jnp.arange(seq, dtype=F32)
    t01 = t / max(seq - 1, 1)
    ang = 2.0 * math.pi * t / seq
    bands = jnp.linspace(1e-4, HY_BANDS - 1, HY_BANDS, dtype=F32)
    pe = jnp.concatenate([t01[:, None], jnp.cos(ang[:, None] * bands), -jnp.sin(ang[:, None] * bands)], axis=-1)
    return jnp.pad(pe, ((0, 0), (0, LANES - pe.shape[1])))


def _layer(x3, mod, mod_row, s0f, s0b, n_seg, p, tiles):
    batch, seq, d = x3.shape
    x = x3.reshape(batch * seq, d)
    dk, dv = p["dk"], p["dv"]
    hy_width = p["hy_width"]

    proj, lr = _premix_proj(x, mod, mod_row, p["g_pre_mix"], p["w_main"], p["wlr_hi"], p["wlr_lo"],
                            tiles["tm_proj"], tiles["tn_proj"])
    y_gla, s_f, s_b = _gla(proj, lr, p["wdf"], p["bdf"], p["wdb"], p["bdb"], p["g_gla"], s0f, s0b,
                           batch, seq, dk, dv)

    ad, hl = _hy_filter(_positional_features(seq), p["hy_w1"], p["hy_b1"], p["hy_w2"], p["hy_b2"], p["hy_freq"],
                        p["hy_w3"], p["hy_deltas"], seq, hy_width, tiles["tc_filter"])
    cs = _dft_cos_sin(seq)
    hspec = _hy_spectrum(cs, ad, seq, hy_width, tiles["tr_spec"], tiles["tc_spec"])
    z_hy = _hy_conv(proj, p["hy_conv_w"], p["hy_conv_b"], cs, hspec, hl, p["hy_bias"], batch, seq, seq // n_seg,
                    p["hy_col0"], hy_width, tiles["tc_conv"])

    x1, h2, gates = _mix_out(y_gla.reshape(batch * seq, -1), z_hy.reshape(batch * seq, -1), x, mod, mod_row,
                             p["g_hy"], p["g_post_mix"], p["g_pre_ffn"], p["w_out"], p["wr_hi"], p["wr_lo"], p["b_r"],
                             tiles["tm_mix"])
    out = _moe(h2, gates, p["w_exp_gate"], p["w_exp_up"], p["w_exp_down"], x1, mod, mod_row, p["g_post_ffn"],
               tiles["tm_moe"])
    return out.reshape(batch, seq, d), s_f, s_b


def kernel(x_prompt, x_sample, c, state_gla_fwd, state_gla_bwd, c_ctx, w_ada, b_ada, g_pre_mix, g_post_mix, g_pre_ffn, g_post_ffn, w_in, w_dec_f, b_dec_f, w_dec_b, b_dec_b, g_gla, hy_conv_w, hy_conv_b, hy_w1, hy_b1, hy_w2, hy_b2, hy_w3, hy_freq, hy_bias, g_hy, w_out, w_router_grp, b_router_grp, w_router_exp, b_router_exp, w_exp_gate, w_exp_up, w_exp_down):
    depth = w_ada.shape[0]
    assert depth == 1
    l = 0
    d = x_prompt.shape[-1]
    dec_batch = x_sample.shape[0]
    heads = GLA_HEADS
    dk, dv = state_gla_fwd.shape[-2:]
    kdim = heads * dk
    gla_width = heads * dv
    hy_width = g_hy.shape[-1]
    hid = hy_w2.shape[-1]

    cond = jnp.concatenate([c_ctx[None, :], c], axis=0)
    cond = jnp.pad(cond, ((0, -cond.shape[0] % 8), (0, 0)))
    mod = _ada_mod(cond, w_ada[l], b_ada[l]).reshape(cond.shape[0], 6, d)

    w_in_l = w_in[l]
    n_main = 2 * kdim + 2 * gla_width
    lr0 = n_main
    hy0 = n_main + 2 * GLA_RANK
    w_main = jnp.concatenate([w_in_l[:, :n_main], w_in_l[:, hy0:]], axis=1).astype(BF16)
    w_lr = jnp.pad(w_in_l[:, lr0:hy0], ((0, 0), (0, LANES - 2 * GLA_RANK)))
    wlr_hi, wlr_lo = _split2(w_lr)

    def dec_weight(w, first_row):
        wh = w.reshape(GLA_RANK, heads, dk).transpose(1, 0, 2)
        return jnp.pad(wh, ((0, 0), (first_row, LANES - GLA_RANK - first_row), (0, 0)))

    deltas = jnp.abs(jnp.linspace(math.log(HY_TARGET) / HY_SLOW_PCT, math.log(HY_TARGET) / HY_FAST_PCT, hy_width,
                                  dtype=F32))
    w_r = jnp.pad(jnp.concatenate([w_router_exp[l], w_router_grp[l]], axis=1),
                  ((0, 0), (0, LANES - N_EXPERTS - N_GROUPS)))
    wr_hi, wr_lo = _split2(w_r)
    b_r = jnp.pad(jnp.concatenate([b_router_exp[l], b_router_grp[l]]), (0, LANES - N_EXPERTS - N_GROUPS))

    p = dict(
        dk=dk, dv=dv, hy_width=hy_width, hy_col0=n_main,
        g_pre_mix=g_pre_mix[l][None, :], g_post_mix=g_post_mix[l][None, :],
        g_pre_ffn=g_pre_ffn[l][None, :], g_post_ffn=g_post_ffn[l][None, :],
        w_main=w_main, wlr_hi=wlr_hi, wlr_lo=wlr_lo,
        wdf=dec_weight(w_dec_f[l], 0), bdf=b_dec_f[l][None, :],
        wdb=dec_weight(w_dec_b[l], GLA_RANK), bdb=b_dec_b[l][None, :],
        g_gla=g_gla[l][None, :],
        hy_conv_w=hy_conv_w[l], hy_conv_b=hy_conv_b[l][None, :],
        hy_w1=jnp.pad(hy_w1[l], ((0, LANES - hy_w1.shape[1]), (0, 0))), hy_b1=hy_b1[l][None, :],
        hy_w2=hy_w2[l], hy_b2=hy_b2[l][None, :], hy_freq=hy_freq[l][None, :],
        hy_w3=hy_w3[l].reshape(hid, 2 * HY_ORDER, hy_width).transpose(1, 0, 2), hy_deltas=deltas[None, :],
        hy_bias=hy_bias[l][:, None, :], g_hy=g_hy[l][None, :],
        w_out=w_out[l].astype(BF16), wr_hi=wr_hi, wr_lo=wr_lo, b_r=b_r[None, :],
        w_exp_gate=w_exp_gate[l].astype(BF16), w_exp_up=w_exp_up[l].astype(BF16),
        w_exp_down=w_exp_down[l].astype(BF16),
    )

    zero_state = jnp.zeros((1, 1, dk, dv), F32)
    dec_seq = x_sample.shape[1]
    tiles_p = dict(tm_proj=512, tn_proj=512, tc_filter=256, tr_spec=256, tc_spec=512, tc_conv=512,
                   tm_mix=256, tm_moe=512)
    tiles_s = dict(tm_proj=512, tn_proj=512, tc_filter=256, tr_spec=512, tc_spec=512, tc_conv=256,
                   tm_mix=256, tm_moe=512)

    y_p, s_f, s_b = _layer(x_prompt, mod, lambda r: 0, zero_state, zero_state, 1, p, tiles_p)
    y_s, _, _ = _layer(x_sample, mod, lambda r: 1 + r // dec_seq, state_gla_fwd[:, l], state_gla_bwd[:, l],
                       dec_seq // GRID_W, p, tiles_s)
    return (y_p, y_s, s_f[:, None].astype(x_prompt.dtype), s_b[:, None].astype(x_prompt.dtype))
```

```python
import functools
import math

import jax
import jax.numpy as jnp
from jax import lax
from jax.experimental import pallas as pl
from jax.experimental.pallas import tpu as pltpu

F32 = jnp.float32
BF16 = jnp.bfloat16

GRID_W = 64
GLA_HEADS = 4
GLA_RANK = 16
GLA_TAU = 16.0
GLA_CHUNK = 64
HY_ORDER = 2
HY_SHORT = 3
HY_BANDS = 16
HY_TARGET = 1e-2
HY_FAST_PCT = 0.3
HY_SLOW_PCT = 1.5
N_GROUPS = 4
EXP_PER_GROUP = 4
N_EXPERTS = N_GROUPS * EXP_PER_GROUP
PAIRS_PER_GROUP = EXP_PER_GROUP * (EXP_PER_GROUP - 1) // 2
ROUTE_BUCKET_LANE = N_EXPERTS
ROUTE_RANK_LANE = N_EXPERTS + 1
EPS = 1e-6

LANES = 128
VMEM_LIMIT = 56 << 20


def _cparams(sem):
    return pltpu.CompilerParams(dimension_semantics=sem, vmem_limit_bytes=VMEM_LIMIT)


def _dot(a, b):
    return jnp.dot(a, b, preferred_element_type=F32)


def _dot_nt(a, b):
    return lax.dot_general(a, b, (((1,), (1,)), ((), ())), preferred_element_type=F32)


def _dot_tn(a, b):
    return lax.dot_general(a, b, (((0,), (0,)), ((), ())), preferred_element_type=F32)


def _split2(x):
    hi = x.astype(BF16)
    lo = (x - hi.astype(F32)).astype(BF16)
    return hi, lo


def _dot_hp(a, b):
    ah, al = _split2(a)
    bh, bl = _split2(b)
    return _dot(ah, bh) + (_dot(ah, bl) + _dot(al, bh))


def _dot_exact_lhs(t, x):
    x0 = x.astype(BF16)
    r = x - x0.astype(F32)
    x1 = r.astype(BF16)
    x2 = (r - x1.astype(F32)).astype(BF16)
    return _dot(t, x0) + (_dot(t, x1) + _dot(t, x2))


def _rms(x, g):
    return x * lax.rsqrt(jnp.mean(x * x, axis=-1, keepdims=True) + EPS) * g


def _silu(x):
    return x / (1.0 + jnp.exp(-x))


def _ada_kernel(c_ref, w_ref, b_ref, o_ref):
    s = _silu(c_ref[...]).astype(BF16)
    o_ref[...] = _dot(s, w_ref[...].astype(BF16)) + b_ref[...]


def _ada_mod(cond, w_ada, b_ada):
    rows, d = cond.shape
    n = w_ada.shape[1]
    tn = 1024
    return pl.pallas_call(
        _ada_kernel,
        grid=(n // tn,),
        in_specs=[pl.BlockSpec((rows, d), lambda j: (0, 0)),
                  pl.BlockSpec((d, tn), lambda j: (0, j)),
                  pl.BlockSpec((1, tn), lambda j: (0, j))],
        out_specs=pl.BlockSpec((rows, tn), lambda j: (0, j)),
        out_shape=jax.ShapeDtypeStruct((rows, n), F32),
        compiler_params=_cparams(("arbitrary",)),
        name="ada_mod",
    )(cond, w_ada, b_ada.reshape(1, n))


def _premix_kernel(x_ref, mod_ref, g_ref, w_ref, wlh_ref, wll_ref, o_ref, lr_ref, h_scr):
    @pl.when(pl.program_id(1) == 0)
    def _():
        h = _rms(x_ref[...], g_ref[...]) * (1.0 + mod_ref[0, 1:2, :]) + mod_ref[0, 0:1, :]
        hh, hl = _split2(h)
        h_scr[...] = hh
        lr_ref[...] = _dot(hh, wlh_ref[...]) + (_dot(hh, wll_ref[...]) + _dot(hl, wlh_ref[...]))

    o_ref[...] = _dot(h_scr[...], w_ref[...]).astype(o_ref.dtype)


def _premix_proj(x, mod, mod_row, g, w_main, wlr_hi, wlr_lo, tm, tn):
    n, d = x.shape
    nc = w_main.shape[1]
    return pl.pallas_call(
        _premix_kernel,
        grid=(n // tm, nc // tn),
        in_specs=[pl.BlockSpec((tm, d), lambda i, j: (i, 0)),
                  pl.BlockSpec((1, 6, d), lambda i, j: (mod_row(i * tm), 0, 0)),
                  pl.BlockSpec((1, d), lambda i, j: (0, 0)),
                  pl.BlockSpec((d, tn), lambda i, j: (0, j)),
                  pl.BlockSpec((d, LANES), lambda i, j: (0, 0)),
                  pl.BlockSpec((d, LANES), lambda i, j: (0, 0))],
        out_specs=[pl.BlockSpec((tm, tn), lambda i, j: (i, j)),
                   pl.BlockSpec((tm, LANES), lambda i, j: (i, 0))],
        out_shape=[jax.ShapeDtypeStruct((n, nc), BF16),
                   jax.ShapeDtypeStruct((n, LANES), F32)],
        scratch_shapes=[pltpu.VMEM((tm, d), BF16)],
        compiler_params=_cparams(("arbitrary", "arbitrary")),
        name="premix_proj",
    )(x, mod, g, w_main, wlr_hi, wlr_lo)


def _log_sigmoid(x):
    return jnp.minimum(x, 0.0) - jnp.log1p(jnp.exp(-jnp.abs(x)))


def _gla_kernel(q_ref, k_ref, v_ref, g_ref, lr_ref, wdf_ref, bdf_ref, wdb_ref, bdb_ref, gg_ref,
                s0f_ref, s0b_ref, y_ref, sf_ref, sb_ref,
                laf_scr, lab_scr, qf_scr, qb_scr, o_scr, uf_scr, ub_scr, df_scr, db_scr, *, seq, dk):
    c = GLA_CHUNK
    n_chunks = seq // c
    scale = dk ** -0.5

    lr = lr_ref[0]
    laf_scr[...] = _log_sigmoid(_dot_hp(lr, wdf_ref[0]) + bdf_ref[...]) / GLA_TAU
    lab_scr[...] = _log_sigmoid(_dot_hp(lr, wdb_ref[0]) + bdb_ref[...]) / GLA_TAU

    row = lax.broadcasted_iota(jnp.int32, (c, c), 0)
    col = lax.broadcasted_iota(jnp.int32, (c, c), 1)
    lower = row >= col
    upper = col >= row
    tl = lower.astype(BF16)
    tu = upper.astype(BF16)

    def chunk_local(n, carry):
        sl = pl.ds(pl.multiple_of(n * c, c), c)
        bf = _dot_exact_lhs(tl, laf_scr[sl, :])
        bb = _dot_exact_lhs(tu, lab_scr[sl, :])
        q = q_ref[0, sl, :].astype(F32) * scale
        k = k_ref[0, sl, :].astype(F32)
        v = v_ref[0, sl, :]
        bf_end = bf[c - 1:c, :]
        bb_end = bb[0:1, :]
        qf = (q * jnp.exp(bf)).astype(BF16)
        kf = (k * jnp.exp(-bf)).astype(BF16)
        ksf = (k * jnp.exp(bf_end - bf)).astype(BF16)
        qb = (q * jnp.exp(bb)).astype(BF16)
        kb = (k * jnp.exp(-bb)).astype(BF16)
        ksb = (k * jnp.exp(bb_end - bb)).astype(BF16)
        att = jnp.where(lower, _dot_nt(qf, kf), 0.0) + jnp.where(upper, _dot_nt(qb, kb), 0.0)
        o_scr[sl, :] = _dot(att.astype(BF16), v)
        qf_scr[sl, :] = qf
        qb_scr[sl, :] = qb
        uf_scr[n] = _dot_tn(v, ksf)
        ub_scr[n] = _dot_tn(v, ksb)
        df_scr[n] = jnp.broadcast_to(jnp.exp(bf_end), (8, dk))
        db_scr[n] = jnp.broadcast_to(jnp.exp(bb_end), (8, dk))
        return carry

    lax.fori_loop(0, n_chunks, chunk_local, 0)

    def scan_fwd(n, s):
        sl = pl.ds(pl.multiple_of(n * c, c), c)
        o_scr[sl, :] += _dot_nt(qf_scr[sl, :], s.astype(BF16))
        return s * df_scr[n][0:1, :] + uf_scr[n]

    def scan_bwd(i, s):
        n = n_chunks - 1 - i
        sl = pl.ds(pl.multiple_of(n * c, c), c)
        o_scr[sl, :] += _dot_nt(qb_scr[sl, :], s.astype(BF16))
        return s * db_scr[n][0:1, :] + ub_scr[n]

    s_f = lax.fori_loop(0, n_chunks, scan_fwd, s0f_ref[0, 0].T)
    s_b = lax.fori_loop(0, n_chunks, scan_bwd, s0b_ref[0, 0].T)
    sf_ref[0, 0] = s_f.T
    sb_ref[0, 0] = s_b.T

    o = _rms(o_scr[...], gg_ref[...])
    y_ref[0] = (o * _silu(g_ref[0].astype(F32))).astype(y_ref.dtype)


def _gla(proj, lr, wdf, bdf, wdb, bdb, g_gla, s0f, s0b, batch, seq, dk, dv):
    heads = GLA_HEADS
    proj3 = proj.reshape(batch, seq, proj.shape[-1])
    lr3 = lr.reshape(batch, seq, LANES)
    kdim = heads * dk
    width = heads * dv
    k_blk = kdim // dk
    v_blk = 2 * kdim // dv
    g_blk = (2 * kdim + width) // dv
    n_chunks = seq // GLA_CHUNK

    def s0_map(s0):
        if s0.shape[0] == batch:
            return lambda b, h: (b, h, 0, 0)
        return lambda b, h: (0, 0, 0, 0)

    kern = functools.partial(_gla_kernel, seq=seq, dk=dk)
    return pl.pallas_call(
        kern,
        grid=(batch, heads),
        in_specs=[pl.BlockSpec((1, seq, dk), lambda b, h: (b, 0, h)),
                  pl.BlockSpec((1, seq, dk), lambda b, h: (b, 0, k_blk + h)),
                  pl.BlockSpec((1, seq, dv), lambda b, h: (b, 0, v_blk + h)),
                  pl.BlockSpec((1, seq, dv), lambda b, h: (b, 0, g_blk + h)),
                  pl.BlockSpec((1, seq, LANES), lambda b, h: (b, 0, 0)),
                  pl.BlockSpec((1, LANES, dk), lambda b, h: (h, 0, 0)),
                  pl.BlockSpec((1, dk), lambda b, h: (0, h)),
                  pl.BlockSpec((1, LANES, dk), lambda b, h: (h, 0, 0)),
                  pl.BlockSpec((1, dk), lambda b, h: (0, h)),
                  pl.BlockSpec((1, dv), lambda b, h: (0, 0)),
                  pl.BlockSpec((1, 1, dk, dv), s0_map(s0f)),
                  pl.BlockSpec((1, 1, dk, dv), s0_map(s0b))],
        out_specs=[pl.BlockSpec((1, seq, dv), lambda b, h: (b, 0, h)),
                   pl.BlockSpec((1, 1, dk, dv), lambda b, h: (b, h, 0, 0)),
                   pl.BlockSpec((1, 1, dk, dv), lambda b, h: (b, h, 0, 0))],
        out_shape=[jax.ShapeDtypeStruct((batch, seq, width), BF16),
                   jax.ShapeDtypeStruct((batch, heads, dk, dv), F32),
                   jax.ShapeDtypeStruct((batch, heads, dk, dv), F32)],
        scratch_shapes=[pltpu.VMEM((seq, dk), F32), pltpu.VMEM((seq, dk), F32),
                        pltpu.VMEM((seq, dk), BF16), pltpu.VMEM((seq, dk), BF16),
                        pltpu.VMEM((seq, dv), F32),
                        pltpu.VMEM((n_chunks, dv, dk), F32), pltpu.VMEM((n_chunks, dv, dk), F32),
                        pltpu.VMEM((n_chunks, 8, dk), F32), pltpu.VMEM((n_chunks, 8, dk), F32)],
        compiler_params=_cparams(("arbitrary", "arbitrary")),
        name="gla",
    )(proj3, proj3, proj3, proj3, lr3, wdf, bdf, wdb, bdb, g_gla, s0f, s0b)


def _filter_kernel(pe_ref, w1_ref, b1_ref, w2_ref, b2_ref, fr_ref, w3_ref, dl_ref, ad_ref, hl_ref, *, seq):
    pe = pe_ref[...]
    fr = fr_ref[...]
    h1 = jnp.sin(fr * (_dot_hp(pe, w1_ref[...]) + b1_ref[...]))
    h2 = jnp.sin(fr * (_dot_hp(h1, w2_ref[...]) + b2_ref[...]))
    dec = jnp.exp(-pe[:, 0:1] * dl_ref[...])
    row = lax.broadcasted_iota(jnp.int32, (seq, 1), 0)
    alt = jnp.where(row % 2 == 0, 1.0, -1.0)
    for o in range(HY_ORDER):
        ff = _dot_hp(h2, w3_ref[o]) * dec
        fb = _dot_hp(h2, w3_ref[HY_ORDER + o]) * dec
        nrm = jnp.sum(jnp.abs(ff), axis=0, keepdims=True) + jnp.sum(jnp.abs(fb), axis=0, keepdims=True)
        ff = ff / nrm
        fb = jnp.where(row == 0, 0.0, fb / nrm)
        a = ff + fb
        ad_ref[o, 0] = a.astype(ad_ref.dtype)
        ad_ref[o, 1] = (fb - ff).astype(ad_ref.dtype)
        hl_ref[o] = jnp.sum(alt * a, axis=0, keepdims=True)


def _hy_filter(pe, w1p, b1, w2, b2, freq, w3r, deltas, seq, width, tc):
    hid = w2.shape[0]
    kern = functools.partial(_filter_kernel, seq=seq)
    return pl.pallas_call(
        kern,
        grid=(width // tc,),
        in_specs=[pl.BlockSpec((seq, LANES), lambda c: (0, 0)),
                  pl.BlockSpec((LANES, hid), lambda c: (0, 0)),
                  pl.BlockSpec((1, hid), lambda c: (0, 0)),
                  pl.BlockSpec((hid, hid), lambda c: (0, 0)),
                  pl.BlockSpec((1, hid), lambda c: (0, 0)),
                  pl.BlockSpec((1, hid), lambda c: (0, 0)),
                  pl.BlockSpec((2 * HY_ORDER, hid, tc), lambda c: (0, 0, c)),
                  pl.BlockSpec((1, tc), lambda c: (0, c))],
        out_specs=[pl.BlockSpec((HY_ORDER, 2, seq, tc), lambda c: (0, 0, 0, c)),
                   pl.BlockSpec((HY_ORDER, 1, tc), lambda c: (0, 0, c))],
        out_shape=[jax.ShapeDtypeStruct((HY_ORDER, 2, seq, width), BF16),
                   jax.ShapeDtypeStruct((HY_ORDER, 1, width), F32)],
        compiler_params=_cparams(("arbitrary",)),
        name="hy_filter",
    )(pe, w1p, b1, w2, b2, freq, w3r, deltas)


def _spectrum_kernel(cs_ref, ad_ref, h_ref, *, seq, tr):
    r = pl.program_id(1)
    k = (r * tr) % seq + lax.broadcasted_iota(jnp.int32, (tr, 1), 0)
    wgt = jnp.where(k == 0, 1.0, 2.0) * (0.5 / seq)
    h_ref[0] = wgt * _dot(cs_ref[...], ad_ref[0, 0])


def _hy_spectrum(cs, ad, seq, width, tr, tc):
    kern = functools.partial(_spectrum_kernel, seq=seq, tr=tr)
    per_half = seq // tr
    return pl.pallas_call(
        kern,
        grid=(HY_ORDER, 2 * seq // tr, width // tc),
        in_specs=[pl.BlockSpec((tr, seq), lambda o, r, c: (r, 0)),
                  pl.BlockSpec((1, 1, seq, tc), lambda o, r, c: (o, r // per_half, 0, c))],
        out_specs=pl.BlockSpec((1, tr, tc), lambda o, r, c: (o, r, c)),
        out_shape=jax.ShapeDtypeStruct((HY_ORDER, 2 * seq, width), F32),
        compiler_params=_cparams(("arbitrary", "arbitrary", "arbitrary")),
        name="hy_spectrum",
    )(cs, ad)


def _hyconv_kernel(u0_ref, u1_ref, u2_ref, cw0_ref, cw1_ref, cw2_ref, cb0_ref, cb1_ref, cb2_ref,
                   cs_ref, h_ref, hl_ref, bias_ref, z_ref, y_scr, *, seq, seg):
    row = lax.broadcasted_iota(jnp.int32, (seq, 1), 0)
    pos = row % seg
    alt = jnp.where(row % 2 == 0, 1.0, -1.0)

    def short_conv(u_ref, cw_ref, cb_ref):
        u = u_ref[0].astype(F32)
        prev = jnp.where(pos == 0, 0.0, pltpu.roll(u, 1, 0))
        nxt = jnp.where(pos == seg - 1, 0.0, pltpu.roll(u, seq - 1, 0))
        return prev * cw_ref[0:1, :] + u * cw_ref[1:2, :] + nxt * cw_ref[2:3, :] + cb_ref[...]

    gates = ((u1_ref, cw1_ref, cb1_ref), (u2_ref, cw2_ref, cb2_ref))
    z = short_conv(u0_ref, cw0_ref, cb0_ref)
    kb = min(seq, 512)
    for n in range(HY_ORDER):
        zb = z.astype(BF16)
        for r in range(0, seq, kb):
            xc = _dot(cs_ref[r:r + kb, :], zb)
            xs = _dot(cs_ref[seq + r:seq + r + kb, :], zb)
            hre = h_ref[n, r:r + kb, :]
            him = h_ref[n, seq + r:seq + r + kb, :]
            y_scr[r:r + kb, :] = (xc * hre + xs * him).astype(BF16)
            y_scr[seq + r:seq + r + kb, :] = (xs * hre - xc * him).astype(BF16)
        nyq = jnp.sum(alt * z, axis=0, keepdims=True) * (hl_ref[n] * (0.5 / seq))
        conv = _dot(cs_ref[:seq, :], y_scr[:seq, :]) + _dot(cs_ref[seq:, :], y_scr[seq:, :]) + alt * nyq
        z = short_conv(*gates[n]) * (conv + bias_ref[n] * z)
    z_ref[0] = z.astype(z_ref.dtype)


def _hy_conv(proj, conv_w, conv_b, cs, hspec, hl, bias, batch, seq, seg, col0, width, tc):
    proj3 = proj.reshape(batch, seq, proj.shape[-1])
    kern = functools.partial(_hyconv_kernel, seq=seq, seg=seg)
    blk0 = col0 // tc
    per = width // tc

    def u_spec(p):
        return pl.BlockSpec((1, seq, tc), lambda c, b: (b, 0, blk0 + p * per + c))

    def w_spec(p, rows):
        return pl.BlockSpec((rows, tc), lambda c, b: (0, p * per + c))

    return pl.pallas_call(
        kern,
        grid=(per, batch),
        in_specs=[u_spec(0), u_spec(1), u_spec(2),
                  w_spec(0, HY_SHORT), w_spec(1, HY_SHORT), w_spec(2, HY_SHORT),
                  w_spec(0, 1), w_spec(1, 1), w_spec(2, 1),
                  pl.BlockSpec((2 * seq, seq), lambda c, b: (0, 0), pipeline_mode=pl.Buffered(1)),
                  pl.BlockSpec((HY_ORDER, 2 * seq, tc), lambda c, b: (0, 0, c), pipeline_mode=pl.Buffered(1)),
                  pl.BlockSpec((HY_ORDER, 1, tc), lambda c, b: (0, 0, c)),
                  pl.BlockSpec((HY_ORDER, 1, tc), lambda c, b: (0, 0, c))],
        out_specs=pl.BlockSpec((1, seq, tc), lambda c, b: (b, 0, c)),
        out_shape=jax.ShapeDtypeStruct((batch, seq, width), BF16),
        scratch_shapes=[pltpu.VMEM((2 * seq, tc), BF16)],
        compiler_params=_cparams(("arbitrary", "arbitrary")),
        name="hy_conv",
    )(proj3, proj3, proj3, conv_w, conv_w, conv_w, conv_b, conv_b, conv_b, cs, hspec, hl, bias)


def _mixout_kernel(yg_ref, zh_ref, x_ref, mod_ref, ghy_ref, gpost_ref, gpre_ref, wo_ref,
                   wrh_ref, wrl_ref, br_ref, x1_ref, hg_ref, route_ref, cnt_ref, cnt_scr, *, gla_width):
    d = x_ref.shape[1]
    tm = x_ref.shape[0]

    @pl.when(pl.program_id(0) == 0)
    def _():
        cnt_scr[...] = jnp.zeros_like(cnt_scr)

    yh = _rms(zh_ref[...].astype(F32), ghy_ref[...]).astype(BF16)
    y = _dot(yg_ref[...], wo_ref[:gla_width, :]) + _dot(yh, wo_ref[gla_width:, :])
    x1 = x_ref[...] + mod_ref[0, 2:3, :] * _rms(y, gpost_ref[...])
    x1_ref[...] = x1
    h2 = _rms(x1, gpre_ref[...]) * (1.0 + mod_ref[0, 4:5, :]) + mod_ref[0, 3:4, :]
    hh, hl = _split2(h2)
    hg_ref[:, :d] = h2
    logits = _dot(hh, wrh_ref[...]) + (_dot(hh, wrl_ref[...]) + _dot(hl, wrh_ref[...])) + br_ref[...]

    lane = lax.broadcasted_iota(jnp.int32, logits.shape, 1).astype(F32)
    neg = -jnp.inf
    is_grp = (lane >= N_EXPERTS) & (lane < N_EXPERTS + N_GROUPS)
    m = jnp.max(jnp.where(is_grp, logits, neg), axis=1, keepdims=True)
    p_grp = 1.0 / jnp.sum(jnp.where(is_grp, jnp.exp(logits - m), 0.0), axis=1, keepdims=True)
    grp = jnp.min(jnp.where(is_grp & (logits == m), lane - N_EXPERTS, 1e9), axis=1, keepdims=True)
    sel = (lane >= grp * EXP_PER_GROUP) & (lane < (grp + 1.0) * EXP_PER_GROUP)
    me = jnp.max(jnp.where(sel, logits, neg), axis=1, keepdims=True)
    pe = jnp.where(sel, jnp.exp(logits - me), -1.0)
    v1 = jnp.max(pe, axis=1, keepdims=True)
    i1 = jnp.min(jnp.where(pe == v1, lane, 1e9), axis=1, keepdims=True)
    pe2 = jnp.where(lane == i1, -1.0, pe)
    v2 = jnp.max(pe2, axis=1, keepdims=True)
    i2 = jnp.min(jnp.where(pe2 == v2, lane, 1e9), axis=1, keepdims=True)
    den = v1 + v2
    gates = jnp.where(lane == i1, v1 / den, jnp.where(lane == i2, v2 / den, 0.0)) * p_grp

    lo = jnp.minimum(i1, i2) - grp * EXP_PER_GROUP
    hi = jnp.maximum(i1, i2) - grp * EXP_PER_GROUP
    bucket = grp * PAIRS_PER_GROUP + lo * (2 * EXP_PER_GROUP - 1 - lo) * 0.5 + (hi - lo - 1.0)
    onehot = lane == bucket
    r_i = lax.broadcasted_iota(jnp.int32, (tm, tm), 0)
    c_i = lax.broadcasted_iota(jnp.int32, (tm, tm), 1)
    earlier = _dot((r_i > c_i).astype(BF16), onehot.astype(BF16)) + cnt_scr[...]
    rank = jnp.sum(jnp.where(onehot, earlier, 0.0), axis=1, keepdims=True)
    cnt_scr[...] += jnp.sum(onehot.astype(F32), axis=0, keepdims=True)
    cnt_ref[...] = cnt_scr[...]
    route = jnp.where(lane == ROUTE_BUCKET_LANE, bucket, jnp.where(lane == ROUTE_RANK_LANE, rank, gates))
    route_ref[...] = route
    hg_ref[:, d:] = route


def _mix_out(yg, zh, x, mod, mod_row, g_hy, g_post, g_pre, w_out, wr_hi, wr_lo, b_r, tm):
    n, d = x.shape
    gw = yg.shape[1]
    hw = zh.shape[1]
    kern = functools.partial(_mixout_kernel, gla_width=gw)
    row = lambda i: (i, 0)
    fixed = lambda i: (0, 0)
    return pl.pallas_call(
        kern,
        grid=(n // tm,),
        in_specs=[pl.BlockSpec((tm, gw), row),
                  pl.BlockSpec((tm, hw), row),
                  pl.BlockSpec((tm, d), row),
                  pl.BlockSpec((1, 6, d), lambda i: (mod_row(i * tm), 0, 0)),
                  pl.BlockSpec((1, hw), fixed),
                  pl.BlockSpec((1, d), fixed),
                  pl.BlockSpec((1, d), fixed),
                  pl.BlockSpec((gw + hw, d), fixed),
                  pl.BlockSpec((d, LANES), fixed),
                  pl.BlockSpec((d, LANES), fixed),
                  pl.BlockSpec((1, LANES), fixed)],
        out_specs=[pl.BlockSpec((tm, d), row), pl.BlockSpec((tm, d + LANES), row), pl.BlockSpec((tm, LANES), row),
                   pl.BlockSpec((1, LANES), fixed)],
        out_shape=[jax.ShapeDtypeStruct((n, d), F32),
                   jax.ShapeDtypeStruct((n, d + LANES), F32),
                   jax.ShapeDtypeStruct((n, LANES), F32),
                   jax.ShapeDtypeStruct((1, LANES), F32)],
        scratch_shapes=[pltpu.VMEM((1, LANES), F32)],
        compiler_params=_cparams(("arbitrary",)),
        name="mix_out",
    )(yg, zh, x, mod, g_hy, g_post, g_pre, w_out, wr_hi, wr_lo, b_r)


def _row_gather_copy(src_hbm, row, buf, slot, r, sem):
    return pltpu.make_async_copy(src_hbm.at[pl.ds(row, 1), :], buf.at[slot, pl.ds(r, 1), :], sem.at[slot])


def _start_row_gather(idx_ref, base, src_hbm, buf, slot, sem, tm):
    def body(r, carry):
        _row_gather_copy(src_hbm, idx_ref[base + r], buf, slot, r, sem).start()
        return carry

    lax.fori_loop(0, tm, body, 0, unroll=8)


def _wait_row_gather(src_hbm, buf, slot, sem, tm):
    pltpu.make_async_copy(src_hbm.at[pl.ds(0, tm), :], buf.at[slot], sem.at[slot]).wait()


def _moe_kernel(src_ref, ea_ref, eb_ref, valid_ref, hg_hbm,
                wga_ref, wua_ref, wda_ref, wgb_ref, wub_ref, wdb_ref, y_ref, buf, sem, *, tm, d):
    t = pl.program_id(0)
    n_t = pl.num_programs(0)
    slot = t % 2
    nxt = jnp.minimum(t + 1, n_t - 1)

    @pl.when(t == 0)
    def _():
        _start_row_gather(src_ref, 0, hg_hbm, buf, 0, sem, tm)

    @pl.when((t + 1 < n_t) & (valid_ref[nxt] == 1))
    def _():
        _start_row_gather(src_ref, nxt * tm, hg_hbm, buf, 1 - slot, sem, tm)

    @pl.when(valid_ref[t] == 1)
    def _():
        _wait_row_gather(hg_hbm, buf, slot, sem, tm)
        h = buf[slot, :, :d].astype(BF16)
        gates = buf[slot, :, d:]
        lane = lax.broadcasted_iota(jnp.int32, gates.shape, 1)

        def expert(wg_ref, wu_ref, wd_ref, e):
            he = (_silu(_dot(h, wg_ref[0])) * _dot(h, wu_ref[0])).astype(BF16)
            gate = jnp.sum(jnp.where(lane == e, gates, 0.0), axis=1, keepdims=True)
            return gate * _dot(he, wd_ref[0])

        y_ref[...] = expert(wga_ref, wua_ref, wda_ref, ea_ref[t]) + expert(wgb_ref, wub_ref, wdb_ref, eb_ref[t])

    @pl.when(valid_ref[t] == 0)
    def _():
        y_ref[...] = jnp.zeros_like(y_ref)


def _moe(hg, src, ea, eb, valid, wg, wu, wd, n_tiles, tm):
    d = hg.shape[1] - LANES
    _, _, de = wg.shape
    kern = functools.partial(_moe_kernel, tm=tm, d=d)
    wa = lambda t, src, ea, eb, valid: (ea[t], 0, 0)
    wb = lambda t, src, ea, eb, valid: (eb[t], 0, 0)
    return pl.pallas_call(
        kern,
        grid_spec=pltpu.PrefetchScalarGridSpec(
            num_scalar_prefetch=4,
            grid=(n_tiles,),
            in_specs=[pl.BlockSpec(memory_space=pl.ANY),
                      pl.BlockSpec((1, d, de), wa), pl.BlockSpec((1, d, de), wa), pl.BlockSpec((1, de, d), wa),
                      pl.BlockSpec((1, d, de), wb), pl.BlockSpec((1, d, de), wb), pl.BlockSpec((1, de, d), wb)],
            out_specs=pl.BlockSpec((tm, d), lambda t, src, ea, eb, valid: (t, 0)),
            scratch_shapes=[pltpu.VMEM((2, tm, d + LANES), F32), pltpu.SemaphoreType.DMA((2,))]),
        out_shape=jax.ShapeDtypeStruct((n_tiles * tm, d), F32),
        compiler_params=_cparams(("arbitrary",)),
        name="moe",
    )(src, ea, eb, valid, hg, wg, wu, wd, wg, wu, wd)


def _ffn_out_kernel(pos_ref, y_hbm, x1_ref, mod_ref, gpost_ref, o_ref, buf, sem, *, tm):
    i = pl.program_id(0)
    n_i = pl.num_programs(0)
    slot = i % 2

    @pl.when(i == 0)
    def _():
        _start_row_gather(pos_ref, 0, y_hbm, buf, 0, sem, tm)

    @pl.when(i + 1 < n_i)
    def _():
        _start_row_gather(pos_ref, (i + 1) * tm, y_hbm, buf, 1 - slot, sem, tm)

    _wait_row_gather(y_hbm, buf, slot, sem, tm)
    o_ref[...] = x1_ref[...] + mod_ref[0, 5:6, :] * _rms(buf[slot], gpost_ref[...])


def _ffn_out(y_sorted, pos, x1, mod, mod_row, g_post, tm):
    n, d = x1.shape
    kern = functools.partial(_ffn_out_kernel, tm=tm)
    return pl.pallas_call(
        kern,
        grid_spec=pltpu.PrefetchScalarGridSpec(
            num_scalar_prefetch=1,
            grid=(n // tm,),
            in_specs=[pl.BlockSpec(memory_space=pl.ANY),
                      pl.BlockSpec((tm, d), lambda i, pos: (i, 0)),
                      pl.BlockSpec((1, 6, d), lambda i, pos: (mod_row(i * tm), 0, 0)),
                      pl.BlockSpec((1, d), lambda i, pos: (0, 0))],
            out_specs=pl.BlockSpec((tm, d), lambda i, pos: (i, 0)),
            scratch_shapes=[pltpu.VMEM((2, tm, d), F32), pltpu.SemaphoreType.DMA((2,))]),
        out_shape=jax.ShapeDtypeStruct((n, d), F32),
        compiler_params=_cparams(("arbitrary",)),
        name="ffn_out",
    )(pos, y_sorted, x1, mod, g_post)


def _route_tables(route, counts, n_tiles, tm):
    n = route.shape[0]
    n_buckets = N_GROUPS * PAIRS_PER_GROUP
    cnt = counts[0, :n_buckets].astype(jnp.int32)
    padded = (cnt + tm - 1) // tm * tm
    ends = jnp.cumsum(padded)
    starts = ends - padded
    bucket = route[:, ROUTE_BUCKET_LANE].astype(jnp.int32)
    rank = route[:, ROUTE_RANK_LANE].astype(jnp.int32)
    pos = starts[bucket] + rank
    src = jnp.zeros((n_tiles * tm,), jnp.int32).at[pos].set(jnp.arange(n, dtype=jnp.int32))
    n_valid = ends[-1] // tm
    tile = jnp.arange(n_tiles, dtype=jnp.int32)
    used = jnp.minimum(tile, n_valid - 1)
    tile_bucket = jnp.searchsorted(ends, used * tm, side="right").astype(jnp.int32)
    grp = tile_bucket // PAIRS_PER_GROUP
    pair = tile_bucket % PAIRS_PER_GROUP
    pair_lo = jnp.array([a for a in range(EXP_PER_GROUP) for b in range(a + 1, EXP_PER_GROUP)], jnp.int32)
    pair_hi = jnp.array([b for a in range(EXP_PER_GROUP) for b in range(a + 1, EXP_PER_GROUP)], jnp.int32)
    ea = grp * EXP_PER_GROUP + pair_lo[pair]
    eb = grp * EXP_PER_GROUP + pair_hi[pair]
    valid = (tile < n_valid).astype(jnp.int32)
    return pos, src, ea, eb, valid


def _dft_cos_sin(seq):
    k = jnp.arange(seq, dtype=jnp.int32)
    ang = ((k[:, None] * k[None, :]) % (2 * seq)).astype(F32) * (math.pi / seq)
    return jnp.concatenate([jnp.cos(ang), jnp.sin(ang)], axis=0).astype(BF16)


def _positional_features(seq):
    t = jnp.arange(seq, dtype=F32)
    t01 = t / max(seq - 1, 1)
    ang = 2.0 * math.pi * t / seq
    bands = jnp.linspace(1e-4, HY_BANDS - 1, HY_BANDS, dtype=F32)
    pe = jnp.concatenate([t01[:, None], jnp.cos(ang[:, None] * bands), -jnp.sin(ang[:, None] * bands)], axis=-1)
    return jnp.pad(pe, ((0, 0), (0, LANES - pe.shape[1])))


def _layer(x3, mod, mod_row, s0f, s0b, n_seg, p, tiles):
    batch, seq, d = x3.shape
    x = x3.reshape(batch * seq, d)
    dk, dv = p["dk"], p["dv"]
    hy_width = p["hy_width"]

    proj, lr = _premix_proj(x, mod, mod_row, p["g_pre_mix"], p["w_main"], p["wlr_hi"], p["wlr_lo"],
                            tiles["tm_proj"], tiles["tn_proj"])
    y_gla, s_f, s_b = _gla(proj, lr, p["wdf"], p["bdf"], p["wdb"], p["bdb"], p["g_gla"], s0f, s0b,
                           batch, seq, dk, dv)

    ad, hl = _hy_filter(_positional_features(seq), p["hy_w1"], p["hy_b1"], p["hy_w2"], p["hy_b2"], p["hy_freq"],
                        p["hy_w3"], p["hy_deltas"], seq, hy_width, tiles["tc_filter"])
    cs = _dft_cos_sin(seq)
    hspec = _hy_spectrum(cs, ad, seq, hy_width, tiles["tr_spec"], tiles["tc_spec"])
    z_hy = _hy_conv(proj, p["hy_conv_w"], p["hy_conv_b"], cs, hspec, hl, p["hy_bias"], batch, seq, seq // n_seg,
                    p["hy_col0"], hy_width, tiles["tc_conv"])

    x1, hg, route, counts = _mix_out(y_gla.reshape(batch * seq, -1), z_hy.reshape(batch * seq, -1), x, mod, mod_row,
                                     p["g_hy"], p["g_post_mix"], p["g_pre_ffn"], p["w_out"], p["wr_hi"], p["wr_lo"],
                                     p["b_r"], tiles["tm_mix"])
    tm = tiles["tm_moe"]
    n_tiles = batch * seq // tm + N_GROUPS * PAIRS_PER_GROUP
    pos, src, ea, eb, valid = _route_tables(route, counts, n_tiles, tm)
    y_sorted = _moe(hg, src, ea, eb, valid, p["w_exp_gate"], p["w_exp_up"], p["w_exp_down"], n_tiles, tm)
    out = _ffn_out(y_sorted, pos, x1, mod, mod_row, p["g_post_ffn"], tiles["tm_out"])
    return out.reshape(batch, seq, d), s_f, s_b


def kernel(x_prompt, x_sample, c, state_gla_fwd, state_gla_bwd, c_ctx, w_ada, b_ada, g_pre_mix, g_post_mix, g_pre_ffn, g_post_ffn, w_in, w_dec_f, b_dec_f, w_dec_b, b_dec_b, g_gla, hy_conv_w, hy_conv_b, hy_w1, hy_b1, hy_w2, hy_b2, hy_w3, hy_freq, hy_bias, g_hy, w_out, w_router_grp, b_router_grp, w_router_exp, b_router_exp, w_exp_gate, w_exp_up, w_exp_down):
    depth = w_ada.shape[0]
    assert depth == 1
    l = 0
    d = x_prompt.shape[-1]
    dec_batch = x_sample.shape[0]
    heads = GLA_HEADS
    dk, dv = state_gla_fwd.shape[-2:]
    kdim = heads * dk
    gla_width = heads * dv
    hy_width = g_hy.shape[-1]
    hid = hy_w2.shape[-1]

    cond = jnp.concatenate([c_ctx[None, :], c], axis=0)
    cond = jnp.pad(cond, ((0, -cond.shape[0] % 8), (0, 0)))
    mod = _ada_mod(cond, w_ada[l], b_ada[l]).reshape(cond.shape[0], 6, d)

    w_in_l = w_in[l]
    n_main = 2 * kdim + 2 * gla_width
    lr0 = n_main
    hy0 = n_main + 2 * GLA_RANK
    w_main = jnp.concatenate([w_in_l[:, :n_main], w_in_l[:, hy0:]], axis=1).astype(BF16)
    w_lr = jnp.pad(w_in_l[:, lr0:hy0], ((0, 0), (0, LANES - 2 * GLA_RANK)))
    wlr_hi, wlr_lo = _split2(w_lr)

    def dec_weight(w, first_row):
        wh = w.reshape(GLA_RANK, heads, dk).transpose(1, 0, 2)
        return jnp.pad(wh, ((0, 0), (first_row, LANES - GLA_RANK - first_row), (0, 0)))

    deltas = jnp.abs(jnp.linspace(math.log(HY_TARGET) / HY_SLOW_PCT, math.log(HY_TARGET) / HY_FAST_PCT, hy_width,
                                  dtype=F32))
    w_r = jnp.pad(jnp.concatenate([w_router_exp[l], w_router_grp[l]], axis=1),
                  ((0, 0), (0, LANES - N_EXPERTS - N_GROUPS)))
    wr_hi, wr_lo = _split2(w_r)
    b_r = jnp.pad(jnp.concatenate([b_router_exp[l], b_router_grp[l]]), (0, LANES - N_EXPERTS - N_GROUPS))

    p = dict(
        dk=dk, dv=dv, hy_width=hy_width, hy_col0=n_main,
        g_pre_mix=g_pre_mix[l][None, :], g_post_mix=g_post_mix[l][None, :],
        g_pre_ffn=g_pre_ffn[l][None, :], g_post_ffn=g_post_ffn[l][None, :],
        w_main=w_main, wlr_hi=wlr_hi, wlr_lo=wlr_lo,
        wdf=dec_weight(w_dec_f[l], 0), bdf=b_dec_f[l][None, :],
        wdb=dec_weight(w_dec_b[l], GLA_RANK), bdb=b_dec_b[l][None, :],
        g_gla=g_gla[l][None, :],
        hy_conv_w=hy_conv_w[l], hy_conv_b=hy_conv_b[l][None, :],
        hy_w1=jnp.pad(hy_w1[l], ((0, LANES - hy_w1.shape[1]), (0, 0))), hy_b1=hy_b1[l][None, :],
        hy_w2=hy_w2[l], hy_b2=hy_b2[l][None, :], hy_freq=hy_freq[l][None, :],
        hy_w3=hy_w3[l].reshape(hid, 2 * HY_ORDER, hy_width).transpose(1, 0, 2), hy_deltas=deltas[None, :],
        hy_bias=hy_bias[l][:, None, :], g_hy=g_hy[l][None, :],
        w_out=w_out[l].astype(BF16), wr_hi=wr_hi, wr_lo=wr_lo, b_r=b_r[None, :],
        w_exp_gate=w_exp_gate[l].astype(BF16), w_exp_up=w_exp_up[l].astype(BF16),
        w_exp_down=w_exp_down[l].astype(BF16),
    )

    zero_state = jnp.zeros((1, 1, dk, dv), F32)
    dec_seq = x_sample.shape[1]
    tiles_p = dict(tm_proj=512, tn_proj=512, tc_filter=256, tr_spec=256, tc_spec=512, tc_conv=512,
                   tm_mix=256, tm_moe=256, tm_out=256)
    tiles_s = dict(tm_proj=512, tn_proj=512, tc_filter=256, tr_spec=512, tc_spec=512, tc_conv=256,
                   tm_mix=256, tm_moe=256, tm_out=256)

    y_p, s_f, s_b = _layer(x_prompt, mod, lambda r: 0, zero_state, zero_state, 1, p, tiles_p)
    y_s, _, _ = _layer(x_sample, mod, lambda r: 1 + r // dec_seq, state_gla_fwd[:, l], state_gla_bwd[:, l],
                       dec_seq // GRID_W, p, tiles_s)
    return (y_p, y_s, s_f[:, None].astype(x_prompt.dtype), s_b[:, None].astype(x_prompt.dtype))
```

```python
import functools
import math

import jax
import jax.numpy as jnp
from jax import lax
from jax.experimental import pallas as pl
from jax.experimental.pallas import tpu as pltpu

F32 = jnp.float32
BF16 = jnp.bfloat16

GRID_W = 64
GLA_HEADS = 4
GLA_RANK = 16
GLA_TAU = 16.0
GLA_CHUNK = 64
GLA_UNROLL = 4
HY_ORDER = 2
HY_SHORT = 3
HY_BANDS = 16
HY_TARGET = 1e-2
HY_FAST_PCT = 0.3
HY_SLOW_PCT = 1.5
N_GROUPS = 4
EXP_PER_GROUP = 4
N_EXPERTS = N_GROUPS * EXP_PER_GROUP
PAIRS_PER_GROUP = EXP_PER_GROUP * (EXP_PER_GROUP - 1) // 2
ROUTE_BUCKET_LANE = N_EXPERTS
ROUTE_RANK_LANE = N_EXPERTS + 1
EPS = 1e-6

LANES = 128
VMEM_LIMIT = 56 << 20


def _cparams(sem):
    return pltpu.CompilerParams(dimension_semantics=sem, vmem_limit_bytes=VMEM_LIMIT)


def _dot(a, b):
    return jnp.dot(a, b, preferred_element_type=F32)


def _dot_nt(a, b):
    return lax.dot_general(a, b, (((1,), (1,)), ((), ())), preferred_element_type=F32)


def _dot_tn(a, b):
    return lax.dot_general(a, b, (((0,), (0,)), ((), ())), preferred_element_type=F32)


def _split2(x):
    hi = x.astype(BF16)
    lo = (x - hi.astype(F32)).astype(BF16)
    return hi, lo


def _dot_hp(a, b):
    ah, al = _split2(a)
    bh, bl = _split2(b)
    return _dot(ah, bh) + (_dot(ah, bl) + _dot(al, bh))


def _dot_exact_lhs(t, x):
    x0 = x.astype(BF16)
    r = x - x0.astype(F32)
    x1 = r.astype(BF16)
    x2 = (r - x1.astype(F32)).astype(BF16)
    return _dot(t, x0) + (_dot(t, x1) + _dot(t, x2))


def _rms(x, g):
    return x * lax.rsqrt(jnp.mean(x * x, axis=-1, keepdims=True) + EPS) * g


def _silu(x):
    return x / (1.0 + jnp.exp(-x))


def _ada_kernel(c_ref, w_ref, b_ref, o_ref):
    s = _silu(c_ref[...]).astype(BF16)
    o_ref[...] = _dot(s, w_ref[...].astype(BF16)) + b_ref[...]


def _ada_mod(cond, w_ada, b_ada):
    rows, d = cond.shape
    n = w_ada.shape[1]
    tn = 1024
    return pl.pallas_call(
        _ada_kernel,
        grid=(n // tn,),
        in_specs=[pl.BlockSpec((rows, d), lambda j: (0, 0)),
                  pl.BlockSpec((d, tn), lambda j: (0, j)),
                  pl.BlockSpec((1, tn), lambda j: (0, j))],
        out_specs=pl.BlockSpec((rows, tn), lambda j: (0, j)),
        out_shape=jax.ShapeDtypeStruct((rows, n), F32),
        compiler_params=_cparams(("arbitrary",)),
        name="ada_mod",
    )(cond, w_ada, b_ada.reshape(1, n))


def _premix_kernel(x_ref, mod_ref, g_ref, w_ref, wlh_ref, wll_ref, o_ref, lr_ref, h_scr):
    @pl.when(pl.program_id(1) == 0)
    def _():
        h = _rms(x_ref[...], g_ref[...]) * (1.0 + mod_ref[0, 1:2, :]) + mod_ref[0, 0:1, :]
        hh, hl = _split2(h)
        h_scr[...] = hh
        lr_ref[...] = _dot(hh, wlh_ref[...]) + (_dot(hh, wll_ref[...]) + _dot(hl, wlh_ref[...]))

    o_ref[...] = _dot(h_scr[...], w_ref[...]).astype(o_ref.dtype)


def _premix_proj(x, mod, mod_row, g, w_main, wlr_hi, wlr_lo, tm, tn):
    n, d = x.shape
    nc = w_main.shape[1]
    return pl.pallas_call(
        _premix_kernel,
        grid=(n // tm, nc // tn),
        in_specs=[pl.BlockSpec((tm, d), lambda i, j: (i, 0)),
                  pl.BlockSpec((1, 6, d), lambda i, j: (mod_row(i * tm), 0, 0)),
                  pl.BlockSpec((1, d), lambda i, j: (0, 0)),
                  pl.BlockSpec((d, tn), lambda i, j: (0, j)),
                  pl.BlockSpec((d, LANES), lambda i, j: (0, 0)),
                  pl.BlockSpec((d, LANES), lambda i, j: (0, 0))],
        out_specs=[pl.BlockSpec((tm, tn), lambda i, j: (i, j)),
                   pl.BlockSpec((tm, LANES), lambda i, j: (i, 0))],
        out_shape=[jax.ShapeDtypeStruct((n, nc), BF16),
                   jax.ShapeDtypeStruct((n, LANES), F32)],
        scratch_shapes=[pltpu.VMEM((tm, d), BF16)],
        compiler_params=_cparams(("arbitrary", "arbitrary")),
        name="premix_proj",
    )(x, mod, g, w_main, wlr_hi, wlr_lo)


def _log_sigmoid(x):
    return jnp.minimum(x, 0.0) - jnp.log1p(jnp.exp(-jnp.abs(x)))


def _gla_kernel(q_ref, k_ref, v_ref, g_ref, lr_ref, wdf_ref, bdf_ref, wdb_ref, bdb_ref, gg_ref,
                s0f_ref, s0b_ref, y_ref, sf_ref, sb_ref,
                laf_scr, lab_scr, qf_scr, qb_scr, o_scr, uf_scr, ub_scr, df_scr, db_scr, sinf_scr, sinb_scr,
                *, seq, dk):
    c = GLA_CHUNK
    n_chunks = seq // c
    scale = dk ** -0.5

    lr = lr_ref[0]
    laf_scr[...] = _log_sigmoid(_dot_hp(lr, wdf_ref[0]) + bdf_ref[...]) / GLA_TAU
    lab_scr[...] = _log_sigmoid(_dot_hp(lr, wdb_ref[0]) + bdb_ref[...]) / GLA_TAU

    row = lax.broadcasted_iota(jnp.int32, (c, c), 0)
    col = lax.broadcasted_iota(jnp.int32, (c, c), 1)
    lower = row >= col
    upper = col >= row
    tl = lower.astype(BF16)
    tu = upper.astype(BF16)

    def chunk_local(n, carry):
        sl = pl.ds(pl.multiple_of(n * c, c), c)
        bf = _dot_exact_lhs(tl, laf_scr[sl, :])
        bb = _dot_exact_lhs(tu, lab_scr[sl, :])
        q = q_ref[0, sl, :].astype(F32) * scale
        k = k_ref[0, sl, :].astype(F32)
        v = v_ref[0, sl, :]
        bf_end = bf[c - 1:c, :]
        bb_end = bb[0:1, :]
        qf = (q * jnp.exp(bf)).astype(BF16)
        kf = (k * jnp.exp(-bf)).astype(BF16)
        ksf = (k * jnp.exp(bf_end - bf)).astype(BF16)
        qb = (q * jnp.exp(bb)).astype(BF16)
        kb = (k * jnp.exp(-bb)).astype(BF16)
        ksb = (k * jnp.exp(bb_end - bb)).astype(BF16)
        att = jnp.where(lower, _dot_nt(qf, kf), 0.0) + jnp.where(upper, _dot_nt(qb, kb), 0.0)
        o_scr[sl, :] = _dot(att.astype(BF16), v)
        qf_scr[sl, :] = qf
        qb_scr[sl, :] = qb
        uf_scr[n] = _dot_tn(v, ksf)
        ub_scr[n] = _dot_tn(v, ksb)
        df_scr[n] = jnp.broadcast_to(jnp.exp(bf_end), (8, dk))
        db_scr[n] = jnp.broadcast_to(jnp.exp(bb_end), (8, dk))
        return carry

    lax.fori_loop(0, n_chunks, chunk_local, 0, unroll=GLA_UNROLL)

    def scan_fwd(n, s):
        sinf_scr[n] = s.astype(BF16)
        return s * df_scr[n][0:1, :] + uf_scr[n]

    def scan_bwd(i, s):
        n = n_chunks - 1 - i
        sinb_scr[n] = s.astype(BF16)
        return s * db_scr[n][0:1, :] + ub_scr[n]

    s_f = lax.fori_loop(0, n_chunks, scan_fwd, s0f_ref[0, 0].T, unroll=GLA_UNROLL)
    s_b = lax.fori_loop(0, n_chunks, scan_bwd, s0b_ref[0, 0].T, unroll=GLA_UNROLL)
    sf_ref[0, 0] = s_f.T
    sb_ref[0, 0] = s_b.T

    def chunk_inter(n, carry):
        sl = pl.ds(pl.multiple_of(n * c, c), c)
        o_scr[sl, :] += _dot_nt(qf_scr[sl, :], sinf_scr[n]) + _dot_nt(qb_scr[sl, :], sinb_scr[n])
        return carry

    lax.fori_loop(0, n_chunks, chunk_inter, 0, unroll=GLA_UNROLL)

    o = _rms(o_scr[...], gg_ref[...])
    y_ref[0] = (o * _silu(g_ref[0].astype(F32))).astype(y_ref.dtype)


def _gla(proj, lr, wdf, bdf, wdb, bdb, g_gla, s0f, s0b, batch, seq, dk, dv):
    heads = GLA_HEADS
    proj3 = proj.reshape(batch, seq, proj.shape[-1])
    lr3 = lr.reshape(batch, seq, LANES)
    kdim = heads * dk
    width = heads * dv
    k_blk = kdim // dk
    v_blk = 2 * kdim // dv
    g_blk = (2 * kdim + width) // dv
    n_chunks = seq // GLA_CHUNK

    def s0_map(s0):
        if s0.shape[0] == batch:
            return lambda b, h: (b, h, 0, 0)
        return lambda b, h: (0, 0, 0, 0)

    kern = functools.partial(_gla_kernel, seq=seq, dk=dk)
    return pl.pallas_call(
        kern,
        grid=(batch, heads),
        in_specs=[pl.BlockSpec((1, seq, dk), lambda b, h: (b, 0, h)),
                  pl.BlockSpec((1, seq, dk), lambda b, h: (b, 0, k_blk + h)),
                  pl.BlockSpec((1, seq, dv), lambda b, h: (b, 0, v_blk + h)),
                  pl.BlockSpec((1, seq, dv), lambda b, h: (b, 0, g_blk + h)),
                  pl.BlockSpec((1, seq, LANES), lambda b, h: (b, 0, 0)),
                  pl.BlockSpec((1, LANES, dk), lambda b, h: (h, 0, 0)),
                  pl.BlockSpec((1, dk), lambda b, h: (0, h)),
                  pl.BlockSpec((1, LANES, dk), lambda b, h: (h, 0, 0)),
                  pl.BlockSpec((1, dk), lambda b, h: (0, h)),
                  pl.BlockSpec((1, dv), lambda b, h: (0, 0)),
                  pl.BlockSpec((1, 1, dk, dv), s0_map(s0f)),
                  pl.BlockSpec((1, 1, dk, dv), s0_map(s0b))],
        out_specs=[pl.BlockSpec((1, seq, dv), lambda b, h: (b, 0, h)),
                   pl.BlockSpec((1, 1, dk, dv), lambda b, h: (b, h, 0, 0)),
                   pl.BlockSpec((1, 1, dk, dv), lambda b, h: (b, h, 0, 0))],
        out_shape=[jax.ShapeDtypeStruct((batch, seq, width), BF16),
                   jax.ShapeDtypeStruct((batch, heads, dk, dv), F32),
                   jax.ShapeDtypeStruct((batch, heads, dk, dv), F32)],
        scratch_shapes=[pltpu.VMEM((seq, dk), F32), pltpu.VMEM((seq, dk), F32),
                        pltpu.VMEM((seq, dk), BF16), pltpu.VMEM((seq, dk), BF16),
                        pltpu.VMEM((seq, dv), F32),
                        pltpu.VMEM((n_chunks, dv, dk), F32), pltpu.VMEM((n_chunks, dv, dk), F32),
                        pltpu.VMEM((n_chunks, 8, dk), F32), pltpu.VMEM((n_chunks, 8, dk), F32),
                        pltpu.VMEM((n_chunks, dv, dk), BF16), pltpu.VMEM((n_chunks, dv, dk), BF16)],
        compiler_params=_cparams(("arbitrary", "arbitrary")),
        name="gla",
    )(proj3, proj3, proj3, proj3, lr3, wdf, bdf, wdb, bdb, g_gla, s0f, s0b)


def _filter_kernel(pe_ref, w1_ref, b1_ref, w2_ref, b2_ref, fr_ref, w3_ref, dl_ref, ad_ref, hl_ref, *, seq):
    pe = pe_ref[...]
    fr = fr_ref[...]
    h1 = jnp.sin(fr * (_dot_hp(pe, w1_ref[...]) + b1_ref[...]))
    h2 = jnp.sin(fr * (_dot_hp(h1, w2_ref[...]) + b2_ref[...]))
    dec = jnp.exp(-pe[:, 0:1] * dl_ref[...])
    row = lax.broadcasted_iota(jnp.int32, (seq, 1), 0)
    alt = jnp.where(row % 2 == 0, 1.0, -1.0)
    for o in range(HY_ORDER):
        ff = _dot_hp(h2, w3_ref[o]) * dec
        fb = _dot_hp(h2, w3_ref[HY_ORDER + o]) * dec
        nrm = jnp.sum(jnp.abs(ff), axis=0, keepdims=True) + jnp.sum(jnp.abs(fb), axis=0, keepdims=True)
        ff = ff / nrm
        fb = jnp.where(row == 0, 0.0, fb / nrm)
        a = ff + fb
        ad_ref[o, 0] = a.astype(ad_ref.dtype)
        ad_ref[o, 1] = (fb - ff).astype(ad_ref.dtype)
        hl_ref[o] = jnp.sum(alt * a, axis=0, keepdims=True)


def _hy_filter(pe, w1p, b1, w2, b2, freq, w3r, deltas, seq, width, tc):
    hid = w2.shape[0]
    kern = functools.partial(_filter_kernel, seq=seq)
    return pl.pallas_call(
        kern,
        grid=(width // tc,),
        in_specs=[pl.BlockSpec((seq, LANES), lambda c: (0, 0)),
                  pl.BlockSpec((LANES, hid), lambda c: (0, 0)),
                  pl.BlockSpec((1, hid), lambda c: (0, 0)),
                  pl.BlockSpec((hid, hid), lambda c: (0, 0)),
                  pl.BlockSpec((1, hid), lambda c: (0, 0)),
                  pl.BlockSpec((1, hid), lambda c: (0, 0)),
                  pl.BlockSpec((2 * HY_ORDER, hid, tc), lambda c: (0, 0, c)),
                  pl.BlockSpec((1, tc), lambda c: (0, c))],
        out_specs=[pl.BlockSpec((HY_ORDER, 2, seq, tc), lambda c: (0, 0, 0, c)),
                   pl.BlockSpec((HY_ORDER, 1, tc), lambda c: (0, 0, c))],
        out_shape=[jax.ShapeDtypeStruct((HY_ORDER, 2, seq, width), BF16),
                   jax.ShapeDtypeStruct((HY_ORDER, 1, width), F32)],
        compiler_params=_cparams(("arbitrary",)),
        name="hy_filter",
    )(pe, w1p, b1, w2, b2, freq, w3r, deltas)


def _spectrum_kernel(cs_ref, ad_ref, h_ref, *, seq, tr):
    r = pl.program_id(1)
    k = (r * tr) % seq + lax.broadcasted_iota(jnp.int32, (tr, 1), 0)
    wgt = jnp.where(k == 0, 1.0, 2.0) * (0.5 / seq)
    h_ref[0] = wgt * _dot(cs_ref[...], ad_ref[0, 0])


def _hy_spectrum(cs, ad, seq, width, tr, tc):
    kern = functools.partial(_spectrum_kernel, seq=seq, tr=tr)
    per_half = seq // tr
    return pl.pallas_call(
        kern,
        grid=(HY_ORDER, 2 * seq // tr, width // tc),
        in_specs=[pl.BlockSpec((tr, seq), lambda o, r, c: (r, 0)),
                  pl.BlockSpec((1, 1, seq, tc), lambda o, r, c: (o, r // per_half, 0, c))],
        out_specs=pl.BlockSpec((1, tr, tc), lambda o, r, c: (o, r, c)),
        out_shape=jax.ShapeDtypeStruct((HY_ORDER, 2 * seq, width), F32),
        compiler_params=_cparams(("arbitrary", "arbitrary", "arbitrary")),
        name="hy_spectrum",
    )(cs, ad)


def _hyconv_kernel(u0_ref, u1_ref, u2_ref, cw0_ref, cw1_ref, cw2_ref, cb0_ref, cb1_ref, cb2_ref,
                   cs_ref, h_ref, hl_ref, bias_ref, z_ref, y_scr, *, seq, seg):
    row = lax.broadcasted_iota(jnp.int32, (seq, 1), 0)
    pos = row % seg
    alt = jnp.where(row % 2 == 0, 1.0, -1.0)

    def short_conv(u_ref, cw_ref, cb_ref):
        u = u_ref[0].astype(F32)
        prev = jnp.where(pos == 0, 0.0, pltpu.roll(u, 1, 0))
        nxt = jnp.where(pos == seg - 1, 0.0, pltpu.roll(u, seq - 1, 0))
        return prev * cw_ref[0:1, :] + u * cw_ref[1:2, :] + nxt * cw_ref[2:3, :] + cb_ref[...]

    gates = ((u1_ref, cw1_ref, cb1_ref), (u2_ref, cw2_ref, cb2_ref))
    z = short_conv(u0_ref, cw0_ref, cb0_ref)
    kb = min(seq, 512)
    for n in range(HY_ORDER):
        zb = z.astype(BF16)
        for r in range(0, seq, kb):
            xc = _dot(cs_ref[r:r + kb, :], zb)
            xs = _dot(cs_ref[seq + r:seq + r + kb, :], zb)
            hre = h_ref[n, r:r + kb, :]
            him = h_ref[n, seq + r:seq + r + kb, :]
            y_scr[r:r + kb, :] = (xc * hre + xs * him).astype(BF16)
            y_scr[seq + r:seq + r + kb, :] = (xs * hre - xc * him).astype(BF16)
        nyq = jnp.sum(alt * z, axis=0, keepdims=True) * (hl_ref[n] * (0.5 / seq))
        conv = _dot(cs_ref[:seq, :], y_scr[:seq, :]) + _dot(cs_ref[seq:, :], y_scr[seq:, :]) + alt * nyq
        z = short_conv(*gates[n]) * (conv + bias_ref[n] * z)
    z_ref[0] = z.astype(z_ref.dtype)


def _hy_conv(proj, conv_w, conv_b, cs, hspec, hl, bias, batch, seq, seg, col0, width, tc):
    proj3 = proj.reshape(batch, seq, proj.shape[-1])
    kern = functools.partial(_hyconv_kernel, seq=seq, seg=seg)
    blk0 = col0 // tc
    per = width // tc

    def u_spec(p):
        return pl.BlockSpec((1, seq, tc), lambda c, b: (b, 0, blk0 + p * per + c))

    def w_spec(p, rows):
        return pl.BlockSpec((rows, tc), lambda c, b: (0, p * per + c))

    return pl.pallas_call(
        kern,
        grid=(per, batch),
        in_specs=[u_spec(0), u_spec(1), u_spec(2),
                  w_spec(0, HY_SHORT), w_spec(1, HY_SHORT), w_spec(2, HY_SHORT),
                  w_spec(0, 1), w_spec(1, 1), w_spec(2, 1),
                  pl.BlockSpec((2 * seq, seq), lambda c, b: (0, 0), pipeline_mode=pl.Buffered(1)),
                  pl.BlockSpec((HY_ORDER, 2 * seq, tc), lambda c, b: (0, 0, c), pipeline_mode=pl.Buffered(1)),
                  pl.BlockSpec((HY_ORDER, 1, tc), lambda c, b: (0, 0, c)),
                  pl.BlockSpec((HY_ORDER, 1, tc), lambda c, b: (0, 0, c))],
        out_specs=pl.BlockSpec((1, seq, tc), lambda c, b: (b, 0, c)),
        out_shape=jax.ShapeDtypeStruct((batch, seq, width), BF16),
        scratch_shapes=[pltpu.VMEM((2 * seq, tc), BF16)],
        compiler_params=_cparams(("arbitrary", "arbitrary")),
        name="hy_conv",
    )(proj3, proj3, proj3, conv_w, conv_w, conv_w, conv_b, conv_b, conv_b, cs, hspec, hl, bias)


def _mixout_kernel(yg_ref, zh_ref, x_ref, mod_ref, ghy_ref, gpost_ref, gpre_ref, wo_ref,
                   wrh_ref, wrl_ref, br_ref, x1_ref, hg_ref, route_ref, cnt_ref, cnt_scr, *, gla_width):
    d = x_ref.shape[1]
    tm = x_ref.shape[0]

    @pl.when(pl.program_id(0) == 0)
    def _():
        cnt_scr[...] = jnp.zeros_like(cnt_scr)

    yh = _rms(zh_ref[...].astype(F32), ghy_ref[...]).astype(BF16)
    y = _dot(yg_ref[...], wo_ref[:gla_width, :]) + _dot(yh, wo_ref[gla_width:, :])
    x1 = x_ref[...] + mod_ref[0, 2:3, :] * _rms(y, gpost_ref[...])
    x1_ref[...] = x1
    h2 = _rms(x1, gpre_ref[...]) * (1.0 + mod_ref[0, 4:5, :]) + mod_ref[0, 3:4, :]
    hh, hl = _split2(h2)
    hg_ref[:, :d] = h2
    logits = _dot(hh, wrh_ref[...]) + (_dot(hh, wrl_ref[...]) + _dot(hl, wrh_ref[...])) + br_ref[...]

    lane = lax.broadcasted_iota(jnp.int32, logits.shape, 1).astype(F32)
    neg = -jnp.inf
    is_grp = (lane >= N_EXPERTS) & (lane < N_EXPERTS + N_GROUPS)
    m = jnp.max(jnp.where(is_grp, logits, neg), axis=1, keepdims=True)
    p_grp = 1.0 / jnp.sum(jnp.where(is_grp, jnp.exp(logits - m), 0.0), axis=1, keepdims=True)
    grp = jnp.min(jnp.where(is_grp & (logits == m), lane - N_EXPERTS, 1e9), axis=1, keepdims=True)
    sel = (lane >= grp * EXP_PER_GROUP) & (lane < (grp + 1.0) * EXP_PER_GROUP)
    me = jnp.max(jnp.where(sel, logits, neg), axis=1, keepdims=True)
    pe = jnp.where(sel, jnp.exp(logits - me), -1.0)
    v1 = jnp.max(pe, axis=1, keepdims=True)
    i1 = jnp.min(jnp.where(pe == v1, lane, 1e9), axis=1, keepdims=True)
    pe2 = jnp.where(lane == i1, -1.0, pe)
    v2 = jnp.max(pe2, axis=1, keepdims=True)
    i2 = jnp.min(jnp.where(pe2 == v2, lane, 1e9), axis=1, keepdims=True)
    den = v1 + v2
    gates = jnp.where(lane == i1, v1 / den, jnp.where(lane == i2, v2 / den, 0.0)) * p_grp

    lo = jnp.minimum(i1, i2) - grp * EXP_PER_GROUP
    hi = jnp.maximum(i1, i2) - grp * EXP_PER_GROUP
    bucket = grp * PAIRS_PER_GROUP + lo * (2 * EXP_PER_GROUP - 1 - lo) * 0.5 + (hi - lo - 1.0)
    onehot = lane == bucket
    r_i = lax.broadcasted_iota(jnp.int32, (tm, tm), 0)
    c_i = lax.broadcasted_iota(jnp.int32, (tm, tm), 1)
    earlier = _dot((r_i > c_i).astype(BF16), onehot.astype(BF16)) + cnt_scr[...]
    rank = jnp.sum(jnp.where(onehot, earlier, 0.0), axis=1, keepdims=True)
    cnt_scr[...] += jnp.sum(onehot.astype(F32), axis=0, keepdims=True)
    cnt_ref[...] = cnt_scr[...]
    route = jnp.where(lane == ROUTE_BUCKET_LANE, bucket, jnp.where(lane == ROUTE_RANK_LANE, rank, gates))
    route_ref[...] = route
    hg_ref[:, d:] = route


def _mix_out(yg, zh, x, mod, mod_row, g_hy, g_post, g_pre, w_out, wr_hi, wr_lo, b_r, tm):
    n, d = x.shape
    gw = yg.shape[1]
    hw = zh.shape[1]
    kern = functools.partial(_mixout_kernel, gla_width=gw)
    row = lambda i: (i, 0)
    fixed = lambda i: (0, 0)
    return pl.pallas_call(
        kern,
        grid=(n // tm,),
        in_specs=[pl.BlockSpec((tm, gw), row),
                  pl.BlockSpec((tm, hw), row),
                  pl.BlockSpec((tm, d), row),
                  pl.BlockSpec((1, 6, d), lambda i: (mod_row(i * tm), 0, 0)),
                  pl.BlockSpec((1, hw), fixed),
                  pl.BlockSpec((1, d), fixed),
                  pl.BlockSpec((1, d), fixed),
                  pl.BlockSpec((gw + hw, d), fixed),
                  pl.BlockSpec((d, LANES), fixed),
                  pl.BlockSpec((d, LANES), fixed),
                  pl.BlockSpec((1, LANES), fixed)],
        out_specs=[pl.BlockSpec((tm, d), row), pl.BlockSpec((tm, d + LANES), row), pl.BlockSpec((tm, LANES), row),
                   pl.BlockSpec((1, LANES), fixed)],
        out_shape=[jax.ShapeDtypeStruct((n, d), F32),
                   jax.ShapeDtypeStruct((n, d + LANES), F32),
                   jax.ShapeDtypeStruct((n, LANES), F32),
                   jax.ShapeDtypeStruct((1, LANES), F32)],
        scratch_shapes=[pltpu.VMEM((1, LANES), F32)],
        compiler_params=_cparams(("arbitrary",)),
        name="mix_out",
    )(yg, zh, x, mod, g_hy, g_post, g_pre, w_out, wr_hi, wr_lo, b_r)


def _row_gather_copy(src_hbm, row, buf, slot, r, sem):
    return pltpu.make_async_copy(src_hbm.at[pl.ds(row, 1), :], buf.at[slot, pl.ds(r, 1), :], sem.at[slot])


def _start_row_gather(idx_ref, base, src_hbm, buf, slot, sem, tm):
    def body(r, carry):
        _row_gather_copy(src_hbm, idx_ref[base + r], buf, slot, r, sem).start()
        return carry

    lax.fori_loop(0, tm, body, 0, unroll=8)


def _start_row_gather_inline(idx_ref, base, src_hbm, buf, slot, sem, tm):
    for r in range(tm):
        _row_gather_copy(src_hbm, idx_ref[base + r], buf, slot, r, sem).start()


def _wait_row_gather(src_hbm, buf, slot, sem, tm):
    pltpu.make_async_copy(src_hbm.at[pl.ds(0, tm), :], buf.at[slot], sem.at[slot]).wait()


def _moe_kernel(src_ref, ea_ref, eb_ref, valid_ref, hg_hbm,
                wga_ref, wua_ref, wda_ref, wgb_ref, wub_ref, wdb_ref, y_ref, buf, sem, *, tm, d):
    t = pl.program_id(0)
    slot = t % 2

    @pl.when(t == 0)
    def _():
        _start_row_gather(src_ref, 0, hg_hbm, buf, 0, sem, tm)

    @pl.when(valid_ref[t] == 1)
    def _():
        _wait_row_gather(hg_hbm, buf, slot, sem, tm)
        _start_row_gather_inline(src_ref, (t + 1) * tm, hg_hbm, buf, 1 - slot, sem, tm)
        h = buf[slot, :, :d].astype(BF16)
        gates = buf[slot, :, d:]
        lane = lax.broadcasted_iota(jnp.int32, gates.shape, 1)

        def expert(wg_ref, wu_ref, wd_ref, e):
            he = (_silu(_dot(h, wg_ref[0])) * _dot(h, wu_ref[0])).astype(BF16)
            gate = jnp.sum(jnp.where(lane == e, gates, 0.0), axis=1, keepdims=True)
            return gate * _dot(he, wd_ref[0])

        y_ref[...] = expert(wga_ref, wua_ref, wda_ref, ea_ref[t]) + expert(wgb_ref, wub_ref, wdb_ref, eb_ref[t])

    @pl.when(valid_ref[t] == 0)
    def _():
        @pl.when(valid_ref[jnp.maximum(t - 1, 0)] == 1)
        def _():
            _wait_row_gather(hg_hbm, buf, slot, sem, tm)

        y_ref[...] = jnp.zeros_like(y_ref)


def _moe(hg, src, ea, eb, valid, wg, wu, wd, n_tiles, tm):
    d = hg.shape[1] - LANES
    _, _, de = wg.shape
    kern = functools.partial(_moe_kernel, tm=tm, d=d)
    wa = lambda t, src, ea, eb, valid: (ea[t], 0, 0)
    wb = lambda t, src, ea, eb, valid: (eb[t], 0, 0)
    return pl.pallas_call(
        kern,
        grid_spec=pltpu.PrefetchScalarGridSpec(
            num_scalar_prefetch=4,
            grid=(n_tiles,),
            in_specs=[pl.BlockSpec(memory_space=pl.ANY),
                      pl.BlockSpec((1, d, de), wa), pl.BlockSpec((1, d, de), wa), pl.BlockSpec((1, de, d), wa),
                      pl.BlockSpec((1, d, de), wb), pl.BlockSpec((1, d, de), wb), pl.BlockSpec((1, de, d), wb)],
            out_specs=pl.BlockSpec((tm, d), lambda t, src, ea, eb, valid: (t, 0)),
            scratch_shapes=[pltpu.VMEM((2, tm, d + LANES), F32), pltpu.SemaphoreType.DMA((2,))]),
        out_shape=jax.ShapeDtypeStruct((n_tiles * tm, d), F32),
        compiler_params=_cparams(("arbitrary",)),
        name="moe",
    )(src, ea, eb, valid, hg, wg, wu, wd, wg, wu, wd)


def _ffn_out_kernel(pos_ref, y_hbm, x1_ref, mod_ref, gpost_ref, o_ref, buf, sem, *, tm):
    i = pl.program_id(0)
    n_i = pl.num_programs(0)
    slot = i % 2

    @pl.when(i == 0)
    def _():
        _start_row_gather(pos_ref, 0, y_hbm, buf, 0, sem, tm)

    _wait_row_gather(y_hbm, buf, slot, sem, tm)

    @pl.when(i + 1 < n_i)
    def _():
        _start_row_gather_inline(pos_ref, (i + 1) * tm, y_hbm, buf, 1 - slot, sem, tm)
        o_ref[...] = x1_ref[...] + mod_ref[0, 5:6, :] * _rms(buf[slot], gpost_ref[...])

    @pl.when(i + 1 == n_i)
    def _():
        o_ref[...] = x1_ref[...] + mod_ref[0, 5:6, :] * _rms(buf[slot], gpost_ref[...])


def _ffn_out(y_sorted, pos, x1, mod, mod_row, g_post, tm):
    n, d = x1.shape
    kern = functools.partial(_ffn_out_kernel, tm=tm)
    return pl.pallas_call(
        kern,
        grid_spec=pltpu.PrefetchScalarGridSpec(
            num_scalar_prefetch=1,
            grid=(n // tm,),
            in_specs=[pl.BlockSpec(memory_space=pl.ANY),
                      pl.BlockSpec((tm, d), lambda i, pos: (i, 0)),
                      pl.BlockSpec((1, 6, d), lambda i, pos: (mod_row(i * tm), 0, 0)),
                      pl.BlockSpec((1, d), lambda i, pos: (0, 0))],
            out_specs=pl.BlockSpec((tm, d), lambda i, pos: (i, 0)),
            scratch_shapes=[pltpu.VMEM((2, tm, d), F32), pltpu.SemaphoreType.DMA((2,))]),
        out_shape=jax.ShapeDtypeStruct((n, d), F32),
        compiler_params=_cparams(("arbitrary",)),
        name="ffn_out",
    )(pos, y_sorted, x1, mod, g_post)


def _route_tables(route, counts, n_tiles, tm):
    n = route.shape[0]
    n_buckets = N_GROUPS * PAIRS_PER_GROUP
    cnt = counts[0, :n_buckets].astype(jnp.int32)
    padded = (cnt + tm - 1) // tm * tm
    ends = jnp.cumsum(padded)
    starts = ends - padded
    bucket = route[:, ROUTE_BUCKET_LANE].astype(jnp.int32)
    rank = route[:, ROUTE_RANK_LANE].astype(jnp.int32)
    pos = starts[bucket] + rank
    src = (jnp.arange(n_tiles * tm, dtype=jnp.int32) % n).at[pos].set(jnp.arange(n, dtype=jnp.int32))
    n_valid = ends[-1] // tm
    tile = jnp.arange(n_tiles, dtype=jnp.int32)
    used = jnp.minimum(tile, n_valid - 1)
    tile_bucket = jnp.sum((ends[None, :] <= (used * tm)[:, None]).astype(jnp.int32), axis=1)
    grp = tile_bucket // PAIRS_PER_GROUP
    pair = tile_bucket % PAIRS_PER_GROUP
    pair_lo = jnp.array([a for a in range(EXP_PER_GROUP) for b in range(a + 1, EXP_PER_GROUP)], jnp.int32)
    pair_hi = jnp.array([b for a in range(EXP_PER_GROUP) for b in range(a + 1, EXP_PER_GROUP)], jnp.int32)
    ea = grp * EXP_PER_GROUP + pair_lo[pair]
    eb = grp * EXP_PER_GROUP + pair_hi[pair]
    valid = (tile < n_valid).astype(jnp.int32)
    return pos, src, ea, eb, valid


def _dft_cos_sin(seq):
    k = jnp.arange(seq, dtype=jnp.int32)
    ang = ((k[:, None] * k[None, :]) % (2 * seq)).astype(F32) * (math.pi / seq)
    return jnp.concatenate([jnp.cos(ang), jnp.sin(ang)], axis=0).astype(BF16)


def _positional_features(seq):
    t = jnp.arange(seq, dtype=F32)
    t01 = t / max(seq - 1, 1)
    ang = 2.0 * math.pi * t / seq
    bands = jnp.linspace(1e-4, HY_BANDS - 1, HY_BANDS, dtype=F32)
    pe = jnp.concatenate([t01[:, None], jnp.cos(ang[:, None] * bands), -jnp.sin(ang[:, None] * bands)], axis=-1)
    return jnp.pad(pe, ((0, 0), (0, LANES - pe.shape[1])))


def _layer(x3, mod, mod_row, s0f, s0b, n_seg, p, tiles):
    batch, seq, d = x3.shape
    x = x3.reshape(batch * seq, d)
    dk, dv = p["dk"], p["dv"]
    hy_width = p["hy_width"]

    proj, lr = _premix_proj(x, mod, mod_row, p["g_pre_mix"], p["w_main"], p["wlr_hi"], p["wlr_lo"],
                            tiles["tm_proj"], tiles["tn_proj"])
    y_gla, s_f, s_b = _gla(proj, lr, p["wdf"], p["bdf"], p["wdb"], p["bdb"], p["g_gla"], s0f, s0b,
                           batch, seq, dk, dv)

    ad, hl = _hy_filter(_positional_features(seq), p["hy_w1"], p["hy_b1"], p["hy_w2"], p["hy_b2"], p["hy_freq"],
                        p["hy_w3"], p["hy_deltas"], seq, hy_width, tiles["tc_filter"])
    cs = _dft_cos_sin(seq)
    hspec = _hy_spectrum(cs, ad, seq, hy_width, tiles["tr_spec"], tiles["tc_spec"])
    z_hy = _hy_conv(proj, p["hy_conv_w"], p["hy_conv_b"], cs, hspec, hl, p["hy_bias"], batch, seq, seq // n_seg,
                    p["hy_col0"], hy_width, tiles["tc_conv"])

    x1, hg, route, counts = _mix_out(y_gla.reshape(batch * seq, -1), z_hy.reshape(batch * seq, -1), x, mod, mod_row,
                                     p["g_hy"], p["g_post_mix"], p["g_pre_ffn"], p["w_out"], p["wr_hi"], p["wr_lo"],
                                     p["b_r"], tiles["tm_mix"])
    tm = tiles["tm_moe"]
    n_tiles = batch * seq // tm + N_GROUPS * PAIRS_PER_GROUP
    pos, src, ea, eb, valid = _route_tables(route, counts, n_tiles, tm)
    y_sorted = _moe(hg, src, ea, eb, valid, p["w_exp_gate"], p["w_exp_up"], p["w_exp_down"], n_tiles, tm)
    out = _ffn_out(y_sorted, pos, x1, mod, mod_row, p["g_post_ffn"], tiles["tm_out"])
    return out.reshape(batch, seq, d), s_f, s_b


def kernel(x_prompt, x_sample, c, state_gla_fwd, state_gla_bwd, c_ctx, w_ada, b_ada, g_pre_mix, g_post_mix, g_pre_ffn, g_post_ffn, w_in, w_dec_f, b_dec_f, w_dec_b, b_dec_b, g_gla, hy_conv_w, hy_conv_b, hy_w1, hy_b1, hy_w2, hy_b2, hy_w3, hy_freq, hy_bias, g_hy, w_out, w_router_grp, b_router_grp, w_router_exp, b_router_exp, w_exp_gate, w_exp_up, w_exp_down):
    depth = w_ada.shape[0]
    assert depth == 1
    l = 0
    d = x_prompt.shape[-1]
    dec_batch = x_sample.shape[0]
    heads = GLA_HEADS
    dk, dv = state_gla_fwd.shape[-2:]
    kdim = heads * dk
    gla_width = heads * dv
    hy_width = g_hy.shape[-1]
    hid = hy_w2.shape[-1]

    cond = jnp.concatenate([c_ctx[None, :], c], axis=0)
    cond = jnp.pad(cond, ((0, -cond.shape[0] % 8), (0, 0)))
    mod = _ada_mod(cond, w_ada[l], b_ada[l]).reshape(cond.shape[0], 6, d)

    w_in_l = w_in[l]
    n_main = 2 * kdim + 2 * gla_width
    lr0 = n_main
    hy0 = n_main + 2 * GLA_RANK
    w_main = jnp.concatenate([w_in_l[:, :n_main], w_in_l[:, hy0:]], axis=1).astype(BF16)
    w_lr = jnp.pad(w_in_l[:, lr0:hy0], ((0, 0), (0, LANES - 2 * GLA_RANK)))
    wlr_hi, wlr_lo = _split2(w_lr)

    def dec_weight(w, first_row):
        wh = w.reshape(GLA_RANK, heads, dk).transpose(1, 0, 2)
        return jnp.pad(wh, ((0, 0), (first_row, LANES - GLA_RANK - first_row), (0, 0)))

    deltas = jnp.abs(jnp.linspace(math.log(HY_TARGET) / HY_SLOW_PCT, math.log(HY_TARGET) / HY_FAST_PCT, hy_width,
                                  dtype=F32))
    w_r = jnp.pad(jnp.concatenate([w_router_exp[l], w_router_grp[l]], axis=1),
                  ((0, 0), (0, LANES - N_EXPERTS - N_GROUPS)))
    wr_hi, wr_lo = _split2(w_r)
    b_r = jnp.pad(jnp.concatenate([b_router_exp[l], b_router_grp[l]]), (0, LANES - N_EXPERTS - N_GROUPS))

    p = dict(
        dk=dk, dv=dv, hy_width=hy_width, hy_col0=n_main,
        g_pre_mix=g_pre_mix[l][None, :], g_post_mix=g_post_mix[l][None, :],
        g_pre_ffn=g_pre_ffn[l][None, :], g_post_ffn=g_post_ffn[l][None, :],
        w_main=w_main, wlr_hi=wlr_hi, wlr_lo=wlr_lo,
        wdf=dec_weight(w_dec_f[l], 0), bdf=b_dec_f[l][None, :],
        wdb=dec_weight(w_dec_b[l], GLA_RANK), bdb=b_dec_b[l][None, :],
        g_gla=g_gla[l][None, :],
        hy_conv_w=hy_conv_w[l], hy_conv_b=hy_conv_b[l][None, :],
        hy_w1=jnp.pad(hy_w1[l], ((0, LANES - hy_w1.shape[1]), (0, 0))), hy_b1=hy_b1[l][None, :],
        hy_w2=hy_w2[l], hy_b2=hy_b2[l][None, :], hy_freq=hy_freq[l][None, :],
        hy_w3=hy_w3[l].reshape(hid, 2 * HY_ORDER, hy_width).transpose(1, 0, 2), hy_deltas=deltas[None, :],
        hy_bias=hy_bias[l][:, None, :], g_hy=g_hy[l][None, :],
        w_out=w_out[l].astype(BF16), wr_hi=wr_hi, wr_lo=wr_lo, b_r=b_r[None, :],
        w_exp_gate=w_exp_gate[l].astype(BF16), w_exp_up=w_exp_up[l].astype(BF16),
        w_exp_down=w_exp_down[l].astype(BF16),
    )

    zero_state = jnp.zeros((1, 1, dk, dv), F32)
    dec_seq = x_sample.shape[1]
    tiles_p = dict(tm_proj=512, tn_proj=512, tc_filter=256, tr_spec=256, tc_spec=512, tc_conv=512,
                   tm_mix=256, tm_moe=256, tm_out=256)
    tiles_s = dict(tm_proj=512, tn_proj=512, tc_filter=256, tr_spec=512, tc_spec=512, tc_conv=256,
                   tm_mix=256, tm_moe=256, tm_out=256)

    y_p, s_f, s_b = _layer(x_prompt, mod, lambda r: 0, zero_state, zero_state, 1, p, tiles_p)
    y_s, _, _ = _layer(x_sample, mod, lambda r: 1 + r // dec_seq, state_gla_fwd[:, l], state_gla_bwd[:, l],
                       dec_seq // GRID_W, p, tiles_s)
    return (y_p, y_s, s_f[:, None].astype(x_prompt.dtype), s_b[:, None].astype(x_prompt.dtype))
```

```python
import functools
import math

import jax
import jax.numpy as jnp
from jax import lax
from jax.experimental import pallas as pl
from jax.experimental.pallas import tpu as pltpu

F32 = jnp.float32
BF16 = jnp.bfloat16

GRID_W = 64
GLA_HEADS = 4
GLA_RANK = 16
GLA_TAU = 16.0
GLA_CHUNK = 64
GLA_UNROLL = 4
MIX_SUBTILES = 2
HY_ORDER = 2
HY_SHORT = 3
HY_BANDS = 16
HY_TARGET = 1e-2
HY_FAST_PCT = 0.3
HY_SLOW_PCT = 1.5
N_GROUPS = 4
EXP_PER_GROUP = 4
N_EXPERTS = N_GROUPS * EXP_PER_GROUP
PAIRS_PER_GROUP = EXP_PER_GROUP * (EXP_PER_GROUP - 1) // 2
ROUTE_BUCKET_LANE = N_EXPERTS
ROUTE_RANK_LANE = N_EXPERTS + 1
EPS = 1e-6

LANES = 128
VMEM_LIMIT = 56 << 20


def _cparams(sem):
    return pltpu.CompilerParams(dimension_semantics=sem, vmem_limit_bytes=VMEM_LIMIT)


def _dot(a, b):
    return jnp.dot(a, b, preferred_element_type=F32)


def _dot_nt(a, b):
    return lax.dot_general(a, b, (((1,), (1,)), ((), ())), preferred_element_type=F32)


def _dot_tn(a, b):
    return lax.dot_general(a, b, (((0,), (0,)), ((), ())), preferred_element_type=F32)


def _split2(x):
    hi = x.astype(BF16)
    lo = (x - hi.astype(F32)).astype(BF16)
    return hi, lo


def _dot_hp(a, b):
    ah, al = _split2(a)
    bh, bl = _split2(b)
    return _dot(ah, bh) + (_dot(ah, bl) + _dot(al, bh))


def _dot_exact_lhs(t, x):
    x0 = x.astype(BF16)
    r = x - x0.astype(F32)
    x1 = r.astype(BF16)
    x2 = (r - x1.astype(F32)).astype(BF16)
    return _dot(t, x0) + (_dot(t, x1) + _dot(t, x2))


def _rms(x, g):
    return x * lax.rsqrt(jnp.mean(x * x, axis=-1, keepdims=True) + EPS) * g


def _silu(x):
    return x / (1.0 + jnp.exp(-x))


def _ada_kernel(c_ref, w_ref, b_ref, o_ref):
    s = _silu(c_ref[...]).astype(BF16)
    o_ref[...] = _dot(s, w_ref[...].astype(BF16)) + b_ref[...]


def _ada_mod(cond, w_ada, b_ada):
    rows, d = cond.shape
    n = w_ada.shape[1]
    tn = 1024
    return pl.pallas_call(
        _ada_kernel,
        grid=(n // tn,),
        in_specs=[pl.BlockSpec((rows, d), lambda j: (0, 0)),
                  pl.BlockSpec((d, tn), lambda j: (0, j)),
                  pl.BlockSpec((1, tn), lambda j: (0, j))],
        out_specs=pl.BlockSpec((rows, tn), lambda j: (0, j)),
        out_shape=jax.ShapeDtypeStruct((rows, n), F32),
        compiler_params=_cparams(("arbitrary",)),
        name="ada_mod",
    )(cond, w_ada, b_ada.reshape(1, n))


def _repack_kernel(w_ref, main_ref, lrh_ref, lrl_ref, *, n_main, n_lr):
    rows = w_ref.shape[0]
    n_rest = w_ref.shape[1] - n_main - n_lr
    main_ref[:, :n_main] = w_ref[:, :n_main].astype(BF16)
    main_ref[:, n_main:] = w_ref[:, n_main + n_lr:].astype(BF16)
    assert main_ref.shape[1] == n_main + n_rest
    lr = jnp.concatenate([w_ref[:, n_main:n_main + n_lr], jnp.zeros((rows, LANES - n_lr), F32)], axis=1)
    hi, lo = _split2(lr)
    lrh_ref[...] = hi
    lrl_ref[...] = lo


def _repack_w_in(w_in, n_main, n_lr, tr):
    d, n_cols = w_in.shape
    n_out = n_cols - n_lr
    kern = functools.partial(_repack_kernel, n_main=n_main, n_lr=n_lr)
    return pl.pallas_call(
        kern,
        grid=(d // tr,),
        in_specs=[pl.BlockSpec((tr, n_cols), lambda i: (i, 0))],
        out_specs=[pl.BlockSpec((tr, n_out), lambda i: (i, 0)),
                   pl.BlockSpec((tr, LANES), lambda i: (i, 0)),
                   pl.BlockSpec((tr, LANES), lambda i: (i, 0))],
        out_shape=[jax.ShapeDtypeStruct((d, n_out), BF16),
                   jax.ShapeDtypeStruct((d, LANES), BF16),
                   jax.ShapeDtypeStruct((d, LANES), BF16)],
        compiler_params=_cparams(("arbitrary",)),
        name="repack_w_in",
    )(w_in)


def _premix_kernel(x_ref, mod_ref, g_ref, w_ref, wlh_ref, wll_ref, o_ref, lr_ref, h_scr):
    @pl.when(pl.program_id(1) == 0)
    def _():
        h = _rms(x_ref[...], g_ref[...]) * (1.0 + mod_ref[0, 1:2, :]) + mod_ref[0, 0:1, :]
        hh, hl = _split2(h)
        h_scr[...] = hh
        lr_ref[...] = _dot(hh, wlh_ref[...]) + (_dot(hh, wll_ref[...]) + _dot(hl, wlh_ref[...]))

    o_ref[...] = _dot(h_scr[...], w_ref[...]).astype(o_ref.dtype)


def _premix_proj(x, mod, mod_row, g, w_main, wlr_hi, wlr_lo, tm, tn):
    n, d = x.shape
    nc = w_main.shape[1]
    return pl.pallas_call(
        _premix_kernel,
        grid=(n // tm, nc // tn),
        in_specs=[pl.BlockSpec((tm, d), lambda i, j: (i, 0)),
                  pl.BlockSpec((1, 6, d), lambda i, j: (mod_row(i * tm), 0, 0)),
                  pl.BlockSpec((1, d), lambda i, j: (0, 0)),
                  pl.BlockSpec((d, tn), lambda i, j: (0, j)),
                  pl.BlockSpec((d, LANES), lambda i, j: (0, 0)),
                  pl.BlockSpec((d, LANES), lambda i, j: (0, 0))],
        out_specs=[pl.BlockSpec((tm, tn), lambda i, j: (i, j)),
                   pl.BlockSpec((tm, LANES), lambda i, j: (i, 0))],
        out_shape=[jax.ShapeDtypeStruct((n, nc), BF16),
                   jax.ShapeDtypeStruct((n, LANES), F32)],
        scratch_shapes=[pltpu.VMEM((tm, d), BF16)],
        compiler_params=_cparams(("arbitrary", "arbitrary")),
        name="premix_proj",
    )(x, mod, g, w_main, wlr_hi, wlr_lo)


def _log_sigmoid(x):
    return jnp.minimum(x, 0.0) - jnp.log1p(jnp.exp(-jnp.abs(x)))


def _gla_kernel(q_ref, k_ref, v_ref, g_ref, lr_ref, wdf_ref, bdf_ref, wdb_ref, bdb_ref, gg_ref,
                s0f_ref, s0b_ref, y_ref, sf_ref, sb_ref,
                laf_scr, lab_scr, qf_scr, qb_scr, o_scr, uf_scr, ub_scr, df_scr, db_scr, sinf_scr, sinb_scr,
                *, seq, dk):
    c = GLA_CHUNK
    n_chunks = seq // c
    scale = dk ** -0.5

    lr = lr_ref[0]
    laf_scr[...] = _log_sigmoid(_dot_hp(lr, wdf_ref[0]) + bdf_ref[...]) / GLA_TAU
    lab_scr[...] = _log_sigmoid(_dot_hp(lr, wdb_ref[0]) + bdb_ref[...]) / GLA_TAU

    row = lax.broadcasted_iota(jnp.int32, (c, c), 0)
    col = lax.broadcasted_iota(jnp.int32, (c, c), 1)
    lower = row >= col
    upper = col >= row
    tl = lower.astype(BF16)
    tu = upper.astype(BF16)

    def chunk_local(n, carry):
        sl = pl.ds(pl.multiple_of(n * c, c), c)
        bf = _dot_exact_lhs(tl, laf_scr[sl, :])
        bb = _dot_exact_lhs(tu, lab_scr[sl, :])
        q = q_ref[0, sl, :].astype(F32) * scale
        k = k_ref[0, sl, :].astype(F32)
        v = v_ref[0, sl, :]
        bf_end = bf[c - 1:c, :]
        bb_end = bb[0:1, :]
        qf = (q * jnp.exp(bf)).astype(BF16)
        kf = (k * jnp.exp(-bf)).astype(BF16)
        ksf = (k * jnp.exp(bf_end - bf)).astype(BF16)
        qb = (q * jnp.exp(bb)).astype(BF16)
        kb = (k * jnp.exp(-bb)).astype(BF16)
        ksb = (k * jnp.exp(bb_end - bb)).astype(BF16)
        att = jnp.where(lower, _dot_nt(qf, kf), 0.0) + jnp.where(upper, _dot_nt(qb, kb), 0.0)
        o_scr[sl, :] = _dot(att.astype(BF16), v)
        qf_scr[sl, :] = qf
        qb_scr[sl, :] = qb
        uf_scr[n] = _dot_tn(v, ksf)
        ub_scr[n] = _dot_tn(v, ksb)
        df_scr[n] = jnp.broadcast_to(jnp.exp(bf_end), (8, dk))
        db_scr[n] = jnp.broadcast_to(jnp.exp(bb_end), (8, dk))
        return carry

    lax.fori_loop(0, n_chunks, chunk_local, 0, unroll=GLA_UNROLL)

    def scan_fwd(n, s):
        sinf_scr[n] = s.astype(BF16)
        return s * df_scr[n][0:1, :] + uf_scr[n]

    def scan_bwd(i, s):
        n = n_chunks - 1 - i
        sinb_scr[n] = s.astype(BF16)
        return s * db_scr[n][0:1, :] + ub_scr[n]

    s_f = lax.fori_loop(0, n_chunks, scan_fwd, s0f_ref[0, 0].T, unroll=GLA_UNROLL)
    s_b = lax.fori_loop(0, n_chunks, scan_bwd, s0b_ref[0, 0].T, unroll=GLA_UNROLL)
    sf_ref[0, 0] = s_f.T
    sb_ref[0, 0] = s_b.T

    def chunk_inter(n, carry):
        sl = pl.ds(pl.multiple_of(n * c, c), c)
        o_scr[sl, :] += _dot_nt(qf_scr[sl, :], sinf_scr[n]) + _dot_nt(qb_scr[sl, :], sinb_scr[n])
        return carry

    lax.fori_loop(0, n_chunks, chunk_inter, 0, unroll=GLA_UNROLL)

    o = _rms(o_scr[...], gg_ref[...])
    y_ref[0] = (o * _silu(g_ref[0].astype(F32))).astype(y_ref.dtype)


def _gla(proj, lr, wdf, bdf, wdb, bdb, g_gla, s0f, s0b, batch, seq, dk, dv):
    heads = GLA_HEADS
    proj3 = proj.reshape(batch, seq, proj.shape[-1])
    lr3 = lr.reshape(batch, seq, LANES)
    kdim = heads * dk
    width = heads * dv
    k_blk = kdim // dk
    v_blk = 2 * kdim // dv
    g_blk = (2 * kdim + width) // dv
    n_chunks = seq // GLA_CHUNK

    def s0_map(s0):
        if s0.shape[0] == batch:
            return lambda b, h: (b, h, 0, 0)
        return lambda b, h: (0, 0, 0, 0)

    kern = functools.partial(_gla_kernel, seq=seq, dk=dk)
    return pl.pallas_call(
        kern,
        grid=(batch, heads),
        in_specs=[pl.BlockSpec((1, seq, dk), lambda b, h: (b, 0, h)),
                  pl.BlockSpec((1, seq, dk), lambda b, h: (b, 0, k_blk + h)),
                  pl.BlockSpec((1, seq, dv), lambda b, h: (b, 0, v_blk + h)),
                  pl.BlockSpec((1, seq, dv), lambda b, h: (b, 0, g_blk + h)),
                  pl.BlockSpec((1, seq, LANES), lambda b, h: (b, 0, 0)),
                  pl.BlockSpec((1, LANES, dk), lambda b, h: (h, 0, 0)),
                  pl.BlockSpec((1, dk), lambda b, h: (0, h)),
                  pl.BlockSpec((1, LANES, dk), lambda b, h: (h, 0, 0)),
                  pl.BlockSpec((1, dk), lambda b, h: (0, h)),
                  pl.BlockSpec((1, dv), lambda b, h: (0, 0)),
                  pl.BlockSpec((1, 1, dk, dv), s0_map(s0f)),
                  pl.BlockSpec((1, 1, dk, dv), s0_map(s0b))],
        out_specs=[pl.BlockSpec((1, seq, dv), lambda b, h: (b, 0, h)),
                   pl.BlockSpec((1, 1, dk, dv), lambda b, h: (b, h, 0, 0)),
                   pl.BlockSpec((1, 1, dk, dv), lambda b, h: (b, h, 0, 0))],
        out_shape=[jax.ShapeDtypeStruct((batch, seq, width), BF16),
                   jax.ShapeDtypeStruct((batch, heads, dk, dv), F32),
                   jax.ShapeDtypeStruct((batch, heads, dk, dv), F32)],
        scratch_shapes=[pltpu.VMEM((seq, dk), F32), pltpu.VMEM((seq, dk), F32),
                        pltpu.VMEM((seq, dk), BF16), pltpu.VMEM((seq, dk), BF16),
                        pltpu.VMEM((seq, dv), F32),
                        pltpu.VMEM((n_chunks, dv, dk), F32), pltpu.VMEM((n_chunks, dv, dk), F32),
                        pltpu.VMEM((n_chunks, 8, dk), F32), pltpu.VMEM((n_chunks, 8, dk), F32),
                        pltpu.VMEM((n_chunks, dv, dk), BF16), pltpu.VMEM((n_chunks, dv, dk), BF16)],
        compiler_params=_cparams(("arbitrary", "arbitrary")),
        name="gla",
    )(proj3, proj3, proj3, proj3, lr3, wdf, bdf, wdb, bdb, g_gla, s0f, s0b)


def _filter_kernel(pe_ref, w1_ref, b1_ref, w2_ref, b2_ref, fr_ref, w3_ref, dl_ref, ad_ref, hl_ref, *, seq):
    pe = pe_ref[...]
    fr = fr_ref[...]
    h1 = jnp.sin(fr * (_dot_hp(pe, w1_ref[...]) + b1_ref[...]))
    h2 = jnp.sin(fr * (_dot_hp(h1, w2_ref[...]) + b2_ref[...]))
    dec = jnp.exp(-pe[:, 0:1] * dl_ref[...])
    row = lax.broadcasted_iota(jnp.int32, (seq, 1), 0)
    alt = jnp.where(row % 2 == 0, 1.0, -1.0)
    for o in range(HY_ORDER):
        ff = _dot_hp(h2, w3_ref[o]) * dec
        fb = _dot_hp(h2, w3_ref[HY_ORDER + o]) * dec
        nrm = jnp.sum(jnp.abs(ff), axis=0, keepdims=True) + jnp.sum(jnp.abs(fb), axis=0, keepdims=True)
        ff = ff / nrm
        fb = jnp.where(row == 0, 0.0, fb / nrm)
        a = ff + fb
        ad_ref[o, 0] = a.astype(ad_ref.dtype)
        ad_ref[o, 1] = (fb - ff).astype(ad_ref.dtype)
        hl_ref[o] = jnp.sum(alt * a, axis=0, keepdims=True)


def _hy_filter(pe, w1p, b1, w2, b2, freq, w3r, deltas, seq, width, tc):
    hid = w2.shape[0]
    kern = functools.partial(_filter_kernel, seq=seq)
    return pl.pallas_call(
        kern,
        grid=(width // tc,),
        in_specs=[pl.BlockSpec((seq, LANES), lambda c: (0, 0)),
                  pl.BlockSpec((LANES, hid), lambda c: (0, 0)),
                  pl.BlockSpec((1, hid), lambda c: (0, 0)),
                  pl.BlockSpec((hid, hid), lambda c: (0, 0)),
                  pl.BlockSpec((1, hid), lambda c: (0, 0)),
                  pl.BlockSpec((1, hid), lambda c: (0, 0)),
                  pl.BlockSpec((2 * HY_ORDER, hid, tc), lambda c: (0, 0, c)),
                  pl.BlockSpec((1, tc), lambda c: (0, c))],
        out_specs=[pl.BlockSpec((HY_ORDER, 2, seq, tc), lambda c: (0, 0, 0, c)),
                   pl.BlockSpec((HY_ORDER, 1, tc), lambda c: (0, 0, c))],
        out_shape=[jax.ShapeDtypeStruct((HY_ORDER, 2, seq, width), BF16),
                   jax.ShapeDtypeStruct((HY_ORDER, 1, width), F32)],
        compiler_params=_cparams(("arbitrary",)),
        name="hy_filter",
    )(pe, w1p, b1, w2, b2, freq, w3r, deltas)


def _spectrum_kernel(cs_ref, ad_ref, h_ref, *, seq, tr):
    r = pl.program_id(1)
    k = (r * tr) % seq + lax.broadcasted_iota(jnp.int32, (tr, 1), 0)
    wgt = jnp.where(k == 0, 1.0, 2.0) * (0.5 / seq)
    h_ref[0] = wgt * _dot(cs_ref[...], ad_ref[0, 0])


def _hy_spectrum(cs, ad, seq, width, tr, tc):
    kern = functools.partial(_spectrum_kernel, seq=seq, tr=tr)
    per_half = seq // tr
    return pl.pallas_call(
        kern,
        grid=(HY_ORDER, 2 * seq // tr, width // tc),
        in_specs=[pl.BlockSpec((tr, seq), lambda o, r, c: (r, 0)),
                  pl.BlockSpec((1, 1, seq, tc), lambda o, r, c: (o, r // per_half, 0, c))],
        out_specs=pl.BlockSpec((1, tr, tc), lambda o, r, c: (o, r, c)),
        out_shape=jax.ShapeDtypeStruct((HY_ORDER, 2 * seq, width), F32),
        compiler_params=_cparams(("arbitrary", "arbitrary", "arbitrary")),
        name="hy_spectrum",
    )(cs, ad)


def _hyconv_kernel(u0_ref, u1_ref, u2_ref, cw0_ref, cw1_ref, cw2_ref, cb0_ref, cb1_ref, cb2_ref,
                   cs_ref, h_ref, hl_ref, bias_ref, z_ref, y_scr, *, seq, seg):
    row = lax.broadcasted_iota(jnp.int32, (seq, 1), 0)
    pos = row % seg
    alt = jnp.where(row % 2 == 0, 1.0, -1.0)

    def short_conv(u_ref, cw_ref, cb_ref):
        u = u_ref[0].astype(F32)
        prev = jnp.where(pos == 0, 0.0, pltpu.roll(u, 1, 0))
        nxt = jnp.where(pos == seg - 1, 0.0, pltpu.roll(u, seq - 1, 0))
        return prev * cw_ref[0:1, :] + u * cw_ref[1:2, :] + nxt * cw_ref[2:3, :] + cb_ref[...]

    gates = ((u1_ref, cw1_ref, cb1_ref), (u2_ref, cw2_ref, cb2_ref))
    z = short_conv(u0_ref, cw0_ref, cb0_ref)
    kb = min(seq, 512)
    for n in range(HY_ORDER):
        zb = z.astype(BF16)
        for r in range(0, seq, kb):
            xc = _dot(cs_ref[r:r + kb, :], zb)
            xs = _dot(cs_ref[seq + r:seq + r + kb, :], zb)
            hre = h_ref[n, r:r + kb, :]
            him = h_ref[n, seq + r:seq + r + kb, :]
            y_scr[r:r + kb, :] = (xc * hre + xs * him).astype(BF16)
            y_scr[seq + r:seq + r + kb, :] = (xs * hre - xc * him).astype(BF16)
        nyq = jnp.sum(alt * z, axis=0, keepdims=True) * (hl_ref[n] * (0.5 / seq))
        conv = _dot(cs_ref[:seq, :], y_scr[:seq, :]) + _dot(cs_ref[seq:, :], y_scr[seq:, :]) + alt * nyq
        z = short_conv(*gates[n]) * (conv + bias_ref[n] * z)
    z_ref[0] = z.astype(z_ref.dtype)


def _hy_conv(proj, conv_w, conv_b, cs, hspec, hl, bias, batch, seq, seg, col0, width, tc):
    proj3 = proj.reshape(batch, seq, proj.shape[-1])
    kern = functools.partial(_hyconv_kernel, seq=seq, seg=seg)
    blk0 = col0 // tc
    per = width // tc

    def u_spec(p):
        return pl.BlockSpec((1, seq, tc), lambda c, b: (b, 0, blk0 + p * per + c))

    def w_spec(p, rows):
        return pl.BlockSpec((rows, tc), lambda c, b: (0, p * per + c))

    return pl.pallas_call(
        kern,
        grid=(per, batch),
        in_specs=[u_spec(0), u_spec(1), u_spec(2),
                  w_spec(0, HY_SHORT), w_spec(1, HY_SHORT), w_spec(2, HY_SHORT),
                  w_spec(0, 1), w_spec(1, 1), w_spec(2, 1),
                  pl.BlockSpec((2 * seq, seq), lambda c, b: (0, 0), pipeline_mode=pl.Buffered(1)),
                  pl.BlockSpec((HY_ORDER, 2 * seq, tc), lambda c, b: (0, 0, c), pipeline_mode=pl.Buffered(1)),
                  pl.BlockSpec((HY_ORDER, 1, tc), lambda c, b: (0, 0, c)),
                  pl.BlockSpec((HY_ORDER, 1, tc), lambda c, b: (0, 0, c))],
        out_specs=pl.BlockSpec((1, seq, tc), lambda c, b: (b, 0, c)),
        out_shape=jax.ShapeDtypeStruct((batch, seq, width), BF16),
        scratch_shapes=[pltpu.VMEM((2 * seq, tc), BF16)],
        compiler_params=_cparams(("arbitrary", "arbitrary")),
        name="hy_conv",
    )(proj3, proj3, proj3, conv_w, conv_w, conv_w, conv_b, conv_b, conv_b, cs, hspec, hl, bias)


def _mixout_kernel(yg_ref, zh_ref, x_ref, mod_ref, ghy_ref, gpost_ref, gpre_ref, wo_ref,
                   wrh_ref, wrl_ref, br_ref, x1_ref, hg_ref, route_ref, cnt_ref, cnt_scr, *, gla_width):
    d = x_ref.shape[1]
    tm = x_ref.shape[0] // MIX_SUBTILES

    @pl.when(pl.program_id(0) == 0)
    def _():
        cnt_scr[...] = jnp.zeros_like(cnt_scr)

    for s in range(MIX_SUBTILES):
        _mixout_rows(slice(s * tm, (s + 1) * tm), tm, d, gla_width, yg_ref, zh_ref, x_ref, mod_ref, ghy_ref,
                     gpost_ref, gpre_ref, wo_ref, wrh_ref, wrl_ref, br_ref, x1_ref, hg_ref, route_ref, cnt_scr)
    cnt_ref[...] = cnt_scr[...]


def _mixout_rows(rows, tm, d, gla_width, yg_ref, zh_ref, x_ref, mod_ref, ghy_ref, gpost_ref, gpre_ref, wo_ref,
                 wrh_ref, wrl_ref, br_ref, x1_ref, hg_ref, route_ref, cnt_scr):
    yh = _rms(zh_ref[rows, :].astype(F32), ghy_ref[...]).astype(BF16)
    y = _dot(yg_ref[rows, :], wo_ref[:gla_width, :]) + _dot(yh, wo_ref[gla_width:, :])
    x1 = x_ref[rows, :] + mod_ref[0, 2:3, :] * _rms(y, gpost_ref[...])
    x1_ref[rows, :] = x1
    h2 = _rms(x1, gpre_ref[...]) * (1.0 + mod_ref[0, 4:5, :]) + mod_ref[0, 3:4, :]
    hh, hl = _split2(h2)
    hg_ref[rows, :d] = h2
    logits = _dot(hh, wrh_ref[...]) + (_dot(hh, wrl_ref[...]) + _dot(hl, wrh_ref[...])) + br_ref[...]

    lane = lax.broadcasted_iota(jnp.int32, logits.shape, 1).astype(F32)
    neg = -jnp.inf
    is_grp = (lane >= N_EXPERTS) & (lane < N_EXPERTS + N_GROUPS)
    m = jnp.max(jnp.where(is_grp, logits, neg), axis=1, keepdims=True)
    p_grp = 1.0 / jnp.sum(jnp.where(is_grp, jnp.exp(logits - m), 0.0), axis=1, keepdims=True)
    grp = jnp.min(jnp.where(is_grp & (logits == m), lane - N_EXPERTS, 1e9), axis=1, keepdims=True)
    sel = (lane >= grp * EXP_PER_GROUP) & (lane < (grp + 1.0) * EXP_PER_GROUP)
    me = jnp.max(jnp.where(sel, logits, neg), axis=1, keepdims=True)
    pe = jnp.where(sel, jnp.exp(logits - me), -1.0)
    v1 = jnp.max(pe, axis=1, keepdims=True)
    i1 = jnp.min(jnp.where(pe == v1, lane, 1e9), axis=1, keepdims=True)
    pe2 = jnp.where(lane == i1, -1.0, pe)
    v2 = jnp.max(pe2, axis=1, keepdims=True)
    i2 = jnp.min(jnp.where(pe2 == v2, lane, 1e9), axis=1, keepdims=True)
    den = v1 + v2
    gates = jnp.where(lane == i1, v1 / den, jnp.where(lane == i2, v2 / den, 0.0)) * p_grp

    lo = jnp.minimum(i1, i2) - grp * EXP_PER_GROUP
    hi = jnp.maximum(i1, i2) - grp * EXP_PER_GROUP
    bucket = grp * PAIRS_PER_GROUP + lo * (2 * EXP_PER_GROUP - 1 - lo) * 0.5 + (hi - lo - 1.0)
    onehot = lane == bucket
    r_i = lax.broadcasted_iota(jnp.int32, (tm, tm), 0)
    c_i = lax.broadcasted_iota(jnp.int32, (tm, tm), 1)
    earlier = _dot((r_i > c_i).astype(BF16), onehot.astype(BF16)) + cnt_scr[...]
    rank = jnp.sum(jnp.where(onehot, earlier, 0.0), axis=1, keepdims=True)
    cnt_scr[...] += jnp.sum(onehot.astype(F32), axis=0, keepdims=True)
    route = jnp.where(lane == ROUTE_BUCKET_LANE, bucket, jnp.where(lane == ROUTE_RANK_LANE, rank, gates))
    hg_ref[rows, d:] = route
    route_ref[:, rows] = route.T[ROUTE_BUCKET_LANE:ROUTE_BUCKET_LANE + 8, :]


def _mix_out(yg, zh, x, mod, mod_row, g_hy, g_post, g_pre, w_out, wr_hi, wr_lo, b_r, tm):
    n, d = x.shape
    gw = yg.shape[1]
    hw = zh.shape[1]
    kern = functools.partial(_mixout_kernel, gla_width=gw)
    row = lambda i: (i, 0)
    fixed = lambda i: (0, 0)
    return pl.pallas_call(
        kern,
        grid=(n // tm,),
        in_specs=[pl.BlockSpec((tm, gw), row),
                  pl.BlockSpec((tm, hw), row),
                  pl.BlockSpec((tm, d), row),
                  pl.BlockSpec((1, 6, d), lambda i: (mod_row(i * tm), 0, 0)),
                  pl.BlockSpec((1, hw), fixed),
                  pl.BlockSpec((1, d), fixed),
                  pl.BlockSpec((1, d), fixed),
                  pl.BlockSpec((gw + hw, d), fixed),
                  pl.BlockSpec((d, LANES), fixed),
                  pl.BlockSpec((d, LANES), fixed),
                  pl.BlockSpec((1, LANES), fixed)],
        out_specs=[pl.BlockSpec((tm, d), row), pl.BlockSpec((tm, d + LANES), row), pl.BlockSpec((8, tm), lambda i: (0, i)),
                   pl.BlockSpec((1, LANES), fixed)],
        out_shape=[jax.ShapeDtypeStruct((n, d), F32),
                   jax.ShapeDtypeStruct((n, d + LANES), F32),
                   jax.ShapeDtypeStruct((8, n), F32),
                   jax.ShapeDtypeStruct((1, LANES), F32)],
        scratch_shapes=[pltpu.VMEM((1, LANES), F32)],
        compiler_params=_cparams(("arbitrary",)),
        name="mix_out",
    )(yg, zh, x, mod, g_hy, g_post, g_pre, w_out, wr_hi, wr_lo, b_r)


def _row_gather_copy(src_hbm, row, buf, slot, r, sem):
    return pltpu.make_async_copy(src_hbm.at[pl.ds(row, 1), :], buf.at[slot, pl.ds(r, 1), :], sem.at[slot])


def _start_row_gather(idx_ref, base, src_hbm, buf, slot, sem, tm):
    def body(r, carry):
        _row_gather_copy(src_hbm, idx_ref[base + r], buf, slot, r, sem).start()
        return carry

    lax.fori_loop(0, tm, body, 0, unroll=8)


def _start_row_gather_inline(idx_ref, base, src_hbm, buf, slot, sem, tm):
    for r in range(tm):
        _row_gather_copy(src_hbm, idx_ref[base + r], buf, slot, r, sem).start()


def _wait_row_gather(src_hbm, buf, slot, sem, tm):
    pltpu.make_async_copy(src_hbm.at[pl.ds(0, tm), :], buf.at[slot], sem.at[slot]).wait()


def _moe_kernel(src_ref, ea_ref, eb_ref, valid_ref, hg_hbm,
                wga_ref, wua_ref, wda_ref, wgb_ref, wub_ref, wdb_ref, y_ref, buf, sem, *, tm, d):
    t = pl.program_id(0)
    slot = t % 2

    @pl.when(t == 0)
    def _():
        _start_row_gather(src_ref, 0, hg_hbm, buf, 0, sem, tm)

    @pl.when(valid_ref[t] == 1)
    def _():
        _wait_row_gather(hg_hbm, buf, slot, sem, tm)
        _start_row_gather_inline(src_ref, (t + 1) * tm, hg_hbm, buf, 1 - slot, sem, tm)
        h = buf[slot, :, :d].astype(BF16)
        gates = buf[slot, :, d:]
        lane = lax.broadcasted_iota(jnp.int32, gates.shape, 1)

        def expert(wg_ref, wu_ref, wd_ref, e):
            he = (_silu(_dot(h, wg_ref[0])) * _dot(h, wu_ref[0])).astype(BF16)
            gate = jnp.sum(jnp.where(lane == e, gates, 0.0), axis=1, keepdims=True)
            return gate * _dot(he, wd_ref[0])

        y_ref[...] = expert(wga_ref, wua_ref, wda_ref, ea_ref[t]) + expert(wgb_ref, wub_ref, wdb_ref, eb_ref[t])

    @pl.when(valid_ref[t] == 0)
    def _():
        @pl.when(valid_ref[jnp.maximum(t - 1, 0)] == 1)
        def _():
            _wait_row_gather(hg_hbm, buf, slot, sem, tm)

        y_ref[...] = jnp.zeros_like(y_ref)


def _moe(hg, src, ea, eb, valid, wg, wu, wd, n_tiles, tm):
    d = hg.shape[1] - LANES
    _, _, de = wg.shape
    kern = functools.partial(_moe_kernel, tm=tm, d=d)
    wa = lambda t, src, ea, eb, valid: (ea[t], 0, 0)
    wb = lambda t, src, ea, eb, valid: (eb[t], 0, 0)
    return pl.pallas_call(
        kern,
        grid_spec=pltpu.PrefetchScalarGridSpec(
            num_scalar_prefetch=4,
            grid=(n_tiles,),
            in_specs=[pl.BlockSpec(memory_space=pl.ANY),
                      pl.BlockSpec((1, d, de), wa), pl.BlockSpec((1, d, de), wa), pl.BlockSpec((1, de, d), wa),
                      pl.BlockSpec((1, d, de), wb), pl.BlockSpec((1, d, de), wb), pl.BlockSpec((1, de, d), wb)],
            out_specs=pl.BlockSpec((tm, d), lambda t, src, ea, eb, valid: (t, 0)),
            scratch_shapes=[pltpu.VMEM((2, tm, d + LANES), F32), pltpu.SemaphoreType.DMA((2,))]),
        out_shape=jax.ShapeDtypeStruct((n_tiles * tm, d), F32),
        compiler_params=_cparams(("arbitrary",)),
        name="moe",
    )(src, ea, eb, valid, hg, wg, wu, wd, wg, wu, wd)


def _ffn_out_kernel(pos_ref, y_hbm, x1_ref, mod_ref, gpost_ref, o_ref, buf, sem, *, tm):
    i = pl.program_id(0)
    n_i = pl.num_programs(0)
    slot = i % 2

    @pl.when(i == 0)
    def _():
        _start_row_gather(pos_ref, 0, y_hbm, buf, 0, sem, tm)

    _wait_row_gather(y_hbm, buf, slot, sem, tm)

    @pl.when(i + 1 < n_i)
    def _():
        _start_row_gather_inline(pos_ref, (i + 1) * tm, y_hbm, buf, 1 - slot, sem, tm)
        o_ref[...] = x1_ref[...] + mod_ref[0, 5:6, :] * _rms(buf[slot], gpost_ref[...])

    @pl.when(i + 1 == n_i)
    def _():
        o_ref[...] = x1_ref[...] + mod_ref[0, 5:6, :] * _rms(buf[slot], gpost_ref[...])


def _ffn_out(y_sorted, pos, x1, mod, mod_row, g_post, tm):
    n, d = x1.shape
    kern = functools.partial(_ffn_out_kernel, tm=tm)
    return pl.pallas_call(
        kern,
        grid_spec=pltpu.PrefetchScalarGridSpec(
            num_scalar_prefetch=1,
            grid=(n // tm,),
            in_specs=[pl.BlockSpec(memory_space=pl.ANY),
                      pl.BlockSpec((tm, d), lambda i, pos: (i, 0)),
                      pl.BlockSpec((1, 6, d), lambda i, pos: (mod_row(i * tm), 0, 0)),
                      pl.BlockSpec((1, d), lambda i, pos: (0, 0))],
            out_specs=pl.BlockSpec((tm, d), lambda i, pos: (i, 0)),
            scratch_shapes=[pltpu.VMEM((2, tm, d), F32), pltpu.SemaphoreType.DMA((2,))]),
        out_shape=jax.ShapeDtypeStruct((n, d), F32),
        compiler_params=_cparams(("arbitrary",)),
        name="ffn_out",
    )(pos, y_sorted, x1, mod, g_post)


def _route_tables(route, counts, n_tiles, tm):
    n = route.shape[1]
    n_buckets = N_GROUPS * PAIRS_PER_GROUP
    cnt = counts[0, :n_buckets].astype(jnp.int32)
    padded = (cnt + tm - 1) // tm * tm
    ends = jnp.cumsum(padded)
    starts = ends - padded
    bucket = route[0].astype(jnp.int32)
    rank = route[ROUTE_RANK_LANE - ROUTE_BUCKET_LANE].astype(jnp.int32)
    pos = starts[bucket] + rank
    src = (jnp.arange(n_tiles * tm, dtype=jnp.int32) % n).at[pos].set(jnp.arange(n, dtype=jnp.int32))
    n_valid = ends[-1] // tm
    tile = jnp.arange(n_tiles, dtype=jnp.int32)
    used = jnp.minimum(tile, n_valid - 1)
    tile_bucket = jnp.sum((ends[None, :] <= (used * tm)[:, None]).astype(jnp.int32), axis=1)
    grp = tile_bucket // PAIRS_PER_GROUP
    pair = tile_bucket % PAIRS_PER_GROUP
    pair_lo = jnp.array([a for a in range(EXP_PER_GROUP) for b in range(a + 1, EXP_PER_GROUP)], jnp.int32)
    pair_hi = jnp.array([b for a in range(EXP_PER_GROUP) for b in range(a + 1, EXP_PER_GROUP)], jnp.int32)
    ea = grp * EXP_PER_GROUP + pair_lo[pair]
    eb = grp * EXP_PER_GROUP + pair_hi[pair]
    valid = (tile < n_valid).astype(jnp.int32)
    return pos, src, ea, eb, valid


def _dft_kernel(cs_ref, *, seq, tr):
    r = pl.program_id(0)
    per_half = seq // tr
    is_sin = r >= per_half
    k = (r % per_half) * tr + lax.broadcasted_iota(jnp.int32, (tr, LANES), 0)
    j = lax.broadcasted_iota(jnp.int32, (tr, LANES), 1)
    period = 2 * seq

    def table(step):
        ang = ((k * j * step) % period).astype(F32) * (math.pi / seq)
        return jnp.cos(ang), jnp.sin(ang)

    c0, s0 = table(1)
    c1, s1 = table(LANES)
    p = jnp.where(is_sin, s0, c0)
    q = jnp.where(is_sin, c0, -s0)
    for t1 in range(seq // LANES):
        cs_ref[:, t1 * LANES:(t1 + 1) * LANES] = (c1[:, t1:t1 + 1] * p + s1[:, t1:t1 + 1] * q).astype(cs_ref.dtype)


def _dft_cos_sin(seq, tr):
    kern = functools.partial(_dft_kernel, seq=seq, tr=tr)
    return pl.pallas_call(
        kern,
        grid=(2 * seq // tr,),
        out_specs=pl.BlockSpec((tr, seq), lambda r: (r, 0)),
        out_shape=jax.ShapeDtypeStruct((2 * seq, seq), BF16),
        compiler_params=_cparams(("arbitrary",)),
        name="dft_matrix",
    )()


def _positional_features(seq):
    t = jnp.arange(seq, dtype=F32)
    t01 = t / max(seq - 1, 1)
    ang = 2.0 * math.pi * t / seq
    bands = jnp.linspace(1e-4, HY_BANDS - 1, HY_BANDS, dtype=F32)
    pe = jnp.concatenate([t01[:, None], jnp.cos(ang[:, None] * bands), -jnp.sin(ang[:, None] * bands)], axis=-1)
    return jnp.pad(pe, ((0, 0), (0, LANES - pe.shape[1])))


def _layer(x3, mod, mod_row, s0f, s0b, n_seg, p, tiles):
    batch, seq, d = x3.shape
    x = x3.reshape(batch * seq, d)
    dk, dv = p["dk"], p["dv"]
    hy_width = p["hy_width"]

    proj, lr = _premix_proj(x, mod, mod_row, p["g_pre_mix"], p["w_main"], p["wlr_hi"], p["wlr_lo"],
                            tiles["tm_proj"], tiles["tn_proj"])
    y_gla, s_f, s_b = _gla(proj, lr, p["wdf"], p["bdf"], p["wdb"], p["bdb"], p["g_gla"], s0f, s0b,
                           batch, seq, dk, dv)

    ad, hl = _hy_filter(_positional_features(seq), p["hy_w1"], p["hy_b1"], p["hy_w2"], p["hy_b2"], p["hy_freq"],
                        p["hy_w3"], p["hy_deltas"], seq, hy_width, tiles["tc_filter"])
    cs = _dft_cos_sin(seq, tiles["tr_spec"])
    hspec = _hy_spectrum(cs, ad, seq, hy_width, tiles["tr_spec"], tiles["tc_spec"])
    z_hy = _hy_conv(proj, p["hy_conv_w"], p["hy_conv_b"], cs, hspec, hl, p["hy_bias"], batch, seq, seq // n_seg,
                    p["hy_col0"], hy_width, tiles["tc_conv"])

    x1, hg, route, counts = _mix_out(y_gla.reshape(batch * seq, -1), z_hy.reshape(batch * seq, -1), x, mod, mod_row,
                                     p["g_hy"], p["g_post_mix"], p["g_pre_ffn"], p["w_out"], p["wr_hi"], p["wr_lo"],
                                     p["b_r"], tiles["tm_mix"])
    tm = tiles["tm_moe"]
    n_tiles = batch * seq // tm + N_GROUPS * PAIRS_PER_GROUP
    pos, src, ea, eb, valid = _route_tables(route, counts, n_tiles, tm)
    y_sorted = _moe(hg, src, ea, eb, valid, p["w_exp_gate"], p["w_exp_up"], p["w_exp_down"], n_tiles, tm)
    out = _ffn_out(y_sorted, pos, x1, mod, mod_row, p["g_post_ffn"], tiles["tm_out"])
    return out.reshape(batch, seq, d), s_f, s_b


def kernel(x_prompt, x_sample, c, state_gla_fwd, state_gla_bwd, c_ctx, w_ada, b_ada, g_pre_mix, g_post_mix, g_pre_ffn, g_post_ffn, w_in, w_dec_f, b_dec_f, w_dec_b, b_dec_b, g_gla, hy_conv_w, hy_conv_b, hy_w1, hy_b1, hy_w2, hy_b2, hy_w3, hy_freq, hy_bias, g_hy, w_out, w_router_grp, b_router_grp, w_router_exp, b_router_exp, w_exp_gate, w_exp_up, w_exp_down):
    depth = w_ada.shape[0]
    assert depth == 1
    l = 0
    d = x_prompt.shape[-1]
    dec_batch = x_sample.shape[0]
    heads = GLA_HEADS
    dk, dv = state_gla_fwd.shape[-2:]
    kdim = heads * dk
    gla_width = heads * dv
    hy_width = g_hy.shape[-1]
    hid = hy_w2.shape[-1]

    cond = jnp.concatenate([c_ctx[None, :], c], axis=0)
    cond = jnp.pad(cond, ((0, -cond.shape[0] % 8), (0, 0)))
    mod = _ada_mod(cond, w_ada[l], b_ada[l]).reshape(cond.shape[0], 6, d)

    n_main = 2 * kdim + 2 * gla_width
    w_main, wlr_hi, wlr_lo = _repack_w_in(w_in[l], n_main, 2 * GLA_RANK, 256)

    def dec_weight(w, first_row):
        wh = w.reshape(GLA_RANK, heads, dk).transpose(1, 0, 2)
        return jnp.pad(wh, ((0, 0), (first_row, LANES - GLA_RANK - first_row), (0, 0)))

    deltas = jnp.abs(jnp.linspace(math.log(HY_TARGET) / HY_SLOW_PCT, math.log(HY_TARGET) / HY_FAST_PCT, hy_width,
                                  dtype=F32))
    w_r = jnp.pad(jnp.concatenate([w_router_exp[l], w_router_grp[l]], axis=1),
                  ((0, 0), (0, LANES - N_EXPERTS - N_GROUPS)))
    wr_hi, wr_lo = _split2(w_r)
    b_r = jnp.pad(jnp.concatenate([b_router_exp[l], b_router_grp[l]]), (0, LANES - N_EXPERTS - N_GROUPS))

    p = dict(
        dk=dk, dv=dv, hy_width=hy_width, hy_col0=n_main,
        g_pre_mix=g_pre_mix[l][None, :], g_post_mix=g_post_mix[l][None, :],
        g_pre_ffn=g_pre_ffn[l][None, :], g_post_ffn=g_post_ffn[l][None, :],
        w_main=w_main, wlr_hi=wlr_hi, wlr_lo=wlr_lo,
        wdf=dec_weight(w_dec_f[l], 0), bdf=b_dec_f[l][None, :],
        wdb=dec_weight(w_dec_b[l], GLA_RANK), bdb=b_dec_b[l][None, :],
        g_gla=g_gla[l][None, :],
        hy_conv_w=hy_conv_w[l], hy_conv_b=hy_conv_b[l][None, :],
        hy_w1=jnp.pad(hy_w1[l], ((0, LANES - hy_w1.shape[1]), (0, 0))), hy_b1=hy_b1[l][None, :],
        hy_w2=hy_w2[l], hy_b2=hy_b2[l][None, :], hy_freq=hy_freq[l][None, :],
        hy_w3=hy_w3[l].reshape(hid, 2 * HY_ORDER, hy_width).transpose(1, 0, 2), hy_deltas=deltas[None, :],
        hy_bias=hy_bias[l][:, None, :], g_hy=g_hy[l][None, :],
        w_out=w_out[l].astype(BF16), wr_hi=wr_hi, wr_lo=wr_lo, b_r=b_r[None, :],
        w_exp_gate=w_exp_gate[l].astype(BF16), w_exp_up=w_exp_up[l].astype(BF16),
        w_exp_down=w_exp_down[l].astype(BF16),
    )

    zero_state = jnp.zeros((1, 1, dk, dv), F32)
    dec_seq = x_sample.shape[1]
    tiles_p = dict(tm_proj=512, tn_proj=2048, tc_filter=256, tr_spec=256, tc_spec=512, tc_conv=512,
                   tm_mix=512, tm_moe=256, tm_out=256)
    tiles_s = dict(tm_proj=512, tn_proj=2048, tc_filter=256, tr_spec=512, tc_spec=512, tc_conv=256,
                   tm_mix=512, tm_moe=256, tm_out=256)

    y_p, s_f, s_b = _layer(x_prompt, mod, lambda r: 0, zero_state, zero_state, 1, p, tiles_p)
    y_s, _, _ = _layer(x_sample, mod, lambda r: 1 + r // dec_seq, state_gla_fwd[:, l], state_gla_bwd[:, l],
                       dec_seq // GRID_W, p, tiles_s)
    return (y_p, y_s, s_f[:, None].astype(x_prompt.dtype), s_b[:, None].astype(x_prompt.dtype))
```

```python
import functools
import math

import jax
import jax.numpy as jnp
from jax import lax
from jax.experimental import pallas as pl
from jax.experimental.pallas import tpu as pltpu

F32 = jnp.float32
BF16 = jnp.bfloat16

GRID_W = 64
GLA_HEADS = 4
GLA_RANK = 16
GLA_TAU = 16.0
GLA_CHUNK = 64
GLA_UNROLL = 4
MIX_SUBTILES = 2
HY_ORDER = 2
HY_SHORT = 3
HY_BANDS = 16
HY_TARGET = 1e-2
HY_FAST_PCT = 0.3
HY_SLOW_PCT = 1.5
N_GROUPS = 4
EXP_PER_GROUP = 4
N_EXPERTS = N_GROUPS * EXP_PER_GROUP
PAIRS_PER_GROUP = EXP_PER_GROUP * (EXP_PER_GROUP - 1) // 2
ROUTE_BUCKET_LANE = N_EXPERTS
ROUTE_RANK_LANE = N_EXPERTS + 1
EPS = 1e-6

LANES = 128
VMEM_LIMIT = 56 << 20


def _cparams(sem):
    return pltpu.CompilerParams(dimension_semantics=sem, vmem_limit_bytes=VMEM_LIMIT)


def _dot(a, b):
    return jnp.dot(a, b, preferred_element_type=F32)


def _dot_nt(a, b):
    return lax.dot_general(a, b, (((1,), (1,)), ((), ())), preferred_element_type=F32)


def _dot_tn(a, b):
    return lax.dot_general(a, b, (((0,), (0,)), ((), ())), preferred_element_type=F32)


def _split2(x):
    hi = x.astype(BF16)
    lo = (x - hi.astype(F32)).astype(BF16)
    return hi, lo


def _dot_hp(a, b):
    ah, al = _split2(a)
    bh, bl = _split2(b)
    return _dot(ah, bh) + (_dot(ah, bl) + _dot(al, bh))


def _dot_exact_lhs(t, x):
    hi, lo = _split2(x)
    return _dot(t, hi) + _dot(t, lo)


def _rms(x, g):
    return x * lax.rsqrt(jnp.mean(x * x, axis=-1, keepdims=True) + EPS) * g


def _silu(x):
    return x / (1.0 + jnp.exp(-x))


def _ada_kernel(c_ref, w_ref, b_ref, o_ref):
    s = _silu(c_ref[...]).astype(BF16)
    o_ref[...] = _dot(s, w_ref[...].astype(BF16)) + b_ref[...]


def _ada_mod(cond, w_ada, b_ada):
    rows, d = cond.shape
    n = w_ada.shape[1]
    tn = 1024
    return pl.pallas_call(
        _ada_kernel,
        grid=(n // tn,),
        in_specs=[pl.BlockSpec((rows, d), lambda j: (0, 0)),
                  pl.BlockSpec((d, tn), lambda j: (0, j)),
                  pl.BlockSpec((1, tn), lambda j: (0, j))],
        out_specs=pl.BlockSpec((rows, tn), lambda j: (0, j)),
        out_shape=jax.ShapeDtypeStruct((rows, n), F32),
        compiler_params=_cparams(("arbitrary",)),
        name="ada_mod",
    )(cond, w_ada, b_ada.reshape(1, n))


def _repack_kernel(w_ref, main_ref, lrh_ref, lrl_ref, *, n_main, n_lr):
    rows = w_ref.shape[1]
    main_ref[:, :n_main] = w_ref[0, :, :n_main].astype(BF16)
    main_ref[:, n_main:] = w_ref[0, :, n_main + n_lr:].astype(BF16)
    lr = jnp.concatenate([w_ref[0, :, n_main:n_main + n_lr], jnp.zeros((rows, LANES - n_lr), F32)], axis=1)
    hi, lo = _split2(lr)
    lrh_ref[...] = hi
    lrl_ref[...] = lo


def _repack_w_in(w_in, layer, n_main, n_lr, tr):
    _, d, n_cols = w_in.shape
    n_out = n_cols - n_lr
    kern = functools.partial(_repack_kernel, n_main=n_main, n_lr=n_lr)
    return pl.pallas_call(
        kern,
        grid=(d // tr,),
        in_specs=[pl.BlockSpec((1, tr, n_cols), lambda i: (layer, i, 0))],
        out_specs=[pl.BlockSpec((tr, n_out), lambda i: (i, 0)),
                   pl.BlockSpec((tr, LANES), lambda i: (i, 0)),
                   pl.BlockSpec((tr, LANES), lambda i: (i, 0))],
        out_shape=[jax.ShapeDtypeStruct((d, n_out), BF16),
                   jax.ShapeDtypeStruct((d, LANES), BF16),
                   jax.ShapeDtypeStruct((d, LANES), BF16)],
        compiler_params=_cparams(("arbitrary",)),
        name="repack_w_in",
    )(w_in)


def _premix_kernel(x_ref, mod_ref, g_ref, w_ref, wlh_ref, wll_ref, o_ref, lr_ref, h_scr):
    @pl.when(pl.program_id(1) == 0)
    def _():
        h = _rms(x_ref[...], g_ref[...]) * (1.0 + mod_ref[0, 1:2, :]) + mod_ref[0, 0:1, :]
        hh, hl = _split2(h)
        h_scr[...] = hh
        lr_ref[...] = _dot(hh, wlh_ref[...]) + (_dot(hh, wll_ref[...]) + _dot(hl, wlh_ref[...]))

    o_ref[...] = _dot(h_scr[...], w_ref[...]).astype(o_ref.dtype)


def _premix_proj(x, mod, mod_row, g, w_main, wlr_hi, wlr_lo, tm, tn):
    n, d = x.shape
    nc = w_main.shape[1]
    return pl.pallas_call(
        _premix_kernel,
        grid=(n // tm, nc // tn),
        in_specs=[pl.BlockSpec((tm, d), lambda i, j: (i, 0)),
                  pl.BlockSpec((1, 6, d), lambda i, j: (mod_row(i * tm), 0, 0)),
                  pl.BlockSpec((1, d), lambda i, j: (0, 0)),
                  pl.BlockSpec((d, tn), lambda i, j: (0, j)),
                  pl.BlockSpec((d, LANES), lambda i, j: (0, 0)),
                  pl.BlockSpec((d, LANES), lambda i, j: (0, 0))],
        out_specs=[pl.BlockSpec((tm, tn), lambda i, j: (i, j)),
                   pl.BlockSpec((tm, LANES), lambda i, j: (i, 0))],
        out_shape=[jax.ShapeDtypeStruct((n, nc), BF16),
                   jax.ShapeDtypeStruct((n, LANES), F32)],
        scratch_shapes=[pltpu.VMEM((tm, d), BF16)],
        compiler_params=_cparams(("arbitrary", "arbitrary")),
        name="premix_proj",
    )(x, mod, g, w_main, wlr_hi, wlr_lo)


def _log_sigmoid(x):
    return jnp.minimum(x, 0.0) - jnp.log1p(jnp.exp(-jnp.abs(x)))


def _gla_kernel(q_ref, k_ref, v_ref, g_ref, lr_ref, wdf_ref, bdf_ref, wdb_ref, bdb_ref, gg_ref,
                s0f_ref, s0b_ref, y_ref, sf_ref, sb_ref,
                laf_scr, lab_scr, qf_scr, qb_scr, o_scr, uf_scr, ub_scr, df_scr, db_scr, sinf_scr, sinb_scr,
                *, seq, dk):
    c = GLA_CHUNK
    n_chunks = seq // c
    scale = dk ** -0.5

    lr = lr_ref[0]
    laf_scr[...] = _log_sigmoid(_dot_hp(lr, wdf_ref[0]) + bdf_ref[...]) / GLA_TAU
    lab_scr[...] = _log_sigmoid(_dot_hp(lr, wdb_ref[0]) + bdb_ref[...]) / GLA_TAU

    per = GLA_UNROLL
    blk = per * c
    row = lax.broadcasted_iota(jnp.int32, (blk, blk), 0)
    col = lax.broadcasted_iota(jnp.int32, (blk, blk), 1)
    same = (row // c) == (col // c)
    lower = same & (row >= col)
    upper = same & (col >= row)
    t_fwd = lower.astype(BF16)
    t_bwd = upper.astype(BF16)

    def chunk_rows(x, r):
        return jnp.concatenate([jnp.broadcast_to(x[j * c + r:j * c + r + 1], (c, dk)) for j in range(per)], axis=0)

    def block_local(m, carry):
        sl = pl.ds(pl.multiple_of(m * blk, blk), blk)
        bf = _dot_exact_lhs(t_fwd, laf_scr[sl, :])
        bb = _dot_exact_lhs(t_bwd, lab_scr[sl, :])
        tot_f = chunk_rows(bf, c - 1)
        tot_b = chunk_rows(bb, 0)
        q = q_ref[0, sl, :].astype(F32) * scale
        k = k_ref[0, sl, :].astype(F32)
        v = v_ref[0, sl, :]
        qf = (q * jnp.exp(bf)).astype(BF16)
        kf = (k * jnp.exp(-bf)).astype(BF16)
        ksf = (k * jnp.exp(tot_f - bf)).astype(BF16)
        qb = (q * jnp.exp(bb)).astype(BF16)
        kb = (k * jnp.exp(-bb)).astype(BF16)
        ksb = (k * jnp.exp(tot_b - bb)).astype(BF16)
        att = jnp.where(lower, _dot_nt(qf, kf), 0.0) + jnp.where(upper, _dot_nt(qb, kb), 0.0)
        o_scr[sl, :] = _dot(att.astype(BF16), v)
        qf_scr[sl, :] = qf
        qb_scr[sl, :] = qb
        dec_f = jnp.exp(tot_f)
        dec_b = jnp.exp(tot_b)
        for j in range(per):
            n = m * per + j
            rows = slice(j * c, (j + 1) * c)
            uf_scr[n] = _dot_tn(v[rows], ksf[rows])
            ub_scr[n] = _dot_tn(v[rows], ksb[rows])
            df_scr[n] = dec_f[j * c:j * c + 8]
            db_scr[n] = dec_b[j * c:j * c + 8]
        return carry

    lax.fori_loop(0, n_chunks // per, block_local, 0)

    def scan_fwd(n, s):
        sinf_scr[n] = s.astype(BF16)
        return s * df_scr[n][0:1, :] + uf_scr[n]

    def scan_bwd(i, s):
        n = n_chunks - 1 - i
        sinb_scr[n] = s.astype(BF16)
        return s * db_scr[n][0:1, :] + ub_scr[n]

    s_f = lax.fori_loop(0, n_chunks, scan_fwd, s0f_ref[0, 0].T, unroll=GLA_UNROLL)
    s_b = lax.fori_loop(0, n_chunks, scan_bwd, s0b_ref[0, 0].T, unroll=GLA_UNROLL)
    sf_ref[0, 0] = s_f.T
    sb_ref[0, 0] = s_b.T

    def chunk_inter(n, carry):
        sl = pl.ds(pl.multiple_of(n * c, c), c)
        o_scr[sl, :] += _dot_nt(qf_scr[sl, :], sinf_scr[n]) + _dot_nt(qb_scr[sl, :], sinb_scr[n])
        return carry

    lax.fori_loop(0, n_chunks, chunk_inter, 0, unroll=GLA_UNROLL)

    o = _rms(o_scr[...], gg_ref[...])
    y_ref[0] = (o * _silu(g_ref[0].astype(F32))).astype(y_ref.dtype)


def _gla(proj, lr, wdf, bdf, wdb, bdb, g_gla, s0f, s0b, batch, seq, dk, dv):
    heads = GLA_HEADS
    proj3 = proj.reshape(batch, seq, proj.shape[-1])
    lr3 = lr.reshape(batch, seq, LANES)
    kdim = heads * dk
    width = heads * dv
    k_blk = kdim // dk
    v_blk = 2 * kdim // dv
    g_blk = (2 * kdim + width) // dv
    n_chunks = seq // GLA_CHUNK

    def s0_map(s0):
        if s0.shape[0] == batch:
            return lambda b, h: (b, h, 0, 0)
        return lambda b, h: (0, 0, 0, 0)

    kern = functools.partial(_gla_kernel, seq=seq, dk=dk)
    return pl.pallas_call(
        kern,
        grid=(batch, heads),
        in_specs=[pl.BlockSpec((1, seq, dk), lambda b, h: (b, 0, h)),
                  pl.BlockSpec((1, seq, dk), lambda b, h: (b, 0, k_blk + h)),
                  pl.BlockSpec((1, seq, dv), lambda b, h: (b, 0, v_blk + h)),
                  pl.BlockSpec((1, seq, dv), lambda b, h: (b, 0, g_blk + h)),
                  pl.BlockSpec((1, seq, LANES), lambda b, h: (b, 0, 0)),
                  pl.BlockSpec((1, LANES, dk), lambda b, h: (h, 0, 0)),
                  pl.BlockSpec((1, dk), lambda b, h: (0, h)),
                  pl.BlockSpec((1, LANES, dk), lambda b, h: (h, 0, 0)),
                  pl.BlockSpec((1, dk), lambda b, h: (0, h)),
                  pl.BlockSpec((1, dv), lambda b, h: (0, 0)),
                  pl.BlockSpec((1, 1, dk, dv), s0_map(s0f)),
                  pl.BlockSpec((1, 1, dk, dv), s0_map(s0b))],
        out_specs=[pl.BlockSpec((1, seq, dv), lambda b, h: (b, 0, h)),
                   pl.BlockSpec((1, 1, dk, dv), lambda b, h: (b, h, 0, 0)),
                   pl.BlockSpec((1, 1, dk, dv), lambda b, h: (b, h, 0, 0))],
        out_shape=[jax.ShapeDtypeStruct((batch, seq, width), BF16),
                   jax.ShapeDtypeStruct((batch, heads, dk, dv), F32),
                   jax.ShapeDtypeStruct((batch, heads, dk, dv), F32)],
        scratch_shapes=[pltpu.VMEM((seq, dk), F32), pltpu.VMEM((seq, dk), F32),
                        pltpu.VMEM((seq, dk), BF16), pltpu.VMEM((seq, dk), BF16),
                        pltpu.VMEM((seq, dv), F32),
                        pltpu.VMEM((n_chunks, dv, dk), F32), pltpu.VMEM((n_chunks, dv, dk), F32),
                        pltpu.VMEM((n_chunks, 8, dk), F32), pltpu.VMEM((n_chunks, 8, dk), F32),
                        pltpu.VMEM((n_chunks, dv, dk), BF16), pltpu.VMEM((n_chunks, dv, dk), BF16)],
        compiler_params=_cparams(("arbitrary", "arbitrary")),
        name="gla",
    )(proj3, proj3, proj3, proj3, lr3, wdf, bdf, wdb, bdb, g_gla, s0f, s0b)


def _filter_kernel(pe_ref, w1_ref, b1_ref, w2_ref, b2_ref, fr_ref, w3_ref, dl_ref, ad_ref, hl_ref, *, seq):
    pe = pe_ref[...]
    fr = fr_ref[...]
    h1 = jnp.sin(fr * (_dot_hp(pe, w1_ref[...]) + b1_ref[...]))
    h2 = jnp.sin(fr * (_dot_hp(h1, w2_ref[...]) + b2_ref[...]))
    dec = jnp.exp(-pe[:, 0:1] * dl_ref[...])
    row = lax.broadcasted_iota(jnp.int32, (seq, 1), 0)
    alt = jnp.where(row % 2 == 0, 1.0, -1.0)
    for o in range(HY_ORDER):
        ff = _dot_hp(h2, w3_ref[o]) * dec
        fb = _dot_hp(h2, w3_ref[HY_ORDER + o]) * dec
        nrm = jnp.sum(jnp.abs(ff), axis=0, keepdims=True) + jnp.sum(jnp.abs(fb), axis=0, keepdims=True)
        ff = ff / nrm
        fb = jnp.where(row == 0, 0.0, fb / nrm)
        a = ff + fb
        ad_ref[o, 0] = a.astype(ad_ref.dtype)
        ad_ref[o, 1] = (fb - ff).astype(ad_ref.dtype)
        hl_ref[o] = jnp.sum(alt * a, axis=0, keepdims=True)


def _hy_filter(pe, w1p, b1, w2, b2, freq, w3r, deltas, seq, width, tc):
    hid = w2.shape[0]
    kern = functools.partial(_filter_kernel, seq=seq)
    return pl.pallas_call(
        kern,
        grid=(width // tc,),
        in_specs=[pl.BlockSpec((seq, LANES), lambda c: (0, 0)),
                  pl.BlockSpec((LANES, hid), lambda c: (0, 0)),
                  pl.BlockSpec((1, hid), lambda c: (0, 0)),
                  pl.BlockSpec((hid, hid), lambda c: (0, 0)),
                  pl.BlockSpec((1, hid), lambda c: (0, 0)),
                  pl.BlockSpec((1, hid), lambda c: (0, 0)),
                  pl.BlockSpec((2 * HY_ORDER, hid, tc), lambda c: (0, 0, c)),
                  pl.BlockSpec((1, tc), lambda c: (0, c))],
        out_specs=[pl.BlockSpec((HY_ORDER, 2, seq, tc), lambda c: (0, 0, 0, c)),
                   pl.BlockSpec((HY_ORDER, 1, tc), lambda c: (0, 0, c))],
        out_shape=[jax.ShapeDtypeStruct((HY_ORDER, 2, seq, width), BF16),
                   jax.ShapeDtypeStruct((HY_ORDER, 1, width), F32)],
        compiler_params=_cparams(("arbitrary",)),
        name="hy_filter",
    )(pe, w1p, b1, w2, b2, freq, w3r, deltas)


def _spectrum_kernel(cs_ref, ad_ref, h_ref, *, seq, tr):
    r = pl.program_id(1)
    k = (r * tr) % seq + lax.broadcasted_iota(jnp.int32, (tr, 1), 0)
    wgt = jnp.where(k == 0, 1.0, 2.0) * (0.5 / seq)
    h_ref[0] = wgt * _dot(cs_ref[...], ad_ref[0, 0])


def _hy_spectrum(cs, ad, seq, width, tr, tc):
    kern = functools.partial(_spectrum_kernel, seq=seq, tr=tr)
    per_half = seq // tr
    return pl.pallas_call(
        kern,
        grid=(HY_ORDER, 2 * seq // tr, width // tc),
        in_specs=[pl.BlockSpec((tr, seq), lambda o, r, c: (r, 0)),
                  pl.BlockSpec((1, 1, seq, tc), lambda o, r, c: (o, r // per_half, 0, c))],
        out_specs=pl.BlockSpec((1, tr, tc), lambda o, r, c: (o, r, c)),
        out_shape=jax.ShapeDtypeStruct((HY_ORDER, 2 * seq, width), F32),
        compiler_params=_cparams(("arbitrary", "arbitrary", "arbitrary")),
        name="hy_spectrum",
    )(cs, ad)


def _hyconv_kernel(u0_ref, u1_ref, u2_ref, cw0_ref, cw1_ref, cw2_ref, cb0_ref, cb1_ref, cb2_ref,
                   cs_ref, h_ref, hl_ref, bias_ref, z_ref, y_scr, *, seq, seg):
    row = lax.broadcasted_iota(jnp.int32, (seq, 1), 0)
    pos = row % seg
    alt = jnp.where(row % 2 == 0, 1.0, -1.0)

    def short_conv(u_ref, cw_ref, cb_ref):
        u = u_ref[0].astype(F32)
        prev = jnp.where(pos == 0, 0.0, pltpu.roll(u, 1, 0))
        nxt = jnp.where(pos == seg - 1, 0.0, pltpu.roll(u, seq - 1, 0))
        return prev * cw_ref[0:1, :] + u * cw_ref[1:2, :] + nxt * cw_ref[2:3, :] + cb_ref[...]

    gates = ((u1_ref, cw1_ref, cb1_ref), (u2_ref, cw2_ref, cb2_ref))
    z = short_conv(u0_ref, cw0_ref, cb0_ref)
    kb = min(seq, 512)
    for n in range(HY_ORDER):
        zb = z.astype(BF16)
        for r in range(0, seq, kb):
            xc = _dot(cs_ref[r:r + kb, :], zb)
            xs = _dot(cs_ref[seq + r:seq + r + kb, :], zb)
            hre = h_ref[n, r:r + kb, :]
            him = h_ref[n, seq + r:seq + r + kb, :]
            y_scr[r:r + kb, :] = (xc * hre + xs * him).astype(BF16)
            y_scr[seq + r:seq + r + kb, :] = (xs * hre - xc * him).astype(BF16)
        nyq = jnp.sum(alt * z, axis=0, keepdims=True) * (hl_ref[n] * (0.5 / seq))
        conv = _dot(cs_ref[:seq, :], y_scr[:seq, :]) + _dot(cs_ref[seq:, :], y_scr[seq:, :]) + alt * nyq
        z = short_conv(*gates[n]) * (conv + bias_ref[n] * z)
    z_ref[0] = z.astype(z_ref.dtype)


def _hy_conv(proj, conv_w, conv_b, cs, hspec, hl, bias, batch, seq, seg, col0, width, tc):
    proj3 = proj.reshape(batch, seq, proj.shape[-1])
    kern = functools.partial(_hyconv_kernel, seq=seq, seg=seg)
    blk0 = col0 // tc
    per = width // tc

    def u_spec(p):
        return pl.BlockSpec((1, seq, tc), lambda c, b: (b, 0, blk0 + p * per + c))

    def w_spec(p, rows):
        return pl.BlockSpec((rows, tc), lambda c, b: (0, p * per + c))

    return pl.pallas_call(
        kern,
        grid=(per, batch),
        in_specs=[u_spec(0), u_spec(1), u_spec(2),
                  w_spec(0, HY_SHORT), w_spec(1, HY_SHORT), w_spec(2, HY_SHORT),
                  w_spec(0, 1), w_spec(1, 1), w_spec(2, 1),
                  pl.BlockSpec((2 * seq, seq), lambda c, b: (0, 0), pipeline_mode=pl.Buffered(1)),
                  pl.BlockSpec((HY_ORDER, 2 * seq, tc), lambda c, b: (0, 0, c), pipeline_mode=pl.Buffered(1)),
                  pl.BlockSpec((HY_ORDER, 1, tc), lambda c, b: (0, 0, c)),
                  pl.BlockSpec((HY_ORDER, 1, tc), lambda c, b: (0, 0, c))],
        out_specs=pl.BlockSpec((1, seq, tc), lambda c, b: (b, 0, c)),
        out_shape=jax.ShapeDtypeStruct((batch, seq, width), BF16),
        scratch_shapes=[pltpu.VMEM((2 * seq, tc), BF16)],
        compiler_params=_cparams(("arbitrary", "arbitrary")),
        name="hy_conv",
    )(proj3, proj3, proj3, conv_w, conv_w, conv_w, conv_b, conv_b, conv_b, cs, hspec, hl, bias)


def _mixout_kernel(yg_ref, zh_ref, x_ref, mod_ref, ghy_ref, gpost_ref, gpre_ref, wo_ref,
                   wrh_ref, wrl_ref, br_ref, x1_ref, hg_ref, route_ref, cnt_ref, cnt_scr, *, gla_width):
    d = x_ref.shape[1]
    tm = x_ref.shape[0] // MIX_SUBTILES

    @pl.when(pl.program_id(0) == 0)
    def _():
        cnt_scr[...] = jnp.zeros_like(cnt_scr)

    for s in range(MIX_SUBTILES):
        _mixout_rows(slice(s * tm, (s + 1) * tm), tm, d, gla_width, yg_ref, zh_ref, x_ref, mod_ref, ghy_ref,
                     gpost_ref, gpre_ref, wo_ref, wrh_ref, wrl_ref, br_ref, x1_ref, hg_ref, route_ref, cnt_scr)
    cnt_ref[...] = cnt_scr[...]


def _mixout_rows(rows, tm, d, gla_width, yg_ref, zh_ref, x_ref, mod_ref, ghy_ref, gpost_ref, gpre_ref, wo_ref,
                 wrh_ref, wrl_ref, br_ref, x1_ref, hg_ref, route_ref, cnt_scr):
    yh = _rms(zh_ref[rows, :].astype(F32), ghy_ref[...]).astype(BF16)
    y = _dot(yg_ref[rows, :], wo_ref[:gla_width, :]) + _dot(yh, wo_ref[gla_width:, :])
    x1 = x_ref[rows, :] + mod_ref[0, 2:3, :] * _rms(y, gpost_ref[...])
    x1_ref[rows, :] = x1
    h2 = _rms(x1, gpre_ref[...]) * (1.0 + mod_ref[0, 4:5, :]) + mod_ref[0, 3:4, :]
    hh, hl = _split2(h2)
    hg_ref[rows, :d] = h2
    logits = _dot(hh, wrh_ref[...]) + (_dot(hh, wrl_ref[...]) + _dot(hl, wrh_ref[...])) + br_ref[...]

    lane = lax.broadcasted_iota(jnp.int32, logits.shape, 1).astype(F32)
    neg = -jnp.inf
    is_grp = (lane >= N_EXPERTS) & (lane < N_EXPERTS + N_GROUPS)
    m = jnp.max(jnp.where(is_grp, logits, neg), axis=1, keepdims=True)
    p_grp = 1.0 / jnp.sum(jnp.where(is_grp, jnp.exp(logits - m), 0.0), axis=1, keepdims=True)
    grp = jnp.min(jnp.where(is_grp & (logits == m), lane - N_EXPERTS, 1e9), axis=1, keepdims=True)
    sel = (lane >= grp * EXP_PER_GROUP) & (lane < (grp + 1.0) * EXP_PER_GROUP)
    me = jnp.max(jnp.where(sel, logits, neg), axis=1, keepdims=True)
    pe = jnp.where(sel, jnp.exp(logits - me), -1.0)
    v1 = jnp.max(pe, axis=1, keepdims=True)
    i1 = jnp.min(jnp.where(pe == v1, lane, 1e9), axis=1, keepdims=True)
    pe2 = jnp.where(lane == i1, -1.0, pe)
    v2 = jnp.max(pe2, axis=1, keepdims=True)
    i2 = jnp.min(jnp.where(pe2 == v2, lane, 1e9), axis=1, keepdims=True)
    den = v1 + v2
    gates = jnp.where(lane == i1, v1 / den, jnp.where(lane == i2, v2 / den, 0.0)) * p_grp

    lo = jnp.minimum(i1, i2) - grp * EXP_PER_GROUP
    hi = jnp.maximum(i1, i2) - grp * EXP_PER_GROUP
    bucket = grp * PAIRS_PER_GROUP + lo * (2 * EXP_PER_GROUP - 1 - lo) * 0.5 + (hi - lo - 1.0)
    onehot = lane == bucket
    r_i = lax.broadcasted_iota(jnp.int32, (tm, tm), 0)
    c_i = lax.broadcasted_iota(jnp.int32, (tm, tm), 1)
    earlier = _dot((r_i > c_i).astype(BF16), onehot.astype(BF16)) + cnt_scr[...]
    rank = jnp.sum(jnp.where(onehot, earlier, 0.0), axis=1, keepdims=True)
    cnt_scr[...] += jnp.sum(onehot.astype(F32), axis=0, keepdims=True)
    route = jnp.where(lane == ROUTE_BUCKET_LANE, bucket, jnp.where(lane == ROUTE_RANK_LANE, rank, gates))
    hg_ref[rows, d:] = route
    route_ref[:, rows] = route.T[ROUTE_BUCKET_LANE:ROUTE_BUCKET_LANE + 8, :]


def _mix_out(yg, zh, x, mod, mod_row, g_hy, g_post, g_pre, w_out, wr_hi, wr_lo, b_r, tm):
    n, d = x.shape
    gw = yg.shape[1]
    hw = zh.shape[1]
    kern = functools.partial(_mixout_kernel, gla_width=gw)
    row = lambda i: (i, 0)
    fixed = lambda i: (0, 0)
    return pl.pallas_call(
        kern,
        grid=(n // tm,),
        in_specs=[pl.BlockSpec((tm, gw), row),
                  pl.BlockSpec((tm, hw), row),
                  pl.BlockSpec((tm, d), row),
                  pl.BlockSpec((1, 6, d), lambda i: (mod_row(i * tm), 0, 0)),
                  pl.BlockSpec((1, hw), fixed),
                  pl.BlockSpec((1, d), fixed),
                  pl.BlockSpec((1, d), fixed),
                  pl.BlockSpec((gw + hw, d), fixed),
                  pl.BlockSpec((d, LANES), fixed),
                  pl.BlockSpec((d, LANES), fixed),
                  pl.BlockSpec((1, LANES), fixed)],
        out_specs=[pl.BlockSpec((tm, d), row), pl.BlockSpec((tm, d + LANES), row), pl.BlockSpec((8, tm), lambda i: (0, i)),
                   pl.BlockSpec((1, LANES), fixed)],
        out_shape=[jax.ShapeDtypeStruct((n, d), F32),
                   jax.ShapeDtypeStruct((n, d + LANES), F32),
                   jax.ShapeDtypeStruct((8, n), F32),
                   jax.ShapeDtypeStruct((1, LANES), F32)],
        scratch_shapes=[pltpu.VMEM((1, LANES), F32)],
        compiler_params=_cparams(("arbitrary",)),
        name="mix_out",
    )(yg, zh, x, mod, g_hy, g_post, g_pre, w_out, wr_hi, wr_lo, b_r)


def _row_gather_copy(src_hbm, row, buf, slot, r, sem):
    return pltpu.make_async_copy(src_hbm.at[pl.ds(row, 1), :], buf.at[slot, pl.ds(r, 1), :], sem.at[slot])


def _start_row_gather(idx_ref, base, src_hbm, buf, slot, sem, tm):
    def body(r, carry):
        _row_gather_copy(src_hbm, idx_ref[base + r], buf, slot, r, sem).start()
        return carry

    lax.fori_loop(0, tm, body, 0, unroll=8)


def _start_row_gather_inline(idx_ref, base, src_hbm, buf, slot, sem, tm):
    for r in range(tm):
        _row_gather_copy(src_hbm, idx_ref[base + r], buf, slot, r, sem).start()


def _wait_row_gather(src_hbm, buf, slot, sem, tm):
    pltpu.make_async_copy(src_hbm.at[pl.ds(0, tm), :], buf.at[slot], sem.at[slot]).wait()


def _moe_kernel(src_ref, ea_ref, eb_ref, valid_ref, hg_hbm,
                wga_ref, wua_ref, wda_ref, wgb_ref, wub_ref, wdb_ref, y_ref, buf, sem, *, tm, d):
    t = pl.program_id(0)
    slot = t % 2

    @pl.when(t == 0)
    def _():
        _start_row_gather(src_ref, 0, hg_hbm, buf, 0, sem, tm)

    @pl.when(valid_ref[t] == 1)
    def _():
        _wait_row_gather(hg_hbm, buf, slot, sem, tm)
        _start_row_gather_inline(src_ref, (t + 1) * tm, hg_hbm, buf, 1 - slot, sem, tm)
        h = buf[slot, :, :d].astype(BF16)
        gates = buf[slot, :, d:]
        lane = lax.broadcasted_iota(jnp.int32, gates.shape, 1)

        def expert(wg_ref, wu_ref, wd_ref, e):
            he = (_silu(_dot(h, wg_ref[0])) * _dot(h, wu_ref[0])).astype(BF16)
            gate = jnp.sum(jnp.where(lane == e, gates, 0.0), axis=1, keepdims=True)
            return gate * _dot(he, wd_ref[0])

        y_ref[...] = expert(wga_ref, wua_ref, wda_ref, ea_ref[t]) + expert(wgb_ref, wub_ref, wdb_ref, eb_ref[t])

    @pl.when(valid_ref[t] == 0)
    def _():
        @pl.when(valid_ref[jnp.maximum(t - 1, 0)] == 1)
        def _():
            _wait_row_gather(hg_hbm, buf, slot, sem, tm)

        y_ref[...] = jnp.zeros_like(y_ref)


def _moe(hg, src, ea, eb, valid, wg, wu, wd, n_tiles, tm):
    d = hg.shape[1] - LANES
    _, _, de = wg.shape
    kern = functools.partial(_moe_kernel, tm=tm, d=d)
    wa = lambda t, src, ea, eb, valid: (ea[t], 0, 0)
    wb = lambda t, src, ea, eb, valid: (eb[t], 0, 0)
    return pl.pallas_call(
        kern,
        grid_spec=pltpu.PrefetchScalarGridSpec(
            num_scalar_prefetch=4,
            grid=(n_tiles,),
            in_specs=[pl.BlockSpec(memory_space=pl.ANY),
                      pl.BlockSpec((1, d, de), wa), pl.BlockSpec((1, d, de), wa), pl.BlockSpec((1, de, d), wa),
                      pl.BlockSpec((1, d, de), wb), pl.BlockSpec((1, d, de), wb), pl.BlockSpec((1, de, d), wb)],
            out_specs=pl.BlockSpec((tm, d), lambda t, src, ea, eb, valid: (t, 0)),
            scratch_shapes=[pltpu.VMEM((2, tm, d + LANES), F32), pltpu.SemaphoreType.DMA((2,))]),
        out_shape=jax.ShapeDtypeStruct((n_tiles * tm, d), F32),
        compiler_params=_cparams(("arbitrary",)),
        name="moe",
    )(src, ea, eb, valid, hg, wg, wu, wd, wg, wu, wd)


def _ffn_out_kernel(pos_ref, y_hbm, x1_ref, mod_ref, gpost_ref, o_ref, buf, sem, *, tm):
    i = pl.program_id(0)
    n_i = pl.num_programs(0)
    slot = i % 2

    @pl.when(i == 0)
    def _():
        _start_row_gather(pos_ref, 0, y_hbm, buf, 0, sem, tm)

    _wait_row_gather(y_hbm, buf, slot, sem, tm)

    @pl.when(i + 1 < n_i)
    def _():
        _start_row_gather_inline(pos_ref, (i + 1) * tm, y_hbm, buf, 1 - slot, sem, tm)
        o_ref[...] = x1_ref[...] + mod_ref[0, 5:6, :] * _rms(buf[slot], gpost_ref[...])

    @pl.when(i + 1 == n_i)
    def _():
        o_ref[...] = x1_ref[...] + mod_ref[0, 5:6, :] * _rms(buf[slot], gpost_ref[...])


def _ffn_out(y_sorted, pos, x1, mod, mod_row, g_post, tm):
    n, d = x1.shape
    kern = functools.partial(_ffn_out_kernel, tm=tm)
    return pl.pallas_call(
        kern,
        grid_spec=pltpu.PrefetchScalarGridSpec(
            num_scalar_prefetch=1,
            grid=(n // tm,),
            in_specs=[pl.BlockSpec(memory_space=pl.ANY),
                      pl.BlockSpec((tm, d), lambda i, pos: (i, 0)),
                      pl.BlockSpec((1, 6, d), lambda i, pos: (mod_row(i * tm), 0, 0)),
                      pl.BlockSpec((1, d), lambda i, pos: (0, 0))],
            out_specs=pl.BlockSpec((tm, d), lambda i, pos: (i, 0)),
            scratch_shapes=[pltpu.VMEM((2, tm, d), F32), pltpu.SemaphoreType.DMA((2,))]),
        out_shape=jax.ShapeDtypeStruct((n, d), F32),
        compiler_params=_cparams(("arbitrary",)),
        name="ffn_out",
    )(pos, y_sorted, x1, mod, g_post)


def _route_tables(route, counts, n_tiles, tm):
    n = route.shape[1]
    n_buckets = N_GROUPS * PAIRS_PER_GROUP
    cnt = counts[0, :n_buckets].astype(jnp.int32)
    padded = (cnt + tm - 1) // tm * tm
    ends = jnp.cumsum(padded)
    starts = ends - padded
    bucket = route[0].astype(jnp.int32)
    rank = route[ROUTE_RANK_LANE - ROUTE_BUCKET_LANE].astype(jnp.int32)
    pos = starts[bucket] + rank
    src = (jnp.arange(n_tiles * tm, dtype=jnp.int32) % n).at[pos].set(jnp.arange(n, dtype=jnp.int32))
    n_valid = ends[-1] // tm
    tile = jnp.arange(n_tiles, dtype=jnp.int32)
    used = jnp.minimum(tile, n_valid - 1)
    tile_bucket = jnp.sum((ends[None, :] <= (used * tm)[:, None]).astype(jnp.int32), axis=1)
    grp = tile_bucket // PAIRS_PER_GROUP
    pair = tile_bucket % PAIRS_PER_GROUP
    pair_lo = jnp.array([a for a in range(EXP_PER_GROUP) for b in range(a + 1, EXP_PER_GROUP)], jnp.int32)
    pair_hi = jnp.array([b for a in range(EXP_PER_GROUP) for b in range(a + 1, EXP_PER_GROUP)], jnp.int32)
    ea = grp * EXP_PER_GROUP + pair_lo[pair]
    eb = grp * EXP_PER_GROUP + pair_hi[pair]
    valid = (tile < n_valid).astype(jnp.int32)
    return pos, src, ea, eb, valid


def _dft_kernel(cs_ref, *, seq, tr):
    r = pl.program_id(0)
    per_half = seq // tr
    is_sin = r >= per_half
    k = (r % per_half) * tr + lax.broadcasted_iota(jnp.int32, (tr, LANES), 0)
    j = lax.broadcasted_iota(jnp.int32, (tr, LANES), 1)
    period = 2 * seq

    def table(step):
        ang = ((k * j * step) % period).astype(F32) * (math.pi / seq)
        return jnp.cos(ang), jnp.sin(ang)

    c0, s0 = table(1)
    c1, s1 = table(LANES)
    p = jnp.where(is_sin, s0, c0)
    q = jnp.where(is_sin, c0, -s0)
    for t1 in range(seq // LANES):
        cs_ref[:, t1 * LANES:(t1 + 1) * LANES] = (c1[:, t1:t1 + 1] * p + s1[:, t1:t1 + 1] * q).astype(cs_ref.dtype)


def _dft_cos_sin(seq, tr):
    kern = functools.partial(_dft_kernel, seq=seq, tr=tr)
    return pl.pallas_call(
        kern,
        grid=(2 * seq // tr,),
        out_specs=pl.BlockSpec((tr, seq), lambda r: (r, 0)),
        out_shape=jax.ShapeDtypeStruct((2 * seq, seq), BF16),
        compiler_params=_cparams(("arbitrary",)),
        name="dft_matrix",
    )()


def _positional_features(seq):
    t = jnp.arange(seq, dtype=F32)
    t01 = t / max(seq - 1, 1)
    ang = 2.0 * math.pi * t / seq
    bands = jnp.linspace(1e-4, HY_BANDS - 1, HY_BANDS, dtype=F32)
    pe = jnp.concatenate([t01[:, None], jnp.cos(ang[:, None] * bands), -jnp.sin(ang[:, None] * bands)], axis=-1)
    return jnp.pad(pe, ((0, 0), (0, LANES - pe.shape[1])))


def _layer(x3, mod, mod_row, s0f, s0b, n_seg, p, tiles):
    batch, seq, d = x3.shape
    x = x3.reshape(batch * seq, d)
    dk, dv = p["dk"], p["dv"]
    hy_width = p["hy_width"]

    proj, lr = _premix_proj(x, mod, mod_row, p["g_pre_mix"], p["w_main"], p["wlr_hi"], p["wlr_lo"],
                            tiles["tm_proj"], tiles["tn_proj"])
    y_gla, s_f, s_b = _gla(proj, lr, p["wdf"], p["bdf"], p["wdb"], p["bdb"], p["g_gla"], s0f, s0b,
                           batch, seq, dk, dv)

    ad, hl = _hy_filter(_positional_features(seq), p["hy_w1"], p["hy_b1"], p["hy_w2"], p["hy_b2"], p["hy_freq"],
                        p["hy_w3"], p["hy_deltas"], seq, hy_width, tiles["tc_filter"])
    cs = _dft_cos_sin(seq, tiles["tr_spec"])
    hspec = _hy_spectrum(cs, ad, seq, hy_width, tiles["tr_spec"], tiles["tc_spec"])
    z_hy = _hy_conv(proj, p["hy_conv_w"], p["hy_conv_b"], cs, hspec, hl, p["hy_bias"], batch, seq, seq // n_seg,
                    p["hy_col0"], hy_width, tiles["tc_conv"])

    x1, hg, route, counts = _mix_out(y_gla.reshape(batch * seq, -1), z_hy.reshape(batch * seq, -1), x, mod, mod_row,
                                     p["g_hy"], p["g_post_mix"], p["g_pre_ffn"], p["w_out"], p["wr_hi"], p["wr_lo"],
                                     p["b_r"], tiles["tm_mix"])
    tm = tiles["tm_moe"]
    n_tiles = batch * seq // tm + N_GROUPS * PAIRS_PER_GROUP
    pos, src, ea, eb, valid = _route_tables(route, counts, n_tiles, tm)
    y_sorted = _moe(hg, src, ea, eb, valid, p["w_exp_gate"], p["w_exp_up"], p["w_exp_down"], n_tiles, tm)
    out = _ffn_out(y_sorted, pos, x1, mod, mod_row, p["g_post_ffn"], tiles["tm_out"])
    return out.reshape(batch, seq, d), s_f, s_b


def kernel(x_prompt, x_sample, c, state_gla_fwd, state_gla_bwd, c_ctx, w_ada, b_ada, g_pre_mix, g_post_mix, g_pre_ffn, g_post_ffn, w_in, w_dec_f, b_dec_f, w_dec_b, b_dec_b, g_gla, hy_conv_w, hy_conv_b, hy_w1, hy_b1, hy_w2, hy_b2, hy_w3, hy_freq, hy_bias, g_hy, w_out, w_router_grp, b_router_grp, w_router_exp, b_router_exp, w_exp_gate, w_exp_up, w_exp_down):
    depth = w_ada.shape[0]
    assert depth == 1
    l = 0
    d = x_prompt.shape[-1]
    dec_batch = x_sample.shape[0]
    heads = GLA_HEADS
    dk, dv = state_gla_fwd.shape[-2:]
    kdim = heads * dk
    gla_width = heads * dv
    hy_width = g_hy.shape[-1]
    hid = hy_w2.shape[-1]

    cond = jnp.concatenate([c_ctx[None, :], c], axis=0)
    cond = jnp.pad(cond, ((0, -cond.shape[0] % 8), (0, 0)))
    mod = _ada_mod(cond, w_ada[l], b_ada[l]).reshape(cond.shape[0], 6, d)

    n_main = 2 * kdim + 2 * gla_width
    w_main, wlr_hi, wlr_lo = _repack_w_in(w_in, l, n_main, 2 * GLA_RANK, 256)

    def dec_weight(w, first_row):
        wh = w.reshape(GLA_RANK, heads, dk).transpose(1, 0, 2)
        return jnp.pad(wh, ((0, 0), (first_row, LANES - GLA_RANK - first_row), (0, 0)))

    deltas = jnp.abs(jnp.linspace(math.log(HY_TARGET) / HY_SLOW_PCT, math.log(HY_TARGET) / HY_FAST_PCT, hy_width,
                                  dtype=F32))
    w_r = jnp.pad(jnp.concatenate([w_router_exp[l], w_router_grp[l]], axis=1),
                  ((0, 0), (0, LANES - N_EXPERTS - N_GROUPS)))
    wr_hi, wr_lo = _split2(w_r)
    b_r = jnp.pad(jnp.concatenate([b_router_exp[l], b_router_grp[l]]), (0, LANES - N_EXPERTS - N_GROUPS))

    p = dict(
        dk=dk, dv=dv, hy_width=hy_width, hy_col0=n_main,
        g_pre_mix=g_pre_mix[l][None, :], g_post_mix=g_post_mix[l][None, :],
        g_pre_ffn=g_pre_ffn[l][None, :], g_post_ffn=g_post_ffn[l][None, :],
        w_main=w_main, wlr_hi=wlr_hi, wlr_lo=wlr_lo,
        wdf=dec_weight(w_dec_f[l], 0), bdf=b_dec_f[l][None, :],
        wdb=dec_weight(w_dec_b[l], GLA_RANK), bdb=b_dec_b[l][None, :],
        g_gla=g_gla[l][None, :],
        hy_conv_w=hy_conv_w[l], hy_conv_b=hy_conv_b[l][None, :],
        hy_w1=jnp.pad(hy_w1[l], ((0, LANES - hy_w1.shape[1]), (0, 0))), hy_b1=hy_b1[l][None, :],
        hy_w2=hy_w2[l], hy_b2=hy_b2[l][None, :], hy_freq=hy_freq[l][None, :],
        hy_w3=hy_w3[l].reshape(hid, 2 * HY_ORDER, hy_width).transpose(1, 0, 2), hy_deltas=deltas[None, :],
        hy_bias=hy_bias[l][:, None, :], g_hy=g_hy[l][None, :],
        w_out=w_out[l].astype(BF16), wr_hi=wr_hi, wr_lo=wr_lo, b_r=b_r[None, :],
        w_exp_gate=w_exp_gate[l].astype(BF16), w_exp_up=w_exp_up[l].astype(BF16),
        w_exp_down=w_exp_down[l].astype(BF16),
    )

    zero_state = jnp.zeros((1, 1, dk, dv), F32)
    dec_seq = x_sample.shape[1]
    tiles_p = dict(tm_proj=512, tn_proj=2048, tc_filter=256, tr_spec=256, tc_spec=512, tc_conv=512,
                   tm_mix=512, tm_moe=256, tm_out=256)
    tiles_s = dict(tm_proj=512, tn_proj=2048, tc_filter=256, tr_spec=512, tc_spec=1024, tc_conv=256,
                   tm_mix=512, tm_moe=256, tm_out=256)

    y_p, s_f, s_b = _layer(x_prompt, mod, lambda r: 0, zero_state, zero_state, 1, p, tiles_p)
    y_s, _, _ = _layer(x_sample, mod, lambda r: 1 + r // dec_seq, state_gla_fwd[:, l], state_gla_bwd[:, l],
                       dec_seq // GRID_W, p, tiles_s)
    return (y_p, y_s, s_f[:, None].astype(x_prompt.dtype), s_b[:, None].astype(x_prompt.dtype))
```

```python
import functools
import math

import jax
import jax.numpy as jnp
from jax import lax
from jax.experimental import pallas as pl
from jax.experimental.pallas import tpu as pltpu

F32 = jnp.float32
BF16 = jnp.bfloat16

GRID_W = 64
GLA_HEADS = 4
GLA_RANK = 16
GLA_TAU = 16.0
GLA_CHUNK = 64
GLA_UNROLL = 4
MIX_SUBTILES = 2
HY_ORDER = 2
HY_SHORT = 3
HY_BANDS = 16
HY_TARGET = 1e-2
HY_FAST_PCT = 0.3
HY_SLOW_PCT = 1.5
N_GROUPS = 4
EXP_PER_GROUP = 4
N_EXPERTS = N_GROUPS * EXP_PER_GROUP
PAIRS_PER_GROUP = EXP_PER_GROUP * (EXP_PER_GROUP - 1) // 2
ROUTE_BUCKET_LANE = N_EXPERTS
ROUTE_RANK_LANE = N_EXPERTS + 1
EPS = 1e-6

LANES = 128
VMEM_LIMIT = 56 << 20


def _cparams(sem):
    return pltpu.CompilerParams(dimension_semantics=sem, vmem_limit_bytes=VMEM_LIMIT)


def _dot(a, b):
    return jnp.dot(a, b, preferred_element_type=F32)


def _dot_nt(a, b):
    return lax.dot_general(a, b, (((1,), (1,)), ((), ())), preferred_element_type=F32)


def _dot_tn(a, b):
    return lax.dot_general(a, b, (((0,), (0,)), ((), ())), preferred_element_type=F32)


def _split2(x):
    hi = x.astype(BF16)
    lo = (x - hi.astype(F32)).astype(BF16)
    return hi, lo


def _dot_hp(a, b):
    ah, al = _split2(a)
    bh, bl = _split2(b)
    return _dot(ah, bh) + (_dot(ah, bl) + _dot(al, bh))


def _dot_exact_lhs(t, x):
    hi, lo = _split2(x)
    return _dot(t, hi) + _dot(t, lo)


def _rms(x, g):
    return x * lax.rsqrt(jnp.mean(x * x, axis=-1, keepdims=True) + EPS) * g


def _silu(x):
    return x / (1.0 + jnp.exp(-x))


def _ada_kernel(c_ref, w_ref, b_ref, o_ref):
    s = _silu(c_ref[...]).astype(BF16)
    o_ref[...] = _dot(s, w_ref[...].astype(BF16)) + b_ref[...]


def _ada_mod(cond, w_ada, b_ada):
    rows, d = cond.shape
    n = w_ada.shape[1]
    tn = 1024
    return pl.pallas_call(
        _ada_kernel,
        grid=(n // tn,),
        in_specs=[pl.BlockSpec((rows, d), lambda j: (0, 0)),
                  pl.BlockSpec((d, tn), lambda j: (0, j)),
                  pl.BlockSpec((1, tn), lambda j: (0, j))],
        out_specs=pl.BlockSpec((rows, tn), lambda j: (0, j)),
        out_shape=jax.ShapeDtypeStruct((rows, n), F32),
        compiler_params=_cparams(("arbitrary",)),
        name="ada_mod",
    )(cond, w_ada, b_ada.reshape(1, n))


def _repack_kernel(wt_hbm, main_ref, lrh_ref, lrl_ref, buf, lr_buf, sem, lr_sem, *, layer, n_main, n_lr, tb):
    i = pl.program_id(0)
    n_i = pl.num_programs(0)
    slot = i % 2
    d = main_ref.shape[0]

    def block_copy(j, s):
        first = j * tb + jnp.where(j * tb >= n_main, n_lr, 0)
        return pltpu.make_async_copy(wt_hbm.at[layer, pl.ds(first, tb), :], buf.at[s], sem.at[s])

    lr_copy = pltpu.make_async_copy(wt_hbm.at[layer, pl.ds(n_main, n_lr), :], lr_buf, lr_sem)

    @pl.when(i == 0)
    def _():
        block_copy(0, 0).start()
        lr_copy.start()

    @pl.when(i + 1 < n_i)
    def _():
        block_copy(i + 1, 1 - slot).start()

    block_copy(i, slot).wait()
    main_ref[...] = buf[slot].T.astype(BF16)

    @pl.when(i == 0)
    def _():
        lr_copy.wait()
        lr = jnp.concatenate([lr_buf[...], jnp.zeros((LANES - n_lr, d), F32)], axis=0).T
        hi, lo = _split2(lr)
        lrh_ref[...] = hi
        lrl_ref[...] = lo


def _repack_w_in(w_in, layer, n_main, n_lr, tb):
    w_t = jnp.swapaxes(w_in, 1, 2)
    _, n_cols, d = w_t.shape
    n_out = n_cols - n_lr
    assert n_main % tb == 0 and n_out % tb == 0
    kern = functools.partial(_repack_kernel, layer=layer, n_main=n_main, n_lr=n_lr, tb=tb)
    return pl.pallas_call(
        kern,
        grid=(n_out // tb,),
        in_specs=[pl.BlockSpec(memory_space=pl.ANY)],
        out_specs=[pl.BlockSpec((d, tb), lambda i: (0, i)),
                   pl.BlockSpec((d, LANES), lambda i: (0, 0)),
                   pl.BlockSpec((d, LANES), lambda i: (0, 0))],
        out_shape=[jax.ShapeDtypeStruct((d, n_out), BF16),
                   jax.ShapeDtypeStruct((d, LANES), BF16),
                   jax.ShapeDtypeStruct((d, LANES), BF16)],
        scratch_shapes=[pltpu.VMEM((2, tb, d), F32), pltpu.VMEM((n_lr, d), F32),
                        pltpu.SemaphoreType.DMA((2,)), pltpu.SemaphoreType.DMA(())],
        compiler_params=_cparams(("arbitrary",)),
        name="repack_w_in",
    )(w_t)


def _premix_kernel(x_ref, mod_ref, g_ref, w_ref, wlh_ref, wll_ref, o_ref, lr_ref, h_scr):
    @pl.when(pl.program_id(1) == 0)
    def _():
        h = _rms(x_ref[...], g_ref[...]) * (1.0 + mod_ref[0, 1:2, :]) + mod_ref[0, 0:1, :]
        hh, hl = _split2(h)
        h_scr[...] = hh
        lr_ref[...] = _dot(hh, wlh_ref[...]) + (_dot(hh, wll_ref[...]) + _dot(hl, wlh_ref[...]))

    o_ref[...] = _dot(h_scr[...], w_ref[...]).astype(o_ref.dtype)


def _premix_proj(x, mod, mod_row, g, w_main, wlr_hi, wlr_lo, tm, tn):
    n, d = x.shape
    nc = w_main.shape[1]
    return pl.pallas_call(
        _premix_kernel,
        grid=(n // tm, nc // tn),
        in_specs=[pl.BlockSpec((tm, d), lambda i, j: (i, 0)),
                  pl.BlockSpec((1, 6, d), lambda i, j: (mod_row(i * tm), 0, 0)),
                  pl.BlockSpec((1, d), lambda i, j: (0, 0)),
                  pl.BlockSpec((d, tn), lambda i, j: (0, j)),
                  pl.BlockSpec((d, LANES), lambda i, j: (0, 0)),
                  pl.BlockSpec((d, LANES), lambda i, j: (0, 0))],
        out_specs=[pl.BlockSpec((tm, tn), lambda i, j: (i, j)),
                   pl.BlockSpec((tm, LANES), lambda i, j: (i, 0))],
        out_shape=[jax.ShapeDtypeStruct((n, nc), BF16),
                   jax.ShapeDtypeStruct((n, LANES), F32)],
        scratch_shapes=[pltpu.VMEM((tm, d), BF16)],
        compiler_params=_cparams(("arbitrary", "arbitrary")),
        name="premix_proj",
    )(x, mod, g, w_main, wlr_hi, wlr_lo)


def _log_sigmoid(x):
    return jnp.minimum(x, 0.0) - jnp.log1p(jnp.exp(-jnp.abs(x)))


def _gla_kernel(q_ref, k_ref, v_ref, g_ref, lr_ref, wdf_ref, bdf_ref, wdb_ref, bdb_ref, gg_ref,
                s0f_ref, s0b_ref, y_ref, sf_ref, sb_ref,
                laf_scr, lab_scr, qf_scr, qb_scr, o_scr, uf_scr, ub_scr, df_scr, db_scr, sinf_scr, sinb_scr,
                *, seq, dk):
    c = GLA_CHUNK
    n_chunks = seq // c
    scale = dk ** -0.5

    lr = lr_ref[0]
    laf_scr[...] = _log_sigmoid(_dot_hp(lr, wdf_ref[0]) + bdf_ref[...]) / GLA_TAU
    lab_scr[...] = _log_sigmoid(_dot_hp(lr, wdb_ref[0]) + bdb_ref[...]) / GLA_TAU

    per = GLA_UNROLL
    blk = per * c
    row = lax.broadcasted_iota(jnp.int32, (blk, blk), 0)
    col = lax.broadcasted_iota(jnp.int32, (blk, blk), 1)
    same = (row // c) == (col // c)
    lower = same & (row >= col)
    upper = same & (col >= row)
    t_fwd = lower.astype(BF16)
    t_bwd = upper.astype(BF16)

    def chunk_rows(x, r):
        return jnp.concatenate([jnp.broadcast_to(x[j * c + r:j * c + r + 1], (c, dk)) for j in range(per)], axis=0)

    def block_local(m, carry):
        sl = pl.ds(pl.multiple_of(m * blk, blk), blk)
        bf = _dot_exact_lhs(t_fwd, laf_scr[sl, :])
        bb = _dot_exact_lhs(t_bwd, lab_scr[sl, :])
        tot_f = chunk_rows(bf, c - 1)
        tot_b = chunk_rows(bb, 0)
        q = q_ref[0, sl, :].astype(F32) * scale
        k = k_ref[0, sl, :].astype(F32)
        v = v_ref[0, sl, :]
        qf = (q * jnp.exp(bf)).astype(BF16)
        kf = (k * jnp.exp(-bf)).astype(BF16)
        ksf = (k * jnp.exp(tot_f - bf)).astype(BF16)
        qb = (q * jnp.exp(bb)).astype(BF16)
        kb = (k * jnp.exp(-bb)).astype(BF16)
        ksb = (k * jnp.exp(tot_b - bb)).astype(BF16)
        att = jnp.where(lower, _dot_nt(qf, kf), 0.0) + jnp.where(upper, _dot_nt(qb, kb), 0.0)
        o_scr[sl, :] = _dot(att.astype(BF16), v)
        qf_scr[sl, :] = qf
        qb_scr[sl, :] = qb
        dec_f = jnp.exp(tot_f)
        dec_b = jnp.exp(tot_b)
        for j in range(per):
            n = m * per + j
            rows = slice(j * c, (j + 1) * c)
            uf_scr[n] = _dot_tn(v[rows], ksf[rows])
            ub_scr[n] = _dot_tn(v[rows], ksb[rows])
            df_scr[n] = dec_f[j * c:j * c + 8]
            db_scr[n] = dec_b[j * c:j * c + 8]
        return carry

    lax.fori_loop(0, n_chunks // per, block_local, 0)

    def scan_fwd(n, s):
        sinf_scr[n] = s.astype(BF16)
        return s * df_scr[n][0:1, :] + uf_scr[n]

    def scan_bwd(i, s):
        n = n_chunks - 1 - i
        sinb_scr[n] = s.astype(BF16)
        return s * db_scr[n][0:1, :] + ub_scr[n]

    s_f = lax.fori_loop(0, n_chunks, scan_fwd, s0f_ref[0, 0].T, unroll=GLA_UNROLL)
    s_b = lax.fori_loop(0, n_chunks, scan_bwd, s0b_ref[0, 0].T, unroll=GLA_UNROLL)
    sf_ref[0, 0] = s_f.T
    sb_ref[0, 0] = s_b.T

    def chunk_inter(n, carry):
        sl = pl.ds(pl.multiple_of(n * c, c), c)
        o_scr[sl, :] += _dot_nt(qf_scr[sl, :], sinf_scr[n]) + _dot_nt(qb_scr[sl, :], sinb_scr[n])
        return carry

    lax.fori_loop(0, n_chunks, chunk_inter, 0, unroll=GLA_UNROLL)

    o = _rms(o_scr[...], gg_ref[...])
    y_ref[0] = (o * _silu(g_ref[0].astype(F32))).astype(y_ref.dtype)


def _gla(proj, lr, wdf, bdf, wdb, bdb, g_gla, s0f, s0b, batch, seq, dk, dv):
    heads = GLA_HEADS
    proj3 = proj.reshape(batch, seq, proj.shape[-1])
    lr3 = lr.reshape(batch, seq, LANES)
    kdim = heads * dk
    width = heads * dv
    k_blk = kdim // dk
    v_blk = 2 * kdim // dv
    g_blk = (2 * kdim + width) // dv
    n_chunks = seq // GLA_CHUNK

    def s0_map(s0):
        if s0.shape[0] == batch:
            return lambda b, h: (b, h, 0, 0)
        return lambda b, h: (0, 0, 0, 0)

    kern = functools.partial(_gla_kernel, seq=seq, dk=dk)
    return pl.pallas_call(
        kern,
        grid=(batch, heads),
        in_specs=[pl.BlockSpec((1, seq, dk), lambda b, h: (b, 0, h)),
                  pl.BlockSpec((1, seq, dk), lambda b, h: (b, 0, k_blk + h)),
                  pl.BlockSpec((1, seq, dv), lambda b, h: (b, 0, v_blk + h)),
                  pl.BlockSpec((1, seq, dv), lambda b, h: (b, 0, g_blk + h)),
                  pl.BlockSpec((1, seq, LANES), lambda b, h: (b, 0, 0)),
                  pl.BlockSpec((1, LANES, dk), lambda b, h: (h, 0, 0)),
                  pl.BlockSpec((1, dk), lambda b, h: (0, h)),
                  pl.BlockSpec((1, LANES, dk), lambda b, h: (h, 0, 0)),
                  pl.BlockSpec((1, dk), lambda b, h: (0, h)),
                  pl.BlockSpec((1, dv), lambda b, h: (0, 0)),
                  pl.BlockSpec((1, 1, dk, dv), s0_map(s0f)),
                  pl.BlockSpec((1, 1, dk, dv), s0_map(s0b))],
        out_specs=[pl.BlockSpec((1, seq, dv), lambda b, h: (b, 0, h)),
                   pl.BlockSpec((1, 1, dk, dv), lambda b, h: (b, h, 0, 0)),
                   pl.BlockSpec((1, 1, dk, dv), lambda b, h: (b, h, 0, 0))],
        out_shape=[jax.ShapeDtypeStruct((batch, seq, width), BF16),
                   jax.ShapeDtypeStruct((batch, heads, dk, dv), F32),
                   jax.ShapeDtypeStruct((batch, heads, dk, dv), F32)],
        scratch_shapes=[pltpu.VMEM((seq, dk), F32), pltpu.VMEM((seq, dk), F32),
                        pltpu.VMEM((seq, dk), BF16), pltpu.VMEM((seq, dk), BF16),
                        pltpu.VMEM((seq, dv), F32),
                        pltpu.VMEM((n_chunks, dv, dk), F32), pltpu.VMEM((n_chunks, dv, dk), F32),
                        pltpu.VMEM((n_chunks, 8, dk), F32), pltpu.VMEM((n_chunks, 8, dk), F32),
                        pltpu.VMEM((n_chunks, dv, dk), BF16), pltpu.VMEM((n_chunks, dv, dk), BF16)],
        compiler_params=_cparams(("arbitrary", "arbitrary")),
        name="gla",
    )(proj3, proj3, proj3, proj3, lr3, wdf, bdf, wdb, bdb, g_gla, s0f, s0b)


def _filter_kernel(pe_ref, w1_ref, b1_ref, w2_ref, b2_ref, fr_ref, w3_ref, dl_ref, ad_ref, hl_ref, *, seq):
    pe = pe_ref[...]
    fr = fr_ref[...]
    h1 = jnp.sin(fr * (_dot_hp(pe, w1_ref[...]) + b1_ref[...]))
    h2 = jnp.sin(fr * (_dot_hp(h1, w2_ref[...]) + b2_ref[...]))
    dec = jnp.exp(-pe[:, 0:1] * dl_ref[...])
    row = lax.broadcasted_iota(jnp.int32, (seq, 1), 0)
    alt = jnp.where(row % 2 == 0, 1.0, -1.0)
    for o in range(HY_ORDER):
        ff = _dot_hp(h2, w3_ref[o]) * dec
        fb = _dot_hp(h2, w3_ref[HY_ORDER + o]) * dec
        nrm = jnp.sum(jnp.abs(ff), axis=0, keepdims=True) + jnp.sum(jnp.abs(fb), axis=0, keepdims=True)
        ff = ff / nrm
        fb = jnp.where(row == 0, 0.0, fb / nrm)
        a = ff + fb
        ad_ref[o, 0] = a.astype(ad_ref.dtype)
        ad_ref[o, 1] = (fb - ff).astype(ad_ref.dtype)
        hl_ref[o] = jnp.sum(alt * a, axis=0, keepdims=True)


def _hy_filter(pe, w1p, b1, w2, b2, freq, w3r, deltas, seq, width, tc):
    hid = w2.shape[0]
    kern = functools.partial(_filter_kernel, seq=seq)
    return pl.pallas_call(
        kern,
        grid=(width // tc,),
        in_specs=[pl.BlockSpec((seq, LANES), lambda c: (0, 0)),
                  pl.BlockSpec((LANES, hid), lambda c: (0, 0)),
                  pl.BlockSpec((1, hid), lambda c: (0, 0)),
                  pl.BlockSpec((hid, hid), lambda c: (0, 0)),
                  pl.BlockSpec((1, hid), lambda c: (0, 0)),
                  pl.BlockSpec((1, hid), lambda c: (0, 0)),
                  pl.BlockSpec((2 * HY_ORDER, hid, tc), lambda c: (0, 0, c)),
                  pl.BlockSpec((1, tc), lambda c: (0, c))],
        out_specs=[pl.BlockSpec((HY_ORDER, 2, seq, tc), lambda c: (0, 0, 0, c)),
                   pl.BlockSpec((HY_ORDER, 1, tc), lambda c: (0, 0, c))],
        out_shape=[jax.ShapeDtypeStruct((HY_ORDER, 2, seq, width), BF16),
                   jax.ShapeDtypeStruct((HY_ORDER, 1, width), F32)],
        compiler_params=_cparams(("arbitrary",)),
        name="hy_filter",
    )(pe, w1p, b1, w2, b2, freq, w3r, deltas)


def _spectrum_kernel(cs_ref, ad_ref, h_ref, *, seq, tr):
    r = pl.program_id(1)
    k = (r * tr) % seq + lax.broadcasted_iota(jnp.int32, (tr, 1), 0)
    wgt = jnp.where(k == 0, 1.0, 2.0) * (0.5 / seq)
    h_ref[0] = wgt * _dot(cs_ref[...], ad_ref[0, 0])


def _hy_spectrum(cs, ad, seq, width, tr, tc):
    kern = functools.partial(_spectrum_kernel, seq=seq, tr=tr)
    per_half = seq // tr
    return pl.pallas_call(
        kern,
        grid=(HY_ORDER, 2 * seq // tr, width // tc),
        in_specs=[pl.BlockSpec((tr, seq), lambda o, r, c: (r, 0)),
                  pl.BlockSpec((1, 1, seq, tc), lambda o, r, c: (o, r // per_half, 0, c))],
        out_specs=pl.BlockSpec((1, tr, tc), lambda o, r, c: (o, r, c)),
        out_shape=jax.ShapeDtypeStruct((HY_ORDER, 2 * seq, width), F32),
        compiler_params=_cparams(("arbitrary", "arbitrary", "arbitrary")),
        name="hy_spectrum",
    )(cs, ad)


def _hyconv_kernel(u0_ref, u1_ref, u2_ref, cw0_ref, cw1_ref, cw2_ref, cb0_ref, cb1_ref, cb2_ref,
                   cs_ref, h_ref, hl_ref, bias_ref, z_ref, y_scr, *, seq, seg):
    row = lax.broadcasted_iota(jnp.int32, (seq, 1), 0)
    pos = row % seg
    alt = jnp.where(row % 2 == 0, 1.0, -1.0)

    def short_conv(u_ref, cw_ref, cb_ref):
        u = u_ref[0].astype(F32)
        prev = jnp.where(pos == 0, 0.0, pltpu.roll(u, 1, 0))
        nxt = jnp.where(pos == seg - 1, 0.0, pltpu.roll(u, seq - 1, 0))
        return prev * cw_ref[0:1, :] + u * cw_ref[1:2, :] + nxt * cw_ref[2:3, :] + cb_ref[...]

    gates = ((u1_ref, cw1_ref, cb1_ref), (u2_ref, cw2_ref, cb2_ref))
    z = short_conv(u0_ref, cw0_ref, cb0_ref)
    kb = min(seq, 512)
    for n in range(HY_ORDER):
        zb = z.astype(BF16)
        for r in range(0, seq, kb):
            xc = _dot(cs_ref[r:r + kb, :], zb)
            xs = _dot(cs_ref[seq + r:seq + r + kb, :], zb)
            hre = h_ref[n, r:r + kb, :]
            him = h_ref[n, seq + r:seq + r + kb, :]
            y_scr[r:r + kb, :] = (xc * hre + xs * him).astype(BF16)
            y_scr[seq + r:seq + r + kb, :] = (xs * hre - xc * him).astype(BF16)
        nyq = jnp.sum(alt * z, axis=0, keepdims=True) * (hl_ref[n] * (0.5 / seq))
        conv = _dot(cs_ref[:seq, :], y_scr[:seq, :]) + _dot(cs_ref[seq:, :], y_scr[seq:, :]) + alt * nyq
        z = short_conv(*gates[n]) * (conv + bias_ref[n] * z)
    z_ref[0] = z.astype(z_ref.dtype)


def _hy_conv(proj, conv_w, conv_b, cs, hspec, hl, bias, batch, seq, seg, col0, width, tc):
    proj3 = proj.reshape(batch, seq, proj.shape[-1])
    kern = functools.partial(_hyconv_kernel, seq=seq, seg=seg)
    blk0 = col0 // tc
    per = width // tc

    def u_spec(p):
        return pl.BlockSpec((1, seq, tc), lambda c, b: (b, 0, blk0 + p * per + c))

    def w_spec(p, rows):
        return pl.BlockSpec((rows, tc), lambda c, b: (0, p * per + c))

    return pl.pallas_call(
        kern,
        grid=(per, batch),
        in_specs=[u_spec(0), u_spec(1), u_spec(2),
                  w_spec(0, HY_SHORT), w_spec(1, HY_SHORT), w_spec(2, HY_SHORT),
                  w_spec(0, 1), w_spec(1, 1), w_spec(2, 1),
                  pl.BlockSpec((2 * seq, seq), lambda c, b: (0, 0), pipeline_mode=pl.Buffered(1)),
                  pl.BlockSpec((HY_ORDER, 2 * seq, tc), lambda c, b: (0, 0, c), pipeline_mode=pl.Buffered(1)),
                  pl.BlockSpec((HY_ORDER, 1, tc), lambda c, b: (0, 0, c)),
                  pl.BlockSpec((HY_ORDER, 1, tc), lambda c, b: (0, 0, c))],
        out_specs=pl.BlockSpec((1, seq, tc), lambda c, b: (b, 0, c)),
        out_shape=jax.ShapeDtypeStruct((batch, seq, width), BF16),
        scratch_shapes=[pltpu.VMEM((2 * seq, tc), BF16)],
        compiler_params=_cparams(("arbitrary", "arbitrary")),
        name="hy_conv",
    )(proj3, proj3, proj3, conv_w, conv_w, conv_w, conv_b, conv_b, conv_b, cs, hspec, hl, bias)


def _mixout_kernel(*refs, gla_width, n_own, aliased):
    (yg_ref, zh_ref, x_ref, mod_ref, ghy_ref, gpost_ref, gpre_ref, wo_ref, wrh_ref, wrl_ref, br_ref,
     cnt_in_ref) = refs[:12]
    x1_ref, hg_ref, route_ref, cnt_ref, cnt_scr = refs[12 + int(aliased):]
    d = x_ref.shape[1]
    tm = x_ref.shape[0] // MIX_SUBTILES
    i = pl.program_id(0)

    @pl.when(i == 0)
    def _():
        cnt_scr[...] = cnt_in_ref[...]

    @pl.when(i < n_own)
    def _():
        for s in range(MIX_SUBTILES):
            _mixout_rows(slice(s * tm, (s + 1) * tm), tm, d, gla_width, yg_ref, zh_ref, x_ref, mod_ref, ghy_ref,
                         gpost_ref, gpre_ref, wo_ref, wrh_ref, wrl_ref, br_ref, x1_ref, hg_ref, route_ref, cnt_scr)

    @pl.when(i >= n_own)
    def _():
        hg_ref[...] = jnp.zeros_like(hg_ref)

    cnt_ref[...] = cnt_scr[...]


def _mixout_rows(rows, tm, d, gla_width, yg_ref, zh_ref, x_ref, mod_ref, ghy_ref, gpost_ref, gpre_ref, wo_ref,
                 wrh_ref, wrl_ref, br_ref, x1_ref, hg_ref, route_ref, cnt_scr):
    yh = _rms(zh_ref[rows, :].astype(F32), ghy_ref[...]).astype(BF16)
    y = _dot(yg_ref[rows, :], wo_ref[:gla_width, :]) + _dot(yh, wo_ref[gla_width:, :])
    x1 = x_ref[rows, :] + mod_ref[0, 2:3, :] * _rms(y, gpost_ref[...])
    x1_ref[rows, :] = x1
    h2 = _rms(x1, gpre_ref[...]) * (1.0 + mod_ref[0, 4:5, :]) + mod_ref[0, 3:4, :]
    hh, hl = _split2(h2)
    hg_ref[rows, :d] = h2
    logits = _dot(hh, wrh_ref[...]) + (_dot(hh, wrl_ref[...]) + _dot(hl, wrh_ref[...])) + br_ref[...]

    lane = lax.broadcasted_iota(jnp.int32, logits.shape, 1).astype(F32)
    neg = -jnp.inf
    is_grp = (lane >= N_EXPERTS) & (lane < N_EXPERTS + N_GROUPS)
    m = jnp.max(jnp.where(is_grp, logits, neg), axis=1, keepdims=True)
    p_grp = 1.0 / jnp.sum(jnp.where(is_grp, jnp.exp(logits - m), 0.0), axis=1, keepdims=True)
    grp = jnp.min(jnp.where(is_grp & (logits == m), lane - N_EXPERTS, 1e9), axis=1, keepdims=True)
    sel = (lane >= grp * EXP_PER_GROUP) & (lane < (grp + 1.0) * EXP_PER_GROUP)
    me = jnp.max(jnp.where(sel, logits, neg), axis=1, keepdims=True)
    pe = jnp.where(sel, jnp.exp(logits - me), -1.0)
    v1 = jnp.max(pe, axis=1, keepdims=True)
    i1 = jnp.min(jnp.where(pe == v1, lane, 1e9), axis=1, keepdims=True)
    pe2 = jnp.where(lane == i1, -1.0, pe)
    v2 = jnp.max(pe2, axis=1, keepdims=True)
    i2 = jnp.min(jnp.where(pe2 == v2, lane, 1e9), axis=1, keepdims=True)
    den = v1 + v2
    gates = jnp.where(lane == i1, v1 / den, jnp.where(lane == i2, v2 / den, 0.0)) * p_grp

    lo = jnp.minimum(i1, i2) - grp * EXP_PER_GROUP
    hi = jnp.maximum(i1, i2) - grp * EXP_PER_GROUP
    bucket = grp * PAIRS_PER_GROUP + lo * (2 * EXP_PER_GROUP - 1 - lo) * 0.5 + (hi - lo - 1.0)
    onehot = lane == bucket
    r_i = lax.broadcasted_iota(jnp.int32, (tm, tm), 0)
    c_i = lax.broadcasted_iota(jnp.int32, (tm, tm), 1)
    earlier = _dot((r_i > c_i).astype(BF16), onehot.astype(BF16)) + cnt_scr[...]
    rank = jnp.sum(jnp.where(onehot, earlier, 0.0), axis=1, keepdims=True)
    cnt_scr[...] += jnp.sum(onehot.astype(F32), axis=0, keepdims=True)
    route = jnp.where(lane == ROUTE_BUCKET_LANE, bucket, jnp.where(lane == ROUTE_RANK_LANE, rank, gates))
    hg_ref[rows, d:] = route
    route_ref[:, rows] = route.T[ROUTE_BUCKET_LANE:ROUTE_BUCKET_LANE + 8, :]


def _mix_out(yg, zh, x, mod, mod_row, g_hy, g_post, g_pre, w_out, wr_hi, wr_lo, b_r, cnt_in, hg_all, row0, n_total, tm):
    n, d = x.shape
    gw = yg.shape[1]
    hw = zh.shape[1]
    aliased = hg_all is not None
    n_own = n // tm
    n_steps = n_own if aliased else n_total // tm
    assert row0 % tm == 0 and n_total % tm == 0 and (row0 + n == n_total if aliased else row0 == 0)
    kern = functools.partial(_mixout_kernel, gla_width=gw, n_own=n_own, aliased=aliased)
    row = lambda i: (jnp.minimum(i, n_own - 1), 0)
    fixed = lambda i: (0, 0)
    in_specs = [pl.BlockSpec((tm, gw), row),
                pl.BlockSpec((tm, hw), row),
                pl.BlockSpec((tm, d), row),
                pl.BlockSpec((1, 6, d), lambda i: (mod_row(jnp.minimum(i, n_own - 1) * tm), 0, 0)),
                pl.BlockSpec((1, hw), fixed),
                pl.BlockSpec((1, d), fixed),
                pl.BlockSpec((1, d), fixed),
                pl.BlockSpec((gw + hw, d), fixed),
                pl.BlockSpec((d, LANES), fixed),
                pl.BlockSpec((d, LANES), fixed),
                pl.BlockSpec((1, LANES), fixed),
                pl.BlockSpec((1, LANES), fixed)]
    operands = [yg, zh, x, mod, g_hy, g_post, g_pre, w_out, wr_hi, wr_lo, b_r, cnt_in]
    if aliased:
        in_specs.append(pl.BlockSpec(memory_space=pl.ANY))
        operands.append(hg_all)
    return pl.pallas_call(
        kern,
        grid=(n_steps,),
        in_specs=in_specs,
        out_specs=[pl.BlockSpec((tm, d), row),
                   pl.BlockSpec((tm, d + LANES), lambda i: (row0 // tm + i, 0)),
                   pl.BlockSpec((8, tm), lambda i: (0, jnp.minimum(i, n_own - 1))),
                   pl.BlockSpec((1, LANES), fixed)],
        out_shape=[jax.ShapeDtypeStruct((n, d), F32),
                   jax.ShapeDtypeStruct((n_total, d + LANES), F32),
                   jax.ShapeDtypeStruct((8, n), F32),
                   jax.ShapeDtypeStruct((1, LANES), F32)],
        scratch_shapes=[pltpu.VMEM((1, LANES), F32)],
        input_output_aliases={len(operands) - 1: 1} if aliased else {},
        compiler_params=_cparams(("arbitrary",)),
        name="mix_out",
    )(*operands)


def _row_gather_copy(src_hbm, row, buf, slot, r, sem):
    return pltpu.make_async_copy(src_hbm.at[pl.ds(row, 1), :], buf.at[slot, pl.ds(r, 1), :], sem.at[slot])


def _start_row_gather(idx_ref, base, src_hbm, buf, slot, sem, tm):
    def body(r, carry):
        _row_gather_copy(src_hbm, idx_ref[base + r], buf, slot, r, sem).start()
        return carry

    lax.fori_loop(0, tm, body, 0, unroll=8)


def _start_row_gather_inline(idx_ref, base, src_hbm, buf, slot, sem, tm):
    for r in range(tm):
        _row_gather_copy(src_hbm, idx_ref[base + r], buf, slot, r, sem).start()


def _wait_row_gather(src_hbm, buf, slot, sem, tm):
    pltpu.make_async_copy(src_hbm.at[pl.ds(0, tm), :], buf.at[slot], sem.at[slot]).wait()


def _moe_kernel(src_ref, ea_ref, eb_ref, valid_ref, hg_hbm,
                wga_ref, wua_ref, wda_ref, wgb_ref, wub_ref, wdb_ref, y_ref, buf, sem, *, tm, d):
    t = pl.program_id(0)
    slot = t % 2

    @pl.when(t == 0)
    def _():
        _start_row_gather(src_ref, 0, hg_hbm, buf, 0, sem, tm)

    @pl.when(valid_ref[t] == 1)
    def _():
        _wait_row_gather(hg_hbm, buf, slot, sem, tm)
        _start_row_gather_inline(src_ref, (t + 1) * tm, hg_hbm, buf, 1 - slot, sem, tm)
        h = buf[slot, :, :d].astype(BF16)
        gates = buf[slot, :, d:]
        lane = lax.broadcasted_iota(jnp.int32, gates.shape, 1)

        def expert(wg_ref, wu_ref, wd_ref, e):
            he = (_silu(_dot(h, wg_ref[0])) * _dot(h, wu_ref[0])).astype(BF16)
            gate = jnp.sum(jnp.where(lane == e, gates, 0.0), axis=1, keepdims=True)
            return gate * _dot(he, wd_ref[0])

        y_ref[...] = expert(wga_ref, wua_ref, wda_ref, ea_ref[t]) + expert(wgb_ref, wub_ref, wdb_ref, eb_ref[t])

    @pl.when(valid_ref[t] == 0)
    def _():
        @pl.when(valid_ref[jnp.maximum(t - 1, 0)] == 1)
        def _():
            _wait_row_gather(hg_hbm, buf, slot, sem, tm)

        y_ref[...] = jnp.zeros_like(y_ref)


def _moe(hg, src, ea, eb, valid, wg, wu, wd, n_tiles, tm):
    d = hg.shape[1] - LANES
    _, _, de = wg.shape
    kern = functools.partial(_moe_kernel, tm=tm, d=d)
    wa = lambda t, src, ea, eb, valid: (ea[t], 0, 0)
    wb = lambda t, src, ea, eb, valid: (eb[t], 0, 0)
    return pl.pallas_call(
        kern,
        grid_spec=pltpu.PrefetchScalarGridSpec(
            num_scalar_prefetch=4,
            grid=(n_tiles,),
            in_specs=[pl.BlockSpec(memory_space=pl.ANY),
                      pl.BlockSpec((1, d, de), wa), pl.BlockSpec((1, d, de), wa), pl.BlockSpec((1, de, d), wa),
                      pl.BlockSpec((1, d, de), wb), pl.BlockSpec((1, d, de), wb), pl.BlockSpec((1, de, d), wb)],
            out_specs=pl.BlockSpec((tm, d), lambda t, src, ea, eb, valid: (t, 0)),
            scratch_shapes=[pltpu.VMEM((2, tm, d + LANES), F32), pltpu.SemaphoreType.DMA((2,))]),
        out_shape=jax.ShapeDtypeStruct((n_tiles * tm, d), F32),
        compiler_params=_cparams(("arbitrary",)),
        name="moe",
    )(src, ea, eb, valid, hg, wg, wu, wd, wg, wu, wd)


def _ffn_out_kernel(pos_ref, y_hbm, x1_ref, mod_ref, gpost_ref, o_ref, buf, sem, *, tm):
    i = pl.program_id(0)
    n_i = pl.num_programs(0)
    slot = i % 2

    @pl.when(i == 0)
    def _():
        _start_row_gather(pos_ref, 0, y_hbm, buf, 0, sem, tm)

    _wait_row_gather(y_hbm, buf, slot, sem, tm)

    @pl.when(i + 1 < n_i)
    def _():
        _start_row_gather_inline(pos_ref, (i + 1) * tm, y_hbm, buf, 1 - slot, sem, tm)
        o_ref[...] = x1_ref[...] + mod_ref[0, 5:6, :] * _rms(buf[slot], gpost_ref[...])

    @pl.when(i + 1 == n_i)
    def _():
        o_ref[...] = x1_ref[...] + mod_ref[0, 5:6, :] * _rms(buf[slot], gpost_ref[...])


def _ffn_out(y_sorted, pos, x1, mod, mod_row, g_post, tm):
    n, d = x1.shape
    kern = functools.partial(_ffn_out_kernel, tm=tm)
    return pl.pallas_call(
        kern,
        grid_spec=pltpu.PrefetchScalarGridSpec(
            num_scalar_prefetch=1,
            grid=(n // tm,),
            in_specs=[pl.BlockSpec(memory_space=pl.ANY),
                      pl.BlockSpec((tm, d), lambda i, pos: (i, 0)),
                      pl.BlockSpec((1, 6, d), lambda i, pos: (mod_row(i * tm), 0, 0)),
                      pl.BlockSpec((1, d), lambda i, pos: (0, 0))],
            out_specs=pl.BlockSpec((tm, d), lambda i, pos: (i, 0)),
            scratch_shapes=[pltpu.VMEM((2, tm, d), F32), pltpu.SemaphoreType.DMA((2,))]),
        out_shape=jax.ShapeDtypeStruct((n, d), F32),
        compiler_params=_cparams(("arbitrary",)),
        name="ffn_out",
    )(pos, y_sorted, x1, mod, g_post)


def _route_tables(route, counts, n_tiles, tm):
    n = route.shape[1]
    n_buckets = N_GROUPS * PAIRS_PER_GROUP
    cnt = counts[0, :n_buckets].astype(jnp.int32)
    padded = (cnt + tm - 1) // tm * tm
    ends = jnp.cumsum(padded)
    starts = ends - padded
    bucket = route[0].astype(jnp.int32)
    rank = route[ROUTE_RANK_LANE - ROUTE_BUCKET_LANE].astype(jnp.int32)
    pos = starts[bucket] + rank
    src = (jnp.arange(n_tiles * tm, dtype=jnp.int32) % n).at[pos].set(jnp.arange(n, dtype=jnp.int32))
    n_valid = ends[-1] // tm
    tile = jnp.arange(n_tiles, dtype=jnp.int32)
    used = jnp.minimum(tile, n_valid - 1)
    tile_bucket = jnp.sum((ends[None, :] <= (used * tm)[:, None]).astype(jnp.int32), axis=1)
    grp = tile_bucket // PAIRS_PER_GROUP
    pair = tile_bucket % PAIRS_PER_GROUP
    pair_lo = jnp.array([a for a in range(EXP_PER_GROUP) for b in range(a + 1, EXP_PER_GROUP)], jnp.int32)
    pair_hi = jnp.array([b for a in range(EXP_PER_GROUP) for b in range(a + 1, EXP_PER_GROUP)], jnp.int32)
    ea = grp * EXP_PER_GROUP + pair_lo[pair]
    eb = grp * EXP_PER_GROUP + pair_hi[pair]
    valid = (tile < n_valid).astype(jnp.int32)
    return pos, src, ea, eb, valid


def _dft_kernel(cs_ref, *, seq, tr):
    k = pl.program_id(0) * tr + lax.broadcasted_iota(jnp.int32, (tr, LANES), 0)
    j = lax.broadcasted_iota(jnp.int32, (tr, LANES), 1)
    period = 2 * seq

    def table(step):
        ang = ((k * j * step) % period).astype(F32) * (math.pi / seq)
        return jnp.cos(ang), jnp.sin(ang)

    c0, s0 = table(1)
    c1, s1 = table(LANES)
    for t1 in range(seq // LANES):
        cols = slice(t1 * LANES, (t1 + 1) * LANES)
        ca = c1[:, t1:t1 + 1]
        sa = s1[:, t1:t1 + 1]
        cs_ref[0, :, cols] = (ca * c0 - sa * s0).astype(cs_ref.dtype)
        cs_ref[1, :, cols] = (sa * c0 + ca * s0).astype(cs_ref.dtype)


def _dft_cos_sin(seq, tr):
    kern = functools.partial(_dft_kernel, seq=seq, tr=tr)
    cs = pl.pallas_call(
        kern,
        grid=(seq // tr,),
        out_specs=pl.BlockSpec((2, tr, seq), lambda r: (0, r, 0)),
        out_shape=jax.ShapeDtypeStruct((2, seq, seq), BF16),
        compiler_params=_cparams(("arbitrary",)),
        name="dft_matrix",
    )()
    return cs.reshape(2 * seq, seq)


def _positional_features(seq):
    t = jnp.arange(seq, dtype=F32)
    t01 = t / max(seq - 1, 1)
    ang = 2.0 * math.pi * t / seq
    bands = jnp.linspace(1e-4, HY_BANDS - 1, HY_BANDS, dtype=F32)
    pe = jnp.concatenate([t01[:, None], jnp.cos(ang[:, None] * bands), -jnp.sin(ang[:, None] * bands)], axis=-1)
    return jnp.pad(pe, ((0, 0), (0, LANES - pe.shape[1])))


def _mixer_and_router(x3, mod, mod_row, s0f, s0b, n_seg, p, tiles, cnt_in, hg_all, row0, n_total):
    batch, seq, d = x3.shape
    x = x3.reshape(batch * seq, d)
    dk, dv = p["dk"], p["dv"]
    hy_width = p["hy_width"]

    proj, lr = _premix_proj(x, mod, mod_row, p["g_pre_mix"], p["w_main"], p["wlr_hi"], p["wlr_lo"],
                            tiles["tm_proj"], tiles["tn_proj"])
    y_gla, s_f, s_b = _gla(proj, lr, p["wdf"], p["bdf"], p["wdb"], p["bdb"], p["g_gla"], s0f, s0b,
                           batch, seq, dk, dv)

    ad, hl = _hy_filter(_positional_features(seq), p["hy_w1"], p["hy_b1"], p["hy_w2"], p["hy_b2"], p["hy_freq"],
                        p["hy_w3"], p["hy_deltas"], seq, hy_width, tiles["tc_filter"])
    cs = _dft_cos_sin(seq, tiles["tr_spec"])
    hspec = _hy_spectrum(cs, ad, seq, hy_width, tiles["tr_spec"], tiles["tc_spec"])
    z_hy = _hy_conv(proj, p["hy_conv_w"], p["hy_conv_b"], cs, hspec, hl, p["hy_bias"], batch, seq, seq // n_seg,
                    p["hy_col0"], hy_width, tiles["tc_conv"])

    x1, hg_all, route, counts = _mix_out(y_gla.reshape(batch * seq, -1), z_hy.reshape(batch * seq, -1), x, mod,
                                         mod_row, p["g_hy"], p["g_post_mix"], p["g_pre_ffn"], p["w_out"], p["wr_hi"],
                                         p["wr_lo"], p["b_r"], cnt_in, hg_all, row0, n_total, tiles["tm_mix"])
    return x1, hg_all, route, counts, s_f, s_b


def kernel(x_prompt, x_sample, c, state_gla_fwd, state_gla_bwd, c_ctx, w_ada, b_ada, g_pre_mix, g_post_mix, g_pre_ffn, g_post_ffn, w_in, w_dec_f, b_dec_f, w_dec_b, b_dec_b, g_gla, hy_conv_w, hy_conv_b, hy_w1, hy_b1, hy_w2, hy_b2, hy_w3, hy_freq, hy_bias, g_hy, w_out, w_router_grp, b_router_grp, w_router_exp, b_router_exp, w_exp_gate, w_exp_up, w_exp_down):
    depth = w_ada.shape[0]
    assert depth == 1
    l = 0
    d = x_prompt.shape[-1]
    dec_batch = x_sample.shape[0]
    heads = GLA_HEADS
    dk, dv = state_gla_fwd.shape[-2:]
    kdim = heads * dk
    gla_width = heads * dv
    hy_width = g_hy.shape[-1]
    hid = hy_w2.shape[-1]

    cond = jnp.concatenate([c_ctx[None, :], c], axis=0)
    cond = jnp.pad(cond, ((0, -cond.shape[0] % 8), (0, 0)))
    mod = _ada_mod(cond, w_ada[l], b_ada[l]).reshape(cond.shape[0], 6, d)

    n_main = 2 * kdim + 2 * gla_width
    w_main, wlr_hi, wlr_lo = _repack_w_in(w_in, l, n_main, 2 * GLA_RANK, 256)

    def dec_weight(w, first_row):
        wh = w.reshape(GLA_RANK, heads, dk).transpose(1, 0, 2)
        return jnp.pad(wh, ((0, 0), (first_row, LANES - GLA_RANK - first_row), (0, 0)))

    deltas = jnp.abs(jnp.linspace(math.log(HY_TARGET) / HY_SLOW_PCT, math.log(HY_TARGET) / HY_FAST_PCT, hy_width,
                                  dtype=F32))
    w_r = jnp.pad(jnp.concatenate([w_router_exp[l], w_router_grp[l]], axis=1),
                  ((0, 0), (0, LANES - N_EXPERTS - N_GROUPS)))
    wr_hi, wr_lo = _split2(w_r)
    b_r = jnp.pad(jnp.concatenate([b_router_exp[l], b_router_grp[l]]), (0, LANES - N_EXPERTS - N_GROUPS))

    p = dict(
        dk=dk, dv=dv, hy_width=hy_width, hy_col0=n_main,
        g_pre_mix=g_pre_mix[l][None, :], g_post_mix=g_post_mix[l][None, :],
        g_pre_ffn=g_pre_ffn[l][None, :], g_post_ffn=g_post_ffn[l][None, :],
        w_main=w_main, wlr_hi=wlr_hi, wlr_lo=wlr_lo,
        wdf=dec_weight(w_dec_f[l], 0), bdf=b_dec_f[l][None, :],
        wdb=dec_weight(w_dec_b[l], GLA_RANK), bdb=b_dec_b[l][None, :],
        g_gla=g_gla[l][None, :],
        hy_conv_w=hy_conv_w[l], hy_conv_b=hy_conv_b[l][None, :],
        hy_w1=jnp.pad(hy_w1[l], ((0, LANES - hy_w1.shape[1]), (0, 0))), hy_b1=hy_b1[l][None, :],
        hy_w2=hy_w2[l], hy_b2=hy_b2[l][None, :], hy_freq=hy_freq[l][None, :],
        hy_w3=hy_w3[l].reshape(hid, 2 * HY_ORDER, hy_width).transpose(1, 0, 2), hy_deltas=deltas[None, :],
        hy_bias=hy_bias[l][:, None, :], g_hy=g_hy[l][None, :],
        w_out=w_out[l].astype(BF16), wr_hi=wr_hi, wr_lo=wr_lo, b_r=b_r[None, :],
        w_exp_gate=w_exp_gate[l].astype(BF16), w_exp_up=w_exp_up[l].astype(BF16),
        w_exp_down=w_exp_down[l].astype(BF16),
    )

    zero_state = jnp.zeros((1, 1, dk, dv), F32)
    dec_seq = x_sample.shape[1]
    tiles_p = dict(tm_proj=512, tn_proj=2048, tc_filter=256, tr_spec=256, tc_spec=512, tc_conv=512, tm_mix=512)
    tiles_s = dict(tm_proj=512, tn_proj=2048, tc_filter=256, tr_spec=512, tc_spec=1024, tc_conv=256, tm_mix=512)
    tm_moe = 256
    tm_out = 256

    n_p = x_prompt.shape[0] * x_prompt.shape[1]
    n_s = dec_batch * dec_seq
    n_all = n_p + n_s
    mod_row_p = lambda r: 0
    mod_row_s = lambda r: 1 + r // dec_seq
    x1_p, hg_all, route_p, counts, s_f, s_b = _mixer_and_router(
        x_prompt, mod, mod_row_p, zero_state, zero_state, 1, p, tiles_p, jnp.zeros((1, LANES), F32), None, 0, n_all)
    x1_s, hg_all, route_s, counts, _, _ = _mixer_and_router(
        x_sample, mod, mod_row_s, state_gla_fwd[:, l], state_gla_bwd[:, l], dec_seq // GRID_W, p, tiles_s,
        counts, hg_all, n_p, n_all)

    n_tiles = n_all // tm_moe + N_GROUPS * PAIRS_PER_GROUP
    pos, src, ea, eb, valid = _route_tables(jnp.concatenate([route_p, route_s], axis=1), counts, n_tiles, tm_moe)
    y_sorted = _moe(hg_all, src, ea, eb, valid, p["w_exp_gate"], p["w_exp_up"], p["w_exp_down"], n_tiles, tm_moe)
    y_p = _ffn_out(y_sorted, pos[:n_p], x1_p, mod, mod_row_p, p["g_post_ffn"], tm_out)
    y_s = _ffn_out(y_sorted, pos[n_p:], x1_s, mod, mod_row_s, p["g_post_ffn"], tm_out)
    return (y_p.reshape(x_prompt.shape), y_s.reshape(x_sample.shape),
            s_f[:, None].astype(x_prompt.dtype), s_b[:, None].astype(x_prompt.dtype))
```

```python
import functools
import math

import jax
import jax.numpy as jnp
from jax import lax
from jax.experimental import pallas as pl
from jax.experimental.pallas import tpu as pltpu

F32 = jnp.float32
BF16 = jnp.bfloat16

GRID_W = 64
GLA_HEADS = 4
GLA_RANK = 16
GLA_TAU = 16.0
GLA_CHUNK = 64
GLA_UNROLL = 4
MIX_SUBTILES = 2
HY_ORDER = 2
HY_SHORT = 3
HY_BANDS = 16
HY_TARGET = 1e-2
HY_FAST_PCT = 0.3
HY_SLOW_PCT = 1.5
N_GROUPS = 4
EXP_PER_GROUP = 4
N_EXPERTS = N_GROUPS * EXP_PER_GROUP
PAIRS_PER_GROUP = EXP_PER_GROUP * (EXP_PER_GROUP - 1) // 2
ROUTE_BUCKET_LANE = N_EXPERTS
ROUTE_RANK_LANE = N_EXPERTS + 1
EPS = 1e-6

LANES = 128
VMEM_LIMIT = 56 << 20


def _cparams(sem):
    return pltpu.CompilerParams(dimension_semantics=sem, vmem_limit_bytes=VMEM_LIMIT)


def _dot(a, b):
    return jnp.dot(a, b, preferred_element_type=F32)


def _dot_nt(a, b):
    return lax.dot_general(a, b, (((1,), (1,)), ((), ())), preferred_element_type=F32)


def _dot_tn(a, b):
    return lax.dot_general(a, b, (((0,), (0,)), ((), ())), preferred_element_type=F32)


def _split2(x):
    hi = x.astype(BF16)
    lo = (x - hi.astype(F32)).astype(BF16)
    return hi, lo


def _dot_hp(a, b):
    ah, al = _split2(a)
    bh, bl = _split2(b)
    return _dot(ah, bh) + (_dot(ah, bl) + _dot(al, bh))


def _dot_exact_lhs(t, x):
    hi, lo = _split2(x)
    return _dot(t, hi) + _dot(t, lo)


def _rms(x, g):
    return x * lax.rsqrt(jnp.mean(x * x, axis=-1, keepdims=True) + EPS) * g


def _silu(x):
    return x / (1.0 + jnp.exp(-x))


def _ada_kernel(c_ref, w_ref, b_ref, o_ref):
    s = _silu(c_ref[...]).astype(BF16)
    o_ref[...] = _dot(s, w_ref[...].astype(BF16)) + b_ref[...]


def _ada_mod(cond, w_ada, b_ada):
    rows, d = cond.shape
    n = w_ada.shape[1]
    tn = 1024
    return pl.pallas_call(
        _ada_kernel,
        grid=(n // tn,),
        in_specs=[pl.BlockSpec((rows, d), lambda j: (0, 0)),
                  pl.BlockSpec((d, tn), lambda j: (0, j)),
                  pl.BlockSpec((1, tn), lambda j: (0, j))],
        out_specs=pl.BlockSpec((rows, tn), lambda j: (0, j)),
        out_shape=jax.ShapeDtypeStruct((rows, n), F32),
        compiler_params=_cparams(("arbitrary",)),
        name="ada_mod",
    )(cond, w_ada, b_ada.reshape(1, n))


def _repack_kernel(wt_hbm, main_ref, lrh_ref, lrl_ref, buf, lr_buf, sem, lr_sem, *, layer, n_main, n_lr, tb):
    i = pl.program_id(0)
    n_i = pl.num_programs(0)
    slot = i % 2
    d = main_ref.shape[0]

    def block_copy(j, s):
        first = j * tb + jnp.where(j * tb >= n_main, n_lr, 0)
        return pltpu.make_async_copy(wt_hbm.at[layer, pl.ds(first, tb), :], buf.at[s], sem.at[s])

    lr_copy = pltpu.make_async_copy(wt_hbm.at[layer, pl.ds(n_main, n_lr), :], lr_buf, lr_sem)

    @pl.when(i == 0)
    def _():
        block_copy(0, 0).start()
        lr_copy.start()

    @pl.when(i + 1 < n_i)
    def _():
        block_copy(i + 1, 1 - slot).start()

    block_copy(i, slot).wait()
    main_ref[...] = buf[slot].T.astype(BF16)

    @pl.when(i == 0)
    def _():
        lr_copy.wait()
        lr = jnp.concatenate([lr_buf[...], jnp.zeros((LANES - n_lr, d), F32)], axis=0).T
        hi, lo = _split2(lr)
        lrh_ref[...] = hi
        lrl_ref[...] = lo


def _repack_w_in(w_in, layer, n_main, n_lr, tb):
    w_t = jnp.swapaxes(w_in, 1, 2)
    _, n_cols, d = w_t.shape
    n_out = n_cols - n_lr
    assert n_main % tb == 0 and n_out % tb == 0
    kern = functools.partial(_repack_kernel, layer=layer, n_main=n_main, n_lr=n_lr, tb=tb)
    return pl.pallas_call(
        kern,
        grid=(n_out // tb,),
        in_specs=[pl.BlockSpec(memory_space=pl.ANY)],
        out_specs=[pl.BlockSpec((d, tb), lambda i: (0, i)),
                   pl.BlockSpec((d, LANES), lambda i: (0, 0)),
                   pl.BlockSpec((d, LANES), lambda i: (0, 0))],
        out_shape=[jax.ShapeDtypeStruct((d, n_out), BF16),
                   jax.ShapeDtypeStruct((d, LANES), BF16),
                   jax.ShapeDtypeStruct((d, LANES), BF16)],
        scratch_shapes=[pltpu.VMEM((2, tb, d), F32), pltpu.VMEM((n_lr, d), F32),
                        pltpu.SemaphoreType.DMA((2,)), pltpu.SemaphoreType.DMA(())],
        compiler_params=_cparams(("arbitrary",)),
        name="repack_w_in",
    )(w_t)


def _premix_kernel(x_ref, mod_ref, g_ref, w_ref, wlh_ref, wll_ref, o_ref, lr_ref, h_scr):
    @pl.when(pl.program_id(1) == 0)
    def _():
        h = _rms(x_ref[...], g_ref[...]) * (1.0 + mod_ref[0, 1:2, :]) + mod_ref[0, 0:1, :]
        hh, hl = _split2(h)
        h_scr[...] = hh
        lr_ref[...] = _dot(hh, wlh_ref[...]) + (_dot(hh, wll_ref[...]) + _dot(hl, wlh_ref[...]))

    o_ref[...] = _dot(h_scr[...], w_ref[...]).astype(o_ref.dtype)


def _premix_proj(x, mod, mod_row, g, w_main, wlr_hi, wlr_lo, tm, tn):
    n, d = x.shape
    nc = w_main.shape[1]
    return pl.pallas_call(
        _premix_kernel,
        grid=(n // tm, nc // tn),
        in_specs=[pl.BlockSpec((tm, d), lambda i, j: (i, 0)),
                  pl.BlockSpec((1, 6, d), lambda i, j: (mod_row(i * tm), 0, 0)),
                  pl.BlockSpec((1, d), lambda i, j: (0, 0)),
                  pl.BlockSpec((d, tn), lambda i, j: (0, j)),
                  pl.BlockSpec((d, LANES), lambda i, j: (0, 0)),
                  pl.BlockSpec((d, LANES), lambda i, j: (0, 0))],
        out_specs=[pl.BlockSpec((tm, tn), lambda i, j: (i, j)),
                   pl.BlockSpec((tm, LANES), lambda i, j: (i, 0))],
        out_shape=[jax.ShapeDtypeStruct((n, nc), BF16),
                   jax.ShapeDtypeStruct((n, LANES), F32)],
        scratch_shapes=[pltpu.VMEM((tm, d), BF16)],
        compiler_params=_cparams(("arbitrary", "arbitrary")),
        name="premix_proj",
    )(x, mod, g, w_main, wlr_hi, wlr_lo)


def _log_sigmoid(x):
    return jnp.minimum(x, 0.0) - jnp.log1p(jnp.exp(-jnp.abs(x)))


def _gla_kernel(q_ref, k_ref, v_ref, g_ref, lr_ref, wdf_ref, bdf_ref, wdb_ref, bdb_ref, gg_ref,
                s0f_ref, s0b_ref, y_ref, sf_ref, sb_ref,
                laf_scr, lab_scr, qf_scr, qb_scr, o_scr, uf_scr, ub_scr, df_scr, db_scr, sinf_scr, sinb_scr,
                *, seq, dk):
    c = GLA_CHUNK
    n_chunks = seq // c
    scale = dk ** -0.5

    lr = lr_ref[0]
    laf_scr[...] = _log_sigmoid(_dot_hp(lr, wdf_ref[0]) + bdf_ref[...]) / GLA_TAU
    lab_scr[...] = _log_sigmoid(_dot_hp(lr, wdb_ref[0]) + bdb_ref[...]) / GLA_TAU

    per = GLA_UNROLL
    blk = per * c
    row = lax.broadcasted_iota(jnp.int32, (blk, blk), 0)
    col = lax.broadcasted_iota(jnp.int32, (blk, blk), 1)
    same = (row // c) == (col // c)
    lower = same & (row >= col)
    upper = same & (col >= row)
    t_fwd = lower.astype(BF16)
    t_bwd = upper.astype(BF16)

    def chunk_rows(x, r):
        return jnp.concatenate([jnp.broadcast_to(x[j * c + r:j * c + r + 1], (c, dk)) for j in range(per)], axis=0)

    def block_local(m, carry):
        sl = pl.ds(pl.multiple_of(m * blk, blk), blk)
        bf = _dot_exact_lhs(t_fwd, laf_scr[sl, :])
        bb = _dot_exact_lhs(t_bwd, lab_scr[sl, :])
        tot_f = chunk_rows(bf, c - 1)
        tot_b = chunk_rows(bb, 0)
        q = q_ref[0, sl, :].astype(F32) * scale
        k = k_ref[0, sl, :].astype(F32)
        v = v_ref[0, sl, :]
        qf = (q * jnp.exp(bf)).astype(BF16)
        kf = (k * jnp.exp(-bf)).astype(BF16)
        ksf = (k * jnp.exp(tot_f - bf)).astype(BF16)
        qb = (q * jnp.exp(bb)).astype(BF16)
        kb = (k * jnp.exp(-bb)).astype(BF16)
        ksb = (k * jnp.exp(tot_b - bb)).astype(BF16)
        att = jnp.where(lower, _dot_nt(qf, kf), 0.0) + jnp.where(upper, _dot_nt(qb, kb), 0.0)
        o_scr[sl, :] = _dot(att.astype(BF16), v)
        qf_scr[sl, :] = qf
        qb_scr[sl, :] = qb
        dec_f = jnp.exp(tot_f)
        dec_b = jnp.exp(tot_b)
        for j in range(per):
            n = m * per + j
            rows = slice(j * c, (j + 1) * c)
            uf_scr[n] = _dot_tn(v[rows], ksf[rows])
            ub_scr[n] = _dot_tn(v[rows], ksb[rows])
            df_scr[n] = dec_f[j * c:j * c + 8]
            db_scr[n] = dec_b[j * c:j * c + 8]
        return carry

    lax.fori_loop(0, n_chunks // per, block_local, 0)

    def scan_fwd(n, s):
        sinf_scr[n] = s.astype(BF16)
        return s * df_scr[n][0:1, :] + uf_scr[n]

    def scan_bwd(i, s):
        n = n_chunks - 1 - i
        sinb_scr[n] = s.astype(BF16)
        return s * db_scr[n][0:1, :] + ub_scr[n]

    s_f = lax.fori_loop(0, n_chunks, scan_fwd, s0f_ref[0, 0].T, unroll=GLA_UNROLL)
    s_b = lax.fori_loop(0, n_chunks, scan_bwd, s0b_ref[0, 0].T, unroll=GLA_UNROLL)
    sf_ref[0, 0] = s_f.T
    sb_ref[0, 0] = s_b.T

    def chunk_inter(n, carry):
        sl = pl.ds(pl.multiple_of(n * c, c), c)
        o_scr[sl, :] += _dot_nt(qf_scr[sl, :], sinf_scr[n]) + _dot_nt(qb_scr[sl, :], sinb_scr[n])
        return carry

    lax.fori_loop(0, n_chunks, chunk_inter, 0, unroll=GLA_UNROLL)

    o = _rms(o_scr[...], gg_ref[...])
    y_ref[0] = (o * _silu(g_ref[0].astype(F32))).astype(y_ref.dtype)


def _gla(proj, lr, wdf, bdf, wdb, bdb, g_gla, s0f, s0b, batch, seq, dk, dv):
    heads = GLA_HEADS
    proj3 = proj.reshape(batch, seq, proj.shape[-1])
    lr3 = lr.reshape(batch, seq, LANES)
    kdim = heads * dk
    width = heads * dv
    k_blk = kdim // dk
    v_blk = 2 * kdim // dv
    g_blk = (2 * kdim + width) // dv
    n_chunks = seq // GLA_CHUNK

    def s0_map(s0):
        if s0.shape[0] == batch:
            return lambda b, h: (b, h, 0, 0)
        return lambda b, h: (0, 0, 0, 0)

    kern = functools.partial(_gla_kernel, seq=seq, dk=dk)
    return pl.pallas_call(
        kern,
        grid=(batch, heads),
        in_specs=[pl.BlockSpec((1, seq, dk), lambda b, h: (b, 0, h)),
                  pl.BlockSpec((1, seq, dk), lambda b, h: (b, 0, k_blk + h)),
                  pl.BlockSpec((1, seq, dv), lambda b, h: (b, 0, v_blk + h)),
                  pl.BlockSpec((1, seq, dv), lambda b, h: (b, 0, g_blk + h)),
                  pl.BlockSpec((1, seq, LANES), lambda b, h: (b, 0, 0)),
                  pl.BlockSpec((1, LANES, dk), lambda b, h: (h, 0, 0)),
                  pl.BlockSpec((1, dk), lambda b, h: (0, h)),
                  pl.BlockSpec((1, LANES, dk), lambda b, h: (h, 0, 0)),
                  pl.BlockSpec((1, dk), lambda b, h: (0, h)),
                  pl.BlockSpec((1, dv), lambda b, h: (0, 0)),
                  pl.BlockSpec((1, 1, dk, dv), s0_map(s0f)),
                  pl.BlockSpec((1, 1, dk, dv), s0_map(s0b))],
        out_specs=[pl.BlockSpec((1, seq, dv), lambda b, h: (b, 0, h)),
                   pl.BlockSpec((1, 1, dk, dv), lambda b, h: (b, h, 0, 0)),
                   pl.BlockSpec((1, 1, dk, dv), lambda b, h: (b, h, 0, 0))],
        out_shape=[jax.ShapeDtypeStruct((batch, seq, width), BF16),
                   jax.ShapeDtypeStruct((batch, heads, dk, dv), F32),
                   jax.ShapeDtypeStruct((batch, heads, dk, dv), F32)],
        scratch_shapes=[pltpu.VMEM((seq, dk), F32), pltpu.VMEM((seq, dk), F32),
                        pltpu.VMEM((seq, dk), BF16), pltpu.VMEM((seq, dk), BF16),
                        pltpu.VMEM((seq, dv), F32),
                        pltpu.VMEM((n_chunks, dv, dk), F32), pltpu.VMEM((n_chunks, dv, dk), F32),
                        pltpu.VMEM((n_chunks, 8, dk), F32), pltpu.VMEM((n_chunks, 8, dk), F32),
                        pltpu.VMEM((n_chunks, dv, dk), BF16), pltpu.VMEM((n_chunks, dv, dk), BF16)],
        compiler_params=_cparams(("arbitrary", "arbitrary")),
        name="gla",
    )(proj3, proj3, proj3, proj3, lr3, wdf, bdf, wdb, bdb, g_gla, s0f, s0b)


def _filter_kernel(pe_ref, w1_ref, b1_ref, w2_ref, b2_ref, fr_ref, w3_ref, dl_ref, ad_ref, hl_ref, hm_ref, mlp_scr,
                   *, seq):
    @pl.when(pl.program_id(0) == 0)
    def _():
        fr = fr_ref[...]
        h1 = jnp.sin(fr * (_dot_hp(pe_ref[...], w1_ref[...]) + b1_ref[...]))
        mlp_scr[...] = jnp.sin(fr * (_dot_hp(h1, w2_ref[...]) + b2_ref[...]))

    h2 = mlp_scr[...]
    dec = jnp.exp(-pe_ref[:, 0:1] * dl_ref[...])
    row = lax.broadcasted_iota(jnp.int32, (seq, 1), 0)
    alt = jnp.where(row % 2 == 0, 1.0, -1.0)
    phase = row % 4
    cos_half = jnp.where(phase == 0, 1.0, jnp.where(phase == 2, -1.0, 0.0))
    sin_half = jnp.where(phase == 1, 1.0, jnp.where(phase == 3, -1.0, 0.0))
    for o in range(HY_ORDER):
        ff = _dot_hp(h2, w3_ref[o]) * dec
        fb = _dot_hp(h2, w3_ref[HY_ORDER + o]) * dec
        nrm = jnp.sum(jnp.abs(ff), axis=0, keepdims=True) + jnp.sum(jnp.abs(fb), axis=0, keepdims=True)
        ff = ff / nrm
        fb = jnp.where(row == 0, 0.0, fb / nrm)
        a = ff + fb
        d = fb - ff
        ad_ref[o, 0] = a.astype(ad_ref.dtype)
        ad_ref[o, 1] = d.astype(ad_ref.dtype)
        ad_ref[o, 2] = (alt * a).astype(ad_ref.dtype)
        ad_ref[o, 3] = (-alt * d).astype(ad_ref.dtype)
        hl_ref[o] = jnp.sum(alt * a, axis=0, keepdims=True)
        hm_ref[o] = jnp.concatenate([jnp.sum(cos_half * a, axis=0, keepdims=True),
                                     jnp.sum(sin_half * d, axis=0, keepdims=True)], axis=0) * (1.0 / seq)


def _hy_filter(pe, w1p, b1, w2, b2, freq, w3r, deltas, seq, width, tc):
    hid = w2.shape[0]
    kern = functools.partial(_filter_kernel, seq=seq)
    return pl.pallas_call(
        kern,
        grid=(width // tc,),
        in_specs=[pl.BlockSpec((seq, LANES), lambda c: (0, 0)),
                  pl.BlockSpec((LANES, hid), lambda c: (0, 0)),
                  pl.BlockSpec((1, hid), lambda c: (0, 0)),
                  pl.BlockSpec((hid, hid), lambda c: (0, 0)),
                  pl.BlockSpec((1, hid), lambda c: (0, 0)),
                  pl.BlockSpec((1, hid), lambda c: (0, 0)),
                  pl.BlockSpec((2 * HY_ORDER, hid, tc), lambda c: (0, 0, c)),
                  pl.BlockSpec((1, tc), lambda c: (0, c))],
        out_specs=[pl.BlockSpec((HY_ORDER, 4, seq, tc), lambda c: (0, 0, 0, c)),
                   pl.BlockSpec((HY_ORDER, 1, tc), lambda c: (0, 0, c)),
                   pl.BlockSpec((HY_ORDER, 2, tc), lambda c: (0, 0, c))],
        out_shape=[jax.ShapeDtypeStruct((HY_ORDER, 4, seq, width), BF16),
                   jax.ShapeDtypeStruct((HY_ORDER, 1, width), F32),
                   jax.ShapeDtypeStruct((HY_ORDER, 2, width), F32)],
        scratch_shapes=[pltpu.VMEM((seq, hid), F32)],
        compiler_params=_cparams(("arbitrary",)),
        name="hy_filter",
    )(pe, w1p, b1, w2, b2, freq, w3r, deltas)


def _spectrum_kernel(cs_ref, ad_ref, h_ref, *, seq, per_part, tr):
    r = pl.program_id(1)
    k = (r % per_part) * tr + lax.broadcasted_iota(jnp.int32, (tr, 1), 0)
    wgt = jnp.where(k == 0, 1.0, 2.0) * (0.5 / seq)
    h_ref[0] = wgt * _dot(cs_ref[0], ad_ref[0, 0])


def _hy_spectrum(cs3, ad, seq, part_rows, n_parts, width, tr, tc):
    per_part = part_rows // tr
    kern = functools.partial(_spectrum_kernel, seq=seq, per_part=per_part, tr=tr)
    return pl.pallas_call(
        kern,
        grid=(HY_ORDER, n_parts * per_part, width // tc),
        in_specs=[pl.BlockSpec((1, tr, seq), lambda o, r, c: ((r // per_part) % 2, r % per_part, 0)),
                  pl.BlockSpec((1, 1, seq, tc), lambda o, r, c: (o, r // per_part, 0, c))],
        out_specs=pl.BlockSpec((1, tr, tc), lambda o, r, c: (o, r, c)),
        out_shape=jax.ShapeDtypeStruct((HY_ORDER, n_parts * part_rows, width), F32),
        compiler_params=_cparams(("arbitrary", "arbitrary", "arbitrary")),
        name="hy_spectrum",
    )(cs3, ad)


def _short_conv_fn(seq, seg):
    row = lax.broadcasted_iota(jnp.int32, (seq, 1), 0)
    pos = row % seg

    def short_conv(u_ref, cw_ref, cb_ref):
        u = u_ref[0].astype(F32)
        prev = jnp.where(pos == 0, 0.0, pltpu.roll(u, 1, 0))
        nxt = jnp.where(pos == seg - 1, 0.0, pltpu.roll(u, seq - 1, 0))
        return prev * cw_ref[0:1, :] + u * cw_ref[1:2, :] + nxt * cw_ref[2:3, :] + cb_ref[...]

    return short_conv


def _hyconv_kernel(u0_ref, u1_ref, u2_ref, cw0_ref, cw1_ref, cw2_ref, cb0_ref, cb1_ref, cb2_ref,
                   cs_ref, h_ref, hl_ref, bias_ref, z_ref, y_scr, *, seq, seg):
    short_conv = _short_conv_fn(seq, seg)
    row = lax.broadcasted_iota(jnp.int32, (seq, 1), 0)
    alt = jnp.where(row % 2 == 0, 1.0, -1.0)
    gates = ((u1_ref, cw1_ref, cb1_ref), (u2_ref, cw2_ref, cb2_ref))
    z = short_conv(u0_ref, cw0_ref, cb0_ref)
    kb = min(seq, 512)
    for n in range(HY_ORDER):
        zb = z.astype(BF16)
        for r in range(0, seq, kb):
            xc = _dot(cs_ref[r:r + kb, :], zb)
            xs = _dot(cs_ref[seq + r:seq + r + kb, :], zb)
            hre = h_ref[n, r:r + kb, :]
            him = h_ref[n, seq + r:seq + r + kb, :]
            y_scr[r:r + kb, :] = (xc * hre + xs * him).astype(BF16)
            y_scr[seq + r:seq + r + kb, :] = (xs * hre - xc * him).astype(BF16)
        nyq = jnp.sum(alt * z, axis=0, keepdims=True) * (hl_ref[n] * (0.5 / seq))
        conv = _dot(cs_ref[:seq, :], y_scr[:seq, :]) + _dot(cs_ref[seq:, :], y_scr[seq:, :]) + alt * nyq
        z = short_conv(*gates[n]) * (conv + bias_ref[n] * z)
    z_ref[0] = z.astype(z_ref.dtype)


def _hyconv_split_kernel(u0_ref, u1_ref, u2_ref, cw0_ref, cw1_ref, cw2_ref, cb0_ref, cb1_ref, cb2_ref,
                         cs_ref, h_ref, hm_ref, tw_ref, bias_ref, z_ref, g_scr, z_scr, c_scr, *, seq, seg):
    m = seq // 2
    tc = z_ref.shape[2]
    n_lane_blocks = tc // LANES
    short_conv = _short_conv_fn(seq, seg)
    row = lax.broadcasted_iota(jnp.int32, (m, 1), 0)
    alt = jnp.where(row % 2 == 0, 1.0, -1.0)
    gates = ((u1_ref, cw1_ref, cb1_ref), (u2_ref, cw2_ref, cb2_ref))
    z = short_conv(u0_ref, cw0_ref, cb0_ref)
    kb = min(m, 512)
    for n in range(HY_ORDER):
        for j in range(n_lane_blocks):
            z_scr[j] = z[:, j * LANES:(j + 1) * LANES]
        ze = jnp.concatenate([z_scr[j, pl.ds(0, m, stride=2), :] for j in range(n_lane_blocks)], axis=1)
        zo = jnp.concatenate([z_scr[j, pl.ds(1, m, stride=2), :] for j in range(n_lane_blocks)], axis=1)
        e_mid = jnp.sum(alt * ze, axis=0, keepdims=True)
        o_mid = jnp.sum(alt * zo, axis=0, keepdims=True)
        zeb = ze.astype(BF16)
        zob = zo.astype(BF16)
        for r in range(0, m, kb):
            rows = slice(r, r + kb)
            srows = slice(m + r, m + r + kb)
            ec = _dot(cs_ref[rows, :], zeb)
            es = _dot(cs_ref[srows, :], zeb)
            oc = _dot(cs_ref[rows, :], zob)
            os_ = _dot(cs_ref[srows, :], zob)
            c = tw_ref[0, rows, :]
            s = tw_ref[1, rows, :]
            pc = c * oc - s * os_
            ps = c * os_ + s * oc
            xca, xsa = ec + pc, es + ps
            xcb, xsb = ec - pc, ps - es
            har = h_ref[n, rows, :]
            hai = h_ref[n, srows, :]
            hbr = h_ref[n, 2 * m + r:2 * m + r + kb, :]
            hbi = h_ref[n, 3 * m + r:3 * m + r + kb, :]
            yar = xca * har + xsa * hai
            yai = xca * hai - xsa * har
            ybr = xcb * hbr + xsb * hbi
            ybi = xcb * hbi - xsb * hbr
            dr = yar - ybr
            di = yai + ybi
            g_scr[0, rows, :] = (yar + ybr).astype(BF16)
            g_scr[0, srows, :] = (ybi - yai).astype(BF16)
            g_scr[1, rows, :] = (c * dr - s * di).astype(BF16)
            g_scr[1, srows, :] = (-(s * dr + c * di)).astype(BF16)
        hr = hm_ref[n, 0:1, :]
        hi = hm_ref[n, 1:2, :]
        ymr = e_mid * hr + o_mid * hi
        ymi = e_mid * hi - o_mid * hr
        y_even = _dot(cs_ref[:m, :], g_scr[0, :m, :]) + _dot(cs_ref[m:, :], g_scr[0, m:, :]) + alt * ymr
        y_odd = _dot(cs_ref[:m, :], g_scr[1, :m, :]) + _dot(cs_ref[m:, :], g_scr[1, m:, :]) - alt * ymi
        for j in range(n_lane_blocks):
            c_scr[j, pl.ds(0, m, stride=2), :] = y_even[:, j * LANES:(j + 1) * LANES]
            c_scr[j, pl.ds(1, m, stride=2), :] = y_odd[:, j * LANES:(j + 1) * LANES]
        conv = jnp.concatenate([c_scr[j] for j in range(n_lane_blocks)], axis=1)
        z = short_conv(*gates[n]) * (conv + bias_ref[n] * z)
    z_ref[0] = z.astype(z_ref.dtype)


def _hy_conv(proj, conv_w, conv_b, cs, hspec, hl, hm, tw, bias, batch, seq, seg, col0, width, tc, split):
    proj3 = proj.reshape(batch, seq, proj.shape[-1])
    blk0 = col0 // tc
    per = width // tc
    once = dict(pipeline_mode=pl.Buffered(1))

    def u_spec(p):
        return pl.BlockSpec((1, seq, tc), lambda c, b: (b, 0, blk0 + p * per + c))

    def w_spec(p, rows):
        return pl.BlockSpec((rows, tc), lambda c, b: (0, p * per + c))

    in_specs = [u_spec(0), u_spec(1), u_spec(2),
                w_spec(0, HY_SHORT), w_spec(1, HY_SHORT), w_spec(2, HY_SHORT),
                w_spec(0, 1), w_spec(1, 1), w_spec(2, 1),
                pl.BlockSpec(cs.shape, lambda c, b: (0, 0), **once),
                pl.BlockSpec((HY_ORDER, 2 * seq, tc), lambda c, b: (0, 0, c), **once)]
    bias_spec = pl.BlockSpec((HY_ORDER, 1, tc), lambda c, b: (0, 0, c))
    operands = [proj3, proj3, proj3, conv_w, conv_w, conv_w, conv_b, conv_b, conv_b, cs, hspec]
    if split:
        kern = functools.partial(_hyconv_split_kernel, seq=seq, seg=seg)
        in_specs += [pl.BlockSpec((HY_ORDER, 2, tc), lambda c, b: (0, 0, c)),
                     pl.BlockSpec((2, seq // 2, tc), lambda c, b: (0, 0, 0)), bias_spec]
        operands += [hm, tw, bias]
        scratch = [pltpu.VMEM((2, seq, tc), BF16), pltpu.VMEM((tc // LANES, seq, LANES), F32),
                   pltpu.VMEM((tc // LANES, seq, LANES), F32)]
    else:
        kern = functools.partial(_hyconv_kernel, seq=seq, seg=seg)
        in_specs += [pl.BlockSpec((HY_ORDER, 1, tc), lambda c, b: (0, 0, c)), bias_spec]
        operands += [hl, bias]
        scratch = [pltpu.VMEM((2 * seq, tc), BF16)]
    return pl.pallas_call(
        kern,
        grid=(per, batch),
        in_specs=in_specs,
        out_specs=pl.BlockSpec((1, seq, tc), lambda c, b: (b, 0, c)),
        out_shape=jax.ShapeDtypeStruct((batch, seq, width), BF16),
        scratch_shapes=scratch,
        compiler_params=_cparams(("arbitrary", "arbitrary")),
        name="hy_conv",
    )(*operands)


def _mixout_kernel(*refs, gla_width, n_own, aliased):
    (yg_ref, zh_ref, x_ref, mod_ref, ghy_ref, gpost_ref, gpre_ref, wo_ref, wrh_ref, wrl_ref, br_ref,
     cnt_in_ref) = refs[:12]
    x1_ref, hg_ref, route_ref, cnt_ref, cnt_scr = refs[12 + int(aliased):]
    d = x_ref.shape[1]
    tm = x_ref.shape[0] // MIX_SUBTILES
    i = pl.program_id(0)

    @pl.when(i == 0)
    def _():
        cnt_scr[...] = cnt_in_ref[...]

    @pl.when(i < n_own)
    def _():
        for s in range(MIX_SUBTILES):
            _mixout_rows(slice(s * tm, (s + 1) * tm), tm, d, gla_width, yg_ref, zh_ref, x_ref, mod_ref, ghy_ref,
                         gpost_ref, gpre_ref, wo_ref, wrh_ref, wrl_ref, br_ref, x1_ref, hg_ref, route_ref, cnt_scr)

    @pl.when(i >= n_own)
    def _():
        hg_ref[...] = jnp.zeros_like(hg_ref)

    cnt_ref[...] = cnt_scr[...]


def _mixout_rows(rows, tm, d, gla_width, yg_ref, zh_ref, x_ref, mod_ref, ghy_ref, gpost_ref, gpre_ref, wo_ref,
                 wrh_ref, wrl_ref, br_ref, x1_ref, hg_ref, route_ref, cnt_scr):
    yh = _rms(zh_ref[rows, :].astype(F32), ghy_ref[...]).astype(BF16)
    y = _dot(yg_ref[rows, :], wo_ref[:gla_width, :]) + _dot(yh, wo_ref[gla_width:, :])
    x1 = x_ref[rows, :] + mod_ref[0, 2:3, :] * _rms(y, gpost_ref[...])
    x1_ref[rows, :] = x1
    h2 = _rms(x1, gpre_ref[...]) * (1.0 + mod_ref[0, 4:5, :]) + mod_ref[0, 3:4, :]
    hh, hl = _split2(h2)
    hg_ref[rows, :d] = h2
    logits = _dot(hh, wrh_ref[...]) + (_dot(hh, wrl_ref[...]) + _dot(hl, wrh_ref[...])) + br_ref[...]

    lane = lax.broadcasted_iota(jnp.int32, logits.shape, 1).astype(F32)
    neg = -jnp.inf
    is_grp = (lane >= N_EXPERTS) & (lane < N_EXPERTS + N_GROUPS)
    m = jnp.max(jnp.where(is_grp, logits, neg), axis=1, keepdims=True)
    p_grp = 1.0 / jnp.sum(jnp.where(is_grp, jnp.exp(logits - m), 0.0), axis=1, keepdims=True)
    grp = jnp.min(jnp.where(is_grp & (logits == m), lane - N_EXPERTS, 1e9), axis=1, keepdims=True)
    sel = (lane >= grp * EXP_PER_GROUP) & (lane < (grp + 1.0) * EXP_PER_GROUP)
    me = jnp.max(jnp.where(sel, logits, neg), axis=1, keepdims=True)
    pe = jnp.where(sel, jnp.exp(logits - me), -1.0)
    v1 = jnp.max(pe, axis=1, keepdims=True)
    i1 = jnp.min(jnp.where(pe == v1, lane, 1e9), axis=1, keepdims=True)
    pe2 = jnp.where(lane == i1, -1.0, pe)
    v2 = jnp.max(pe2, axis=1, keepdims=True)
    i2 = jnp.min(jnp.where(pe2 == v2, lane, 1e9), axis=1, keepdims=True)
    den = v1 + v2
    gates = jnp.where(lane == i1, v1 / den, jnp.where(lane == i2, v2 / den, 0.0)) * p_grp

    lo = jnp.minimum(i1, i2) - grp * EXP_PER_GROUP
    hi = jnp.maximum(i1, i2) - grp * EXP_PER_GROUP
    bucket = grp * PAIRS_PER_GROUP + lo * (2 * EXP_PER_GROUP - 1 - lo) * 0.5 + (hi - lo - 1.0)
    onehot = lane == bucket
    r_i = lax.broadcasted_iota(jnp.int32, (tm, tm), 0)
    c_i = lax.broadcasted_iota(jnp.int32, (tm, tm), 1)
    earlier = _dot((r_i > c_i).astype(BF16), onehot.astype(BF16)) + cnt_scr[...]
    rank = jnp.sum(jnp.where(onehot, earlier, 0.0), axis=1, keepdims=True)
    cnt_scr[...] += jnp.sum(onehot.astype(F32), axis=0, keepdims=True)
    route = jnp.where(lane == ROUTE_BUCKET_LANE, bucket, jnp.where(lane == ROUTE_RANK_LANE, rank, gates))
    hg_ref[rows, d:] = route
    route_ref[:, rows] = route.T[ROUTE_BUCKET_LANE:ROUTE_BUCKET_LANE + 8, :]


def _mix_out(yg, zh, x, mod, mod_row, g_hy, g_post, g_pre, w_out, wr_hi, wr_lo, b_r, cnt_in, hg_all, row0, n_total, tm):
    n, d = x.shape
    gw = yg.shape[1]
    hw = zh.shape[1]
    aliased = hg_all is not None
    n_own = n // tm
    n_steps = n_own if aliased else n_total // tm
    assert row0 % tm == 0 and n_total % tm == 0 and (row0 + n == n_total if aliased else row0 == 0)
    kern = functools.partial(_mixout_kernel, gla_width=gw, n_own=n_own, aliased=aliased)
    row = lambda i: (jnp.minimum(i, n_own - 1), 0)
    fixed = lambda i: (0, 0)
    in_specs = [pl.BlockSpec((tm, gw), row),
                pl.BlockSpec((tm, hw), row),
                pl.BlockSpec((tm, d), row),
                pl.BlockSpec((1, 6, d), lambda i: (mod_row(jnp.minimum(i, n_own - 1) * tm), 0, 0)),
                pl.BlockSpec((1, hw), fixed),
                pl.BlockSpec((1, d), fixed),
                pl.BlockSpec((1, d), fixed),
                pl.BlockSpec((gw + hw, d), fixed),
                pl.BlockSpec((d, LANES), fixed),
                pl.BlockSpec((d, LANES), fixed),
                pl.BlockSpec((1, LANES), fixed),
                pl.BlockSpec((1, LANES), fixed)]
    operands = [yg, zh, x, mod, g_hy, g_post, g_pre, w_out, wr_hi, wr_lo, b_r, cnt_in]
    if aliased:
        in_specs.append(pl.BlockSpec(memory_space=pl.ANY))
        operands.append(hg_all)
    return pl.pallas_call(
        kern,
        grid=(n_steps,),
        in_specs=in_specs,
        out_specs=[pl.BlockSpec((tm, d), row),
                   pl.BlockSpec((tm, d + LANES), lambda i: (row0 // tm + i, 0)),
                   pl.BlockSpec((8, tm), lambda i: (0, jnp.minimum(i, n_own - 1))),
                   pl.BlockSpec((1, LANES), fixed)],
        out_shape=[jax.ShapeDtypeStruct((n, d), F32),
                   jax.ShapeDtypeStruct((n_total, d + LANES), F32),
                   jax.ShapeDtypeStruct((8, n), F32),
                   jax.ShapeDtypeStruct((1, LANES), F32)],
        scratch_shapes=[pltpu.VMEM((1, LANES), F32)],
        input_output_aliases={len(operands) - 1: 1} if aliased else {},
        compiler_params=_cparams(("arbitrary",)),
        name="mix_out",
    )(*operands)


def _row_gather_copy(src_hbm, row, buf, slot, r, sem):
    return pltpu.make_async_copy(src_hbm.at[pl.ds(row, 1), :], buf.at[slot, pl.ds(r, 1), :], sem.at[slot])


def _start_row_gather(idx_ref, base, src_hbm, buf, slot, sem, tm):
    def body(r, carry):
        _row_gather_copy(src_hbm, idx_ref[base + r], buf, slot, r, sem).start()
        return carry

    lax.fori_loop(0, tm, body, 0, unroll=8)


def _start_row_gather_inline(idx_ref, base, src_hbm, buf, slot, sem, tm):
    for r in range(tm):
        _row_gather_copy(src_hbm, idx_ref[base + r], buf, slot, r, sem).start()


def _wait_row_gather(src_hbm, buf, slot, sem, tm):
    pltpu.make_async_copy(src_hbm.at[pl.ds(0, tm), :], buf.at[slot], sem.at[slot]).wait()


def _moe_kernel(src_ref, ea_ref, eb_ref, valid_ref, hg_hbm,
                wga_ref, wua_ref, wda_ref, wgb_ref, wub_ref, wdb_ref, y_ref, buf, sem, *, tm, d):
    t = pl.program_id(0)
    slot = t % 2

    @pl.when(t == 0)
    def _():
        _start_row_gather(src_ref, 0, hg_hbm, buf, 0, sem, tm)

    @pl.when(valid_ref[t] == 1)
    def _():
        _wait_row_gather(hg_hbm, buf, slot, sem, tm)
        _start_row_gather_inline(src_ref, (t + 1) * tm, hg_hbm, buf, 1 - slot, sem, tm)
        h = buf[slot, :, :d].astype(BF16)
        gates = buf[slot, :, d:]
        lane = lax.broadcasted_iota(jnp.int32, gates.shape, 1)

        def expert(wg_ref, wu_ref, wd_ref, e):
            he = (_silu(_dot(h, wg_ref[0])) * _dot(h, wu_ref[0])).astype(BF16)
            gate = jnp.sum(jnp.where(lane == e, gates, 0.0), axis=1, keepdims=True)
            return gate * _dot(he, wd_ref[0])

        y_ref[...] = expert(wga_ref, wua_ref, wda_ref, ea_ref[t]) + expert(wgb_ref, wub_ref, wdb_ref, eb_ref[t])

    @pl.when(valid_ref[t] == 0)
    def _():
        @pl.when(valid_ref[jnp.maximum(t - 1, 0)] == 1)
        def _():
            _wait_row_gather(hg_hbm, buf, slot, sem, tm)

        y_ref[...] = jnp.zeros_like(y_ref)


def _moe(hg, src, ea, eb, valid, wg, wu, wd, n_tiles, tm):
    d = hg.shape[1] - LANES
    _, _, de = wg.shape
    kern = functools.partial(_moe_kernel, tm=tm, d=d)
    wa = lambda t, src, ea, eb, valid: (ea[t], 0, 0)
    wb = lambda t, src, ea, eb, valid: (eb[t], 0, 0)
    return pl.pallas_call(
        kern,
        grid_spec=pltpu.PrefetchScalarGridSpec(
            num_scalar_prefetch=4,
            grid=(n_tiles,),
            in_specs=[pl.BlockSpec(memory_space=pl.ANY),
                      pl.BlockSpec((1, d, de), wa), pl.BlockSpec((1, d, de), wa), pl.BlockSpec((1, de, d), wa),
                      pl.BlockSpec((1, d, de), wb), pl.BlockSpec((1, d, de), wb), pl.BlockSpec((1, de, d), wb)],
            out_specs=pl.BlockSpec((tm, d), lambda t, src, ea, eb, valid: (t, 0)),
            scratch_shapes=[pltpu.VMEM((2, tm, d + LANES), F32), pltpu.SemaphoreType.DMA((2,))]),
        out_shape=jax.ShapeDtypeStruct((n_tiles * tm, d), F32),
        compiler_params=_cparams(("arbitrary",)),
        name="moe",
    )(src, ea, eb, valid, hg, wg, wu, wd, wg, wu, wd)


def _ffn_out_kernel(pos_ref, y_hbm, x1_ref, mod_ref, gpost_ref, o_ref, buf, sem, *, tm):
    i = pl.program_id(0)
    n_i = pl.num_programs(0)
    slot = i % 2

    @pl.when(i == 0)
    def _():
        _start_row_gather(pos_ref, 0, y_hbm, buf, 0, sem, tm)

    _wait_row_gather(y_hbm, buf, slot, sem, tm)

    @pl.when(i + 1 < n_i)
    def _():
        _start_row_gather_inline(pos_ref, (i + 1) * tm, y_hbm, buf, 1 - slot, sem, tm)
        o_ref[...] = x1_ref[...] + mod_ref[0, 5:6, :] * _rms(buf[slot], gpost_ref[...])

    @pl.when(i + 1 == n_i)
    def _():
        o_ref[...] = x1_ref[...] + mod_ref[0, 5:6, :] * _rms(buf[slot], gpost_ref[...])


def _ffn_out(y_sorted, pos, x1, mod, mod_row, g_post, tm):
    n, d = x1.shape
    kern = functools.partial(_ffn_out_kernel, tm=tm)
    return pl.pallas_call(
        kern,
        grid_spec=pltpu.PrefetchScalarGridSpec(
            num_scalar_prefetch=1,
            grid=(n // tm,),
            in_specs=[pl.BlockSpec(memory_space=pl.ANY),
                      pl.BlockSpec((tm, d), lambda i, pos: (i, 0)),
                      pl.BlockSpec((1, 6, d), lambda i, pos: (mod_row(i * tm), 0, 0)),
                      pl.BlockSpec((1, d), lambda i, pos: (0, 0))],
            out_specs=pl.BlockSpec((tm, d), lambda i, pos: (i, 0)),
            scratch_shapes=[pltpu.VMEM((2, tm, d), F32), pltpu.SemaphoreType.DMA((2,))]),
        out_shape=jax.ShapeDtypeStruct((n, d), F32),
        compiler_params=_cparams(("arbitrary",)),
        name="ffn_out",
    )(pos, y_sorted, x1, mod, g_post)


def _route_tables(route, counts, n_tiles, tm):
    n = route.shape[1]
    n_buckets = N_GROUPS * PAIRS_PER_GROUP
    cnt = counts[0, :n_buckets].astype(jnp.int32)
    padded = (cnt + tm - 1) // tm * tm
    ends = jnp.cumsum(padded)
    starts = ends - padded
    bucket = route[0].astype(jnp.int32)
    rank = route[ROUTE_RANK_LANE - ROUTE_BUCKET_LANE].astype(jnp.int32)
    pos = starts[bucket] + rank
    src = (jnp.arange(n_tiles * tm, dtype=jnp.int32) % n).at[pos].set(jnp.arange(n, dtype=jnp.int32))
    n_valid = ends[-1] // tm
    tile = jnp.arange(n_tiles, dtype=jnp.int32)
    used = jnp.minimum(tile, n_valid - 1)
    tile_bucket = jnp.sum((ends[None, :] <= (used * tm)[:, None]).astype(jnp.int32), axis=1)
    grp = tile_bucket // PAIRS_PER_GROUP
    pair = tile_bucket % PAIRS_PER_GROUP
    pair_lo = jnp.array([a for a in range(EXP_PER_GROUP) for b in range(a + 1, EXP_PER_GROUP)], jnp.int32)
    pair_hi = jnp.array([b for a in range(EXP_PER_GROUP) for b in range(a + 1, EXP_PER_GROUP)], jnp.int32)
    ea = grp * EXP_PER_GROUP + pair_lo[pair]
    eb = grp * EXP_PER_GROUP + pair_hi[pair]
    valid = (tile < n_valid).astype(jnp.int32)
    return pos, src, ea, eb, valid


def _dft_kernel(cs_ref, *, seq, tr):
    k = pl.program_id(0) * tr + lax.broadcasted_iota(jnp.int32, (tr, LANES), 0)
    j = lax.broadcasted_iota(jnp.int32, (tr, LANES), 1)
    period = 2 * seq

    def table(step):
        ang = ((k * j * step) % period).astype(F32) * (math.pi / seq)
        return jnp.cos(ang), jnp.sin(ang)

    c0, s0 = table(1)
    c1, s1 = table(LANES)
    for t1 in range(seq // LANES):
        cols = slice(t1 * LANES, (t1 + 1) * LANES)
        ca = c1[:, t1:t1 + 1]
        sa = s1[:, t1:t1 + 1]
        cs_ref[0, :, cols] = (ca * c0 - sa * s0).astype(cs_ref.dtype)
        cs_ref[1, :, cols] = (sa * c0 + ca * s0).astype(cs_ref.dtype)


def _dft_cos_sin(seq, n_rows, tr):
    kern = functools.partial(_dft_kernel, seq=seq, tr=tr)
    return pl.pallas_call(
        kern,
        grid=(n_rows // tr,),
        out_specs=pl.BlockSpec((2, tr, seq), lambda r: (0, r, 0)),
        out_shape=jax.ShapeDtypeStruct((2, n_rows, seq), BF16),
        compiler_params=_cparams(("arbitrary",)),
        name="dft_matrix",
    )()


def _positional_features(seq):
    t = jnp.arange(seq, dtype=F32)
    t01 = t / max(seq - 1, 1)
    ang = 2.0 * math.pi * t / seq
    bands = jnp.linspace(1e-4, HY_BANDS - 1, HY_BANDS, dtype=F32)
    pe = jnp.concatenate([t01[:, None], jnp.cos(ang[:, None] * bands), -jnp.sin(ang[:, None] * bands)], axis=-1)
    return jnp.pad(pe, ((0, 0), (0, LANES - pe.shape[1])))


def _mixer_and_router(x3, mod, mod_row, s0f, s0b, n_seg, p, tiles, cnt_in, hg_all, row0, n_total):
    batch, seq, d = x3.shape
    x = x3.reshape(batch * seq, d)
    dk, dv = p["dk"], p["dv"]
    hy_width = p["hy_width"]

    proj, lr = _premix_proj(x, mod, mod_row, p["g_pre_mix"], p["w_main"], p["wlr_hi"], p["wlr_lo"],
                            tiles["tm_proj"], tiles["tn_proj"])
    y_gla, s_f, s_b = _gla(proj, lr, p["wdf"], p["bdf"], p["wdb"], p["bdb"], p["g_gla"], s0f, s0b,
                           batch, seq, dk, dv)

    ad, hl, hm = _hy_filter(_positional_features(seq), p["hy_w1"], p["hy_b1"], p["hy_w2"], p["hy_b2"], p["hy_freq"],
                            p["hy_w3"], p["hy_deltas"], seq, hy_width, tiles["tc_filter"])
    tr, tc_conv = tiles["tr_spec"], tiles["tc_conv"]
    if tiles["hy_split"]:
        half = seq // 2
        hspec = _hy_spectrum(_dft_cos_sin(seq, half, tr), ad, seq, half, 4, hy_width, tr, tiles["tc_spec"])
        cs = _dft_cos_sin(half, half, tr).reshape(seq, half)
        ang = jnp.arange(half, dtype=F32) * (math.pi / seq)
        tw = jnp.broadcast_to(jnp.stack([jnp.cos(ang), jnp.sin(ang)])[:, :, None], (2, half, tc_conv))
    else:
        cs3 = _dft_cos_sin(seq, seq, tr)
        hspec = _hy_spectrum(cs3, ad, seq, seq, 2, hy_width, tr, tiles["tc_spec"])
        cs = cs3.reshape(2 * seq, seq)
        tw = None
    z_hy = _hy_conv(proj, p["hy_conv_w"], p["hy_conv_b"], cs, hspec, hl, hm, tw, p["hy_bias"], batch, seq,
                    seq // n_seg, p["hy_col0"], hy_width, tc_conv, tiles["hy_split"])

    x1, hg_all, route, counts = _mix_out(y_gla.reshape(batch * seq, -1), z_hy.reshape(batch * seq, -1), x, mod,
                                         mod_row, p["g_hy"], p["g_post_mix"], p["g_pre_ffn"], p["w_out"], p["wr_hi"],
                                         p["wr_lo"], p["b_r"], cnt_in, hg_all, row0, n_total, tiles["tm_mix"])
    return x1, hg_all, route, counts, s_f, s_b


def kernel(x_prompt, x_sample, c, state_gla_fwd, state_gla_bwd, c_ctx, w_ada, b_ada, g_pre_mix, g_post_mix, g_pre_ffn, g_post_ffn, w_in, w_dec_f, b_dec_f, w_dec_b, b_dec_b, g_gla, hy_conv_w, hy_conv_b, hy_w1, hy_b1, hy_w2, hy_b2, hy_w3, hy_freq, hy_bias, g_hy, w_out, w_router_grp, b_router_grp, w_router_exp, b_router_exp, w_exp_gate, w_exp_up, w_exp_down):
    depth = w_ada.shape[0]
    assert depth == 1
    l = 0
    d = x_prompt.shape[-1]
    dec_batch = x_sample.shape[0]
    heads = GLA_HEADS
    dk, dv = state_gla_fwd.shape[-2:]
    kdim = heads * dk
    gla_width = heads * dv
    hy_width = g_hy.shape[-1]
    hid = hy_w2.shape[-1]

    cond = jnp.concatenate([c_ctx[None, :], c], axis=0)
    cond = jnp.pad(cond, ((0, -cond.shape[0] % 8), (0, 0)))
    mod = _ada_mod(cond, w_ada[l], b_ada[l]).reshape(cond.shape[0], 6, d)

    n_main = 2 * kdim + 2 * gla_width
    w_main, wlr_hi, wlr_lo = _repack_w_in(w_in, l, n_main, 2 * GLA_RANK, 256)

    def dec_weight(w, first_row):
        wh = w.reshape(GLA_RANK, heads, dk).transpose(1, 0, 2)
        return jnp.pad(wh, ((0, 0), (first_row, LANES - GLA_RANK - first_row), (0, 0)))

    deltas = jnp.abs(jnp.linspace(math.log(HY_TARGET) / HY_SLOW_PCT, math.log(HY_TARGET) / HY_FAST_PCT, hy_width,
                                  dtype=F32))
    w_r = jnp.pad(jnp.concatenate([w_router_exp[l], w_router_grp[l]], axis=1),
                  ((0, 0), (0, LANES - N_EXPERTS - N_GROUPS)))
    wr_hi, wr_lo = _split2(w_r)
    b_r = jnp.pad(jnp.concatenate([b_router_exp[l], b_router_grp[l]]), (0, LANES - N_EXPERTS - N_GROUPS))

    p = dict(
        dk=dk, dv=dv, hy_width=hy_width, hy_col0=n_main,
        g_pre_mix=g_pre_mix[l][None, :], g_post_mix=g_post_mix[l][None, :],
        g_pre_ffn=g_pre_ffn[l][None, :], g_post_ffn=g_post_ffn[l][None, :],
        w_main=w_main, wlr_hi=wlr_hi, wlr_lo=wlr_lo,
        wdf=dec_weight(w_dec_f[l], 0), bdf=b_dec_f[l][None, :],
        wdb=dec_weight(w_dec_b[l], GLA_RANK), bdb=b_dec_b[l][None, :],
        g_gla=g_gla[l][None, :],
        hy_conv_w=hy_conv_w[l], hy_conv_b=hy_conv_b[l][None, :],
        hy_w1=jnp.pad(hy_w1[l], ((0, LANES - hy_w1.shape[1]), (0, 0))), hy_b1=hy_b1[l][None, :],
        hy_w2=hy_w2[l], hy_b2=hy_b2[l][None, :], hy_freq=hy_freq[l][None, :],
        hy_w3=hy_w3[l].reshape(hid, 2 * HY_ORDER, hy_width).transpose(1, 0, 2), hy_deltas=deltas[None, :],
        hy_bias=hy_bias[l][:, None, :], g_hy=g_hy[l][None, :],
        w_out=w_out[l].astype(BF16), wr_hi=wr_hi, wr_lo=wr_lo, b_r=b_r[None, :],
        w_exp_gate=w_exp_gate[l].astype(BF16), w_exp_up=w_exp_up[l].astype(BF16),
        w_exp_down=w_exp_down[l].astype(BF16),
    )

    zero_state = jnp.zeros((1, 1, dk, dv), F32)
    dec_seq = x_sample.shape[1]
    tiles_p = dict(tm_proj=512, tn_proj=2048, tc_filter=256, tr_spec=256, tc_spec=512, tc_conv=512, tm_mix=512,
                   hy_split=False)
    tiles_s = dict(tm_proj=512, tn_proj=2048, tc_filter=256, tr_spec=512, tc_spec=1024, tc_conv=256, tm_mix=512,
                   hy_split=True)
    tm_moe = 256
    tm_out = 256

    n_p = x_prompt.shape[0] * x_prompt.shape[1]
    n_s = dec_batch * dec_seq
    n_all = n_p + n_s
    mod_row_p = lambda r: 0
    mod_row_s = lambda r: 1 + r // dec_seq
    x1_p, hg_all, route_p, counts, s_f, s_b = _mixer_and_router(
        x_prompt, mod, mod_row_p, zero_state, zero_state, 1, p, tiles_p, jnp.zeros((1, LANES), F32), None, 0, n_all)
    x1_s, hg_all, route_s, counts, _, _ = _mixer_and_router(
        x_sample, mod, mod_row_s, state_gla_fwd[:, l], state_gla_bwd[:, l], dec_seq // GRID_W, p, tiles_s,
        counts, hg_all, n_p, n_all)

    n_tiles = n_all // tm_moe + N_GROUPS * PAIRS_PER_GROUP
    pos, src, ea, eb, valid = _route_tables(jnp.concatenate([route_p, route_s], axis=1), counts, n_tiles, tm_moe)
    y_sorted = _moe(hg_all, src, ea, eb, valid, p["w_exp_gate"], p["w_exp_up"], p["w_exp_down"], n_tiles, tm_moe)
    y_p = _ffn_out(y_sorted, pos[:n_p], x1_p, mod, mod_row_p, p["g_post_ffn"], tm_out)
    y_s = _ffn_out(y_sorted, pos[n_p:], x1_s, mod, mod_row_s, p["g_post_ffn"], tm_out)
    return (y_p.reshape(x_prompt.shape), y_s.reshape(x_sample.shape),
            s_f[:, None].astype(x_prompt.dtype), s_b[:, None].astype(x_prompt.dtype))
```

```python
import functools
import math

import jax
import jax.numpy as jnp
from jax import lax
from jax.experimental import pallas as pl
from jax.experimental.pallas import tpu as pltpu

F32 = jnp.float32
BF16 = jnp.bfloat16

GRID_W = 64
GLA_HEADS = 4
GLA_RANK = 16
GLA_TAU = 16.0
GLA_CHUNK = 64
GLA_UNROLL = 4
MIX_SUBTILES = 2
HY_ORDER = 2
HY_SHORT = 3
HY_BANDS = 16
HY_TARGET = 1e-2
HY_FAST_PCT = 0.3
HY_SLOW_PCT = 1.5
N_GROUPS = 4
EXP_PER_GROUP = 4
N_EXPERTS = N_GROUPS * EXP_PER_GROUP
PAIRS_PER_GROUP = EXP_PER_GROUP * (EXP_PER_GROUP - 1) // 2
ROUTE_BUCKET_LANE = N_EXPERTS
ROUTE_RANK_LANE = N_EXPERTS + 1
EPS = 1e-6

LANES = 128
VMEM_LIMIT = 56 << 20


def _cparams(sem):
    return pltpu.CompilerParams(dimension_semantics=sem, vmem_limit_bytes=VMEM_LIMIT)


def _dot(a, b):
    return jnp.dot(a, b, preferred_element_type=F32)


def _dot_nt(a, b):
    return lax.dot_general(a, b, (((1,), (1,)), ((), ())), preferred_element_type=F32)


def _dot_tn(a, b):
    return lax.dot_general(a, b, (((0,), (0,)), ((), ())), preferred_element_type=F32)


def _split2(x):
    hi = x.astype(BF16)
    lo = (x - hi.astype(F32)).astype(BF16)
    return hi, lo


def _dot_hp(a, b):
    ah, al = _split2(a)
    bh, bl = _split2(b)
    return _dot(ah, bh) + (_dot(ah, bl) + _dot(al, bh))


def _dot_exact_lhs(t, x):
    hi, lo = _split2(x)
    return _dot(t, hi) + _dot(t, lo)


def _rms(x, g):
    return x * lax.rsqrt(jnp.mean(x * x, axis=-1, keepdims=True) + EPS) * g


def _silu(x):
    return x / (1.0 + jnp.exp(-x))


def _ada_kernel(c_ref, w_ref, b_ref, o_ref):
    s = _silu(c_ref[...]).astype(BF16)
    o_ref[...] = _dot(s, w_ref[...].astype(BF16)) + b_ref[...]


def _ada_mod(cond, w_ada, b_ada):
    rows, d = cond.shape
    n = w_ada.shape[1]
    tn = 1024
    return pl.pallas_call(
        _ada_kernel,
        grid=(n // tn,),
        in_specs=[pl.BlockSpec((rows, d), lambda j: (0, 0)),
                  pl.BlockSpec((d, tn), lambda j: (0, j)),
                  pl.BlockSpec((1, tn), lambda j: (0, j))],
        out_specs=pl.BlockSpec((rows, tn), lambda j: (0, j)),
        out_shape=jax.ShapeDtypeStruct((rows, n), F32),
        compiler_params=_cparams(("arbitrary",)),
        name="ada_mod",
    )(cond, w_ada, b_ada.reshape(1, n))


def _repack_kernel(wt_hbm, main_ref, lrh_ref, lrl_ref, buf, lr_buf, sem, lr_sem, *, layer, n_main, n_lr, tb):
    i = pl.program_id(0)
    n_i = pl.num_programs(0)
    slot = i % 2
    d = main_ref.shape[0]

    def block_copy(j, s):
        first = j * tb + jnp.where(j * tb >= n_main, n_lr, 0)
        return pltpu.make_async_copy(wt_hbm.at[layer, pl.ds(first, tb), :], buf.at[s], sem.at[s])

    lr_copy = pltpu.make_async_copy(wt_hbm.at[layer, pl.ds(n_main, n_lr), :], lr_buf, lr_sem)

    @pl.when(i == 0)
    def _():
        block_copy(0, 0).start()
        lr_copy.start()

    @pl.when(i + 1 < n_i)
    def _():
        block_copy(i + 1, 1 - slot).start()

    block_copy(i, slot).wait()
    main_ref[...] = buf[slot].T.astype(BF16)

    @pl.when(i == 0)
    def _():
        lr_copy.wait()
        lr = jnp.concatenate([lr_buf[...], jnp.zeros((LANES - n_lr, d), F32)], axis=0).T
        hi, lo = _split2(lr)
        lrh_ref[...] = hi
        lrl_ref[...] = lo


def _repack_w_in(w_in, layer, n_main, n_lr, tb):
    w_t = jnp.swapaxes(w_in, 1, 2)
    _, n_cols, d = w_t.shape
    n_out = n_cols - n_lr
    assert n_main % tb == 0 and n_out % tb == 0
    kern = functools.partial(_repack_kernel, layer=layer, n_main=n_main, n_lr=n_lr, tb=tb)
    return pl.pallas_call(
        kern,
        grid=(n_out // tb,),
        in_specs=[pl.BlockSpec(memory_space=pl.ANY)],
        out_specs=[pl.BlockSpec((d, tb), lambda i: (0, i)),
                   pl.BlockSpec((d, LANES), lambda i: (0, 0)),
                   pl.BlockSpec((d, LANES), lambda i: (0, 0))],
        out_shape=[jax.ShapeDtypeStruct((d, n_out), BF16),
                   jax.ShapeDtypeStruct((d, LANES), BF16),
                   jax.ShapeDtypeStruct((d, LANES), BF16)],
        scratch_shapes=[pltpu.VMEM((2, tb, d), F32), pltpu.VMEM((n_lr, d), F32),
                        pltpu.SemaphoreType.DMA((2,)), pltpu.SemaphoreType.DMA(())],
        compiler_params=_cparams(("arbitrary",)),
        name="repack_w_in",
    )(w_t)


def _premix_kernel(*refs, n_cast):
    x_ref, mod_ref, g_ref, w_ref, wlh_ref, wll_ref = refs[:6]
    cast_in = refs[6:6 + n_cast]
    o_ref, lr_ref = refs[6 + n_cast:8 + n_cast]
    cast_out = refs[8 + n_cast:8 + 2 * n_cast]
    h_scr = refs[8 + 2 * n_cast]

    @pl.when(pl.program_id(1) == 0)
    def _():
        h = _rms(x_ref[...], g_ref[...]) * (1.0 + mod_ref[0, 1:2, :]) + mod_ref[0, 0:1, :]
        hh, hl = _split2(h)
        h_scr[...] = hh
        lr_ref[...] = _dot(hh, wlh_ref[...]) + (_dot(hh, wll_ref[...]) + _dot(hl, wlh_ref[...]))

    o_ref[...] = _dot(h_scr[...], w_ref[...]).astype(o_ref.dtype)
    for src, dst in zip(cast_in, cast_out):
        dst[...] = src[...].astype(dst.dtype)


def _premix_proj(x, mod, mod_row, g, w_main, wlr_hi, wlr_lo, tm, tn, cast_srcs):
    n, d = x.shape
    nc = w_main.shape[1]
    n_j = nc // tn
    n_steps = (n // tm) * n_j
    n_slabs = 1 << (n_steps.bit_length() - 1)

    def slab_spec(rows, cols):
        assert rows % n_slabs == 0
        return pl.BlockSpec((rows // n_slabs, cols), lambda i, j: (jnp.minimum(i * n_j + j, n_slabs - 1), 0))

    cast_specs = [slab_spec(*w.shape) for w in cast_srcs]
    kern = functools.partial(_premix_kernel, n_cast=len(cast_srcs))
    return pl.pallas_call(
        kern,
        grid=(n // tm, n_j),
        in_specs=[pl.BlockSpec((tm, d), lambda i, j: (i, 0)),
                  pl.BlockSpec((1, 6, d), lambda i, j: (mod_row(i * tm), 0, 0)),
                  pl.BlockSpec((1, d), lambda i, j: (0, 0)),
                  pl.BlockSpec((d, tn), lambda i, j: (0, j)),
                  pl.BlockSpec((d, LANES), lambda i, j: (0, 0)),
                  pl.BlockSpec((d, LANES), lambda i, j: (0, 0))] + cast_specs,
        out_specs=[pl.BlockSpec((tm, tn), lambda i, j: (i, j)),
                   pl.BlockSpec((tm, LANES), lambda i, j: (i, 0))] + cast_specs,
        out_shape=[jax.ShapeDtypeStruct((n, nc), BF16),
                   jax.ShapeDtypeStruct((n, LANES), F32)] + [jax.ShapeDtypeStruct(w.shape, BF16) for w in cast_srcs],
        scratch_shapes=[pltpu.VMEM((tm, d), BF16)],
        compiler_params=_cparams(("arbitrary", "arbitrary")),
        name="premix_proj",
    )(x, mod, g, w_main, wlr_hi, wlr_lo, *cast_srcs)


def _log_sigmoid(x):
    return jnp.minimum(x, 0.0) - jnp.log1p(jnp.exp(-jnp.abs(x)))


def _gla_kernel(q_ref, k_ref, v_ref, g_ref, lr_ref, wdf_ref, bdf_ref, wdb_ref, bdb_ref, gg_ref,
                s0f_ref, s0b_ref, y_ref, sf_ref, sb_ref,
                laf_scr, lab_scr, qf_scr, qb_scr, o_scr, uf_scr, ub_scr, df_scr, db_scr, *, seq, dk, dv, hp):
    c = GLA_CHUNK
    n_chunks = seq // c
    scale = dk ** -0.5
    heads = range(hp)
    kcols = [slice(h * dk, (h + 1) * dk) for h in heads]
    vcols = [slice(h * dv, (h + 1) * dv) for h in heads]

    lr = lr_ref[0]
    for h in heads:
        laf_scr[h] = _log_sigmoid(_dot_hp(lr, wdf_ref[h]) + bdf_ref[:, kcols[h]]) / GLA_TAU
        lab_scr[h] = _log_sigmoid(_dot_hp(lr, wdb_ref[h]) + bdb_ref[:, kcols[h]]) / GLA_TAU

    per = GLA_UNROLL
    blk = per * c
    row = lax.broadcasted_iota(jnp.int32, (blk, blk), 0)
    col = lax.broadcasted_iota(jnp.int32, (blk, blk), 1)
    same = (row // c) == (col // c)
    lower = same & (row >= col)
    upper = same & (col >= row)
    t_fwd = lower.astype(BF16)
    t_bwd = upper.astype(BF16)

    def chunk_rows(x, r):
        return jnp.concatenate([jnp.broadcast_to(x[j * c + r:j * c + r + 1], (c, dk)) for j in range(per)], axis=0)

    def block_local(m, carry):
        sl = pl.ds(pl.multiple_of(m * blk, blk), blk)
        bf = [_dot_exact_lhs(t_fwd, laf_scr[h, sl, :]) for h in heads]
        bb = [_dot_exact_lhs(t_bwd, lab_scr[h, sl, :]) for h in heads]
        tot_f = [chunk_rows(x, c - 1) for x in bf]
        tot_b = [chunk_rows(x, 0) for x in bb]
        q = [q_ref[0, sl, kcols[h]].astype(F32) * scale for h in heads]
        k = [k_ref[0, sl, kcols[h]].astype(F32) for h in heads]
        v = [v_ref[0, sl, vcols[h]] for h in heads]
        qf = [(q[h] * jnp.exp(bf[h])).astype(BF16) for h in heads]
        kf = [(k[h] * jnp.exp(-bf[h])).astype(BF16) for h in heads]
        qb = [(q[h] * jnp.exp(bb[h])).astype(BF16) for h in heads]
        kb = [(k[h] * jnp.exp(-bb[h])).astype(BF16) for h in heads]
        ksf = [(k[h] * jnp.exp(tot_f[h] - bf[h])).astype(BF16) for h in heads]
        ksb = [(k[h] * jnp.exp(tot_b[h] - bb[h])).astype(BF16) for h in heads]
        sc_f = [_dot_nt(qf[h], kf[h]) for h in heads]
        sc_b = [_dot_nt(qb[h], kb[h]) for h in heads]
        att = [(jnp.where(lower, sc_f[h], 0.0) + jnp.where(upper, sc_b[h], 0.0)).astype(BF16) for h in heads]
        o_loc = [_dot(att[h], v[h]) for h in heads]
        dec_f = [jnp.exp(x) for x in tot_f]
        dec_b = [jnp.exp(x) for x in tot_b]
        for h in heads:
            o_scr[h, sl, :] = o_loc[h]
            qf_scr[h, sl, :] = qf[h]
            qb_scr[h, sl, :] = qb[h]
        for j in range(per):
            n = m * per + j
            rows = slice(j * c, (j + 1) * c)
            for h in heads:
                uf_scr[h, n] = _dot_tn(v[h][rows], ksf[h][rows])
                ub_scr[h, n] = _dot_tn(v[h][rows], ksb[h][rows])
                df_scr[h, n] = dec_f[h][j * c:j * c + 8]
                db_scr[h, n] = dec_b[h][j * c:j * c + 8]
        return carry

    lax.fori_loop(0, n_chunks // per, block_local, 0)

    def scan_fwd(n, s):
        upd = [uf_scr[h, n] for h in heads]
        for h in heads:
            uf_scr[h, n] = s[h]
        return tuple(s[h] * df_scr[h, n][0:1, :] + upd[h] for h in heads)

    def scan_bwd(i, s):
        n = n_chunks - 1 - i
        upd = [ub_scr[h, n] for h in heads]
        for h in heads:
            ub_scr[h, n] = s[h]
        return tuple(s[h] * db_scr[h, n][0:1, :] + upd[h] for h in heads)

    s_f = lax.fori_loop(0, n_chunks, scan_fwd, tuple(s0f_ref[0, h].T for h in heads), unroll=GLA_UNROLL)
    s_b = lax.fori_loop(0, n_chunks, scan_bwd, tuple(s0b_ref[0, h].T for h in heads), unroll=GLA_UNROLL)
    for h in heads:
        sf_ref[0, h] = s_f[h].T
        sb_ref[0, h] = s_b[h].T

    def chunk_inter(n, carry):
        sl = pl.ds(pl.multiple_of(n * c, c), c)
        inter = [_dot_nt(qf_scr[h, sl, :], uf_scr[h, n].astype(BF16))
                 + _dot_nt(qb_scr[h, sl, :], ub_scr[h, n].astype(BF16)) for h in heads]
        for h in heads:
            o_scr[h, sl, :] += inter[h]
        return carry

    lax.fori_loop(0, n_chunks, chunk_inter, 0, unroll=GLA_UNROLL)

    for h in heads:
        o = _rms(o_scr[h], gg_ref[...])
        y_ref[0, :, vcols[h]] = (o * _silu(g_ref[0, :, vcols[h]].astype(F32))).astype(y_ref.dtype)


def _gla(proj, lr, wdf, bdf, wdb, bdb, g_gla, s0f, s0b, batch, seq, dk, dv, hp):
    heads = GLA_HEADS
    proj3 = proj.reshape(batch, seq, proj.shape[-1])
    lr3 = lr.reshape(batch, seq, LANES)
    kdim = heads * dk
    width = heads * dv
    bk, bv = hp * dk, hp * dv
    k_blk = kdim // bk
    v_blk = 2 * kdim // bv
    g_blk = (2 * kdim + width) // bv
    n_chunks = seq // GLA_CHUNK

    def s0_map(s0):
        if s0.shape[0] == batch:
            return lambda b, h: (b, h, 0, 0)
        return lambda b, h: (0, h, 0, 0)

    kern = functools.partial(_gla_kernel, seq=seq, dk=dk, dv=dv, hp=hp)
    return pl.pallas_call(
        kern,
        grid=(batch, heads // hp),
        in_specs=[pl.BlockSpec((1, seq, bk), lambda b, h: (b, 0, h)),
                  pl.BlockSpec((1, seq, bk), lambda b, h: (b, 0, k_blk + h)),
                  pl.BlockSpec((1, seq, bv), lambda b, h: (b, 0, v_blk + h)),
                  pl.BlockSpec((1, seq, bv), lambda b, h: (b, 0, g_blk + h)),
                  pl.BlockSpec((1, seq, LANES), lambda b, h: (b, 0, 0)),
                  pl.BlockSpec((hp, LANES, dk), lambda b, h: (h, 0, 0)),
                  pl.BlockSpec((1, bk), lambda b, h: (0, h)),
                  pl.BlockSpec((hp, LANES, dk), lambda b, h: (h, 0, 0)),
                  pl.BlockSpec((1, bk), lambda b, h: (0, h)),
                  pl.BlockSpec((1, dv), lambda b, h: (0, 0)),
                  pl.BlockSpec((1, hp, dk, dv), s0_map(s0f)),
                  pl.BlockSpec((1, hp, dk, dv), s0_map(s0b))],
        out_specs=[pl.BlockSpec((1, seq, bv), lambda b, h: (b, 0, h)),
                   pl.BlockSpec((1, hp, dk, dv), lambda b, h: (b, h, 0, 0)),
                   pl.BlockSpec((1, hp, dk, dv), lambda b, h: (b, h, 0, 0))],
        out_shape=[jax.ShapeDtypeStruct((batch, seq, width), BF16),
                   jax.ShapeDtypeStruct((batch, heads, dk, dv), F32),
                   jax.ShapeDtypeStruct((batch, heads, dk, dv), F32)],
        scratch_shapes=[pltpu.VMEM((hp, seq, dk), F32), pltpu.VMEM((hp, seq, dk), F32),
                        pltpu.VMEM((hp, seq, dk), BF16), pltpu.VMEM((hp, seq, dk), BF16),
                        pltpu.VMEM((hp, seq, dv), F32),
                        pltpu.VMEM((hp, n_chunks, dv, dk), F32), pltpu.VMEM((hp, n_chunks, dv, dk), F32),
                        pltpu.VMEM((hp, n_chunks, 8, dk), F32), pltpu.VMEM((hp, n_chunks, 8, dk), F32)],
        compiler_params=_cparams(("arbitrary", "arbitrary")),
        name="gla",
    )(proj3, proj3, proj3, proj3, lr3, wdf, bdf, wdb, bdb, g_gla, s0f, s0b)


def _filter_kernel(pe_ref, w1_ref, b1_ref, w2_ref, b2_ref, fr_ref, w3_ref, dl_ref, ad_ref, hl_ref, hm_ref, mlp_scr,
                   *, seq):
    @pl.when(pl.program_id(0) == 0)
    def _():
        fr = fr_ref[...]
        h1 = jnp.sin(fr * (_dot_hp(pe_ref[...], w1_ref[...]) + b1_ref[...]))
        mlp_scr[...] = jnp.sin(fr * (_dot_hp(h1, w2_ref[...]) + b2_ref[...]))

    h2 = mlp_scr[...]
    dec = jnp.exp(-pe_ref[:, 0:1] * dl_ref[...])
    row = lax.broadcasted_iota(jnp.int32, (seq, 1), 0)
    alt = jnp.where(row % 2 == 0, 1.0, -1.0)
    phase = row % 4
    cos_half = jnp.where(phase == 0, 1.0, jnp.where(phase == 2, -1.0, 0.0))
    sin_half = jnp.where(phase == 1, 1.0, jnp.where(phase == 3, -1.0, 0.0))
    for o in range(HY_ORDER):
        ff = _dot_hp(h2, w3_ref[o]) * dec
        fb = _dot_hp(h2, w3_ref[HY_ORDER + o]) * dec
        nrm = jnp.sum(jnp.abs(ff), axis=0, keepdims=True) + jnp.sum(jnp.abs(fb), axis=0, keepdims=True)
        ff = ff / nrm
        fb = jnp.where(row == 0, 0.0, fb / nrm)
        a = ff + fb
        d = fb - ff
        ad_ref[o, 0] = a.astype(ad_ref.dtype)
        ad_ref[o, 1] = d.astype(ad_ref.dtype)
        ad_ref[o, 2] = (alt * a).astype(ad_ref.dtype)
        ad_ref[o, 3] = (-alt * d).astype(ad_ref.dtype)
        hl_ref[o] = jnp.sum(alt * a, axis=0, keepdims=True)
        hm_ref[o] = jnp.concatenate([jnp.sum(cos_half * a, axis=0, keepdims=True),
                                     jnp.sum(sin_half * d, axis=0, keepdims=True)], axis=0) * (1.0 / seq)


def _hy_filter(pe, w1p, b1, w2, b2, freq, w3r, deltas, seq, width, tc):
    hid = w2.shape[0]
    kern = functools.partial(_filter_kernel, seq=seq)
    return pl.pallas_call(
        kern,
        grid=(width // tc,),
        in_specs=[pl.BlockSpec((seq, LANES), lambda c: (0, 0)),
                  pl.BlockSpec((LANES, hid), lambda c: (0, 0)),
                  pl.BlockSpec((1, hid), lambda c: (0, 0)),
                  pl.BlockSpec((hid, hid), lambda c: (0, 0)),
                  pl.BlockSpec((1, hid), lambda c: (0, 0)),
                  pl.BlockSpec((1, hid), lambda c: (0, 0)),
                  pl.BlockSpec((2 * HY_ORDER, hid, tc), lambda c: (0, 0, c)),
                  pl.BlockSpec((1, tc), lambda c: (0, c))],
        out_specs=[pl.BlockSpec((HY_ORDER, 4, seq, tc), lambda c: (0, 0, 0, c)),
                   pl.BlockSpec((HY_ORDER, 1, tc), lambda c: (0, 0, c)),
                   pl.BlockSpec((HY_ORDER, 2, tc), lambda c: (0, 0, c))],
        out_shape=[jax.ShapeDtypeStruct((HY_ORDER, 4, seq, width), BF16),
                   jax.ShapeDtypeStruct((HY_ORDER, 1, width), F32),
                   jax.ShapeDtypeStruct((HY_ORDER, 2, width), F32)],
        scratch_shapes=[pltpu.VMEM((seq, hid), F32)],
        compiler_params=_cparams(("arbitrary",)),
        name="hy_filter",
    )(pe, w1p, b1, w2, b2, freq, w3r, deltas)


def _spectrum_kernel(cs_ref, ad_ref, h_ref, *, seq, per_part, tr):
    r = pl.program_id(1)
    k = (r % per_part) * tr + lax.broadcasted_iota(jnp.int32, (tr, 1), 0)
    wgt = jnp.where(k == 0, 1.0, 2.0) * (0.5 / seq)
    h_ref[0] = wgt * _dot(cs_ref[0], ad_ref[0, 0])


def _hy_spectrum(cs3, ad, seq, part_rows, n_parts, width, tr, tc):
    per_part = part_rows // tr
    kern = functools.partial(_spectrum_kernel, seq=seq, per_part=per_part, tr=tr)
    return pl.pallas_call(
        kern,
        grid=(HY_ORDER, n_parts * per_part, width // tc),
        in_specs=[pl.BlockSpec((1, tr, seq), lambda o, r, c: ((r // per_part) % 2, r % per_part, 0)),
                  pl.BlockSpec((1, 1, seq, tc), lambda o, r, c: (o, r // per_part, 0, c))],
        out_specs=pl.BlockSpec((1, tr, tc), lambda o, r, c: (o, r, c)),
        out_shape=jax.ShapeDtypeStruct((HY_ORDER, n_parts * part_rows, width), F32),
        compiler_params=_cparams(("arbitrary", "arbitrary", "arbitrary")),
        name="hy_spectrum",
    )(cs3, ad)


def _short_conv_fn(seq, seg):
    row = lax.broadcasted_iota(jnp.int32, (seq, 1), 0)
    pos = row % seg

    def short_conv(u_ref, cw_ref, cb_ref):
        u = u_ref[0].astype(F32)
        prev = jnp.where(pos == 0, 0.0, pltpu.roll(u, 1, 0))
        nxt = jnp.where(pos == seg - 1, 0.0, pltpu.roll(u, seq - 1, 0))
        return prev * cw_ref[0:1, :] + u * cw_ref[1:2, :] + nxt * cw_ref[2:3, :] + cb_ref[...]

    return short_conv


def _hyconv_kernel(u0_ref, u1_ref, u2_ref, cw0_ref, cw1_ref, cw2_ref, cb0_ref, cb1_ref, cb2_ref,
                   cs_ref, h_ref, hl_ref, bias_ref, z_ref, y_scr, *, seq, seg):
    short_conv = _short_conv_fn(seq, seg)
    row = lax.broadcasted_iota(jnp.int32, (seq, 1), 0)
    alt = jnp.where(row % 2 == 0, 1.0, -1.0)
    gates = ((u1_ref, cw1_ref, cb1_ref), (u2_ref, cw2_ref, cb2_ref))
    z = short_conv(u0_ref, cw0_ref, cb0_ref)
    kb = min(seq, 512)
    for n in range(HY_ORDER):
        zb = z.astype(BF16)
        for r in range(0, seq, kb):
            xc = _dot(cs_ref[r:r + kb, :], zb)
            xs = _dot(cs_ref[seq + r:seq + r + kb, :], zb)
            hre = h_ref[n, r:r + kb, :]
            him = h_ref[n, seq + r:seq + r + kb, :]
            y_scr[r:r + kb, :] = (xc * hre + xs * him).astype(BF16)
            y_scr[seq + r:seq + r + kb, :] = (xs * hre - xc * him).astype(BF16)
        nyq = jnp.sum(alt * z, axis=0, keepdims=True) * (hl_ref[n] * (0.5 / seq))
        conv = _dot(cs_ref[:seq, :], y_scr[:seq, :]) + _dot(cs_ref[seq:, :], y_scr[seq:, :]) + alt * nyq
        z = short_conv(*gates[n]) * (conv + bias_ref[n] * z)
    z_ref[0] = z.astype(z_ref.dtype)


def _hyconv_split_kernel(u0_ref, u1_ref, u2_ref, cw0_ref, cw1_ref, cw2_ref, cb0_ref, cb1_ref, cb2_ref,
                         cs_ref, h_ref, hm_ref, tw_ref, bias_ref, z_ref, g_scr, z_scr, c_scr, *, seq, seg):
    m = seq // 2
    tc = z_ref.shape[2]
    n_lane_blocks = tc // LANES
    short_conv = _short_conv_fn(seq, seg)
    row = lax.broadcasted_iota(jnp.int32, (m, 1), 0)
    alt = jnp.where(row % 2 == 0, 1.0, -1.0)
    gates = ((u1_ref, cw1_ref, cb1_ref), (u2_ref, cw2_ref, cb2_ref))
    z = short_conv(u0_ref, cw0_ref, cb0_ref)
    kb = min(m, 512)
    for n in range(HY_ORDER):
        for j in range(n_lane_blocks):
            z_scr[j] = z[:, j * LANES:(j + 1) * LANES]
        ze = jnp.concatenate([z_scr[j, pl.ds(0, m, stride=2), :] for j in range(n_lane_blocks)], axis=1)
        zo = jnp.concatenate([z_scr[j, pl.ds(1, m, stride=2), :] for j in range(n_lane_blocks)], axis=1)
        e_mid = jnp.sum(alt * ze, axis=0, keepdims=True)
        o_mid = jnp.sum(alt * zo, axis=0, keepdims=True)
        zeb = ze.astype(BF16)
        zob = zo.astype(BF16)
        for r in range(0, m, kb):
            rows = slice(r, r + kb)
            srows = slice(m + r, m + r + kb)
            ec = _dot(cs_ref[rows, :], zeb)
            es = _dot(cs_ref[srows, :], zeb)
            oc = _dot(cs_ref[rows, :], zob)
            os_ = _dot(cs_ref[srows, :], zob)
            c = tw_ref[0, rows, :]
            s = tw_ref[1, rows, :]
            pc = c * oc - s * os_
            ps = c * os_ + s * oc
            xca, xsa = ec + pc, es + ps
            xcb, xsb = ec - pc, ps - es
            har = h_ref[n, rows, :]
            hai = h_ref[n, srows, :]
            hbr = h_ref[n, 2 * m + r:2 * m + r + kb, :]
            hbi = h_ref[n, 3 * m + r:3 * m + r + kb, :]
            yar = xca * har + xsa * hai
            yai = xca * hai - xsa * har
            ybr = xcb * hbr + xsb * hbi
            ybi = xcb * hbi - xsb * hbr
            dr = yar - ybr
            di = yai + ybi
            g_scr[0, rows, :] = (yar + ybr).astype(BF16)
            g_scr[0, srows, :] = (ybi - yai).astype(BF16)
            g_scr[1, rows, :] = (c * dr - s * di).astype(BF16)
            g_scr[1, srows, :] = (-(s * dr + c * di)).astype(BF16)
        hr = hm_ref[n, 0:1, :]
        hi = hm_ref[n, 1:2, :]
        ymr = e_mid * hr + o_mid * hi
        ymi = e_mid * hi - o_mid * hr
        y_even = _dot(cs_ref[:m, :], g_scr[0, :m, :]) + _dot(cs_ref[m:, :], g_scr[0, m:, :]) + alt * ymr
        y_odd = _dot(cs_ref[:m, :], g_scr[1, :m, :]) + _dot(cs_ref[m:, :], g_scr[1, m:, :]) - alt * ymi
        for j in range(n_lane_blocks):
            c_scr[j, pl.ds(0, m, stride=2), :] = y_even[:, j * LANES:(j + 1) * LANES]
            c_scr[j, pl.ds(1, m, stride=2), :] = y_odd[:, j * LANES:(j + 1) * LANES]
        conv = jnp.concatenate([c_scr[j] for j in range(n_lane_blocks)], axis=1)
        z = short_conv(*gates[n]) * (conv + bias_ref[n] * z)
    z_ref[0] = z.astype(z_ref.dtype)


def _hy_conv(proj, conv_w, conv_b, cs, hspec, hl, hm, tw, bias, batch, seq, seg, col0, width, tc, split):
    proj3 = proj.reshape(batch, seq, proj.shape[-1])
    blk0 = col0 // tc
    per = width // tc
    once = dict(pipeline_mode=pl.Buffered(1))

    def u_spec(p):
        return pl.BlockSpec((1, seq, tc), lambda c, b: (b, 0, blk0 + p * per + c))

    def w_spec(p, rows):
        return pl.BlockSpec((rows, tc), lambda c, b: (0, p * per + c))

    in_specs = [u_spec(0), u_spec(1), u_spec(2),
                w_spec(0, HY_SHORT), w_spec(1, HY_SHORT), w_spec(2, HY_SHORT),
                w_spec(0, 1), w_spec(1, 1), w_spec(2, 1),
                pl.BlockSpec(cs.shape, lambda c, b: (0, 0), **once),
                pl.BlockSpec((HY_ORDER, 2 * seq, tc), lambda c, b: (0, 0, c), **once)]
    bias_spec = pl.BlockSpec((HY_ORDER, 1, tc), lambda c, b: (0, 0, c))
    operands = [proj3, proj3, proj3, conv_w, conv_w, conv_w, conv_b, conv_b, conv_b, cs, hspec]
    if split:
        kern = functools.partial(_hyconv_split_kernel, seq=seq, seg=seg)
        in_specs += [pl.BlockSpec((HY_ORDER, 2, tc), lambda c, b: (0, 0, c)),
                     pl.BlockSpec((2, seq // 2, tc), lambda c, b: (0, 0, 0)), bias_spec]
        operands += [hm, tw, bias]
        scratch = [pltpu.VMEM((2, seq, tc), BF16), pltpu.VMEM((tc // LANES, seq, LANES), F32),
                   pltpu.VMEM((tc // LANES, seq, LANES), F32)]
    else:
        kern = functools.partial(_hyconv_kernel, seq=seq, seg=seg)
        in_specs += [pl.BlockSpec((HY_ORDER, 1, tc), lambda c, b: (0, 0, c)), bias_spec]
        operands += [hl, bias]
        scratch = [pltpu.VMEM((2 * seq, tc), BF16)]
    return pl.pallas_call(
        kern,
        grid=(per, batch),
        in_specs=in_specs,
        out_specs=pl.BlockSpec((1, seq, tc), lambda c, b: (b, 0, c)),
        out_shape=jax.ShapeDtypeStruct((batch, seq, width), BF16),
        scratch_shapes=scratch,
        compiler_params=_cparams(("arbitrary", "arbitrary")),
        name="hy_conv",
    )(*operands)


def _mixout_kernel(*refs, gla_width, n_own, aliased):
    (yg_ref, zh_ref, x_ref, mod_ref, ghy_ref, gpost_ref, gpre_ref, wo_ref, wrh_ref, wrl_ref, br_ref,
     cnt_in_ref) = refs[:12]
    x1_ref, hg_ref, route_ref, cnt_ref, cnt_scr = refs[12 + int(aliased):]
    d = x_ref.shape[1]
    tm = x_ref.shape[0] // MIX_SUBTILES
    i = pl.program_id(0)

    @pl.when(i == 0)
    def _():
        cnt_scr[...] = cnt_in_ref[...]

    @pl.when(i < n_own)
    def _():
        for s in range(MIX_SUBTILES):
            _mixout_rows(slice(s * tm, (s + 1) * tm), tm, d, gla_width, yg_ref, zh_ref, x_ref, mod_ref, ghy_ref,
                         gpost_ref, gpre_ref, wo_ref, wrh_ref, wrl_ref, br_ref, x1_ref, hg_ref, route_ref, cnt_scr)

    @pl.when(i >= n_own)
    def _():
        hg_ref[...] = jnp.zeros_like(hg_ref)

    cnt_ref[...] = cnt_scr[...]


def _mixout_rows(rows, tm, d, gla_width, yg_ref, zh_ref, x_ref, mod_ref, ghy_ref, gpost_ref, gpre_ref, wo_ref,
                 wrh_ref, wrl_ref, br_ref, x1_ref, hg_ref, route_ref, cnt_scr):
    yh = _rms(zh_ref[rows, :].astype(F32), ghy_ref[...]).astype(BF16)
    y = _dot(yg_ref[rows, :], wo_ref[:gla_width, :]) + _dot(yh, wo_ref[gla_width:, :])
    x1 = x_ref[rows, :] + mod_ref[0, 2:3, :] * _rms(y, gpost_ref[...])
    x1_ref[rows, :] = x1
    h2 = _rms(x1, gpre_ref[...]) * (1.0 + mod_ref[0, 4:5, :]) + mod_ref[0, 3:4, :]
    hh, hl = _split2(h2)
    hg_ref[rows, :d] = h2
    logits = _dot(hh, wrh_ref[...]) + (_dot(hh, wrl_ref[...]) + _dot(hl, wrh_ref[...])) + br_ref[...]

    lane = lax.broadcasted_iota(jnp.int32, logits.shape, 1).astype(F32)
    neg = -jnp.inf
    is_grp = (lane >= N_EXPERTS) & (lane < N_EXPERTS + N_GROUPS)
    m = jnp.max(jnp.where(is_grp, logits, neg), axis=1, keepdims=True)
    p_grp = 1.0 / jnp.sum(jnp.where(is_grp, jnp.exp(logits - m), 0.0), axis=1, keepdims=True)
    grp = jnp.min(jnp.where(is_grp & (logits == m), lane - N_EXPERTS, 1e9), axis=1, keepdims=True)
    sel = (lane >= grp * EXP_PER_GROUP) & (lane < (grp + 1.0) * EXP_PER_GROUP)
    me = jnp.max(jnp.where(sel, logits, neg), axis=1, keepdims=True)
    pe = jnp.where(sel, jnp.exp(logits - me), -1.0)
    v1 = jnp.max(pe, axis=1, keepdims=True)
    i1 = jnp.min(jnp.where(pe == v1, lane, 1e9), axis=1, keepdims=True)
    pe2 = jnp.where(lane == i1, -1.0, pe)
    v2 = jnp.max(pe2, axis=1, keepdims=True)
    i2 = jnp.min(jnp.where(pe2 == v2, lane, 1e9), axis=1, keepdims=True)
    den = v1 + v2
    gates = jnp.where(lane == i1, v1 / den, jnp.where(lane == i2, v2 / den, 0.0)) * p_grp

    lo = jnp.minimum(i1, i2) - grp * EXP_PER_GROUP
    hi = jnp.maximum(i1, i2) - grp * EXP_PER_GROUP
    bucket = grp * PAIRS_PER_GROUP + lo * (2 * EXP_PER_GROUP - 1 - lo) * 0.5 + (hi - lo - 1.0)
    onehot = lane == bucket
    r_i = lax.broadcasted_iota(jnp.int32, (tm, tm), 0)
    c_i = lax.broadcasted_iota(jnp.int32, (tm, tm), 1)
    earlier = _dot((r_i > c_i).astype(BF16), onehot.astype(BF16)) + cnt_scr[...]
    rank = jnp.sum(jnp.where(onehot, earlier, 0.0), axis=1, keepdims=True)
    cnt_scr[...] += jnp.sum(onehot.astype(F32), axis=0, keepdims=True)
    route = jnp.where(lane == ROUTE_BUCKET_LANE, bucket, jnp.where(lane == ROUTE_RANK_LANE, rank, gates))
    hg_ref[rows, d:] = route
    route_ref[:, rows] = route.T[ROUTE_BUCKET_LANE:ROUTE_BUCKET_LANE + 8, :]


def _mix_out(yg, zh, x, mod, mod_row, g_hy, g_post, g_pre, w_out, wr_hi, wr_lo, b_r, cnt_in, hg_all, row0, n_total, tm):
    n, d = x.shape
    gw = yg.shape[1]
    hw = zh.shape[1]
    aliased = hg_all is not None
    n_own = n // tm
    n_steps = n_own if aliased else n_total // tm
    assert row0 % tm == 0 and n_total % tm == 0 and (row0 + n == n_total if aliased else row0 == 0)
    kern = functools.partial(_mixout_kernel, gla_width=gw, n_own=n_own, aliased=aliased)
    row = lambda i: (jnp.minimum(i, n_own - 1), 0)
    fixed = lambda i: (0, 0)
    in_specs = [pl.BlockSpec((tm, gw), row),
                pl.BlockSpec((tm, hw), row),
                pl.BlockSpec((tm, d), row),
                pl.BlockSpec((1, 6, d), lambda i: (mod_row(jnp.minimum(i, n_own - 1) * tm), 0, 0)),
                pl.BlockSpec((1, hw), fixed),
                pl.BlockSpec((1, d), fixed),
                pl.BlockSpec((1, d), fixed),
                pl.BlockSpec((gw + hw, d), fixed),
                pl.BlockSpec((d, LANES), fixed),
                pl.BlockSpec((d, LANES), fixed),
                pl.BlockSpec((1, LANES), fixed),
                pl.BlockSpec((1, LANES), fixed)]
    operands = [yg, zh, x, mod, g_hy, g_post, g_pre, w_out, wr_hi, wr_lo, b_r, cnt_in]
    if aliased:
        in_specs.append(pl.BlockSpec(memory_space=pl.ANY))
        operands.append(hg_all)
    return pl.pallas_call(
        kern,
        grid=(n_steps,),
        in_specs=in_specs,
        out_specs=[pl.BlockSpec((tm, d), row),
                   pl.BlockSpec((tm, d + LANES), lambda i: (row0 // tm + i, 0)),
                   pl.BlockSpec((8, tm), lambda i: (0, jnp.minimum(i, n_own - 1))),
                   pl.BlockSpec((1, LANES), fixed)],
        out_shape=[jax.ShapeDtypeStruct((n, d), F32),
                   jax.ShapeDtypeStruct((n_total, d + LANES), F32),
                   jax.ShapeDtypeStruct((8, n), F32),
                   jax.ShapeDtypeStruct((1, LANES), F32)],
        scratch_shapes=[pltpu.VMEM((1, LANES), F32)],
        input_output_aliases={len(operands) - 1: 1} if aliased else {},
        compiler_params=_cparams(("arbitrary",)),
        name="mix_out",
    )(*operands)


def _row_gather_copy(src_hbm, row, buf, slot, r, sem):
    return pltpu.make_async_copy(src_hbm.at[pl.ds(row, 1), :], buf.at[slot, pl.ds(r, 1), :], sem.at[slot])


def _start_row_gather(idx_ref, base, src_hbm, buf, slot, sem, tm):
    def body(r, carry):
        _row_gather_copy(src_hbm, idx_ref[base + r], buf, slot, r, sem).start()
        return carry

    lax.fori_loop(0, tm, body, 0, unroll=8)


def _start_row_gather_inline(idx_ref, base, src_hbm, buf, slot, sem, tm):
    for r in range(tm):
        _row_gather_copy(src_hbm, idx_ref[base + r], buf, slot, r, sem).start()


def _wait_row_gather(src_hbm, buf, slot, sem, tm):
    pltpu.make_async_copy(src_hbm.at[pl.ds(0, tm), :], buf.at[slot], sem.at[slot]).wait()


def _moe_kernel(src_ref, ea_ref, eb_ref, valid_ref, hg_hbm,
                wga_ref, wua_ref, wda_ref, wgb_ref, wub_ref, wdb_ref, y_ref, buf, sem, *, tm, d):
    t = pl.program_id(0)
    slot = t % 2

    @pl.when(t == 0)
    def _():
        _start_row_gather(src_ref, 0, hg_hbm, buf, 0, sem, tm)

    @pl.when(valid_ref[t] == 1)
    def _():
        _wait_row_gather(hg_hbm, buf, slot, sem, tm)
        _start_row_gather_inline(src_ref, (t + 1) * tm, hg_hbm, buf, 1 - slot, sem, tm)
        h = buf[slot, :, :d].astype(BF16)
        gates = buf[slot, :, d:]
        lane = lax.broadcasted_iota(jnp.int32, gates.shape, 1)

        def expert(wg_ref, wu_ref, wd_ref, e):
            he = (_silu(_dot(h, wg_ref[0])) * _dot(h, wu_ref[0])).astype(BF16)
            gate = jnp.sum(jnp.where(lane == e, gates, 0.0), axis=1, keepdims=True)
            return gate * _dot(he, wd_ref[0])

        y_ref[...] = expert(wga_ref, wua_ref, wda_ref, ea_ref[t]) + expert(wgb_ref, wub_ref, wdb_ref, eb_ref[t])

    @pl.when(valid_ref[t] == 0)
    def _():
        @pl.when(valid_ref[jnp.maximum(t - 1, 0)] == 1)
        def _():
            _wait_row_gather(hg_hbm, buf, slot, sem, tm)

        y_ref[...] = jnp.zeros_like(y_ref)


def _moe(hg, src, ea, eb, valid, wg, wu, wd, n_tiles, tm):
    d = hg.shape[1] - LANES
    _, _, de = wg.shape
    kern = functools.partial(_moe_kernel, tm=tm, d=d)
    wa = lambda t, src, ea, eb, valid: (ea[t], 0, 0)
    wb = lambda t, src, ea, eb, valid: (eb[t], 0, 0)
    return pl.pallas_call(
        kern,
        grid_spec=pltpu.PrefetchScalarGridSpec(
            num_scalar_prefetch=4,
            grid=(n_tiles,),
            in_specs=[pl.BlockSpec(memory_space=pl.ANY),
                      pl.BlockSpec((1, d, de), wa), pl.BlockSpec((1, d, de), wa), pl.BlockSpec((1, de, d), wa),
                      pl.BlockSpec((1, d, de), wb), pl.BlockSpec((1, d, de), wb), pl.BlockSpec((1, de, d), wb)],
            out_specs=pl.BlockSpec((tm, d), lambda t, src, ea, eb, valid: (t, 0)),
            scratch_shapes=[pltpu.VMEM((2, tm, d + LANES), F32), pltpu.SemaphoreType.DMA((2,))]),
        out_shape=jax.ShapeDtypeStruct((n_tiles * tm, d), F32),
        compiler_params=_cparams(("arbitrary",)),
        name="moe",
    )(src, ea, eb, valid, hg, wg, wu, wd, wg, wu, wd)


def _ffn_out_kernel(pos_ref, y_hbm, x1_ref, mod_ref, gpost_ref, o_ref, buf, sem, *, tm):
    i = pl.program_id(0)
    n_i = pl.num_programs(0)
    slot = i % 2

    @pl.when(i == 0)
    def _():
        _start_row_gather(pos_ref, 0, y_hbm, buf, 0, sem, tm)

    _wait_row_gather(y_hbm, buf, slot, sem, tm)

    @pl.when(i + 1 < n_i)
    def _():
        _start_row_gather_inline(pos_ref, (i + 1) * tm, y_hbm, buf, 1 - slot, sem, tm)
        o_ref[...] = x1_ref[...] + mod_ref[0, 5:6, :] * _rms(buf[slot], gpost_ref[...])

    @pl.when(i + 1 == n_i)
    def _():
        o_ref[...] = x1_ref[...] + mod_ref[0, 5:6, :] * _rms(buf[slot], gpost_ref[...])


def _ffn_out(y_sorted, pos, x1, mod, mod_row, g_post, tm):
    n, d = x1.shape
    kern = functools.partial(_ffn_out_kernel, tm=tm)
    return pl.pallas_call(
        kern,
        grid_spec=pltpu.PrefetchScalarGridSpec(
            num_scalar_prefetch=1,
            grid=(n // tm,),
            in_specs=[pl.BlockSpec(memory_space=pl.ANY),
                      pl.BlockSpec((tm, d), lambda i, pos: (i, 0)),
                      pl.BlockSpec((1, 6, d), lambda i, pos: (mod_row(i * tm), 0, 0)),
                      pl.BlockSpec((1, d), lambda i, pos: (0, 0))],
            out_specs=pl.BlockSpec((tm, d), lambda i, pos: (i, 0)),
            scratch_shapes=[pltpu.VMEM((2, tm, d), F32), pltpu.SemaphoreType.DMA((2,))]),
        out_shape=jax.ShapeDtypeStruct((n, d), F32),
        compiler_params=_cparams(("arbitrary",)),
        name="ffn_out",
    )(pos, y_sorted, x1, mod, g_post)


def _route_tables(route, counts, n_tiles, tm):
    n = route.shape[1]
    n_buckets = N_GROUPS * PAIRS_PER_GROUP
    cnt = counts[0, :n_buckets].astype(jnp.int32)
    padded = (cnt + tm - 1) // tm * tm
    ends = jnp.cumsum(padded)
    starts = ends - padded
    bucket = route[0].astype(jnp.int32)
    rank = route[ROUTE_RANK_LANE - ROUTE_BUCKET_LANE].astype(jnp.int32)
    pos = starts[bucket] + rank
    src = (jnp.arange(n_tiles * tm, dtype=jnp.int32) % n).at[pos].set(jnp.arange(n, dtype=jnp.int32))
    n_valid = ends[-1] // tm
    tile = jnp.arange(n_tiles, dtype=jnp.int32)
    used = jnp.minimum(tile, n_valid - 1)
    tile_bucket = jnp.sum((ends[None, :] <= (used * tm)[:, None]).astype(jnp.int32), axis=1)
    grp = tile_bucket // PAIRS_PER_GROUP
    pair = tile_bucket % PAIRS_PER_GROUP
    pair_lo = jnp.array([a for a in range(EXP_PER_GROUP) for b in range(a + 1, EXP_PER_GROUP)], jnp.int32)
    pair_hi = jnp.array([b for a in range(EXP_PER_GROUP) for b in range(a + 1, EXP_PER_GROUP)], jnp.int32)
    ea = grp * EXP_PER_GROUP + pair_lo[pair]
    eb = grp * EXP_PER_GROUP + pair_hi[pair]
    valid = (tile < n_valid).astype(jnp.int32)
    return pos, src, ea, eb, valid


def _dft_kernel(cs_ref, *, seq, tr):
    k = pl.program_id(0) * tr + lax.broadcasted_iota(jnp.int32, (tr, LANES), 0)
    j = lax.broadcasted_iota(jnp.int32, (tr, LANES), 1)
    period = 2 * seq

    def table(step):
        ang = ((k * j * step) % period).astype(F32) * (math.pi / seq)
        return jnp.cos(ang), jnp.sin(ang)

    c0, s0 = table(1)
    c1, s1 = table(LANES)
    for t1 in range(seq // LANES):
        cols = slice(t1 * LANES, (t1 + 1) * LANES)
        ca = c1[:, t1:t1 + 1]
        sa = s1[:, t1:t1 + 1]
        cs_ref[0, :, cols] = (ca * c0 - sa * s0).astype(cs_ref.dtype)
        cs_ref[1, :, cols] = (sa * c0 + ca * s0).astype(cs_ref.dtype)


def _dft_cos_sin(seq, n_rows, tr):
    kern = functools.partial(_dft_kernel, seq=seq, tr=tr)
    return pl.pallas_call(
        kern,
        grid=(n_rows // tr,),
        out_specs=pl.BlockSpec((2, tr, seq), lambda r: (0, r, 0)),
        out_shape=jax.ShapeDtypeStruct((2, n_rows, seq), BF16),
        compiler_params=_cparams(("arbitrary",)),
        name="dft_matrix",
    )()


def _positional_features(seq):
    t = jnp.arange(seq, dtype=F32)
    t01 = t / max(seq - 1, 1)
    ang = 2.0 * math.pi * t / seq
    bands = jnp.linspace(1e-4, HY_BANDS - 1, HY_BANDS, dtype=F32)
    pe = jnp.concatenate([t01[:, None], jnp.cos(ang[:, None] * bands), -jnp.sin(ang[:, None] * bands)], axis=-1)
    return jnp.pad(pe, ((0, 0), (0, LANES - pe.shape[1])))


def _mixer_and_router(x3, mod, mod_row, s0f, s0b, n_seg, p, tiles, cnt_in, hg_all, row0, n_total, w_out, cast_srcs):
    batch, seq, d = x3.shape
    x = x3.reshape(batch * seq, d)
    dk, dv = p["dk"], p["dv"]
    hy_width = p["hy_width"]

    proj, lr, *casted = _premix_proj(x, mod, mod_row, p["g_pre_mix"], p["w_main"], p["wlr_hi"], p["wlr_lo"],
                                     tiles["tm_proj"], tiles["tn_proj"], cast_srcs)
    if w_out is None:
        w_out = casted[0]
    y_gla, s_f, s_b = _gla(proj, lr, p["wdf"], p["bdf"], p["wdb"], p["bdb"], p["g_gla"], s0f, s0b,
                           batch, seq, dk, dv, tiles["gla_heads_per_step"])

    ad, hl, hm = _hy_filter(_positional_features(seq), p["hy_w1"], p["hy_b1"], p["hy_w2"], p["hy_b2"], p["hy_freq"],
                            p["hy_w3"], p["hy_deltas"], seq, hy_width, tiles["tc_filter"])
    tr, tc_conv = tiles["tr_spec"], tiles["tc_conv"]
    if tiles["hy_split"]:
        half = seq // 2
        hspec = _hy_spectrum(_dft_cos_sin(seq, half, tr), ad, seq, half, 4, hy_width, tr, tiles["tc_spec"])
        cs = _dft_cos_sin(half, half, tr).reshape(seq, half)
        ang = jnp.arange(half, dtype=F32) * (math.pi / seq)
        tw = jnp.broadcast_to(jnp.stack([jnp.cos(ang), jnp.sin(ang)])[:, :, None], (2, half, tc_conv))
    else:
        cs3 = _dft_cos_sin(seq, seq, tr)
        hspec = _hy_spectrum(cs3, ad, seq, seq, 2, hy_width, tr, tiles["tc_spec"])
        cs = cs3.reshape(2 * seq, seq)
        tw = None
    z_hy = _hy_conv(proj, p["hy_conv_w"], p["hy_conv_b"], cs, hspec, hl, hm, tw, p["hy_bias"], batch, seq,
                    seq // n_seg, p["hy_col0"], hy_width, tc_conv, tiles["hy_split"])

    x1, hg_all, route, counts = _mix_out(y_gla.reshape(batch * seq, -1), z_hy.reshape(batch * seq, -1), x, mod,
                                         mod_row, p["g_hy"], p["g_post_mix"], p["g_pre_ffn"], w_out, p["wr_hi"],
                                         p["wr_lo"], p["b_r"], cnt_in, hg_all, row0, n_total, tiles["tm_mix"])
    return x1, hg_all, route, counts, s_f, s_b, casted


def kernel(x_prompt, x_sample, c, state_gla_fwd, state_gla_bwd, c_ctx, w_ada, b_ada, g_pre_mix, g_post_mix, g_pre_ffn, g_post_ffn, w_in, w_dec_f, b_dec_f, w_dec_b, b_dec_b, g_gla, hy_conv_w, hy_conv_b, hy_w1, hy_b1, hy_w2, hy_b2, hy_w3, hy_freq, hy_bias, g_hy, w_out, w_router_grp, b_router_grp, w_router_exp, b_router_exp, w_exp_gate, w_exp_up, w_exp_down):
    depth = w_ada.shape[0]
    assert depth == 1
    l = 0
    d = x_prompt.shape[-1]
    dec_batch = x_sample.shape[0]
    heads = GLA_HEADS
    dk, dv = state_gla_fwd.shape[-2:]
    kdim = heads * dk
    gla_width = heads * dv
    hy_width = g_hy.shape[-1]
    hid = hy_w2.shape[-1]

    cond = jnp.concatenate([c_ctx[None, :], c], axis=0)
    cond = jnp.pad(cond, ((0, -cond.shape[0] % 8), (0, 0)))
    mod = _ada_mod(cond, w_ada[l], b_ada[l]).reshape(cond.shape[0], 6, d)

    n_main = 2 * kdim + 2 * gla_width
    w_main, wlr_hi, wlr_lo = _repack_w_in(w_in, l, n_main, 2 * GLA_RANK, 256)

    def dec_weight(w, first_row):
        wh = w.reshape(GLA_RANK, heads, dk).transpose(1, 0, 2)
        return jnp.pad(wh, ((0, 0), (first_row, LANES - GLA_RANK - first_row), (0, 0)))

    deltas = jnp.abs(jnp.linspace(math.log(HY_TARGET) / HY_SLOW_PCT, math.log(HY_TARGET) / HY_FAST_PCT, hy_width,
                                  dtype=F32))
    w_r = jnp.pad(jnp.concatenate([w_router_exp[l], w_router_grp[l]], axis=1),
                  ((0, 0), (0, LANES - N_EXPERTS - N_GROUPS)))
    wr_hi, wr_lo = _split2(w_r)
    b_r = jnp.pad(jnp.concatenate([b_router_exp[l], b_router_grp[l]]), (0, LANES - N_EXPERTS - N_GROUPS))

    p = dict(
        dk=dk, dv=dv, hy_width=hy_width, hy_col0=n_main,
        g_pre_mix=g_pre_mix[l][None, :], g_post_mix=g_post_mix[l][None, :],
        g_pre_ffn=g_pre_ffn[l][None, :], g_post_ffn=g_post_ffn[l][None, :],
        w_main=w_main, wlr_hi=wlr_hi, wlr_lo=wlr_lo,
        wdf=dec_weight(w_dec_f[l], 0), bdf=b_dec_f[l][None, :],
        wdb=dec_weight(w_dec_b[l], GLA_RANK), bdb=b_dec_b[l][None, :],
        g_gla=g_gla[l][None, :],
        hy_conv_w=hy_conv_w[l], hy_conv_b=hy_conv_b[l][None, :],
        hy_w1=jnp.pad(hy_w1[l], ((0, LANES - hy_w1.shape[1]), (0, 0))), hy_b1=hy_b1[l][None, :],
        hy_w2=hy_w2[l], hy_b2=hy_b2[l][None, :], hy_freq=hy_freq[l][None, :],
        hy_w3=hy_w3[l].reshape(hid, 2 * HY_ORDER, hy_width).transpose(1, 0, 2), hy_deltas=deltas[None, :],
        hy_bias=hy_bias[l][:, None, :], g_hy=g_hy[l][None, :],
        wr_hi=wr_hi, wr_lo=wr_lo, b_r=b_r[None, :],
    )

    zero_state = jnp.zeros((1, heads, dk, dv), F32)
    dec_seq = x_sample.shape[1]
    tiles_p = dict(tm_proj=512, tn_proj=2048, tc_filter=256, tr_spec=256, tc_spec=512, tc_conv=512, tm_mix=512,
                   hy_split=False, gla_heads_per_step=4)
    tiles_s = dict(tm_proj=512, tn_proj=2048, tc_filter=256, tr_spec=512, tc_spec=1024, tc_conv=256, tm_mix=512,
                   hy_split=True, gla_heads_per_step=2)
    tm_moe = 256
    tm_out = 256

    n_p = x_prompt.shape[0] * x_prompt.shape[1]
    n_s = dec_batch * dec_seq
    n_all = n_p + n_s
    mod_row_p = lambda r: 0
    mod_row_s = lambda r: 1 + r // dec_seq
    n_exp, _, d_exp = w_exp_gate.shape[1:]
    x1_p, hg_all, route_p, counts, s_f, s_b, (w_out_b, w_gate_b, w_up_b) = _mixer_and_router(
        x_prompt, mod, mod_row_p, zero_state, zero_state, 1, p, tiles_p, jnp.zeros((1, LANES), F32), None, 0, n_all,
        None, [w_out[l], w_exp_gate[l].reshape(n_exp * d, d_exp), w_exp_up[l].reshape(n_exp * d, d_exp)])
    x1_s, hg_all, route_s, counts, _, _, (w_down_b,) = _mixer_and_router(
        x_sample, mod, mod_row_s, state_gla_fwd[:, l], state_gla_bwd[:, l], dec_seq // GRID_W, p, tiles_s,
        counts, hg_all, n_p, n_all, w_out_b, [w_exp_down[l].reshape(n_exp * d_exp, d)])

    n_tiles = n_all // tm_moe + N_GROUPS * PAIRS_PER_GROUP
    pos, src, ea, eb, valid = _route_tables(jnp.concatenate([route_p, route_s], axis=1), counts, n_tiles, tm_moe)
    y_sorted = _moe(hg_all, src, ea, eb, valid, w_gate_b.reshape(n_exp, d, d_exp), w_up_b.reshape(n_exp, d, d_exp),
                    w_down_b.reshape(n_exp, d_exp, d), n_tiles, tm_moe)
    y_p = _ffn_out(y_sorted, pos[:n_p], x1_p, mod, mod_row_p, p["g_post_ffn"], tm_out)
    y_s = _ffn_out(y_sorted, pos[n_p:], x1_s, mod, mod_row_s, p["g_post_ffn"], tm_out)
    return (y_p.reshape(x_prompt.shape), y_s.reshape(x_sample.shape),
            s_f[:, None].astype(x_prompt.dtype), s_b[:, None].astype(x_prompt.dtype))
```

```python
import functools
import math

import jax
import jax.numpy as jnp
from jax import lax
from jax.experimental import pallas as pl
from jax.experimental.pallas import tpu as pltpu

F32 = jnp.float32
BF16 = jnp.bfloat16

GRID_W = 64
GLA_HEADS = 4
GLA_RANK = 16
GLA_TAU = 16.0
GLA_CHUNK = 64
GLA_UNROLL = 4
MIX_SUBTILES = 2
HY_ORDER = 2
HY_SHORT = 3
HY_BANDS = 16
HY_TARGET = 1e-2
HY_FAST_PCT = 0.3
HY_SLOW_PCT = 1.5
N_GROUPS = 4
EXP_PER_GROUP = 4
N_EXPERTS = N_GROUPS * EXP_PER_GROUP
PAIRS_PER_GROUP = EXP_PER_GROUP * (EXP_PER_GROUP - 1) // 2
ROUTE_BUCKET_LANE = N_EXPERTS
ROUTE_RANK_LANE = N_EXPERTS + 1
EPS = 1e-6

LANES = 128
VMEM_LIMIT = 56 << 20


def _cparams(sem):
    return pltpu.CompilerParams(dimension_semantics=sem, vmem_limit_bytes=VMEM_LIMIT)


def _dot(a, b):
    return jnp.dot(a, b, preferred_element_type=F32)


def _dot_nt(a, b):
    return lax.dot_general(a, b, (((1,), (1,)), ((), ())), preferred_element_type=F32)


def _dot_tn(a, b):
    return lax.dot_general(a, b, (((0,), (0,)), ((), ())), preferred_element_type=F32)


def _split2(x):
    hi = x.astype(BF16)
    lo = (x - hi.astype(F32)).astype(BF16)
    return hi, lo


def _dot_hp(a, b):
    ah, al = _split2(a)
    bh, bl = _split2(b)
    return _dot(ah, bh) + (_dot(ah, bl) + _dot(al, bh))


def _dot_exact_lhs(t, x):
    hi, lo = _split2(x)
    return _dot(t, hi) + _dot(t, lo)


def _rms(x, g):
    return x * lax.rsqrt(jnp.mean(x * x, axis=-1, keepdims=True) + EPS) * g


def _silu(x):
    return x / (1.0 + jnp.exp(-x))


def _ada_kernel(c_ref, w_ref, b_ref, o_ref):
    s = _silu(c_ref[...]).astype(BF16)
    o_ref[...] = _dot(s, w_ref[...].astype(BF16)) + b_ref[...]


def _ada_mod(cond, w_ada, b_ada):
    rows, d = cond.shape
    n = w_ada.shape[1]
    tn = 1024
    return pl.pallas_call(
        _ada_kernel,
        grid=(n // tn,),
        in_specs=[pl.BlockSpec((rows, d), lambda j: (0, 0)),
                  pl.BlockSpec((d, tn), lambda j: (0, j)),
                  pl.BlockSpec((1, tn), lambda j: (0, j))],
        out_specs=pl.BlockSpec((rows, tn), lambda j: (0, j)),
        out_shape=jax.ShapeDtypeStruct((rows, n), F32),
        compiler_params=_cparams(("arbitrary",)),
        name="ada_mod",
    )(cond, w_ada, b_ada.reshape(1, n))


def _repack_kernel(wt_hbm, main_ref, lrh_ref, lrl_ref, buf, lr_buf, sem, lr_sem, *, layer, n_main, n_lr, tb):
    i = pl.program_id(0)
    n_i = pl.num_programs(0)
    slot = i % 2
    d = main_ref.shape[0]

    def block_copy(j, s):
        first = j * tb + jnp.where(j * tb >= n_main, n_lr, 0)
        return pltpu.make_async_copy(wt_hbm.at[layer, pl.ds(first, tb), :], buf.at[s], sem.at[s])

    lr_copy = pltpu.make_async_copy(wt_hbm.at[layer, pl.ds(n_main, n_lr), :], lr_buf, lr_sem)

    @pl.when(i == 0)
    def _():
        block_copy(0, 0).start()
        lr_copy.start()

    @pl.when(i + 1 < n_i)
    def _():
        block_copy(i + 1, 1 - slot).start()

    block_copy(i, slot).wait()
    main_ref[...] = buf[slot].T.astype(BF16)

    @pl.when(i == 0)
    def _():
        lr_copy.wait()
        lr = jnp.concatenate([lr_buf[...], jnp.zeros((LANES - n_lr, d), F32)], axis=0).T
        hi, lo = _split2(lr)
        lrh_ref[...] = hi
        lrl_ref[...] = lo


def _repack_w_in(w_in, layer, n_main, n_lr, tb):
    w_t = jnp.swapaxes(w_in, 1, 2)
    _, n_cols, d = w_t.shape
    n_out = n_cols - n_lr
    assert n_main % tb == 0 and n_out % tb == 0
    kern = functools.partial(_repack_kernel, layer=layer, n_main=n_main, n_lr=n_lr, tb=tb)
    return pl.pallas_call(
        kern,
        grid=(n_out // tb,),
        in_specs=[pl.BlockSpec(memory_space=pl.ANY)],
        out_specs=[pl.BlockSpec((d, tb), lambda i: (0, i)),
                   pl.BlockSpec((d, LANES), lambda i: (0, 0)),
                   pl.BlockSpec((d, LANES), lambda i: (0, 0))],
        out_shape=[jax.ShapeDtypeStruct((d, n_out), BF16),
                   jax.ShapeDtypeStruct((d, LANES), BF16),
                   jax.ShapeDtypeStruct((d, LANES), BF16)],
        scratch_shapes=[pltpu.VMEM((2, tb, d), F32), pltpu.VMEM((n_lr, d), F32),
                        pltpu.SemaphoreType.DMA((2,)), pltpu.SemaphoreType.DMA(())],
        compiler_params=_cparams(("arbitrary",)),
        name="repack_w_in",
    )(w_t)


def _with_cast_riders(kernel_fn, n_in, n_out, n_cast):
    def kernel(*refs):
        ins = refs[:n_in]
        cast_in = refs[n_in:n_in + n_cast]
        outs = refs[n_in + n_cast:n_in + n_cast + n_out]
        cast_out = refs[n_in + n_cast + n_out:n_in + 2 * n_cast + n_out]
        kernel_fn(*ins, *outs, *refs[n_in + 2 * n_cast + n_out:])
        for src, dst in zip(cast_in, cast_out):
            dst[...] = src[...].astype(dst.dtype)

    return kernel


def _cast_rider_specs(cast_srcs, n_steps, step_of):
    n_slabs = 1 << (n_steps.bit_length() - 1)
    specs = []
    for w in cast_srcs:
        rows, cols = w.shape
        assert rows % n_slabs == 0
        specs.append(pl.BlockSpec((rows // n_slabs, cols),
                                  lambda *idx: (jnp.minimum(step_of(*idx), n_slabs - 1), 0)))
    return specs, [jax.ShapeDtypeStruct(w.shape, BF16) for w in cast_srcs]


def _premix_kernel(x_ref, mod_ref, g_ref, w_ref, wlh_ref, wll_ref, o_ref, lr_ref, h_scr):
    @pl.when(pl.program_id(1) == 0)
    def _():
        h = _rms(x_ref[...], g_ref[...]) * (1.0 + mod_ref[0, 1:2, :]) + mod_ref[0, 0:1, :]
        hh, hl = _split2(h)
        h_scr[...] = hh
        lr_ref[...] = _dot(hh, wlh_ref[...]) + (_dot(hh, wll_ref[...]) + _dot(hl, wlh_ref[...]))

    o_ref[...] = _dot(h_scr[...], w_ref[...]).astype(o_ref.dtype)


def _premix_proj(x, mod, mod_row, g, w_main, wlr_hi, wlr_lo, tm, tn, cast_srcs):
    n, d = x.shape
    nc = w_main.shape[1]
    n_j = nc // tn
    cast_specs, cast_shapes = _cast_rider_specs(cast_srcs, (n // tm) * n_j, lambda i, j: i * n_j + j)
    kern = _with_cast_riders(_premix_kernel, 6, 2, len(cast_srcs))
    return pl.pallas_call(
        kern,
        grid=(n // tm, n_j),
        in_specs=[pl.BlockSpec((tm, d), lambda i, j: (i, 0)),
                  pl.BlockSpec((1, 6, d), lambda i, j: (mod_row(i * tm), 0, 0)),
                  pl.BlockSpec((1, d), lambda i, j: (0, 0)),
                  pl.BlockSpec((d, tn), lambda i, j: (0, j)),
                  pl.BlockSpec((d, LANES), lambda i, j: (0, 0)),
                  pl.BlockSpec((d, LANES), lambda i, j: (0, 0))] + cast_specs,
        out_specs=[pl.BlockSpec((tm, tn), lambda i, j: (i, j)),
                   pl.BlockSpec((tm, LANES), lambda i, j: (i, 0))] + cast_specs,
        out_shape=[jax.ShapeDtypeStruct((n, nc), BF16), jax.ShapeDtypeStruct((n, LANES), F32)] + cast_shapes,
        scratch_shapes=[pltpu.VMEM((tm, d), BF16)],
        compiler_params=_cparams(("arbitrary", "arbitrary")),
        name="premix_proj",
    )(x, mod, g, w_main, wlr_hi, wlr_lo, *cast_srcs)


def _log_sigmoid(x):
    return jnp.minimum(x, 0.0) - jnp.log1p(jnp.exp(-jnp.abs(x)))


def _gla_kernel(q_ref, k_ref, v_ref, g_ref, lr_ref, wdf_ref, bdf_ref, wdb_ref, bdb_ref, gg_ref,
                s0f_ref, s0b_ref, y_ref, sf_ref, sb_ref,
                laf_scr, lab_scr, qf_scr, qb_scr, o_scr, uf_scr, ub_scr, df_scr, db_scr, *, seq, dk, dv, hp):
    c = GLA_CHUNK
    n_chunks = seq // c
    scale = dk ** -0.5
    heads = range(hp)
    kcols = [slice(h * dk, (h + 1) * dk) for h in heads]
    vcols = [slice(h * dv, (h + 1) * dv) for h in heads]

    lr = lr_ref[0]
    for h in heads:
        laf_scr[h] = _log_sigmoid(_dot_hp(lr, wdf_ref[h]) + bdf_ref[:, kcols[h]]) / GLA_TAU
        lab_scr[h] = _log_sigmoid(_dot_hp(lr, wdb_ref[h]) + bdb_ref[:, kcols[h]]) / GLA_TAU

    per = GLA_UNROLL
    blk = per * c
    row = lax.broadcasted_iota(jnp.int32, (blk, blk), 0)
    col = lax.broadcasted_iota(jnp.int32, (blk, blk), 1)
    same = (row // c) == (col // c)
    lower = same & (row >= col)
    upper = same & (col >= row)
    t_fwd = lower.astype(BF16)
    t_bwd = upper.astype(BF16)

    def chunk_rows(x, r):
        return jnp.concatenate([jnp.broadcast_to(x[j * c + r:j * c + r + 1], (c, dk)) for j in range(per)], axis=0)

    def block_local(m, carry):
        sl = pl.ds(pl.multiple_of(m * blk, blk), blk)
        bf = [_dot_exact_lhs(t_fwd, laf_scr[h, sl, :]) for h in heads]
        bb = [_dot_exact_lhs(t_bwd, lab_scr[h, sl, :]) for h in heads]
        tot_f = [chunk_rows(x, c - 1) for x in bf]
        tot_b = [chunk_rows(x, 0) for x in bb]
        q = [q_ref[0, sl, kcols[h]].astype(F32) * scale for h in heads]
        k = [k_ref[0, sl, kcols[h]].astype(F32) for h in heads]
        v = [v_ref[0, sl, vcols[h]] for h in heads]
        qf = [(q[h] * jnp.exp(bf[h])).astype(BF16) for h in heads]
        kf = [(k[h] * jnp.exp(-bf[h])).astype(BF16) for h in heads]
        qb = [(q[h] * jnp.exp(bb[h])).astype(BF16) for h in heads]
        kb = [(k[h] * jnp.exp(-bb[h])).astype(BF16) for h in heads]
        ksf = [(k[h] * jnp.exp(tot_f[h] - bf[h])).astype(BF16) for h in heads]
        ksb = [(k[h] * jnp.exp(tot_b[h] - bb[h])).astype(BF16) for h in heads]
        sc_f = [_dot_nt(qf[h], kf[h]) for h in heads]
        sc_b = [_dot_nt(qb[h], kb[h]) for h in heads]
        att = [(jnp.where(lower, sc_f[h], 0.0) + jnp.where(upper, sc_b[h], 0.0)).astype(BF16) for h in heads]
        o_loc = [_dot(att[h], v[h]) for h in heads]
        dec_f = [jnp.exp(x) for x in tot_f]
        dec_b = [jnp.exp(x) for x in tot_b]
        for h in heads:
            o_scr[h, sl, :] = o_loc[h]
            qf_scr[h, sl, :] = qf[h]
            qb_scr[h, sl, :] = qb[h]
        for j in range(per):
            n = m * per + j
            rows = slice(j * c, (j + 1) * c)
            for h in heads:
                uf_scr[h, n] = _dot_tn(v[h][rows], ksf[h][rows])
                ub_scr[h, n] = _dot_tn(v[h][rows], ksb[h][rows])
                df_scr[h, n] = dec_f[h][j * c:j * c + 8]
                db_scr[h, n] = dec_b[h][j * c:j * c + 8]
        return carry

    lax.fori_loop(0, n_chunks // per, block_local, 0)

    def scan_fwd(n, s):
        upd = [uf_scr[h, n] for h in heads]
        for h in heads:
            uf_scr[h, n] = s[h]
        return tuple(s[h] * df_scr[h, n][0:1, :] + upd[h] for h in heads)

    def scan_bwd(i, s):
        n = n_chunks - 1 - i
        upd = [ub_scr[h, n] for h in heads]
        for h in heads:
            ub_scr[h, n] = s[h]
        return tuple(s[h] * db_scr[h, n][0:1, :] + upd[h] for h in heads)

    s_f = lax.fori_loop(0, n_chunks, scan_fwd, tuple(s0f_ref[0, h].T for h in heads), unroll=GLA_UNROLL)
    s_b = lax.fori_loop(0, n_chunks, scan_bwd, tuple(s0b_ref[0, h].T for h in heads), unroll=GLA_UNROLL)
    for h in heads:
        sf_ref[0, h] = s_f[h].T
        sb_ref[0, h] = s_b[h].T

    def chunk_inter(n, carry):
        sl = pl.ds(pl.multiple_of(n * c, c), c)
        inter = [_dot_nt(qf_scr[h, sl, :], uf_scr[h, n].astype(BF16))
                 + _dot_nt(qb_scr[h, sl, :], ub_scr[h, n].astype(BF16)) for h in heads]
        for h in heads:
            o_scr[h, sl, :] += inter[h]
        return carry

    lax.fori_loop(0, n_chunks, chunk_inter, 0, unroll=GLA_UNROLL)

    for h in heads:
        o = _rms(o_scr[h], gg_ref[...])
        y_ref[0, :, vcols[h]] = (o * _silu(g_ref[0, :, vcols[h]].astype(F32))).astype(y_ref.dtype)


def _gla(proj, lr, wdf, bdf, wdb, bdb, g_gla, s0f, s0b, batch, seq, dk, dv, hp, cast_srcs):
    heads = GLA_HEADS
    proj3 = proj.reshape(batch, seq, proj.shape[-1])
    lr3 = lr.reshape(batch, seq, LANES)
    kdim = heads * dk
    width = heads * dv
    bk, bv = hp * dk, hp * dv
    k_blk = kdim // bk
    v_blk = 2 * kdim // bv
    g_blk = (2 * kdim + width) // bv
    n_chunks = seq // GLA_CHUNK

    def s0_map(s0):
        if s0.shape[0] == batch:
            return lambda b, h: (b, h, 0, 0)
        return lambda b, h: (0, h, 0, 0)

    n_hsteps = heads // hp
    cast_specs, cast_shapes = _cast_rider_specs(cast_srcs, batch * n_hsteps, lambda b, h: b * n_hsteps + h)
    kern = _with_cast_riders(functools.partial(_gla_kernel, seq=seq, dk=dk, dv=dv, hp=hp), 12, 3, len(cast_srcs))
    return pl.pallas_call(
        kern,
        grid=(batch, n_hsteps),
        in_specs=[pl.BlockSpec((1, seq, bk), lambda b, h: (b, 0, h)),
                  pl.BlockSpec((1, seq, bk), lambda b, h: (b, 0, k_blk + h)),
                  pl.BlockSpec((1, seq, bv), lambda b, h: (b, 0, v_blk + h)),
                  pl.BlockSpec((1, seq, bv), lambda b, h: (b, 0, g_blk + h)),
                  pl.BlockSpec((1, seq, LANES), lambda b, h: (b, 0, 0)),
                  pl.BlockSpec((hp, LANES, dk), lambda b, h: (h, 0, 0)),
                  pl.BlockSpec((1, bk), lambda b, h: (0, h)),
                  pl.BlockSpec((hp, LANES, dk), lambda b, h: (h, 0, 0)),
                  pl.BlockSpec((1, bk), lambda b, h: (0, h)),
                  pl.BlockSpec((1, dv), lambda b, h: (0, 0)),
                  pl.BlockSpec((1, hp, dk, dv), s0_map(s0f)),
                  pl.BlockSpec((1, hp, dk, dv), s0_map(s0b))] + cast_specs,
        out_specs=[pl.BlockSpec((1, seq, bv), lambda b, h: (b, 0, h)),
                   pl.BlockSpec((1, hp, dk, dv), lambda b, h: (b, h, 0, 0)),
                   pl.BlockSpec((1, hp, dk, dv), lambda b, h: (b, h, 0, 0))] + cast_specs,
        out_shape=[jax.ShapeDtypeStruct((batch, seq, width), BF16),
                   jax.ShapeDtypeStruct((batch, heads, dk, dv), F32),
                   jax.ShapeDtypeStruct((batch, heads, dk, dv), F32)] + cast_shapes,
        scratch_shapes=[pltpu.VMEM((hp, seq, dk), F32), pltpu.VMEM((hp, seq, dk), F32),
                        pltpu.VMEM((hp, seq, dk), BF16), pltpu.VMEM((hp, seq, dk), BF16),
                        pltpu.VMEM((hp, seq, dv), F32),
                        pltpu.VMEM((hp, n_chunks, dv, dk), F32), pltpu.VMEM((hp, n_chunks, dv, dk), F32),
                        pltpu.VMEM((hp, n_chunks, 8, dk), F32), pltpu.VMEM((hp, n_chunks, 8, dk), F32)],
        compiler_params=_cparams(("arbitrary", "arbitrary")),
        name="gla",
    )(proj3, proj3, proj3, proj3, lr3, wdf, bdf, wdb, bdb, g_gla, s0f, s0b, *cast_srcs)


def _filter_kernel(pe_ref, w1_ref, b1_ref, w2_ref, b2_ref, fr_ref, w3_ref, dl_ref, ad_ref, hl_ref, hm_ref, mlp_scr,
                   *, seq):
    @pl.when(pl.program_id(0) == 0)
    def _():
        fr = fr_ref[...]
        h1 = jnp.sin(fr * (_dot_hp(pe_ref[...], w1_ref[...]) + b1_ref[...]))
        mlp_scr[...] = jnp.sin(fr * (_dot_hp(h1, w2_ref[...]) + b2_ref[...]))

    h2 = mlp_scr[...]
    dec = jnp.exp(-pe_ref[:, 0:1] * dl_ref[...])
    row = lax.broadcasted_iota(jnp.int32, (seq, 1), 0)
    alt = jnp.where(row % 2 == 0, 1.0, -1.0)
    phase = row % 4
    cos_half = jnp.where(phase == 0, 1.0, jnp.where(phase == 2, -1.0, 0.0))
    sin_half = jnp.where(phase == 1, 1.0, jnp.where(phase == 3, -1.0, 0.0))
    for o in range(HY_ORDER):
        ff = _dot_hp(h2, w3_ref[o]) * dec
        fb = _dot_hp(h2, w3_ref[HY_ORDER + o]) * dec
        nrm = jnp.sum(jnp.abs(ff), axis=0, keepdims=True) + jnp.sum(jnp.abs(fb), axis=0, keepdims=True)
        ff = ff / nrm
        fb = jnp.where(row == 0, 0.0, fb / nrm)
        a = ff + fb
        d = fb - ff
        ad_ref[o, 0] = a.astype(ad_ref.dtype)
        ad_ref[o, 1] = d.astype(ad_ref.dtype)
        ad_ref[o, 2] = (alt * a).astype(ad_ref.dtype)
        ad_ref[o, 3] = (-alt * d).astype(ad_ref.dtype)
        hl_ref[o] = jnp.sum(alt * a, axis=0, keepdims=True)
        hm_ref[o] = jnp.concatenate([jnp.sum(cos_half * a, axis=0, keepdims=True),
                                     jnp.sum(sin_half * d, axis=0, keepdims=True)], axis=0) * (1.0 / seq)


def _hy_filter(pe, w1p, b1, w2, b2, freq, w3r, deltas, seq, width, tc):
    hid = w2.shape[0]
    kern = functools.partial(_filter_kernel, seq=seq)
    return pl.pallas_call(
        kern,
        grid=(width // tc,),
        in_specs=[pl.BlockSpec((seq, LANES), lambda c: (0, 0)),
                  pl.BlockSpec((LANES, hid), lambda c: (0, 0)),
                  pl.BlockSpec((1, hid), lambda c: (0, 0)),
                  pl.BlockSpec((hid, hid), lambda c: (0, 0)),
                  pl.BlockSpec((1, hid), lambda c: (0, 0)),
                  pl.BlockSpec((1, hid), lambda c: (0, 0)),
                  pl.BlockSpec((2 * HY_ORDER, hid, tc), lambda c: (0, 0, c)),
                  pl.BlockSpec((1, tc), lambda c: (0, c))],
        out_specs=[pl.BlockSpec((HY_ORDER, 4, seq, tc), lambda c: (0, 0, 0, c)),
                   pl.BlockSpec((HY_ORDER, 1, tc), lambda c: (0, 0, c)),
                   pl.BlockSpec((HY_ORDER, 2, tc), lambda c: (0, 0, c))],
        out_shape=[jax.ShapeDtypeStruct((HY_ORDER, 4, seq, width), BF16),
                   jax.ShapeDtypeStruct((HY_ORDER, 1, width), F32),
                   jax.ShapeDtypeStruct((HY_ORDER, 2, width), F32)],
        scratch_shapes=[pltpu.VMEM((seq, hid), F32)],
        compiler_params=_cparams(("arbitrary",)),
        name="hy_filter",
    )(pe, w1p, b1, w2, b2, freq, w3r, deltas)


def _spectrum_kernel(cs_ref, ad_ref, h_ref, *, seq, per_part, tr):
    r = pl.program_id(1)
    k = (r % per_part) * tr + lax.broadcasted_iota(jnp.int32, (tr, 1), 0)
    wgt = jnp.where(k == 0, 1.0, 2.0) * (0.5 / seq)
    h_ref[0] = wgt * _dot(cs_ref[0], ad_ref[0, 0])


def _hy_spectrum(cs3, ad, seq, part_rows, n_parts, width, tr, tc):
    per_part = part_rows // tr
    kern = functools.partial(_spectrum_kernel, seq=seq, per_part=per_part, tr=tr)
    return pl.pallas_call(
        kern,
        grid=(HY_ORDER, n_parts * per_part, width // tc),
        in_specs=[pl.BlockSpec((1, tr, seq), lambda o, r, c: ((r // per_part) % 2, r % per_part, 0)),
                  pl.BlockSpec((1, 1, seq, tc), lambda o, r, c: (o, r // per_part, 0, c))],
        out_specs=pl.BlockSpec((1, tr, tc), lambda o, r, c: (o, r, c)),
        out_shape=jax.ShapeDtypeStruct((HY_ORDER, n_parts * part_rows, width), F32),
        compiler_params=_cparams(("arbitrary", "arbitrary", "arbitrary")),
        name="hy_spectrum",
    )(cs3, ad)


def _short_conv_fn(seq, seg):
    row = lax.broadcasted_iota(jnp.int32, (seq, 1), 0)
    pos = row % seg

    def short_conv(u_ref, cw_ref, cb_ref, b=0):
        u = u_ref[b].astype(F32)
        prev = jnp.where(pos == 0, 0.0, pltpu.roll(u, 1, 0))
        nxt = jnp.where(pos == seg - 1, 0.0, pltpu.roll(u, seq - 1, 0))
        return prev * cw_ref[0:1, :] + u * cw_ref[1:2, :] + nxt * cw_ref[2:3, :] + cb_ref[...]

    return short_conv


def _hyconv_kernel(u0_ref, u1_ref, u2_ref, cw0_ref, cw1_ref, cw2_ref, cb0_ref, cb1_ref, cb2_ref,
                   cs_ref, h_ref, hl_ref, bias_ref, z_ref, y_scr, *, seq, seg):
    short_conv = _short_conv_fn(seq, seg)
    row = lax.broadcasted_iota(jnp.int32, (seq, 1), 0)
    alt = jnp.where(row % 2 == 0, 1.0, -1.0)
    gates = ((u1_ref, cw1_ref, cb1_ref), (u2_ref, cw2_ref, cb2_ref))
    items = range(z_ref.shape[0])
    z = [short_conv(u0_ref, cw0_ref, cb0_ref, b) for b in items]
    kb = min(seq, 512)
    for n in range(HY_ORDER):
        zb = [z[b].astype(BF16) for b in items]
        for r in range(0, seq, kb):
            xc = [_dot(cs_ref[r:r + kb, :], zb[b]) for b in items]
            xs = [_dot(cs_ref[seq + r:seq + r + kb, :], zb[b]) for b in items]
            hre = h_ref[n, r:r + kb, :]
            him = h_ref[n, seq + r:seq + r + kb, :]
            for b in items:
                y_scr[b, r:r + kb, :] = (xc[b] * hre + xs[b] * him).astype(BF16)
                y_scr[b, seq + r:seq + r + kb, :] = (xs[b] * hre - xc[b] * him).astype(BF16)
        nyq = [jnp.sum(alt * z[b], axis=0, keepdims=True) * (hl_ref[n] * (0.5 / seq)) for b in items]
        conv = [_dot(cs_ref[:seq, :], y_scr[b, :seq, :]) + _dot(cs_ref[seq:, :], y_scr[b, seq:, :]) + alt * nyq[b]
                for b in items]
        gate = [short_conv(*gates[n], b) for b in items]
        z = [gate[b] * (conv[b] + bias_ref[n] * z[b]) for b in items]
    for b in items:
        z_ref[b] = z[b].astype(z_ref.dtype)


def _hyconv_split_kernel(u0_ref, u1_ref, u2_ref, cw0_ref, cw1_ref, cw2_ref, cb0_ref, cb1_ref, cb2_ref,
                         cs_ref, h_ref, hm_ref, tw_ref, bias_ref, z_ref, g_scr, z_scr, c_scr, *, seq, seg):
    m = seq // 2
    tc = z_ref.shape[2]
    n_lane_blocks = tc // LANES
    short_conv = _short_conv_fn(seq, seg)
    row = lax.broadcasted_iota(jnp.int32, (m, 1), 0)
    alt = jnp.where(row % 2 == 0, 1.0, -1.0)
    gates = ((u1_ref, cw1_ref, cb1_ref), (u2_ref, cw2_ref, cb2_ref))
    z = short_conv(u0_ref, cw0_ref, cb0_ref)
    kb = min(m, 512)
    for n in range(HY_ORDER):
        for j in range(n_lane_blocks):
            z_scr[j] = z[:, j * LANES:(j + 1) * LANES]
        ze = jnp.concatenate([z_scr[j, pl.ds(0, m, stride=2), :] for j in range(n_lane_blocks)], axis=1)
        zo = jnp.concatenate([z_scr[j, pl.ds(1, m, stride=2), :] for j in range(n_lane_blocks)], axis=1)
        e_mid = jnp.sum(alt * ze, axis=0, keepdims=True)
        o_mid = jnp.sum(alt * zo, axis=0, keepdims=True)
        zeb = ze.astype(BF16)
        zob = zo.astype(BF16)
        for r in range(0, m, kb):
            rows = slice(r, r + kb)
            srows = slice(m + r, m + r + kb)
            ec = _dot(cs_ref[rows, :], zeb)
            es = _dot(cs_ref[srows, :], zeb)
            oc = _dot(cs_ref[rows, :], zob)
            os_ = _dot(cs_ref[srows, :], zob)
            c = tw_ref[0, rows, :]
            s = tw_ref[1, rows, :]
            pc = c * oc - s * os_
            ps = c * os_ + s * oc
            xca, xsa = ec + pc, es + ps
            xcb, xsb = ec - pc, ps - es
            har = h_ref[n, rows, :]
            hai = h_ref[n, srows, :]
            hbr = h_ref[n, 2 * m + r:2 * m + r + kb, :]
            hbi = h_ref[n, 3 * m + r:3 * m + r + kb, :]
            yar = xca * har + xsa * hai
            yai = xca * hai - xsa * har
            ybr = xcb * hbr + xsb * hbi
            ybi = xcb * hbi - xsb * hbr
            dr = yar - ybr
            di = yai + ybi
            g_scr[0, rows, :] = (yar + ybr).astype(BF16)
            g_scr[0, srows, :] = (ybi - yai).astype(BF16)
            g_scr[1, rows, :] = (c * dr - s * di).astype(BF16)
            g_scr[1, srows, :] = (-(s * dr + c * di)).astype(BF16)
        hr = hm_ref[n, 0:1, :]
        hi = hm_ref[n, 1:2, :]
        ymr = e_mid * hr + o_mid * hi
        ymi = e_mid * hi - o_mid * hr
        y_even = _dot(cs_ref[:m, :], g_scr[0, :m, :]) + _dot(cs_ref[m:, :], g_scr[0, m:, :]) + alt * ymr
        y_odd = _dot(cs_ref[:m, :], g_scr[1, :m, :]) + _dot(cs_ref[m:, :], g_scr[1, m:, :]) - alt * ymi
        for j in range(n_lane_blocks):
            c_scr[j, pl.ds(0, m, stride=2), :] = y_even[:, j * LANES:(j + 1) * LANES]
            c_scr[j, pl.ds(1, m, stride=2), :] = y_odd[:, j * LANES:(j + 1) * LANES]
        conv = jnp.concatenate([c_scr[j] for j in range(n_lane_blocks)], axis=1)
        z = short_conv(*gates[n]) * (conv + bias_ref[n] * z)
    z_ref[0] = z.astype(z_ref.dtype)


def _hy_conv(proj, conv_w, conv_b, cs, hspec, hl, hm, tw, bias, batch, seq, seg, col0, width, tc, split, bp,
             cast_srcs):
    proj3 = proj.reshape(batch, seq, proj.shape[-1])
    blk0 = col0 // tc
    per = width // tc
    n_b = batch // bp
    assert bp == 1 or not split
    once = dict(pipeline_mode=pl.Buffered(1))

    def u_spec(p):
        return pl.BlockSpec((bp, seq, tc), lambda c, b: (b, 0, blk0 + p * per + c))

    def w_spec(p, rows):
        return pl.BlockSpec((rows, tc), lambda c, b: (0, p * per + c))

    in_specs = [u_spec(0), u_spec(1), u_spec(2),
                w_spec(0, HY_SHORT), w_spec(1, HY_SHORT), w_spec(2, HY_SHORT),
                w_spec(0, 1), w_spec(1, 1), w_spec(2, 1),
                pl.BlockSpec(cs.shape, lambda c, b: (0, 0), **once),
                pl.BlockSpec((HY_ORDER, 2 * seq, tc), lambda c, b: (0, 0, c), **once)]
    bias_spec = pl.BlockSpec((HY_ORDER, 1, tc), lambda c, b: (0, 0, c))
    operands = [proj3, proj3, proj3, conv_w, conv_w, conv_w, conv_b, conv_b, conv_b, cs, hspec]
    if split:
        kern = functools.partial(_hyconv_split_kernel, seq=seq, seg=seg)
        in_specs += [pl.BlockSpec((HY_ORDER, 2, tc), lambda c, b: (0, 0, c)),
                     pl.BlockSpec((2, seq // 2, tc), lambda c, b: (0, 0, 0)), bias_spec]
        operands += [hm, tw, bias]
        scratch = [pltpu.VMEM((2, seq, tc), BF16), pltpu.VMEM((tc // LANES, seq, LANES), F32),
                   pltpu.VMEM((tc // LANES, seq, LANES), F32)]
    else:
        kern = functools.partial(_hyconv_kernel, seq=seq, seg=seg)
        in_specs += [pl.BlockSpec((HY_ORDER, 1, tc), lambda c, b: (0, 0, c)), bias_spec]
        operands += [hl, bias]
        scratch = [pltpu.VMEM((bp, 2 * seq, tc), BF16)]
    cast_specs, cast_shapes = _cast_rider_specs(cast_srcs, per * n_b, lambda c, b: c * n_b + b)
    kern = _with_cast_riders(kern, len(operands), 1, len(cast_srcs))
    return pl.pallas_call(
        kern,
        grid=(per, n_b),
        in_specs=in_specs + cast_specs,
        out_specs=[pl.BlockSpec((bp, seq, tc), lambda c, b: (b, 0, c))] + cast_specs,
        out_shape=[jax.ShapeDtypeStruct((batch, seq, width), BF16)] + cast_shapes,
        scratch_shapes=scratch,
        compiler_params=_cparams(("arbitrary", "arbitrary")),
        name="hy_conv",
    )(*operands, *cast_srcs)


def _mixout_kernel(*refs, gla_width, n_own, aliased):
    (yg_ref, zh_ref, x_ref, mod_ref, ghy_ref, gpost_ref, gpre_ref, wo_ref, wrh_ref, wrl_ref, br_ref,
     cnt_in_ref) = refs[:12]
    x1_ref, hg_ref, route_ref, cnt_ref, cnt_scr = refs[12 + int(aliased):]
    d = x_ref.shape[1]
    tm = x_ref.shape[0] // MIX_SUBTILES
    i = pl.program_id(0)

    @pl.when(i == 0)
    def _():
        cnt_scr[...] = cnt_in_ref[...]

    @pl.when(i < n_own)
    def _():
        for s in range(MIX_SUBTILES):
            _mixout_rows(slice(s * tm, (s + 1) * tm), tm, d, gla_width, yg_ref, zh_ref, x_ref, mod_ref, ghy_ref,
                         gpost_ref, gpre_ref, wo_ref, wrh_ref, wrl_ref, br_ref, x1_ref, hg_ref, route_ref, cnt_scr)

    @pl.when(i >= n_own)
    def _():
        hg_ref[...] = jnp.zeros_like(hg_ref)

    cnt_ref[...] = cnt_scr[...]


def _mixout_rows(rows, tm, d, gla_width, yg_ref, zh_ref, x_ref, mod_ref, ghy_ref, gpost_ref, gpre_ref, wo_ref,
                 wrh_ref, wrl_ref, br_ref, x1_ref, hg_ref, route_ref, cnt_scr):
    yh = _rms(zh_ref[rows, :].astype(F32), ghy_ref[...]).astype(BF16)
    y = _dot(yg_ref[rows, :], wo_ref[:gla_width, :]) + _dot(yh, wo_ref[gla_width:, :])
    x1 = x_ref[rows, :] + mod_ref[0, 2:3, :] * _rms(y, gpost_ref[...])
    x1_ref[rows, :] = x1
    h2 = _rms(x1, gpre_ref[...]) * (1.0 + mod_ref[0, 4:5, :]) + mod_ref[0, 3:4, :]
    hh, hl = _split2(h2)
    hg_ref[rows, :d] = h2
    logits = _dot(hh, wrh_ref[...]) + (_dot(hh, wrl_ref[...]) + _dot(hl, wrh_ref[...])) + br_ref[...]

    lane = lax.broadcasted_iota(jnp.int32, logits.shape, 1).astype(F32)
    neg = -jnp.inf
    is_grp = (lane >= N_EXPERTS) & (lane < N_EXPERTS + N_GROUPS)
    m = jnp.max(jnp.where(is_grp, logits, neg), axis=1, keepdims=True)
    p_grp = 1.0 / jnp.sum(jnp.where(is_grp, jnp.exp(logits - m), 0.0), axis=1, keepdims=True)
    grp = jnp.min(jnp.where(is_grp & (logits == m), lane - N_EXPERTS, 1e9), axis=1, keepdims=True)
    sel = (lane >= grp * EXP_PER_GROUP) & (lane < (grp + 1.0) * EXP_PER_GROUP)
    me = jnp.max(jnp.where(sel, logits, neg), axis=1, keepdims=True)
    pe = jnp.where(sel, jnp.exp(logits - me), -1.0)
    v1 = jnp.max(pe, axis=1, keepdims=True)
    i1 = jnp.min(jnp.where(pe == v1, lane, 1e9), axis=1, keepdims=True)
    pe2 = jnp.where(lane == i1, -1.0, pe)
    v2 = jnp.max(pe2, axis=1, keepdims=True)
    i2 = jnp.min(jnp.where(pe2 == v2, lane, 1e9), axis=1, keepdims=True)
    den = v1 + v2
    gates = jnp.where(lane == i1, v1 / den, jnp.where(lane == i2, v2 / den, 0.0)) * p_grp

    lo = jnp.minimum(i1, i2) - grp * EXP_PER_GROUP
    hi = jnp.maximum(i1, i2) - grp * EXP_PER_GROUP
    bucket = grp * PAIRS_PER_GROUP + lo * (2 * EXP_PER_GROUP - 1 - lo) * 0.5 + (hi - lo - 1.0)
    onehot = lane == bucket
    r_i = lax.broadcasted_iota(jnp.int32, (tm, tm), 0)
    c_i = lax.broadcasted_iota(jnp.int32, (tm, tm), 1)
    earlier = _dot((r_i > c_i).astype(BF16), onehot.astype(BF16)) + cnt_scr[...]
    rank = jnp.sum(jnp.where(onehot, earlier, 0.0), axis=1, keepdims=True)
    cnt_scr[...] += jnp.sum(onehot.astype(F32), axis=0, keepdims=True)
    route = jnp.where(lane == ROUTE_BUCKET_LANE, bucket, jnp.where(lane == ROUTE_RANK_LANE, rank, gates))
    hg_ref[rows, d:] = route
    route_ref[:, rows] = route.T[ROUTE_BUCKET_LANE:ROUTE_BUCKET_LANE + 8, :]


def _mix_out(yg, zh, x, mod, mod_row, g_hy, g_post, g_pre, w_out, wr_hi, wr_lo, b_r, cnt_in, hg_all, row0, n_total, tm):
    n, d = x.shape
    gw = yg.shape[1]
    hw = zh.shape[1]
    aliased = hg_all is not None
    n_own = n // tm
    n_steps = n_own if aliased else n_total // tm
    assert row0 % tm == 0 and n_total % tm == 0 and (row0 + n == n_total if aliased else row0 == 0)
    kern = functools.partial(_mixout_kernel, gla_width=gw, n_own=n_own, aliased=aliased)
    row = lambda i: (jnp.minimum(i, n_own - 1), 0)
    fixed = lambda i: (0, 0)
    in_specs = [pl.BlockSpec((tm, gw), row),
                pl.BlockSpec((tm, hw), row),
                pl.BlockSpec((tm, d), row),
                pl.BlockSpec((1, 6, d), lambda i: (mod_row(jnp.minimum(i, n_own - 1) * tm), 0, 0)),
                pl.BlockSpec((1, hw), fixed),
                pl.BlockSpec((1, d), fixed),
                pl.BlockSpec((1, d), fixed),
                pl.BlockSpec((gw + hw, d), fixed),
                pl.BlockSpec((d, LANES), fixed),
                pl.BlockSpec((d, LANES), fixed),
                pl.BlockSpec((1, LANES), fixed),
                pl.BlockSpec((1, LANES), fixed)]
    operands = [yg, zh, x, mod, g_hy, g_post, g_pre, w_out, wr_hi, wr_lo, b_r, cnt_in]
    if aliased:
        in_specs.append(pl.BlockSpec(memory_space=pl.ANY))
        operands.append(hg_all)
    return pl.pallas_call(
        kern,
        grid=(n_steps,),
        in_specs=in_specs,
        out_specs=[pl.BlockSpec((tm, d), row),
                   pl.BlockSpec((tm, d + LANES), lambda i: (row0 // tm + i, 0)),
                   pl.BlockSpec((8, tm), lambda i: (0, jnp.minimum(i, n_own - 1))),
                   pl.BlockSpec((1, LANES), fixed)],
        out_shape=[jax.ShapeDtypeStruct((n, d), F32),
                   jax.ShapeDtypeStruct((n_total, d + LANES), F32),
                   jax.ShapeDtypeStruct((8, n), F32),
                   jax.ShapeDtypeStruct((1, LANES), F32)],
        scratch_shapes=[pltpu.VMEM((1, LANES), F32)],
        input_output_aliases={len(operands) - 1: 1} if aliased else {},
        compiler_params=_cparams(("arbitrary",)),
        name="mix_out",
    )(*operands)


def _row_gather_copy(src_hbm, row, buf, slot, r, sem):
    return pltpu.make_async_copy(src_hbm.at[pl.ds(row, 1), :], buf.at[slot, pl.ds(r, 1), :], sem.at[slot])


def _start_row_gather(idx_ref, base, src_hbm, buf, slot, sem, tm):
    def body(r, carry):
        _row_gather_copy(src_hbm, idx_ref[base + r], buf, slot, r, sem).start()
        return carry

    lax.fori_loop(0, tm, body, 0, unroll=8)


def _start_row_gather_inline(idx_ref, base, src_hbm, buf, slot, sem, tm):
    for r in range(tm):
        _row_gather_copy(src_hbm, idx_ref[base + r], buf, slot, r, sem).start()


def _wait_row_gather(src_hbm, buf, slot, sem, tm):
    pltpu.make_async_copy(src_hbm.at[pl.ds(0, tm), :], buf.at[slot], sem.at[slot]).wait()


def _moe_kernel(src_ref, ea_ref, eb_ref, valid_ref, hg_hbm,
                wga_ref, wua_ref, wda_ref, wgb_ref, wub_ref, wdb_ref, y_ref, buf, sem, *, tm, d):
    t = pl.program_id(0)
    slot = t % 2

    @pl.when(t == 0)
    def _():
        _start_row_gather(src_ref, 0, hg_hbm, buf, 0, sem, tm)

    @pl.when(valid_ref[t] == 1)
    def _():
        _wait_row_gather(hg_hbm, buf, slot, sem, tm)
        h = buf[slot, :, :d].astype(BF16)
        gates = buf[slot, :, d:]
        lane = lax.broadcasted_iota(jnp.int32, gates.shape, 1)
        n_groups = 8
        issued = [0]

        def request_rows():
            g = issued[0]
            issued[0] += 1
            for r in range(g * tm // n_groups, (g + 1) * tm // n_groups):
                _row_gather_copy(hg_hbm, src_ref[(t + 1) * tm + r], buf, 1 - slot, r, sem).start()

        def expert(wg_ref, wu_ref, wd_ref, e):
            request_rows()
            a = _dot(h, wg_ref[0])
            request_rows()
            b = _dot(h, wu_ref[0])
            request_rows()
            he = (_silu(a) * b).astype(BF16)
            gate = jnp.sum(jnp.where(lane == e, gates, 0.0), axis=1, keepdims=True)
            y = gate * _dot(he, wd_ref[0])
            request_rows()
            return y

        y_ref[...] = expert(wga_ref, wua_ref, wda_ref, ea_ref[t]) + expert(wgb_ref, wub_ref, wdb_ref, eb_ref[t])
        assert issued[0] == n_groups

    @pl.when(valid_ref[t] == 0)
    def _():
        @pl.when(valid_ref[jnp.maximum(t - 1, 0)] == 1)
        def _():
            _wait_row_gather(hg_hbm, buf, slot, sem, tm)

        y_ref[...] = jnp.zeros_like(y_ref)


def _moe(hg, src, ea, eb, valid, wg, wu, wd, n_tiles, tm):
    d = hg.shape[1] - LANES
    _, _, de = wg.shape
    kern = functools.partial(_moe_kernel, tm=tm, d=d)
    wa = lambda t, src, ea, eb, valid: (ea[t], 0, 0)
    wb = lambda t, src, ea, eb, valid: (eb[t], 0, 0)
    return pl.pallas_call(
        kern,
        grid_spec=pltpu.PrefetchScalarGridSpec(
            num_scalar_prefetch=4,
            grid=(n_tiles,),
            in_specs=[pl.BlockSpec(memory_space=pl.ANY),
                      pl.BlockSpec((1, d, de), wa), pl.BlockSpec((1, d, de), wa), pl.BlockSpec((1, de, d), wa),
                      pl.BlockSpec((1, d, de), wb), pl.BlockSpec((1, d, de), wb), pl.BlockSpec((1, de, d), wb)],
            out_specs=pl.BlockSpec((tm, d), lambda t, src, ea, eb, valid: (t, 0)),
            scratch_shapes=[pltpu.VMEM((2, tm, d + LANES), F32), pltpu.SemaphoreType.DMA((2,))]),
        out_shape=jax.ShapeDtypeStruct((n_tiles * tm, d), F32),
        compiler_params=_cparams(("arbitrary",)),
        name="moe",
    )(src, ea, eb, valid, hg, wg, wu, wd, wg, wu, wd)


def _ffn_out_kernel(pos_ref, y_hbm, x1_ref, mod_ref, gpost_ref, o_ref, buf, sem, *, tm):
    i = pl.program_id(0)
    n_i = pl.num_programs(0)
    slot = i % 2

    @pl.when(i == 0)
    def _():
        _start_row_gather(pos_ref, 0, y_hbm, buf, 0, sem, tm)

    _wait_row_gather(y_hbm, buf, slot, sem, tm)

    @pl.when(i + 1 < n_i)
    def _():
        _start_row_gather_inline(pos_ref, (i + 1) * tm, y_hbm, buf, 1 - slot, sem, tm)
        o_ref[...] = x1_ref[...] + mod_ref[0, 5:6, :] * _rms(buf[slot], gpost_ref[...])

    @pl.when(i + 1 == n_i)
    def _():
        o_ref[...] = x1_ref[...] + mod_ref[0, 5:6, :] * _rms(buf[slot], gpost_ref[...])


def _ffn_out(y_sorted, pos, x1, mod, mod_row, g_post, tm):
    n, d = x1.shape
    kern = functools.partial(_ffn_out_kernel, tm=tm)
    return pl.pallas_call(
        kern,
        grid_spec=pltpu.PrefetchScalarGridSpec(
            num_scalar_prefetch=1,
            grid=(n // tm,),
            in_specs=[pl.BlockSpec(memory_space=pl.ANY),
                      pl.BlockSpec((tm, d), lambda i, pos: (i, 0)),
                      pl.BlockSpec((1, 6, d), lambda i, pos: (mod_row(i * tm), 0, 0)),
                      pl.BlockSpec((1, d), lambda i, pos: (0, 0))],
            out_specs=pl.BlockSpec((tm, d), lambda i, pos: (i, 0)),
            scratch_shapes=[pltpu.VMEM((2, tm, d), F32), pltpu.SemaphoreType.DMA((2,))]),
        out_shape=jax.ShapeDtypeStruct((n, d), F32),
        compiler_params=_cparams(("arbitrary",)),
        name="ffn_out",
    )(pos, y_sorted, x1, mod, g_post)


def _route_tables(route, counts, n_tiles, tm):
    n = route.shape[1]
    n_buckets = N_GROUPS * PAIRS_PER_GROUP
    cnt = counts[0, :n_buckets].astype(jnp.int32)
    padded = (cnt + tm - 1) // tm * tm
    ends = jnp.cumsum(padded)
    starts = ends - padded
    bucket = route[0].astype(jnp.int32)
    rank = route[ROUTE_RANK_LANE - ROUTE_BUCKET_LANE].astype(jnp.int32)
    pos = starts[bucket] + rank
    src = (jnp.arange(n_tiles * tm, dtype=jnp.int32) % n).at[pos].set(jnp.arange(n, dtype=jnp.int32))
    n_valid = ends[-1] // tm
    tile = jnp.arange(n_tiles, dtype=jnp.int32)
    used = jnp.minimum(tile, n_valid - 1)
    tile_bucket = jnp.sum((ends[None, :] <= (used * tm)[:, None]).astype(jnp.int32), axis=1)
    grp = tile_bucket // PAIRS_PER_GROUP
    pair = tile_bucket % PAIRS_PER_GROUP
    pair_lo = jnp.array([a for a in range(EXP_PER_GROUP) for b in range(a + 1, EXP_PER_GROUP)], jnp.int32)
    pair_hi = jnp.array([b for a in range(EXP_PER_GROUP) for b in range(a + 1, EXP_PER_GROUP)], jnp.int32)
    ea = grp * EXP_PER_GROUP + pair_lo[pair]
    eb = grp * EXP_PER_GROUP + pair_hi[pair]
    valid = (tile < n_valid).astype(jnp.int32)
    return pos, src, ea, eb, valid


def _dft_kernel(cs_ref, *, seq, tr):
    k = pl.program_id(0) * tr + lax.broadcasted_iota(jnp.int32, (tr, LANES), 0)
    j = lax.broadcasted_iota(jnp.int32, (tr, LANES), 1)
    period = 2 * seq

    def table(step):
        ang = ((k * j * step) % period).astype(F32) * (math.pi / seq)
        return jnp.cos(ang), jnp.sin(ang)

    c0, s0 = table(1)
    c1, s1 = table(LANES)
    for t1 in range(seq // LANES):
        cols = slice(t1 * LANES, (t1 + 1) * LANES)
        ca = c1[:, t1:t1 + 1]
        sa = s1[:, t1:t1 + 1]
        cs_ref[0, :, cols] = (ca * c0 - sa * s0).astype(cs_ref.dtype)
        cs_ref[1, :, cols] = (sa * c0 + ca * s0).astype(cs_ref.dtype)


def _dft_cos_sin(seq, n_rows, tr):
    kern = functools.partial(_dft_kernel, seq=seq, tr=tr)
    return pl.pallas_call(
        kern,
        grid=(n_rows // tr,),
        out_specs=pl.BlockSpec((2, tr, seq), lambda r: (0, r, 0)),
        out_shape=jax.ShapeDtypeStruct((2, n_rows, seq), BF16),
        compiler_params=_cparams(("arbitrary",)),
        name="dft_matrix",
    )()


def _positional_features(seq):
    t = jnp.arange(seq, dtype=F32)
    t01 = t / max(seq - 1, 1)
    ang = 2.0 * math.pi * t / seq
    bands = jnp.linspace(1e-4, HY_BANDS - 1, HY_BANDS, dtype=F32)
    pe = jnp.concatenate([t01[:, None], jnp.cos(ang[:, None] * bands), -jnp.sin(ang[:, None] * bands)], axis=-1)
    return jnp.pad(pe, ((0, 0), (0, LANES - pe.shape[1])))


def _mixer_and_router(x3, mod, mod_row, s0f, s0b, n_seg, p, tiles, cnt_in, hg_all, row0, n_total, w_out, casts):
    batch, seq, d = x3.shape
    x = x3.reshape(batch * seq, d)
    dk, dv = p["dk"], p["dv"]
    hy_width = p["hy_width"]
    casted = {}

    proj, lr, *casted["premix"] = _premix_proj(x, mod, mod_row, p["g_pre_mix"], p["w_main"], p["wlr_hi"],
                                               p["wlr_lo"], tiles["tm_proj"], tiles["tn_proj"], casts["premix"])
    if w_out is None:
        w_out = casted["premix"][0]
    y_gla, s_f, s_b, *casted["gla"] = _gla(proj, lr, p["wdf"], p["bdf"], p["wdb"], p["bdb"], p["g_gla"], s0f, s0b,
                                           batch, seq, dk, dv, tiles["gla_heads_per_step"], casts["gla"])

    ad, hl, hm = _hy_filter(_positional_features(seq), p["hy_w1"], p["hy_b1"], p["hy_w2"], p["hy_b2"], p["hy_freq"],
                            p["hy_w3"], p["hy_deltas"], seq, hy_width, tiles["tc_filter"])
    tr, tc_conv = tiles["tr_spec"], tiles["tc_conv"]
    if tiles["hy_split"]:
        half = seq // 2
        hspec = _hy_spectrum(_dft_cos_sin(seq, half, tr), ad, seq, half, 4, hy_width, tr, tiles["tc_spec"])
        cs = _dft_cos_sin(half, half, tr).reshape(seq, half)
        ang = jnp.arange(half, dtype=F32) * (math.pi / seq)
        tw = jnp.broadcast_to(jnp.stack([jnp.cos(ang), jnp.sin(ang)])[:, :, None], (2, half, tc_conv))
    else:
        cs3 = _dft_cos_sin(seq, seq, tr)
        hspec = _hy_spectrum(cs3, ad, seq, seq, 2, hy_width, tr, tiles["tc_spec"])
        cs = cs3.reshape(2 * seq, seq)
        tw = None
    z_hy, *casted["hy_conv"] = _hy_conv(proj, p["hy_conv_w"], p["hy_conv_b"], cs, hspec, hl, hm, tw, p["hy_bias"],
                                        batch, seq, seq // n_seg, p["hy_col0"], hy_width, tc_conv, tiles["hy_split"],
                                        tiles["hy_batch_per_step"], casts["hy_conv"])

    x1, hg_all, route, counts = _mix_out(y_gla.reshape(batch * seq, -1), z_hy.reshape(batch * seq, -1), x, mod,
                                         mod_row, p["g_hy"], p["g_post_mix"], p["g_pre_ffn"], w_out, p["wr_hi"],
                                         p["wr_lo"], p["b_r"], cnt_in, hg_all, row0, n_total, tiles["tm_mix"])
    return x1, hg_all, route, counts, s_f, s_b, casted


def kernel(x_prompt, x_sample, c, state_gla_fwd, state_gla_bwd, c_ctx, w_ada, b_ada, g_pre_mix, g_post_mix, g_pre_ffn, g_post_ffn, w_in, w_dec_f, b_dec_f, w_dec_b, b_dec_b, g_gla, hy_conv_w, hy_conv_b, hy_w1, hy_b1, hy_w2, hy_b2, hy_w3, hy_freq, hy_bias, g_hy, w_out, w_router_grp, b_router_grp, w_router_exp, b_router_exp, w_exp_gate, w_exp_up, w_exp_down):
    depth = w_ada.shape[0]
    assert depth == 1
    l = 0
    d = x_prompt.shape[-1]
    dec_batch = x_sample.shape[0]
    heads = GLA_HEADS
    dk, dv = state_gla_fwd.shape[-2:]
    kdim = heads * dk
    gla_width = heads * dv
    hy_width = g_hy.shape[-1]
    hid = hy_w2.shape[-1]

    cond = jnp.concatenate([c_ctx[None, :], c], axis=0)
    cond = jnp.pad(cond, ((0, -cond.shape[0] % 8), (0, 0)))
    mod = _ada_mod(cond, w_ada[l], b_ada[l]).reshape(cond.shape[0], 6, d)

    n_main = 2 * kdim + 2 * gla_width
    w_main, wlr_hi, wlr_lo = _repack_w_in(w_in, l, n_main, 2 * GLA_RANK, 256)

    def dec_weight(w, first_row):
        wh = w.reshape(GLA_RANK, heads, dk).transpose(1, 0, 2)
        return jnp.pad(wh, ((0, 0), (first_row, LANES - GLA_RANK - first_row), (0, 0)))

    deltas = jnp.abs(jnp.linspace(math.log(HY_TARGET) / HY_SLOW_PCT, math.log(HY_TARGET) / HY_FAST_PCT, hy_width,
                                  dtype=F32))
    w_r = jnp.pad(jnp.concatenate([w_router_exp[l], w_router_grp[l]], axis=1),
                  ((0, 0), (0, LANES - N_EXPERTS - N_GROUPS)))
    wr_hi, wr_lo = _split2(w_r)
    b_r = jnp.pad(jnp.concatenate([b_router_exp[l], b_router_grp[l]]), (0, LANES - N_EXPERTS - N_GROUPS))

    p = dict(
        dk=dk, dv=dv, hy_width=hy_width, hy_col0=n_main,
        g_pre_mix=g_pre_mix[l][None, :], g_post_mix=g_post_mix[l][None, :],
        g_pre_ffn=g_pre_ffn[l][None, :], g_post_ffn=g_post_ffn[l][None, :],
        w_main=w_main, wlr_hi=wlr_hi, wlr_lo=wlr_lo,
        wdf=dec_weight(w_dec_f[l], 0), bdf=b_dec_f[l][None, :],
        wdb=dec_weight(w_dec_b[l], GLA_RANK), bdb=b_dec_b[l][None, :],
        g_gla=g_gla[l][None, :],
        hy_conv_w=hy_conv_w[l], hy_conv_b=hy_conv_b[l][None, :],
        hy_w1=jnp.pad(hy_w1[l], ((0, LANES - hy_w1.shape[1]), (0, 0))), hy_b1=hy_b1[l][None, :],
        hy_w2=hy_w2[l], hy_b2=hy_b2[l][None, :], hy_freq=hy_freq[l][None, :],
        hy_w3=hy_w3[l].reshape(hid, 2 * HY_ORDER, hy_width).transpose(1, 0, 2), hy_deltas=deltas[None, :],
        hy_bias=hy_bias[l][:, None, :], g_hy=g_hy[l][None, :],
        wr_hi=wr_hi, wr_lo=wr_lo, b_r=b_r[None, :],
    )

    zero_state = jnp.zeros((1, heads, dk, dv), F32)
    dec_seq = x_sample.shape[1]
    tiles_p = dict(tm_proj=512, tn_proj=2048, tc_filter=256, tr_spec=256, tc_spec=512, tc_conv=512, tm_mix=512,
                   hy_split=False, gla_heads_per_step=4, hy_batch_per_step=2)
    tiles_s = dict(tm_proj=512, tn_proj=2048, tc_filter=256, tr_spec=512, tc_spec=1024, tc_conv=256, tm_mix=512,
                   hy_split=True, gla_heads_per_step=2, hy_batch_per_step=1)
    tm_moe = 256
    tm_out = 256

    n_p = x_prompt.shape[0] * x_prompt.shape[1]
    n_s = dec_batch * dec_seq
    n_all = n_p + n_s
    mod_row_p = lambda r: 0
    mod_row_s = lambda r: 1 + r // dec_seq
    n_exp, _, d_exp = w_exp_gate.shape[1:]
    casts_p = dict(premix=[w_out[l]],
                   gla=[w_exp_gate[l].reshape(n_exp * d, d_exp), w_exp_down[l].reshape(n_exp * d_exp, d)],
                   hy_conv=[w_exp_up[l].reshape(n_exp * d, d_exp)])
    casts_s = dict(premix=[], gla=[], hy_conv=[])
    x1_p, hg_all, route_p, counts, s_f, s_b, casted = _mixer_and_router(
        x_prompt, mod, mod_row_p, zero_state, zero_state, 1, p, tiles_p, jnp.zeros((1, LANES), F32), None, 0, n_all,
        None, casts_p)
    (w_out_b,), (w_gate_b, w_down_b), (w_up_b,) = casted["premix"], casted["gla"], casted["hy_conv"]
    x1_s, hg_all, route_s, counts, _, _, _ = _mixer_and_router(
        x_sample, mod, mod_row_s, state_gla_fwd[:, l], state_gla_bwd[:, l], dec_seq // GRID_W, p, tiles_s,
        counts, hg_all, n_p, n_all, w_out_b, casts_s)

    n_tiles = n_all // tm_moe + N_GROUPS * PAIRS_PER_GROUP
    pos, src, ea, eb, valid = _route_tables(jnp.concatenate([route_p, route_s], axis=1), counts, n_tiles, tm_moe)
    y_sorted = _moe(hg_all, src, ea, eb, valid, w_gate_b.reshape(n_exp, d, d_exp), w_up_b.reshape(n_exp, d, d_exp),
                    w_down_b.reshape(n_exp, d_exp, d), n_tiles, tm_moe)
    y_p = _ffn_out(y_sorted, pos[:n_p], x1_p, mod, mod_row_p, p["g_post_ffn"], tm_out)
    y_s = _ffn_out(y_sorted, pos[n_p:], x1_s, mod, mod_row_s, p["g_post_ffn"], tm_out)
    return (y_p.reshape(x_prompt.shape), y_s.reshape(x_sample.shape),
            s_f[:, None].astype(x_prompt.dtype), s_b[:, None].astype(x_prompt.dtype))
```

```python
import functools
import math

import jax
import jax.numpy as jnp
from jax import lax
from jax.experimental import pallas as pl
from jax.experimental.pallas import tpu as pltpu

F32 = jnp.float32
BF16 = jnp.bfloat16

GRID_W = 64
GLA_HEADS = 4
GLA_RANK = 16
GLA_TAU = 16.0
GLA_CHUNK = 64
GLA_UNROLL = 4
MIX_SUBTILES = 2
HY_ORDER = 2
HY_SHORT = 3
HY_BANDS = 16
HY_TARGET = 1e-2
HY_FAST_PCT = 0.3
HY_SLOW_PCT = 1.5
N_GROUPS = 4
EXP_PER_GROUP = 4
N_EXPERTS = N_GROUPS * EXP_PER_GROUP
PAIRS_PER_GROUP = EXP_PER_GROUP * (EXP_PER_GROUP - 1) // 2
ROUTE_BUCKET_LANE = N_EXPERTS
ROUTE_RANK_LANE = N_EXPERTS + 1
EPS = 1e-6

LANES = 128
VMEM_LIMIT = 56 << 20


def _cparams(sem):
    return pltpu.CompilerParams(dimension_semantics=sem, vmem_limit_bytes=VMEM_LIMIT)


def _dot(a, b):
    return jnp.dot(a, b, preferred_element_type=F32)


def _dot_nt(a, b):
    return lax.dot_general(a, b, (((1,), (1,)), ((), ())), preferred_element_type=F32)


def _dot_tn(a, b):
    return lax.dot_general(a, b, (((0,), (0,)), ((), ())), preferred_element_type=F32)


def _split2(x):
    hi = x.astype(BF16)
    lo = (x - hi.astype(F32)).astype(BF16)
    return hi, lo


def _dot_hp(a, b):
    ah, al = _split2(a)
    bh, bl = _split2(b)
    return _dot(ah, bh) + (_dot(ah, bl) + _dot(al, bh))


def _dot_exact_lhs(t, x):
    hi, lo = _split2(x)
    return _dot(t, hi) + _dot(t, lo)


def _rms(x, g):
    return x * lax.rsqrt(jnp.mean(x * x, axis=-1, keepdims=True) + EPS) * g


def _silu(x):
    return x / (1.0 + jnp.exp(-x))


def _ada_kernel(c_ref, w_ref, b_ref, o_ref):
    s = _silu(c_ref[...]).astype(BF16)
    o_ref[...] = _dot(s, w_ref[...].astype(BF16)) + b_ref[...]


def _ada_mod(cond, w_ada, b_ada):
    rows, d = cond.shape
    n = w_ada.shape[1]
    tn = 1024
    return pl.pallas_call(
        _ada_kernel,
        grid=(n // tn,),
        in_specs=[pl.BlockSpec((rows, d), lambda j: (0, 0)),
                  pl.BlockSpec((d, tn), lambda j: (0, j)),
                  pl.BlockSpec((1, tn), lambda j: (0, j))],
        out_specs=pl.BlockSpec((rows, tn), lambda j: (0, j)),
        out_shape=jax.ShapeDtypeStruct((rows, n), F32),
        compiler_params=_cparams(("arbitrary",)),
        name="ada_mod",
    )(cond, w_ada, b_ada.reshape(1, n))


def _repack_kernel(wt_hbm, main_ref, lrh_ref, lrl_ref, buf, lr_buf, sem, lr_sem, *, layer, n_main, n_lr, tb):
    i = pl.program_id(0)
    n_i = pl.num_programs(0)
    slot = i % 2
    d = main_ref.shape[0]

    def block_copy(j, s):
        first = j * tb + jnp.where(j * tb >= n_main, n_lr, 0)
        return pltpu.make_async_copy(wt_hbm.at[layer, pl.ds(first, tb), :], buf.at[s], sem.at[s])

    lr_copy = pltpu.make_async_copy(wt_hbm.at[layer, pl.ds(n_main, n_lr), :], lr_buf, lr_sem)

    @pl.when(i == 0)
    def _():
        block_copy(0, 0).start()
        lr_copy.start()

    @pl.when(i + 1 < n_i)
    def _():
        block_copy(i + 1, 1 - slot).start()

    block_copy(i, slot).wait()
    main_ref[...] = buf[slot].T.astype(BF16)

    @pl.when(i == 0)
    def _():
        lr_copy.wait()
        lr = jnp.concatenate([lr_buf[...], jnp.zeros((LANES - n_lr, d), F32)], axis=0).T
        hi, lo = _split2(lr)
        lrh_ref[...] = hi
        lrl_ref[...] = lo


def _repack_w_in(w_in, layer, n_main, n_lr, tb):
    w_t = jnp.swapaxes(w_in, 1, 2)
    _, n_cols, d = w_t.shape
    n_out = n_cols - n_lr
    assert n_main % tb == 0 and n_out % tb == 0
    kern = functools.partial(_repack_kernel, layer=layer, n_main=n_main, n_lr=n_lr, tb=tb)
    return pl.pallas_call(
        kern,
        grid=(n_out // tb,),
        in_specs=[pl.BlockSpec(memory_space=pl.ANY)],
        out_specs=[pl.BlockSpec((d, tb), lambda i: (0, i)),
                   pl.BlockSpec((d, LANES), lambda i: (0, 0)),
                   pl.BlockSpec((d, LANES), lambda i: (0, 0))],
        out_shape=[jax.ShapeDtypeStruct((d, n_out), BF16),
                   jax.ShapeDtypeStruct((d, LANES), BF16),
                   jax.ShapeDtypeStruct((d, LANES), BF16)],
        scratch_shapes=[pltpu.VMEM((2, tb, d), F32), pltpu.VMEM((n_lr, d), F32),
                        pltpu.SemaphoreType.DMA((2,)), pltpu.SemaphoreType.DMA(())],
        compiler_params=_cparams(("arbitrary",)),
        name="repack_w_in",
    )(w_t)


def _with_cast_riders(kernel_fn, n_in, n_out, n_cast):
    def kernel(*refs):
        ins = refs[:n_in]
        cast_in = refs[n_in:n_in + n_cast]
        outs = refs[n_in + n_cast:n_in + n_cast + n_out]
        cast_out = refs[n_in + n_cast + n_out:n_in + 2 * n_cast + n_out]
        kernel_fn(*ins, *outs, *refs[n_in + 2 * n_cast + n_out:])
        for src, dst in zip(cast_in, cast_out):
            dst[...] = src[...].astype(dst.dtype)

    return kernel


def _cast_rider_specs(cast_srcs, n_steps, step_of):
    n_slabs = 1 << (n_steps.bit_length() - 1)
    specs = []
    for w in cast_srcs:
        rows, cols = w.shape
        assert rows % n_slabs == 0
        specs.append(pl.BlockSpec((rows // n_slabs, cols),
                                  lambda *idx: (jnp.minimum(step_of(*idx), n_slabs - 1), 0)))
    return specs, [jax.ShapeDtypeStruct(w.shape, BF16) for w in cast_srcs]


def _premix_kernel(x_ref, mod_ref, g_ref, w_ref, wlh_ref, wll_ref, o_ref, lr_ref, h_scr):
    @pl.when(pl.program_id(1) == 0)
    def _():
        h = _rms(x_ref[...], g_ref[...]) * (1.0 + mod_ref[0, 1:2, :]) + mod_ref[0, 0:1, :]
        hh, hl = _split2(h)
        h_scr[...] = hh
        lr_ref[...] = _dot(hh, wlh_ref[...]) + (_dot(hh, wll_ref[...]) + _dot(hl, wlh_ref[...]))

    o_ref[...] = _dot(h_scr[...], w_ref[...]).astype(o_ref.dtype)


def _premix_proj(x, mod, mod_row, g, w_main, wlr_hi, wlr_lo, tm, tn, cast_srcs):
    n, d = x.shape
    nc = w_main.shape[1]
    n_j = nc // tn
    cast_specs, cast_shapes = _cast_rider_specs(cast_srcs, (n // tm) * n_j, lambda i, j: i * n_j + j)
    kern = _with_cast_riders(_premix_kernel, 6, 2, len(cast_srcs))
    return pl.pallas_call(
        kern,
        grid=(n // tm, n_j),
        in_specs=[pl.BlockSpec((tm, d), lambda i, j: (i, 0)),
                  pl.BlockSpec((1, 6, d), lambda i, j: (mod_row(i * tm), 0, 0)),
                  pl.BlockSpec((1, d), lambda i, j: (0, 0)),
                  pl.BlockSpec((d, tn), lambda i, j: (0, j)),
                  pl.BlockSpec((d, LANES), lambda i, j: (0, 0)),
                  pl.BlockSpec((d, LANES), lambda i, j: (0, 0))] + cast_specs,
        out_specs=[pl.BlockSpec((tm, tn), lambda i, j: (i, j)),
                   pl.BlockSpec((tm, LANES), lambda i, j: (i, 0))] + cast_specs,
        out_shape=[jax.ShapeDtypeStruct((n, nc), BF16), jax.ShapeDtypeStruct((n, LANES), F32)] + cast_shapes,
        scratch_shapes=[pltpu.VMEM((tm, d), BF16)],
        compiler_params=_cparams(("arbitrary", "arbitrary")),
        name="premix_proj",
    )(x, mod, g, w_main, wlr_hi, wlr_lo, *cast_srcs)


def _log_sigmoid(x):
    return jnp.minimum(x, 0.0) - jnp.log(1.0 + jnp.exp(-jnp.abs(x)))


def _gla_kernel(q_ref, k_ref, v_ref, g_ref, lr_ref, wdf_ref, bdf_ref, wdb_ref, bdb_ref, gg_ref,
                s0f_ref, s0b_ref, y_ref, sf_ref, sb_ref,
                laf_scr, lab_scr, qf_scr, qb_scr, o_scr, uf_scr, ub_scr, df_scr, db_scr, *, seq, dk, dv, hp):
    c = GLA_CHUNK
    n_chunks = seq // c
    scale = dk ** -0.5
    heads = range(hp)
    kcols = [slice(h * dk, (h + 1) * dk) for h in heads]
    vcols = [slice(h * dv, (h + 1) * dv) for h in heads]

    lr = lr_ref[0]
    for h in heads:
        laf_scr[h] = _log_sigmoid(_dot_hp(lr, wdf_ref[h]) + bdf_ref[:, kcols[h]]) / GLA_TAU
        lab_scr[h] = _log_sigmoid(_dot_hp(lr, wdb_ref[h]) + bdb_ref[:, kcols[h]]) / GLA_TAU

    per = GLA_UNROLL
    blk = per * c
    row = lax.broadcasted_iota(jnp.int32, (blk, blk), 0)
    col = lax.broadcasted_iota(jnp.int32, (blk, blk), 1)
    same = (row // c) == (col // c)
    lower = same & (row >= col)
    upper = same & (col >= row)
    t_fwd = lower.astype(BF16)
    t_bwd = upper.astype(BF16)

    def chunk_rows(x, r):
        return jnp.concatenate([jnp.broadcast_to(x[j * c + r:j * c + r + 1], (c, dk)) for j in range(per)], axis=0)

    def block_local(m, carry):
        sl = pl.ds(pl.multiple_of(m * blk, blk), blk)
        bf = [_dot_exact_lhs(t_fwd, laf_scr[h, sl, :]) for h in heads]
        bb = [_dot_exact_lhs(t_bwd, lab_scr[h, sl, :]) for h in heads]
        tot_f = [chunk_rows(x, c - 1) for x in bf]
        tot_b = [chunk_rows(x, 0) for x in bb]
        q = [q_ref[0, sl, kcols[h]].astype(F32) * scale for h in heads]
        k = [k_ref[0, sl, kcols[h]].astype(F32) for h in heads]
        v = [v_ref[0, sl, vcols[h]] for h in heads]
        qf = [(q[h] * jnp.exp(bf[h])).astype(BF16) for h in heads]
        kf = [(k[h] * jnp.exp(-bf[h])).astype(BF16) for h in heads]
        qb = [(q[h] * jnp.exp(bb[h])).astype(BF16) for h in heads]
        kb = [(k[h] * jnp.exp(-bb[h])).astype(BF16) for h in heads]
        ksf = [(k[h] * jnp.exp(tot_f[h] - bf[h])).astype(BF16) for h in heads]
        ksb = [(k[h] * jnp.exp(tot_b[h] - bb[h])).astype(BF16) for h in heads]
        sc_f = [_dot_nt(qf[h], kf[h]) for h in heads]
        sc_b = [_dot_nt(qb[h], kb[h]) for h in heads]
        att = [(jnp.where(lower, sc_f[h], 0.0) + jnp.where(upper, sc_b[h], 0.0)).astype(BF16) for h in heads]
        o_loc = [_dot(att[h], v[h]) for h in heads]
        dec_f = [jnp.exp(x) for x in tot_f]
        dec_b = [jnp.exp(x) for x in tot_b]
        for h in heads:
            o_scr[h, sl, :] = o_loc[h]
            qf_scr[h, sl, :] = qf[h]
            qb_scr[h, sl, :] = qb[h]
        for j in range(per):
            n = m * per + j
            rows = slice(j * c, (j + 1) * c)
            for h in heads:
                uf_scr[h, n] = _dot_tn(v[h][rows], ksf[h][rows])
                ub_scr[h, n] = _dot_tn(v[h][rows], ksb[h][rows])
                df_scr[h, n] = dec_f[h][j * c:j * c + 8]
                db_scr[h, n] = dec_b[h][j * c:j * c + 8]
        return carry

    lax.fori_loop(0, n_chunks // per, block_local, 0)

    def scan_fwd(n, s):
        upd = [uf_scr[h, n] for h in heads]
        for h in heads:
            uf_scr[h, n] = s[h]
        return tuple(s[h] * df_scr[h, n][0:1, :] + upd[h] for h in heads)

    def scan_bwd(i, s):
        n = n_chunks - 1 - i
        upd = [ub_scr[h, n] for h in heads]
        for h in heads:
            ub_scr[h, n] = s[h]
        return tuple(s[h] * db_scr[h, n][0:1, :] + upd[h] for h in heads)

    s_f = lax.fori_loop(0, n_chunks, scan_fwd, tuple(s0f_ref[0, h].T for h in heads), unroll=GLA_UNROLL)
    s_b = lax.fori_loop(0, n_chunks, scan_bwd, tuple(s0b_ref[0, h].T for h in heads), unroll=GLA_UNROLL)
    for h in heads:
        sf_ref[0, h] = s_f[h].T
        sb_ref[0, h] = s_b[h].T

    def chunk_inter(n, carry):
        sl = pl.ds(pl.multiple_of(n * c, c), c)
        inter = [_dot_nt(qf_scr[h, sl, :], uf_scr[h, n].astype(BF16))
                 + _dot_nt(qb_scr[h, sl, :], ub_scr[h, n].astype(BF16)) for h in heads]
        for h in heads:
            o_scr[h, sl, :] += inter[h]
        return carry

    lax.fori_loop(0, n_chunks, chunk_inter, 0, unroll=GLA_UNROLL)

    for h in heads:
        o = _rms(o_scr[h], gg_ref[...])
        y_ref[0, :, vcols[h]] = (o * _silu(g_ref[0, :, vcols[h]].astype(F32))).astype(y_ref.dtype)


def _gla(proj, lr, wdf, bdf, wdb, bdb, g_gla, s0f, s0b, batch, seq, dk, dv, hp, cast_srcs):
    heads = GLA_HEADS
    proj3 = proj.reshape(batch, seq, proj.shape[-1])
    lr3 = lr.reshape(batch, seq, LANES)
    kdim = heads * dk
    width = heads * dv
    bk, bv = hp * dk, hp * dv
    k_blk = kdim // bk
    v_blk = 2 * kdim // bv
    g_blk = (2 * kdim + width) // bv
    n_chunks = seq // GLA_CHUNK

    def s0_map(s0):
        if s0.shape[0] == batch:
            return lambda b, h: (b, h, 0, 0)
        return lambda b, h: (0, h, 0, 0)

    n_hsteps = heads // hp
    cast_specs, cast_shapes = _cast_rider_specs(cast_srcs, batch * n_hsteps, lambda b, h: b * n_hsteps + h)
    kern = _with_cast_riders(functools.partial(_gla_kernel, seq=seq, dk=dk, dv=dv, hp=hp), 12, 3, len(cast_srcs))
    return pl.pallas_call(
        kern,
        grid=(batch, n_hsteps),
        in_specs=[pl.BlockSpec((1, seq, bk), lambda b, h: (b, 0, h)),
                  pl.BlockSpec((1, seq, bk), lambda b, h: (b, 0, k_blk + h)),
                  pl.BlockSpec((1, seq, bv), lambda b, h: (b, 0, v_blk + h)),
                  pl.BlockSpec((1, seq, bv), lambda b, h: (b, 0, g_blk + h)),
                  pl.BlockSpec((1, seq, LANES), lambda b, h: (b, 0, 0)),
                  pl.BlockSpec((hp, LANES, dk), lambda b, h: (h, 0, 0)),
                  pl.BlockSpec((1, bk), lambda b, h: (0, h)),
                  pl.BlockSpec((hp, LANES, dk), lambda b, h: (h, 0, 0)),
                  pl.BlockSpec((1, bk), lambda b, h: (0, h)),
                  pl.BlockSpec((1, dv), lambda b, h: (0, 0)),
                  pl.BlockSpec((1, hp, dk, dv), s0_map(s0f)),
                  pl.BlockSpec((1, hp, dk, dv), s0_map(s0b))] + cast_specs,
        out_specs=[pl.BlockSpec((1, seq, bv), lambda b, h: (b, 0, h)),
                   pl.BlockSpec((1, hp, dk, dv), lambda b, h: (b, h, 0, 0)),
                   pl.BlockSpec((1, hp, dk, dv), lambda b, h: (b, h, 0, 0))] + cast_specs,
        out_shape=[jax.ShapeDtypeStruct((batch, seq, width), BF16),
                   jax.ShapeDtypeStruct((batch, heads, dk, dv), F32),
                   jax.ShapeDtypeStruct((batch, heads, dk, dv), F32)] + cast_shapes,
        scratch_shapes=[pltpu.VMEM((hp, seq, dk), F32), pltpu.VMEM((hp, seq, dk), F32),
                        pltpu.VMEM((hp, seq, dk), BF16), pltpu.VMEM((hp, seq, dk), BF16),
                        pltpu.VMEM((hp, seq, dv), F32),
                        pltpu.VMEM((hp, n_chunks, dv, dk), F32), pltpu.VMEM((hp, n_chunks, dv, dk), F32),
                        pltpu.VMEM((hp, n_chunks, 8, dk), F32), pltpu.VMEM((hp, n_chunks, 8, dk), F32)],
        compiler_params=_cparams(("arbitrary", "arbitrary")),
        name="gla",
    )(proj3, proj3, proj3, proj3, lr3, wdf, bdf, wdb, bdb, g_gla, s0f, s0b, *cast_srcs)


def _filter_kernel(pe_ref, w1_ref, b1_ref, w2_ref, b2_ref, fr_ref, w3_ref, dl_ref, ad_ref, hl_ref, hm_ref, mlp_scr,
                   *, seq):
    @pl.when(pl.program_id(0) == 0)
    def _():
        fr = fr_ref[...]
        h1 = jnp.sin(fr * (_dot_hp(pe_ref[...], w1_ref[...]) + b1_ref[...]))
        mlp_scr[...] = jnp.sin(fr * (_dot_hp(h1, w2_ref[...]) + b2_ref[...]))

    h2 = mlp_scr[...]
    dec = jnp.exp(-pe_ref[:, 0:1] * dl_ref[...])
    row = lax.broadcasted_iota(jnp.int32, (seq, 1), 0)
    alt = jnp.where(row % 2 == 0, 1.0, -1.0)
    phase = row % 4
    cos_half = jnp.where(phase == 0, 1.0, jnp.where(phase == 2, -1.0, 0.0))
    sin_half = jnp.where(phase == 1, 1.0, jnp.where(phase == 3, -1.0, 0.0))
    for o in range(HY_ORDER):
        ff = _dot_hp(h2, w3_ref[o]) * dec
        fb = _dot_hp(h2, w3_ref[HY_ORDER + o]) * dec
        nrm = jnp.sum(jnp.abs(ff), axis=0, keepdims=True) + jnp.sum(jnp.abs(fb), axis=0, keepdims=True)
        ff = ff / nrm
        fb = jnp.where(row == 0, 0.0, fb / nrm)
        a = ff + fb
        d = fb - ff
        ad_ref[o, 0] = a.astype(ad_ref.dtype)
        ad_ref[o, 1] = d.astype(ad_ref.dtype)
        ad_ref[o, 2] = (alt * a).astype(ad_ref.dtype)
        ad_ref[o, 3] = (-alt * d).astype(ad_ref.dtype)
        hl_ref[o] = jnp.sum(alt * a, axis=0, keepdims=True)
        hm_ref[o] = jnp.concatenate([jnp.sum(cos_half * a, axis=0, keepdims=True),
                                     jnp.sum(sin_half * d, axis=0, keepdims=True)], axis=0) * (1.0 / seq)


def _hy_filter(pe, w1p, b1, w2, b2, freq, w3r, deltas, seq, width, tc):
    hid = w2.shape[0]
    kern = functools.partial(_filter_kernel, seq=seq)
    return pl.pallas_call(
        kern,
        grid=(width // tc,),
        in_specs=[pl.BlockSpec((seq, LANES), lambda c: (0, 0)),
                  pl.BlockSpec((LANES, hid), lambda c: (0, 0)),
                  pl.BlockSpec((1, hid), lambda c: (0, 0)),
                  pl.BlockSpec((hid, hid), lambda c: (0, 0)),
                  pl.BlockSpec((1, hid), lambda c: (0, 0)),
                  pl.BlockSpec((1, hid), lambda c: (0, 0)),
                  pl.BlockSpec((2 * HY_ORDER, hid, tc), lambda c: (0, 0, c)),
                  pl.BlockSpec((1, tc), lambda c: (0, c))],
        out_specs=[pl.BlockSpec((HY_ORDER, 4, seq, tc), lambda c: (0, 0, 0, c)),
                   pl.BlockSpec((HY_ORDER, 1, tc), lambda c: (0, 0, c)),
                   pl.BlockSpec((HY_ORDER, 2, tc), lambda c: (0, 0, c))],
        out_shape=[jax.ShapeDtypeStruct((HY_ORDER, 4, seq, width), BF16),
                   jax.ShapeDtypeStruct((HY_ORDER, 1, width), F32),
                   jax.ShapeDtypeStruct((HY_ORDER, 2, width), F32)],
        scratch_shapes=[pltpu.VMEM((seq, hid), F32)],
        compiler_params=_cparams(("arbitrary",)),
        name="hy_filter",
    )(pe, w1p, b1, w2, b2, freq, w3r, deltas)


def _spectrum_kernel(cs_ref, ad_ref, h_ref, *, seq, per_part, tr):
    r = pl.program_id(1)
    k = (r % per_part) * tr + lax.broadcasted_iota(jnp.int32, (tr, 1), 0)
    wgt = jnp.where(k == 0, 1.0, 2.0) * (0.5 / seq)
    h_ref[0] = wgt * _dot(cs_ref[0], ad_ref[0, 0])


def _hy_spectrum(cs3, ad, seq, part_rows, n_parts, width, tr, tc):
    per_part = part_rows // tr
    kern = functools.partial(_spectrum_kernel, seq=seq, per_part=per_part, tr=tr)
    return pl.pallas_call(
        kern,
        grid=(HY_ORDER, n_parts * per_part, width // tc),
        in_specs=[pl.BlockSpec((1, tr, seq), lambda o, r, c: ((r // per_part) % 2, r % per_part, 0)),
                  pl.BlockSpec((1, 1, seq, tc), lambda o, r, c: (o, r // per_part, 0, c))],
        out_specs=pl.BlockSpec((1, tr, tc), lambda o, r, c: (o, r, c)),
        out_shape=jax.ShapeDtypeStruct((HY_ORDER, n_parts * part_rows, width), F32),
        compiler_params=_cparams(("arbitrary", "arbitrary", "arbitrary")),
        name="hy_spectrum",
    )(cs3, ad)


def _short_conv_fn(seq, seg):
    row = lax.broadcasted_iota(jnp.int32, (seq, 1), 0)
    pos = row % seg

    def short_conv(u_ref, cw_ref, cb_ref, b=0):
        u = u_ref[b].astype(F32)
        prev = jnp.where(pos == 0, 0.0, pltpu.roll(u, 1, 0))
        nxt = jnp.where(pos == seg - 1, 0.0, pltpu.roll(u, seq - 1, 0))
        return prev * cw_ref[0:1, :] + u * cw_ref[1:2, :] + nxt * cw_ref[2:3, :] + cb_ref[...]

    return short_conv


def _hyconv_kernel(u0_ref, u1_ref, u2_ref, cw0_ref, cw1_ref, cw2_ref, cb0_ref, cb1_ref, cb2_ref,
                   cs_ref, h_ref, hl_ref, bias_ref, z_ref, y_scr, *, seq, seg):
    short_conv = _short_conv_fn(seq, seg)
    row = lax.broadcasted_iota(jnp.int32, (seq, 1), 0)
    alt = jnp.where(row % 2 == 0, 1.0, -1.0)
    gates = ((u1_ref, cw1_ref, cb1_ref), (u2_ref, cw2_ref, cb2_ref))
    items = range(z_ref.shape[0])
    z = [short_conv(u0_ref, cw0_ref, cb0_ref, b) for b in items]
    kb = min(seq, 512)
    for n in range(HY_ORDER):
        zb = [z[b].astype(BF16) for b in items]
        for r in range(0, seq, kb):
            xc = [_dot(cs_ref[r:r + kb, :], zb[b]) for b in items]
            xs = [_dot(cs_ref[seq + r:seq + r + kb, :], zb[b]) for b in items]
            hre = h_ref[n, r:r + kb, :]
            him = h_ref[n, seq + r:seq + r + kb, :]
            for b in items:
                y_scr[b, r:r + kb, :] = (xc[b] * hre + xs[b] * him).astype(BF16)
                y_scr[b, seq + r:seq + r + kb, :] = (xs[b] * hre - xc[b] * him).astype(BF16)
        nyq = [jnp.sum(alt * z[b], axis=0, keepdims=True) * (hl_ref[n] * (0.5 / seq)) for b in items]
        conv = [_dot(cs_ref[:seq, :], y_scr[b, :seq, :]) + _dot(cs_ref[seq:, :], y_scr[b, seq:, :]) + alt * nyq[b]
                for b in items]
        gate = [short_conv(*gates[n], b) for b in items]
        z = [gate[b] * (conv[b] + bias_ref[n] * z[b]) for b in items]
    for b in items:
        z_ref[b] = z[b].astype(z_ref.dtype)


def _hyconv_split_kernel(u0_ref, u1_ref, u2_ref, cw0_ref, cw1_ref, cw2_ref, cb0_ref, cb1_ref, cb2_ref,
                         cs_ref, h_ref, hm_ref, tw_ref, bias_ref, z_ref, g_scr, z_scr, c_scr, *, seq, seg):
    m = seq // 2
    tc = z_ref.shape[2]
    n_lane_blocks = tc // LANES
    short_conv = _short_conv_fn(seq, seg)
    row = lax.broadcasted_iota(jnp.int32, (m, 1), 0)
    alt = jnp.where(row % 2 == 0, 1.0, -1.0)
    gates = ((u1_ref, cw1_ref, cb1_ref), (u2_ref, cw2_ref, cb2_ref))
    z = short_conv(u0_ref, cw0_ref, cb0_ref)
    kb = min(m, 512)
    for n in range(HY_ORDER):
        for j in range(n_lane_blocks):
            z_scr[j] = z[:, j * LANES:(j + 1) * LANES]
        ze = jnp.concatenate([z_scr[j, pl.ds(0, m, stride=2), :] for j in range(n_lane_blocks)], axis=1)
        zo = jnp.concatenate([z_scr[j, pl.ds(1, m, stride=2), :] for j in range(n_lane_blocks)], axis=1)
        e_mid = jnp.sum(alt * ze, axis=0, keepdims=True)
        o_mid = jnp.sum(alt * zo, axis=0, keepdims=True)
        zeb = ze.astype(BF16)
        zob = zo.astype(BF16)
        for r in range(0, m, kb):
            rows = slice(r, r + kb)
            srows = slice(m + r, m + r + kb)
            ec = _dot(cs_ref[rows, :], zeb)
            es = _dot(cs_ref[srows, :], zeb)
            oc = _dot(cs_ref[rows, :], zob)
            os_ = _dot(cs_ref[srows, :], zob)
            c = tw_ref[0, rows, :]
            s = tw_ref[1, rows, :]
            pc = c * oc - s * os_
            ps = c * os_ + s * oc
            xca, xsa = ec + pc, es + ps
            xcb, xsb = ec - pc, ps - es
            har = h_ref[n, rows, :]
            hai = h_ref[n, srows, :]
            hbr = h_ref[n, 2 * m + r:2 * m + r + kb, :]
            hbi = h_ref[n, 3 * m + r:3 * m + r + kb, :]
            yar = xca * har + xsa * hai
            yai = xca * hai - xsa * har
            ybr = xcb * hbr + xsb * hbi
            ybi = xcb * hbi - xsb * hbr
            dr = yar - ybr
            di = yai + ybi
            g_scr[0, rows, :] = (yar + ybr).astype(BF16)
            g_scr[0, srows, :] = (ybi - yai).astype(BF16)
            g_scr[1, rows, :] = (c * dr - s * di).astype(BF16)
            g_scr[1, srows, :] = (-(s * dr + c * di)).astype(BF16)
        hr = hm_ref[n, 0:1, :]
        hi = hm_ref[n, 1:2, :]
        ymr = e_mid * hr + o_mid * hi
        ymi = e_mid * hi - o_mid * hr
        y_even = _dot(cs_ref[:m, :], g_scr[0, :m, :]) + _dot(cs_ref[m:, :], g_scr[0, m:, :]) + alt * ymr
        y_odd = _dot(cs_ref[:m, :], g_scr[1, :m, :]) + _dot(cs_ref[m:, :], g_scr[1, m:, :]) - alt * ymi
        for j in range(n_lane_blocks):
            c_scr[j, pl.ds(0, m, stride=2), :] = y_even[:, j * LANES:(j + 1) * LANES]
            c_scr[j, pl.ds(1, m, stride=2), :] = y_odd[:, j * LANES:(j + 1) * LANES]
        conv = jnp.concatenate([c_scr[j] for j in range(n_lane_blocks)], axis=1)
        z = short_conv(*gates[n]) * (conv + bias_ref[n] * z)
    z_ref[0] = z.astype(z_ref.dtype)


def _hy_conv(proj, conv_w, conv_b, cs, hspec, hl, hm, tw, bias, batch, seq, seg, col0, width, tc, split, bp,
             cast_srcs):
    proj3 = proj.reshape(batch, seq, proj.shape[-1])
    blk0 = col0 // tc
    per = width // tc
    n_b = batch // bp
    assert bp == 1 or not split
    once = dict(pipeline_mode=pl.Buffered(1))

    def u_spec(p):
        return pl.BlockSpec((bp, seq, tc), lambda c, b: (b, 0, blk0 + p * per + c))

    def w_spec(p, rows):
        return pl.BlockSpec((rows, tc), lambda c, b: (0, p * per + c))

    in_specs = [u_spec(0), u_spec(1), u_spec(2),
                w_spec(0, HY_SHORT), w_spec(1, HY_SHORT), w_spec(2, HY_SHORT),
                w_spec(0, 1), w_spec(1, 1), w_spec(2, 1),
                pl.BlockSpec(cs.shape, lambda c, b: (0, 0), **once),
                pl.BlockSpec((HY_ORDER, 2 * seq, tc), lambda c, b: (0, 0, c), **once)]
    bias_spec = pl.BlockSpec((HY_ORDER, 1, tc), lambda c, b: (0, 0, c))
    operands = [proj3, proj3, proj3, conv_w, conv_w, conv_w, conv_b, conv_b, conv_b, cs, hspec]
    if split:
        kern = functools.partial(_hyconv_split_kernel, seq=seq, seg=seg)
        in_specs += [pl.BlockSpec((HY_ORDER, 2, tc), lambda c, b: (0, 0, c)),
                     pl.BlockSpec((2, seq // 2, tc), lambda c, b: (0, 0, 0)), bias_spec]
        operands += [hm, tw, bias]
        scratch = [pltpu.VMEM((2, seq, tc), BF16), pltpu.VMEM((tc // LANES, seq, LANES), F32),
                   pltpu.VMEM((tc // LANES, seq, LANES), F32)]
    else:
        kern = functools.partial(_hyconv_kernel, seq=seq, seg=seg)
        in_specs += [pl.BlockSpec((HY_ORDER, 1, tc), lambda c, b: (0, 0, c)), bias_spec]
        operands += [hl, bias]
        scratch = [pltpu.VMEM((bp, 2 * seq, tc), BF16)]
    cast_specs, cast_shapes = _cast_rider_specs(cast_srcs, per * n_b, lambda c, b: c * n_b + b)
    kern = _with_cast_riders(kern, len(operands), 1, len(cast_srcs))
    return pl.pallas_call(
        kern,
        grid=(per, n_b),
        in_specs=in_specs + cast_specs,
        out_specs=[pl.BlockSpec((bp, seq, tc), lambda c, b: (b, 0, c))] + cast_specs,
        out_shape=[jax.ShapeDtypeStruct((batch, seq, width), BF16)] + cast_shapes,
        scratch_shapes=scratch,
        compiler_params=_cparams(("arbitrary", "arbitrary")),
        name="hy_conv",
    )(*operands, *cast_srcs)


def _mixout_kernel(*refs, gla_width, n_own, aliased):
    (yg_ref, zh_ref, x_ref, mod_ref, ghy_ref, gpost_ref, gpre_ref, wo_ref, wrh_ref, wrl_ref, br_ref,
     cnt_in_ref) = refs[:12]
    x1_ref, hg_ref, route_ref, cnt_ref, cnt_scr = refs[12 + int(aliased):]
    d = x_ref.shape[1]
    tm = x_ref.shape[0] // MIX_SUBTILES
    i = pl.program_id(0)

    @pl.when(i == 0)
    def _():
        cnt_scr[...] = cnt_in_ref[...]

    @pl.when(i < n_own)
    def _():
        for s in range(MIX_SUBTILES):
            _mixout_rows(slice(s * tm, (s + 1) * tm), tm, d, gla_width, yg_ref, zh_ref, x_ref, mod_ref, ghy_ref,
                         gpost_ref, gpre_ref, wo_ref, wrh_ref, wrl_ref, br_ref, x1_ref, hg_ref, route_ref, cnt_scr)

    @pl.when(i >= n_own)
    def _():
        hg_ref[...] = jnp.zeros_like(hg_ref)

    cnt_ref[...] = cnt_scr[...]


def _mixout_rows(rows, tm, d, gla_width, yg_ref, zh_ref, x_ref, mod_ref, ghy_ref, gpost_ref, gpre_ref, wo_ref,
                 wrh_ref, wrl_ref, br_ref, x1_ref, hg_ref, route_ref, cnt_scr):
    yh = _rms(zh_ref[rows, :].astype(F32), ghy_ref[...]).astype(BF16)
    y = _dot(yg_ref[rows, :], wo_ref[:gla_width, :]) + _dot(yh, wo_ref[gla_width:, :])
    x1 = x_ref[rows, :] + mod_ref[0, 2:3, :] * _rms(y, gpost_ref[...])
    x1_ref[rows, :] = x1
    h2 = _rms(x1, gpre_ref[...]) * (1.0 + mod_ref[0, 4:5, :]) + mod_ref[0, 3:4, :]
    hh, hl = _split2(h2)
    hg_ref[rows, :d] = h2
    logits = _dot(hh, wrh_ref[...]) + (_dot(hh, wrl_ref[...]) + _dot(hl, wrh_ref[...])) + br_ref[...]

    lane = lax.broadcasted_iota(jnp.int32, logits.shape, 1).astype(F32)
    neg = -jnp.inf
    is_grp = (lane >= N_EXPERTS) & (lane < N_EXPERTS + N_GROUPS)
    m = jnp.max(jnp.where(is_grp, logits, neg), axis=1, keepdims=True)
    p_grp = 1.0 / jnp.sum(jnp.where(is_grp, jnp.exp(logits - m), 0.0), axis=1, keepdims=True)
    grp = jnp.min(jnp.where(is_grp & (logits == m), lane - N_EXPERTS, 1e9), axis=1, keepdims=True)
    sel = (lane >= grp * EXP_PER_GROUP) & (lane < (grp + 1.0) * EXP_PER_GROUP)
    me = jnp.max(jnp.where(sel, logits, neg), axis=1, keepdims=True)
    pe = jnp.where(sel, jnp.exp(logits - me), -1.0)
    v1 = jnp.max(pe, axis=1, keepdims=True)
    i1 = jnp.min(jnp.where(pe == v1, lane, 1e9), axis=1, keepdims=True)
    pe2 = jnp.where(lane == i1, -1.0, pe)
    v2 = jnp.max(pe2, axis=1, keepdims=True)
    i2 = jnp.min(jnp.where(pe2 == v2, lane, 1e9), axis=1, keepdims=True)
    den = v1 + v2
    gates = jnp.where(lane == i1, v1 / den, jnp.where(lane == i2, v2 / den, 0.0)) * p_grp

    lo = jnp.minimum(i1, i2) - grp * EXP_PER_GROUP
    hi = jnp.maximum(i1, i2) - grp * EXP_PER_GROUP
    bucket = grp * PAIRS_PER_GROUP + lo * (2 * EXP_PER_GROUP - 1 - lo) * 0.5 + (hi - lo - 1.0)
    onehot = lane == bucket
    r_i = lax.broadcasted_iota(jnp.int32, (tm, tm), 0)
    c_i = lax.broadcasted_iota(jnp.int32, (tm, tm), 1)
    earlier = _dot((r_i > c_i).astype(BF16), onehot.astype(BF16)) + cnt_scr[...]
    rank = jnp.sum(jnp.where(onehot, earlier, 0.0), axis=1, keepdims=True)
    cnt_scr[...] += jnp.sum(onehot.astype(F32), axis=0, keepdims=True)
    route = jnp.where(lane == ROUTE_BUCKET_LANE, bucket, jnp.where(lane == ROUTE_RANK_LANE, rank, gates))
    hg_ref[rows, d:] = route
    route_ref[:, rows] = route.T[ROUTE_BUCKET_LANE:ROUTE_BUCKET_LANE + 8, :]


def _mix_out(yg, zh, x, mod, mod_row, g_hy, g_post, g_pre, w_out, wr_hi, wr_lo, b_r, cnt_in, hg_all, row0, n_total, tm):
    n, d = x.shape
    gw = yg.shape[1]
    hw = zh.shape[1]
    aliased = hg_all is not None
    n_own = n // tm
    n_steps = n_own if aliased else n_total // tm
    assert row0 % tm == 0 and n_total % tm == 0 and (row0 + n == n_total if aliased else row0 == 0)
    kern = functools.partial(_mixout_kernel, gla_width=gw, n_own=n_own, aliased=aliased)
    row = lambda i: (jnp.minimum(i, n_own - 1), 0)
    fixed = lambda i: (0, 0)
    in_specs = [pl.BlockSpec((tm, gw), row),
                pl.BlockSpec((tm, hw), row),
                pl.BlockSpec((tm, d), row),
                pl.BlockSpec((1, 6, d), lambda i: (mod_row(jnp.minimum(i, n_own - 1) * tm), 0, 0)),
                pl.BlockSpec((1, hw), fixed),
                pl.BlockSpec((1, d), fixed),
                pl.BlockSpec((1, d), fixed),
                pl.BlockSpec((gw + hw, d), fixed),
                pl.BlockSpec((d, LANES), fixed),
                pl.BlockSpec((d, LANES), fixed),
                pl.BlockSpec((1, LANES), fixed),
                pl.BlockSpec((1, LANES), fixed)]
    operands = [yg, zh, x, mod, g_hy, g_post, g_pre, w_out, wr_hi, wr_lo, b_r, cnt_in]
    if aliased:
        in_specs.append(pl.BlockSpec(memory_space=pl.ANY))
        operands.append(hg_all)
    return pl.pallas_call(
        kern,
        grid=(n_steps,),
        in_specs=in_specs,
        out_specs=[pl.BlockSpec((tm, d), row),
                   pl.BlockSpec((tm, d + LANES), lambda i: (row0 // tm + i, 0)),
                   pl.BlockSpec((8, tm), lambda i: (0, jnp.minimum(i, n_own - 1))),
                   pl.BlockSpec((1, LANES), fixed)],
        out_shape=[jax.ShapeDtypeStruct((n, d), F32),
                   jax.ShapeDtypeStruct((n_total, d + LANES), F32),
                   jax.ShapeDtypeStruct((8, n), F32),
                   jax.ShapeDtypeStruct((1, LANES), F32)],
        scratch_shapes=[pltpu.VMEM((1, LANES), F32)],
        input_output_aliases={len(operands) - 1: 1} if aliased else {},
        compiler_params=_cparams(("arbitrary",)),
        name="mix_out",
    )(*operands)


def _row_gather_copy(src_hbm, row, buf, slot, r, sem):
    return pltpu.make_async_copy(src_hbm.at[pl.ds(row, 1), :], buf.at[slot, pl.ds(r, 1), :], sem.at[slot])


def _start_row_gather(idx_ref, base, src_hbm, buf, slot, sem, tm):
    def body(r, carry):
        _row_gather_copy(src_hbm, idx_ref[base + r], buf, slot, r, sem).start()
        return carry

    lax.fori_loop(0, tm, body, 0, unroll=8)


def _start_row_gather_inline(idx_ref, base, src_hbm, buf, slot, sem, tm):
    for r in range(tm):
        _row_gather_copy(src_hbm, idx_ref[base + r], buf, slot, r, sem).start()


def _wait_row_gather(src_hbm, buf, slot, sem, tm):
    pltpu.make_async_copy(src_hbm.at[pl.ds(0, tm), :], buf.at[slot], sem.at[slot]).wait()


def _moe_kernel(src_ref, ea_ref, eb_ref, valid_ref, hg_hbm,
                wga_ref, wua_ref, wda_ref, wgb_ref, wub_ref, wdb_ref, y_ref, buf, sem, *, tm, d):
    t = pl.program_id(0)
    slot = t % 2

    @pl.when(t == 0)
    def _():
        _start_row_gather(src_ref, 0, hg_hbm, buf, 0, sem, tm)

    @pl.when(valid_ref[t] == 1)
    def _():
        _wait_row_gather(hg_hbm, buf, slot, sem, tm)
        _start_row_gather_inline(src_ref, (t + 1) * tm, hg_hbm, buf, 1 - slot, sem, tm)
        h = buf[slot, :, :d].astype(BF16)
        gates = buf[slot, :, d:]
        lane = lax.broadcasted_iota(jnp.int32, gates.shape, 1)

        def expert(wg_ref, wu_ref, wd_ref, e):
            he = (_silu(_dot(h, wg_ref[0])) * _dot(h, wu_ref[0])).astype(BF16)
            gate = jnp.sum(jnp.where(lane == e, gates, 0.0), axis=1, keepdims=True)
            return gate * _dot(he, wd_ref[0])

        y_ref[...] = expert(wga_ref, wua_ref, wda_ref, ea_ref[t]) + expert(wgb_ref, wub_ref, wdb_ref, eb_ref[t])

    @pl.when(valid_ref[t] == 0)
    def _():
        @pl.when(valid_ref[jnp.maximum(t - 1, 0)] == 1)
        def _():
            _wait_row_gather(hg_hbm, buf, slot, sem, tm)

        y_ref[...] = jnp.zeros_like(y_ref)


def _moe(hg, src, ea, eb, valid, wg, wu, wd, n_tiles, tm):
    d = hg.shape[1] - LANES
    _, _, de = wg.shape
    kern = functools.partial(_moe_kernel, tm=tm, d=d)
    wa = lambda t, src, ea, eb, valid: (ea[t], 0, 0)
    wb = lambda t, src, ea, eb, valid: (eb[t], 0, 0)
    return pl.pallas_call(
        kern,
        grid_spec=pltpu.PrefetchScalarGridSpec(
            num_scalar_prefetch=4,
            grid=(n_tiles,),
            in_specs=[pl.BlockSpec(memory_space=pl.ANY),
                      pl.BlockSpec((1, d, de), wa), pl.BlockSpec((1, d, de), wa), pl.BlockSpec((1, de, d), wa),
                      pl.BlockSpec((1, d, de), wb), pl.BlockSpec((1, d, de), wb), pl.BlockSpec((1, de, d), wb)],
            out_specs=pl.BlockSpec((tm, d), lambda t, src, ea, eb, valid: (t, 0)),
            scratch_shapes=[pltpu.VMEM((2, tm, d + LANES), F32), pltpu.SemaphoreType.DMA((2,))]),
        out_shape=jax.ShapeDtypeStruct((n_tiles * tm, d), F32),
        compiler_params=_cparams(("arbitrary",)),
        name="moe",
    )(src, ea, eb, valid, hg, wg, wu, wd, wg, wu, wd)


def _ffn_out_kernel(pos_ref, y_hbm, x1_ref, mod_ref, gpost_ref, o_ref, buf, sem, *, tm):
    i = pl.program_id(0)
    n_i = pl.num_programs(0)
    slot = i % 2

    @pl.when(i == 0)
    def _():
        _start_row_gather(pos_ref, 0, y_hbm, buf, 0, sem, tm)

    _wait_row_gather(y_hbm, buf, slot, sem, tm)

    @pl.when(i + 1 < n_i)
    def _():
        _start_row_gather_inline(pos_ref, (i + 1) * tm, y_hbm, buf, 1 - slot, sem, tm)
        o_ref[...] = x1_ref[...] + mod_ref[0, 5:6, :] * _rms(buf[slot], gpost_ref[...])

    @pl.when(i + 1 == n_i)
    def _():
        o_ref[...] = x1_ref[...] + mod_ref[0, 5:6, :] * _rms(buf[slot], gpost_ref[...])


def _ffn_out(y_sorted, pos, x1, mod, mod_row, g_post, tm):
    n, d = x1.shape
    kern = functools.partial(_ffn_out_kernel, tm=tm)
    return pl.pallas_call(
        kern,
        grid_spec=pltpu.PrefetchScalarGridSpec(
            num_scalar_prefetch=1,
            grid=(n // tm,),
            in_specs=[pl.BlockSpec(memory_space=pl.ANY),
                      pl.BlockSpec((tm, d), lambda i, pos: (i, 0)),
                      pl.BlockSpec((1, 6, d), lambda i, pos: (mod_row(i * tm), 0, 0)),
                      pl.BlockSpec((1, d), lambda i, pos: (0, 0))],
            out_specs=pl.BlockSpec((tm, d), lambda i, pos: (i, 0)),
            scratch_shapes=[pltpu.VMEM((2, tm, d), F32), pltpu.SemaphoreType.DMA((2,))]),
        out_shape=jax.ShapeDtypeStruct((n, d), F32),
        compiler_params=_cparams(("arbitrary",)),
        name="ffn_out",
    )(pos, y_sorted, x1, mod, g_post)


def _route_tables(route, counts, n_tiles, tm):
    n = route.shape[1]
    n_buckets = N_GROUPS * PAIRS_PER_GROUP
    cnt = counts[0, :n_buckets].astype(jnp.int32)
    padded = (cnt + tm - 1) // tm * tm
    ends = jnp.cumsum(padded)
    starts = ends - padded
    bucket = route[0].astype(jnp.int32)
    rank = route[ROUTE_RANK_LANE - ROUTE_BUCKET_LANE].astype(jnp.int32)
    pos = starts[bucket] + rank
    src = (jnp.arange(n_tiles * tm, dtype=jnp.int32) % n).at[pos].set(jnp.arange(n, dtype=jnp.int32))
    n_valid = ends[-1] // tm
    tile = jnp.arange(n_tiles, dtype=jnp.int32)
    used = jnp.minimum(tile, n_valid - 1)
    tile_bucket = jnp.sum((ends[None, :] <= (used * tm)[:, None]).astype(jnp.int32), axis=1)
    grp = tile_bucket // PAIRS_PER_GROUP
    pair = tile_bucket % PAIRS_PER_GROUP
    pair_lo = jnp.array([a for a in range(EXP_PER_GROUP) for b in range(a + 1, EXP_PER_GROUP)], jnp.int32)
    pair_hi = jnp.array([b for a in range(EXP_PER_GROUP) for b in range(a + 1, EXP_PER_GROUP)], jnp.int32)
    ea = grp * EXP_PER_GROUP + pair_lo[pair]
    eb = grp * EXP_PER_GROUP + pair_hi[pair]
    valid = (tile < n_valid).astype(jnp.int32)
    return pos, src, ea, eb, valid


def _dft_kernel(cs_ref, *, seq, tr):
    k = pl.program_id(0) * tr + lax.broadcasted_iota(jnp.int32, (tr, LANES), 0)
    j = lax.broadcasted_iota(jnp.int32, (tr, LANES), 1)
    period = 2 * seq

    def table(step):
        ang = ((k * j * step) % period).astype(F32) * (math.pi / seq)
        return jnp.cos(ang), jnp.sin(ang)

    c0, s0 = table(1)
    c1, s1 = table(LANES)
    for t1 in range(seq // LANES):
        cols = slice(t1 * LANES, (t1 + 1) * LANES)
        ca = c1[:, t1:t1 + 1]
        sa = s1[:, t1:t1 + 1]
        cs_ref[0, :, cols] = (ca * c0 - sa * s0).astype(cs_ref.dtype)
        cs_ref[1, :, cols] = (sa * c0 + ca * s0).astype(cs_ref.dtype)


def _dft_cos_sin(seq, n_rows, tr):
    kern = functools.partial(_dft_kernel, seq=seq, tr=tr)
    return pl.pallas_call(
        kern,
        grid=(n_rows // tr,),
        out_specs=pl.BlockSpec((2, tr, seq), lambda r: (0, r, 0)),
        out_shape=jax.ShapeDtypeStruct((2, n_rows, seq), BF16),
        compiler_params=_cparams(("arbitrary",)),
        name="dft_matrix",
    )()


def _positional_features(seq):
    t = jnp.arange(seq, dtype=F32)
    t01 = t / max(seq - 1, 1)
    ang = 2.0 * math.pi * t / seq
    bands = jnp.linspace(1e-4, HY_BANDS - 1, HY_BANDS, dtype=F32)
    pe = jnp.concatenate([t01[:, None], jnp.cos(ang[:, None] * bands), -jnp.sin(ang[:, None] * bands)], axis=-1)
    return jnp.pad(pe, ((0, 0), (0, LANES - pe.shape[1])))


def _mixer_and_router(x3, mod, mod_row, s0f, s0b, n_seg, p, tiles, cnt_in, hg_all, row0, n_total, w_out, casts):
    batch, seq, d = x3.shape
    x = x3.reshape(batch * seq, d)
    dk, dv = p["dk"], p["dv"]
    hy_width = p["hy_width"]
    casted = {}

    proj, lr, *casted["premix"] = _premix_proj(x, mod, mod_row, p["g_pre_mix"], p["w_main"], p["wlr_hi"],
                                               p["wlr_lo"], tiles["tm_proj"], tiles["tn_proj"], casts["premix"])
    if w_out is None:
        w_out = casted["premix"][0]
    y_gla, s_f, s_b, *casted["gla"] = _gla(proj, lr, p["wdf"], p["bdf"], p["wdb"], p["bdb"], p["g_gla"], s0f, s0b,
                                           batch, seq, dk, dv, tiles["gla_heads_per_step"], casts["gla"])

    ad, hl, hm = _hy_filter(_positional_features(seq), p["hy_w1"], p["hy_b1"], p["hy_w2"], p["hy_b2"], p["hy_freq"],
                            p["hy_w3"], p["hy_deltas"], seq, hy_width, tiles["tc_filter"])
    tr, tc_conv = tiles["tr_spec"], tiles["tc_conv"]
    if tiles["hy_split"]:
        half = seq // 2
        hspec = _hy_spectrum(_dft_cos_sin(seq, half, tr), ad, seq, half, 4, hy_width, tr, tiles["tc_spec"])
        cs = _dft_cos_sin(half, half, tr).reshape(seq, half)
        ang = jnp.arange(half, dtype=F32) * (math.pi / seq)
        tw = jnp.broadcast_to(jnp.stack([jnp.cos(ang), jnp.sin(ang)])[:, :, None], (2, half, tc_conv))
    else:
        cs3 = _dft_cos_sin(seq, seq, tr)
        hspec = _hy_spectrum(cs3, ad, seq, seq, 2, hy_width, tr, tiles["tc_spec"])
        cs = cs3.reshape(2 * seq, seq)
        tw = None
    z_hy, *casted["hy_conv"] = _hy_conv(proj, p["hy_conv_w"], p["hy_conv_b"], cs, hspec, hl, hm, tw, p["hy_bias"],
                                        batch, seq, seq // n_seg, p["hy_col0"], hy_width, tc_conv, tiles["hy_split"],
                                        tiles["hy_batch_per_step"], casts["hy_conv"])

    x1, hg_all, route, counts = _mix_out(y_gla.reshape(batch * seq, -1), z_hy.reshape(batch * seq, -1), x, mod,
                                         mod_row, p["g_hy"], p["g_post_mix"], p["g_pre_ffn"], w_out, p["wr_hi"],
                                         p["wr_lo"], p["b_r"], cnt_in, hg_all, row0, n_total, tiles["tm_mix"])
    return x1, hg_all, route, counts, s_f, s_b, casted


def kernel(x_prompt, x_sample, c, state_gla_fwd, state_gla_bwd, c_ctx, w_ada, b_ada, g_pre_mix, g_post_mix, g_pre_ffn, g_post_ffn, w_in, w_dec_f, b_dec_f, w_dec_b, b_dec_b, g_gla, hy_conv_w, hy_conv_b, hy_w1, hy_b1, hy_w2, hy_b2, hy_w3, hy_freq, hy_bias, g_hy, w_out, w_router_grp, b_router_grp, w_router_exp, b_router_exp, w_exp_gate, w_exp_up, w_exp_down):
    depth = w_ada.shape[0]
    assert depth == 1
    l = 0
    d = x_prompt.shape[-1]
    dec_batch = x_sample.shape[0]
    heads = GLA_HEADS
    dk, dv = state_gla_fwd.shape[-2:]
    kdim = heads * dk
    gla_width = heads * dv
    hy_width = g_hy.shape[-1]
    hid = hy_w2.shape[-1]

    cond = jnp.concatenate([c_ctx[None, :], c], axis=0)
    cond = jnp.pad(cond, ((0, -cond.shape[0] % 8), (0, 0)))
    mod = _ada_mod(cond, w_ada[l], b_ada[l]).reshape(cond.shape[0], 6, d)

    n_main = 2 * kdim + 2 * gla_width
    w_main, wlr_hi, wlr_lo = _repack_w_in(w_in, l, n_main, 2 * GLA_RANK, 256)

    def dec_weight(w, first_row):
        wh = w.reshape(GLA_RANK, heads, dk).transpose(1, 0, 2)
        return jnp.pad(wh, ((0, 0), (first_row, LANES - GLA_RANK - first_row), (0, 0)))

    deltas = jnp.abs(jnp.linspace(math.log(HY_TARGET) / HY_SLOW_PCT, math.log(HY_TARGET) / HY_FAST_PCT, hy_width,
                                  dtype=F32))
    w_r = jnp.pad(jnp.concatenate([w_router_exp[l], w_router_grp[l]], axis=1),
                  ((0, 0), (0, LANES - N_EXPERTS - N_GROUPS)))
    wr_hi, wr_lo = _split2(w_r)
    b_r = jnp.pad(jnp.concatenate([b_router_exp[l], b_router_grp[l]]), (0, LANES - N_EXPERTS - N_GROUPS))

    p = dict(
        dk=dk, dv=dv, hy_width=hy_width, hy_col0=n_main,
        g_pre_mix=g_pre_mix[l][None, :], g_post_mix=g_post_mix[l][None, :],
        g_pre_ffn=g_pre_ffn[l][None, :], g_post_ffn=g_post_ffn[l][None, :],
        w_main=w_main, wlr_hi=wlr_hi, wlr_lo=wlr_lo,
        wdf=dec_weight(w_dec_f[l], 0), bdf=b_dec_f[l][None, :],
        wdb=dec_weight(w_dec_b[l], GLA_RANK), bdb=b_dec_b[l][None, :],
        g_gla=g_gla[l][None, :],
        hy_conv_w=hy_conv_w[l], hy_conv_b=hy_conv_b[l][None, :],
        hy_w1=jnp.pad(hy_w1[l], ((0, LANES - hy_w1.shape[1]), (0, 0))), hy_b1=hy_b1[l][None, :],
        hy_w2=hy_w2[l], hy_b2=hy_b2[l][None, :], hy_freq=hy_freq[l][None, :],
        hy_w3=hy_w3[l].reshape(hid, 2 * HY_ORDER, hy_width).transpose(1, 0, 2), hy_deltas=deltas[None, :],
        hy_bias=hy_bias[l][:, None, :], g_hy=g_hy[l][None, :],
        wr_hi=wr_hi, wr_lo=wr_lo, b_r=b_r[None, :],
    )

    zero_state = jnp.zeros((1, heads, dk, dv), F32)
    dec_seq = x_sample.shape[1]
    tiles_p = dict(tm_proj=512, tn_proj=2048, tc_filter=256, tr_spec=256, tc_spec=512, tc_conv=512, tm_mix=512,
                   hy_split=False, gla_heads_per_step=4, hy_batch_per_step=2)
    tiles_s = dict(tm_proj=512, tn_proj=2048, tc_filter=256, tr_spec=512, tc_spec=1024, tc_conv=256, tm_mix=512,
                   hy_split=True, gla_heads_per_step=2, hy_batch_per_step=1)
    tm_moe = 256
    tm_out = 256

    n_p = x_prompt.shape[0] * x_prompt.shape[1]
    n_s = dec_batch * dec_seq
    n_all = n_p + n_s
    mod_row_p = lambda r: 0
    mod_row_s = lambda r: 1 + r // dec_seq
    n_exp, _, d_exp = w_exp_gate.shape[1:]
    casts_p = dict(premix=[w_out[l]],
                   gla=[w_exp_gate[l].reshape(n_exp * d, d_exp), w_exp_down[l].reshape(n_exp * d_exp, d)],
                   hy_conv=[w_exp_up[l].reshape(n_exp * d, d_exp)])
    casts_s = dict(premix=[], gla=[], hy_conv=[])
    x1_p, hg_all, route_p, counts, s_f, s_b, casted = _mixer_and_router(
        x_prompt, mod, mod_row_p, zero_state, zero_state, 1, p, tiles_p, jnp.zeros((1, LANES), F32), None, 0, n_all,
        None, casts_p)
    (w_out_b,), (w_gate_b, w_down_b), (w_up_b,) = casted["premix"], casted["gla"], casted["hy_conv"]
    x1_s, hg_all, route_s, counts, _, _, _ = _mixer_and_router(
        x_sample, mod, mod_row_s, state_gla_fwd[:, l], state_gla_bwd[:, l], dec_seq // GRID_W, p, tiles_s,
        counts, hg_all, n_p, n_all, w_out_b, casts_s)

    n_tiles = n_all // tm_moe + N_GROUPS * PAIRS_PER_GROUP
    pos, src, ea, eb, valid = _route_tables(jnp.concatenate([route_p, route_s], axis=1), counts, n_tiles, tm_moe)
    y_sorted = _moe(hg_all, src, ea, eb, valid, w_gate_b.reshape(n_exp, d, d_exp), w_up_b.reshape(n_exp, d, d_exp),
                    w_down_b.reshape(n_exp, d_exp, d), n_tiles, tm_moe)
    y_p = _ffn_out(y_sorted, pos[:n_p], x1_p, mod, mod_row_p, p["g_post_ffn"], tm_out)
    y_s = _ffn_out(y_sorted, pos[n_p:], x1_s, mod, mod_row_s, p["g_post_ffn"], tm_out)
    return (y_p.reshape(x_prompt.shape), y_s.reshape(x_sample.shape),
            s_f[:, None].astype(x_prompt.dtype), s_b[:, None].astype(x_prompt.dtype))
```

```python
import functools
import math

import jax
import jax.numpy as jnp
from jax import lax
from jax.experimental import pallas as pl
from jax.experimental.pallas import tpu as pltpu

F32 = jnp.float32
BF16 = jnp.bfloat16

GRID_W = 64
GLA_HEADS = 4
GLA_RANK = 16
GLA_TAU = 16.0
GLA_CHUNK = 64
GLA_UNROLL = 4
MIX_SUBTILES = 2
HY_ORDER = 2
HY_SHORT = 3
HY_BANDS = 16
HY_TARGET = 1e-2
HY_FAST_PCT = 0.3
HY_SLOW_PCT = 1.5
N_GROUPS = 4
EXP_PER_GROUP = 4
N_EXPERTS = N_GROUPS * EXP_PER_GROUP
PAIRS_PER_GROUP = EXP_PER_GROUP * (EXP_PER_GROUP - 1) // 2
ROUTE_BUCKET_LANE = N_EXPERTS
ROUTE_RANK_LANE = N_EXPERTS + 1
EPS = 1e-6

LANES = 128
SUBLANES = 8
MXU_DIM = 256
VMEM_LIMIT = 56 << 20
REPACK_COLS = MXU_DIM
FREQ_BLOCK = 2 * MXU_DIM


def _tile_plan(seq):
    long_seq = seq >= 1024
    return dict(
        tm_proj=512, tn_proj=2048,
        tc_filter=256,
        tr_spec=512 if long_seq else 256,
        tc_spec=1024 if long_seq else 512,
        hy_split=long_seq,
        tc_conv=256 if long_seq else 512,
        hy_batch_per_step=1 if long_seq else 2,
        gla_heads_per_step=2 if long_seq else 4,
        tm_mix=512,
    )


def _cparams(sem):
    return pltpu.CompilerParams(dimension_semantics=sem, vmem_limit_bytes=VMEM_LIMIT)


def _dot(a, b):
    return jnp.dot(a, b, preferred_element_type=F32)


def _dot_nt(a, b):
    return lax.dot_general(a, b, (((1,), (1,)), ((), ())), preferred_element_type=F32)


def _dot_tn(a, b):
    return lax.dot_general(a, b, (((0,), (0,)), ((), ())), preferred_element_type=F32)


def _split2(x):
    hi = x.astype(BF16)
    lo = (x - hi.astype(F32)).astype(BF16)
    return hi, lo


def _dot_hp(a, b):
    ah, al = _split2(a)
    bh, bl = _split2(b)
    return _dot(ah, bh) + (_dot(ah, bl) + _dot(al, bh))


def _dot_exact_lhs(t, x):
    hi, lo = _split2(x)
    return _dot(t, hi) + _dot(t, lo)


def _rms(x, g):
    return x * lax.rsqrt(jnp.mean(x * x, axis=-1, keepdims=True) + EPS) * g


def _silu(x):
    return x / (1.0 + jnp.exp(-x))


def _ada_kernel(c_ref, w_ref, b_ref, o_ref):
    s = _silu(c_ref[...]).astype(BF16)
    o_ref[...] = _dot(s, w_ref[...].astype(BF16)) + b_ref[...]


def _ada_mod(cond, w_ada, b_ada):
    rows, d = cond.shape
    n = w_ada.shape[1]
    tn = 1024
    return pl.pallas_call(
        _ada_kernel,
        grid=(n // tn,),
        in_specs=[pl.BlockSpec((rows, d), lambda j: (0, 0)),
                  pl.BlockSpec((d, tn), lambda j: (0, j)),
                  pl.BlockSpec((1, tn), lambda j: (0, j))],
        out_specs=pl.BlockSpec((rows, tn), lambda j: (0, j)),
        out_shape=jax.ShapeDtypeStruct((rows, n), F32),
        compiler_params=_cparams(("arbitrary",)),
        name="ada_mod",
    )(cond, w_ada, b_ada.reshape(1, n))


def _repack_kernel(wt_hbm, main_ref, lr_ref, buf, lr_buf, sem, lr_sem, *, layer, n_main, n_lr, tb):
    i = pl.program_id(0)
    n_i = pl.num_programs(0)
    slot = i % 2
    d = main_ref.shape[0]

    def block_copy(j, s):
        first = j * tb + jnp.where(j * tb >= n_main, n_lr, 0)
        return pltpu.make_async_copy(wt_hbm.at[layer, pl.ds(first, tb), :], buf.at[s], sem.at[s])

    lr_copy = pltpu.make_async_copy(wt_hbm.at[layer, pl.ds(n_main, n_lr), :], lr_buf, lr_sem)

    @pl.when(i == 0)
    def _():
        block_copy(0, 0).start()
        lr_copy.start()

    @pl.when(i + 1 < n_i)
    def _():
        block_copy(i + 1, 1 - slot).start()

    block_copy(i, slot).wait()
    main_ref[...] = buf[slot].T.astype(BF16)

    @pl.when(i == 0)
    def _():
        lr_copy.wait()
        lr = jnp.concatenate([lr_buf[...], jnp.zeros((LANES - n_lr, d), F32)], axis=0).T
        lr_ref[...] = lr.astype(BF16)


def _repack_w_in(w_in, layer, n_main, n_lr, tb):
    w_t = jnp.swapaxes(w_in, 1, 2)
    _, n_cols, d = w_t.shape
    n_out = n_cols - n_lr
    assert n_main % tb == 0 and n_out % tb == 0
    kern = functools.partial(_repack_kernel, layer=layer, n_main=n_main, n_lr=n_lr, tb=tb)
    return pl.pallas_call(
        kern,
        grid=(n_out // tb,),
        in_specs=[pl.BlockSpec(memory_space=pl.ANY)],
        out_specs=[pl.BlockSpec((d, tb), lambda i: (0, i)),
                   pl.BlockSpec((d, LANES), lambda i: (0, 0))],
        out_shape=[jax.ShapeDtypeStruct((d, n_out), BF16),
                   jax.ShapeDtypeStruct((d, LANES), BF16)],
        scratch_shapes=[pltpu.VMEM((2, tb, d), F32), pltpu.VMEM((n_lr, d), F32),
                        pltpu.SemaphoreType.DMA((2,)), pltpu.SemaphoreType.DMA(())],
        compiler_params=_cparams(("arbitrary",)),
        name="repack_w_in",
    )(w_t)


def _with_cast_riders(kernel_fn, n_in, n_out, n_cast):
    def kernel(*refs):
        ins = refs[:n_in]
        cast_in = refs[n_in:n_in + n_cast]
        outs = refs[n_in + n_cast:n_in + n_cast + n_out]
        cast_out = refs[n_in + n_cast + n_out:n_in + 2 * n_cast + n_out]
        kernel_fn(*ins, *outs, *refs[n_in + 2 * n_cast + n_out:])
        for src, dst in zip(cast_in, cast_out):
            dst[...] = src[...].astype(dst.dtype)

    return kernel


def _cast_rider_specs(cast_srcs, n_steps, step_of):
    n_slabs = 1 << (n_steps.bit_length() - 1)
    specs = []
    for w in cast_srcs:
        rows, cols = w.shape
        assert rows % n_slabs == 0
        specs.append(pl.BlockSpec((rows // n_slabs, cols),
                                  lambda *idx: (jnp.minimum(step_of(*idx), n_slabs - 1), 0)))
    return specs, [jax.ShapeDtypeStruct(w.shape, BF16) for w in cast_srcs]


def _premix_kernel(x_ref, mod_ref, g_ref, w_ref, wlr_ref, o_ref, lr_ref, h_scr):
    @pl.when(pl.program_id(1) == 0)
    def _():
        h = _rms(x_ref[...], g_ref[...]) * (1.0 + mod_ref[0, 1:2, :]) + mod_ref[0, 0:1, :]
        h_scr[...] = h.astype(BF16)
        lr_ref[...] = _dot(h_scr[...], wlr_ref[...])

    o_ref[...] = _dot(h_scr[...], w_ref[...]).astype(o_ref.dtype)


def _premix_proj(x, mod, mod_row, g, w_main, w_lr, tm, tn, cast_srcs):
    n, d = x.shape
    nc = w_main.shape[1]
    n_j = nc // tn
    cast_specs, cast_shapes = _cast_rider_specs(cast_srcs, (n // tm) * n_j, lambda i, j: i * n_j + j)
    kern = _with_cast_riders(_premix_kernel, 5, 2, len(cast_srcs))
    return pl.pallas_call(
        kern,
        grid=(n // tm, n_j),
        in_specs=[pl.BlockSpec((tm, d), lambda i, j: (i, 0)),
                  pl.BlockSpec((1, 6, d), lambda i, j: (mod_row(i * tm), 0, 0)),
                  pl.BlockSpec((1, d), lambda i, j: (0, 0)),
                  pl.BlockSpec((d, tn), lambda i, j: (0, j)),
                  pl.BlockSpec((d, LANES), lambda i, j: (0, 0))] + cast_specs,
        out_specs=[pl.BlockSpec((tm, tn), lambda i, j: (i, j)),
                   pl.BlockSpec((tm, LANES), lambda i, j: (i, 0))] + cast_specs,
        out_shape=[jax.ShapeDtypeStruct((n, nc), BF16), jax.ShapeDtypeStruct((n, LANES), F32)] + cast_shapes,
        scratch_shapes=[pltpu.VMEM((tm, d), BF16)],
        compiler_params=_cparams(("arbitrary", "arbitrary")),
        name="premix_proj",
    )(x, mod, g, w_main, w_lr, *cast_srcs)


def _log_sigmoid(x):
    return jnp.minimum(x, 0.0) - jnp.log(1.0 + jnp.exp(-jnp.abs(x)))


def _gla_kernel(q_ref, k_ref, v_ref, g_ref, lr_ref, wdf_ref, bdf_ref, wdb_ref, bdb_ref, gg_ref,
                s0f_ref, s0b_ref, y_ref, sf_ref, sb_ref,
                laf_scr, lab_scr, qf_scr, qb_scr, o_scr, uf_scr, ub_scr, df_scr, db_scr, *, seq, dk, dv, hp):
    c = GLA_CHUNK
    n_chunks = seq // c
    scale = dk ** -0.5
    heads = range(hp)
    kcols = [slice(h * dk, (h + 1) * dk) for h in heads]
    vcols = [slice(h * dv, (h + 1) * dv) for h in heads]

    lr = lr_ref[0]
    for h in heads:
        laf_scr[h] = _log_sigmoid(_dot_hp(lr, wdf_ref[h]) + bdf_ref[:, kcols[h]]) / GLA_TAU
        lab_scr[h] = _log_sigmoid(_dot_hp(lr, wdb_ref[h]) + bdb_ref[:, kcols[h]]) / GLA_TAU

    per = GLA_UNROLL
    blk = per * c
    row = lax.broadcasted_iota(jnp.int32, (blk, blk), 0)
    col = lax.broadcasted_iota(jnp.int32, (blk, blk), 1)
    same = (row // c) == (col // c)
    lower = same & (row >= col)
    upper = same & (col >= row)
    t_fwd = lower.astype(BF16)
    t_bwd = upper.astype(BF16)

    def chunk_rows(x, r):
        return jnp.concatenate([jnp.broadcast_to(x[j * c + r:j * c + r + 1], (c, dk)) for j in range(per)], axis=0)

    def block_local(m, carry):
        sl = pl.ds(pl.multiple_of(m * blk, blk), blk)
        bf = [_dot_exact_lhs(t_fwd, laf_scr[h, sl, :]) for h in heads]
        bb = [_dot_exact_lhs(t_bwd, lab_scr[h, sl, :]) for h in heads]
        tot_f = [chunk_rows(x, c - 1) for x in bf]
        tot_b = [chunk_rows(x, 0) for x in bb]
        q = [q_ref[0, sl, kcols[h]].astype(F32) * scale for h in heads]
        k = [k_ref[0, sl, kcols[h]].astype(F32) for h in heads]
        v = [v_ref[0, sl, vcols[h]] for h in heads]
        qf = [(q[h] * jnp.exp(bf[h])).astype(BF16) for h in heads]
        kf = [(k[h] * jnp.exp(-bf[h])).astype(BF16) for h in heads]
        qb = [(q[h] * jnp.exp(bb[h])).astype(BF16) for h in heads]
        kb = [(k[h] * jnp.exp(-bb[h])).astype(BF16) for h in heads]
        ksf = [(k[h] * jnp.exp(tot_f[h] - bf[h])).astype(BF16) for h in heads]
        ksb = [(k[h] * jnp.exp(tot_b[h] - bb[h])).astype(BF16) for h in heads]
        sc_f = [_dot_nt(qf[h], kf[h]) for h in heads]
        sc_b = [_dot_nt(qb[h], kb[h]) for h in heads]
        att = [(jnp.where(lower, sc_f[h], 0.0) + jnp.where(upper, sc_b[h], 0.0)).astype(BF16) for h in heads]
        o_loc = [_dot(att[h], v[h]) for h in heads]
        dec_f = [jnp.exp(x) for x in tot_f]
        dec_b = [jnp.exp(x) for x in tot_b]
        for h in heads:
            o_scr[h, sl, :] = o_loc[h]
            qf_scr[h, sl, :] = qf[h]
            qb_scr[h, sl, :] = qb[h]
        for j in range(per):
            n = m * per + j
            rows = slice(j * c, (j + 1) * c)
            for h in heads:
                uf_scr[h, n] = _dot_tn(v[h][rows], ksf[h][rows])
                ub_scr[h, n] = _dot_tn(v[h][rows], ksb[h][rows])
                df_scr[h, n] = dec_f[h][j * c:j * c + SUBLANES]
                db_scr[h, n] = dec_b[h][j * c:j * c + SUBLANES]
        return carry

    lax.fori_loop(0, n_chunks // per, block_local, 0)

    def scan_fwd(n, s):
        upd = [uf_scr[h, n] for h in heads]
        for h in heads:
            uf_scr[h, n] = s[h]
        return tuple(s[h] * df_scr[h, n][0:1, :] + upd[h] for h in heads)

    def scan_bwd(i, s):
        n = n_chunks - 1 - i
        upd = [ub_scr[h, n] for h in heads]
        for h in heads:
            ub_scr[h, n] = s[h]
        return tuple(s[h] * db_scr[h, n][0:1, :] + upd[h] for h in heads)

    s_f = lax.fori_loop(0, n_chunks, scan_fwd, tuple(s0f_ref[0, h].T for h in heads), unroll=GLA_UNROLL)
    s_b = lax.fori_loop(0, n_chunks, scan_bwd, tuple(s0b_ref[0, h].T for h in heads), unroll=GLA_UNROLL)
    for h in heads:
        sf_ref[0, h] = s_f[h].T
        sb_ref[0, h] = s_b[h].T

    def chunk_inter(n, carry):
        sl = pl.ds(pl.multiple_of(n * c, c), c)
        inter = [_dot_nt(qf_scr[h, sl, :], uf_scr[h, n].astype(BF16))
                 + _dot_nt(qb_scr[h, sl, :], ub_scr[h, n].astype(BF16)) for h in heads]
        for h in heads:
            o_scr[h, sl, :] += inter[h]
        return carry

    lax.fori_loop(0, n_chunks, chunk_inter, 0, unroll=GLA_UNROLL)

    for h in heads:
        o = _rms(o_scr[h], gg_ref[...])
        y_ref[0, :, vcols[h]] = (o * _silu(g_ref[0, :, vcols[h]].astype(F32))).astype(y_ref.dtype)


def _gla(proj, lr, wdf, bdf, wdb, bdb, g_gla, s0f, s0b, batch, seq, dk, dv, hp, cast_srcs):
    heads = GLA_HEADS
    proj3 = proj.reshape(batch, seq, proj.shape[-1])
    lr3 = lr.reshape(batch, seq, LANES)
    kdim = heads * dk
    width = heads * dv
    bk, bv = hp * dk, hp * dv
    k_blk = kdim // bk
    v_blk = 2 * kdim // bv
    g_blk = (2 * kdim + width) // bv
    n_chunks = seq // GLA_CHUNK

    def s0_map(s0):
        if s0.shape[0] == batch:
            return lambda b, h: (b, h, 0, 0)
        return lambda b, h: (0, h, 0, 0)

    n_hsteps = heads // hp
    cast_specs, cast_shapes = _cast_rider_specs(cast_srcs, batch * n_hsteps, lambda b, h: b * n_hsteps + h)
    kern = _with_cast_riders(functools.partial(_gla_kernel, seq=seq, dk=dk, dv=dv, hp=hp), 12, 3, len(cast_srcs))
    return pl.pallas_call(
        kern,
        grid=(batch, n_hsteps),
        in_specs=[pl.BlockSpec((1, seq, bk), lambda b, h: (b, 0, h)),
                  pl.BlockSpec((1, seq, bk), lambda b, h: (b, 0, k_blk + h)),
                  pl.BlockSpec((1, seq, bv), lambda b, h: (b, 0, v_blk + h)),
                  pl.BlockSpec((1, seq, bv), lambda b, h: (b, 0, g_blk + h)),
                  pl.BlockSpec((1, seq, LANES), lambda b, h: (b, 0, 0)),
                  pl.BlockSpec((hp, LANES, dk), lambda b, h: (h, 0, 0)),
                  pl.BlockSpec((1, bk), lambda b, h: (0, h)),
                  pl.BlockSpec((hp, LANES, dk), lambda b, h: (h, 0, 0)),
                  pl.BlockSpec((1, bk), lambda b, h: (0, h)),
                  pl.BlockSpec((1, dv), lambda b, h: (0, 0)),
                  pl.BlockSpec((1, hp, dk, dv), s0_map(s0f)),
                  pl.BlockSpec((1, hp, dk, dv), s0_map(s0b))] + cast_specs,
        out_specs=[pl.BlockSpec((1, seq, bv), lambda b, h: (b, 0, h)),
                   pl.BlockSpec((1, hp, dk, dv), lambda b, h: (b, h, 0, 0)),
                   pl.BlockSpec((1, hp, dk, dv), lambda b, h: (b, h, 0, 0))] + cast_specs,
        out_shape=[jax.ShapeDtypeStruct((batch, seq, width), BF16),
                   jax.ShapeDtypeStruct((batch, heads, dk, dv), F32),
                   jax.ShapeDtypeStruct((batch, heads, dk, dv), F32)] + cast_shapes,
        scratch_shapes=[pltpu.VMEM((hp, seq, dk), F32), pltpu.VMEM((hp, seq, dk), F32),
                        pltpu.VMEM((hp, seq, dk), BF16), pltpu.VMEM((hp, seq, dk), BF16),
                        pltpu.VMEM((hp, seq, dv), F32),
                        pltpu.VMEM((hp, n_chunks, dv, dk), F32), pltpu.VMEM((hp, n_chunks, dv, dk), F32),
                        pltpu.VMEM((hp, n_chunks, SUBLANES, dk), F32), pltpu.VMEM((hp, n_chunks, SUBLANES, dk), F32)],
        compiler_params=_cparams(("arbitrary", "arbitrary")),
        name="gla",
    )(proj3, proj3, proj3, proj3, lr3, wdf, bdf, wdb, bdb, g_gla, s0f, s0b, *cast_srcs)


def _filter_kernel(pe_ref, w1_ref, b1_ref, w2_ref, b2_ref, fr_ref, w3_ref, dl_ref, ad_ref, hl_ref, hm_ref, mlp_scr,
                   *, seq):
    @pl.when(pl.program_id(0) == 0)
    def _():
        fr = fr_ref[...]
        h1 = jnp.sin(fr * (_dot_hp(pe_ref[...], w1_ref[...]) + b1_ref[...]))
        mlp_scr[...] = jnp.sin(fr * (_dot_hp(h1, w2_ref[...]) + b2_ref[...]))

    h2 = mlp_scr[...]
    dec = jnp.exp(-pe_ref[:, 0:1] * dl_ref[...])
    row = lax.broadcasted_iota(jnp.int32, (seq, 1), 0)
    alt = jnp.where(row % 2 == 0, 1.0, -1.0)
    phase = row % 4
    cos_half = jnp.where(phase == 0, 1.0, jnp.where(phase == 2, -1.0, 0.0))
    sin_half = jnp.where(phase == 1, 1.0, jnp.where(phase == 3, -1.0, 0.0))
    for o in range(HY_ORDER):
        ff = _dot_hp(h2, w3_ref[o]) * dec
        fb = _dot_hp(h2, w3_ref[HY_ORDER + o]) * dec
        nrm = jnp.sum(jnp.abs(ff), axis=0, keepdims=True) + jnp.sum(jnp.abs(fb), axis=0, keepdims=True)
        ff = ff / nrm
        fb = jnp.where(row == 0, 0.0, fb / nrm)
        a = ff + fb
        d = fb - ff
        ad_ref[o, 0] = a.astype(ad_ref.dtype)
        ad_ref[o, 1] = d.astype(ad_ref.dtype)
        ad_ref[o, 2] = (alt * a).astype(ad_ref.dtype)
        ad_ref[o, 3] = (-alt * d).astype(ad_ref.dtype)
        hl_ref[o] = jnp.sum(alt * a, axis=0, keepdims=True)
        hm_ref[o] = jnp.concatenate([jnp.sum(cos_half * a, axis=0, keepdims=True),
                                     jnp.sum(sin_half * d, axis=0, keepdims=True)], axis=0) * (1.0 / seq)


def _hy_filter(pe, w1p, b1, w2, b2, freq, w3r, deltas, seq, width, tc):
    hid = w2.shape[0]
    kern = functools.partial(_filter_kernel, seq=seq)
    return pl.pallas_call(
        kern,
        grid=(width // tc,),
        in_specs=[pl.BlockSpec((seq, LANES), lambda c: (0, 0)),
                  pl.BlockSpec((LANES, hid), lambda c: (0, 0)),
                  pl.BlockSpec((1, hid), lambda c: (0, 0)),
                  pl.BlockSpec((hid, hid), lambda c: (0, 0)),
                  pl.BlockSpec((1, hid), lambda c: (0, 0)),
                  pl.BlockSpec((1, hid), lambda c: (0, 0)),
                  pl.BlockSpec((2 * HY_ORDER, hid, tc), lambda c: (0, 0, c)),
                  pl.BlockSpec((1, tc), lambda c: (0, c))],
        out_specs=[pl.BlockSpec((HY_ORDER, 4, seq, tc), lambda c: (0, 0, 0, c)),
                   pl.BlockSpec((HY_ORDER, 1, tc), lambda c: (0, 0, c)),
                   pl.BlockSpec((HY_ORDER, 2, tc), lambda c: (0, 0, c))],
        out_shape=[jax.ShapeDtypeStruct((HY_ORDER, 4, seq, width), BF16),
                   jax.ShapeDtypeStruct((HY_ORDER, 1, width), F32),
                   jax.ShapeDtypeStruct((HY_ORDER, 2, width), F32)],
        scratch_shapes=[pltpu.VMEM((seq, hid), F32)],
        compiler_params=_cparams(("arbitrary",)),
        name="hy_filter",
    )(pe, w1p, b1, w2, b2, freq, w3r, deltas)


def _spectrum_kernel(cs_ref, ad_ref, h_ref, *, seq, per_part, tr):
    r = pl.program_id(1)
    k = (r % per_part) * tr + lax.broadcasted_iota(jnp.int32, (tr, 1), 0)
    wgt = jnp.where(k == 0, 1.0, 2.0) * (0.5 / seq)
    h_ref[0] = wgt * _dot(cs_ref[0], ad_ref[0, 0])


def _hy_spectrum(cs3, ad, seq, part_rows, n_parts, width, tr, tc):
    per_part = part_rows // tr
    kern = functools.partial(_spectrum_kernel, seq=seq, per_part=per_part, tr=tr)
    return pl.pallas_call(
        kern,
        grid=(HY_ORDER, n_parts * per_part, width // tc),
        in_specs=[pl.BlockSpec((1, tr, seq), lambda o, r, c: ((r // per_part) % 2, r % per_part, 0)),
                  pl.BlockSpec((1, 1, seq, tc), lambda o, r, c: (o, r // per_part, 0, c))],
        out_specs=pl.BlockSpec((1, tr, tc), lambda o, r, c: (o, r, c)),
        out_shape=jax.ShapeDtypeStruct((HY_ORDER, n_parts * part_rows, width), F32),
        compiler_params=_cparams(("arbitrary", "arbitrary", "arbitrary")),
        name="hy_spectrum",
    )(cs3, ad)


def _short_conv_fn(seq, seg):
    row = lax.broadcasted_iota(jnp.int32, (seq, 1), 0)
    pos = row % seg

    def short_conv(u_ref, cw_ref, cb_ref, b=0):
        u = u_ref[b].astype(F32)
        prev = jnp.where(pos == 0, 0.0, pltpu.roll(u, 1, 0))
        nxt = jnp.where(pos == seg - 1, 0.0, pltpu.roll(u, seq - 1, 0))
        return prev * cw_ref[0:1, :] + u * cw_ref[1:2, :] + nxt * cw_ref[2:3, :] + cb_ref[...]

    return short_conv


def _hyconv_kernel(u0_ref, u1_ref, u2_ref, cw0_ref, cw1_ref, cw2_ref, cb0_ref, cb1_ref, cb2_ref,
                   cs_ref, h_ref, hl_ref, bias_ref, z_ref, y_scr, *, seq, seg):
    short_conv = _short_conv_fn(seq, seg)
    row = lax.broadcasted_iota(jnp.int32, (seq, 1), 0)
    alt = jnp.where(row % 2 == 0, 1.0, -1.0)
    gates = ((u1_ref, cw1_ref, cb1_ref), (u2_ref, cw2_ref, cb2_ref))
    items = range(z_ref.shape[0])
    z = [short_conv(u0_ref, cw0_ref, cb0_ref, b) for b in items]
    kb = min(seq, FREQ_BLOCK)
    for n in range(HY_ORDER):
        zb = [z[b].astype(BF16) for b in items]
        for r in range(0, seq, kb):
            xc = [_dot(cs_ref[r:r + kb, :], zb[b]) for b in items]
            xs = [_dot(cs_ref[seq + r:seq + r + kb, :], zb[b]) for b in items]
            hre = h_ref[n, r:r + kb, :]
            him = h_ref[n, seq + r:seq + r + kb, :]
            for b in items:
                y_scr[b, r:r + kb, :] = (xc[b] * hre + xs[b] * him).astype(BF16)
                y_scr[b, seq + r:seq + r + kb, :] = (xs[b] * hre - xc[b] * him).astype(BF16)
        nyq = [jnp.sum(alt * z[b], axis=0, keepdims=True) * (hl_ref[n] * (0.5 / seq)) for b in items]
        conv = [_dot(cs_ref[:seq, :], y_scr[b, :seq, :]) + _dot(cs_ref[seq:, :], y_scr[b, seq:, :]) + alt * nyq[b]
                for b in items]
        gate = [short_conv(*gates[n], b) for b in items]
        z = [gate[b] * (conv[b] + bias_ref[n] * z[b]) for b in items]
    for b in items:
        z_ref[b] = z[b].astype(z_ref.dtype)


def _hyconv_split_kernel(u0_ref, u1_ref, u2_ref, cw0_ref, cw1_ref, cw2_ref, cb0_ref, cb1_ref, cb2_ref,
                         cs_ref, h_ref, hm_ref, tw_ref, bias_ref, z_ref, g_scr, z_scr, c_scr, *, seq, seg):
    m = seq // 2
    tc = z_ref.shape[2]
    n_lane_blocks = tc // LANES
    short_conv = _short_conv_fn(seq, seg)
    row = lax.broadcasted_iota(jnp.int32, (m, 1), 0)
    alt = jnp.where(row % 2 == 0, 1.0, -1.0)
    gates = ((u1_ref, cw1_ref, cb1_ref), (u2_ref, cw2_ref, cb2_ref))
    z = short_conv(u0_ref, cw0_ref, cb0_ref)
    kb = min(m, FREQ_BLOCK)
    for n in range(HY_ORDER):
        for j in range(n_lane_blocks):
            z_scr[j] = z[:, j * LANES:(j + 1) * LANES]
        ze = jnp.concatenate([z_scr[j, pl.ds(0, m, stride=2), :] for j in range(n_lane_blocks)], axis=1)
        zo = jnp.concatenate([z_scr[j, pl.ds(1, m, stride=2), :] for j in range(n_lane_blocks)], axis=1)
        e_mid = jnp.sum(alt * ze, axis=0, keepdims=True)
        o_mid = jnp.sum(alt * zo, axis=0, keepdims=True)
        zeb = ze.astype(BF16)
        zob = zo.astype(BF16)
        for r in range(0, m, kb):
            rows = slice(r, r + kb)
            srows = slice(m + r, m + r + kb)
            ec = _dot(cs_ref[rows, :], zeb)
            es = _dot(cs_ref[srows, :], zeb)
            oc = _dot(cs_ref[rows, :], zob)
            os_ = _dot(cs_ref[srows, :], zob)
            c = tw_ref[0, rows, :]
            s = tw_ref[1, rows, :]
            pc = c * oc - s * os_
            ps = c * os_ + s * oc
            xca, xsa = ec + pc, es + ps
            xcb, xsb = ec - pc, ps - es
            har = h_ref[n, rows, :]
            hai = h_ref[n, srows, :]
            hbr = h_ref[n, 2 * m + r:2 * m + r + kb, :]
            hbi = h_ref[n, 3 * m + r:3 * m + r + kb, :]
            yar = xca * har + xsa * hai
            yai = xca * hai - xsa * har
            ybr = xcb * hbr + xsb * hbi
            ybi = xcb * hbi - xsb * hbr
            dr = yar - ybr
            di = yai + ybi
            g_scr[0, rows, :] = (yar + ybr).astype(BF16)
            g_scr[0, srows, :] = (ybi - yai).astype(BF16)
            g_scr[1, rows, :] = (c * dr - s * di).astype(BF16)
            g_scr[1, srows, :] = (-(s * dr + c * di)).astype(BF16)
        hr = hm_ref[n, 0:1, :]
        hi = hm_ref[n, 1:2, :]
        ymr = e_mid * hr + o_mid * hi
        ymi = e_mid * hi - o_mid * hr
        y_even = _dot(cs_ref[:m, :], g_scr[0, :m, :]) + _dot(cs_ref[m:, :], g_scr[0, m:, :]) + alt * ymr
        y_odd = _dot(cs_ref[:m, :], g_scr[1, :m, :]) + _dot(cs_ref[m:, :], g_scr[1, m:, :]) - alt * ymi
        for j in range(n_lane_blocks):
            c_scr[j, pl.ds(0, m, stride=2), :] = y_even[:, j * LANES:(j + 1) * LANES]
            c_scr[j, pl.ds(1, m, stride=2), :] = y_odd[:, j * LANES:(j + 1) * LANES]
        conv = jnp.concatenate([c_scr[j] for j in range(n_lane_blocks)], axis=1)
        z = short_conv(*gates[n]) * (conv + bias_ref[n] * z)
    z_ref[0] = z.astype(z_ref.dtype)


def _hy_conv(proj, conv_w, conv_b, cs, hspec, hl, hm, tw, bias, batch, seq, seg, col0, width, tc, split, bp,
             cast_srcs):
    proj3 = proj.reshape(batch, seq, proj.shape[-1])
    blk0 = col0 // tc
    per = width // tc
    n_b = batch // bp
    assert bp == 1 or not split
    once = dict(pipeline_mode=pl.Buffered(1))

    def u_spec(p):
        return pl.BlockSpec((bp, seq, tc), lambda c, b: (b, 0, blk0 + p * per + c))

    def w_spec(p, rows):
        return pl.BlockSpec((rows, tc), lambda c, b: (0, p * per + c))

    in_specs = [u_spec(0), u_spec(1), u_spec(2),
                w_spec(0, HY_SHORT), w_spec(1, HY_SHORT), w_spec(2, HY_SHORT),
                w_spec(0, 1), w_spec(1, 1), w_spec(2, 1),
                pl.BlockSpec(cs.shape, lambda c, b: (0, 0), **once),
                pl.BlockSpec((HY_ORDER, 2 * seq, tc), lambda c, b: (0, 0, c), **once)]
    bias_spec = pl.BlockSpec((HY_ORDER, 1, tc), lambda c, b: (0, 0, c))
    operands = [proj3, proj3, proj3, conv_w, conv_w, conv_w, conv_b, conv_b, conv_b, cs, hspec]
    if split:
        kern = functools.partial(_hyconv_split_kernel, seq=seq, seg=seg)
        in_specs += [pl.BlockSpec((HY_ORDER, 2, tc), lambda c, b: (0, 0, c)),
                     pl.BlockSpec((2, seq // 2, tc), lambda c, b: (0, 0, 0)), bias_spec]
        operands += [hm, tw, bias]
        scratch = [pltpu.VMEM((2, seq, tc), BF16), pltpu.VMEM((tc // LANES, seq, LANES), F32),
                   pltpu.VMEM((tc // LANES, seq, LANES), F32)]
    else:
        kern = functools.partial(_hyconv_kernel, seq=seq, seg=seg)
        in_specs += [pl.BlockSpec((HY_ORDER, 1, tc), lambda c, b: (0, 0, c)), bias_spec]
        operands += [hl, bias]
        scratch = [pltpu.VMEM((bp, 2 * seq, tc), BF16)]
    cast_specs, cast_shapes = _cast_rider_specs(cast_srcs, per * n_b, lambda c, b: c * n_b + b)
    kern = _with_cast_riders(kern, len(operands), 1, len(cast_srcs))
    return pl.pallas_call(
        kern,
        grid=(per, n_b),
        in_specs=in_specs + cast_specs,
        out_specs=[pl.BlockSpec((bp, seq, tc), lambda c, b: (b, 0, c))] + cast_specs,
        out_shape=[jax.ShapeDtypeStruct((batch, seq, width), BF16)] + cast_shapes,
        scratch_shapes=scratch,
        compiler_params=_cparams(("arbitrary", "arbitrary")),
        name="hy_conv",
    )(*operands, *cast_srcs)


def _mixout_kernel(*refs, gla_width, n_own, aliased):
    (yg_ref, zh_ref, x_ref, mod_ref, ghy_ref, gpost_ref, gpre_ref, wo_ref, wrh_ref, wrl_ref, br_ref,
     cnt_in_ref) = refs[:12]
    x1_ref, hg_ref, route_ref, cnt_ref, cnt_scr = refs[12 + int(aliased):]
    d = x_ref.shape[1]
    tm = x_ref.shape[0] // MIX_SUBTILES
    i = pl.program_id(0)

    @pl.when(i == 0)
    def _():
        cnt_scr[...] = cnt_in_ref[...]

    @pl.when(i < n_own)
    def _():
        for s in range(MIX_SUBTILES):
            _mixout_rows(slice(s * tm, (s + 1) * tm), tm, d, gla_width, yg_ref, zh_ref, x_ref, mod_ref, ghy_ref,
                         gpost_ref, gpre_ref, wo_ref, wrh_ref, wrl_ref, br_ref, x1_ref, hg_ref, route_ref, cnt_scr)

    @pl.when(i >= n_own)
    def _():
        hg_ref[...] = jnp.zeros_like(hg_ref)

    cnt_ref[...] = cnt_scr[...]


def _mixout_rows(rows, tm, d, gla_width, yg_ref, zh_ref, x_ref, mod_ref, ghy_ref, gpost_ref, gpre_ref, wo_ref,
                 wrh_ref, wrl_ref, br_ref, x1_ref, hg_ref, route_ref, cnt_scr):
    yh = _rms(zh_ref[rows, :].astype(F32), ghy_ref[...]).astype(BF16)
    y = _dot(yg_ref[rows, :], wo_ref[:gla_width, :]) + _dot(yh, wo_ref[gla_width:, :])
    x1 = x_ref[rows, :] + mod_ref[0, 2:3, :] * _rms(y, gpost_ref[...])
    x1_ref[rows, :] = x1
    h2 = _rms(x1, gpre_ref[...]) * (1.0 + mod_ref[0, 4:5, :]) + mod_ref[0, 3:4, :]
    hh, hl = _split2(h2)
    hg_ref[rows, :d] = h2
    logits = _dot(hh, wrh_ref[...]) + (_dot(hh, wrl_ref[...]) + _dot(hl, wrh_ref[...])) + br_ref[...]

    lane = lax.broadcasted_iota(jnp.int32, logits.shape, 1).astype(F32)
    neg = -jnp.inf
    is_grp = (lane >= N_EXPERTS) & (lane < N_EXPERTS + N_GROUPS)
    m = jnp.max(jnp.where(is_grp, logits, neg), axis=1, keepdims=True)
    p_grp = 1.0 / jnp.sum(jnp.where(is_grp, jnp.exp(logits - m), 0.0), axis=1, keepdims=True)
    grp = jnp.min(jnp.where(is_grp & (logits == m), lane - N_EXPERTS, 1e9), axis=1, keepdims=True)
    sel = (lane >= grp * EXP_PER_GROUP) & (lane < (grp + 1.0) * EXP_PER_GROUP)
    me = jnp.max(jnp.where(sel, logits, neg), axis=1, keepdims=True)
    pe = jnp.where(sel, jnp.exp(logits - me), -1.0)
    v1 = jnp.max(pe, axis=1, keepdims=True)
    i1 = jnp.min(jnp.where(pe == v1, lane, 1e9), axis=1, keepdims=True)
    pe2 = jnp.where(lane == i1, -1.0, pe)
    v2 = jnp.max(pe2, axis=1, keepdims=True)
    i2 = jnp.min(jnp.where(pe2 == v2, lane, 1e9), axis=1, keepdims=True)
    den = v1 + v2
    gates = jnp.where(lane == i1, v1 / den, jnp.where(lane == i2, v2 / den, 0.0)) * p_grp

    lo = jnp.minimum(i1, i2) - grp * EXP_PER_GROUP
    hi = jnp.maximum(i1, i2) - grp * EXP_PER_GROUP
    bucket = grp * PAIRS_PER_GROUP + lo * (2 * EXP_PER_GROUP - 1 - lo) * 0.5 + (hi - lo - 1.0)
    onehot = lane == bucket
    r_i = lax.broadcasted_iota(jnp.int32, (tm, tm), 0)
    c_i = lax.broadcasted_iota(jnp.int32, (tm, tm), 1)
    earlier = _dot((r_i > c_i).astype(BF16), onehot.astype(BF16)) + cnt_scr[...]
    rank = jnp.sum(jnp.where(onehot, earlier, 0.0), axis=1, keepdims=True)
    cnt_scr[...] += jnp.sum(onehot.astype(F32), axis=0, keepdims=True)
    route = jnp.where(lane == ROUTE_BUCKET_LANE, bucket, jnp.where(lane == ROUTE_RANK_LANE, rank, gates))
    hg_ref[rows, d:] = route
    route_ref[:, rows] = route.T[ROUTE_BUCKET_LANE:ROUTE_BUCKET_LANE + SUBLANES, :]


def _mix_out(yg, zh, x, mod, mod_row, g_hy, g_post, g_pre, w_out, wr_hi, wr_lo, b_r, cnt_in, hg_all, row0, n_total, tm):
    n, d = x.shape
    gw = yg.shape[1]
    hw = zh.shape[1]
    aliased = hg_all is not None
    n_own = n // tm
    n_steps = n_own if aliased else n_total // tm
    assert row0 % tm == 0 and n_total % tm == 0 and (row0 + n == n_total if aliased else row0 == 0)
    kern = functools.partial(_mixout_kernel, gla_width=gw, n_own=n_own, aliased=aliased)
    row = lambda i: (jnp.minimum(i, n_own - 1), 0)
    fixed = lambda i: (0, 0)
    in_specs = [pl.BlockSpec((tm, gw), row),
                pl.BlockSpec((tm, hw), row),
                pl.BlockSpec((tm, d), row),
                pl.BlockSpec((1, 6, d), lambda i: (mod_row(jnp.minimum(i, n_own - 1) * tm), 0, 0)),
                pl.BlockSpec((1, hw), fixed),
                pl.BlockSpec((1, d), fixed),
                pl.BlockSpec((1, d), fixed),
                pl.BlockSpec((gw + hw, d), fixed),
                pl.BlockSpec((d, LANES), fixed),
                pl.BlockSpec((d, LANES), fixed),
                pl.BlockSpec((1, LANES), fixed),
                pl.BlockSpec((1, LANES), fixed)]
    operands = [yg, zh, x, mod, g_hy, g_post, g_pre, w_out, wr_hi, wr_lo, b_r, cnt_in]
    if aliased:
        in_specs.append(pl.BlockSpec(memory_space=pl.ANY))
        operands.append(hg_all)
    return pl.pallas_call(
        kern,
        grid=(n_steps,),
        in_specs=in_specs,
        out_specs=[pl.BlockSpec((tm, d), row),
                   pl.BlockSpec((tm, d + LANES), lambda i: (row0 // tm + i, 0)),
                   pl.BlockSpec((SUBLANES, tm), lambda i: (0, jnp.minimum(i, n_own - 1))),
                   pl.BlockSpec((1, LANES), fixed)],
        out_shape=[jax.ShapeDtypeStruct((n, d), F32),
                   jax.ShapeDtypeStruct((n_total, d + LANES), F32),
                   jax.ShapeDtypeStruct((SUBLANES, n), F32),
                   jax.ShapeDtypeStruct((1, LANES), F32)],
        scratch_shapes=[pltpu.VMEM((1, LANES), F32)],
        input_output_aliases={len(operands) - 1: 1} if aliased else {},
        compiler_params=_cparams(("arbitrary",)),
        name="mix_out",
    )(*operands)


def _row_gather_copy(src_hbm, row, buf, slot, r, sem):
    return pltpu.make_async_copy(src_hbm.at[pl.ds(row, 1), :], buf.at[slot, pl.ds(r, 1), :], sem.at[slot])


def _start_row_gather(idx_ref, base, src_hbm, buf, slot, sem, tm):
    def body(r, carry):
        _row_gather_copy(src_hbm, idx_ref[base + r], buf, slot, r, sem).start()
        return carry

    lax.fori_loop(0, tm, body, 0, unroll=8)


def _start_row_gather_inline(idx_ref, base, src_hbm, buf, slot, sem, tm):
    for r in range(tm):
        _row_gather_copy(src_hbm, idx_ref[base + r], buf, slot, r, sem).start()


def _wait_row_gather(src_hbm, buf, slot, sem, tm):
    pltpu.make_async_copy(src_hbm.at[pl.ds(0, tm), :], buf.at[slot], sem.at[slot]).wait()


def _moe_kernel(src_ref, ea_ref, eb_ref, valid_ref, hg_hbm,
                wga_ref, wua_ref, wda_ref, wgb_ref, wub_ref, wdb_ref, y_ref, buf, sem, *, tm, d):
    t = pl.program_id(0)
    slot = t % 2

    @pl.when(t == 0)
    def _():
        _start_row_gather(src_ref, 0, hg_hbm, buf, 0, sem, tm)

    @pl.when(valid_ref[t] == 1)
    def _():
        _wait_row_gather(hg_hbm, buf, slot, sem, tm)
        _start_row_gather_inline(src_ref, (t + 1) * tm, hg_hbm, buf, 1 - slot, sem, tm)
        h = buf[slot, :, :d].astype(BF16)
        gates = buf[slot, :, d:]
        lane = lax.broadcasted_iota(jnp.int32, gates.shape, 1)

        def expert(wg_ref, wu_ref, wd_ref, e):
            he = (_silu(_dot(h, wg_ref[0])) * _dot(h, wu_ref[0])).astype(BF16)
            gate = jnp.sum(jnp.where(lane == e, gates, 0.0), axis=1, keepdims=True)
            return gate * _dot(he, wd_ref[0])

        y_ref[...] = expert(wga_ref, wua_ref, wda_ref, ea_ref[t]) + expert(wgb_ref, wub_ref, wdb_ref, eb_ref[t])

    @pl.when(valid_ref[t] == 0)
    def _():
        @pl.when(valid_ref[jnp.maximum(t - 1, 0)] == 1)
        def _():
            _wait_row_gather(hg_hbm, buf, slot, sem, tm)

        y_ref[...] = jnp.zeros_like(y_ref)


def _moe(hg, src, ea, eb, valid, wg, wu, wd, n_tiles, tm):
    d = hg.shape[1] - LANES
    _, _, de = wg.shape
    kern = functools.partial(_moe_kernel, tm=tm, d=d)
    wa = lambda t, src, ea, eb, valid: (ea[t], 0, 0)
    wb = lambda t, src, ea, eb, valid: (eb[t], 0, 0)
    return pl.pallas_call(
        kern,
        grid_spec=pltpu.PrefetchScalarGridSpec(
            num_scalar_prefetch=4,
            grid=(n_tiles,),
            in_specs=[pl.BlockSpec(memory_space=pl.ANY),
                      pl.BlockSpec((1, d, de), wa), pl.BlockSpec((1, d, de), wa), pl.BlockSpec((1, de, d), wa),
                      pl.BlockSpec((1, d, de), wb), pl.BlockSpec((1, d, de), wb), pl.BlockSpec((1, de, d), wb)],
            out_specs=pl.BlockSpec((tm, d), lambda t, src, ea, eb, valid: (t, 0)),
            scratch_shapes=[pltpu.VMEM((2, tm, d + LANES), F32), pltpu.SemaphoreType.DMA((2,))]),
        out_shape=jax.ShapeDtypeStruct((n_tiles * tm, d), F32),
        compiler_params=_cparams(("arbitrary",)),
        name="moe",
    )(src, ea, eb, valid, hg, wg, wu, wd, wg, wu, wd)


def _ffn_out_kernel(pos_ref, y_hbm, x1_ref, mod_ref, gpost_ref, o_ref, buf, sem, *, tm):
    i = pl.program_id(0)
    n_i = pl.num_programs(0)
    slot = i % 2

    @pl.when(i == 0)
    def _():
        _start_row_gather(pos_ref, 0, y_hbm, buf, 0, sem, tm)

    _wait_row_gather(y_hbm, buf, slot, sem, tm)

    @pl.when(i + 1 < n_i)
    def _():
        _start_row_gather_inline(pos_ref, (i + 1) * tm, y_hbm, buf, 1 - slot, sem, tm)
        o_ref[...] = x1_ref[...] + mod_ref[0, 5:6, :] * _rms(buf[slot], gpost_ref[...])

    @pl.when(i + 1 == n_i)
    def _():
        o_ref[...] = x1_ref[...] + mod_ref[0, 5:6, :] * _rms(buf[slot], gpost_ref[...])


def _ffn_out(y_sorted, pos, x1, mod, mod_row, g_post, tm):
    n, d = x1.shape
    kern = functools.partial(_ffn_out_kernel, tm=tm)
    return pl.pallas_call(
        kern,
        grid_spec=pltpu.PrefetchScalarGridSpec(
            num_scalar_prefetch=1,
            grid=(n // tm,),
            in_specs=[pl.BlockSpec(memory_space=pl.ANY),
                      pl.BlockSpec((tm, d), lambda i, pos: (i, 0)),
                      pl.BlockSpec((1, 6, d), lambda i, pos: (mod_row(i * tm), 0, 0)),
                      pl.BlockSpec((1, d), lambda i, pos: (0, 0))],
            out_specs=pl.BlockSpec((tm, d), lambda i, pos: (i, 0)),
            scratch_shapes=[pltpu.VMEM((2, tm, d), F32), pltpu.SemaphoreType.DMA((2,))]),
        out_shape=jax.ShapeDtypeStruct((n, d), F32),
        compiler_params=_cparams(("arbitrary",)),
        name="ffn_out",
    )(pos, y_sorted, x1, mod, g_post)


def _route_tables(route, counts, n_tiles, tm):
    n = route.shape[1]
    n_buckets = N_GROUPS * PAIRS_PER_GROUP
    cnt = counts[0, :n_buckets].astype(jnp.int32)
    padded = (cnt + tm - 1) // tm * tm
    ends = jnp.cumsum(padded)
    starts = ends - padded
    bucket = route[0].astype(jnp.int32)
    rank = route[ROUTE_RANK_LANE - ROUTE_BUCKET_LANE].astype(jnp.int32)
    pos = starts[bucket] + rank
    src = (jnp.arange(n_tiles * tm, dtype=jnp.int32) % n).at[pos].set(jnp.arange(n, dtype=jnp.int32))
    n_valid = ends[-1] // tm
    tile = jnp.arange(n_tiles, dtype=jnp.int32)
    used = jnp.minimum(tile, n_valid - 1)
    tile_bucket = jnp.sum((ends[None, :] <= (used * tm)[:, None]).astype(jnp.int32), axis=1)
    grp = tile_bucket // PAIRS_PER_GROUP
    pair = tile_bucket % PAIRS_PER_GROUP
    pair_lo = jnp.array([a for a in range(EXP_PER_GROUP) for b in range(a + 1, EXP_PER_GROUP)], jnp.int32)
    pair_hi = jnp.array([b for a in range(EXP_PER_GROUP) for b in range(a + 1, EXP_PER_GROUP)], jnp.int32)
    ea = grp * EXP_PER_GROUP + pair_lo[pair]
    eb = grp * EXP_PER_GROUP + pair_hi[pair]
    valid = (tile < n_valid).astype(jnp.int32)
    return pos, src, ea, eb, valid


def _dft_kernel(cs_ref, *, seq, tr):
    k = pl.program_id(0) * tr + lax.broadcasted_iota(jnp.int32, (tr, LANES), 0)
    j = lax.broadcasted_iota(jnp.int32, (tr, LANES), 1)
    period = 2 * seq

    def table(step):
        ang = ((k * j * step) % period).astype(F32) * (math.pi / seq)
        return jnp.cos(ang), jnp.sin(ang)

    c0, s0 = table(1)
    c1, s1 = table(LANES)
    for t1 in range(seq // LANES):
        cols = slice(t1 * LANES, (t1 + 1) * LANES)
        ca = c1[:, t1:t1 + 1]
        sa = s1[:, t1:t1 + 1]
        cs_ref[0, :, cols] = (ca * c0 - sa * s0).astype(cs_ref.dtype)
        cs_ref[1, :, cols] = (sa * c0 + ca * s0).astype(cs_ref.dtype)


def _dft_cos_sin(seq, n_rows, tr):
    kern = functools.partial(_dft_kernel, seq=seq, tr=tr)
    return pl.pallas_call(
        kern,
        grid=(n_rows // tr,),
        out_specs=pl.BlockSpec((2, tr, seq), lambda r: (0, r, 0)),
        out_shape=jax.ShapeDtypeStruct((2, n_rows, seq), BF16),
        compiler_params=_cparams(("arbitrary",)),
        name="dft_matrix",
    )()


def _positional_features(seq):
    t = jnp.arange(seq, dtype=F32)
    t01 = t / max(seq - 1, 1)
    ang = 2.0 * math.pi * t / seq
    bands = jnp.linspace(1e-4, HY_BANDS - 1, HY_BANDS, dtype=F32)
    pe = jnp.concatenate([t01[:, None], jnp.cos(ang[:, None] * bands), -jnp.sin(ang[:, None] * bands)], axis=-1)
    return jnp.pad(pe, ((0, 0), (0, LANES - pe.shape[1])))


def _mixer_and_router(x3, mod, mod_row, s0f, s0b, n_seg, p, tiles, cnt_in, hg_all, row0, n_total, w_out, casts):
    batch, seq, d = x3.shape
    x = x3.reshape(batch * seq, d)
    dk, dv = p["dk"], p["dv"]
    hy_width = p["hy_width"]
    casted = {}

    proj, lr, *casted["premix"] = _premix_proj(x, mod, mod_row, p["g_pre_mix"], p["w_main"], p["w_lr"],
                                               tiles["tm_proj"], tiles["tn_proj"], casts["premix"])
    if w_out is None:
        w_out = casted["premix"][0]
    y_gla, s_f, s_b, *casted["gla"] = _gla(proj, lr, p["wdf"], p["bdf"], p["wdb"], p["bdb"], p["g_gla"], s0f, s0b,
                                           batch, seq, dk, dv, tiles["gla_heads_per_step"], casts["gla"])

    ad, hl, hm = _hy_filter(_positional_features(seq), p["hy_w1"], p["hy_b1"], p["hy_w2"], p["hy_b2"], p["hy_freq"],
                            p["hy_w3"], p["hy_deltas"], seq, hy_width, tiles["tc_filter"])
    tr, tc_conv = tiles["tr_spec"], tiles["tc_conv"]
    if tiles["hy_split"]:
        half = seq // 2
        hspec = _hy_spectrum(_dft_cos_sin(seq, half, tr), ad, seq, half, 4, hy_width, tr, tiles["tc_spec"])
        cs = _dft_cos_sin(half, half, tr).reshape(seq, half)
        ang = jnp.arange(half, dtype=F32) * (math.pi / seq)
        tw = jnp.broadcast_to(jnp.stack([jnp.cos(ang), jnp.sin(ang)])[:, :, None], (2, half, tc_conv))
    else:
        cs3 = _dft_cos_sin(seq, seq, tr)
        hspec = _hy_spectrum(cs3, ad, seq, seq, 2, hy_width, tr, tiles["tc_spec"])
        cs = cs3.reshape(2 * seq, seq)
        tw = None
    z_hy, *casted["hy_conv"] = _hy_conv(proj, p["hy_conv_w"], p["hy_conv_b"], cs, hspec, hl, hm, tw, p["hy_bias"],
                                        batch, seq, seq // n_seg, p["hy_col0"], hy_width, tc_conv, tiles["hy_split"],
                                        tiles["hy_batch_per_step"], casts["hy_conv"])

    x1, hg_all, route, counts = _mix_out(y_gla.reshape(batch * seq, -1), z_hy.reshape(batch * seq, -1), x, mod,
                                         mod_row, p["g_hy"], p["g_post_mix"], p["g_pre_ffn"], w_out, p["wr_hi"],
                                         p["wr_lo"], p["b_r"], cnt_in, hg_all, row0, n_total, tiles["tm_mix"])
    return x1, hg_all, route, counts, s_f, s_b, casted


def kernel(x_prompt, x_sample, c, state_gla_fwd, state_gla_bwd, c_ctx, w_ada, b_ada, g_pre_mix, g_post_mix, g_pre_ffn, g_post_ffn, w_in, w_dec_f, b_dec_f, w_dec_b, b_dec_b, g_gla, hy_conv_w, hy_conv_b, hy_w1, hy_b1, hy_w2, hy_b2, hy_w3, hy_freq, hy_bias, g_hy, w_out, w_router_grp, b_router_grp, w_router_exp, b_router_exp, w_exp_gate, w_exp_up, w_exp_down):
    depth = w_ada.shape[0]
    assert depth == 1
    l = 0
    d = x_prompt.shape[-1]
    dec_batch = x_sample.shape[0]
    heads = GLA_HEADS
    dk, dv = state_gla_fwd.shape[-2:]
    kdim = heads * dk
    gla_width = heads * dv
    hy_width = g_hy.shape[-1]
    hid = hy_w2.shape[-1]

    cond = jnp.concatenate([c_ctx[None, :], c], axis=0)
    cond = jnp.pad(cond, ((0, -cond.shape[0] % SUBLANES), (0, 0)))
    mod = _ada_mod(cond, w_ada[l], b_ada[l]).reshape(cond.shape[0], 6, d)

    n_main = 2 * kdim + 2 * gla_width
    w_main, w_lr = _repack_w_in(w_in, l, n_main, 2 * GLA_RANK, REPACK_COLS)

    def dec_weight(w, first_row):
        wh = w.reshape(GLA_RANK, heads, dk).transpose(1, 0, 2)
        return jnp.pad(wh, ((0, 0), (first_row, LANES - GLA_RANK - first_row), (0, 0)))

    deltas = jnp.abs(jnp.linspace(math.log(HY_TARGET) / HY_SLOW_PCT, math.log(HY_TARGET) / HY_FAST_PCT, hy_width,
                                  dtype=F32))
    w_r = jnp.pad(jnp.concatenate([w_router_exp[l], w_router_grp[l]], axis=1),
                  ((0, 0), (0, LANES - N_EXPERTS - N_GROUPS)))
    wr_hi, wr_lo = _split2(w_r)
    b_r = jnp.pad(jnp.concatenate([b_router_exp[l], b_router_grp[l]]), (0, LANES - N_EXPERTS - N_GROUPS))

    p = dict(
        dk=dk, dv=dv, hy_width=hy_width, hy_col0=n_main,
        g_pre_mix=g_pre_mix[l][None, :], g_post_mix=g_post_mix[l][None, :],
        g_pre_ffn=g_pre_ffn[l][None, :], g_post_ffn=g_post_ffn[l][None, :],
        w_main=w_main, w_lr=w_lr,
        wdf=dec_weight(w_dec_f[l], 0), bdf=b_dec_f[l][None, :],
        wdb=dec_weight(w_dec_b[l], GLA_RANK), bdb=b_dec_b[l][None, :],
        g_gla=g_gla[l][None, :],
        hy_conv_w=hy_conv_w[l], hy_conv_b=hy_conv_b[l][None, :],
        hy_w1=jnp.pad(hy_w1[l], ((0, LANES - hy_w1.shape[1]), (0, 0))), hy_b1=hy_b1[l][None, :],
        hy_w2=hy_w2[l], hy_b2=hy_b2[l][None, :], hy_freq=hy_freq[l][None, :],
        hy_w3=hy_w3[l].reshape(hid, 2 * HY_ORDER, hy_width).transpose(1, 0, 2), hy_deltas=deltas[None, :],
        hy_bias=hy_bias[l][:, None, :], g_hy=g_hy[l][None, :],
        wr_hi=wr_hi, wr_lo=wr_lo, b_r=b_r[None, :],
    )

    zero_state = jnp.zeros((1, heads, dk, dv), F32)
    dec_seq = x_sample.shape[1]
    tiles_p = _tile_plan(x_prompt.shape[1])
    tiles_s = _tile_plan(dec_seq)
    tm_moe = MXU_DIM
    tm_out = 512

    n_p = x_prompt.shape[0] * x_prompt.shape[1]
    n_s = dec_batch * dec_seq
    n_all = n_p + n_s
    mod_row_p = lambda r: 0
    mod_row_s = lambda r: 1 + r // dec_seq
    n_exp, _, d_exp = w_exp_gate.shape[1:]
    casts_p = dict(premix=[w_out[l]],
                   gla=[w_exp_gate[l].reshape(n_exp * d, d_exp), w_exp_down[l].reshape(n_exp * d_exp, d)],
                   hy_conv=[w_exp_up[l].reshape(n_exp * d, d_exp)])
    casts_s = dict(premix=[], gla=[], hy_conv=[])
    x1_p, hg_all, route_p, counts, s_f, s_b, casted = _mixer_and_router(
        x_prompt, mod, mod_row_p, zero_state, zero_state, 1, p, tiles_p, jnp.zeros((1, LANES), F32), None, 0, n_all,
        None, casts_p)
    (w_out_b,), (w_gate_b, w_down_b), (w_up_b,) = casted["premix"], casted["gla"], casted["hy_conv"]
    x1_s, hg_all, route_s, counts, _, _, _ = _mixer_and_router(
        x_sample, mod, mod_row_s, state_gla_fwd[:, l], state_gla_bwd[:, l], dec_seq // GRID_W, p, tiles_s,
        counts, hg_all, n_p, n_all, w_out_b, casts_s)

    n_tiles = n_all // tm_moe + N_GROUPS * PAIRS_PER_GROUP
    pos, src, ea, eb, valid = _route_tables(jnp.concatenate([route_p, route_s], axis=1), counts, n_tiles, tm_moe)
    y_sorted = _moe(hg_all, src, ea, eb, valid, w_gate_b.reshape(n_exp, d, d_exp), w_up_b.reshape(n_exp, d, d_exp),
                    w_down_b.reshape(n_exp, d_exp, d), n_tiles, tm_moe)
    y_p = _ffn_out(y_sorted, pos[:n_p], x1_p, mod, mod_row_p, p["g_post_ffn"], tm_out)
    y_s = _ffn_out(y_sorted, pos[n_p:], x1_s, mod, mod_row_s, p["g_post_ffn"], tm_out)
    return (y_p.reshape(x_prompt.shape), y_s.reshape(x_sample.shape),
            s_f[:, None].astype(x_prompt.dtype), s_b[:, None].astype(x_prompt.dtype))
```

```python
import functools
import math

import jax
import jax.numpy as jnp
from jax import lax
from jax.experimental import pallas as pl
from jax.experimental.pallas import tpu as pltpu

F32 = jnp.float32
BF16 = jnp.bfloat16

GRID_W = 64
GLA_HEADS = 4
GLA_RANK = 16
GLA_TAU = 16.0
GLA_CHUNK = 64
GLA_UNROLL = 4
MIX_SUBTILES = 2
HY_ORDER = 2
HY_SHORT = 3
HY_BANDS = 16
HY_TARGET = 1e-2
HY_FAST_PCT = 0.3
HY_SLOW_PCT = 1.5
N_GROUPS = 4
EXP_PER_GROUP = 4
N_EXPERTS = N_GROUPS * EXP_PER_GROUP
PAIRS_PER_GROUP = EXP_PER_GROUP * (EXP_PER_GROUP - 1) // 2
ROUTE_BUCKET_LANE = N_EXPERTS
ROUTE_RANK_LANE = N_EXPERTS + 1
EPS = 1e-6

LANES = 128
SUBLANES = 8
MXU_DIM = 256
VMEM_LIMIT = 56 << 20
REPACK_COLS = MXU_DIM
FREQ_BLOCK = 2 * MXU_DIM


def _tile_plan(seq):
    long_seq = seq >= 1024
    return dict(
        tm_proj=512, tn_proj=2048,
        tc_filter=256,
        tr_spec=512 if long_seq else 256,
        tc_spec=1024 if long_seq else 512,
        hy_split=long_seq,
        tc_conv=256 if long_seq else 512,
        hy_batch_per_step=1 if long_seq else 2,
        gla_heads_per_step=2 if long_seq else 4,
        tm_mix=512,
    )


def _cparams(sem):
    return pltpu.CompilerParams(dimension_semantics=sem, vmem_limit_bytes=VMEM_LIMIT)


def _dot(a, b):
    return jnp.dot(a, b, preferred_element_type=F32)


def _dot_nt(a, b):
    return lax.dot_general(a, b, (((1,), (1,)), ((), ())), preferred_element_type=F32)


def _dot_tn(a, b):
    return lax.dot_general(a, b, (((0,), (0,)), ((), ())), preferred_element_type=F32)


def _split2(x):
    hi = x.astype(BF16)
    lo = (x - hi.astype(F32)).astype(BF16)
    return hi, lo


def _dot_hp(a, b):
    ah, al = a if isinstance(a, tuple) else _split2(a)
    bh, bl = _split2(b)
    return _dot(ah, bh) + (_dot(ah, bl) + _dot(al, bh))


def _dot_exact_lhs(t, x):
    hi, lo = _split2(x)
    return _dot(t, hi) + _dot(t, lo)


def _rms(x, g):
    return x * lax.rsqrt(jnp.mean(x * x, axis=-1, keepdims=True) + EPS) * g


def _silu(x):
    return x / (1.0 + jnp.exp(-x))


def _ada_kernel(c_ref, w_ref, b_ref, o_ref):
    s = _silu(c_ref[...]).astype(BF16)
    o_ref[...] = _dot(s, w_ref[...].astype(BF16)) + b_ref[...]


def _ada_mod(cond, w_ada, b_ada):
    rows, d = cond.shape
    n = w_ada.shape[1]
    tn = 1024
    return pl.pallas_call(
        _ada_kernel,
        grid=(n // tn,),
        in_specs=[pl.BlockSpec((rows, d), lambda j: (0, 0)),
                  pl.BlockSpec((d, tn), lambda j: (0, j)),
                  pl.BlockSpec((1, tn), lambda j: (0, j))],
        out_specs=pl.BlockSpec((rows, tn), lambda j: (0, j)),
        out_shape=jax.ShapeDtypeStruct((rows, n), F32),
        compiler_params=_cparams(("arbitrary",)),
        name="ada_mod",
    )(cond, w_ada, b_ada.reshape(1, n))


def _repack_kernel(wt_hbm, main_ref, lr_ref, buf, lr_buf, sem, lr_sem, *, layer, n_main, n_lr, tb):
    i = pl.program_id(0)
    n_i = pl.num_programs(0)
    slot = i % 2
    d = main_ref.shape[0]

    def block_copy(j, s):
        first = j * tb + jnp.where(j * tb >= n_main, n_lr, 0)
        return pltpu.make_async_copy(wt_hbm.at[layer, pl.ds(first, tb), :], buf.at[s], sem.at[s])

    lr_copy = pltpu.make_async_copy(wt_hbm.at[layer, pl.ds(n_main, n_lr), :], lr_buf, lr_sem)

    @pl.when(i == 0)
    def _():
        block_copy(0, 0).start()
        lr_copy.start()

    @pl.when(i + 1 < n_i)
    def _():
        block_copy(i + 1, 1 - slot).start()

    block_copy(i, slot).wait()
    main_ref[...] = buf[slot].T.astype(BF16)

    @pl.when(i == 0)
    def _():
        lr_copy.wait()
        lr = jnp.concatenate([lr_buf[...], jnp.zeros((LANES - n_lr, d), F32)], axis=0).T
        lr_ref[...] = lr.astype(BF16)


def _repack_w_in(w_in, layer, n_main, n_lr, tb):
    w_t = jnp.swapaxes(w_in, 1, 2)
    _, n_cols, d = w_t.shape
    n_out = n_cols - n_lr
    assert n_main % tb == 0 and n_out % tb == 0
    kern = functools.partial(_repack_kernel, layer=layer, n_main=n_main, n_lr=n_lr, tb=tb)
    return pl.pallas_call(
        kern,
        grid=(n_out // tb,),
        in_specs=[pl.BlockSpec(memory_space=pl.ANY)],
        out_specs=[pl.BlockSpec((d, tb), lambda i: (0, i)),
                   pl.BlockSpec((d, LANES), lambda i: (0, 0))],
        out_shape=[jax.ShapeDtypeStruct((d, n_out), BF16),
                   jax.ShapeDtypeStruct((d, LANES), BF16)],
        scratch_shapes=[pltpu.VMEM((2, tb, d), F32), pltpu.VMEM((n_lr, d), F32),
                        pltpu.SemaphoreType.DMA((2,)), pltpu.SemaphoreType.DMA(())],
        compiler_params=_cparams(("arbitrary",)),
        name="repack_w_in",
    )(w_t)


def _with_cast_riders(kernel_fn, n_in, n_out, n_cast):
    def kernel(*refs):
        ins = refs[:n_in]
        cast_in = refs[n_in:n_in + n_cast]
        outs = refs[n_in + n_cast:n_in + n_cast + n_out]
        cast_out = refs[n_in + n_cast + n_out:n_in + 2 * n_cast + n_out]
        kernel_fn(*ins, *outs, *refs[n_in + 2 * n_cast + n_out:])
        for src, dst in zip(cast_in, cast_out):
            dst[...] = src[...].astype(dst.dtype)

    return kernel


def _cast_rider_specs(cast_srcs, n_steps, step_of):
    n_slabs = 1 << (n_steps.bit_length() - 1)
    specs = []
    for w in cast_srcs:
        rows, cols = w.shape
        assert rows % n_slabs == 0
        specs.append(pl.BlockSpec((rows // n_slabs, cols),
                                  lambda *idx: (jnp.minimum(step_of(*idx), n_slabs - 1), 0)))
    return specs, [jax.ShapeDtypeStruct(w.shape, BF16) for w in cast_srcs]


def _premix_kernel(x_ref, mod_ref, g_ref, w_ref, wlr_ref, o_ref, lr_ref, h_scr):
    @pl.when(pl.program_id(1) == 0)
    def _():
        h = _rms(x_ref[...], g_ref[...] * (1.0 + mod_ref[0, 1:2, :])) + mod_ref[0, 0:1, :]
        h_scr[...] = h.astype(BF16)
        lr_ref[...] = _dot(h_scr[...], wlr_ref[...])

    o_ref[...] = _dot(h_scr[...], w_ref[...]).astype(o_ref.dtype)


def _premix_proj(x, mod, mod_row, g, w_main, w_lr, tm, tn, cast_srcs):
    n, d = x.shape
    nc = w_main.shape[1]
    n_j = nc // tn
    cast_specs, cast_shapes = _cast_rider_specs(cast_srcs, (n // tm) * n_j, lambda i, j: i * n_j + j)
    kern = _with_cast_riders(_premix_kernel, 5, 2, len(cast_srcs))
    return pl.pallas_call(
        kern,
        grid=(n // tm, n_j),
        in_specs=[pl.BlockSpec((tm, d), lambda i, j: (i, 0)),
                  pl.BlockSpec((1, 6, d), lambda i, j: (mod_row(i * tm), 0, 0)),
                  pl.BlockSpec((1, d), lambda i, j: (0, 0)),
                  pl.BlockSpec((d, tn), lambda i, j: (0, j)),
                  pl.BlockSpec((d, LANES), lambda i, j: (0, 0))] + cast_specs,
        out_specs=[pl.BlockSpec((tm, tn), lambda i, j: (i, j)),
                   pl.BlockSpec((tm, LANES), lambda i, j: (i, 0))] + cast_specs,
        out_shape=[jax.ShapeDtypeStruct((n, nc), BF16), jax.ShapeDtypeStruct((n, LANES), F32)] + cast_shapes,
        scratch_shapes=[pltpu.VMEM((tm, d), BF16)],
        compiler_params=_cparams(("arbitrary", "arbitrary")),
        name="premix_proj",
    )(x, mod, g, w_main, w_lr, *cast_srcs)


def _log_sigmoid(x):
    return jnp.minimum(x, 0.0) - jnp.log(1.0 + jnp.exp(-jnp.abs(x)))


def _gla_kernel(q_ref, k_ref, v_ref, g_ref, lr_ref, wdf_ref, bdf_ref, wdb_ref, bdb_ref, gg_ref,
                s0f_ref, s0b_ref, y_ref, sf_ref, sb_ref,
                laf_scr, lab_scr, qf_scr, qb_scr, o_scr, uf_scr, ub_scr, df_scr, db_scr, *, seq, dk, dv, hp):
    c = GLA_CHUNK
    n_chunks = seq // c
    scale = dk ** -0.5
    heads = range(hp)
    kcols = [slice(h * dk, (h + 1) * dk) for h in heads]
    vcols = [slice(h * dv, (h + 1) * dv) for h in heads]

    lr_split = _split2(lr_ref[0])
    for h in heads:
        laf_scr[h] = _log_sigmoid(_dot_hp(lr_split, wdf_ref[h]) + bdf_ref[:, kcols[h]]) / GLA_TAU
        lab_scr[h] = _log_sigmoid(_dot_hp(lr_split, wdb_ref[h]) + bdb_ref[:, kcols[h]]) / GLA_TAU

    per = GLA_UNROLL
    blk = per * c
    row = lax.broadcasted_iota(jnp.int32, (blk, blk), 0)
    col = lax.broadcasted_iota(jnp.int32, (blk, blk), 1)
    same = (row // c) == (col // c)
    lower = same & (row >= col)
    upper = same & (col >= row)
    t_fwd = lower.astype(BF16)
    t_bwd = upper.astype(BF16)

    def chunk_rows(x, r):
        return jnp.concatenate([jnp.broadcast_to(x[j * c + r:j * c + r + 1], (c, dk)) for j in range(per)], axis=0)

    def block_local(m, carry):
        sl = pl.ds(pl.multiple_of(m * blk, blk), blk)
        bf = [_dot_exact_lhs(t_fwd, laf_scr[h, sl, :]) for h in heads]
        bb = [_dot_exact_lhs(t_bwd, lab_scr[h, sl, :]) for h in heads]
        tot_f = [chunk_rows(x, c - 1) for x in bf]
        tot_b = [chunk_rows(x, 0) for x in bb]
        q = [q_ref[0, sl, kcols[h]].astype(F32) * scale for h in heads]
        k = [k_ref[0, sl, kcols[h]].astype(F32) for h in heads]
        v = [v_ref[0, sl, vcols[h]] for h in heads]
        qf = [(q[h] * jnp.exp(bf[h])).astype(BF16) for h in heads]
        kf = [(k[h] * jnp.exp(-bf[h])).astype(BF16) for h in heads]
        qb = [(q[h] * jnp.exp(bb[h])).astype(BF16) for h in heads]
        kb = [(k[h] * jnp.exp(-bb[h])).astype(BF16) for h in heads]
        ksf = [(k[h] * jnp.exp(tot_f[h] - bf[h])).astype(BF16) for h in heads]
        ksb = [(k[h] * jnp.exp(tot_b[h] - bb[h])).astype(BF16) for h in heads]
        sc_f = [_dot_nt(qf[h], kf[h]) for h in heads]
        sc_b = [_dot_nt(qb[h], kb[h]) for h in heads]
        att = [(jnp.where(lower, sc_f[h], 0.0) + jnp.where(upper, sc_b[h], 0.0)).astype(BF16) for h in heads]
        o_loc = [_dot(att[h], v[h]) for h in heads]
        dec_f = [jnp.exp(x) for x in tot_f]
        dec_b = [jnp.exp(x) for x in tot_b]
        for h in heads:
            o_scr[h, sl, :] = o_loc[h]
            qf_scr[h, sl, :] = qf[h]
            qb_scr[h, sl, :] = qb[h]
        for j in range(per):
            n = m * per + j
            rows = slice(j * c, (j + 1) * c)
            for h in heads:
                uf_scr[h, n] = _dot_tn(v[h][rows], ksf[h][rows])
                ub_scr[h, n] = _dot_tn(v[h][rows], ksb[h][rows])
                df_scr[h, n] = dec_f[h][j * c:j * c + SUBLANES]
                db_scr[h, n] = dec_b[h][j * c:j * c + SUBLANES]
        return carry

    lax.fori_loop(0, n_chunks // per, block_local, 0)

    def scan_fwd(n, s):
        upd = [uf_scr[h, n] for h in heads]
        for h in heads:
            uf_scr[h, n] = s[h]
        return tuple(s[h] * df_scr[h, n][0:1, :] + upd[h] for h in heads)

    def scan_bwd(i, s):
        n = n_chunks - 1 - i
        upd = [ub_scr[h, n] for h in heads]
        for h in heads:
            ub_scr[h, n] = s[h]
        return tuple(s[h] * db_scr[h, n][0:1, :] + upd[h] for h in heads)

    s_f = lax.fori_loop(0, n_chunks, scan_fwd, tuple(s0f_ref[0, h].T for h in heads), unroll=GLA_UNROLL)
    s_b = lax.fori_loop(0, n_chunks, scan_bwd, tuple(s0b_ref[0, h].T for h in heads), unroll=GLA_UNROLL)
    for h in heads:
        sf_ref[0, h] = s_f[h].T
        sb_ref[0, h] = s_b[h].T

    def chunk_inter(n, carry):
        sl = pl.ds(pl.multiple_of(n * c, c), c)
        inter = [_dot_nt(qf_scr[h, sl, :], uf_scr[h, n].astype(BF16))
                 + _dot_nt(qb_scr[h, sl, :], ub_scr[h, n].astype(BF16)) for h in heads]
        for h in heads:
            o_scr[h, sl, :] += inter[h]
        return carry

    lax.fori_loop(0, n_chunks, chunk_inter, 0, unroll=GLA_UNROLL)

    for h in heads:
        o = _rms(o_scr[h], gg_ref[...])
        y_ref[0, :, vcols[h]] = (o * _silu(g_ref[0, :, vcols[h]].astype(F32))).astype(y_ref.dtype)


def _gla(proj, lr, wdf, bdf, wdb, bdb, g_gla, s0f, s0b, batch, seq, dk, dv, hp, cast_srcs):
    heads = GLA_HEADS
    proj3 = proj.reshape(batch, seq, proj.shape[-1])
    lr3 = lr.reshape(batch, seq, LANES)
    kdim = heads * dk
    width = heads * dv
    bk, bv = hp * dk, hp * dv
    k_blk = kdim // bk
    v_blk = 2 * kdim // bv
    g_blk = (2 * kdim + width) // bv
    n_chunks = seq // GLA_CHUNK

    def s0_map(s0):
        if s0.shape[0] == batch:
            return lambda b, h: (b, h, 0, 0)
        return lambda b, h: (0, h, 0, 0)

    n_hsteps = heads // hp
    cast_specs, cast_shapes = _cast_rider_specs(cast_srcs, batch * n_hsteps, lambda b, h: b * n_hsteps + h)
    kern = _with_cast_riders(functools.partial(_gla_kernel, seq=seq, dk=dk, dv=dv, hp=hp), 12, 3, len(cast_srcs))
    return pl.pallas_call(
        kern,
        grid=(batch, n_hsteps),
        in_specs=[pl.BlockSpec((1, seq, bk), lambda b, h: (b, 0, h)),
                  pl.BlockSpec((1, seq, bk), lambda b, h: (b, 0, k_blk + h)),
                  pl.BlockSpec((1, seq, bv), lambda b, h: (b, 0, v_blk + h)),
                  pl.BlockSpec((1, seq, bv), lambda b, h: (b, 0, g_blk + h)),
                  pl.BlockSpec((1, seq, LANES), lambda b, h: (b, 0, 0)),
                  pl.BlockSpec((hp, LANES, dk), lambda b, h: (h, 0, 0)),
                  pl.BlockSpec((1, bk), lambda b, h: (0, h)),
                  pl.BlockSpec((hp, LANES, dk), lambda b, h: (h, 0, 0)),
                  pl.BlockSpec((1, bk), lambda b, h: (0, h)),
                  pl.BlockSpec((1, dv), lambda b, h: (0, 0)),
                  pl.BlockSpec((1, hp, dk, dv), s0_map(s0f)),
                  pl.BlockSpec((1, hp, dk, dv), s0_map(s0b))] + cast_specs,
        out_specs=[pl.BlockSpec((1, seq, bv), lambda b, h: (b, 0, h)),
                   pl.BlockSpec((1, hp, dk, dv), lambda b, h: (b, h, 0, 0)),
                   pl.BlockSpec((1, hp, dk, dv), lambda b, h: (b, h, 0, 0))] + cast_specs,
        out_shape=[jax.ShapeDtypeStruct((batch, seq, width), BF16),
                   jax.ShapeDtypeStruct((batch, heads, dk, dv), F32),
                   jax.ShapeDtypeStruct((batch, heads, dk, dv), F32)] + cast_shapes,
        scratch_shapes=[pltpu.VMEM((hp, seq, dk), F32), pltpu.VMEM((hp, seq, dk), F32),
                        pltpu.VMEM((hp, seq, dk), BF16), pltpu.VMEM((hp, seq, dk), BF16),
                        pltpu.VMEM((hp, seq, dv), F32),
                        pltpu.VMEM((hp, n_chunks, dv, dk), F32), pltpu.VMEM((hp, n_chunks, dv, dk), F32),
                        pltpu.VMEM((hp, n_chunks, SUBLANES, dk), F32), pltpu.VMEM((hp, n_chunks, SUBLANES, dk), F32)],
        compiler_params=_cparams(("arbitrary", "arbitrary")),
        name="gla",
    )(proj3, proj3, proj3, proj3, lr3, wdf, bdf, wdb, bdb, g_gla, s0f, s0b, *cast_srcs)


def _filter_kernel(pe_ref, w1_ref, b1_ref, w2_ref, b2_ref, fr_ref, w3_ref, dl_ref, ad_ref, hl_ref, hm_ref, mlp_scr,
                   *, seq):
    @pl.when(pl.program_id(0) == 0)
    def _():
        fr = fr_ref[...]
        h1 = jnp.sin(fr * (_dot_hp(pe_ref[...], w1_ref[...]) + b1_ref[...]))
        mlp_scr[...] = jnp.sin(fr * (_dot_hp(h1, w2_ref[...]) + b2_ref[...]))

    h2 = _split2(mlp_scr[...])
    dec = jnp.exp(-pe_ref[:, 0:1] * dl_ref[...])
    row = lax.broadcasted_iota(jnp.int32, (seq, 1), 0)
    alt = jnp.where(row % 2 == 0, 1.0, -1.0)
    phase = row % 4
    cos_half = jnp.where(phase == 0, 1.0, jnp.where(phase == 2, -1.0, 0.0))
    sin_half = jnp.where(phase == 1, 1.0, jnp.where(phase == 3, -1.0, 0.0))
    for o in range(HY_ORDER):
        ff = _dot_hp(h2, w3_ref[o]) * dec
        fb = _dot_hp(h2, w3_ref[HY_ORDER + o]) * dec
        nrm = jnp.sum(jnp.abs(ff), axis=0, keepdims=True) + jnp.sum(jnp.abs(fb), axis=0, keepdims=True)
        ff = ff / nrm
        fb = jnp.where(row == 0, 0.0, fb / nrm)
        a = ff + fb
        d = fb - ff
        ad_ref[o, 0] = a.astype(ad_ref.dtype)
        ad_ref[o, 1] = d.astype(ad_ref.dtype)
        ad_ref[o, 2] = (alt * a).astype(ad_ref.dtype)
        ad_ref[o, 3] = (-alt * d).astype(ad_ref.dtype)
        hl_ref[o] = jnp.sum(alt * a, axis=0, keepdims=True)
        hm_ref[o] = jnp.concatenate([jnp.sum(cos_half * a, axis=0, keepdims=True),
                                     jnp.sum(sin_half * d, axis=0, keepdims=True)], axis=0) * (1.0 / seq)


def _hy_filter(pe, w1p, b1, w2, b2, freq, w3r, deltas, seq, width, tc):
    hid = w2.shape[0]
    kern = functools.partial(_filter_kernel, seq=seq)
    return pl.pallas_call(
        kern,
        grid=(width // tc,),
        in_specs=[pl.BlockSpec((seq, LANES), lambda c: (0, 0)),
                  pl.BlockSpec((LANES, hid), lambda c: (0, 0)),
                  pl.BlockSpec((1, hid), lambda c: (0, 0)),
                  pl.BlockSpec((hid, hid), lambda c: (0, 0)),
                  pl.BlockSpec((1, hid), lambda c: (0, 0)),
                  pl.BlockSpec((1, hid), lambda c: (0, 0)),
                  pl.BlockSpec((2 * HY_ORDER, hid, tc), lambda c: (0, 0, c)),
                  pl.BlockSpec((1, tc), lambda c: (0, c))],
        out_specs=[pl.BlockSpec((HY_ORDER, 4, seq, tc), lambda c: (0, 0, 0, c)),
                   pl.BlockSpec((HY_ORDER, 1, tc), lambda c: (0, 0, c)),
                   pl.BlockSpec((HY_ORDER, 2, tc), lambda c: (0, 0, c))],
        out_shape=[jax.ShapeDtypeStruct((HY_ORDER, 4, seq, width), BF16),
                   jax.ShapeDtypeStruct((HY_ORDER, 1, width), F32),
                   jax.ShapeDtypeStruct((HY_ORDER, 2, width), F32)],
        scratch_shapes=[pltpu.VMEM((seq, hid), F32)],
        compiler_params=_cparams(("arbitrary",)),
        name="hy_filter",
    )(pe, w1p, b1, w2, b2, freq, w3r, deltas)


def _spectrum_kernel(cs_ref, ad_ref, h_ref, *, seq, per_part, tr):
    r = pl.program_id(1)
    k = (r % per_part) * tr + lax.broadcasted_iota(jnp.int32, (tr, 1), 0)
    wgt = jnp.where(k == 0, 1.0, 2.0) * (0.5 / seq)
    h_ref[0] = wgt * _dot(cs_ref[0], ad_ref[0, 0])


def _hy_spectrum(cs3, ad, seq, part_rows, n_parts, width, tr, tc):
    per_part = part_rows // tr
    kern = functools.partial(_spectrum_kernel, seq=seq, per_part=per_part, tr=tr)
    return pl.pallas_call(
        kern,
        grid=(HY_ORDER, n_parts * per_part, width // tc),
        in_specs=[pl.BlockSpec((1, tr, seq), lambda o, r, c: ((r // per_part) % 2, r % per_part, 0)),
                  pl.BlockSpec((1, 1, seq, tc), lambda o, r, c: (o, r // per_part, 0, c))],
        out_specs=pl.BlockSpec((1, tr, tc), lambda o, r, c: (o, r, c)),
        out_shape=jax.ShapeDtypeStruct((HY_ORDER, n_parts * part_rows, width), F32),
        compiler_params=_cparams(("arbitrary", "arbitrary", "arbitrary")),
        name="hy_spectrum",
    )(cs3, ad)


def _short_conv_fn(seq, seg):
    row = lax.broadcasted_iota(jnp.int32, (seq, 1), 0)
    pos = row % seg

    def short_conv(u_ref, cw_ref, cb_ref, b=0):
        u = u_ref[b].astype(F32)
        prev = jnp.where(pos == 0, 0.0, pltpu.roll(u, 1, 0))
        nxt = jnp.where(pos == seg - 1, 0.0, pltpu.roll(u, seq - 1, 0))
        return prev * cw_ref[0:1, :] + u * cw_ref[1:2, :] + nxt * cw_ref[2:3, :] + cb_ref[...]

    return short_conv


def _hyconv_kernel(u0_ref, u1_ref, u2_ref, cw0_ref, cw1_ref, cw2_ref, cb0_ref, cb1_ref, cb2_ref,
                   cs_ref, h_ref, hl_ref, bias_ref, z_ref, y_scr, *, seq, seg):
    short_conv = _short_conv_fn(seq, seg)
    row = lax.broadcasted_iota(jnp.int32, (seq, 1), 0)
    alt = jnp.where(row % 2 == 0, 1.0, -1.0)
    gates = ((u1_ref, cw1_ref, cb1_ref), (u2_ref, cw2_ref, cb2_ref))
    items = range(z_ref.shape[0])
    z = [short_conv(u0_ref, cw0_ref, cb0_ref, b) for b in items]
    kb = min(seq, FREQ_BLOCK)
    for n in range(HY_ORDER):
        zb = [z[b].astype(BF16) for b in items]
        for r in range(0, seq, kb):
            xc = [_dot(cs_ref[r:r + kb, :], zb[b]) for b in items]
            xs = [_dot(cs_ref[seq + r:seq + r + kb, :], zb[b]) for b in items]
            hre = h_ref[n, r:r + kb, :]
            him = h_ref[n, seq + r:seq + r + kb, :]
            for b in items:
                y_scr[b, r:r + kb, :] = (xc[b] * hre + xs[b] * him).astype(BF16)
                y_scr[b, seq + r:seq + r + kb, :] = (xs[b] * hre - xc[b] * him).astype(BF16)
        nyq = [jnp.sum(alt * z[b], axis=0, keepdims=True) * (hl_ref[n] * (0.5 / seq)) for b in items]
        conv = [_dot(cs_ref[:seq, :], y_scr[b, :seq, :]) + _dot(cs_ref[seq:, :], y_scr[b, seq:, :]) + alt * nyq[b]
                for b in items]
        gate = [short_conv(*gates[n], b) for b in items]
        z = [gate[b] * (conv[b] + bias_ref[n] * z[b]) for b in items]
    for b in items:
        z_ref[b] = z[b].astype(z_ref.dtype)


def _hyconv_split_kernel(u0_ref, u1_ref, u2_ref, cw0_ref, cw1_ref, cw2_ref, cb0_ref, cb1_ref, cb2_ref,
                         cs_ref, h_ref, hm_ref, tw_ref, bias_ref, z_ref, g_scr, z_scr, c_scr, *, seq, seg):
    m = seq // 2
    tc = z_ref.shape[2]
    n_lane_blocks = tc // LANES
    short_conv = _short_conv_fn(seq, seg)
    row = lax.broadcasted_iota(jnp.int32, (m, 1), 0)
    alt = jnp.where(row % 2 == 0, 1.0, -1.0)
    gates = ((u1_ref, cw1_ref, cb1_ref), (u2_ref, cw2_ref, cb2_ref))
    z = short_conv(u0_ref, cw0_ref, cb0_ref)
    kb = min(m, FREQ_BLOCK)
    for n in range(HY_ORDER):
        for j in range(n_lane_blocks):
            z_scr[j] = z[:, j * LANES:(j + 1) * LANES]
        ze = jnp.concatenate([z_scr[j, pl.ds(0, m, stride=2), :] for j in range(n_lane_blocks)], axis=1)
        zo = jnp.concatenate([z_scr[j, pl.ds(1, m, stride=2), :] for j in range(n_lane_blocks)], axis=1)
        e_mid = jnp.sum(alt * ze, axis=0, keepdims=True)
        o_mid = jnp.sum(alt * zo, axis=0, keepdims=True)
        zeb = ze.astype(BF16)
        zob = zo.astype(BF16)
        for r in range(0, m, kb):
            rows = slice(r, r + kb)
            srows = slice(m + r, m + r + kb)
            ec = _dot(cs_ref[rows, :], zeb)
            es = _dot(cs_ref[srows, :], zeb)
            oc = _dot(cs_ref[rows, :], zob)
            os_ = _dot(cs_ref[srows, :], zob)
            c = tw_ref[0, rows, :]
            s = tw_ref[1, rows, :]
            pc = c * oc - s * os_
            ps = c * os_ + s * oc
            xca, xsa = ec + pc, es + ps
            xcb, xsb = ec - pc, ps - es
            har = h_ref[n, rows, :]
            hai = h_ref[n, srows, :]
            hbr = h_ref[n, 2 * m + r:2 * m + r + kb, :]
            hbi = h_ref[n, 3 * m + r:3 * m + r + kb, :]
            yar = xca * har + xsa * hai
            yai = xca * hai - xsa * har
            ybr = xcb * hbr + xsb * hbi
            ybi = xcb * hbi - xsb * hbr
            dr = yar - ybr
            di = yai + ybi
            g_scr[0, rows, :] = (yar + ybr).astype(BF16)
            g_scr[0, srows, :] = (ybi - yai).astype(BF16)
            g_scr[1, rows, :] = (c * dr - s * di).astype(BF16)
            g_scr[1, srows, :] = (-(s * dr + c * di)).astype(BF16)
        hr = hm_ref[n, 0:1, :]
        hi = hm_ref[n, 1:2, :]
        ymr = e_mid * hr + o_mid * hi
        ymi = e_mid * hi - o_mid * hr
        y_even = _dot(cs_ref[:m, :], g_scr[0, :m, :]) + _dot(cs_ref[m:, :], g_scr[0, m:, :]) + alt * ymr
        y_odd = _dot(cs_ref[:m, :], g_scr[1, :m, :]) + _dot(cs_ref[m:, :], g_scr[1, m:, :]) - alt * ymi
        for j in range(n_lane_blocks):
            c_scr[j, pl.ds(0, m, stride=2), :] = y_even[:, j * LANES:(j + 1) * LANES]
            c_scr[j, pl.ds(1, m, stride=2), :] = y_odd[:, j * LANES:(j + 1) * LANES]
        conv = jnp.concatenate([c_scr[j] for j in range(n_lane_blocks)], axis=1)
        z = short_conv(*gates[n]) * (conv + bias_ref[n] * z)
    z_ref[0] = z.astype(z_ref.dtype)


def _hy_conv(proj, conv_w, conv_b, cs, hspec, hl, hm, tw, bias, batch, seq, seg, col0, width, tc, split, bp,
             cast_srcs):
    proj3 = proj.reshape(batch, seq, proj.shape[-1])
    blk0 = col0 // tc
    per = width // tc
    n_b = batch // bp
    assert bp == 1 or not split
    once = dict(pipeline_mode=pl.Buffered(1))

    def u_spec(p):
        return pl.BlockSpec((bp, seq, tc), lambda c, b: (b, 0, blk0 + p * per + c))

    def w_spec(p, rows):
        return pl.BlockSpec((rows, tc), lambda c, b: (0, p * per + c))

    in_specs = [u_spec(0), u_spec(1), u_spec(2),
                w_spec(0, HY_SHORT), w_spec(1, HY_SHORT), w_spec(2, HY_SHORT),
                w_spec(0, 1), w_spec(1, 1), w_spec(2, 1),
                pl.BlockSpec(cs.shape, lambda c, b: (0, 0), **once),
                pl.BlockSpec((HY_ORDER, 2 * seq, tc), lambda c, b: (0, 0, c), **once)]
    bias_spec = pl.BlockSpec((HY_ORDER, 1, tc), lambda c, b: (0, 0, c))
    operands = [proj3, proj3, proj3, conv_w, conv_w, conv_w, conv_b, conv_b, conv_b, cs, hspec]
    if split:
        kern = functools.partial(_hyconv_split_kernel, seq=seq, seg=seg)
        in_specs += [pl.BlockSpec((HY_ORDER, 2, tc), lambda c, b: (0, 0, c)),
                     pl.BlockSpec((2, seq // 2, tc), lambda c, b: (0, 0, 0)), bias_spec]
        operands += [hm, tw, bias]
        scratch = [pltpu.VMEM((2, seq, tc), BF16), pltpu.VMEM((tc // LANES, seq, LANES), F32),
                   pltpu.VMEM((tc // LANES, seq, LANES), F32)]
    else:
        kern = functools.partial(_hyconv_kernel, seq=seq, seg=seg)
        in_specs += [pl.BlockSpec((HY_ORDER, 1, tc), lambda c, b: (0, 0, c)), bias_spec]
        operands += [hl, bias]
        scratch = [pltpu.VMEM((bp, 2 * seq, tc), BF16)]
    cast_specs, cast_shapes = _cast_rider_specs(cast_srcs, per * n_b, lambda c, b: c * n_b + b)
    kern = _with_cast_riders(kern, len(operands), 1, len(cast_srcs))
    return pl.pallas_call(
        kern,
        grid=(per, n_b),
        in_specs=in_specs + cast_specs,
        out_specs=[pl.BlockSpec((bp, seq, tc), lambda c, b: (b, 0, c))] + cast_specs,
        out_shape=[jax.ShapeDtypeStruct((batch, seq, width), BF16)] + cast_shapes,
        scratch_shapes=scratch,
        compiler_params=_cparams(("arbitrary", "arbitrary")),
        name="hy_conv",
    )(*operands, *cast_srcs)


def _mixout_kernel(*refs, gla_width, n_own, aliased):
    (yg_ref, zh_ref, x_ref, mod_ref, ghy_ref, gpost_ref, gpre_ref, wo_ref, wrh_ref, wrl_ref, br_ref,
     cnt_in_ref) = refs[:12]
    x1_ref, hg_ref, route_ref, cnt_ref, cnt_scr = refs[12 + int(aliased):]
    d = x_ref.shape[1]
    tm = x_ref.shape[0] // MIX_SUBTILES
    i = pl.program_id(0)

    @pl.when(i == 0)
    def _():
        cnt_scr[...] = cnt_in_ref[...]

    @pl.when(i < n_own)
    def _():
        for s in range(MIX_SUBTILES):
            _mixout_rows(slice(s * tm, (s + 1) * tm), tm, d, gla_width, yg_ref, zh_ref, x_ref, mod_ref, ghy_ref,
                         gpost_ref, gpre_ref, wo_ref, wrh_ref, wrl_ref, br_ref, x1_ref, hg_ref, route_ref, cnt_scr)

    @pl.when(i >= n_own)
    def _():
        hg_ref[...] = jnp.zeros_like(hg_ref)

    cnt_ref[...] = cnt_scr[...]


def _mixout_rows(rows, tm, d, gla_width, yg_ref, zh_ref, x_ref, mod_ref, ghy_ref, gpost_ref, gpre_ref, wo_ref,
                 wrh_ref, wrl_ref, br_ref, x1_ref, hg_ref, route_ref, cnt_scr):
    yh = _rms(zh_ref[rows, :].astype(F32), ghy_ref[...]).astype(BF16)
    y = _dot(yg_ref[rows, :], wo_ref[:gla_width, :]) + _dot(yh, wo_ref[gla_width:, :])
    x1 = x_ref[rows, :] + _rms(y, gpost_ref[...] * mod_ref[0, 2:3, :])
    x1_ref[rows, :] = x1
    h2 = _rms(x1, gpre_ref[...] * (1.0 + mod_ref[0, 4:5, :])) + mod_ref[0, 3:4, :]
    hh, hl = _split2(h2)
    hg_ref[rows, :d] = h2
    logits = _dot(hh, wrh_ref[...]) + (_dot(hh, wrl_ref[...]) + _dot(hl, wrh_ref[...])) + br_ref[...]

    lane = lax.broadcasted_iota(jnp.int32, logits.shape, 1).astype(F32)
    neg = -jnp.inf
    is_grp = (lane >= N_EXPERTS) & (lane < N_EXPERTS + N_GROUPS)
    m = jnp.max(jnp.where(is_grp, logits, neg), axis=1, keepdims=True)
    p_grp = 1.0 / jnp.sum(jnp.where(is_grp, jnp.exp(logits - m), 0.0), axis=1, keepdims=True)
    grp = jnp.min(jnp.where(is_grp & (logits == m), lane - N_EXPERTS, 1e9), axis=1, keepdims=True)
    sel = (lane >= grp * EXP_PER_GROUP) & (lane < (grp + 1.0) * EXP_PER_GROUP)
    me = jnp.max(jnp.where(sel, logits, neg), axis=1, keepdims=True)
    pe = jnp.where(sel, jnp.exp(logits - me), -1.0)
    v1 = jnp.max(pe, axis=1, keepdims=True)
    i1 = jnp.min(jnp.where(pe == v1, lane, 1e9), axis=1, keepdims=True)
    pe2 = jnp.where(lane == i1, -1.0, pe)
    v2 = jnp.max(pe2, axis=1, keepdims=True)
    i2 = jnp.min(jnp.where(pe2 == v2, lane, 1e9), axis=1, keepdims=True)
    den = v1 + v2
    gates = jnp.where(lane == i1, v1 / den, jnp.where(lane == i2, v2 / den, 0.0)) * p_grp

    lo = jnp.minimum(i1, i2) - grp * EXP_PER_GROUP
    hi = jnp.maximum(i1, i2) - grp * EXP_PER_GROUP
    bucket = grp * PAIRS_PER_GROUP + lo * (2 * EXP_PER_GROUP - 1 - lo) * 0.5 + (hi - lo - 1.0)
    onehot = lane == bucket
    r_i = lax.broadcasted_iota(jnp.int32, (tm, tm), 0)
    c_i = lax.broadcasted_iota(jnp.int32, (tm, tm), 1)
    earlier = _dot((r_i > c_i).astype(BF16), onehot.astype(BF16)) + cnt_scr[...]
    rank = jnp.sum(jnp.where(onehot, earlier, 0.0), axis=1, keepdims=True)
    cnt_scr[...] += jnp.sum(onehot.astype(F32), axis=0, keepdims=True)
    route = jnp.where(lane == ROUTE_BUCKET_LANE, bucket, jnp.where(lane == ROUTE_RANK_LANE, rank, gates))
    hg_ref[rows, d:] = route
    route_ref[:, rows] = route.T[ROUTE_BUCKET_LANE:ROUTE_BUCKET_LANE + SUBLANES, :]


def _mix_out(yg, zh, x, mod, mod_row, g_hy, g_post, g_pre, w_out, wr_hi, wr_lo, b_r, cnt_in, hg_all, row0, n_total, tm):
    n, d = x.shape
    gw = yg.shape[1]
    hw = zh.shape[1]
    aliased = hg_all is not None
    n_own = n // tm
    n_steps = n_own if aliased else n_total // tm
    assert row0 % tm == 0 and n_total % tm == 0 and (row0 + n == n_total if aliased else row0 == 0)
    kern = functools.partial(_mixout_kernel, gla_width=gw, n_own=n_own, aliased=aliased)
    row = lambda i: (jnp.minimum(i, n_own - 1), 0)
    fixed = lambda i: (0, 0)
    in_specs = [pl.BlockSpec((tm, gw), row),
                pl.BlockSpec((tm, hw), row),
                pl.BlockSpec((tm, d), row),
                pl.BlockSpec((1, 6, d), lambda i: (mod_row(jnp.minimum(i, n_own - 1) * tm), 0, 0)),
                pl.BlockSpec((1, hw), fixed),
                pl.BlockSpec((1, d), fixed),
                pl.BlockSpec((1, d), fixed),
                pl.BlockSpec((gw + hw, d), fixed),
                pl.BlockSpec((d, LANES), fixed),
                pl.BlockSpec((d, LANES), fixed),
                pl.BlockSpec((1, LANES), fixed),
                pl.BlockSpec((1, LANES), fixed)]
    operands = [yg, zh, x, mod, g_hy, g_post, g_pre, w_out, wr_hi, wr_lo, b_r, cnt_in]
    if aliased:
        in_specs.append(pl.BlockSpec(memory_space=pl.ANY))
        operands.append(hg_all)
    return pl.pallas_call(
        kern,
        grid=(n_steps,),
        in_specs=in_specs,
        out_specs=[pl.BlockSpec((tm, d), row),
                   pl.BlockSpec((tm, d + LANES), lambda i: (row0 // tm + i, 0)),
                   pl.BlockSpec((SUBLANES, tm), lambda i: (0, jnp.minimum(i, n_own - 1))),
                   pl.BlockSpec((1, LANES), fixed)],
        out_shape=[jax.ShapeDtypeStruct((n, d), F32),
                   jax.ShapeDtypeStruct((n_total, d + LANES), F32),
                   jax.ShapeDtypeStruct((SUBLANES, n), F32),
                   jax.ShapeDtypeStruct((1, LANES), F32)],
        scratch_shapes=[pltpu.VMEM((1, LANES), F32)],
        input_output_aliases={len(operands) - 1: 1} if aliased else {},
        compiler_params=_cparams(("arbitrary",)),
        name="mix_out",
    )(*operands)


def _row_gather_copy(src_hbm, row, buf, slot, r, sem):
    return pltpu.make_async_copy(src_hbm.at[pl.ds(row, 1), :], buf.at[slot, pl.ds(r, 1), :], sem.at[slot])


def _start_row_gather(idx_ref, base, src_hbm, buf, slot, sem, tm):
    def body(r, carry):
        _row_gather_copy(src_hbm, idx_ref[base + r], buf, slot, r, sem).start()
        return carry

    lax.fori_loop(0, tm, body, 0, unroll=8)


def _start_row_gather_inline(idx_ref, base, src_hbm, buf, slot, sem, tm):
    for r in range(tm):
        _row_gather_copy(src_hbm, idx_ref[base + r], buf, slot, r, sem).start()


def _wait_row_gather(src_hbm, buf, slot, sem, tm):
    pltpu.make_async_copy(src_hbm.at[pl.ds(0, tm), :], buf.at[slot], sem.at[slot]).wait()


def _moe_kernel(src_ref, ea_ref, eb_ref, valid_ref, hg_hbm,
                wga_ref, wua_ref, wda_ref, wgb_ref, wub_ref, wdb_ref, y_ref, buf, sem, *, tm, d):
    t = pl.program_id(0)
    slot = t % 2

    @pl.when(t == 0)
    def _():
        _start_row_gather(src_ref, 0, hg_hbm, buf, 0, sem, tm)

    @pl.when(valid_ref[t] == 1)
    def _():
        _wait_row_gather(hg_hbm, buf, slot, sem, tm)
        _start_row_gather_inline(src_ref, (t + 1) * tm, hg_hbm, buf, 1 - slot, sem, tm)
        h = buf[slot, :, :d].astype(BF16)
        gates = buf[slot, :, d:]
        lane = lax.broadcasted_iota(jnp.int32, gates.shape, 1)

        def expert(wg_ref, wu_ref, wd_ref, e):
            he = (_silu(_dot(h, wg_ref[0])) * _dot(h, wu_ref[0])).astype(BF16)
            gate = jnp.sum(jnp.where(lane == e, gates, 0.0), axis=1, keepdims=True)
            return gate * _dot(he, wd_ref[0])

        y_ref[...] = expert(wga_ref, wua_ref, wda_ref, ea_ref[t]) + expert(wgb_ref, wub_ref, wdb_ref, eb_ref[t])

    @pl.when(valid_ref[t] == 0)
    def _():
        @pl.when(valid_ref[jnp.maximum(t - 1, 0)] == 1)
        def _():
            _wait_row_gather(hg_hbm, buf, slot, sem, tm)

        y_ref[...] = jnp.zeros_like(y_ref)


def _moe(hg, src, ea, eb, valid, wg, wu, wd, n_tiles, tm):
    d = hg.shape[1] - LANES
    _, _, de = wg.shape
    kern = functools.partial(_moe_kernel, tm=tm, d=d)
    wa = lambda t, src, ea, eb, valid: (ea[t], 0, 0)
    wb = lambda t, src, ea, eb, valid: (eb[t], 0, 0)
    return pl.pallas_call(
        kern,
        grid_spec=pltpu.PrefetchScalarGridSpec(
            num_scalar_prefetch=4,
            grid=(n_tiles,),
            in_specs=[pl.BlockSpec(memory_space=pl.ANY),
                      pl.BlockSpec((1, d, de), wa), pl.BlockSpec((1, d, de), wa), pl.BlockSpec((1, de, d), wa),
                      pl.BlockSpec((1, d, de), wb), pl.BlockSpec((1, d, de), wb), pl.BlockSpec((1, de, d), wb)],
            out_specs=pl.BlockSpec((tm, d), lambda t, src, ea, eb, valid: (t, 0)),
            scratch_shapes=[pltpu.VMEM((2, tm, d + LANES), F32), pltpu.SemaphoreType.DMA((2,))]),
        out_shape=jax.ShapeDtypeStruct((n_tiles * tm, d), F32),
        compiler_params=_cparams(("arbitrary",)),
        name="moe",
    )(src, ea, eb, valid, hg, wg, wu, wd, wg, wu, wd)


def _ffn_out_kernel(pos_ref, y_hbm, x1_ref, mod_ref, gpost_ref, o_ref, buf, sem, *, tm):
    i = pl.program_id(0)
    n_i = pl.num_programs(0)
    slot = i % 2

    @pl.when(i == 0)
    def _():
        _start_row_gather(pos_ref, 0, y_hbm, buf, 0, sem, tm)

    _wait_row_gather(y_hbm, buf, slot, sem, tm)

    @pl.when(i + 1 < n_i)
    def _():
        _start_row_gather_inline(pos_ref, (i + 1) * tm, y_hbm, buf, 1 - slot, sem, tm)
        o_ref[...] = x1_ref[...] + _rms(buf[slot], gpost_ref[...] * mod_ref[0, 5:6, :])

    @pl.when(i + 1 == n_i)
    def _():
        o_ref[...] = x1_ref[...] + _rms(buf[slot], gpost_ref[...] * mod_ref[0, 5:6, :])


def _ffn_out(y_sorted, pos, x1, mod, mod_row, g_post, tm):
    n, d = x1.shape
    kern = functools.partial(_ffn_out_kernel, tm=tm)
    return pl.pallas_call(
        kern,
        grid_spec=pltpu.PrefetchScalarGridSpec(
            num_scalar_prefetch=1,
            grid=(n // tm,),
            in_specs=[pl.BlockSpec(memory_space=pl.ANY),
                      pl.BlockSpec((tm, d), lambda i, pos: (i, 0)),
                      pl.BlockSpec((1, 6, d), lambda i, pos: (mod_row(i * tm), 0, 0)),
                      pl.BlockSpec((1, d), lambda i, pos: (0, 0))],
            out_specs=pl.BlockSpec((tm, d), lambda i, pos: (i, 0)),
            scratch_shapes=[pltpu.VMEM((2, tm, d), F32), pltpu.SemaphoreType.DMA((2,))]),
        out_shape=jax.ShapeDtypeStruct((n, d), F32),
        compiler_params=_cparams(("arbitrary",)),
        name="ffn_out",
    )(pos, y_sorted, x1, mod, g_post)


def _route_tables(route, counts, n_tiles, tm):
    n = route.shape[1]
    n_buckets = N_GROUPS * PAIRS_PER_GROUP
    cnt = counts[0, :n_buckets].astype(jnp.int32)
    padded = (cnt + tm - 1) // tm * tm
    ends = jnp.cumsum(padded)
    starts = ends - padded
    bucket = route[0].astype(jnp.int32)
    rank = route[ROUTE_RANK_LANE - ROUTE_BUCKET_LANE].astype(jnp.int32)
    pos = starts[bucket] + rank
    src = (jnp.arange(n_tiles * tm, dtype=jnp.int32) % n).at[pos].set(jnp.arange(n, dtype=jnp.int32))
    n_valid = ends[-1] // tm
    tile = jnp.arange(n_tiles, dtype=jnp.int32)
    used = jnp.minimum(tile, n_valid - 1)
    tile_bucket = jnp.sum((ends[None, :] <= (used * tm)[:, None]).astype(jnp.int32), axis=1)
    grp = tile_bucket // PAIRS_PER_GROUP
    pair = tile_bucket % PAIRS_PER_GROUP
    pair_lo = jnp.array([a for a in range(EXP_PER_GROUP) for b in range(a + 1, EXP_PER_GROUP)], jnp.int32)
    pair_hi = jnp.array([b for a in range(EXP_PER_GROUP) for b in range(a + 1, EXP_PER_GROUP)], jnp.int32)
    ea = grp * EXP_PER_GROUP + pair_lo[pair]
    eb = grp * EXP_PER_GROUP + pair_hi[pair]
    valid = (tile < n_valid).astype(jnp.int32)
    return pos, src, ea, eb, valid


def _dft_kernel(cs_ref, *, seq, tr):
    k = pl.program_id(0) * tr + lax.broadcasted_iota(jnp.int32, (tr, LANES), 0)
    j = lax.broadcasted_iota(jnp.int32, (tr, LANES), 1)
    period = 2 * seq

    def table(step):
        ang = ((k * j * step) % period).astype(F32) * (math.pi / seq)
        return jnp.cos(ang), jnp.sin(ang)

    c0, s0 = table(1)
    c1, s1 = table(LANES)
    for t1 in range(seq // LANES):
        cols = slice(t1 * LANES, (t1 + 1) * LANES)
        ca = c1[:, t1:t1 + 1]
        sa = s1[:, t1:t1 + 1]
        cs_ref[0, :, cols] = (ca * c0 - sa * s0).astype(cs_ref.dtype)
        cs_ref[1, :, cols] = (sa * c0 + ca * s0).astype(cs_ref.dtype)


def _dft_cos_sin(seq, n_rows, tr):
    kern = functools.partial(_dft_kernel, seq=seq, tr=tr)
    return pl.pallas_call(
        kern,
        grid=(n_rows // tr,),
        out_specs=pl.BlockSpec((2, tr, seq), lambda r: (0, r, 0)),
        out_shape=jax.ShapeDtypeStruct((2, n_rows, seq), BF16),
        compiler_params=_cparams(("arbitrary",)),
        name="dft_matrix",
    )()


def _positional_features(seq):
    t = jnp.arange(seq, dtype=F32)
    t01 = t / max(seq - 1, 1)
    ang = 2.0 * math.pi * t / seq
    bands = jnp.linspace(1e-4, HY_BANDS - 1, HY_BANDS, dtype=F32)
    pe = jnp.concatenate([t01[:, None], jnp.cos(ang[:, None] * bands), -jnp.sin(ang[:, None] * bands)], axis=-1)
    return jnp.pad(pe, ((0, 0), (0, LANES - pe.shape[1])))


def _mixer_and_router(x3, mod, mod_row, s0f, s0b, n_seg, p, tiles, cnt_in, hg_all, row0, n_total, w_out, casts):
    batch, seq, d = x3.shape
    x = x3.reshape(batch * seq, d)
    dk, dv = p["dk"], p["dv"]
    hy_width = p["hy_width"]
    casted = {}

    proj, lr, *casted["premix"] = _premix_proj(x, mod, mod_row, p["g_pre_mix"], p["w_main"], p["w_lr"],
                                               tiles["tm_proj"], tiles["tn_proj"], casts["premix"])
    if w_out is None:
        w_out = casted["premix"][0]
    y_gla, s_f, s_b, *casted["gla"] = _gla(proj, lr, p["wdf"], p["bdf"], p["wdb"], p["bdb"], p["g_gla"], s0f, s0b,
                                           batch, seq, dk, dv, tiles["gla_heads_per_step"], casts["gla"])

    ad, hl, hm = _hy_filter(_positional_features(seq), p["hy_w1"], p["hy_b1"], p["hy_w2"], p["hy_b2"], p["hy_freq"],
                            p["hy_w3"], p["hy_deltas"], seq, hy_width, tiles["tc_filter"])
    tr, tc_conv = tiles["tr_spec"], tiles["tc_conv"]
    if tiles["hy_split"]:
        half = seq // 2
        hspec = _hy_spectrum(_dft_cos_sin(seq, half, tr), ad, seq, half, 4, hy_width, tr, tiles["tc_spec"])
        cs = _dft_cos_sin(half, half, tr).reshape(seq, half)
        ang = jnp.arange(half, dtype=F32) * (math.pi / seq)
        tw = jnp.broadcast_to(jnp.stack([jnp.cos(ang), jnp.sin(ang)])[:, :, None], (2, half, tc_conv))
    else:
        cs3 = _dft_cos_sin(seq, seq, tr)
        hspec = _hy_spectrum(cs3, ad, seq, seq, 2, hy_width, tr, tiles["tc_spec"])
        cs = cs3.reshape(2 * seq, seq)
        tw = None
    z_hy, *casted["hy_conv"] = _hy_conv(proj, p["hy_conv_w"], p["hy_conv_b"], cs, hspec, hl, hm, tw, p["hy_bias"],
                                        batch, seq, seq // n_seg, p["hy_col0"], hy_width, tc_conv, tiles["hy_split"],
                                        tiles["hy_batch_per_step"], casts["hy_conv"])

    x1, hg_all, route, counts = _mix_out(y_gla.reshape(batch * seq, -1), z_hy.reshape(batch * seq, -1), x, mod,
                                         mod_row, p["g_hy"], p["g_post_mix"], p["g_pre_ffn"], w_out, p["wr_hi"],
                                         p["wr_lo"], p["b_r"], cnt_in, hg_all, row0, n_total, tiles["tm_mix"])
    return x1, hg_all, route, counts, s_f, s_b, casted


def kernel(x_prompt, x_sample, c, state_gla_fwd, state_gla_bwd, c_ctx, w_ada, b_ada, g_pre_mix, g_post_mix, g_pre_ffn, g_post_ffn, w_in, w_dec_f, b_dec_f, w_dec_b, b_dec_b, g_gla, hy_conv_w, hy_conv_b, hy_w1, hy_b1, hy_w2, hy_b2, hy_w3, hy_freq, hy_bias, g_hy, w_out, w_router_grp, b_router_grp, w_router_exp, b_router_exp, w_exp_gate, w_exp_up, w_exp_down):
    depth = w_ada.shape[0]
    assert depth == 1
    l = 0
    d = x_prompt.shape[-1]
    dec_batch = x_sample.shape[0]
    heads = GLA_HEADS
    dk, dv = state_gla_fwd.shape[-2:]
    kdim = heads * dk
    gla_width = heads * dv
    hy_width = g_hy.shape[-1]
    hid = hy_w2.shape[-1]

    cond = jnp.concatenate([c_ctx[None, :], c], axis=0)
    cond = jnp.pad(cond, ((0, -cond.shape[0] % SUBLANES), (0, 0)))
    mod = _ada_mod(cond, w_ada[l], b_ada[l]).reshape(cond.shape[0], 6, d)

    n_main = 2 * kdim + 2 * gla_width
    w_main, w_lr = _repack_w_in(w_in, l, n_main, 2 * GLA_RANK, REPACK_COLS)

    def dec_weight(w, first_row):
        wh = w.reshape(GLA_RANK, heads, dk).transpose(1, 0, 2)
        return jnp.pad(wh, ((0, 0), (first_row, LANES - GLA_RANK - first_row), (0, 0)))

    deltas = jnp.abs(jnp.linspace(math.log(HY_TARGET) / HY_SLOW_PCT, math.log(HY_TARGET) / HY_FAST_PCT, hy_width,
                                  dtype=F32))
    w_r = jnp.pad(jnp.concatenate([w_router_exp[l], w_router_grp[l]], axis=1),
                  ((0, 0), (0, LANES - N_EXPERTS - N_GROUPS)))
    wr_hi, wr_lo = _split2(w_r)
    b_r = jnp.pad(jnp.concatenate([b_router_exp[l], b_router_grp[l]]), (0, LANES - N_EXPERTS - N_GROUPS))

    p = dict(
        dk=dk, dv=dv, hy_width=hy_width, hy_col0=n_main,
        g_pre_mix=g_pre_mix[l][None, :], g_post_mix=g_post_mix[l][None, :],
        g_pre_ffn=g_pre_ffn[l][None, :], g_post_ffn=g_post_ffn[l][None, :],
        w_main=w_main, w_lr=w_lr,
        wdf=dec_weight(w_dec_f[l], 0), bdf=b_dec_f[l][None, :],
        wdb=dec_weight(w_dec_b[l], GLA_RANK), bdb=b_dec_b[l][None, :],
        g_gla=g_gla[l][None, :],
        hy_conv_w=hy_conv_w[l], hy_conv_b=hy_conv_b[l][None, :],
        hy_w1=jnp.pad(hy_w1[l], ((0, LANES - hy_w1.shape[1]), (0, 0))), hy_b1=hy_b1[l][None, :],
        hy_w2=hy_w2[l], hy_b2=hy_b2[l][None, :], hy_freq=hy_freq[l][None, :],
        hy_w3=hy_w3[l].reshape(hid, 2 * HY_ORDER, hy_width).transpose(1, 0, 2), hy_deltas=deltas[None, :],
        hy_bias=hy_bias[l][:, None, :], g_hy=g_hy[l][None, :],
        wr_hi=wr_hi, wr_lo=wr_lo, b_r=b_r[None, :],
    )

    zero_state = jnp.zeros((1, heads, dk, dv), F32)
    dec_seq = x_sample.shape[1]
    tiles_p = _tile_plan(x_prompt.shape[1])
    tiles_s = _tile_plan(dec_seq)
    tm_moe = MXU_DIM
    tm_out = 512

    n_p = x_prompt.shape[0] * x_prompt.shape[1]
    n_s = dec_batch * dec_seq
    n_all = n_p + n_s
    mod_row_p = lambda r: 0
    mod_row_s = lambda r: 1 + r // dec_seq
    n_exp, _, d_exp = w_exp_gate.shape[1:]
    casts_p = dict(premix=[w_out[l]],
                   gla=[w_exp_gate[l].reshape(n_exp * d, d_exp), w_exp_down[l].reshape(n_exp * d_exp, d)],
                   hy_conv=[w_exp_up[l].reshape(n_exp * d, d_exp)])
    casts_s = dict(premix=[], gla=[], hy_conv=[])
    x1_p, hg_all, route_p, counts, s_f, s_b, casted = _mixer_and_router(
        x_prompt, mod, mod_row_p, zero_state, zero_state, 1, p, tiles_p, jnp.zeros((1, LANES), F32), None, 0, n_all,
        None, casts_p)
    (w_out_b,), (w_gate_b, w_down_b), (w_up_b,) = casted["premix"], casted["gla"], casted["hy_conv"]
    x1_s, hg_all, route_s, counts, _, _, _ = _mixer_and_router(
        x_sample, mod, mod_row_s, state_gla_fwd[:, l], state_gla_bwd[:, l], dec_seq // GRID_W, p, tiles_s,
        counts, hg_all, n_p, n_all, w_out_b, casts_s)

    n_tiles = n_all // tm_moe + N_GROUPS * PAIRS_PER_GROUP
    pos, src, ea, eb, valid = _route_tables(jnp.concatenate([route_p, route_s], axis=1), counts, n_tiles, tm_moe)
    y_sorted = _moe(hg_all, src, ea, eb, valid, w_gate_b.reshape(n_exp, d, d_exp), w_up_b.reshape(n_exp, d, d_exp),
                    w_down_b.reshape(n_exp, d_exp, d), n_tiles, tm_moe)
    y_p = _ffn_out(y_sorted, pos[:n_p], x1_p, mod, mod_row_p, p["g_post_ffn"], tm_out)
    y_s = _ffn_out(y_sorted, pos[n_p:], x1_s, mod, mod_row_s, p["g_post_ffn"], tm_out)
    return (y_p.reshape(x_prompt.shape), y_s.reshape(x_sample.shape),
            s_f[:, None].astype(x_prompt.dtype), s_b[:, None].astype(x_prompt.dtype))
```

```python
import functools
import math

import jax
import jax.numpy as jnp
from jax import lax
from jax.experimental import pallas as pl
from jax.experimental.pallas import tpu as pltpu

F32 = jnp.float32
BF16 = jnp.bfloat16

GRID_W = 64
GLA_HEADS = 4
GLA_RANK = 16
GLA_TAU = 16.0
GLA_CHUNK = 64
GLA_UNROLL = 4
MIX_SUBTILES = 1
HY_ORDER = 2
HY_SHORT = 3
HY_BANDS = 16
HY_TARGET = 1e-2
HY_FAST_PCT = 0.3
HY_SLOW_PCT = 1.5
N_GROUPS = 4
EXP_PER_GROUP = 4
N_EXPERTS = N_GROUPS * EXP_PER_GROUP
PAIRS_PER_GROUP = EXP_PER_GROUP * (EXP_PER_GROUP - 1) // 2
ROUTE_BUCKET_LANE = N_EXPERTS
ROUTE_RANK_LANE = N_EXPERTS + 1
EPS = 1e-6

LANES = 128
SUBLANES = 8
MXU_DIM = 256
VMEM_LIMIT = 56 << 20
REPACK_COLS = MXU_DIM
FREQ_BLOCK = 2 * MXU_DIM


def _tile_plan(seq):
    long_seq = seq >= 1024
    return dict(
        tm_proj=1024, tn_proj=2048,
        tc_filter=256,
        tr_spec=512 if long_seq else 256,
        tc_spec=1024 if long_seq else 512,
        hy_split=long_seq,
        tc_conv=256 if long_seq else 512,
        hy_batch_per_step=1 if long_seq else 2,
        gla_heads_per_step=2 if long_seq else 4,
        tm_mix=512,
    )


def _cparams(sem):
    return pltpu.CompilerParams(dimension_semantics=sem, vmem_limit_bytes=VMEM_LIMIT)


def _dot(a, b):
    return jnp.dot(a, b, preferred_element_type=F32)


def _dot_nt(a, b):
    return lax.dot_general(a, b, (((1,), (1,)), ((), ())), preferred_element_type=F32)


def _dot_tn(a, b):
    return lax.dot_general(a, b, (((0,), (0,)), ((), ())), preferred_element_type=F32)


def _split2(x):
    hi = x.astype(BF16)
    lo = (x - hi.astype(F32)).astype(BF16)
    return hi, lo


def _dot_hp(a, b):
    ah, al = a if isinstance(a, tuple) else _split2(a)
    bh, bl = _split2(b)
    return _dot(ah, bh) + (_dot(ah, bl) + _dot(al, bh))


def _dot_exact_lhs(t, x):
    hi, lo = _split2(x)
    return _dot(t, hi) + _dot(t, lo)


def _rms(x, g):
    return x * lax.rsqrt(jnp.mean(x * x, axis=-1, keepdims=True) + EPS) * g


def _silu(x):
    return x / (1.0 + jnp.exp(-x))


def _ada_kernel(c_ref, w_ref, b_ref, o_ref):
    s = _silu(c_ref[...]).astype(BF16)
    o_ref[...] = _dot(s, w_ref[...].astype(BF16)) + b_ref[...]


def _ada_mod(cond, w_ada, b_ada):
    rows, d = cond.shape
    n = w_ada.shape[1]
    tn = 1024
    return pl.pallas_call(
        _ada_kernel,
        grid=(n // tn,),
        in_specs=[pl.BlockSpec((rows, d), lambda j: (0, 0)),
                  pl.BlockSpec((d, tn), lambda j: (0, j)),
                  pl.BlockSpec((1, tn), lambda j: (0, j))],
        out_specs=pl.BlockSpec((rows, tn), lambda j: (0, j)),
        out_shape=jax.ShapeDtypeStruct((rows, n), F32),
        compiler_params=_cparams(("arbitrary",)),
        name="ada_mod",
    )(cond, w_ada, b_ada.reshape(1, n))


def _repack_kernel(wt_hbm, main_ref, lr_ref, buf, lr_buf, sem, lr_sem, *, layer, n_main, n_lr, tb):
    i = pl.program_id(0)
    n_i = pl.num_programs(0)
    slot = i % 2
    d = main_ref.shape[0]

    def block_copy(j, s):
        first = j * tb + jnp.where(j * tb >= n_main, n_lr, 0)
        return pltpu.make_async_copy(wt_hbm.at[layer, pl.ds(first, tb), :], buf.at[s], sem.at[s])

    lr_copy = pltpu.make_async_copy(wt_hbm.at[layer, pl.ds(n_main, n_lr), :], lr_buf, lr_sem)

    @pl.when(i == 0)
    def _():
        block_copy(0, 0).start()
        lr_copy.start()

    @pl.when(i + 1 < n_i)
    def _():
        block_copy(i + 1, 1 - slot).start()

    block_copy(i, slot).wait()
    main_ref[...] = buf[slot].T.astype(BF16)

    @pl.when(i == 0)
    def _():
        lr_copy.wait()
        lr = jnp.concatenate([lr_buf[...], jnp.zeros((LANES - n_lr, d), F32)], axis=0).T
        lr_ref[...] = lr.astype(BF16)


def _repack_w_in(w_in, layer, n_main, n_lr, tb):
    w_t = jnp.swapaxes(w_in, 1, 2)
    _, n_cols, d = w_t.shape
    n_out = n_cols - n_lr
    assert n_main % tb == 0 and n_out % tb == 0
    kern = functools.partial(_repack_kernel, layer=layer, n_main=n_main, n_lr=n_lr, tb=tb)
    return pl.pallas_call(
        kern,
        grid=(n_out // tb,),
        in_specs=[pl.BlockSpec(memory_space=pl.ANY)],
        out_specs=[pl.BlockSpec((d, tb), lambda i: (0, i)),
                   pl.BlockSpec((d, LANES), lambda i: (0, 0))],
        out_shape=[jax.ShapeDtypeStruct((d, n_out), BF16),
                   jax.ShapeDtypeStruct((d, LANES), BF16)],
        scratch_shapes=[pltpu.VMEM((2, tb, d), F32), pltpu.VMEM((n_lr, d), F32),
                        pltpu.SemaphoreType.DMA((2,)), pltpu.SemaphoreType.DMA(())],
        compiler_params=_cparams(("arbitrary",)),
        name="repack_w_in",
    )(w_t)


def _with_cast_riders(kernel_fn, n_in, n_out, n_cast):
    def kernel(*refs):
        ins = refs[:n_in]
        cast_in = refs[n_in:n_in + n_cast]
        outs = refs[n_in + n_cast:n_in + n_cast + n_out]
        cast_out = refs[n_in + n_cast + n_out:n_in + 2 * n_cast + n_out]
        kernel_fn(*ins, *outs, *refs[n_in + 2 * n_cast + n_out:])
        for src, dst in zip(cast_in, cast_out):
            dst[...] = src[...].astype(dst.dtype)

    return kernel


def _cast_rider_specs(cast_srcs, n_steps, step_of):
    n_slabs = 1 << (n_steps.bit_length() - 1)
    specs = []
    for w in cast_srcs:
        rows, cols = w.shape
        assert rows % n_slabs == 0
        specs.append(pl.BlockSpec((rows // n_slabs, cols),
                                  lambda *idx: (jnp.minimum(step_of(*idx), n_slabs - 1), 0)))
    return specs, [jax.ShapeDtypeStruct(w.shape, BF16) for w in cast_srcs]


def _premix_kernel(x_ref, mod_ref, g_ref, w_ref, wlr_ref, o_ref, lr_ref, h_scr):
    @pl.when(pl.program_id(1) == 0)
    def _():
        h = _rms(x_ref[...], g_ref[...] * (1.0 + mod_ref[0, 1:2, :])) + mod_ref[0, 0:1, :]
        h_scr[...] = h.astype(BF16)
        lr_ref[...] = _dot(h_scr[...], wlr_ref[...])

    o_ref[...] = _dot(h_scr[...], w_ref[...]).astype(o_ref.dtype)


def _premix_proj(x, mod, mod_row, g, w_main, w_lr, tm, tn, cast_srcs):
    n, d = x.shape
    nc = w_main.shape[1]
    n_j = nc // tn
    cast_specs, cast_shapes = _cast_rider_specs(cast_srcs, (n // tm) * n_j, lambda i, j: i * n_j + j)
    kern = _with_cast_riders(_premix_kernel, 5, 2, len(cast_srcs))
    return pl.pallas_call(
        kern,
        grid=(n // tm, n_j),
        in_specs=[pl.BlockSpec((tm, d), lambda i, j: (i, 0)),
                  pl.BlockSpec((1, 6, d), lambda i, j: (mod_row(i * tm), 0, 0)),
                  pl.BlockSpec((1, d), lambda i, j: (0, 0)),
                  pl.BlockSpec((d, tn), lambda i, j: (0, j)),
                  pl.BlockSpec((d, LANES), lambda i, j: (0, 0))] + cast_specs,
        out_specs=[pl.BlockSpec((tm, tn), lambda i, j: (i, j)),
                   pl.BlockSpec((tm, LANES), lambda i, j: (i, 0))] + cast_specs,
        out_shape=[jax.ShapeDtypeStruct((n, nc), BF16), jax.ShapeDtypeStruct((n, LANES), F32)] + cast_shapes,
        scratch_shapes=[pltpu.VMEM((tm, d), BF16)],
        compiler_params=_cparams(("arbitrary", "arbitrary")),
        name="premix_proj",
    )(x, mod, g, w_main, w_lr, *cast_srcs)


def _log_sigmoid(x):
    return jnp.minimum(x, 0.0) - jnp.log(1.0 + jnp.exp(-jnp.abs(x)))


def _gla_kernel(q_ref, k_ref, v_ref, g_ref, lr_ref, wdf_ref, bdf_ref, wdb_ref, bdb_ref, gg_ref,
                s0f_ref, s0b_ref, y_ref, sf_ref, sb_ref,
                laf_scr, lab_scr, qf_scr, qb_scr, o_scr, uf_scr, ub_scr, df_scr, db_scr, *, seq, dk, dv, hp):
    c = GLA_CHUNK
    n_chunks = seq // c
    scale = dk ** -0.5
    heads = range(hp)
    kcols = [slice(h * dk, (h + 1) * dk) for h in heads]
    vcols = [slice(h * dv, (h + 1) * dv) for h in heads]

    lr_split = _split2(lr_ref[0])
    for h in heads:
        laf_scr[h] = _log_sigmoid(_dot_hp(lr_split, wdf_ref[h]) + bdf_ref[:, kcols[h]]) / GLA_TAU
        lab_scr[h] = _log_sigmoid(_dot_hp(lr_split, wdb_ref[h]) + bdb_ref[:, kcols[h]]) / GLA_TAU

    per = GLA_UNROLL
    blk = per * c
    row = lax.broadcasted_iota(jnp.int32, (blk, blk), 0)
    col = lax.broadcasted_iota(jnp.int32, (blk, blk), 1)
    same = (row // c) == (col // c)
    lower = same & (row >= col)
    upper = same & (col >= row)
    t_fwd = lower.astype(BF16)
    t_bwd = upper.astype(BF16)

    def chunk_rows(x, r):
        return jnp.concatenate([jnp.broadcast_to(x[j * c + r:j * c + r + 1], (c, dk)) for j in range(per)], axis=0)

    def block_local(m, carry):
        sl = pl.ds(pl.multiple_of(m * blk, blk), blk)
        bf = [_dot_exact_lhs(t_fwd, laf_scr[h, sl, :]) for h in heads]
        bb = [_dot_exact_lhs(t_bwd, lab_scr[h, sl, :]) for h in heads]
        tot_f = [chunk_rows(x, c - 1) for x in bf]
        tot_b = [chunk_rows(x, 0) for x in bb]
        q = [q_ref[0, sl, kcols[h]].astype(F32) * scale for h in heads]
        k = [k_ref[0, sl, kcols[h]].astype(F32) for h in heads]
        v = [v_ref[0, sl, vcols[h]] for h in heads]
        qf = [(q[h] * jnp.exp(bf[h])).astype(BF16) for h in heads]
        kf = [(k[h] * jnp.exp(-bf[h])).astype(BF16) for h in heads]
        qb = [(q[h] * jnp.exp(bb[h])).astype(BF16) for h in heads]
        kb = [(k[h] * jnp.exp(-bb[h])).astype(BF16) for h in heads]
        ksf = [(k[h] * jnp.exp(tot_f[h] - bf[h])).astype(BF16) for h in heads]
        ksb = [(k[h] * jnp.exp(tot_b[h] - bb[h])).astype(BF16) for h in heads]
        sc_f = [_dot_nt(qf[h], kf[h]) for h in heads]
        sc_b = [_dot_nt(qb[h], kb[h]) for h in heads]
        att = [(jnp.where(lower, sc_f[h], 0.0) + jnp.where(upper, sc_b[h], 0.0)).astype(BF16) for h in heads]
        o_loc = [_dot(att[h], v[h]) for h in heads]
        dec_f = [jnp.exp(x) for x in tot_f]
        dec_b = [jnp.exp(x) for x in tot_b]
        for h in heads:
            o_scr[h, sl, :] = o_loc[h]
            qf_scr[h, sl, :] = qf[h]
            qb_scr[h, sl, :] = qb[h]
        for j in range(per):
            n = m * per + j
            rows = slice(j * c, (j + 1) * c)
            for h in heads:
                uf_scr[h, n] = _dot_tn(v[h][rows], ksf[h][rows])
                ub_scr[h, n] = _dot_tn(v[h][rows], ksb[h][rows])
                df_scr[h, n] = dec_f[h][j * c:j * c + SUBLANES]
                db_scr[h, n] = dec_b[h][j * c:j * c + SUBLANES]
        return carry

    lax.fori_loop(0, n_chunks // per, block_local, 0)

    def scan_fwd(n, s):
        upd = [uf_scr[h, n] for h in heads]
        for h in heads:
            uf_scr[h, n] = s[h]
        return tuple(s[h] * df_scr[h, n][0:1, :] + upd[h] for h in heads)

    def scan_bwd(i, s):
        n = n_chunks - 1 - i
        upd = [ub_scr[h, n] for h in heads]
        for h in heads:
            ub_scr[h, n] = s[h]
        return tuple(s[h] * db_scr[h, n][0:1, :] + upd[h] for h in heads)

    s_f = lax.fori_loop(0, n_chunks, scan_fwd, tuple(s0f_ref[0, h].T for h in heads), unroll=GLA_UNROLL)
    s_b = lax.fori_loop(0, n_chunks, scan_bwd, tuple(s0b_ref[0, h].T for h in heads), unroll=GLA_UNROLL)
    for h in heads:
        sf_ref[0, h] = s_f[h].T
        sb_ref[0, h] = s_b[h].T

    def chunk_inter(n, carry):
        sl = pl.ds(pl.multiple_of(n * c, c), c)
        inter = [_dot_nt(qf_scr[h, sl, :], uf_scr[h, n].astype(BF16))
                 + _dot_nt(qb_scr[h, sl, :], ub_scr[h, n].astype(BF16)) for h in heads]
        for h in heads:
            o_scr[h, sl, :] += inter[h]
        return carry

    lax.fori_loop(0, n_chunks, chunk_inter, 0, unroll=GLA_UNROLL)

    for h in heads:
        o = _rms(o_scr[h], gg_ref[...])
        y_ref[0, :, vcols[h]] = (o * _silu(g_ref[0, :, vcols[h]].astype(F32))).astype(y_ref.dtype)


def _gla(proj, lr, wdf, bdf, wdb, bdb, g_gla, s0f, s0b, batch, seq, dk, dv, hp, cast_srcs):
    heads = GLA_HEADS
    proj3 = proj.reshape(batch, seq, proj.shape[-1])
    lr3 = lr.reshape(batch, seq, LANES)
    kdim = heads * dk
    width = heads * dv
    bk, bv = hp * dk, hp * dv
    k_blk = kdim // bk
    v_blk = 2 * kdim // bv
    g_blk = (2 * kdim + width) // bv
    n_chunks = seq // GLA_CHUNK

    def s0_map(s0):
        if s0.shape[0] == batch:
            return lambda b, h: (b, h, 0, 0)
        return lambda b, h: (0, h, 0, 0)

    n_hsteps = heads // hp
    cast_specs, cast_shapes = _cast_rider_specs(cast_srcs, batch * n_hsteps, lambda b, h: b * n_hsteps + h)
    kern = _with_cast_riders(functools.partial(_gla_kernel, seq=seq, dk=dk, dv=dv, hp=hp), 12, 3, len(cast_srcs))
    return pl.pallas_call(
        kern,
        grid=(batch, n_hsteps),
        in_specs=[pl.BlockSpec((1, seq, bk), lambda b, h: (b, 0, h)),
                  pl.BlockSpec((1, seq, bk), lambda b, h: (b, 0, k_blk + h)),
                  pl.BlockSpec((1, seq, bv), lambda b, h: (b, 0, v_blk + h)),
                  pl.BlockSpec((1, seq, bv), lambda b, h: (b, 0, g_blk + h)),
                  pl.BlockSpec((1, seq, LANES), lambda b, h: (b, 0, 0)),
                  pl.BlockSpec((hp, LANES, dk), lambda b, h: (h, 0, 0)),
                  pl.BlockSpec((1, bk), lambda b, h: (0, h)),
                  pl.BlockSpec((hp, LANES, dk), lambda b, h: (h, 0, 0)),
                  pl.BlockSpec((1, bk), lambda b, h: (0, h)),
                  pl.BlockSpec((1, dv), lambda b, h: (0, 0)),
                  pl.BlockSpec((1, hp, dk, dv), s0_map(s0f)),
                  pl.BlockSpec((1, hp, dk, dv), s0_map(s0b))] + cast_specs,
        out_specs=[pl.BlockSpec((1, seq, bv), lambda b, h: (b, 0, h)),
                   pl.BlockSpec((1, hp, dk, dv), lambda b, h: (b, h, 0, 0)),
                   pl.BlockSpec((1, hp, dk, dv), lambda b, h: (b, h, 0, 0))] + cast_specs,
        out_shape=[jax.ShapeDtypeStruct((batch, seq, width), BF16),
                   jax.ShapeDtypeStruct((batch, heads, dk, dv), F32),
                   jax.ShapeDtypeStruct((batch, heads, dk, dv), F32)] + cast_shapes,
        scratch_shapes=[pltpu.VMEM((hp, seq, dk), F32), pltpu.VMEM((hp, seq, dk), F32),
                        pltpu.VMEM((hp, seq, dk), BF16), pltpu.VMEM((hp, seq, dk), BF16),
                        pltpu.VMEM((hp, seq, dv), F32),
                        pltpu.VMEM((hp, n_chunks, dv, dk), F32), pltpu.VMEM((hp, n_chunks, dv, dk), F32),
                        pltpu.VMEM((hp, n_chunks, SUBLANES, dk), F32), pltpu.VMEM((hp, n_chunks, SUBLANES, dk), F32)],
        compiler_params=_cparams(("arbitrary", "arbitrary")),
        name="gla",
    )(proj3, proj3, proj3, proj3, lr3, wdf, bdf, wdb, bdb, g_gla, s0f, s0b, *cast_srcs)


def _filter_kernel(pe_ref, w1_ref, b1_ref, w2_ref, b2_ref, fr_ref, w3_ref, dl_ref, ad_ref, hl_ref, hm_ref, mlp_scr,
                   *, seq):
    @pl.when(pl.program_id(0) == 0)
    def _():
        fr = fr_ref[...]
        h1 = jnp.sin(fr * (_dot_hp(pe_ref[...], w1_ref[...]) + b1_ref[...]))
        mlp_scr[...] = jnp.sin(fr * (_dot_hp(h1, w2_ref[...]) + b2_ref[...]))

    h2 = _split2(mlp_scr[...])
    dec = jnp.exp(-pe_ref[:, 0:1] * dl_ref[...])
    row = lax.broadcasted_iota(jnp.int32, (seq, 1), 0)
    alt = jnp.where(row % 2 == 0, 1.0, -1.0)
    phase = row % 4
    cos_half = jnp.where(phase == 0, 1.0, jnp.where(phase == 2, -1.0, 0.0))
    sin_half = jnp.where(phase == 1, 1.0, jnp.where(phase == 3, -1.0, 0.0))
    for o in range(HY_ORDER):
        ff = _dot_hp(h2, w3_ref[o]) * dec
        fb = _dot_hp(h2, w3_ref[HY_ORDER + o]) * dec
        nrm = jnp.sum(jnp.abs(ff), axis=0, keepdims=True) + jnp.sum(jnp.abs(fb), axis=0, keepdims=True)
        ff = ff / nrm
        fb = jnp.where(row == 0, 0.0, fb / nrm)
        a = ff + fb
        d = fb - ff
        ad_ref[o, 0] = a.astype(ad_ref.dtype)
        ad_ref[o, 1] = d.astype(ad_ref.dtype)
        ad_ref[o, 2] = (alt * a).astype(ad_ref.dtype)
        ad_ref[o, 3] = (-alt * d).astype(ad_ref.dtype)
        hl_ref[o] = jnp.sum(alt * a, axis=0, keepdims=True)
        hm_ref[o] = jnp.concatenate([jnp.sum(cos_half * a, axis=0, keepdims=True),
                                     jnp.sum(sin_half * d, axis=0, keepdims=True)], axis=0) * (1.0 / seq)


def _hy_filter(pe, w1p, b1, w2, b2, freq, w3r, deltas, seq, width, tc):
    hid = w2.shape[0]
    kern = functools.partial(_filter_kernel, seq=seq)
    return pl.pallas_call(
        kern,
        grid=(width // tc,),
        in_specs=[pl.BlockSpec((seq, LANES), lambda c: (0, 0)),
                  pl.BlockSpec((LANES, hid), lambda c: (0, 0)),
                  pl.BlockSpec((1, hid), lambda c: (0, 0)),
                  pl.BlockSpec((hid, hid), lambda c: (0, 0)),
                  pl.BlockSpec((1, hid), lambda c: (0, 0)),
                  pl.BlockSpec((1, hid), lambda c: (0, 0)),
                  pl.BlockSpec((2 * HY_ORDER, hid, tc), lambda c: (0, 0, c)),
                  pl.BlockSpec((1, tc), lambda c: (0, c))],
        out_specs=[pl.BlockSpec((HY_ORDER, 4, seq, tc), lambda c: (0, 0, 0, c)),
                   pl.BlockSpec((HY_ORDER, 1, tc), lambda c: (0, 0, c)),
                   pl.BlockSpec((HY_ORDER, 2, tc), lambda c: (0, 0, c))],
        out_shape=[jax.ShapeDtypeStruct((HY_ORDER, 4, seq, width), BF16),
                   jax.ShapeDtypeStruct((HY_ORDER, 1, width), F32),
                   jax.ShapeDtypeStruct((HY_ORDER, 2, width), F32)],
        scratch_shapes=[pltpu.VMEM((seq, hid), F32)],
        compiler_params=_cparams(("arbitrary",)),
        name="hy_filter",
    )(pe, w1p, b1, w2, b2, freq, w3r, deltas)


def _spectrum_kernel(cs_ref, ad_ref, h_ref, *, seq, per_part, tr):
    r = pl.program_id(1)
    k = (r % per_part) * tr + lax.broadcasted_iota(jnp.int32, (tr, 1), 0)
    wgt = jnp.where(k == 0, 1.0, 2.0) * (0.5 / seq)
    h_ref[0] = wgt * _dot(cs_ref[0], ad_ref[0, 0])


def _hy_spectrum(cs3, ad, seq, part_rows, n_parts, width, tr, tc):
    per_part = part_rows // tr
    kern = functools.partial(_spectrum_kernel, seq=seq, per_part=per_part, tr=tr)
    return pl.pallas_call(
        kern,
        grid=(HY_ORDER, n_parts * per_part, width // tc),
        in_specs=[pl.BlockSpec((1, tr, seq), lambda o, r, c: ((r // per_part) % 2, r % per_part, 0)),
                  pl.BlockSpec((1, 1, seq, tc), lambda o, r, c: (o, r // per_part, 0, c))],
        out_specs=pl.BlockSpec((1, tr, tc), lambda o, r, c: (o, r, c)),
        out_shape=jax.ShapeDtypeStruct((HY_ORDER, n_parts * part_rows, width), F32),
        compiler_params=_cparams(("arbitrary", "arbitrary", "arbitrary")),
        name="hy_spectrum",
    )(cs3, ad)


def _short_conv_fn(seq, seg):
    row = lax.broadcasted_iota(jnp.int32, (seq, 1), 0)
    pos = row % seg

    def short_conv(u_ref, cw_ref, cb_ref, b=0):
        u = u_ref[b].astype(F32)
        prev = jnp.where(pos == 0, 0.0, pltpu.roll(u, 1, 0))
        nxt = jnp.where(pos == seg - 1, 0.0, pltpu.roll(u, seq - 1, 0))
        return prev * cw_ref[0:1, :] + u * cw_ref[1:2, :] + nxt * cw_ref[2:3, :] + cb_ref[...]

    return short_conv


def _hyconv_kernel(u0_ref, u1_ref, u2_ref, cw0_ref, cw1_ref, cw2_ref, cb0_ref, cb1_ref, cb2_ref,
                   cs_ref, h_ref, hl_ref, bias_ref, z_ref, y_scr, *, seq, seg):
    short_conv = _short_conv_fn(seq, seg)
    row = lax.broadcasted_iota(jnp.int32, (seq, 1), 0)
    alt = jnp.where(row % 2 == 0, 1.0, -1.0)
    gates = ((u1_ref, cw1_ref, cb1_ref), (u2_ref, cw2_ref, cb2_ref))
    items = range(z_ref.shape[0])
    z = [short_conv(u0_ref, cw0_ref, cb0_ref, b) for b in items]
    kb = min(seq, FREQ_BLOCK)
    for n in range(HY_ORDER):
        zb = [z[b].astype(BF16) for b in items]
        for r in range(0, seq, kb):
            xc = [_dot(cs_ref[r:r + kb, :], zb[b]) for b in items]
            xs = [_dot(cs_ref[seq + r:seq + r + kb, :], zb[b]) for b in items]
            hre = h_ref[n, r:r + kb, :]
            him = h_ref[n, seq + r:seq + r + kb, :]
            for b in items:
                y_scr[b, r:r + kb, :] = (xc[b] * hre + xs[b] * him).astype(BF16)
                y_scr[b, seq + r:seq + r + kb, :] = (xs[b] * hre - xc[b] * him).astype(BF16)
        nyq = [jnp.sum(alt * z[b], axis=0, keepdims=True) * (hl_ref[n] * (0.5 / seq)) for b in items]
        conv = [_dot(cs_ref[:seq, :], y_scr[b, :seq, :]) + _dot(cs_ref[seq:, :], y_scr[b, seq:, :]) + alt * nyq[b]
                for b in items]
        gate = [short_conv(*gates[n], b) for b in items]
        z = [gate[b] * (conv[b] + bias_ref[n] * z[b]) for b in items]
    for b in items:
        z_ref[b] = z[b].astype(z_ref.dtype)


def _hyconv_split_kernel(u0_ref, u1_ref, u2_ref, cw0_ref, cw1_ref, cw2_ref, cb0_ref, cb1_ref, cb2_ref,
                         cs_ref, h_ref, hm_ref, tw_ref, bias_ref, z_ref, g_scr, z_scr, c_scr, *, seq, seg):
    m = seq // 2
    tc = z_ref.shape[2]
    n_lane_blocks = tc // LANES
    short_conv = _short_conv_fn(seq, seg)
    row = lax.broadcasted_iota(jnp.int32, (m, 1), 0)
    alt = jnp.where(row % 2 == 0, 1.0, -1.0)
    gates = ((u1_ref, cw1_ref, cb1_ref), (u2_ref, cw2_ref, cb2_ref))
    z = short_conv(u0_ref, cw0_ref, cb0_ref)
    kb = min(m, FREQ_BLOCK)
    for n in range(HY_ORDER):
        for j in range(n_lane_blocks):
            z_scr[j] = z[:, j * LANES:(j + 1) * LANES]
        ze = jnp.concatenate([z_scr[j, pl.ds(0, m, stride=2), :] for j in range(n_lane_blocks)], axis=1)
        zo = jnp.concatenate([z_scr[j, pl.ds(1, m, stride=2), :] for j in range(n_lane_blocks)], axis=1)
        e_mid = jnp.sum(alt * ze, axis=0, keepdims=True)
        o_mid = jnp.sum(alt * zo, axis=0, keepdims=True)
        zeb = ze.astype(BF16)
        zob = zo.astype(BF16)
        for r in range(0, m, kb):
            rows = slice(r, r + kb)
            srows = slice(m + r, m + r + kb)
            ec = _dot(cs_ref[rows, :], zeb)
            es = _dot(cs_ref[srows, :], zeb)
            oc = _dot(cs_ref[rows, :], zob)
            os_ = _dot(cs_ref[srows, :], zob)
            c = tw_ref[0, rows, :]
            s = tw_ref[1, rows, :]
            pc = c * oc - s * os_
            ps = c * os_ + s * oc
            xca, xsa = ec + pc, es + ps
            xcb, xsb = ec - pc, ps - es
            har = h_ref[n, rows, :]
            hai = h_ref[n, srows, :]
            hbr = h_ref[n, 2 * m + r:2 * m + r + kb, :]
            hbi = h_ref[n, 3 * m + r:3 * m + r + kb, :]
            yar = xca * har + xsa * hai
            yai = xca * hai - xsa * har
            ybr = xcb * hbr + xsb * hbi
            ybi = xcb * hbi - xsb * hbr
            dr = yar - ybr
            di = yai + ybi
            g_scr[0, rows, :] = (yar + ybr).astype(BF16)
            g_scr[0, srows, :] = (ybi - yai).astype(BF16)
            g_scr[1, rows, :] = (c * dr - s * di).astype(BF16)
            g_scr[1, srows, :] = (-(s * dr + c * di)).astype(BF16)
        hr = hm_ref[n, 0:1, :]
        hi = hm_ref[n, 1:2, :]
        ymr = e_mid * hr + o_mid * hi
        ymi = e_mid * hi - o_mid * hr
        y_even = _dot(cs_ref[:m, :], g_scr[0, :m, :]) + _dot(cs_ref[m:, :], g_scr[0, m:, :]) + alt * ymr
        y_odd = _dot(cs_ref[:m, :], g_scr[1, :m, :]) + _dot(cs_ref[m:, :], g_scr[1, m:, :]) - alt * ymi
        for j in range(n_lane_blocks):
            c_scr[j, pl.ds(0, m, stride=2), :] = y_even[:, j * LANES:(j + 1) * LANES]
            c_scr[j, pl.ds(1, m, stride=2), :] = y_odd[:, j * LANES:(j + 1) * LANES]
        conv = jnp.concatenate([c_scr[j] for j in range(n_lane_blocks)], axis=1)
        z = short_conv(*gates[n]) * (conv + bias_ref[n] * z)
    z_ref[0] = z.astype(z_ref.dtype)


def _hy_conv(proj, conv_w, conv_b, cs, hspec, hl, hm, tw, bias, batch, seq, seg, col0, width, tc, split, bp,
             cast_srcs):
    proj3 = proj.reshape(batch, seq, proj.shape[-1])
    blk0 = col0 // tc
    per = width // tc
    n_b = batch // bp
    assert bp == 1 or not split
    once = dict(pipeline_mode=pl.Buffered(1))

    def u_spec(p):
        return pl.BlockSpec((bp, seq, tc), lambda c, b: (b, 0, blk0 + p * per + c))

    def w_spec(p, rows):
        return pl.BlockSpec((rows, tc), lambda c, b: (0, p * per + c))

    in_specs = [u_spec(0), u_spec(1), u_spec(2),
                w_spec(0, HY_SHORT), w_spec(1, HY_SHORT), w_spec(2, HY_SHORT),
                w_spec(0, 1), w_spec(1, 1), w_spec(2, 1),
                pl.BlockSpec(cs.shape, lambda c, b: (0, 0), **once),
                pl.BlockSpec((HY_ORDER, 2 * seq, tc), lambda c, b: (0, 0, c), **once)]
    bias_spec = pl.BlockSpec((HY_ORDER, 1, tc), lambda c, b: (0, 0, c))
    operands = [proj3, proj3, proj3, conv_w, conv_w, conv_w, conv_b, conv_b, conv_b, cs, hspec]
    if split:
        kern = functools.partial(_hyconv_split_kernel, seq=seq, seg=seg)
        in_specs += [pl.BlockSpec((HY_ORDER, 2, tc), lambda c, b: (0, 0, c)),
                     pl.BlockSpec((2, seq // 2, tc), lambda c, b: (0, 0, 0)), bias_spec]
        operands += [hm, tw, bias]
        scratch = [pltpu.VMEM((2, seq, tc), BF16), pltpu.VMEM((tc // LANES, seq, LANES), F32),
                   pltpu.VMEM((tc // LANES, seq, LANES), F32)]
    else:
        kern = functools.partial(_hyconv_kernel, seq=seq, seg=seg)
        in_specs += [pl.BlockSpec((HY_ORDER, 1, tc), lambda c, b: (0, 0, c)), bias_spec]
        operands += [hl, bias]
        scratch = [pltpu.VMEM((bp, 2 * seq, tc), BF16)]
    cast_specs, cast_shapes = _cast_rider_specs(cast_srcs, per * n_b, lambda c, b: c * n_b + b)
    kern = _with_cast_riders(kern, len(operands), 1, len(cast_srcs))
    return pl.pallas_call(
        kern,
        grid=(per, n_b),
        in_specs=in_specs + cast_specs,
        out_specs=[pl.BlockSpec((bp, seq, tc), lambda c, b: (b, 0, c))] + cast_specs,
        out_shape=[jax.ShapeDtypeStruct((batch, seq, width), BF16)] + cast_shapes,
        scratch_shapes=scratch,
        compiler_params=_cparams(("arbitrary", "arbitrary")),
        name="hy_conv",
    )(*operands, *cast_srcs)


def _mixout_kernel(*refs, gla_width, n_own, aliased):
    (yg_ref, zh_ref, x_ref, mod_ref, ghy_ref, gpost_ref, gpre_ref, wo_ref, wrh_ref, wrl_ref, br_ref,
     cnt_in_ref) = refs[:12]
    x1_ref, hg_ref, route_ref, cnt_ref, cnt_scr = refs[12 + int(aliased):]
    d = x_ref.shape[1]
    tm = x_ref.shape[0] // MIX_SUBTILES
    i = pl.program_id(0)

    @pl.when(i == 0)
    def _():
        cnt_scr[...] = cnt_in_ref[...]

    @pl.when(i < n_own)
    def _():
        for s in range(MIX_SUBTILES):
            _mixout_rows(slice(s * tm, (s + 1) * tm), tm, d, gla_width, yg_ref, zh_ref, x_ref, mod_ref, ghy_ref,
                         gpost_ref, gpre_ref, wo_ref, wrh_ref, wrl_ref, br_ref, x1_ref, hg_ref, route_ref, cnt_scr)

    @pl.when(i >= n_own)
    def _():
        hg_ref[...] = jnp.zeros_like(hg_ref)

    cnt_ref[...] = cnt_scr[...]


def _mixout_rows(rows, tm, d, gla_width, yg_ref, zh_ref, x_ref, mod_ref, ghy_ref, gpost_ref, gpre_ref, wo_ref,
                 wrh_ref, wrl_ref, br_ref, x1_ref, hg_ref, route_ref, cnt_scr):
    yh = _rms(zh_ref[rows, :].astype(F32), ghy_ref[...]).astype(BF16)
    y = _dot(yg_ref[rows, :], wo_ref[:gla_width, :]) + _dot(yh, wo_ref[gla_width:, :])
    x1 = x_ref[rows, :] + _rms(y, gpost_ref[...] * mod_ref[0, 2:3, :])
    x1_ref[rows, :] = x1
    h2 = _rms(x1, gpre_ref[...] * (1.0 + mod_ref[0, 4:5, :])) + mod_ref[0, 3:4, :]
    hh, hl = _split2(h2)
    hg_ref[rows, :d] = h2
    logits = _dot(hh, wrh_ref[...]) + (_dot(hh, wrl_ref[...]) + _dot(hl, wrh_ref[...])) + br_ref[...]

    lane = lax.broadcasted_iota(jnp.int32, logits.shape, 1).astype(F32)
    neg = -jnp.inf
    is_grp = (lane >= N_EXPERTS) & (lane < N_EXPERTS + N_GROUPS)
    m = jnp.max(jnp.where(is_grp, logits, neg), axis=1, keepdims=True)
    p_grp = 1.0 / jnp.sum(jnp.where(is_grp, jnp.exp(logits - m), 0.0), axis=1, keepdims=True)
    grp = jnp.min(jnp.where(is_grp & (logits == m), lane - N_EXPERTS, 1e9), axis=1, keepdims=True)
    sel = (lane >= grp * EXP_PER_GROUP) & (lane < (grp + 1.0) * EXP_PER_GROUP)
    me = jnp.max(jnp.where(sel, logits, neg), axis=1, keepdims=True)
    pe = jnp.where(sel, jnp.exp(logits - me), -1.0)
    v1 = jnp.max(pe, axis=1, keepdims=True)
    i1 = jnp.min(jnp.where(pe == v1, lane, 1e9), axis=1, keepdims=True)
    pe2 = jnp.where(lane == i1, -1.0, pe)
    v2 = jnp.max(pe2, axis=1, keepdims=True)
    i2 = jnp.min(jnp.where(pe2 == v2, lane, 1e9), axis=1, keepdims=True)
    den = v1 + v2
    gates = jnp.where(lane == i1, v1 / den, jnp.where(lane == i2, v2 / den, 0.0)) * p_grp

    lo = jnp.minimum(i1, i2) - grp * EXP_PER_GROUP
    hi = jnp.maximum(i1, i2) - grp * EXP_PER_GROUP
    bucket = grp * PAIRS_PER_GROUP + lo * (2 * EXP_PER_GROUP - 1 - lo) * 0.5 + (hi - lo - 1.0)
    onehot = lane == bucket
    r_i = lax.broadcasted_iota(jnp.int32, (tm, tm), 0)
    c_i = lax.broadcasted_iota(jnp.int32, (tm, tm), 1)
    earlier = _dot((r_i > c_i).astype(BF16), onehot.astype(BF16)) + cnt_scr[...]
    rank = jnp.sum(jnp.where(onehot, earlier, 0.0), axis=1, keepdims=True)
    cnt_scr[...] += jnp.sum(onehot.astype(F32), axis=0, keepdims=True)
    route = jnp.where(lane == ROUTE_BUCKET_LANE, bucket, jnp.where(lane == ROUTE_RANK_LANE, rank, gates))
    hg_ref[rows, d:] = route
    route_ref[:, rows] = route.T[ROUTE_BUCKET_LANE:ROUTE_BUCKET_LANE + SUBLANES, :]


def _mix_out(yg, zh, x, mod, mod_row, g_hy, g_post, g_pre, w_out, wr_hi, wr_lo, b_r, cnt_in, hg_all, row0, n_total, tm):
    n, d = x.shape
    gw = yg.shape[1]
    hw = zh.shape[1]
    aliased = hg_all is not None
    n_own = n // tm
    n_steps = n_own if aliased else n_total // tm
    assert row0 % tm == 0 and n_total % tm == 0 and (row0 + n == n_total if aliased else row0 == 0)
    kern = functools.partial(_mixout_kernel, gla_width=gw, n_own=n_own, aliased=aliased)
    row = lambda i: (jnp.minimum(i, n_own - 1), 0)
    fixed = lambda i: (0, 0)
    in_specs = [pl.BlockSpec((tm, gw), row),
                pl.BlockSpec((tm, hw), row),
                pl.BlockSpec((tm, d), row),
                pl.BlockSpec((1, 6, d), lambda i: (mod_row(jnp.minimum(i, n_own - 1) * tm), 0, 0)),
                pl.BlockSpec((1, hw), fixed),
                pl.BlockSpec((1, d), fixed),
                pl.BlockSpec((1, d), fixed),
                pl.BlockSpec((gw + hw, d), fixed),
                pl.BlockSpec((d, LANES), fixed),
                pl.BlockSpec((d, LANES), fixed),
                pl.BlockSpec((1, LANES), fixed),
                pl.BlockSpec((1, LANES), fixed)]
    operands = [yg, zh, x, mod, g_hy, g_post, g_pre, w_out, wr_hi, wr_lo, b_r, cnt_in]
    if aliased:
        in_specs.append(pl.BlockSpec(memory_space=pl.ANY))
        operands.append(hg_all)
    return pl.pallas_call(
        kern,
        grid=(n_steps,),
        in_specs=in_specs,
        out_specs=[pl.BlockSpec((tm, d), row),
                   pl.BlockSpec((tm, d + LANES), lambda i: (row0 // tm + i, 0)),
                   pl.BlockSpec((SUBLANES, tm), lambda i: (0, jnp.minimum(i, n_own - 1))),
                   pl.BlockSpec((1, LANES), fixed)],
        out_shape=[jax.ShapeDtypeStruct((n, d), F32),
                   jax.ShapeDtypeStruct((n_total, d + LANES), F32),
                   jax.ShapeDtypeStruct((SUBLANES, n), F32),
                   jax.ShapeDtypeStruct((1, LANES), F32)],
        scratch_shapes=[pltpu.VMEM((1, LANES), F32)],
        input_output_aliases={len(operands) - 1: 1} if aliased else {},
        compiler_params=_cparams(("arbitrary",)),
        name="mix_out",
    )(*operands)


def _row_gather_copy(src_hbm, row, buf, slot, r, sem):
    return pltpu.make_async_copy(src_hbm.at[pl.ds(row, 1), :], buf.at[slot, pl.ds(r, 1), :], sem.at[slot])


def _start_row_gather(idx_ref, base, src_hbm, buf, slot, sem, tm):
    def body(r, carry):
        _row_gather_copy(src_hbm, idx_ref[base + r], buf, slot, r, sem).start()
        return carry

    lax.fori_loop(0, tm, body, 0, unroll=8)


def _start_row_gather_inline(idx_ref, base, src_hbm, buf, slot, sem, tm):
    for r in range(tm):
        _row_gather_copy(src_hbm, idx_ref[base + r], buf, slot, r, sem).start()


def _wait_row_gather(src_hbm, buf, slot, sem, tm):
    pltpu.make_async_copy(src_hbm.at[pl.ds(0, tm), :], buf.at[slot], sem.at[slot]).wait()


def _moe_kernel(src_ref, ea_ref, eb_ref, valid_ref, hg_hbm,
                wga_ref, wua_ref, wda_ref, wgb_ref, wub_ref, wdb_ref, y_ref, buf, sem, *, tm, d):
    t = pl.program_id(0)
    slot = t % 2

    @pl.when(t == 0)
    def _():
        _start_row_gather(src_ref, 0, hg_hbm, buf, 0, sem, tm)

    @pl.when(valid_ref[t] == 1)
    def _():
        _wait_row_gather(hg_hbm, buf, slot, sem, tm)
        _start_row_gather_inline(src_ref, (t + 1) * tm, hg_hbm, buf, 1 - slot, sem, tm)
        h = buf[slot, :, :d].astype(BF16)
        gates = buf[slot, :, d:]
        lane = lax.broadcasted_iota(jnp.int32, gates.shape, 1)

        experts = ((wga_ref, wua_ref, wda_ref, ea_ref[t]), (wgb_ref, wub_ref, wdb_ref, eb_ref[t]))
        pre = [_dot(h, wg_ref[0]) for wg_ref, _, _, _ in experts]
        up = [_dot(h, wu_ref[0]) for _, wu_ref, _, _ in experts]
        he = [(_silu(pre[i]) * up[i]).astype(BF16) for i in range(2)]
        gate = [jnp.sum(jnp.where(lane == e, gates, 0.0), axis=1, keepdims=True) for _, _, _, e in experts]
        out = [gate[i] * _dot(he[i], experts[i][2][0]) for i in range(2)]
        y_ref[...] = out[0] + out[1]

    @pl.when(valid_ref[t] == 0)
    def _():
        @pl.when(valid_ref[jnp.maximum(t - 1, 0)] == 1)
        def _():
            _wait_row_gather(hg_hbm, buf, slot, sem, tm)

        y_ref[...] = jnp.zeros_like(y_ref)


def _moe(hg, src, ea, eb, valid, wg, wu, wd, n_tiles, tm):
    d = hg.shape[1] - LANES
    _, _, de = wg.shape
    kern = functools.partial(_moe_kernel, tm=tm, d=d)
    wa = lambda t, src, ea, eb, valid: (ea[t], 0, 0)
    wb = lambda t, src, ea, eb, valid: (eb[t], 0, 0)
    return pl.pallas_call(
        kern,
        grid_spec=pltpu.PrefetchScalarGridSpec(
            num_scalar_prefetch=4,
            grid=(n_tiles,),
            in_specs=[pl.BlockSpec(memory_space=pl.ANY),
                      pl.BlockSpec((1, d, de), wa), pl.BlockSpec((1, d, de), wa), pl.BlockSpec((1, de, d), wa),
                      pl.BlockSpec((1, d, de), wb), pl.BlockSpec((1, d, de), wb), pl.BlockSpec((1, de, d), wb)],
            out_specs=pl.BlockSpec((tm, d), lambda t, src, ea, eb, valid: (t, 0)),
            scratch_shapes=[pltpu.VMEM((2, tm, d + LANES), F32), pltpu.SemaphoreType.DMA((2,))]),
        out_shape=jax.ShapeDtypeStruct((n_tiles * tm, d), F32),
        compiler_params=_cparams(("arbitrary",)),
        name="moe",
    )(src, ea, eb, valid, hg, wg, wu, wd, wg, wu, wd)


def _ffn_out_kernel(pos_ref, y_hbm, x1_ref, mod_ref, gpost_ref, o_ref, buf, sem, *, tm):
    i = pl.program_id(0)
    n_i = pl.num_programs(0)
    slot = i % 2

    @pl.when(i == 0)
    def _():
        _start_row_gather(pos_ref, 0, y_hbm, buf, 0, sem, tm)

    _wait_row_gather(y_hbm, buf, slot, sem, tm)

    @pl.when(i + 1 < n_i)
    def _():
        _start_row_gather_inline(pos_ref, (i + 1) * tm, y_hbm, buf, 1 - slot, sem, tm)
        o_ref[...] = x1_ref[...] + _rms(buf[slot], gpost_ref[...] * mod_ref[0, 5:6, :])

    @pl.when(i + 1 == n_i)
    def _():
        o_ref[...] = x1_ref[...] + _rms(buf[slot], gpost_ref[...] * mod_ref[0, 5:6, :])


def _ffn_out(y_sorted, pos, x1, mod, mod_row, g_post, tm):
    n, d = x1.shape
    kern = functools.partial(_ffn_out_kernel, tm=tm)
    return pl.pallas_call(
        kern,
        grid_spec=pltpu.PrefetchScalarGridSpec(
            num_scalar_prefetch=1,
            grid=(n // tm,),
            in_specs=[pl.BlockSpec(memory_space=pl.ANY),
                      pl.BlockSpec((tm, d), lambda i, pos: (i, 0)),
                      pl.BlockSpec((1, 6, d), lambda i, pos: (mod_row(i * tm), 0, 0)),
                      pl.BlockSpec((1, d), lambda i, pos: (0, 0))],
            out_specs=pl.BlockSpec((tm, d), lambda i, pos: (i, 0)),
            scratch_shapes=[pltpu.VMEM((2, tm, d), F32), pltpu.SemaphoreType.DMA((2,))]),
        out_shape=jax.ShapeDtypeStruct((n, d), F32),
        compiler_params=_cparams(("arbitrary",)),
        name="ffn_out",
    )(pos, y_sorted, x1, mod, g_post)


def _route_tables(route, counts, n_tiles, tm):
    n = route.shape[1]
    n_buckets = N_GROUPS * PAIRS_PER_GROUP
    cnt = counts[0, :n_buckets].astype(jnp.int32)
    padded = (cnt + tm - 1) // tm * tm
    ends = jnp.cumsum(padded)
    starts = ends - padded
    bucket = route[0].astype(jnp.int32)
    rank = route[ROUTE_RANK_LANE - ROUTE_BUCKET_LANE].astype(jnp.int32)
    pos = starts[bucket] + rank
    src = (jnp.arange(n_tiles * tm, dtype=jnp.int32) % n).at[pos].set(jnp.arange(n, dtype=jnp.int32))
    n_valid = ends[-1] // tm
    tile = jnp.arange(n_tiles, dtype=jnp.int32)
    used = jnp.minimum(tile, n_valid - 1)
    tile_bucket = jnp.sum((ends[None, :] <= (used * tm)[:, None]).astype(jnp.int32), axis=1)
    grp = tile_bucket // PAIRS_PER_GROUP
    pair = tile_bucket % PAIRS_PER_GROUP
    pair_lo = jnp.array([a for a in range(EXP_PER_GROUP) for b in range(a + 1, EXP_PER_GROUP)], jnp.int32)
    pair_hi = jnp.array([b for a in range(EXP_PER_GROUP) for b in range(a + 1, EXP_PER_GROUP)], jnp.int32)
    ea = grp * EXP_PER_GROUP + pair_lo[pair]
    eb = grp * EXP_PER_GROUP + pair_hi[pair]
    valid = (tile < n_valid).astype(jnp.int32)
    return pos, src, ea, eb, valid


def _dft_kernel(cs_ref, *, seq, tr):
    k = pl.program_id(0) * tr + lax.broadcasted_iota(jnp.int32, (tr, LANES), 0)
    j = lax.broadcasted_iota(jnp.int32, (tr, LANES), 1)
    period = 2 * seq

    def table(step):
        ang = ((k * j * step) % period).astype(F32) * (math.pi / seq)
        return jnp.cos(ang), jnp.sin(ang)

    c0, s0 = table(1)
    c1, s1 = table(LANES)
    for t1 in range(seq // LANES):
        cols = slice(t1 * LANES, (t1 + 1) * LANES)
        ca = c1[:, t1:t1 + 1]
        sa = s1[:, t1:t1 + 1]
        cs_ref[0, :, cols] = (ca * c0 - sa * s0).astype(cs_ref.dtype)
        cs_ref[1, :, cols] = (sa * c0 + ca * s0).astype(cs_ref.dtype)


def _dft_cos_sin(seq, n_rows, tr):
    kern = functools.partial(_dft_kernel, seq=seq, tr=tr)
    return pl.pallas_call(
        kern,
        grid=(n_rows // tr,),
        out_specs=pl.BlockSpec((2, tr, seq), lambda r: (0, r, 0)),
        out_shape=jax.ShapeDtypeStruct((2, n_rows, seq), BF16),
        compiler_params=_cparams(("arbitrary",)),
        name="dft_matrix",
    )()


def _positional_features(seq):
    t = jnp.arange(seq, dtype=F32)
    t01 = t / max(seq - 1, 1)
    ang = 2.0 * math.pi * t / seq
    bands = jnp.linspace(1e-4, HY_BANDS - 1, HY_BANDS, dtype=F32)
    pe = jnp.concatenate([t01[:, None], jnp.cos(ang[:, None] * bands), -jnp.sin(ang[:, None] * bands)], axis=-1)
    return jnp.pad(pe, ((0, 0), (0, LANES - pe.shape[1])))


def _mixer_and_router(x3, mod, mod_row, s0f, s0b, n_seg, p, tiles, cnt_in, hg_all, row0, n_total, w_out, casts):
    batch, seq, d = x3.shape
    x = x3.reshape(batch * seq, d)
    dk, dv = p["dk"], p["dv"]
    hy_width = p["hy_width"]
    casted = {}

    proj, lr, *casted["premix"] = _premix_proj(x, mod, mod_row, p["g_pre_mix"], p["w_main"], p["w_lr"],
                                               tiles["tm_proj"], tiles["tn_proj"], casts["premix"])
    if w_out is None:
        w_out = casted["premix"][0]
    y_gla, s_f, s_b, *casted["gla"] = _gla(proj, lr, p["wdf"], p["bdf"], p["wdb"], p["bdb"], p["g_gla"], s0f, s0b,
                                           batch, seq, dk, dv, tiles["gla_heads_per_step"], casts["gla"])

    ad, hl, hm = _hy_filter(_positional_features(seq), p["hy_w1"], p["hy_b1"], p["hy_w2"], p["hy_b2"], p["hy_freq"],
                            p["hy_w3"], p["hy_deltas"], seq, hy_width, tiles["tc_filter"])
    tr, tc_conv = tiles["tr_spec"], tiles["tc_conv"]
    if tiles["hy_split"]:
        half = seq // 2
        hspec = _hy_spectrum(_dft_cos_sin(seq, half, tr), ad, seq, half, 4, hy_width, tr, tiles["tc_spec"])
        cs = _dft_cos_sin(half, half, tr).reshape(seq, half)
        ang = jnp.arange(half, dtype=F32) * (math.pi / seq)
        tw = jnp.broadcast_to(jnp.stack([jnp.cos(ang), jnp.sin(ang)])[:, :, None], (2, half, tc_conv))
    else:
        cs3 = _dft_cos_sin(seq, seq, tr)
        hspec = _hy_spectrum(cs3, ad, seq, seq, 2, hy_width, tr, tiles["tc_spec"])
        cs = cs3.reshape(2 * seq, seq)
        tw = None
    z_hy, *casted["hy_conv"] = _hy_conv(proj, p["hy_conv_w"], p["hy_conv_b"], cs, hspec, hl, hm, tw, p["hy_bias"],
                                        batch, seq, seq // n_seg, p["hy_col0"], hy_width, tc_conv, tiles["hy_split"],
                                        tiles["hy_batch_per_step"], casts["hy_conv"])

    x1, hg_all, route, counts = _mix_out(y_gla.reshape(batch * seq, -1), z_hy.reshape(batch * seq, -1), x, mod,
                                         mod_row, p["g_hy"], p["g_post_mix"], p["g_pre_ffn"], w_out, p["wr_hi"],
                                         p["wr_lo"], p["b_r"], cnt_in, hg_all, row0, n_total, tiles["tm_mix"])
    return x1, hg_all, route, counts, s_f, s_b, casted


def kernel(x_prompt, x_sample, c, state_gla_fwd, state_gla_bwd, c_ctx, w_ada, b_ada, g_pre_mix, g_post_mix, g_pre_ffn, g_post_ffn, w_in, w_dec_f, b_dec_f, w_dec_b, b_dec_b, g_gla, hy_conv_w, hy_conv_b, hy_w1, hy_b1, hy_w2, hy_b2, hy_w3, hy_freq, hy_bias, g_hy, w_out, w_router_grp, b_router_grp, w_router_exp, b_router_exp, w_exp_gate, w_exp_up, w_exp_down):
    depth = w_ada.shape[0]
    assert depth == 1
    l = 0
    d = x_prompt.shape[-1]
    dec_batch = x_sample.shape[0]
    heads = GLA_HEADS
    dk, dv = state_gla_fwd.shape[-2:]
    kdim = heads * dk
    gla_width = heads * dv
    hy_width = g_hy.shape[-1]
    hid = hy_w2.shape[-1]

    cond = jnp.concatenate([c_ctx[None, :], c], axis=0)
    cond = jnp.pad(cond, ((0, -cond.shape[0] % SUBLANES), (0, 0)))
    mod = _ada_mod(cond, w_ada[l], b_ada[l]).reshape(cond.shape[0], 6, d)

    n_main = 2 * kdim + 2 * gla_width
    w_main, w_lr = _repack_w_in(w_in, l, n_main, 2 * GLA_RANK, REPACK_COLS)

    def dec_weight(w, first_row):
        wh = w.reshape(GLA_RANK, heads, dk).transpose(1, 0, 2)
        return jnp.pad(wh, ((0, 0), (first_row, LANES - GLA_RANK - first_row), (0, 0)))

    deltas = jnp.abs(jnp.linspace(math.log(HY_TARGET) / HY_SLOW_PCT, math.log(HY_TARGET) / HY_FAST_PCT, hy_width,
                                  dtype=F32))
    w_r = jnp.pad(jnp.concatenate([w_router_exp[l], w_router_grp[l]], axis=1),
                  ((0, 0), (0, LANES - N_EXPERTS - N_GROUPS)))
    wr_hi, wr_lo = _split2(w_r)
    b_r = jnp.pad(jnp.concatenate([b_router_exp[l], b_router_grp[l]]), (0, LANES - N_EXPERTS - N_GROUPS))

    p = dict(
        dk=dk, dv=dv, hy_width=hy_width, hy_col0=n_main,
        g_pre_mix=g_pre_mix[l][None, :], g_post_mix=g_post_mix[l][None, :],
        g_pre_ffn=g_pre_ffn[l][None, :], g_post_ffn=g_post_ffn[l][None, :],
        w_main=w_main, w_lr=w_lr,
        wdf=dec_weight(w_dec_f[l], 0), bdf=b_dec_f[l][None, :],
        wdb=dec_weight(w_dec_b[l], GLA_RANK), bdb=b_dec_b[l][None, :],
        g_gla=g_gla[l][None, :],
        hy_conv_w=hy_conv_w[l], hy_conv_b=hy_conv_b[l][None, :],
        hy_w1=jnp.pad(hy_w1[l], ((0, LANES - hy_w1.shape[1]), (0, 0))), hy_b1=hy_b1[l][None, :],
        hy_w2=hy_w2[l], hy_b2=hy_b2[l][None, :], hy_freq=hy_freq[l][None, :],
        hy_w3=hy_w3[l].reshape(hid, 2 * HY_ORDER, hy_width).transpose(1, 0, 2), hy_deltas=deltas[None, :],
        hy_bias=hy_bias[l][:, None, :], g_hy=g_hy[l][None, :],
        wr_hi=wr_hi, wr_lo=wr_lo, b_r=b_r[None, :],
    )

    zero_state = jnp.zeros((1, heads, dk, dv), F32)
    dec_seq = x_sample.shape[1]
    tiles_p = _tile_plan(x_prompt.shape[1])
    tiles_s = _tile_plan(dec_seq)
    tm_moe = MXU_DIM
    tm_out = 512

    n_p = x_prompt.shape[0] * x_prompt.shape[1]
    n_s = dec_batch * dec_seq
    n_all = n_p + n_s
    mod_row_p = lambda r: 0
    mod_row_s = lambda r: 1 + r // dec_seq
    n_exp, _, d_exp = w_exp_gate.shape[1:]
    casts_p = dict(premix=[w_out[l]],
                   gla=[w_exp_gate[l].reshape(n_exp * d, d_exp), w_exp_down[l].reshape(n_exp * d_exp, d)],
                   hy_conv=[w_exp_up[l].reshape(n_exp * d, d_exp)])
    casts_s = dict(premix=[], gla=[], hy_conv=[])
    x1_p, hg_all, route_p, counts, s_f, s_b, casted = _mixer_and_router(
        x_prompt, mod, mod_row_p, zero_state, zero_state, 1, p, tiles_p, jnp.zeros((1, LANES), F32), None, 0, n_all,
        None, casts_p)
    (w_out_b,), (w_gate_b, w_down_b), (w_up_b,) = casted["premix"], casted["gla"], casted["hy_conv"]
    x1_s, hg_all, route_s, counts, _, _, _ = _mixer_and_router(
        x_sample, mod, mod_row_s, state_gla_fwd[:, l], state_gla_bwd[:, l], dec_seq // GRID_W, p, tiles_s,
        counts, hg_all, n_p, n_all, w_out_b, casts_s)

    n_tiles = n_all // tm_moe + N_GROUPS * PAIRS_PER_GROUP
    pos, src, ea, eb, valid = _route_tables(jnp.concatenate([route_p, route_s], axis=1), counts, n_tiles, tm_moe)
    y_sorted = _moe(hg_all, src, ea, eb, valid, w_gate_b.reshape(n_exp, d, d_exp), w_up_b.reshape(n_exp, d, d_exp),
                    w_down_b.reshape(n_exp, d_exp, d), n_tiles, tm_moe)
    y_p = _ffn_out(y_sorted, pos[:n_p], x1_p, mod, mod_row_p, p["g_post_ffn"], tm_out)
    y_s = _ffn_out(y_sorted, pos[n_p:], x1_s, mod, mod_row_s, p["g_post_ffn"], tm_out)
    return (y_p.reshape(x_prompt.shape), y_s.reshape(x_sample.shape),
            s_f[:, None].astype(x_prompt.dtype), s_b[:, None].astype(x_prompt.dtype))
```

```python
import functools
import math

import jax
import jax.numpy as jnp
from jax import lax
from jax.experimental import pallas as pl
from jax.experimental.pallas import tpu as pltpu

F32 = jnp.float32
BF16 = jnp.bfloat16

GRID_W = 64
GLA_HEADS = 4
GLA_RANK = 16
GLA_TAU = 16.0
GLA_CHUNK = 64
GLA_UNROLL = 4
MIX_SUBTILES = 1
HY_ORDER = 2
HY_SHORT = 3
HY_BANDS = 16
HY_TARGET = 1e-2
HY_FAST_PCT = 0.3
HY_SLOW_PCT = 1.5
N_GROUPS = 4
EXP_PER_GROUP = 4
N_EXPERTS = N_GROUPS * EXP_PER_GROUP
PAIRS_PER_GROUP = EXP_PER_GROUP * (EXP_PER_GROUP - 1) // 2
ROUTE_BUCKET_LANE = N_EXPERTS
ROUTE_RANK_LANE = N_EXPERTS + 1
EPS = 1e-6

LANES = 128
SUBLANES = 8
MXU_DIM = 256
VMEM_LIMIT = 56 << 20
REPACK_COLS = MXU_DIM
FREQ_BLOCK = 2 * MXU_DIM


def _tile_plan(seq):
    long_seq = seq >= 1024
    return dict(
        tm_proj=1024, tn_proj=2048,
        tc_filter=128,
        tr_spec=512 if long_seq else 256,
        tc_spec=1024 if long_seq else 512,
        hy_split=long_seq,
        tc_conv=256 if long_seq else 512,
        hy_batch_per_step=1 if long_seq else 2,
        gla_heads_per_step=2 if long_seq else 4,
        tm_mix=512,
    )


def _cparams(sem):
    return pltpu.CompilerParams(dimension_semantics=sem, vmem_limit_bytes=VMEM_LIMIT)


def _dot(a, b):
    return jnp.dot(a, b, preferred_element_type=F32)


def _dot_nt(a, b):
    return lax.dot_general(a, b, (((1,), (1,)), ((), ())), preferred_element_type=F32)


def _dot_tn(a, b):
    return lax.dot_general(a, b, (((0,), (0,)), ((), ())), preferred_element_type=F32)


def _split2(x):
    hi = x.astype(BF16)
    lo = (x - hi.astype(F32)).astype(BF16)
    return hi, lo


def _dot_hp(a, b):
    ah, al = a if isinstance(a, tuple) else _split2(a)
    bh, bl = _split2(b)
    return _dot(ah, bh) + (_dot(ah, bl) + _dot(al, bh))


def _dot_exact_lhs(t, x):
    hi, lo = _split2(x)
    return _dot(t, hi) + _dot(t, lo)


def _rms(x, g):
    return x * lax.rsqrt(jnp.mean(x * x, axis=-1, keepdims=True) + EPS) * g


def _silu(x):
    return x / (1.0 + jnp.exp(-x))


def _ada_kernel(c_ref, w_ref, b_ref, o_ref):
    s = _silu(c_ref[...]).astype(BF16)
    o_ref[...] = _dot(s, w_ref[...].astype(BF16)) + b_ref[...]


def _ada_mod(cond, w_ada, b_ada):
    rows, d = cond.shape
    n = w_ada.shape[1]
    tn = 1024
    return pl.pallas_call(
        _ada_kernel,
        grid=(n // tn,),
        in_specs=[pl.BlockSpec((rows, d), lambda j: (0, 0)),
                  pl.BlockSpec((d, tn), lambda j: (0, j)),
                  pl.BlockSpec((1, tn), lambda j: (0, j))],
        out_specs=pl.BlockSpec((rows, tn), lambda j: (0, j)),
        out_shape=jax.ShapeDtypeStruct((rows, n), F32),
        compiler_params=_cparams(("arbitrary",)),
        name="ada_mod",
    )(cond, w_ada, b_ada.reshape(1, n))


def _repack_kernel(wt_hbm, main_ref, lr_ref, buf, lr_buf, sem, lr_sem, *, layer, n_main, n_lr, tb):
    i = pl.program_id(0)
    n_i = pl.num_programs(0)
    slot = i % 2
    d = main_ref.shape[0]

    def block_copy(j, s):
        first = j * tb + jnp.where(j * tb >= n_main, n_lr, 0)
        return pltpu.make_async_copy(wt_hbm.at[layer, pl.ds(first, tb), :], buf.at[s], sem.at[s])

    lr_copy = pltpu.make_async_copy(wt_hbm.at[layer, pl.ds(n_main, n_lr), :], lr_buf, lr_sem)

    @pl.when(i == 0)
    def _():
        block_copy(0, 0).start()
        lr_copy.start()

    @pl.when(i + 1 < n_i)
    def _():
        block_copy(i + 1, 1 - slot).start()

    block_copy(i, slot).wait()
    main_ref[...] = buf[slot].T.astype(BF16)

    @pl.when(i == 0)
    def _():
        lr_copy.wait()
        lr = jnp.concatenate([lr_buf[...], jnp.zeros((LANES - n_lr, d), F32)], axis=0).T
        lr_ref[...] = lr.astype(BF16)


def _repack_w_in(w_in, layer, n_main, n_lr, tb):
    w_t = jnp.swapaxes(w_in, 1, 2)
    _, n_cols, d = w_t.shape
    n_out = n_cols - n_lr
    assert n_main % tb == 0 and n_out % tb == 0
    kern = functools.partial(_repack_kernel, layer=layer, n_main=n_main, n_lr=n_lr, tb=tb)
    return pl.pallas_call(
        kern,
        grid=(n_out // tb,),
        in_specs=[pl.BlockSpec(memory_space=pl.ANY)],
        out_specs=[pl.BlockSpec((d, tb), lambda i: (0, i)),
                   pl.BlockSpec((d, LANES), lambda i: (0, 0))],
        out_shape=[jax.ShapeDtypeStruct((d, n_out), BF16),
                   jax.ShapeDtypeStruct((d, LANES), BF16)],
        scratch_shapes=[pltpu.VMEM((2, tb, d), F32), pltpu.VMEM((n_lr, d), F32),
                        pltpu.SemaphoreType.DMA((2,)), pltpu.SemaphoreType.DMA(())],
        compiler_params=_cparams(("arbitrary",)),
        name="repack_w_in",
    )(w_t)


def _with_cast_riders(kernel_fn, n_in, n_out, n_cast):
    def kernel(*refs):
        ins = refs[:n_in]
        cast_in = refs[n_in:n_in + n_cast]
        outs = refs[n_in + n_cast:n_in + n_cast + n_out]
        cast_out = refs[n_in + n_cast + n_out:n_in + 2 * n_cast + n_out]
        kernel_fn(*ins, *outs, *refs[n_in + 2 * n_cast + n_out:])
        for src, dst in zip(cast_in, cast_out):
            dst[...] = src[...].astype(dst.dtype)

    return kernel


def _cast_rider_specs(cast_srcs, n_steps, step_of):
    n_slabs = 1 << (n_steps.bit_length() - 1)
    specs = []
    for w in cast_srcs:
        rows, cols = w.shape
        assert rows % n_slabs == 0
        specs.append(pl.BlockSpec((rows // n_slabs, cols),
                                  lambda *idx: (jnp.minimum(step_of(*idx), n_slabs - 1), 0)))
    return specs, [jax.ShapeDtypeStruct(w.shape, BF16) for w in cast_srcs]


def _premix_kernel(x_ref, mod_ref, g_ref, w_ref, wlr_ref, o_ref, lr_ref, h_scr):
    @pl.when(pl.program_id(1) == 0)
    def _():
        h = _rms(x_ref[...], g_ref[...] * (1.0 + mod_ref[0, 1:2, :])) + mod_ref[0, 0:1, :]
        h_scr[...] = h.astype(BF16)
        lr_ref[...] = _dot(h_scr[...], wlr_ref[...])

    o_ref[...] = _dot(h_scr[...], w_ref[...]).astype(o_ref.dtype)


def _premix_proj(x, mod, mod_row, g, w_main, w_lr, tm, tn, cast_srcs):
    n, d = x.shape
    nc = w_main.shape[1]
    n_j = nc // tn
    cast_specs, cast_shapes = _cast_rider_specs(cast_srcs, (n // tm) * n_j, lambda i, j: i * n_j + j)
    kern = _with_cast_riders(_premix_kernel, 5, 2, len(cast_srcs))
    return pl.pallas_call(
        kern,
        grid=(n // tm, n_j),
        in_specs=[pl.BlockSpec((tm, d), lambda i, j: (i, 0)),
                  pl.BlockSpec((1, 6, d), lambda i, j: (mod_row(i * tm), 0, 0)),
                  pl.BlockSpec((1, d), lambda i, j: (0, 0)),
                  pl.BlockSpec((d, tn), lambda i, j: (0, j)),
                  pl.BlockSpec((d, LANES), lambda i, j: (0, 0))] + cast_specs,
        out_specs=[pl.BlockSpec((tm, tn), lambda i, j: (i, j)),
                   pl.BlockSpec((tm, LANES), lambda i, j: (i, 0))] + cast_specs,
        out_shape=[jax.ShapeDtypeStruct((n, nc), BF16), jax.ShapeDtypeStruct((n, LANES), F32)] + cast_shapes,
        scratch_shapes=[pltpu.VMEM((tm, d), BF16)],
        compiler_params=_cparams(("arbitrary", "arbitrary")),
        name="premix_proj",
    )(x, mod, g, w_main, w_lr, *cast_srcs)


def _log_sigmoid(x):
    return jnp.minimum(x, 0.0) - jnp.log(1.0 + jnp.exp(-jnp.abs(x)))


def _gla_kernel(q_ref, k_ref, v_ref, g_ref, lr_ref, wdf_ref, bdf_ref, wdb_ref, bdb_ref, gg_ref,
                s0f_ref, s0b_ref, y_ref, sf_ref, sb_ref,
                laf_scr, lab_scr, qf_scr, qb_scr, o_scr, uf_scr, ub_scr, df_scr, db_scr, *, seq, dk, dv, hp):
    c = GLA_CHUNK
    n_chunks = seq // c
    scale = dk ** -0.5
    heads = range(hp)
    kcols = [slice(h * dk, (h + 1) * dk) for h in heads]
    vcols = [slice(h * dv, (h + 1) * dv) for h in heads]

    lr_split = _split2(lr_ref[0])
    for h in heads:
        laf_scr[h] = _log_sigmoid(_dot_hp(lr_split, wdf_ref[h]) + bdf_ref[:, kcols[h]]) / GLA_TAU
        lab_scr[h] = _log_sigmoid(_dot_hp(lr_split, wdb_ref[h]) + bdb_ref[:, kcols[h]]) / GLA_TAU

    per = GLA_UNROLL
    blk = per * c
    row = lax.broadcasted_iota(jnp.int32, (blk, blk), 0)
    col = lax.broadcasted_iota(jnp.int32, (blk, blk), 1)
    same = (row // c) == (col // c)
    lower = same & (row >= col)
    upper = same & (col >= row)
    t_fwd = lower.astype(BF16)
    t_bwd = upper.astype(BF16)

    def chunk_rows(x, r):
        return jnp.concatenate([jnp.broadcast_to(x[j * c + r:j * c + r + 1], (c, dk)) for j in range(per)], axis=0)

    def block_local(m, carry):
        sl = pl.ds(pl.multiple_of(m * blk, blk), blk)
        bf = [_dot_exact_lhs(t_fwd, laf_scr[h, sl, :]) for h in heads]
        bb = [_dot_exact_lhs(t_bwd, lab_scr[h, sl, :]) for h in heads]
        tot_f = [chunk_rows(x, c - 1) for x in bf]
        tot_b = [chunk_rows(x, 0) for x in bb]
        q = [q_ref[0, sl, kcols[h]].astype(F32) * scale for h in heads]
        k = [k_ref[0, sl, kcols[h]].astype(F32) for h in heads]
        v = [v_ref[0, sl, vcols[h]] for h in heads]
        qf = [(q[h] * jnp.exp(bf[h])).astype(BF16) for h in heads]
        kf = [(k[h] * jnp.exp(-bf[h])).astype(BF16) for h in heads]
        qb = [(q[h] * jnp.exp(bb[h])).astype(BF16) for h in heads]
        kb = [(k[h] * jnp.exp(-bb[h])).astype(BF16) for h in heads]
        ksf = [(k[h] * jnp.exp(tot_f[h] - bf[h])).astype(BF16) for h in heads]
        ksb = [(k[h] * jnp.exp(tot_b[h] - bb[h])).astype(BF16) for h in heads]
        sc_f = [_dot_nt(qf[h], kf[h]) for h in heads]
        sc_b = [_dot_nt(qb[h], kb[h]) for h in heads]
        att = [(jnp.where(lower, sc_f[h], 0.0) + jnp.where(upper, sc_b[h], 0.0)).astype(BF16) for h in heads]
        o_loc = [_dot(att[h], v[h]) for h in heads]
        dec_f = [jnp.exp(x) for x in tot_f]
        dec_b = [jnp.exp(x) for x in tot_b]
        for h in heads:
            o_scr[h, sl, :] = o_loc[h]
            qf_scr[h, sl, :] = qf[h]
            qb_scr[h, sl, :] = qb[h]
        for j in range(per):
            n = m * per + j
            rows = slice(j * c, (j + 1) * c)
            for h in heads:
                uf_scr[h, n] = _dot_tn(v[h][rows], ksf[h][rows])
                ub_scr[h, n] = _dot_tn(v[h][rows], ksb[h][rows])
                df_scr[h, n] = dec_f[h][j * c:j * c + SUBLANES]
                db_scr[h, n] = dec_b[h][j * c:j * c + SUBLANES]
        return carry

    lax.fori_loop(0, n_chunks // per, block_local, 0)

    def scan_fwd(n, s):
        upd = [uf_scr[h, n] for h in heads]
        for h in heads:
            uf_scr[h, n] = s[h]
        return tuple(s[h] * df_scr[h, n][0:1, :] + upd[h] for h in heads)

    def scan_bwd(i, s):
        n = n_chunks - 1 - i
        upd = [ub_scr[h, n] for h in heads]
        for h in heads:
            ub_scr[h, n] = s[h]
        return tuple(s[h] * db_scr[h, n][0:1, :] + upd[h] for h in heads)

    s_f = lax.fori_loop(0, n_chunks, scan_fwd, tuple(s0f_ref[0, h].T for h in heads), unroll=GLA_UNROLL)
    s_b = lax.fori_loop(0, n_chunks, scan_bwd, tuple(s0b_ref[0, h].T for h in heads), unroll=GLA_UNROLL)
    for h in heads:
        sf_ref[0, h] = s_f[h].T
        sb_ref[0, h] = s_b[h].T

    def chunk_inter(n, carry):
        sl = pl.ds(pl.multiple_of(n * c, c), c)
        inter = [_dot_nt(qf_scr[h, sl, :], uf_scr[h, n].astype(BF16))
                 + _dot_nt(qb_scr[h, sl, :], ub_scr[h, n].astype(BF16)) for h in heads]
        for h in heads:
            o_scr[h, sl, :] += inter[h]
        return carry

    lax.fori_loop(0, n_chunks, chunk_inter, 0, unroll=GLA_UNROLL)

    for h in heads:
        o = _rms(o_scr[h], gg_ref[...])
        y_ref[0, :, vcols[h]] = (o * _silu(g_ref[0, :, vcols[h]].astype(F32))).astype(y_ref.dtype)


def _gla(proj, lr, wdf, bdf, wdb, bdb, g_gla, s0f, s0b, batch, seq, dk, dv, hp, cast_srcs):
    heads = GLA_HEADS
    proj3 = proj.reshape(batch, seq, proj.shape[-1])
    lr3 = lr.reshape(batch, seq, LANES)
    kdim = heads * dk
    width = heads * dv
    bk, bv = hp * dk, hp * dv
    k_blk = kdim // bk
    v_blk = 2 * kdim // bv
    g_blk = (2 * kdim + width) // bv
    n_chunks = seq // GLA_CHUNK

    def s0_map(s0):
        if s0.shape[0] == batch:
            return lambda b, h: (b, h, 0, 0)
        return lambda b, h: (0, h, 0, 0)

    n_hsteps = heads // hp
    cast_specs, cast_shapes = _cast_rider_specs(cast_srcs, batch * n_hsteps, lambda b, h: b * n_hsteps + h)
    kern = _with_cast_riders(functools.partial(_gla_kernel, seq=seq, dk=dk, dv=dv, hp=hp), 12, 3, len(cast_srcs))
    return pl.pallas_call(
        kern,
        grid=(batch, n_hsteps),
        in_specs=[pl.BlockSpec((1, seq, bk), lambda b, h: (b, 0, h)),
                  pl.BlockSpec((1, seq, bk), lambda b, h: (b, 0, k_blk + h)),
                  pl.BlockSpec((1, seq, bv), lambda b, h: (b, 0, v_blk + h)),
                  pl.BlockSpec((1, seq, bv), lambda b, h: (b, 0, g_blk + h)),
                  pl.BlockSpec((1, seq, LANES), lambda b, h: (b, 0, 0)),
                  pl.BlockSpec((hp, LANES, dk), lambda b, h: (h, 0, 0)),
                  pl.BlockSpec((1, bk), lambda b, h: (0, h)),
                  pl.BlockSpec((hp, LANES, dk), lambda b, h: (h, 0, 0)),
                  pl.BlockSpec((1, bk), lambda b, h: (0, h)),
                  pl.BlockSpec((1, dv), lambda b, h: (0, 0)),
                  pl.BlockSpec((1, hp, dk, dv), s0_map(s0f)),
                  pl.BlockSpec((1, hp, dk, dv), s0_map(s0b))] + cast_specs,
        out_specs=[pl.BlockSpec((1, seq, bv), lambda b, h: (b, 0, h)),
                   pl.BlockSpec((1, hp, dk, dv), lambda b, h: (b, h, 0, 0)),
                   pl.BlockSpec((1, hp, dk, dv), lambda b, h: (b, h, 0, 0))] + cast_specs,
        out_shape=[jax.ShapeDtypeStruct((batch, seq, width), BF16),
                   jax.ShapeDtypeStruct((batch, heads, dk, dv), F32),
                   jax.ShapeDtypeStruct((batch, heads, dk, dv), F32)] + cast_shapes,
        scratch_shapes=[pltpu.VMEM((hp, seq, dk), F32), pltpu.VMEM((hp, seq, dk), F32),
                        pltpu.VMEM((hp, seq, dk), BF16), pltpu.VMEM((hp, seq, dk), BF16),
                        pltpu.VMEM((hp, seq, dv), F32),
                        pltpu.VMEM((hp, n_chunks, dv, dk), F32), pltpu.VMEM((hp, n_chunks, dv, dk), F32),
                        pltpu.VMEM((hp, n_chunks, SUBLANES, dk), F32), pltpu.VMEM((hp, n_chunks, SUBLANES, dk), F32)],
        compiler_params=_cparams(("arbitrary", "arbitrary")),
        name="gla",
    )(proj3, proj3, proj3, proj3, lr3, wdf, bdf, wdb, bdb, g_gla, s0f, s0b, *cast_srcs)


def _filter_kernel(pe_ref, w1_ref, b1_ref, w2_ref, b2_ref, fr_ref, w3_ref, dl_ref, ad_ref, hl_ref, hm_ref, mlp_scr,
                   *, seq):
    @pl.when(pl.program_id(0) == 0)
    def _():
        fr = fr_ref[...]
        h1 = jnp.sin(fr * (_dot_hp(pe_ref[...], w1_ref[...]) + b1_ref[...]))
        mlp_scr[...] = jnp.sin(fr * (_dot_hp(h1, w2_ref[...]) + b2_ref[...]))

    h2 = _split2(mlp_scr[...])
    dec = jnp.exp(-pe_ref[:, 0:1] * dl_ref[...])
    row = lax.broadcasted_iota(jnp.int32, (seq, 1), 0)
    alt = jnp.where(row % 2 == 0, 1.0, -1.0)
    phase = row % 4
    cos_half = jnp.where(phase == 0, 1.0, jnp.where(phase == 2, -1.0, 0.0))
    sin_half = jnp.where(phase == 1, 1.0, jnp.where(phase == 3, -1.0, 0.0))
    for o in range(HY_ORDER):
        ff = _dot_hp(h2, w3_ref[o]) * dec
        fb = _dot_hp(h2, w3_ref[HY_ORDER + o]) * dec
        nrm = jnp.sum(jnp.abs(ff), axis=0, keepdims=True) + jnp.sum(jnp.abs(fb), axis=0, keepdims=True)
        inv = 1.0 / nrm
        ff = ff * inv
        fb = jnp.where(row == 0, 0.0, fb * inv)
        a = ff + fb
        d = fb - ff
        ad_ref[o, 0] = a.astype(ad_ref.dtype)
        ad_ref[o, 1] = d.astype(ad_ref.dtype)
        ad_ref[o, 2] = (alt * a).astype(ad_ref.dtype)
        ad_ref[o, 3] = (-alt * d).astype(ad_ref.dtype)
        hl_ref[o] = jnp.sum(alt * a, axis=0, keepdims=True)
        hm_ref[o] = jnp.concatenate([jnp.sum(cos_half * a, axis=0, keepdims=True),
                                     jnp.sum(sin_half * d, axis=0, keepdims=True)], axis=0) * (1.0 / seq)


def _hy_filter(pe, w1p, b1, w2, b2, freq, w3r, deltas, seq, width, tc):
    hid = w2.shape[0]
    kern = functools.partial(_filter_kernel, seq=seq)
    return pl.pallas_call(
        kern,
        grid=(width // tc,),
        in_specs=[pl.BlockSpec((seq, LANES), lambda c: (0, 0)),
                  pl.BlockSpec((LANES, hid), lambda c: (0, 0)),
                  pl.BlockSpec((1, hid), lambda c: (0, 0)),
                  pl.BlockSpec((hid, hid), lambda c: (0, 0)),
                  pl.BlockSpec((1, hid), lambda c: (0, 0)),
                  pl.BlockSpec((1, hid), lambda c: (0, 0)),
                  pl.BlockSpec((2 * HY_ORDER, hid, tc), lambda c: (0, 0, c)),
                  pl.BlockSpec((1, tc), lambda c: (0, c))],
        out_specs=[pl.BlockSpec((HY_ORDER, 4, seq, tc), lambda c: (0, 0, 0, c)),
                   pl.BlockSpec((HY_ORDER, 1, tc), lambda c: (0, 0, c)),
                   pl.BlockSpec((HY_ORDER, 2, tc), lambda c: (0, 0, c))],
        out_shape=[jax.ShapeDtypeStruct((HY_ORDER, 4, seq, width), BF16),
                   jax.ShapeDtypeStruct((HY_ORDER, 1, width), F32),
                   jax.ShapeDtypeStruct((HY_ORDER, 2, width), F32)],
        scratch_shapes=[pltpu.VMEM((seq, hid), F32)],
        compiler_params=_cparams(("arbitrary",)),
        name="hy_filter",
    )(pe, w1p, b1, w2, b2, freq, w3r, deltas)


def _spectrum_kernel(cs_ref, ad_ref, h_ref, *, seq, per_part, tr):
    r = pl.program_id(1)
    k = (r % per_part) * tr + lax.broadcasted_iota(jnp.int32, (tr, 1), 0)
    wgt = jnp.where(k == 0, 1.0, 2.0) * (0.5 / seq)
    h_ref[0] = wgt * _dot(cs_ref[0], ad_ref[0, 0])


def _hy_spectrum(cs3, ad, seq, part_rows, n_parts, width, tr, tc):
    per_part = part_rows // tr
    kern = functools.partial(_spectrum_kernel, seq=seq, per_part=per_part, tr=tr)
    return pl.pallas_call(
        kern,
        grid=(HY_ORDER, n_parts * per_part, width // tc),
        in_specs=[pl.BlockSpec((1, tr, seq), lambda o, r, c: ((r // per_part) % 2, r % per_part, 0)),
                  pl.BlockSpec((1, 1, seq, tc), lambda o, r, c: (o, r // per_part, 0, c))],
        out_specs=pl.BlockSpec((1, tr, tc), lambda o, r, c: (o, r, c)),
        out_shape=jax.ShapeDtypeStruct((HY_ORDER, n_parts * part_rows, width), F32),
        compiler_params=_cparams(("arbitrary", "arbitrary", "arbitrary")),
        name="hy_spectrum",
    )(cs3, ad)


def _short_conv_fn(seq, seg):
    row = lax.broadcasted_iota(jnp.int32, (seq, 1), 0)
    pos = row % seg

    def short_conv(u_ref, cw_ref, cb_ref, b=0):
        u = u_ref[b].astype(F32)
        prev = jnp.where(pos == 0, 0.0, pltpu.roll(u, 1, 0))
        nxt = jnp.where(pos == seg - 1, 0.0, pltpu.roll(u, seq - 1, 0))
        return prev * cw_ref[0:1, :] + u * cw_ref[1:2, :] + nxt * cw_ref[2:3, :] + cb_ref[...]

    return short_conv


def _hyconv_kernel(u0_ref, u1_ref, u2_ref, cw0_ref, cw1_ref, cw2_ref, cb0_ref, cb1_ref, cb2_ref,
                   cs_ref, h_ref, hl_ref, bias_ref, z_ref, y_scr, *, seq, seg):
    short_conv = _short_conv_fn(seq, seg)
    row = lax.broadcasted_iota(jnp.int32, (seq, 1), 0)
    alt = jnp.where(row % 2 == 0, 1.0, -1.0)
    gates = ((u1_ref, cw1_ref, cb1_ref), (u2_ref, cw2_ref, cb2_ref))
    items = range(z_ref.shape[0])
    z = [short_conv(u0_ref, cw0_ref, cb0_ref, b) for b in items]
    kb = min(seq, FREQ_BLOCK)
    for n in range(HY_ORDER):
        zb = [z[b].astype(BF16) for b in items]
        for r in range(0, seq, kb):
            xc = [_dot(cs_ref[r:r + kb, :], zb[b]) for b in items]
            xs = [_dot(cs_ref[seq + r:seq + r + kb, :], zb[b]) for b in items]
            hre = h_ref[n, r:r + kb, :]
            him = h_ref[n, seq + r:seq + r + kb, :]
            for b in items:
                y_scr[b, r:r + kb, :] = (xc[b] * hre + xs[b] * him).astype(BF16)
                y_scr[b, seq + r:seq + r + kb, :] = (xs[b] * hre - xc[b] * him).astype(BF16)
        nyq = [jnp.sum(alt * z[b], axis=0, keepdims=True) * (hl_ref[n] * (0.5 / seq)) for b in items]
        conv = [_dot(cs_ref[:seq, :], y_scr[b, :seq, :]) + _dot(cs_ref[seq:, :], y_scr[b, seq:, :]) + alt * nyq[b]
                for b in items]
        gate = [short_conv(*gates[n], b) for b in items]
        z = [gate[b] * (conv[b] + bias_ref[n] * z[b]) for b in items]
    for b in items:
        z_ref[b] = z[b].astype(z_ref.dtype)


def _hyconv_split_kernel(u0_ref, u1_ref, u2_ref, cw0_ref, cw1_ref, cw2_ref, cb0_ref, cb1_ref, cb2_ref,
                         cs_ref, h_ref, hm_ref, tw_ref, bias_ref, z_ref, g_scr, z_scr, c_scr, *, seq, seg):
    m = seq // 2
    tc = z_ref.shape[2]
    n_lane_blocks = tc // LANES
    short_conv = _short_conv_fn(seq, seg)
    row = lax.broadcasted_iota(jnp.int32, (m, 1), 0)
    alt = jnp.where(row % 2 == 0, 1.0, -1.0)
    gates = ((u1_ref, cw1_ref, cb1_ref), (u2_ref, cw2_ref, cb2_ref))
    z = short_conv(u0_ref, cw0_ref, cb0_ref)
    kb = min(m, FREQ_BLOCK)
    for n in range(HY_ORDER):
        for j in range(n_lane_blocks):
            z_scr[j] = z[:, j * LANES:(j + 1) * LANES]
        ze = jnp.concatenate([z_scr[j, pl.ds(0, m, stride=2), :] for j in range(n_lane_blocks)], axis=1)
        zo = jnp.concatenate([z_scr[j, pl.ds(1, m, stride=2), :] for j in range(n_lane_blocks)], axis=1)
        e_mid = jnp.sum(alt * ze, axis=0, keepdims=True)
        o_mid = jnp.sum(alt * zo, axis=0, keepdims=True)
        zeb = ze.astype(BF16)
        zob = zo.astype(BF16)
        for r in range(0, m, kb):
            rows = slice(r, r + kb)
            srows = slice(m + r, m + r + kb)
            ec = _dot(cs_ref[rows, :], zeb)
            es = _dot(cs_ref[srows, :], zeb)
            oc = _dot(cs_ref[rows, :], zob)
            os_ = _dot(cs_ref[srows, :], zob)
            c = tw_ref[0, rows, :]
            s = tw_ref[1, rows, :]
            pc = c * oc - s * os_
            ps = c * os_ + s * oc
            xca, xsa = ec + pc, es + ps
            xcb, xsb = ec - pc, ps - es
            har = h_ref[n, rows, :]
            hai = h_ref[n, srows, :]
            hbr = h_ref[n, 2 * m + r:2 * m + r + kb, :]
            hbi = h_ref[n, 3 * m + r:3 * m + r + kb, :]
            yar = xca * har + xsa * hai
            yai = xca * hai - xsa * har
            ybr = xcb * hbr + xsb * hbi
            ybi = xcb * hbi - xsb * hbr
            dr = yar - ybr
            di = yai + ybi
            g_scr[0, rows, :] = (yar + ybr).astype(BF16)
            g_scr[0, srows, :] = (ybi - yai).astype(BF16)
            g_scr[1, rows, :] = (c * dr - s * di).astype(BF16)
            g_scr[1, srows, :] = (-(s * dr + c * di)).astype(BF16)
        hr = hm_ref[n, 0:1, :]
        hi = hm_ref[n, 1:2, :]
        ymr = e_mid * hr + o_mid * hi
        ymi = e_mid * hi - o_mid * hr
        y_even = _dot(cs_ref[:m, :], g_scr[0, :m, :]) + _dot(cs_ref[m:, :], g_scr[0, m:, :]) + alt * ymr
        y_odd = _dot(cs_ref[:m, :], g_scr[1, :m, :]) + _dot(cs_ref[m:, :], g_scr[1, m:, :]) - alt * ymi
        for j in range(n_lane_blocks):
            c_scr[j, pl.ds(0, m, stride=2), :] = y_even[:, j * LANES:(j + 1) * LANES]
            c_scr[j, pl.ds(1, m, stride=2), :] = y_odd[:, j * LANES:(j + 1) * LANES]
        conv = jnp.concatenate([c_scr[j] for j in range(n_lane_blocks)], axis=1)
        z = short_conv(*gates[n]) * (conv + bias_ref[n] * z)
    z_ref[0] = z.astype(z_ref.dtype)


def _hy_conv(proj, conv_w, conv_b, cs, hspec, hl, hm, tw, bias, batch, seq, seg, col0, width, tc, split, bp,
             cast_srcs):
    proj3 = proj.reshape(batch, seq, proj.shape[-1])
    blk0 = col0 // tc
    per = width // tc
    n_b = batch // bp
    assert bp == 1 or not split
    once = dict(pipeline_mode=pl.Buffered(1))

    def u_spec(p):
        return pl.BlockSpec((bp, seq, tc), lambda c, b: (b, 0, blk0 + p * per + c))

    def w_spec(p, rows):
        return pl.BlockSpec((rows, tc), lambda c, b: (0, p * per + c))

    in_specs = [u_spec(0), u_spec(1), u_spec(2),
                w_spec(0, HY_SHORT), w_spec(1, HY_SHORT), w_spec(2, HY_SHORT),
                w_spec(0, 1), w_spec(1, 1), w_spec(2, 1),
                pl.BlockSpec(cs.shape, lambda c, b: (0, 0), **once),
                pl.BlockSpec((HY_ORDER, 2 * seq, tc), lambda c, b: (0, 0, c), **once)]
    bias_spec = pl.BlockSpec((HY_ORDER, 1, tc), lambda c, b: (0, 0, c))
    operands = [proj3, proj3, proj3, conv_w, conv_w, conv_w, conv_b, conv_b, conv_b, cs, hspec]
    if split:
        kern = functools.partial(_hyconv_split_kernel, seq=seq, seg=seg)
        in_specs += [pl.BlockSpec((HY_ORDER, 2, tc), lambda c, b: (0, 0, c)),
                     pl.BlockSpec((2, seq // 2, tc), lambda c, b: (0, 0, 0)), bias_spec]
        operands += [hm, tw, bias]
        scratch = [pltpu.VMEM((2, seq, tc), BF16), pltpu.VMEM((tc // LANES, seq, LANES), F32),
                   pltpu.VMEM((tc // LANES, seq, LANES), F32)]
    else:
        kern = functools.partial(_hyconv_kernel, seq=seq, seg=seg)
        in_specs += [pl.BlockSpec((HY_ORDER, 1, tc), lambda c, b: (0, 0, c)), bias_spec]
        operands += [hl, bias]
        scratch = [pltpu.VMEM((bp, 2 * seq, tc), BF16)]
    cast_specs, cast_shapes = _cast_rider_specs(cast_srcs, per * n_b, lambda c, b: c * n_b + b)
    kern = _with_cast_riders(kern, len(operands), 1, len(cast_srcs))
    return pl.pallas_call(
        kern,
        grid=(per, n_b),
        in_specs=in_specs + cast_specs,
        out_specs=[pl.BlockSpec((bp, seq, tc), lambda c, b: (b, 0, c))] + cast_specs,
        out_shape=[jax.ShapeDtypeStruct((batch, seq, width), BF16)] + cast_shapes,
        scratch_shapes=scratch,
        compiler_params=_cparams(("arbitrary", "arbitrary")),
        name="hy_conv",
    )(*operands, *cast_srcs)


def _mixout_kernel(*refs, gla_width, n_own, aliased):
    (yg_ref, zh_ref, x_ref, mod_ref, ghy_ref, gpost_ref, gpre_ref, wo_ref, wrh_ref, wrl_ref, br_ref,
     cnt_in_ref) = refs[:12]
    x1_ref, hg_ref, route_ref, cnt_ref, cnt_scr = refs[12 + int(aliased):]
    d = x_ref.shape[1]
    tm = x_ref.shape[0] // MIX_SUBTILES
    i = pl.program_id(0)

    @pl.when(i == 0)
    def _():
        cnt_scr[...] = cnt_in_ref[...]

    @pl.when(i < n_own)
    def _():
        for s in range(MIX_SUBTILES):
            _mixout_rows(slice(s * tm, (s + 1) * tm), tm, d, gla_width, yg_ref, zh_ref, x_ref, mod_ref, ghy_ref,
                         gpost_ref, gpre_ref, wo_ref, wrh_ref, wrl_ref, br_ref, x1_ref, hg_ref, route_ref, cnt_scr)

    @pl.when(i >= n_own)
    def _():
        hg_ref[...] = jnp.zeros_like(hg_ref)

    cnt_ref[...] = cnt_scr[...]


def _mixout_rows(rows, tm, d, gla_width, yg_ref, zh_ref, x_ref, mod_ref, ghy_ref, gpost_ref, gpre_ref, wo_ref,
                 wrh_ref, wrl_ref, br_ref, x1_ref, hg_ref, route_ref, cnt_scr):
    yh = _rms(zh_ref[rows, :].astype(F32), ghy_ref[...]).astype(BF16)
    y = _dot(yg_ref[rows, :], wo_ref[:gla_width, :]) + _dot(yh, wo_ref[gla_width:, :])
    x1 = x_ref[rows, :] + _rms(y, gpost_ref[...] * mod_ref[0, 2:3, :])
    x1_ref[rows, :] = x1
    h2 = _rms(x1, gpre_ref[...] * (1.0 + mod_ref[0, 4:5, :])) + mod_ref[0, 3:4, :]
    hh, hl = _split2(h2)
    hg_ref[rows, :d] = h2
    logits = _dot(hh, wrh_ref[...]) + (_dot(hh, wrl_ref[...]) + _dot(hl, wrh_ref[...])) + br_ref[...]

    lane = lax.broadcasted_iota(jnp.int32, logits.shape, 1).astype(F32)
    neg = -jnp.inf
    is_grp = (lane >= N_EXPERTS) & (lane < N_EXPERTS + N_GROUPS)
    m = jnp.max(jnp.where(is_grp, logits, neg), axis=1, keepdims=True)
    p_grp = 1.0 / jnp.sum(jnp.where(is_grp, jnp.exp(logits - m), 0.0), axis=1, keepdims=True)
    grp = jnp.min(jnp.where(is_grp & (logits == m), lane - N_EXPERTS, 1e9), axis=1, keepdims=True)
    sel = (lane >= grp * EXP_PER_GROUP) & (lane < (grp + 1.0) * EXP_PER_GROUP)
    me = jnp.max(jnp.where(sel, logits, neg), axis=1, keepdims=True)
    pe = jnp.where(sel, jnp.exp(logits - me), -1.0)
    v1 = jnp.max(pe, axis=1, keepdims=True)
    i1 = jnp.min(jnp.where(pe == v1, lane, 1e9), axis=1, keepdims=True)
    pe2 = jnp.where(lane == i1, -1.0, pe)
    v2 = jnp.max(pe2, axis=1, keepdims=True)
    i2 = jnp.min(jnp.where(pe2 == v2, lane, 1e9), axis=1, keepdims=True)
    den = v1 + v2
    gates = jnp.where(lane == i1, v1 / den, jnp.where(lane == i2, v2 / den, 0.0)) * p_grp

    lo = jnp.minimum(i1, i2) - grp * EXP_PER_GROUP
    hi = jnp.maximum(i1, i2) - grp * EXP_PER_GROUP
    bucket = grp * PAIRS_PER_GROUP + lo * (2 * EXP_PER_GROUP - 1 - lo) * 0.5 + (hi - lo - 1.0)
    onehot = lane == bucket
    r_i = lax.broadcasted_iota(jnp.int32, (tm, tm), 0)
    c_i = lax.broadcasted_iota(jnp.int32, (tm, tm), 1)
    earlier = _dot((r_i > c_i).astype(BF16), onehot.astype(BF16)) + cnt_scr[...]
    rank = jnp.sum(jnp.where(onehot, earlier, 0.0), axis=1, keepdims=True)
    cnt_scr[...] += jnp.sum(onehot.astype(F32), axis=0, keepdims=True)
    route = jnp.where(lane == ROUTE_BUCKET_LANE, bucket, jnp.where(lane == ROUTE_RANK_LANE, rank, gates))
    hg_ref[rows, d:] = route
    route_ref[:, rows] = route.T[ROUTE_BUCKET_LANE:ROUTE_BUCKET_LANE + SUBLANES, :]


def _mix_out(yg, zh, x, mod, mod_row, g_hy, g_post, g_pre, w_out, wr_hi, wr_lo, b_r, cnt_in, hg_all, row0, n_total, tm):
    n, d = x.shape
    gw = yg.shape[1]
    hw = zh.shape[1]
    aliased = hg_all is not None
    n_own = n // tm
    n_steps = n_own if aliased else n_total // tm
    assert row0 % tm == 0 and n_total % tm == 0 and (row0 + n == n_total if aliased else row0 == 0)
    kern = functools.partial(_mixout_kernel, gla_width=gw, n_own=n_own, aliased=aliased)
    row = lambda i: (jnp.minimum(i, n_own - 1), 0)
    fixed = lambda i: (0, 0)
    in_specs = [pl.BlockSpec((tm, gw), row),
                pl.BlockSpec((tm, hw), row),
                pl.BlockSpec((tm, d), row),
                pl.BlockSpec((1, 6, d), lambda i: (mod_row(jnp.minimum(i, n_own - 1) * tm), 0, 0)),
                pl.BlockSpec((1, hw), fixed),
                pl.BlockSpec((1, d), fixed),
                pl.BlockSpec((1, d), fixed),
                pl.BlockSpec((gw + hw, d), fixed),
                pl.BlockSpec((d, LANES), fixed),
                pl.BlockSpec((d, LANES), fixed),
                pl.BlockSpec((1, LANES), fixed),
                pl.BlockSpec((1, LANES), fixed)]
    operands = [yg, zh, x, mod, g_hy, g_post, g_pre, w_out, wr_hi, wr_lo, b_r, cnt_in]
    if aliased:
        in_specs.append(pl.BlockSpec(memory_space=pl.ANY))
        operands.append(hg_all)
    return pl.pallas_call(
        kern,
        grid=(n_steps,),
        in_specs=in_specs,
        out_specs=[pl.BlockSpec((tm, d), row),
                   pl.BlockSpec((tm, d + LANES), lambda i: (row0 // tm + i, 0)),
                   pl.BlockSpec((SUBLANES, tm), lambda i: (0, jnp.minimum(i, n_own - 1))),
                   pl.BlockSpec((1, LANES), fixed)],
        out_shape=[jax.ShapeDtypeStruct((n, d), F32),
                   jax.ShapeDtypeStruct((n_total, d + LANES), F32),
                   jax.ShapeDtypeStruct((SUBLANES, n), F32),
                   jax.ShapeDtypeStruct((1, LANES), F32)],
        scratch_shapes=[pltpu.VMEM((1, LANES), F32)],
        input_output_aliases={len(operands) - 1: 1} if aliased else {},
        compiler_params=_cparams(("arbitrary",)),
        name="mix_out",
    )(*operands)


def _row_gather_copy(src_hbm, row, buf, slot, r, sem):
    return pltpu.make_async_copy(src_hbm.at[pl.ds(row, 1), :], buf.at[slot, pl.ds(r, 1), :], sem.at[slot])


def _start_row_gather(idx_ref, base, src_hbm, buf, slot, sem, tm):
    def body(r, carry):
        _row_gather_copy(src_hbm, idx_ref[base + r], buf, slot, r, sem).start()
        return carry

    lax.fori_loop(0, tm, body, 0, unroll=8)


def _start_row_gather_inline(idx_ref, base, src_hbm, buf, slot, sem, tm):
    for r in range(tm):
        _row_gather_copy(src_hbm, idx_ref[base + r], buf, slot, r, sem).start()


def _wait_row_gather(src_hbm, buf, slot, sem, tm):
    pltpu.make_async_copy(src_hbm.at[pl.ds(0, tm), :], buf.at[slot], sem.at[slot]).wait()


def _moe_kernel(src_ref, ea_ref, eb_ref, valid_ref, hg_hbm,
                wga_ref, wua_ref, wda_ref, wgb_ref, wub_ref, wdb_ref, y_ref, buf, sem, *, tm, d):
    t = pl.program_id(0)
    slot = t % 2

    @pl.when(t == 0)
    def _():
        _start_row_gather(src_ref, 0, hg_hbm, buf, 0, sem, tm)

    @pl.when(valid_ref[t] == 1)
    def _():
        _wait_row_gather(hg_hbm, buf, slot, sem, tm)
        _start_row_gather_inline(src_ref, (t + 1) * tm, hg_hbm, buf, 1 - slot, sem, tm)
        h = buf[slot, :, :d].astype(BF16)
        gates = buf[slot, :, d:]
        lane = lax.broadcasted_iota(jnp.int32, gates.shape, 1)

        experts = ((wga_ref, wua_ref, wda_ref, ea_ref[t]), (wgb_ref, wub_ref, wdb_ref, eb_ref[t]))
        pre = [_dot(h, wg_ref[0]) for wg_ref, _, _, _ in experts]
        up = [_dot(h, wu_ref[0]) for _, wu_ref, _, _ in experts]
        he = [(_silu(pre[i]) * up[i]).astype(BF16) for i in range(2)]
        gate = [jnp.sum(jnp.where(lane == e, gates, 0.0), axis=1, keepdims=True) for _, _, _, e in experts]
        out = [gate[i] * _dot(he[i], experts[i][2][0]) for i in range(2)]
        y_ref[...] = out[0] + out[1]

    @pl.when(valid_ref[t] == 0)
    def _():
        @pl.when(valid_ref[jnp.maximum(t - 1, 0)] == 1)
        def _():
            _wait_row_gather(hg_hbm, buf, slot, sem, tm)

        y_ref[...] = jnp.zeros_like(y_ref)


def _moe(hg, src, ea, eb, valid, wg, wu, wd, n_tiles, tm):
    d = hg.shape[1] - LANES
    _, _, de = wg.shape
    kern = functools.partial(_moe_kernel, tm=tm, d=d)
    wa = lambda t, src, ea, eb, valid: (ea[t], 0, 0)
    wb = lambda t, src, ea, eb, valid: (eb[t], 0, 0)
    return pl.pallas_call(
        kern,
        grid_spec=pltpu.PrefetchScalarGridSpec(
            num_scalar_prefetch=4,
            grid=(n_tiles,),
            in_specs=[pl.BlockSpec(memory_space=pl.ANY),
                      pl.BlockSpec((1, d, de), wa), pl.BlockSpec((1, d, de), wa), pl.BlockSpec((1, de, d), wa),
                      pl.BlockSpec((1, d, de), wb), pl.BlockSpec((1, d, de), wb), pl.BlockSpec((1, de, d), wb)],
            out_specs=pl.BlockSpec((tm, d), lambda t, src, ea, eb, valid: (t, 0)),
            scratch_shapes=[pltpu.VMEM((2, tm, d + LANES), F32), pltpu.SemaphoreType.DMA((2,))]),
        out_shape=jax.ShapeDtypeStruct((n_tiles * tm, d), F32),
        compiler_params=_cparams(("arbitrary",)),
        name="moe",
    )(src, ea, eb, valid, hg, wg, wu, wd, wg, wu, wd)


def _ffn_out_kernel(pos_ref, y_hbm, x1_ref, mod_ref, gpost_ref, o_ref, buf, sem, *, tm):
    i = pl.program_id(0)
    n_i = pl.num_programs(0)
    slot = i % 2

    @pl.when(i == 0)
    def _():
        _start_row_gather(pos_ref, 0, y_hbm, buf, 0, sem, tm)

    _wait_row_gather(y_hbm, buf, slot, sem, tm)

    @pl.when(i + 1 < n_i)
    def _():
        _start_row_gather_inline(pos_ref, (i + 1) * tm, y_hbm, buf, 1 - slot, sem, tm)
        o_ref[...] = x1_ref[...] + _rms(buf[slot], gpost_ref[...] * mod_ref[0, 5:6, :])

    @pl.when(i + 1 == n_i)
    def _():
        o_ref[...] = x1_ref[...] + _rms(buf[slot], gpost_ref[...] * mod_ref[0, 5:6, :])


def _ffn_out(y_sorted, pos, x1, mod, mod_row, g_post, tm):
    n, d = x1.shape
    kern = functools.partial(_ffn_out_kernel, tm=tm)
    return pl.pallas_call(
        kern,
        grid_spec=pltpu.PrefetchScalarGridSpec(
            num_scalar_prefetch=1,
            grid=(n // tm,),
            in_specs=[pl.BlockSpec(memory_space=pl.ANY),
                      pl.BlockSpec((tm, d), lambda i, pos: (i, 0)),
                      pl.BlockSpec((1, 6, d), lambda i, pos: (mod_row(i * tm), 0, 0)),
                      pl.BlockSpec((1, d), lambda i, pos: (0, 0))],
            out_specs=pl.BlockSpec((tm, d), lambda i, pos: (i, 0)),
            scratch_shapes=[pltpu.VMEM((2, tm, d), F32), pltpu.SemaphoreType.DMA((2,))]),
        out_shape=jax.ShapeDtypeStruct((n, d), F32),
        compiler_params=_cparams(("arbitrary",)),
        name="ffn_out",
    )(pos, y_sorted, x1, mod, g_post)


def _route_tables(route, counts, n_tiles, tm):
    n = route.shape[1]
    n_buckets = N_GROUPS * PAIRS_PER_GROUP
    cnt = counts[0, :n_buckets].astype(jnp.int32)
    padded = (cnt + tm - 1) // tm * tm
    ends = jnp.cumsum(padded)
    starts = ends - padded
    bucket = route[0].astype(jnp.int32)
    rank = route[ROUTE_RANK_LANE - ROUTE_BUCKET_LANE].astype(jnp.int32)
    pos = starts[bucket] + rank
    src = (jnp.arange(n_tiles * tm, dtype=jnp.int32) % n).at[pos].set(jnp.arange(n, dtype=jnp.int32))
    n_valid = ends[-1] // tm
    tile = jnp.arange(n_tiles, dtype=jnp.int32)
    used = jnp.minimum(tile, n_valid - 1)
    tile_bucket = jnp.sum((ends[None, :] <= (used * tm)[:, None]).astype(jnp.int32), axis=1)
    grp = tile_bucket // PAIRS_PER_GROUP
    pair = tile_bucket % PAIRS_PER_GROUP
    pair_lo = jnp.array([a for a in range(EXP_PER_GROUP) for b in range(a + 1, EXP_PER_GROUP)], jnp.int32)
    pair_hi = jnp.array([b for a in range(EXP_PER_GROUP) for b in range(a + 1, EXP_PER_GROUP)], jnp.int32)
    ea = grp * EXP_PER_GROUP + pair_lo[pair]
    eb = grp * EXP_PER_GROUP + pair_hi[pair]
    valid = (tile < n_valid).astype(jnp.int32)
    return pos, src, ea, eb, valid


def _dft_kernel(cs_ref, *, seq, tr):
    k = pl.program_id(0) * tr + lax.broadcasted_iota(jnp.int32, (tr, LANES), 0)
    j = lax.broadcasted_iota(jnp.int32, (tr, LANES), 1)
    period = 2 * seq

    ang = ((k * j) % period).astype(F32) * (math.pi / seq)
    c0 = jnp.cos(ang)
    s0 = jnp.sin(ang)
    ch = c0[:, LANES // 2:LANES // 2 + 1]
    sh = s0[:, LANES // 2:LANES // 2 + 1]
    c_step = 2.0 * ch * ch - 1.0
    s_step = 2.0 * sh * ch
    ca = jnp.ones_like(ch)
    sa = jnp.zeros_like(ch)
    for t1 in range(seq // LANES):
        cols = slice(t1 * LANES, (t1 + 1) * LANES)
        cs_ref[0, :, cols] = (ca * c0 - sa * s0).astype(cs_ref.dtype)
        cs_ref[1, :, cols] = (sa * c0 + ca * s0).astype(cs_ref.dtype)
        ca, sa = ca * c_step - sa * s_step, sa * c_step + ca * s_step


def _dft_cos_sin(seq, n_rows, tr):
    kern = functools.partial(_dft_kernel, seq=seq, tr=tr)
    return pl.pallas_call(
        kern,
        grid=(n_rows // tr,),
        out_specs=pl.BlockSpec((2, tr, seq), lambda r: (0, r, 0)),
        out_shape=jax.ShapeDtypeStruct((2, n_rows, seq), BF16),
        compiler_params=_cparams(("arbitrary",)),
        name="dft_matrix",
    )()


def _positional_features(seq):
    t = jnp.arange(seq, dtype=F32)
    t01 = t / max(seq - 1, 1)
    ang = 2.0 * math.pi * t / seq
    bands = jnp.linspace(1e-4, HY_BANDS - 1, HY_BANDS, dtype=F32)
    pe = jnp.concatenate([t01[:, None], jnp.cos(ang[:, None] * bands), -jnp.sin(ang[:, None] * bands)], axis=-1)
    return jnp.pad(pe, ((0, 0), (0, LANES - pe.shape[1])))


def _mixer_and_router(x3, mod, mod_row, s0f, s0b, n_seg, p, tiles, cnt_in, hg_all, row0, n_total, w_out, casts):
    batch, seq, d = x3.shape
    x = x3.reshape(batch * seq, d)
    dk, dv = p["dk"], p["dv"]
    hy_width = p["hy_width"]
    casted = {}

    proj, lr, *casted["premix"] = _premix_proj(x, mod, mod_row, p["g_pre_mix"], p["w_main"], p["w_lr"],
                                               tiles["tm_proj"], tiles["tn_proj"], casts["premix"])
    if w_out is None:
        w_out = casted["premix"][0]
    y_gla, s_f, s_b, *casted["gla"] = _gla(proj, lr, p["wdf"], p["bdf"], p["wdb"], p["bdb"], p["g_gla"], s0f, s0b,
                                           batch, seq, dk, dv, tiles["gla_heads_per_step"], casts["gla"])

    ad, hl, hm = _hy_filter(_positional_features(seq), p["hy_w1"], p["hy_b1"], p["hy_w2"], p["hy_b2"], p["hy_freq"],
                            p["hy_w3"], p["hy_deltas"], seq, hy_width, tiles["tc_filter"])
    tr, tc_conv = tiles["tr_spec"], tiles["tc_conv"]
    if tiles["hy_split"]:
        half = seq // 2
        hspec = _hy_spectrum(_dft_cos_sin(seq, half, tr), ad, seq, half, 4, hy_width, tr, tiles["tc_spec"])
        cs = _dft_cos_sin(half, half, tr).reshape(seq, half)
        ang = jnp.arange(half, dtype=F32) * (math.pi / seq)
        tw = jnp.broadcast_to(jnp.stack([jnp.cos(ang), jnp.sin(ang)])[:, :, None], (2, half, tc_conv))
    else:
        cs3 = _dft_cos_sin(seq, seq, tr)
        hspec = _hy_spectrum(cs3, ad, seq, seq, 2, hy_width, tr, tiles["tc_spec"])
        cs = cs3.reshape(2 * seq, seq)
        tw = None
    z_hy, *casted["hy_conv"] = _hy_conv(proj, p["hy_conv_w"], p["hy_conv_b"], cs, hspec, hl, hm, tw, p["hy_bias"],
                                        batch, seq, seq // n_seg, p["hy_col0"], hy_width, tc_conv, tiles["hy_split"],
                                        tiles["hy_batch_per_step"], casts["hy_conv"])

    x1, hg_all, route, counts = _mix_out(y_gla.reshape(batch * seq, -1), z_hy.reshape(batch * seq, -1), x, mod,
                                         mod_row, p["g_hy"], p["g_post_mix"], p["g_pre_ffn"], w_out, p["wr_hi"],
                                         p["wr_lo"], p["b_r"], cnt_in, hg_all, row0, n_total, tiles["tm_mix"])
    return x1, hg_all, route, counts, s_f, s_b, casted


def kernel(x_prompt, x_sample, c, state_gla_fwd, state_gla_bwd, c_ctx, w_ada, b_ada, g_pre_mix, g_post_mix, g_pre_ffn, g_post_ffn, w_in, w_dec_f, b_dec_f, w_dec_b, b_dec_b, g_gla, hy_conv_w, hy_conv_b, hy_w1, hy_b1, hy_w2, hy_b2, hy_w3, hy_freq, hy_bias, g_hy, w_out, w_router_grp, b_router_grp, w_router_exp, b_router_exp, w_exp_gate, w_exp_up, w_exp_down):
    depth = w_ada.shape[0]
    assert depth == 1
    l = 0
    d = x_prompt.shape[-1]
    dec_batch = x_sample.shape[0]
    heads = GLA_HEADS
    dk, dv = state_gla_fwd.shape[-2:]
    kdim = heads * dk
    gla_width = heads * dv
    hy_width = g_hy.shape[-1]
    hid = hy_w2.shape[-1]

    cond = jnp.concatenate([c_ctx[None, :], c], axis=0)
    cond = jnp.pad(cond, ((0, -cond.shape[0] % SUBLANES), (0, 0)))
    mod = _ada_mod(cond, w_ada[l], b_ada[l]).reshape(cond.shape[0], 6, d)

    n_main = 2 * kdim + 2 * gla_width
    w_main, w_lr = _repack_w_in(w_in, l, n_main, 2 * GLA_RANK, REPACK_COLS)

    def dec_weight(w, first_row):
        wh = w.reshape(GLA_RANK, heads, dk).transpose(1, 0, 2)
        return jnp.pad(wh, ((0, 0), (first_row, LANES - GLA_RANK - first_row), (0, 0)))

    deltas = jnp.abs(jnp.linspace(math.log(HY_TARGET) / HY_SLOW_PCT, math.log(HY_TARGET) / HY_FAST_PCT, hy_width,
                                  dtype=F32))
    w_r = jnp.pad(jnp.concatenate([w_router_exp[l], w_router_grp[l]], axis=1),
                  ((0, 0), (0, LANES - N_EXPERTS - N_GROUPS)))
    wr_hi, wr_lo = _split2(w_r)
    b_r = jnp.pad(jnp.concatenate([b_router_exp[l], b_router_grp[l]]), (0, LANES - N_EXPERTS - N_GROUPS))

    p = dict(
        dk=dk, dv=dv, hy_width=hy_width, hy_col0=n_main,
        g_pre_mix=g_pre_mix[l][None, :], g_post_mix=g_post_mix[l][None, :],
        g_pre_ffn=g_pre_ffn[l][None, :], g_post_ffn=g_post_ffn[l][None, :],
        w_main=w_main, w_lr=w_lr,
        wdf=dec_weight(w_dec_f[l], 0), bdf=b_dec_f[l][None, :],
        wdb=dec_weight(w_dec_b[l], GLA_RANK), bdb=b_dec_b[l][None, :],
        g_gla=g_gla[l][None, :],
        hy_conv_w=hy_conv_w[l], hy_conv_b=hy_conv_b[l][None, :],
        hy_w1=jnp.pad(hy_w1[l], ((0, LANES - hy_w1.shape[1]), (0, 0))), hy_b1=hy_b1[l][None, :],
        hy_w2=hy_w2[l], hy_b2=hy_b2[l][None, :], hy_freq=hy_freq[l][None, :],
        hy_w3=hy_w3[l].reshape(hid, 2 * HY_ORDER, hy_width).transpose(1, 0, 2), hy_deltas=deltas[None, :],
        hy_bias=hy_bias[l][:, None, :], g_hy=g_hy[l][None, :],
        wr_hi=wr_hi, wr_lo=wr_lo, b_r=b_r[None, :],
    )

    zero_state = jnp.zeros((1, heads, dk, dv), F32)
    dec_seq = x_sample.shape[1]
    tiles_p = _tile_plan(x_prompt.shape[1])
    tiles_s = _tile_plan(dec_seq)
    tm_moe = MXU_DIM
    tm_out = 512

    n_p = x_prompt.shape[0] * x_prompt.shape[1]
    n_s = dec_batch * dec_seq
    n_all = n_p + n_s
    mod_row_p = lambda r: 0
    mod_row_s = lambda r: 1 + r // dec_seq
    n_exp, _, d_exp = w_exp_gate.shape[1:]
    casts_p = dict(premix=[w_out[l]],
                   gla=[w_exp_gate[l].reshape(n_exp * d, d_exp), w_exp_down[l].reshape(n_exp * d_exp, d)],
                   hy_conv=[w_exp_up[l].reshape(n_exp * d, d_exp)])
    casts_s = dict(premix=[], gla=[], hy_conv=[])
    x1_p, hg_all, route_p, counts, s_f, s_b, casted = _mixer_and_router(
        x_prompt, mod, mod_row_p, zero_state, zero_state, 1, p, tiles_p, jnp.zeros((1, LANES), F32), None, 0, n_all,
        None, casts_p)
    (w_out_b,), (w_gate_b, w_down_b), (w_up_b,) = casted["premix"], casted["gla"], casted["hy_conv"]
    x1_s, hg_all, route_s, counts, _, _, _ = _mixer_and_router(
        x_sample, mod, mod_row_s, state_gla_fwd[:, l], state_gla_bwd[:, l], dec_seq // GRID_W, p, tiles_s,
        counts, hg_all, n_p, n_all, w_out_b, casts_s)

    n_tiles = n_all // tm_moe + N_GROUPS * PAIRS_PER_GROUP
    pos, src, ea, eb, valid = _route_tables(jnp.concatenate([route_p, route_s], axis=1), counts, n_tiles, tm_moe)
    y_sorted = _moe(hg_all, src, ea, eb, valid, w_gate_b.reshape(n_exp, d, d_exp), w_up_b.reshape(n_exp, d, d_exp),
                    w_down_b.reshape(n_exp, d_exp, d), n_tiles, tm_moe)
    y_p = _ffn_out(y_sorted, pos[:n_p], x1_p, mod, mod_row_p, p["g_post_ffn"], tm_out)
    y_s = _ffn_out(y_sorted, pos[n_p:], x1_s, mod, mod_row_s, p["g_post_ffn"], tm_out)
    return (y_p.reshape(x_prompt.shape), y_s.reshape(x_sample.shape),
            s_f[:, None].astype(x_prompt.dtype), s_b[:, None].astype(x_prompt.dtype))
```

```python
import functools
import math

import jax
import jax.numpy as jnp
from jax import lax
from jax.experimental import pallas as pl
from jax.experimental.pallas import tpu as pltpu

F32 = jnp.float32
BF16 = jnp.bfloat16

GRID_W = 64
GLA_HEADS = 4
GLA_RANK = 16
GLA_TAU = 16.0
GLA_CHUNK = 64
GLA_UNROLL = 4
MIX_SUBTILES = 1
HY_ORDER = 2
HY_SHORT = 3
HY_BANDS = 16
HY_TARGET = 1e-2
HY_FAST_PCT = 0.3
HY_SLOW_PCT = 1.5
N_GROUPS = 4
EXP_PER_GROUP = 4
N_EXPERTS = N_GROUPS * EXP_PER_GROUP
PAIRS_PER_GROUP = EXP_PER_GROUP * (EXP_PER_GROUP - 1) // 2
ROUTE_BUCKET_LANE = N_EXPERTS
ROUTE_RANK_LANE = N_EXPERTS + 1
EPS = 1e-6

LANES = 128
SUBLANES = 8
MXU_DIM = 256
VMEM_LIMIT = 56 << 20
REPACK_COLS = MXU_DIM
FREQ_BLOCK = 2 * MXU_DIM


def _tile_plan(seq):
    long_seq = seq >= 1024
    return dict(
        tm_proj=1024, tn_proj=2048,
        tc_filter=128 if long_seq else 256,
        tr_spec=512 if long_seq else 256,
        tc_spec=1024 if long_seq else 512,
        hy_split=long_seq,
        tc_conv=256 if long_seq else 512,
        hy_batch_per_step=1 if long_seq else 2,
        gla_heads_per_step=2 if long_seq else 4,
        tm_mix=512,
    )


def _cparams(sem):
    return pltpu.CompilerParams(dimension_semantics=sem, vmem_limit_bytes=VMEM_LIMIT)


def _dot(a, b):
    return jnp.dot(a, b, preferred_element_type=F32)


def _dot_nt(a, b):
    return lax.dot_general(a, b, (((1,), (1,)), ((), ())), preferred_element_type=F32)


def _dot_tn(a, b):
    return lax.dot_general(a, b, (((0,), (0,)), ((), ())), preferred_element_type=F32)


def _split2(x):
    hi = x.astype(BF16)
    lo = (x - hi.astype(F32)).astype(BF16)
    return hi, lo


def _dot_hp(a, b):
    ah, al = a if isinstance(a, tuple) else _split2(a)
    bh, bl = _split2(b)
    return _dot(ah, bh) + (_dot(ah, bl) + _dot(al, bh))


def _dot_exact_lhs(t, x):
    hi, lo = _split2(x)
    return _dot(t, hi) + _dot(t, lo)


def _rms(x, g):
    return x * lax.rsqrt(jnp.mean(x * x, axis=-1, keepdims=True) + EPS) * g


def _silu(x):
    return x / (1.0 + jnp.exp(-x))


def _ada_kernel(c_ref, w_ref, b_ref, o_ref):
    s = _silu(c_ref[...]).astype(BF16)
    o_ref[...] = _dot(s, w_ref[...].astype(BF16)) + b_ref[...]


def _ada_mod(cond, w_ada, b_ada):
    rows, d = cond.shape
    n = w_ada.shape[1]
    tn = 1024
    return pl.pallas_call(
        _ada_kernel,
        grid=(n // tn,),
        in_specs=[pl.BlockSpec((rows, d), lambda j: (0, 0)),
                  pl.BlockSpec((d, tn), lambda j: (0, j)),
                  pl.BlockSpec((1, tn), lambda j: (0, j))],
        out_specs=pl.BlockSpec((rows, tn), lambda j: (0, j)),
        out_shape=jax.ShapeDtypeStruct((rows, n), F32),
        compiler_params=_cparams(("arbitrary",)),
        name="ada_mod",
    )(cond, w_ada, b_ada.reshape(1, n))


def _repack_kernel(wt_hbm, main_ref, lr_ref, buf, lr_buf, sem, lr_sem, *, layer, n_main, n_lr, tb):
    i = pl.program_id(0)
    n_i = pl.num_programs(0)
    slot = i % 2
    d = main_ref.shape[0]

    def block_copy(j, s):
        first = j * tb + jnp.where(j * tb >= n_main, n_lr, 0)
        return pltpu.make_async_copy(wt_hbm.at[layer, pl.ds(first, tb), :], buf.at[s], sem.at[s])

    lr_copy = pltpu.make_async_copy(wt_hbm.at[layer, pl.ds(n_main, n_lr), :], lr_buf, lr_sem)

    @pl.when(i == 0)
    def _():
        block_copy(0, 0).start()
        lr_copy.start()

    @pl.when(i + 1 < n_i)
    def _():
        block_copy(i + 1, 1 - slot).start()

    block_copy(i, slot).wait()
    main_ref[...] = buf[slot].T.astype(BF16)

    @pl.when(i == 0)
    def _():
        lr_copy.wait()
        lr = jnp.concatenate([lr_buf[...], jnp.zeros((LANES - n_lr, d), F32)], axis=0).T
        lr_ref[...] = lr.astype(BF16)


def _repack_w_in(w_in, layer, n_main, n_lr, tb):
    w_t = jnp.swapaxes(w_in, 1, 2)
    _, n_cols, d = w_t.shape
    n_out = n_cols - n_lr
    assert n_main % tb == 0 and n_out % tb == 0
    kern = functools.partial(_repack_kernel, layer=layer, n_main=n_main, n_lr=n_lr, tb=tb)
    return pl.pallas_call(
        kern,
        grid=(n_out // tb,),
        in_specs=[pl.BlockSpec(memory_space=pl.ANY)],
        out_specs=[pl.BlockSpec((d, tb), lambda i: (0, i)),
                   pl.BlockSpec((d, LANES), lambda i: (0, 0))],
        out_shape=[jax.ShapeDtypeStruct((d, n_out), BF16),
                   jax.ShapeDtypeStruct((d, LANES), BF16)],
        scratch_shapes=[pltpu.VMEM((2, tb, d), F32), pltpu.VMEM((n_lr, d), F32),
                        pltpu.SemaphoreType.DMA((2,)), pltpu.SemaphoreType.DMA(())],
        compiler_params=_cparams(("arbitrary",)),
        name="repack_w_in",
    )(w_t)


def _with_cast_riders(kernel_fn, n_in, n_out, n_cast):
    def kernel(*refs):
        ins = refs[:n_in]
        cast_in = refs[n_in:n_in + n_cast]
        outs = refs[n_in + n_cast:n_in + n_cast + n_out]
        cast_out = refs[n_in + n_cast + n_out:n_in + 2 * n_cast + n_out]
        kernel_fn(*ins, *outs, *refs[n_in + 2 * n_cast + n_out:])
        for src, dst in zip(cast_in, cast_out):
            dst[...] = src[...].astype(dst.dtype)

    return kernel


def _cast_rider_specs(cast_srcs, n_steps, step_of):
    n_slabs = 1 << (n_steps.bit_length() - 1)
    specs = []
    for w in cast_srcs:
        rows, cols = w.shape
        assert rows % n_slabs == 0
        specs.append(pl.BlockSpec((rows // n_slabs, cols),
                                  lambda *idx: (jnp.minimum(step_of(*idx), n_slabs - 1), 0)))
    return specs, [jax.ShapeDtypeStruct(w.shape, BF16) for w in cast_srcs]


def _premix_kernel(x_ref, mod_ref, g_ref, w_ref, wlr_ref, o_ref, lr_ref, h_scr):
    @pl.when(pl.program_id(1) == 0)
    def _():
        h = _rms(x_ref[...], g_ref[...] * (1.0 + mod_ref[0, 1:2, :])) + mod_ref[0, 0:1, :]
        h_scr[...] = h.astype(BF16)
        lr_ref[...] = _dot(h_scr[...], wlr_ref[...])

    o_ref[...] = _dot(h_scr[...], w_ref[...]).astype(o_ref.dtype)


def _premix_proj(x, mod, mod_row, g, w_main, w_lr, tm, tn, cast_srcs):
    n, d = x.shape
    nc = w_main.shape[1]
    n_j = nc // tn
    cast_specs, cast_shapes = _cast_rider_specs(cast_srcs, (n // tm) * n_j, lambda i, j: i * n_j + j)
    kern = _with_cast_riders(_premix_kernel, 5, 2, len(cast_srcs))
    return pl.pallas_call(
        kern,
        grid=(n // tm, n_j),
        in_specs=[pl.BlockSpec((tm, d), lambda i, j: (i, 0)),
                  pl.BlockSpec((1, 6, d), lambda i, j: (mod_row(i * tm), 0, 0)),
                  pl.BlockSpec((1, d), lambda i, j: (0, 0)),
                  pl.BlockSpec((d, tn), lambda i, j: (0, j)),
                  pl.BlockSpec((d, LANES), lambda i, j: (0, 0))] + cast_specs,
        out_specs=[pl.BlockSpec((tm, tn), lambda i, j: (i, j)),
                   pl.BlockSpec((tm, LANES), lambda i, j: (i, 0))] + cast_specs,
        out_shape=[jax.ShapeDtypeStruct((n, nc), BF16), jax.ShapeDtypeStruct((n, LANES), F32)] + cast_shapes,
        scratch_shapes=[pltpu.VMEM((tm, d), BF16)],
        compiler_params=_cparams(("arbitrary", "arbitrary")),
        name="premix_proj",
    )(x, mod, g, w_main, w_lr, *cast_srcs)


def _log_sigmoid(x):
    return jnp.minimum(x, 0.0) - jnp.log(1.0 + jnp.exp(-jnp.abs(x)))


def _gla_kernel(q_ref, k_ref, v_ref, g_ref, lr_ref, wdf_ref, bdf_ref, wdb_ref, bdb_ref, gg_ref,
                s0f_ref, s0b_ref, y_ref, sf_ref, sb_ref,
                laf_scr, lab_scr, qf_scr, qb_scr, o_scr, uf_scr, ub_scr, df_scr, db_scr, *, seq, dk, dv, hp):
    c = GLA_CHUNK
    n_chunks = seq // c
    scale = dk ** -0.5
    heads = range(hp)
    kcols = [slice(h * dk, (h + 1) * dk) for h in heads]
    vcols = [slice(h * dv, (h + 1) * dv) for h in heads]

    lr_split = _split2(lr_ref[0])
    for h in heads:
        laf_scr[h] = _log_sigmoid(_dot_hp(lr_split, wdf_ref[h]) + bdf_ref[:, kcols[h]]) / GLA_TAU
        lab_scr[h] = _log_sigmoid(_dot_hp(lr_split, wdb_ref[h]) + bdb_ref[:, kcols[h]]) / GLA_TAU

    per = GLA_UNROLL
    blk = per * c
    row = lax.broadcasted_iota(jnp.int32, (blk, blk), 0)
    col = lax.broadcasted_iota(jnp.int32, (blk, blk), 1)
    same = (row // c) == (col // c)
    lower = same & (row >= col)
    upper = same & (col >= row)
    t_fwd = lower.astype(BF16)
    t_bwd = upper.astype(BF16)

    def chunk_rows(x, r):
        return jnp.concatenate([jnp.broadcast_to(x[j * c + r:j * c + r + 1], (c, dk)) for j in range(per)], axis=0)

    def block_local(m, carry):
        sl = pl.ds(pl.multiple_of(m * blk, blk), blk)
        bf = [_dot_exact_lhs(t_fwd, laf_scr[h, sl, :]) for h in heads]
        bb = [_dot_exact_lhs(t_bwd, lab_scr[h, sl, :]) for h in heads]
        tot_f = [chunk_rows(x, c - 1) for x in bf]
        tot_b = [chunk_rows(x, 0) for x in bb]
        q = [q_ref[0, sl, kcols[h]].astype(F32) * scale for h in heads]
        k = [k_ref[0, sl, kcols[h]].astype(F32) for h in heads]
        v = [v_ref[0, sl, vcols[h]] for h in heads]
        qf = [(q[h] * jnp.exp(bf[h])).astype(BF16) for h in heads]
        kf = [(k[h] * jnp.exp(-bf[h])).astype(BF16) for h in heads]
        qb = [(q[h] * jnp.exp(bb[h])).astype(BF16) for h in heads]
        kb = [(k[h] * jnp.exp(-bb[h])).astype(BF16) for h in heads]
        ksf = [(k[h] * jnp.exp(tot_f[h] - bf[h])).astype(BF16) for h in heads]
        ksb = [(k[h] * jnp.exp(tot_b[h] - bb[h])).astype(BF16) for h in heads]
        sc_f = [_dot_nt(qf[h], kf[h]) for h in heads]
        sc_b = [_dot_nt(qb[h], kb[h]) for h in heads]
        att = [(jnp.where(lower, sc_f[h], 0.0) + jnp.where(upper, sc_b[h], 0.0)).astype(BF16) for h in heads]
        o_loc = [_dot(att[h], v[h]) for h in heads]
        dec_f = [jnp.exp(x) for x in tot_f]
        dec_b = [jnp.exp(x) for x in tot_b]
        for h in heads:
            o_scr[h, sl, :] = o_loc[h]
            qf_scr[h, sl, :] = qf[h]
            qb_scr[h, sl, :] = qb[h]
        for j in range(per):
            n = m * per + j
            rows = slice(j * c, (j + 1) * c)
            for h in heads:
                uf_scr[h, n] = _dot_tn(v[h][rows], ksf[h][rows])
                ub_scr[h, n] = _dot_tn(v[h][rows], ksb[h][rows])
                df_scr[h, n] = dec_f[h][j * c:j * c + SUBLANES]
                db_scr[h, n] = dec_b[h][j * c:j * c + SUBLANES]
        return carry

    lax.fori_loop(0, n_chunks // per, block_local, 0)

    def scan_fwd(n, s):
        upd = [uf_scr[h, n] for h in heads]
        for h in heads:
            uf_scr[h, n] = s[h]
        return tuple(s[h] * df_scr[h, n][0:1, :] + upd[h] for h in heads)

    def scan_bwd(i, s):
        n = n_chunks - 1 - i
        upd = [ub_scr[h, n] for h in heads]
        for h in heads:
            ub_scr[h, n] = s[h]
        return tuple(s[h] * db_scr[h, n][0:1, :] + upd[h] for h in heads)

    s_f = lax.fori_loop(0, n_chunks, scan_fwd, tuple(s0f_ref[0, h].T for h in heads), unroll=GLA_UNROLL)
    s_b = lax.fori_loop(0, n_chunks, scan_bwd, tuple(s0b_ref[0, h].T for h in heads), unroll=GLA_UNROLL)
    for h in heads:
        sf_ref[0, h] = s_f[h].T
        sb_ref[0, h] = s_b[h].T

    def chunk_inter(n, carry):
        sl = pl.ds(pl.multiple_of(n * c, c), c)
        inter = [_dot_nt(qf_scr[h, sl, :], uf_scr[h, n].astype(BF16))
                 + _dot_nt(qb_scr[h, sl, :], ub_scr[h, n].astype(BF16)) for h in heads]
        for h in heads:
            o_scr[h, sl, :] += inter[h]
        return carry

    lax.fori_loop(0, n_chunks, chunk_inter, 0, unroll=GLA_UNROLL)

    for h in heads:
        o = _rms(o_scr[h], gg_ref[...])
        y_ref[0, :, vcols[h]] = (o * _silu(g_ref[0, :, vcols[h]].astype(F32))).astype(y_ref.dtype)


def _gla(proj, lr, wdf, bdf, wdb, bdb, g_gla, s0f, s0b, batch, seq, dk, dv, hp, cast_srcs):
    heads = GLA_HEADS
    proj3 = proj.reshape(batch, seq, proj.shape[-1])
    lr3 = lr.reshape(batch, seq, LANES)
    kdim = heads * dk
    width = heads * dv
    bk, bv = hp * dk, hp * dv
    k_blk = kdim // bk
    v_blk = 2 * kdim // bv
    g_blk = (2 * kdim + width) // bv
    n_chunks = seq // GLA_CHUNK

    def s0_map(s0):
        if s0.shape[0] == batch:
            return lambda b, h: (b, h, 0, 0)
        return lambda b, h: (0, h, 0, 0)

    n_hsteps = heads // hp
    cast_specs, cast_shapes = _cast_rider_specs(cast_srcs, batch * n_hsteps, lambda b, h: b * n_hsteps + h)
    kern = _with_cast_riders(functools.partial(_gla_kernel, seq=seq, dk=dk, dv=dv, hp=hp), 12, 3, len(cast_srcs))
    return pl.pallas_call(
        kern,
        grid=(batch, n_hsteps),
        in_specs=[pl.BlockSpec((1, seq, bk), lambda b, h: (b, 0, h)),
                  pl.BlockSpec((1, seq, bk), lambda b, h: (b, 0, k_blk + h)),
                  pl.BlockSpec((1, seq, bv), lambda b, h: (b, 0, v_blk + h)),
                  pl.BlockSpec((1, seq, bv), lambda b, h: (b, 0, g_blk + h)),
                  pl.BlockSpec((1, seq, LANES), lambda b, h: (b, 0, 0)),
                  pl.BlockSpec((hp, LANES, dk), lambda b, h: (h, 0, 0)),
                  pl.BlockSpec((1, bk), lambda b, h: (0, h)),
                  pl.BlockSpec((hp, LANES, dk), lambda b, h: (h, 0, 0)),
                  pl.BlockSpec((1, bk), lambda b, h: (0, h)),
                  pl.BlockSpec((1, dv), lambda b, h: (0, 0)),
                  pl.BlockSpec((1, hp, dk, dv), s0_map(s0f)),
                  pl.BlockSpec((1, hp, dk, dv), s0_map(s0b))] + cast_specs,
        out_specs=[pl.BlockSpec((1, seq, bv), lambda b, h: (b, 0, h)),
                   pl.BlockSpec((1, hp, dk, dv), lambda b, h: (b, h, 0, 0)),
                   pl.BlockSpec((1, hp, dk, dv), lambda b, h: (b, h, 0, 0))] + cast_specs,
        out_shape=[jax.ShapeDtypeStruct((batch, seq, width), BF16),
                   jax.ShapeDtypeStruct((batch, heads, dk, dv), F32),
                   jax.ShapeDtypeStruct((batch, heads, dk, dv), F32)] + cast_shapes,
        scratch_shapes=[pltpu.VMEM((hp, seq, dk), F32), pltpu.VMEM((hp, seq, dk), F32),
                        pltpu.VMEM((hp, seq, dk), BF16), pltpu.VMEM((hp, seq, dk), BF16),
                        pltpu.VMEM((hp, seq, dv), F32),
                        pltpu.VMEM((hp, n_chunks, dv, dk), F32), pltpu.VMEM((hp, n_chunks, dv, dk), F32),
                        pltpu.VMEM((hp, n_chunks, SUBLANES, dk), F32), pltpu.VMEM((hp, n_chunks, SUBLANES, dk), F32)],
        compiler_params=_cparams(("arbitrary", "arbitrary")),
        name="gla",
    )(proj3, proj3, proj3, proj3, lr3, wdf, bdf, wdb, bdb, g_gla, s0f, s0b, *cast_srcs)


def _filter_kernel(pe_ref, w1_ref, b1_ref, w2_ref, b2_ref, fr_ref, w3_ref, dl_ref, ad_ref, hl_ref, hm_ref, mlp_scr,
                   *, seq):
    @pl.when(pl.program_id(0) == 0)
    def _():
        fr = fr_ref[...]
        h1 = jnp.sin(fr * (_dot_hp(pe_ref[...], w1_ref[...]) + b1_ref[...]))
        mlp_scr[...] = jnp.sin(fr * (_dot_hp(h1, w2_ref[...]) + b2_ref[...]))

    h2 = _split2(mlp_scr[...])
    dec = jnp.exp(-pe_ref[:, 0:1] * dl_ref[...])
    row = lax.broadcasted_iota(jnp.int32, (seq, 1), 0)
    alt = jnp.where(row % 2 == 0, 1.0, -1.0)
    phase = row % 4
    cos_half = jnp.where(phase == 0, 1.0, jnp.where(phase == 2, -1.0, 0.0))
    sin_half = jnp.where(phase == 1, 1.0, jnp.where(phase == 3, -1.0, 0.0))
    for o in range(HY_ORDER):
        ff = _dot_hp(h2, w3_ref[o]) * dec
        fb = _dot_hp(h2, w3_ref[HY_ORDER + o]) * dec
        nrm = jnp.sum(jnp.abs(ff), axis=0, keepdims=True) + jnp.sum(jnp.abs(fb), axis=0, keepdims=True)
        inv = 1.0 / nrm
        ff = ff * inv
        fb = jnp.where(row == 0, 0.0, fb * inv)
        a = ff + fb
        d = fb - ff
        ad_ref[o, 0] = a.astype(ad_ref.dtype)
        ad_ref[o, 1] = d.astype(ad_ref.dtype)
        ad_ref[o, 2] = (alt * a).astype(ad_ref.dtype)
        ad_ref[o, 3] = (-alt * d).astype(ad_ref.dtype)
        hl_ref[o] = jnp.sum(alt * a, axis=0, keepdims=True)
        hm_ref[o] = jnp.concatenate([jnp.sum(cos_half * a, axis=0, keepdims=True),
                                     jnp.sum(sin_half * d, axis=0, keepdims=True)], axis=0) * (1.0 / seq)


def _hy_filter(pe, w1p, b1, w2, b2, freq, w3r, deltas, seq, width, tc):
    hid = w2.shape[0]
    kern = functools.partial(_filter_kernel, seq=seq)
    return pl.pallas_call(
        kern,
        grid=(width // tc,),
        in_specs=[pl.BlockSpec((seq, LANES), lambda c: (0, 0)),
                  pl.BlockSpec((LANES, hid), lambda c: (0, 0)),
                  pl.BlockSpec((1, hid), lambda c: (0, 0)),
                  pl.BlockSpec((hid, hid), lambda c: (0, 0)),
                  pl.BlockSpec((1, hid), lambda c: (0, 0)),
                  pl.BlockSpec((1, hid), lambda c: (0, 0)),
                  pl.BlockSpec((2 * HY_ORDER, hid, tc), lambda c: (0, 0, c)),
                  pl.BlockSpec((1, tc), lambda c: (0, c))],
        out_specs=[pl.BlockSpec((HY_ORDER, 4, seq, tc), lambda c: (0, 0, 0, c)),
                   pl.BlockSpec((HY_ORDER, 1, tc), lambda c: (0, 0, c)),
                   pl.BlockSpec((HY_ORDER, 2, tc), lambda c: (0, 0, c))],
        out_shape=[jax.ShapeDtypeStruct((HY_ORDER, 4, seq, width), BF16),
                   jax.ShapeDtypeStruct((HY_ORDER, 1, width), F32),
                   jax.ShapeDtypeStruct((HY_ORDER, 2, width), F32)],
        scratch_shapes=[pltpu.VMEM((seq, hid), F32)],
        compiler_params=_cparams(("arbitrary",)),
        name="hy_filter",
    )(pe, w1p, b1, w2, b2, freq, w3r, deltas)


def _spectrum_kernel(cs_ref, ad_ref, h_ref, *, seq, per_part, tr):
    r = pl.program_id(1)
    k = (r % per_part) * tr + lax.broadcasted_iota(jnp.int32, (tr, 1), 0)
    wgt = jnp.where(k == 0, 1.0, 2.0) * (0.5 / seq)
    h_ref[0] = wgt * _dot(cs_ref[0], ad_ref[0, 0])


def _hy_spectrum(cs3, ad, seq, part_rows, n_parts, width, tr, tc):
    per_part = part_rows // tr
    kern = functools.partial(_spectrum_kernel, seq=seq, per_part=per_part, tr=tr)
    return pl.pallas_call(
        kern,
        grid=(HY_ORDER, n_parts * per_part, width // tc),
        in_specs=[pl.BlockSpec((1, tr, seq), lambda o, r, c: ((r // per_part) % 2, r % per_part, 0)),
                  pl.BlockSpec((1, 1, seq, tc), lambda o, r, c: (o, r // per_part, 0, c))],
        out_specs=pl.BlockSpec((1, tr, tc), lambda o, r, c: (o, r, c)),
        out_shape=jax.ShapeDtypeStruct((HY_ORDER, n_parts * part_rows, width), F32),
        compiler_params=_cparams(("arbitrary", "arbitrary", "arbitrary")),
        name="hy_spectrum",
    )(cs3, ad)


def _short_conv_fn(seq, seg):
    row = lax.broadcasted_iota(jnp.int32, (seq, 1), 0)
    pos = row % seg

    def short_conv(u_ref, cw_ref, cb_ref, b=0):
        u = u_ref[b].astype(F32)
        prev = jnp.where(pos == 0, 0.0, pltpu.roll(u, 1, 0))
        nxt = jnp.where(pos == seg - 1, 0.0, pltpu.roll(u, seq - 1, 0))
        return prev * cw_ref[0:1, :] + u * cw_ref[1:2, :] + nxt * cw_ref[2:3, :] + cb_ref[...]

    return short_conv


def _hyconv_kernel(u0_ref, u1_ref, u2_ref, cw0_ref, cw1_ref, cw2_ref, cb0_ref, cb1_ref, cb2_ref,
                   cs_ref, h_ref, hl_ref, bias_ref, z_ref, y_scr, *, seq, seg):
    short_conv = _short_conv_fn(seq, seg)
    row = lax.broadcasted_iota(jnp.int32, (seq, 1), 0)
    alt = jnp.where(row % 2 == 0, 1.0, -1.0)
    gates = ((u1_ref, cw1_ref, cb1_ref), (u2_ref, cw2_ref, cb2_ref))
    items = range(z_ref.shape[0])
    z = [short_conv(u0_ref, cw0_ref, cb0_ref, b) for b in items]
    kb = min(seq, FREQ_BLOCK)
    for n in range(HY_ORDER):
        zb = [z[b].astype(BF16) for b in items]
        for r in range(0, seq, kb):
            xc = [_dot(cs_ref[r:r + kb, :], zb[b]) for b in items]
            xs = [_dot(cs_ref[seq + r:seq + r + kb, :], zb[b]) for b in items]
            hre = h_ref[n, r:r + kb, :]
            him = h_ref[n, seq + r:seq + r + kb, :]
            for b in items:
                y_scr[b, r:r + kb, :] = (xc[b] * hre + xs[b] * him).astype(BF16)
                y_scr[b, seq + r:seq + r + kb, :] = (xs[b] * hre - xc[b] * him).astype(BF16)
        nyq = [jnp.sum(alt * z[b], axis=0, keepdims=True) * (hl_ref[n] * (0.5 / seq)) for b in items]
        conv = [_dot(cs_ref[:seq, :], y_scr[b, :seq, :]) + _dot(cs_ref[seq:, :], y_scr[b, seq:, :]) + alt * nyq[b]
                for b in items]
        gate = [short_conv(*gates[n], b) for b in items]
        z = [gate[b] * (conv[b] + bias_ref[n] * z[b]) for b in items]
    for b in items:
        z_ref[b] = z[b].astype(z_ref.dtype)


def _hyconv_split_kernel(u0_ref, u1_ref, u2_ref, cw0_ref, cw1_ref, cw2_ref, cb0_ref, cb1_ref, cb2_ref,
                         cs_ref, h_ref, hm_ref, tw_ref, bias_ref, z_ref, g_scr, z_scr, c_scr, *, seq, seg):
    m = seq // 2
    tc = z_ref.shape[2]
    n_lane_blocks = tc // LANES
    short_conv = _short_conv_fn(seq, seg)
    row = lax.broadcasted_iota(jnp.int32, (m, 1), 0)
    alt = jnp.where(row % 2 == 0, 1.0, -1.0)
    gates = ((u1_ref, cw1_ref, cb1_ref), (u2_ref, cw2_ref, cb2_ref))
    z = short_conv(u0_ref, cw0_ref, cb0_ref)
    kb = min(m, FREQ_BLOCK)
    for n in range(HY_ORDER):
        for j in range(n_lane_blocks):
            z_scr[j] = z[:, j * LANES:(j + 1) * LANES]
        ze = jnp.concatenate([z_scr[j, pl.ds(0, m, stride=2), :] for j in range(n_lane_blocks)], axis=1)
        zo = jnp.concatenate([z_scr[j, pl.ds(1, m, stride=2), :] for j in range(n_lane_blocks)], axis=1)
        e_mid = jnp.sum(alt * ze, axis=0, keepdims=True)
        o_mid = jnp.sum(alt * zo, axis=0, keepdims=True)
        zeb = ze.astype(BF16)
        zob = zo.astype(BF16)
        for r in range(0, m, kb):
            rows = slice(r, r + kb)
            srows = slice(m + r, m + r + kb)
            ec = _dot(cs_ref[rows, :], zeb)
            es = _dot(cs_ref[srows, :], zeb)
            oc = _dot(cs_ref[rows, :], zob)
            os_ = _dot(cs_ref[srows, :], zob)
            c = tw_ref[0, rows, :]
            s = tw_ref[1, rows, :]
            pc = c * oc - s * os_
            ps = c * os_ + s * oc
            xca, xsa = ec + pc, es + ps
            xcb, xsb = ec - pc, ps - es
            har = h_ref[n, rows, :]
            hai = h_ref[n, srows, :]
            hbr = h_ref[n, 2 * m + r:2 * m + r + kb, :]
            hbi = h_ref[n, 3 * m + r:3 * m + r + kb, :]
            yar = xca * har + xsa * hai
            yai = xca * hai - xsa * har
            ybr = xcb * hbr + xsb * hbi
            ybi = xcb * hbi - xsb * hbr
            dr = yar - ybr
            di = yai + ybi
            g_scr[0, rows, :] = (yar + ybr).astype(BF16)
            g_scr[0, srows, :] = (ybi - yai).astype(BF16)
            g_scr[1, rows, :] = (c * dr - s * di).astype(BF16)
            g_scr[1, srows, :] = (-(s * dr + c * di)).astype(BF16)
        hr = hm_ref[n, 0:1, :]
        hi = hm_ref[n, 1:2, :]
        ymr = e_mid * hr + o_mid * hi
        ymi = e_mid * hi - o_mid * hr
        y_even = _dot(cs_ref[:m, :], g_scr[0, :m, :]) + _dot(cs_ref[m:, :], g_scr[0, m:, :]) + alt * ymr
        y_odd = _dot(cs_ref[:m, :], g_scr[1, :m, :]) + _dot(cs_ref[m:, :], g_scr[1, m:, :]) - alt * ymi
        for j in range(n_lane_blocks):
            c_scr[j, pl.ds(0, m, stride=2), :] = y_even[:, j * LANES:(j + 1) * LANES]
            c_scr[j, pl.ds(1, m, stride=2), :] = y_odd[:, j * LANES:(j + 1) * LANES]
        conv = jnp.concatenate([c_scr[j] for j in range(n_lane_blocks)], axis=1)
        z = short_conv(*gates[n]) * (conv + bias_ref[n] * z)
    z_ref[0] = z.astype(z_ref.dtype)


def _hy_conv(proj, conv_w, conv_b, cs, hspec, hl, hm, tw, bias, batch, seq, seg, col0, width, tc, split, bp,
             cast_srcs):
    proj3 = proj.reshape(batch, seq, proj.shape[-1])
    blk0 = col0 // tc
    per = width // tc
    n_b = batch // bp
    assert bp == 1 or not split
    once = dict(pipeline_mode=pl.Buffered(1))

    def u_spec(p):
        return pl.BlockSpec((bp, seq, tc), lambda c, b: (b, 0, blk0 + p * per + c))

    def w_spec(p, rows):
        return pl.BlockSpec((rows, tc), lambda c, b: (0, p * per + c))

    in_specs = [u_spec(0), u_spec(1), u_spec(2),
                w_spec(0, HY_SHORT), w_spec(1, HY_SHORT), w_spec(2, HY_SHORT),
                w_spec(0, 1), w_spec(1, 1), w_spec(2, 1),
                pl.BlockSpec(cs.shape, lambda c, b: (0, 0), **once),
                pl.BlockSpec((HY_ORDER, 2 * seq, tc), lambda c, b: (0, 0, c), **once)]
    bias_spec = pl.BlockSpec((HY_ORDER, 1, tc), lambda c, b: (0, 0, c))
    operands = [proj3, proj3, proj3, conv_w, conv_w, conv_w, conv_b, conv_b, conv_b, cs, hspec]
    if split:
        kern = functools.partial(_hyconv_split_kernel, seq=seq, seg=seg)
        in_specs += [pl.BlockSpec((HY_ORDER, 2, tc), lambda c, b: (0, 0, c)),
                     pl.BlockSpec((2, seq // 2, tc), lambda c, b: (0, 0, 0)), bias_spec]
        operands += [hm, tw, bias]
        scratch = [pltpu.VMEM((2, seq, tc), BF16), pltpu.VMEM((tc // LANES, seq, LANES), F32),
                   pltpu.VMEM((tc // LANES, seq, LANES), F32)]
    else:
        kern = functools.partial(_hyconv_kernel, seq=seq, seg=seg)
        in_specs += [pl.BlockSpec((HY_ORDER, 1, tc), lambda c, b: (0, 0, c)), bias_spec]
        operands += [hl, bias]
        scratch = [pltpu.VMEM((bp, 2 * seq, tc), BF16)]
    cast_specs, cast_shapes = _cast_rider_specs(cast_srcs, per * n_b, lambda c, b: c * n_b + b)
    kern = _with_cast_riders(kern, len(operands), 1, len(cast_srcs))
    return pl.pallas_call(
        kern,
        grid=(per, n_b),
        in_specs=in_specs + cast_specs,
        out_specs=[pl.BlockSpec((bp, seq, tc), lambda c, b: (b, 0, c))] + cast_specs,
        out_shape=[jax.ShapeDtypeStruct((batch, seq, width), BF16)] + cast_shapes,
        scratch_shapes=scratch,
        compiler_params=_cparams(("arbitrary", "arbitrary")),
        name="hy_conv",
    )(*operands, *cast_srcs)


def _mixout_kernel(*refs, gla_width, n_own, aliased):
    (yg_ref, zh_ref, x_ref, mod_ref, ghy_ref, gpost_ref, gpre_ref, wo_ref, wrh_ref, wrl_ref, br_ref,
     cnt_in_ref) = refs[:12]
    x1_ref, hg_ref, route_ref, cnt_ref, cnt_scr = refs[12 + int(aliased):]
    d = x_ref.shape[1]
    tm = x_ref.shape[0] // MIX_SUBTILES
    i = pl.program_id(0)

    @pl.when(i == 0)
    def _():
        cnt_scr[...] = cnt_in_ref[...]

    @pl.when(i < n_own)
    def _():
        for s in range(MIX_SUBTILES):
            _mixout_rows(slice(s * tm, (s + 1) * tm), tm, d, gla_width, yg_ref, zh_ref, x_ref, mod_ref, ghy_ref,
                         gpost_ref, gpre_ref, wo_ref, wrh_ref, wrl_ref, br_ref, x1_ref, hg_ref, route_ref, cnt_scr)

    @pl.when(i >= n_own)
    def _():
        hg_ref[...] = jnp.zeros_like(hg_ref)

    cnt_ref[...] = cnt_scr[...]


def _mixout_rows(rows, tm, d, gla_width, yg_ref, zh_ref, x_ref, mod_ref, ghy_ref, gpost_ref, gpre_ref, wo_ref,
                 wrh_ref, wrl_ref, br_ref, x1_ref, hg_ref, route_ref, cnt_scr):
    yh = _rms(zh_ref[rows, :].astype(F32), ghy_ref[...]).astype(BF16)
    y = _dot(yg_ref[rows, :], wo_ref[:gla_width, :]) + _dot(yh, wo_ref[gla_width:, :])
    x1 = x_ref[rows, :] + _rms(y, gpost_ref[...] * mod_ref[0, 2:3, :])
    x1_ref[rows, :] = x1
    h2 = _rms(x1, gpre_ref[...] * (1.0 + mod_ref[0, 4:5, :])) + mod_ref[0, 3:4, :]
    hh, hl = _split2(h2)
    hg_ref[rows, :d] = h2
    logits = _dot(hh, wrh_ref[...]) + (_dot(hh, wrl_ref[...]) + _dot(hl, wrh_ref[...])) + br_ref[...]

    lane = lax.broadcasted_iota(jnp.int32, logits.shape, 1).astype(F32)
    neg = -jnp.inf
    is_grp = (lane >= N_EXPERTS) & (lane < N_EXPERTS + N_GROUPS)
    m = jnp.max(jnp.where(is_grp, logits, neg), axis=1, keepdims=True)
    p_grp = 1.0 / jnp.sum(jnp.where(is_grp, jnp.exp(logits - m), 0.0), axis=1, keepdims=True)
    grp = jnp.min(jnp.where(is_grp & (logits == m), lane - N_EXPERTS, 1e9), axis=1, keepdims=True)
    sel = (lane >= grp * EXP_PER_GROUP) & (lane < (grp + 1.0) * EXP_PER_GROUP)
    me = jnp.max(jnp.where(sel, logits, neg), axis=1, keepdims=True)
    pe = jnp.where(sel, jnp.exp(logits - me), -1.0)
    v1 = jnp.max(pe, axis=1, keepdims=True)
    i1 = jnp.min(jnp.where(pe == v1, lane, 1e9), axis=1, keepdims=True)
    pe2 = jnp.where(lane == i1, -1.0, pe)
    v2 = jnp.max(pe2, axis=1, keepdims=True)
    i2 = jnp.min(jnp.where(pe2 == v2, lane, 1e9), axis=1, keepdims=True)
    den = v1 + v2
    gates = jnp.where(lane == i1, v1 / den, jnp.where(lane == i2, v2 / den, 0.0)) * p_grp

    lo = jnp.minimum(i1, i2) - grp * EXP_PER_GROUP
    hi = jnp.maximum(i1, i2) - grp * EXP_PER_GROUP
    bucket = grp * PAIRS_PER_GROUP + lo * (2 * EXP_PER_GROUP - 1 - lo) * 0.5 + (hi - lo - 1.0)
    onehot = lane == bucket
    r_i = lax.broadcasted_iota(jnp.int32, (tm, tm), 0)
    c_i = lax.broadcasted_iota(jnp.int32, (tm, tm), 1)
    earlier = _dot((r_i > c_i).astype(BF16), onehot.astype(BF16)) + cnt_scr[...]
    rank = jnp.sum(jnp.where(onehot, earlier, 0.0), axis=1, keepdims=True)
    cnt_scr[...] += jnp.sum(onehot.astype(F32), axis=0, keepdims=True)
    route = jnp.where(lane == ROUTE_BUCKET_LANE, bucket, jnp.where(lane == ROUTE_RANK_LANE, rank, gates))
    hg_ref[rows, d:] = route
    route_ref[:, rows] = route.T[ROUTE_BUCKET_LANE:ROUTE_BUCKET_LANE + SUBLANES, :]


def _mix_out(yg, zh, x, mod, mod_row, g_hy, g_post, g_pre, w_out, wr_hi, wr_lo, b_r, cnt_in, hg_all, row0, n_total, tm):
    n, d = x.shape
    gw = yg.shape[1]
    hw = zh.shape[1]
    aliased = hg_all is not None
    n_own = n // tm
    n_steps = n_own if aliased else n_total // tm
    assert row0 % tm == 0 and n_total % tm == 0 and (row0 + n == n_total if aliased else row0 == 0)
    kern = functools.partial(_mixout_kernel, gla_width=gw, n_own=n_own, aliased=aliased)
    row = lambda i: (jnp.minimum(i, n_own - 1), 0)
    fixed = lambda i: (0, 0)
    in_specs = [pl.BlockSpec((tm, gw), row),
                pl.BlockSpec((tm, hw), row),
                pl.BlockSpec((tm, d), row),
                pl.BlockSpec((1, 6, d), lambda i: (mod_row(jnp.minimum(i, n_own - 1) * tm), 0, 0)),
                pl.BlockSpec((1, hw), fixed),
                pl.BlockSpec((1, d), fixed),
                pl.BlockSpec((1, d), fixed),
                pl.BlockSpec((gw + hw, d), fixed),
                pl.BlockSpec((d, LANES), fixed),
                pl.BlockSpec((d, LANES), fixed),
                pl.BlockSpec((1, LANES), fixed),
                pl.BlockSpec((1, LANES), fixed)]
    operands = [yg, zh, x, mod, g_hy, g_post, g_pre, w_out, wr_hi, wr_lo, b_r, cnt_in]
    if aliased:
        in_specs.append(pl.BlockSpec(memory_space=pl.ANY))
        operands.append(hg_all)
    return pl.pallas_call(
        kern,
        grid=(n_steps,),
        in_specs=in_specs,
        out_specs=[pl.BlockSpec((tm, d), row),
                   pl.BlockSpec((tm, d + LANES), lambda i: (row0 // tm + i, 0)),
                   pl.BlockSpec((SUBLANES, tm), lambda i: (0, jnp.minimum(i, n_own - 1))),
                   pl.BlockSpec((1, LANES), fixed)],
        out_shape=[jax.ShapeDtypeStruct((n, d), F32),
                   jax.ShapeDtypeStruct((n_total, d + LANES), F32),
                   jax.ShapeDtypeStruct((SUBLANES, n), F32),
                   jax.ShapeDtypeStruct((1, LANES), F32)],
        scratch_shapes=[pltpu.VMEM((1, LANES), F32)],
        input_output_aliases={len(operands) - 1: 1} if aliased else {},
        compiler_params=_cparams(("arbitrary",)),
        name="mix_out",
    )(*operands)


def _row_gather_copy(src_hbm, row, buf, slot, r, sem):
    return pltpu.make_async_copy(src_hbm.at[pl.ds(row, 1), :], buf.at[slot, pl.ds(r, 1), :], sem.at[slot])


def _start_row_gather(idx_ref, base, src_hbm, buf, slot, sem, tm):
    def body(r, carry):
        _row_gather_copy(src_hbm, idx_ref[base + r], buf, slot, r, sem).start()
        return carry

    lax.fori_loop(0, tm, body, 0, unroll=8)


def _start_row_gather_inline(idx_ref, base, src_hbm, buf, slot, sem, tm):
    for r in range(tm):
        _row_gather_copy(src_hbm, idx_ref[base + r], buf, slot, r, sem).start()


def _wait_row_gather(src_hbm, buf, slot, sem, tm):
    pltpu.make_async_copy(src_hbm.at[pl.ds(0, tm), :], buf.at[slot], sem.at[slot]).wait()


def _moe_kernel(src_ref, ea_ref, eb_ref, valid_ref, hg_hbm,
                wga_ref, wua_ref, wda_ref, wgb_ref, wub_ref, wdb_ref, y_ref, buf, sem, *, tm, d):
    t = pl.program_id(0)
    slot = t % 2

    @pl.when(t == 0)
    def _():
        _start_row_gather(src_ref, 0, hg_hbm, buf, 0, sem, tm)

    def used_tile(slot):
        _wait_row_gather(hg_hbm, buf, slot, sem, tm)
        _start_row_gather_inline(src_ref, (t + 1) * tm, hg_hbm, buf, 1 - slot, sem, tm)
        h = buf[slot, :, :d].astype(BF16)
        gates = buf[slot, :, d:]
        lane = lax.broadcasted_iota(jnp.int32, gates.shape, 1)

        experts = ((wga_ref, wua_ref, wda_ref, ea_ref[t]), (wgb_ref, wub_ref, wdb_ref, eb_ref[t]))
        pre = [_dot(h, wg_ref[0]) for wg_ref, _, _, _ in experts]
        up = [_dot(h, wu_ref[0]) for _, wu_ref, _, _ in experts]
        he = [(_silu(pre[i]) * up[i]).astype(BF16) for i in range(2)]
        gate = [jnp.sum(jnp.where(lane == e, gates, 0.0), axis=1, keepdims=True) for _, _, _, e in experts]
        out = [gate[i] * _dot(he[i], experts[i][2][0]) for i in range(2)]
        y_ref[...] = out[0] + out[1]

    for parity in range(2):
        pl.when((valid_ref[t] == 1) & (slot == parity))(functools.partial(used_tile, parity))

    @pl.when(valid_ref[t] == 0)
    def _():
        @pl.when(valid_ref[jnp.maximum(t - 1, 0)] == 1)
        def _():
            _wait_row_gather(hg_hbm, buf, slot, sem, tm)

        y_ref[...] = jnp.zeros_like(y_ref)


def _moe(hg, src, ea, eb, valid, wg, wu, wd, n_tiles, tm):
    d = hg.shape[1] - LANES
    _, _, de = wg.shape
    kern = functools.partial(_moe_kernel, tm=tm, d=d)
    wa = lambda t, src, ea, eb, valid: (ea[t], 0, 0)
    wb = lambda t, src, ea, eb, valid: (eb[t], 0, 0)
    return pl.pallas_call(
        kern,
        grid_spec=pltpu.PrefetchScalarGridSpec(
            num_scalar_prefetch=4,
            grid=(n_tiles,),
            in_specs=[pl.BlockSpec(memory_space=pl.ANY),
                      pl.BlockSpec((1, d, de), wa), pl.BlockSpec((1, d, de), wa), pl.BlockSpec((1, de, d), wa),
                      pl.BlockSpec((1, d, de), wb), pl.BlockSpec((1, d, de), wb), pl.BlockSpec((1, de, d), wb)],
            out_specs=pl.BlockSpec((tm, d), lambda t, src, ea, eb, valid: (t, 0)),
            scratch_shapes=[pltpu.VMEM((2, tm, d + LANES), F32), pltpu.SemaphoreType.DMA((2,))]),
        out_shape=jax.ShapeDtypeStruct((n_tiles * tm, d), F32),
        compiler_params=_cparams(("arbitrary",)),
        name="moe",
    )(src, ea, eb, valid, hg, wg, wu, wd, wg, wu, wd)


def _ffn_out_kernel(pos_ref, y_hbm, x1_ref, mod_ref, gpost_ref, o_ref, buf, sem, *, tm):
    i = pl.program_id(0)
    n_i = pl.num_programs(0)
    slot = i % 2

    @pl.when(i == 0)
    def _():
        _start_row_gather(pos_ref, 0, y_hbm, buf, 0, sem, tm)

    _wait_row_gather(y_hbm, buf, slot, sem, tm)

    @pl.when(i + 1 < n_i)
    def _():
        _start_row_gather_inline(pos_ref, (i + 1) * tm, y_hbm, buf, 1 - slot, sem, tm)
        o_ref[...] = x1_ref[...] + _rms(buf[slot], gpost_ref[...] * mod_ref[0, 5:6, :])

    @pl.when(i + 1 == n_i)
    def _():
        o_ref[...] = x1_ref[...] + _rms(buf[slot], gpost_ref[...] * mod_ref[0, 5:6, :])


def _ffn_out(y_sorted, pos, x1, mod, mod_row, g_post, tm):
    n, d = x1.shape
    kern = functools.partial(_ffn_out_kernel, tm=tm)
    return pl.pallas_call(
        kern,
        grid_spec=pltpu.PrefetchScalarGridSpec(
            num_scalar_prefetch=1,
            grid=(n // tm,),
            in_specs=[pl.BlockSpec(memory_space=pl.ANY),
                      pl.BlockSpec((tm, d), lambda i, pos: (i, 0)),
                      pl.BlockSpec((1, 6, d), lambda i, pos: (mod_row(i * tm), 0, 0)),
                      pl.BlockSpec((1, d), lambda i, pos: (0, 0))],
            out_specs=pl.BlockSpec((tm, d), lambda i, pos: (i, 0)),
            scratch_shapes=[pltpu.VMEM((2, tm, d), F32), pltpu.SemaphoreType.DMA((2,))]),
        out_shape=jax.ShapeDtypeStruct((n, d), F32),
        compiler_params=_cparams(("arbitrary",)),
        name="ffn_out",
    )(pos, y_sorted, x1, mod, g_post)


def _route_tables(route, counts, n_tiles, tm):
    n = route.shape[1]
    n_buckets = N_GROUPS * PAIRS_PER_GROUP
    cnt = counts[0, :n_buckets].astype(jnp.int32)
    padded = (cnt + tm - 1) // tm * tm
    ends = jnp.cumsum(padded)
    starts = ends - padded
    bucket = route[0].astype(jnp.int32)
    rank = route[ROUTE_RANK_LANE - ROUTE_BUCKET_LANE].astype(jnp.int32)
    pos = starts[bucket] + rank
    src = (jnp.arange(n_tiles * tm, dtype=jnp.int32) % n).at[pos].set(jnp.arange(n, dtype=jnp.int32))
    n_valid = ends[-1] // tm
    tile = jnp.arange(n_tiles, dtype=jnp.int32)
    used = jnp.minimum(tile, n_valid - 1)
    tile_bucket = jnp.sum((ends[None, :] <= (used * tm)[:, None]).astype(jnp.int32), axis=1)
    grp = tile_bucket // PAIRS_PER_GROUP
    pair = tile_bucket % PAIRS_PER_GROUP
    pair_lo = jnp.array([a for a in range(EXP_PER_GROUP) for b in range(a + 1, EXP_PER_GROUP)], jnp.int32)
    pair_hi = jnp.array([b for a in range(EXP_PER_GROUP) for b in range(a + 1, EXP_PER_GROUP)], jnp.int32)
    ea = grp * EXP_PER_GROUP + pair_lo[pair]
    eb = grp * EXP_PER_GROUP + pair_hi[pair]
    valid = (tile < n_valid).astype(jnp.int32)
    return pos, src, ea, eb, valid


def _dft_kernel(cs_ref, *, seq, tr):
    k = pl.program_id(0) * tr + lax.broadcasted_iota(jnp.int32, (tr, LANES), 0)
    j = lax.broadcasted_iota(jnp.int32, (tr, LANES), 1)
    period = 2 * seq

    ang = ((k * j) % period).astype(F32) * (math.pi / seq)
    c0 = jnp.cos(ang)
    s0 = jnp.sin(ang)
    ch = c0[:, LANES // 2:LANES // 2 + 1]
    sh = s0[:, LANES // 2:LANES // 2 + 1]
    c_step = 2.0 * ch * ch - 1.0
    s_step = 2.0 * sh * ch
    ca = jnp.ones_like(ch)
    sa = jnp.zeros_like(ch)
    for t1 in range(seq // LANES):
        cols = slice(t1 * LANES, (t1 + 1) * LANES)
        cs_ref[0, :, cols] = (ca * c0 - sa * s0).astype(cs_ref.dtype)
        cs_ref[1, :, cols] = (sa * c0 + ca * s0).astype(cs_ref.dtype)
        ca, sa = ca * c_step - sa * s_step, sa * c_step + ca * s_step


def _dft_cos_sin(seq, n_rows, tr):
    kern = functools.partial(_dft_kernel, seq=seq, tr=tr)
    return pl.pallas_call(
        kern,
        grid=(n_rows // tr,),
        out_specs=pl.BlockSpec((2, tr, seq), lambda r: (0, r, 0)),
        out_shape=jax.ShapeDtypeStruct((2, n_rows, seq), BF16),
        compiler_params=_cparams(("arbitrary",)),
        name="dft_matrix",
    )()


def _positional_features(seq):
    t = jnp.arange(seq, dtype=F32)
    t01 = t / max(seq - 1, 1)
    ang = 2.0 * math.pi * t / seq
    bands = jnp.linspace(1e-4, HY_BANDS - 1, HY_BANDS, dtype=F32)
    pe = jnp.concatenate([t01[:, None], jnp.cos(ang[:, None] * bands), -jnp.sin(ang[:, None] * bands)], axis=-1)
    return jnp.pad(pe, ((0, 0), (0, LANES - pe.shape[1])))


def _mixer_and_router(x3, mod, mod_row, s0f, s0b, n_seg, p, tiles, cnt_in, hg_all, row0, n_total, w_out, casts):
    batch, seq, d = x3.shape
    x = x3.reshape(batch * seq, d)
    dk, dv = p["dk"], p["dv"]
    hy_width = p["hy_width"]
    casted = {}

    proj, lr, *casted["premix"] = _premix_proj(x, mod, mod_row, p["g_pre_mix"], p["w_main"], p["w_lr"],
                                               tiles["tm_proj"], tiles["tn_proj"], casts["premix"])
    if w_out is None:
        w_out = casted["premix"][0]
    y_gla, s_f, s_b, *casted["gla"] = _gla(proj, lr, p["wdf"], p["bdf"], p["wdb"], p["bdb"], p["g_gla"], s0f, s0b,
                                           batch, seq, dk, dv, tiles["gla_heads_per_step"], casts["gla"])

    ad, hl, hm = _hy_filter(_positional_features(seq), p["hy_w1"], p["hy_b1"], p["hy_w2"], p["hy_b2"], p["hy_freq"],
                            p["hy_w3"], p["hy_deltas"], seq, hy_width, tiles["tc_filter"])
    tr, tc_conv = tiles["tr_spec"], tiles["tc_conv"]
    if tiles["hy_split"]:
        half = seq // 2
        hspec = _hy_spectrum(_dft_cos_sin(seq, half, tr), ad, seq, half, 4, hy_width, tr, tiles["tc_spec"])
        cs = _dft_cos_sin(half, half, tr).reshape(seq, half)
        ang = jnp.arange(half, dtype=F32) * (math.pi / seq)
        tw = jnp.broadcast_to(jnp.stack([jnp.cos(ang), jnp.sin(ang)])[:, :, None], (2, half, tc_conv))
    else:
        cs3 = _dft_cos_sin(seq, seq, tr)
        hspec = _hy_spectrum(cs3, ad, seq, seq, 2, hy_width, tr, tiles["tc_spec"])
        cs = cs3.reshape(2 * seq, seq)
        tw = None
    z_hy, *casted["hy_conv"] = _hy_conv(proj, p["hy_conv_w"], p["hy_conv_b"], cs, hspec, hl, hm, tw, p["hy_bias"],
                                        batch, seq, seq // n_seg, p["hy_col0"], hy_width, tc_conv, tiles["hy_split"],
                                        tiles["hy_batch_per_step"], casts["hy_conv"])

    x1, hg_all, route, counts = _mix_out(y_gla.reshape(batch * seq, -1), z_hy.reshape(batch * seq, -1), x, mod,
                                         mod_row, p["g_hy"], p["g_post_mix"], p["g_pre_ffn"], w_out, p["wr_hi"],
                                         p["wr_lo"], p["b_r"], cnt_in, hg_all, row0, n_total, tiles["tm_mix"])
    return x1, hg_all, route, counts, s_f, s_b, casted


def kernel(x_prompt, x_sample, c, state_gla_fwd, state_gla_bwd, c_ctx, w_ada, b_ada, g_pre_mix, g_post_mix, g_pre_ffn, g_post_ffn, w_in, w_dec_f, b_dec_f, w_dec_b, b_dec_b, g_gla, hy_conv_w, hy_conv_b, hy_w1, hy_b1, hy_w2, hy_b2, hy_w3, hy_freq, hy_bias, g_hy, w_out, w_router_grp, b_router_grp, w_router_exp, b_router_exp, w_exp_gate, w_exp_up, w_exp_down):
    depth = w_ada.shape[0]
    assert depth == 1
    l = 0
    d = x_prompt.shape[-1]
    dec_batch = x_sample.shape[0]
    heads = GLA_HEADS
    dk, dv = state_gla_fwd.shape[-2:]
    kdim = heads * dk
    gla_width = heads * dv
    hy_width = g_hy.shape[-1]
    hid = hy_w2.shape[-1]

    cond = jnp.concatenate([c_ctx[None, :], c], axis=0)
    cond = jnp.pad(cond, ((0, -cond.shape[0] % SUBLANES), (0, 0)))
    mod = _ada_mod(cond, w_ada[l], b_ada[l]).reshape(cond.shape[0], 6, d)

    n_main = 2 * kdim + 2 * gla_width
    w_main, w_lr = _repack_w_in(w_in, l, n_main, 2 * GLA_RANK, REPACK_COLS)

    def dec_weight(w, first_row):
        wh = w.reshape(GLA_RANK, heads, dk).transpose(1, 0, 2)
        return jnp.pad(wh, ((0, 0), (first_row, LANES - GLA_RANK - first_row), (0, 0)))

    deltas = jnp.abs(jnp.linspace(math.log(HY_TARGET) / HY_SLOW_PCT, math.log(HY_TARGET) / HY_FAST_PCT, hy_width,
                                  dtype=F32))
    w_r = jnp.pad(jnp.concatenate([w_router_exp[l], w_router_grp[l]], axis=1),
                  ((0, 0), (0, LANES - N_EXPERTS - N_GROUPS)))
    wr_hi, wr_lo = _split2(w_r)
    b_r = jnp.pad(jnp.concatenate([b_router_exp[l], b_router_grp[l]]), (0, LANES - N_EXPERTS - N_GROUPS))

    p = dict(
        dk=dk, dv=dv, hy_width=hy_width, hy_col0=n_main,
        g_pre_mix=g_pre_mix[l][None, :], g_post_mix=g_post_mix[l][None, :],
        g_pre_ffn=g_pre_ffn[l][None, :], g_post_ffn=g_post_ffn[l][None, :],
        w_main=w_main, w_lr=w_lr,
        wdf=dec_weight(w_dec_f[l], 0), bdf=b_dec_f[l][None, :],
        wdb=dec_weight(w_dec_b[l], GLA_RANK), bdb=b_dec_b[l][None, :],
        g_gla=g_gla[l][None, :],
        hy_conv_w=hy_conv_w[l], hy_conv_b=hy_conv_b[l][None, :],
        hy_w1=jnp.pad(hy_w1[l], ((0, LANES - hy_w1.shape[1]), (0, 0))), hy_b1=hy_b1[l][None, :],
        hy_w2=hy_w2[l], hy_b2=hy_b2[l][None, :], hy_freq=hy_freq[l][None, :],
        hy_w3=hy_w3[l].reshape(hid, 2 * HY_ORDER, hy_width).transpose(1, 0, 2), hy_deltas=deltas[None, :],
        hy_bias=hy_bias[l][:, None, :], g_hy=g_hy[l][None, :],
        wr_hi=wr_hi, wr_lo=wr_lo, b_r=b_r[None, :],
    )

    zero_state = jnp.zeros((1, heads, dk, dv), F32)
    dec_seq = x_sample.shape[1]
    tiles_p = _tile_plan(x_prompt.shape[1])
    tiles_s = _tile_plan(dec_seq)
    tm_moe = MXU_DIM
    tm_out = 512

    n_p = x_prompt.shape[0] * x_prompt.shape[1]
    n_s = dec_batch * dec_seq
    n_all = n_p + n_s
    mod_row_p = lambda r: 0
    mod_row_s = lambda r: 1 + r // dec_seq
    n_exp, _, d_exp = w_exp_gate.shape[1:]
    casts_p = dict(premix=[w_out[l]],
                   gla=[w_exp_gate[l].reshape(n_exp * d, d_exp), w_exp_down[l].reshape(n_exp * d_exp, d)],
                   hy_conv=[w_exp_up[l].reshape(n_exp * d, d_exp)])
    casts_s = dict(premix=[], gla=[], hy_conv=[])
    x1_p, hg_all, route_p, counts, s_f, s_b, casted = _mixer_and_router(
        x_prompt, mod, mod_row_p, zero_state, zero_state, 1, p, tiles_p, jnp.zeros((1, LANES), F32), None, 0, n_all,
        None, casts_p)
    (w_out_b,), (w_gate_b, w_down_b), (w_up_b,) = casted["premix"], casted["gla"], casted["hy_conv"]
    x1_s, hg_all, route_s, counts, _, _, _ = _mixer_and_router(
        x_sample, mod, mod_row_s, state_gla_fwd[:, l], state_gla_bwd[:, l], dec_seq // GRID_W, p, tiles_s,
        counts, hg_all, n_p, n_all, w_out_b, casts_s)

    n_tiles = n_all // tm_moe + N_GROUPS * PAIRS_PER_GROUP
    pos, src, ea, eb, valid = _route_tables(jnp.concatenate([route_p, route_s], axis=1), counts, n_tiles, tm_moe)
    y_sorted = _moe(hg_all, src, ea, eb, valid, w_gate_b.reshape(n_exp, d, d_exp), w_up_b.reshape(n_exp, d, d_exp),
                    w_down_b.reshape(n_exp, d_exp, d), n_tiles, tm_moe)
    y_p = _ffn_out(y_sorted, pos[:n_p], x1_p, mod, mod_row_p, p["g_post_ffn"], tm_out)
    y_s = _ffn_out(y_sorted, pos[n_p:], x1_s, mod, mod_row_s, p["g_post_ffn"], tm_out)
    return (y_p.reshape(x_prompt.shape), y_s.reshape(x_sample.shape),
            s_f[:, None].astype(x_prompt.dtype), s_b[:, None].astype(x_prompt.dtype))
```

```python
import functools
import math

import jax
import jax.numpy as jnp
from jax import lax
from jax.experimental import pallas as pl
from jax.experimental.pallas import tpu as pltpu

F32 = jnp.float32
BF16 = jnp.bfloat16

GRID_W = 64
GLA_HEADS = 4
GLA_RANK = 16
GLA_TAU = 16.0
GLA_CHUNK = 64
GLA_UNROLL = 4
MIX_SUBTILES = 1
HY_ORDER = 2
HY_SHORT = 3
HY_BANDS = 16
HY_TARGET = 1e-2
HY_FAST_PCT = 0.3
HY_SLOW_PCT = 1.5
N_GROUPS = 4
EXP_PER_GROUP = 4
N_EXPERTS = N_GROUPS * EXP_PER_GROUP
PAIRS_PER_GROUP = EXP_PER_GROUP * (EXP_PER_GROUP - 1) // 2
ROUTE_BUCKET_LANE = N_EXPERTS
ROUTE_RANK_LANE = N_EXPERTS + 1
EPS = 1e-6

LANES = 128
SUBLANES = 8
MXU_DIM = 256
VMEM_LIMIT = 56 << 20
REPACK_COLS = MXU_DIM
FREQ_BLOCK = 2 * MXU_DIM


def _tile_plan(seq):
    long_seq = seq >= 1024
    return dict(
        tm_proj=1024, tn_proj=2048,
        tc_filter=128 if long_seq else 256,
        tr_spec=512 if long_seq else 256,
        tc_spec=1024 if long_seq else 512,
        hy_split=long_seq,
        tc_conv=256 if long_seq else 512,
        hy_batch_per_step=1 if long_seq else 2,
        gla_heads_per_step=2 if long_seq else 4,
        tm_mix=512,
    )


def _cparams(sem):
    return pltpu.CompilerParams(dimension_semantics=sem, vmem_limit_bytes=VMEM_LIMIT)


def _dot(a, b):
    return jnp.dot(a, b, preferred_element_type=F32)


def _dot_nt(a, b):
    return lax.dot_general(a, b, (((1,), (1,)), ((), ())), preferred_element_type=F32)


def _dot_tn(a, b):
    return lax.dot_general(a, b, (((0,), (0,)), ((), ())), preferred_element_type=F32)


def _split2(x):
    hi = x.astype(BF16)
    lo = (x - hi.astype(F32)).astype(BF16)
    return hi, lo


def _dot_hp(a, b):
    ah, al = a if isinstance(a, tuple) else _split2(a)
    bh, bl = _split2(b)
    return _dot(ah, bh) + (_dot(ah, bl) + _dot(al, bh))


def _dot_exact_lhs(t, x):
    hi, lo = _split2(x)
    return _dot(t, hi) + _dot(t, lo)


def _rms(x, g):
    return x * lax.rsqrt(jnp.mean(x * x, axis=-1, keepdims=True) + EPS) * g


def _silu(x):
    return x / (1.0 + jnp.exp(-x))


def _ada_kernel(c_ref, w_ref, b_ref, o_ref):
    s = _silu(c_ref[...]).astype(BF16)
    o_ref[...] = _dot(s, w_ref[...].astype(BF16)) + b_ref[...]


def _ada_mod(cond, w_ada, b_ada):
    rows, d = cond.shape
    n = w_ada.shape[1]
    tn = 2048
    return pl.pallas_call(
        _ada_kernel,
        grid=(n // tn,),
        in_specs=[pl.BlockSpec((rows, d), lambda j: (0, 0)),
                  pl.BlockSpec((d, tn), lambda j: (0, j)),
                  pl.BlockSpec((1, tn), lambda j: (0, j))],
        out_specs=pl.BlockSpec((rows, tn), lambda j: (0, j)),
        out_shape=jax.ShapeDtypeStruct((rows, n), F32),
        compiler_params=_cparams(("arbitrary",)),
        name="ada_mod",
    )(cond, w_ada, b_ada.reshape(1, n))


def _repack_kernel(wt_hbm, main_ref, lr_ref, buf, lr_buf, sem, lr_sem, *, layer, n_main, n_lr, tb):
    i = pl.program_id(0)
    n_i = pl.num_programs(0)
    slot = i % 2
    d = main_ref.shape[0]

    def block_copy(j, s):
        first = j * tb + jnp.where(j * tb >= n_main, n_lr, 0)
        return pltpu.make_async_copy(wt_hbm.at[layer, pl.ds(first, tb), :], buf.at[s], sem.at[s])

    lr_copy = pltpu.make_async_copy(wt_hbm.at[layer, pl.ds(n_main, n_lr), :], lr_buf, lr_sem)

    @pl.when(i == 0)
    def _():
        block_copy(0, 0).start()
        lr_copy.start()

    @pl.when(i + 1 < n_i)
    def _():
        block_copy(i + 1, 1 - slot).start()

    block_copy(i, slot).wait()
    main_ref[...] = buf[slot].T.astype(BF16)

    @pl.when(i == 0)
    def _():
        lr_copy.wait()
        lr = jnp.concatenate([lr_buf[...], jnp.zeros((LANES - n_lr, d), F32)], axis=0).T
        lr_ref[...] = lr.astype(BF16)


def _repack_w_in(w_in, layer, n_main, n_lr, tb):
    w_t = jnp.swapaxes(w_in, 1, 2)
    _, n_cols, d = w_t.shape
    n_out = n_cols - n_lr
    assert n_main % tb == 0 and n_out % tb == 0
    kern = functools.partial(_repack_kernel, layer=layer, n_main=n_main, n_lr=n_lr, tb=tb)
    return pl.pallas_call(
        kern,
        grid=(n_out // tb,),
        in_specs=[pl.BlockSpec(memory_space=pl.ANY)],
        out_specs=[pl.BlockSpec((d, tb), lambda i: (0, i)),
                   pl.BlockSpec((d, LANES), lambda i: (0, 0))],
        out_shape=[jax.ShapeDtypeStruct((d, n_out), BF16),
                   jax.ShapeDtypeStruct((d, LANES), BF16)],
        scratch_shapes=[pltpu.VMEM((2, tb, d), F32), pltpu.VMEM((n_lr, d), F32),
                        pltpu.SemaphoreType.DMA((2,)), pltpu.SemaphoreType.DMA(())],
        compiler_params=_cparams(("arbitrary",)),
        name="repack_w_in",
    )(w_t)


def _with_cast_riders(kernel_fn, n_in, n_out, n_cast):
    def kernel(*refs):
        ins = refs[:n_in]
        cast_in = refs[n_in:n_in + n_cast]
        outs = refs[n_in + n_cast:n_in + n_cast + n_out]
        cast_out = refs[n_in + n_cast + n_out:n_in + 2 * n_cast + n_out]
        kernel_fn(*ins, *outs, *refs[n_in + 2 * n_cast + n_out:])
        for src, dst in zip(cast_in, cast_out):
            dst[...] = src[...].astype(dst.dtype)

    return kernel


def _cast_rider_specs(cast_srcs, n_steps, step_of):
    n_slabs = 1 << (n_steps.bit_length() - 1)
    specs = []
    for w in cast_srcs:
        rows, cols = w.shape
        assert rows % n_slabs == 0
        specs.append(pl.BlockSpec((rows // n_slabs, cols),
                                  lambda *idx: (jnp.minimum(step_of(*idx), n_slabs - 1), 0)))
    return specs, [jax.ShapeDtypeStruct(w.shape, BF16) for w in cast_srcs]


def _premix_kernel(x_ref, mod_ref, g_ref, w_ref, wlr_ref, o_ref, lr_ref, h_scr):
    @pl.when(pl.program_id(1) == 0)
    def _():
        h = _rms(x_ref[...], g_ref[...] * (1.0 + mod_ref[0, 1:2, :])) + mod_ref[0, 0:1, :]
        h_scr[...] = h.astype(BF16)
        lr_ref[...] = _dot(h_scr[...], wlr_ref[...])

    o_ref[...] = _dot(h_scr[...], w_ref[...]).astype(o_ref.dtype)


def _premix_proj(x, mod, mod_row, g, w_main, w_lr, tm, tn, cast_srcs):
    n, d = x.shape
    nc = w_main.shape[1]
    n_j = nc // tn
    cast_specs, cast_shapes = _cast_rider_specs(cast_srcs, (n // tm) * n_j, lambda i, j: i * n_j + j)
    kern = _with_cast_riders(_premix_kernel, 5, 2, len(cast_srcs))
    return pl.pallas_call(
        kern,
        grid=(n // tm, n_j),
        in_specs=[pl.BlockSpec((tm, d), lambda i, j: (i, 0)),
                  pl.BlockSpec((1, 6, d), lambda i, j: (mod_row(i * tm), 0, 0)),
                  pl.BlockSpec((1, d), lambda i, j: (0, 0)),
                  pl.BlockSpec((d, tn), lambda i, j: (0, j)),
                  pl.BlockSpec((d, LANES), lambda i, j: (0, 0))] + cast_specs,
        out_specs=[pl.BlockSpec((tm, tn), lambda i, j: (i, j)),
                   pl.BlockSpec((tm, LANES), lambda i, j: (i, 0))] + cast_specs,
        out_shape=[jax.ShapeDtypeStruct((n, nc), BF16), jax.ShapeDtypeStruct((n, LANES), F32)] + cast_shapes,
        scratch_shapes=[pltpu.VMEM((tm, d), BF16)],
        compiler_params=_cparams(("arbitrary", "arbitrary")),
        name="premix_proj",
    )(x, mod, g, w_main, w_lr, *cast_srcs)


def _log_sigmoid(x):
    return jnp.minimum(x, 0.0) - jnp.log(1.0 + jnp.exp(-jnp.abs(x)))


def _gla_kernel(q_ref, k_ref, v_ref, g_ref, lr_ref, wdf_ref, bdf_ref, wdb_ref, bdb_ref, gg_ref,
                s0f_ref, s0b_ref, y_ref, sf_ref, sb_ref,
                laf_scr, lab_scr, qf_scr, qb_scr, o_scr, uf_scr, ub_scr, df_scr, db_scr, *, seq, dk, dv, hp):
    c = GLA_CHUNK
    n_chunks = seq // c
    scale = dk ** -0.5
    heads = range(hp)
    kcols = [slice(h * dk, (h + 1) * dk) for h in heads]
    vcols = [slice(h * dv, (h + 1) * dv) for h in heads]

    lr_split = _split2(lr_ref[0])
    for h in heads:
        laf_scr[h] = _log_sigmoid(_dot_hp(lr_split, wdf_ref[h]) + bdf_ref[:, kcols[h]]) / GLA_TAU
        lab_scr[h] = _log_sigmoid(_dot_hp(lr_split, wdb_ref[h]) + bdb_ref[:, kcols[h]]) / GLA_TAU

    per = GLA_UNROLL
    blk = per * c
    row = lax.broadcasted_iota(jnp.int32, (blk, blk), 0)
    col = lax.broadcasted_iota(jnp.int32, (blk, blk), 1)
    same = (row // c) == (col // c)
    lower = same & (row >= col)
    upper = same & (col >= row)
    t_fwd = lower.astype(BF16)
    t_bwd = upper.astype(BF16)

    def chunk_rows(x, r):
        return jnp.concatenate([jnp.broadcast_to(x[j * c + r:j * c + r + 1], (c, dk)) for j in range(per)], axis=0)

    def block_local(m, carry):
        sl = pl.ds(pl.multiple_of(m * blk, blk), blk)
        bf = [_dot_exact_lhs(t_fwd, laf_scr[h, sl, :]) for h in heads]
        bb = [_dot_exact_lhs(t_bwd, lab_scr[h, sl, :]) for h in heads]
        tot_f = [chunk_rows(x, c - 1) for x in bf]
        tot_b = [chunk_rows(x, 0) for x in bb]
        q = [q_ref[0, sl, kcols[h]].astype(F32) * scale for h in heads]
        k = [k_ref[0, sl, kcols[h]].astype(F32) for h in heads]
        v = [v_ref[0, sl, vcols[h]] for h in heads]
        qf = [(q[h] * jnp.exp(bf[h])).astype(BF16) for h in heads]
        kf = [(k[h] * jnp.exp(-bf[h])).astype(BF16) for h in heads]
        qb = [(q[h] * jnp.exp(bb[h])).astype(BF16) for h in heads]
        kb = [(k[h] * jnp.exp(-bb[h])).astype(BF16) for h in heads]
        ksf = [(k[h] * jnp.exp(tot_f[h] - bf[h])).astype(BF16) for h in heads]
        ksb = [(k[h] * jnp.exp(tot_b[h] - bb[h])).astype(BF16) for h in heads]
        sc_f = [_dot_nt(qf[h], kf[h]) for h in heads]
        sc_b = [_dot_nt(qb[h], kb[h]) for h in heads]
        att = [(jnp.where(lower, sc_f[h], 0.0) + jnp.where(upper, sc_b[h], 0.0)).astype(BF16) for h in heads]
        o_loc = [_dot(att[h], v[h]) for h in heads]
        dec_f = [jnp.exp(x) for x in tot_f]
        dec_b = [jnp.exp(x) for x in tot_b]
        for h in heads:
            o_scr[h, sl, :] = o_loc[h]
            qf_scr[h, sl, :] = qf[h]
            qb_scr[h, sl, :] = qb[h]
        for j in range(per):
            n = m * per + j
            rows = slice(j * c, (j + 1) * c)
            for h in heads:
                uf_scr[h, n] = _dot_tn(v[h][rows], ksf[h][rows])
                ub_scr[h, n] = _dot_tn(v[h][rows], ksb[h][rows])
                df_scr[h, n] = dec_f[h][j * c:j * c + SUBLANES]
                db_scr[h, n] = dec_b[h][j * c:j * c + SUBLANES]
        return carry

    lax.fori_loop(0, n_chunks // per, block_local, 0)

    def scan_fwd(n, s):
        upd = [uf_scr[h, n] for h in heads]
        for h in heads:
            uf_scr[h, n] = s[h]
        return tuple(s[h] * df_scr[h, n][0:1, :] + upd[h] for h in heads)

    def scan_bwd(i, s):
        n = n_chunks - 1 - i
        upd = [ub_scr[h, n] for h in heads]
        for h in heads:
            ub_scr[h, n] = s[h]
        return tuple(s[h] * db_scr[h, n][0:1, :] + upd[h] for h in heads)

    s_f = lax.fori_loop(0, n_chunks, scan_fwd, tuple(s0f_ref[0, h].T for h in heads), unroll=GLA_UNROLL)
    s_b = lax.fori_loop(0, n_chunks, scan_bwd, tuple(s0b_ref[0, h].T for h in heads), unroll=GLA_UNROLL)
    for h in heads:
        sf_ref[0, h] = s_f[h].T
        sb_ref[0, h] = s_b[h].T

    def chunk_inter(n, carry):
        sl = pl.ds(pl.multiple_of(n * c, c), c)
        inter = [_dot_nt(qf_scr[h, sl, :], uf_scr[h, n].astype(BF16))
                 + _dot_nt(qb_scr[h, sl, :], ub_scr[h, n].astype(BF16)) for h in heads]
        for h in heads:
            o_scr[h, sl, :] += inter[h]
        return carry

    lax.fori_loop(0, n_chunks, chunk_inter, 0, unroll=GLA_UNROLL)

    for h in heads:
        o = _rms(o_scr[h], gg_ref[...])
        y_ref[0, :, vcols[h]] = (o * _silu(g_ref[0, :, vcols[h]].astype(F32))).astype(y_ref.dtype)


def _gla(proj, lr, wdf, bdf, wdb, bdb, g_gla, s0f, s0b, batch, seq, dk, dv, hp, cast_srcs):
    heads = GLA_HEADS
    proj3 = proj.reshape(batch, seq, proj.shape[-1])
    lr3 = lr.reshape(batch, seq, LANES)
    kdim = heads * dk
    width = heads * dv
    bk, bv = hp * dk, hp * dv
    k_blk = kdim // bk
    v_blk = 2 * kdim // bv
    g_blk = (2 * kdim + width) // bv
    n_chunks = seq // GLA_CHUNK

    def s0_map(s0):
        if s0.shape[0] == batch:
            return lambda b, h: (b, h, 0, 0)
        return lambda b, h: (0, h, 0, 0)

    n_hsteps = heads // hp
    cast_specs, cast_shapes = _cast_rider_specs(cast_srcs, batch * n_hsteps, lambda b, h: b * n_hsteps + h)
    kern = _with_cast_riders(functools.partial(_gla_kernel, seq=seq, dk=dk, dv=dv, hp=hp), 12, 3, len(cast_srcs))
    return pl.pallas_call(
        kern,
        grid=(batch, n_hsteps),
        in_specs=[pl.BlockSpec((1, seq, bk), lambda b, h: (b, 0, h)),
                  pl.BlockSpec((1, seq, bk), lambda b, h: (b, 0, k_blk + h)),
                  pl.BlockSpec((1, seq, bv), lambda b, h: (b, 0, v_blk + h)),
                  pl.BlockSpec((1, seq, bv), lambda b, h: (b, 0, g_blk + h)),
                  pl.BlockSpec((1, seq, LANES), lambda b, h: (b, 0, 0)),
                  pl.BlockSpec((hp, LANES, dk), lambda b, h: (h, 0, 0)),
                  pl.BlockSpec((1, bk), lambda b, h: (0, h)),
                  pl.BlockSpec((hp, LANES, dk), lambda b, h: (h, 0, 0)),
                  pl.BlockSpec((1, bk), lambda b, h: (0, h)),
                  pl.BlockSpec((1, dv), lambda b, h: (0, 0)),
                  pl.BlockSpec((1, hp, dk, dv), s0_map(s0f)),
                  pl.BlockSpec((1, hp, dk, dv), s0_map(s0b))] + cast_specs,
        out_specs=[pl.BlockSpec((1, seq, bv), lambda b, h: (b, 0, h)),
                   pl.BlockSpec((1, hp, dk, dv), lambda b, h: (b, h, 0, 0)),
                   pl.BlockSpec((1, hp, dk, dv), lambda b, h: (b, h, 0, 0))] + cast_specs,
        out_shape=[jax.ShapeDtypeStruct((batch, seq, width), BF16),
                   jax.ShapeDtypeStruct((batch, heads, dk, dv), F32),
                   jax.ShapeDtypeStruct((batch, heads, dk, dv), F32)] + cast_shapes,
        scratch_shapes=[pltpu.VMEM((hp, seq, dk), F32), pltpu.VMEM((hp, seq, dk), F32),
                        pltpu.VMEM((hp, seq, dk), BF16), pltpu.VMEM((hp, seq, dk), BF16),
                        pltpu.VMEM((hp, seq, dv), F32),
                        pltpu.VMEM((hp, n_chunks, dv, dk), F32), pltpu.VMEM((hp, n_chunks, dv, dk), F32),
                        pltpu.VMEM((hp, n_chunks, SUBLANES, dk), F32), pltpu.VMEM((hp, n_chunks, SUBLANES, dk), F32)],
        compiler_params=_cparams(("arbitrary", "arbitrary")),
        name="gla",
    )(proj3, proj3, proj3, proj3, lr3, wdf, bdf, wdb, bdb, g_gla, s0f, s0b, *cast_srcs)


def _filter_kernel(pe_ref, w1_ref, b1_ref, w2_ref, b2_ref, fr_ref, w3_ref, dl_ref, ad_ref, hl_ref, hm_ref, mlp_scr,
                   *, seq):
    @pl.when(pl.program_id(0) == 0)
    def _():
        fr = fr_ref[...]
        h1 = jnp.sin(fr * (_dot_hp(pe_ref[...], w1_ref[...]) + b1_ref[...]))
        mlp_scr[...] = jnp.sin(fr * (_dot_hp(h1, w2_ref[...]) + b2_ref[...]))

    h2 = _split2(mlp_scr[...])
    dec = jnp.exp(-pe_ref[:, 0:1] * dl_ref[...])
    row = lax.broadcasted_iota(jnp.int32, (seq, 1), 0)
    alt = jnp.where(row % 2 == 0, 1.0, -1.0)
    phase = row % 4
    cos_half = jnp.where(phase == 0, 1.0, jnp.where(phase == 2, -1.0, 0.0))
    sin_half = jnp.where(phase == 1, 1.0, jnp.where(phase == 3, -1.0, 0.0))
    for o in range(HY_ORDER):
        ff = _dot_hp(h2, w3_ref[o]) * dec
        fb = _dot_hp(h2, w3_ref[HY_ORDER + o]) * dec
        nrm = jnp.sum(jnp.abs(ff), axis=0, keepdims=True) + jnp.sum(jnp.abs(fb), axis=0, keepdims=True)
        inv = 1.0 / nrm
        ff = ff * inv
        fb = jnp.where(row == 0, 0.0, fb * inv)
        a = ff + fb
        d = fb - ff
        ad_ref[o, 0] = a.astype(ad_ref.dtype)
        ad_ref[o, 1] = d.astype(ad_ref.dtype)
        ad_ref[o, 2] = (alt * a).astype(ad_ref.dtype)
        ad_ref[o, 3] = (-alt * d).astype(ad_ref.dtype)
        hl_ref[o] = jnp.sum(alt * a, axis=0, keepdims=True)
        hm_ref[o] = jnp.concatenate([jnp.sum(cos_half * a, axis=0, keepdims=True),
                                     jnp.sum(sin_half * d, axis=0, keepdims=True)], axis=0) * (1.0 / seq)


def _hy_filter(pe, w1p, b1, w2, b2, freq, w3r, deltas, seq, width, tc):
    hid = w2.shape[0]
    kern = functools.partial(_filter_kernel, seq=seq)
    return pl.pallas_call(
        kern,
        grid=(width // tc,),
        in_specs=[pl.BlockSpec((seq, LANES), lambda c: (0, 0)),
                  pl.BlockSpec((LANES, hid), lambda c: (0, 0)),
                  pl.BlockSpec((1, hid), lambda c: (0, 0)),
                  pl.BlockSpec((hid, hid), lambda c: (0, 0)),
                  pl.BlockSpec((1, hid), lambda c: (0, 0)),
                  pl.BlockSpec((1, hid), lambda c: (0, 0)),
                  pl.BlockSpec((2 * HY_ORDER, hid, tc), lambda c: (0, 0, c)),
                  pl.BlockSpec((1, tc), lambda c: (0, c))],
        out_specs=[pl.BlockSpec((HY_ORDER, 4, seq, tc), lambda c: (0, 0, 0, c)),
                   pl.BlockSpec((HY_ORDER, 1, tc), lambda c: (0, 0, c)),
                   pl.BlockSpec((HY_ORDER, 2, tc), lambda c: (0, 0, c))],
        out_shape=[jax.ShapeDtypeStruct((HY_ORDER, 4, seq, width), BF16),
                   jax.ShapeDtypeStruct((HY_ORDER, 1, width), F32),
                   jax.ShapeDtypeStruct((HY_ORDER, 2, width), F32)],
        scratch_shapes=[pltpu.VMEM((seq, hid), F32)],
        compiler_params=_cparams(("arbitrary",)),
        name="hy_filter",
    )(pe, w1p, b1, w2, b2, freq, w3r, deltas)


def _spectrum_kernel(cs_ref, ad_ref, h_ref, *, seq, per_part, tr):
    r = pl.program_id(1)
    k = (r % per_part) * tr + lax.broadcasted_iota(jnp.int32, (tr, 1), 0)
    wgt = jnp.where(k == 0, 1.0, 2.0) * (0.5 / seq)
    h_ref[0] = wgt * _dot(cs_ref[0], ad_ref[0, 0])


def _hy_spectrum(cs3, ad, seq, part_rows, n_parts, width, tr, tc):
    per_part = part_rows // tr
    kern = functools.partial(_spectrum_kernel, seq=seq, per_part=per_part, tr=tr)
    return pl.pallas_call(
        kern,
        grid=(HY_ORDER, n_parts * per_part, width // tc),
        in_specs=[pl.BlockSpec((1, tr, seq), lambda o, r, c: ((r // per_part) % 2, r % per_part, 0)),
                  pl.BlockSpec((1, 1, seq, tc), lambda o, r, c: (o, r // per_part, 0, c))],
        out_specs=pl.BlockSpec((1, tr, tc), lambda o, r, c: (o, r, c)),
        out_shape=jax.ShapeDtypeStruct((HY_ORDER, n_parts * part_rows, width), F32),
        compiler_params=_cparams(("arbitrary", "arbitrary", "arbitrary")),
        name="hy_spectrum",
    )(cs3, ad)


def _short_conv_fn(seq, seg):
    row = lax.broadcasted_iota(jnp.int32, (seq, 1), 0)
    pos = row % seg

    def short_conv(u_ref, cw_ref, cb_ref, b=0):
        u = u_ref[b].astype(F32)
        prev = jnp.where(pos == 0, 0.0, pltpu.roll(u, 1, 0))
        nxt = jnp.where(pos == seg - 1, 0.0, pltpu.roll(u, seq - 1, 0))
        return prev * cw_ref[0:1, :] + u * cw_ref[1:2, :] + nxt * cw_ref[2:3, :] + cb_ref[...]

    return short_conv


def _hyconv_kernel(u0_ref, u1_ref, u2_ref, cw0_ref, cw1_ref, cw2_ref, cb0_ref, cb1_ref, cb2_ref,
                   cs_ref, h_ref, hl_ref, bias_ref, z_ref, y_scr, *, seq, seg):
    short_conv = _short_conv_fn(seq, seg)
    row = lax.broadcasted_iota(jnp.int32, (seq, 1), 0)
    alt = jnp.where(row % 2 == 0, 1.0, -1.0)
    gates = ((u1_ref, cw1_ref, cb1_ref), (u2_ref, cw2_ref, cb2_ref))
    items = range(z_ref.shape[0])
    z = [short_conv(u0_ref, cw0_ref, cb0_ref, b) for b in items]
    kb = min(seq, FREQ_BLOCK)
    for n in range(HY_ORDER):
        zb = [z[b].astype(BF16) for b in items]
        for r in range(0, seq, kb):
            xc = [_dot(cs_ref[r:r + kb, :], zb[b]) for b in items]
            xs = [_dot(cs_ref[seq + r:seq + r + kb, :], zb[b]) for b in items]
            hre = h_ref[n, r:r + kb, :]
            him = h_ref[n, seq + r:seq + r + kb, :]
            for b in items:
                y_scr[b, r:r + kb, :] = (xc[b] * hre + xs[b] * him).astype(BF16)
                y_scr[b, seq + r:seq + r + kb, :] = (xs[b] * hre - xc[b] * him).astype(BF16)
        nyq = [jnp.sum(alt * z[b], axis=0, keepdims=True) * (hl_ref[n] * (0.5 / seq)) for b in items]
        conv = [_dot(cs_ref[:seq, :], y_scr[b, :seq, :]) + _dot(cs_ref[seq:, :], y_scr[b, seq:, :]) + alt * nyq[b]
                for b in items]
        gate = [short_conv(*gates[n], b) for b in items]
        z = [gate[b] * (conv[b] + bias_ref[n] * z[b]) for b in items]
    for b in items:
        z_ref[b] = z[b].astype(z_ref.dtype)


def _hyconv_split_kernel(u0_ref, u1_ref, u2_ref, cw0_ref, cw1_ref, cw2_ref, cb0_ref, cb1_ref, cb2_ref,
                         cs_ref, h_ref, hm_ref, tw_ref, bias_ref, z_ref, g_scr, z_scr, c_scr, *, seq, seg):
    m = seq // 2
    tc = z_ref.shape[2]
    n_lane_blocks = tc // LANES
    short_conv = _short_conv_fn(seq, seg)
    row = lax.broadcasted_iota(jnp.int32, (m, 1), 0)
    alt = jnp.where(row % 2 == 0, 1.0, -1.0)
    gates = ((u1_ref, cw1_ref, cb1_ref), (u2_ref, cw2_ref, cb2_ref))
    z = short_conv(u0_ref, cw0_ref, cb0_ref)
    kb = min(m, FREQ_BLOCK)
    for n in range(HY_ORDER):
        for j in range(n_lane_blocks):
            z_scr[j] = z[:, j * LANES:(j + 1) * LANES]
        ze = jnp.concatenate([z_scr[j, pl.ds(0, m, stride=2), :] for j in range(n_lane_blocks)], axis=1)
        zo = jnp.concatenate([z_scr[j, pl.ds(1, m, stride=2), :] for j in range(n_lane_blocks)], axis=1)
        e_mid = jnp.sum(alt * ze, axis=0, keepdims=True)
        o_mid = jnp.sum(alt * zo, axis=0, keepdims=True)
        zeb = ze.astype(BF16)
        zob = zo.astype(BF16)
        for r in range(0, m, kb):
            rows = slice(r, r + kb)
            srows = slice(m + r, m + r + kb)
            ec = _dot(cs_ref[rows, :], zeb)
            es = _dot(cs_ref[srows, :], zeb)
            oc = _dot(cs_ref[rows, :], zob)
            os_ = _dot(cs_ref[srows, :], zob)
            c = tw_ref[0, rows, :]
            s = tw_ref[1, rows, :]
            pc = c * oc - s * os_
            ps = c * os_ + s * oc
            xca, xsa = ec + pc, es + ps
            xcb, xsb = ec - pc, ps - es
            har = h_ref[n, rows, :]
            hai = h_ref[n, srows, :]
            hbr = h_ref[n, 2 * m + r:2 * m + r + kb, :]
            hbi = h_ref[n, 3 * m + r:3 * m + r + kb, :]
            yar = xca * har + xsa * hai
            yai = xca * hai - xsa * har
            ybr = xcb * hbr + xsb * hbi
            ybi = xcb * hbi - xsb * hbr
            dr = yar - ybr
            di = yai + ybi
            g_scr[0, rows, :] = (yar + ybr).astype(BF16)
            g_scr[0, srows, :] = (ybi - yai).astype(BF16)
            g_scr[1, rows, :] = (c * dr - s * di).astype(BF16)
            g_scr[1, srows, :] = (-(s * dr + c * di)).astype(BF16)
        hr = hm_ref[n, 0:1, :]
        hi = hm_ref[n, 1:2, :]
        ymr = e_mid * hr + o_mid * hi
        ymi = e_mid * hi - o_mid * hr
        y_even = _dot(cs_ref[:m, :], g_scr[0, :m, :]) + _dot(cs_ref[m:, :], g_scr[0, m:, :]) + alt * ymr
        y_odd = _dot(cs_ref[:m, :], g_scr[1, :m, :]) + _dot(cs_ref[m:, :], g_scr[1, m:, :]) - alt * ymi
        for j in range(n_lane_blocks):
            c_scr[j, pl.ds(0, m, stride=2), :] = y_even[:, j * LANES:(j + 1) * LANES]
            c_scr[j, pl.ds(1, m, stride=2), :] = y_odd[:, j * LANES:(j + 1) * LANES]
        conv = jnp.concatenate([c_scr[j] for j in range(n_lane_blocks)], axis=1)
        z = short_conv(*gates[n]) * (conv + bias_ref[n] * z)
    z_ref[0] = z.astype(z_ref.dtype)


def _hy_conv(proj, conv_w, conv_b, cs, hspec, hl, hm, tw, bias, batch, seq, seg, col0, width, tc, split, bp,
             cast_srcs):
    proj3 = proj.reshape(batch, seq, proj.shape[-1])
    blk0 = col0 // tc
    per = width // tc
    n_b = batch // bp
    assert bp == 1 or not split
    once = dict(pipeline_mode=pl.Buffered(1))

    def u_spec(p):
        return pl.BlockSpec((bp, seq, tc), lambda c, b: (b, 0, blk0 + p * per + c))

    def w_spec(p, rows):
        return pl.BlockSpec((rows, tc), lambda c, b: (0, p * per + c))

    in_specs = [u_spec(0), u_spec(1), u_spec(2),
                w_spec(0, HY_SHORT), w_spec(1, HY_SHORT), w_spec(2, HY_SHORT),
                w_spec(0, 1), w_spec(1, 1), w_spec(2, 1),
                pl.BlockSpec(cs.shape, lambda c, b: (0, 0), **once),
                pl.BlockSpec((HY_ORDER, 2 * seq, tc), lambda c, b: (0, 0, c), **once)]
    bias_spec = pl.BlockSpec((HY_ORDER, 1, tc), lambda c, b: (0, 0, c))
    operands = [proj3, proj3, proj3, conv_w, conv_w, conv_w, conv_b, conv_b, conv_b, cs, hspec]
    if split:
        kern = functools.partial(_hyconv_split_kernel, seq=seq, seg=seg)
        in_specs += [pl.BlockSpec((HY_ORDER, 2, tc), lambda c, b: (0, 0, c)),
                     pl.BlockSpec((2, seq // 2, tc), lambda c, b: (0, 0, 0)), bias_spec]
        operands += [hm, tw, bias]
        scratch = [pltpu.VMEM((2, seq, tc), BF16), pltpu.VMEM((tc // LANES, seq, LANES), F32),
                   pltpu.VMEM((tc // LANES, seq, LANES), F32)]
    else:
        kern = functools.partial(_hyconv_kernel, seq=seq, seg=seg)
        in_specs += [pl.BlockSpec((HY_ORDER, 1, tc), lambda c, b: (0, 0, c)), bias_spec]
        operands += [hl, bias]
        scratch = [pltpu.VMEM((bp, 2 * seq, tc), BF16)]
    cast_specs, cast_shapes = _cast_rider_specs(cast_srcs, per * n_b, lambda c, b: c * n_b + b)
    kern = _with_cast_riders(kern, len(operands), 1, len(cast_srcs))
    return pl.pallas_call(
        kern,
        grid=(per, n_b),
        in_specs=in_specs + cast_specs,
        out_specs=[pl.BlockSpec((bp, seq, tc), lambda c, b: (b, 0, c))] + cast_specs,
        out_shape=[jax.ShapeDtypeStruct((batch, seq, width), BF16)] + cast_shapes,
        scratch_shapes=scratch,
        compiler_params=_cparams(("arbitrary", "arbitrary")),
        name="hy_conv",
    )(*operands, *cast_srcs)


def _mixout_kernel(*refs, gla_width, n_own, aliased):
    (yg_ref, zh_ref, x_ref, mod_ref, ghy_ref, gpost_ref, gpre_ref, wo_ref, wrh_ref, wrl_ref, br_ref,
     cnt_in_ref) = refs[:12]
    x1_ref, hg_ref, route_ref, cnt_ref, cnt_scr = refs[12 + int(aliased):]
    d = x_ref.shape[1]
    tm = x_ref.shape[0] // MIX_SUBTILES
    i = pl.program_id(0)

    @pl.when(i == 0)
    def _():
        cnt_scr[...] = cnt_in_ref[...]

    @pl.when(i < n_own)
    def _():
        for s in range(MIX_SUBTILES):
            _mixout_rows(slice(s * tm, (s + 1) * tm), tm, d, gla_width, yg_ref, zh_ref, x_ref, mod_ref, ghy_ref,
                         gpost_ref, gpre_ref, wo_ref, wrh_ref, wrl_ref, br_ref, x1_ref, hg_ref, route_ref, cnt_scr)

    @pl.when(i >= n_own)
    def _():
        hg_ref[...] = jnp.zeros_like(hg_ref)

    cnt_ref[...] = cnt_scr[...]


def _mixout_rows(rows, tm, d, gla_width, yg_ref, zh_ref, x_ref, mod_ref, ghy_ref, gpost_ref, gpre_ref, wo_ref,
                 wrh_ref, wrl_ref, br_ref, x1_ref, hg_ref, route_ref, cnt_scr):
    yh = _rms(zh_ref[rows, :].astype(F32), ghy_ref[...]).astype(BF16)
    y = _dot(yg_ref[rows, :], wo_ref[:gla_width, :]) + _dot(yh, wo_ref[gla_width:, :])
    x1 = x_ref[rows, :] + _rms(y, gpost_ref[...] * mod_ref[0, 2:3, :])
    x1_ref[rows, :] = x1
    h2 = _rms(x1, gpre_ref[...] * (1.0 + mod_ref[0, 4:5, :])) + mod_ref[0, 3:4, :]
    hh, hl = _split2(h2)
    hg_ref[rows, :d] = h2
    logits = _dot(hh, wrh_ref[...]) + (_dot(hh, wrl_ref[...]) + _dot(hl, wrh_ref[...])) + br_ref[...]

    lane = lax.broadcasted_iota(jnp.int32, logits.shape, 1).astype(F32)
    neg = -jnp.inf
    is_grp = (lane >= N_EXPERTS) & (lane < N_EXPERTS + N_GROUPS)
    m = jnp.max(jnp.where(is_grp, logits, neg), axis=1, keepdims=True)
    p_grp = 1.0 / jnp.sum(jnp.where(is_grp, jnp.exp(logits - m), 0.0), axis=1, keepdims=True)
    grp = jnp.min(jnp.where(is_grp & (logits == m), lane - N_EXPERTS, 1e9), axis=1, keepdims=True)
    sel = (lane >= grp * EXP_PER_GROUP) & (lane < (grp + 1.0) * EXP_PER_GROUP)
    me = jnp.max(jnp.where(sel, logits, neg), axis=1, keepdims=True)
    pe = jnp.where(sel, jnp.exp(logits - me), -1.0)
    v1 = jnp.max(pe, axis=1, keepdims=True)
    i1 = jnp.min(jnp.where(pe == v1, lane, 1e9), axis=1, keepdims=True)
    pe2 = jnp.where(lane == i1, -1.0, pe)
    v2 = jnp.max(pe2, axis=1, keepdims=True)
    i2 = jnp.min(jnp.where(pe2 == v2, lane, 1e9), axis=1, keepdims=True)
    den = v1 + v2
    gates = jnp.where(lane == i1, v1 / den, jnp.where(lane == i2, v2 / den, 0.0)) * p_grp

    lo = jnp.minimum(i1, i2) - grp * EXP_PER_GROUP
    hi = jnp.maximum(i1, i2) - grp * EXP_PER_GROUP
    bucket = grp * PAIRS_PER_GROUP + lo * (2 * EXP_PER_GROUP - 1 - lo) * 0.5 + (hi - lo - 1.0)
    onehot = lane == bucket
    r_i = lax.broadcasted_iota(jnp.int32, (tm, tm), 0)
    c_i = lax.broadcasted_iota(jnp.int32, (tm, tm), 1)
    earlier = _dot((r_i > c_i).astype(BF16), onehot.astype(BF16)) + cnt_scr[...]
    rank = jnp.sum(jnp.where(onehot, earlier, 0.0), axis=1, keepdims=True)
    cnt_scr[...] += jnp.sum(onehot.astype(F32), axis=0, keepdims=True)
    route = jnp.where(lane == ROUTE_BUCKET_LANE, bucket, jnp.where(lane == ROUTE_RANK_LANE, rank, gates))
    hg_ref[rows, d:] = route
    route_ref[:, rows] = route.T[ROUTE_BUCKET_LANE:ROUTE_BUCKET_LANE + SUBLANES, :]


def _mix_out(yg, zh, x, mod, mod_row, g_hy, g_post, g_pre, w_out, wr_hi, wr_lo, b_r, cnt_in, hg_all, row0, n_total, tm):
    n, d = x.shape
    gw = yg.shape[1]
    hw = zh.shape[1]
    aliased = hg_all is not None
    n_own = n // tm
    n_steps = n_own if aliased else n_total // tm
    assert row0 % tm == 0 and n_total % tm == 0 and (row0 + n == n_total if aliased else row0 == 0)
    kern = functools.partial(_mixout_kernel, gla_width=gw, n_own=n_own, aliased=aliased)
    row = lambda i: (jnp.minimum(i, n_own - 1), 0)
    fixed = lambda i: (0, 0)
    in_specs = [pl.BlockSpec((tm, gw), row),
                pl.BlockSpec((tm, hw), row),
                pl.BlockSpec((tm, d), row),
                pl.BlockSpec((1, 6, d), lambda i: (mod_row(jnp.minimum(i, n_own - 1) * tm), 0, 0)),
                pl.BlockSpec((1, hw), fixed),
                pl.BlockSpec((1, d), fixed),
                pl.BlockSpec((1, d), fixed),
                pl.BlockSpec((gw + hw, d), fixed),
                pl.BlockSpec((d, LANES), fixed),
                pl.BlockSpec((d, LANES), fixed),
                pl.BlockSpec((1, LANES), fixed),
                pl.BlockSpec((1, LANES), fixed)]
    operands = [yg, zh, x, mod, g_hy, g_post, g_pre, w_out, wr_hi, wr_lo, b_r, cnt_in]
    if aliased:
        in_specs.append(pl.BlockSpec(memory_space=pl.ANY))
        operands.append(hg_all)
    return pl.pallas_call(
        kern,
        grid=(n_steps,),
        in_specs=in_specs,
        out_specs=[pl.BlockSpec((tm, d), row),
                   pl.BlockSpec((tm, d + LANES), lambda i: (row0 // tm + i, 0)),
                   pl.BlockSpec((SUBLANES, tm), lambda i: (0, jnp.minimum(i, n_own - 1))),
                   pl.BlockSpec((1, LANES), fixed)],
        out_shape=[jax.ShapeDtypeStruct((n, d), F32),
                   jax.ShapeDtypeStruct((n_total, d + LANES), F32),
                   jax.ShapeDtypeStruct((SUBLANES, n), F32),
                   jax.ShapeDtypeStruct((1, LANES), F32)],
        scratch_shapes=[pltpu.VMEM((1, LANES), F32)],
        input_output_aliases={len(operands) - 1: 1} if aliased else {},
        compiler_params=_cparams(("arbitrary",)),
        name="mix_out",
    )(*operands)


def _row_gather_copy(src_hbm, row, buf, slot, r, sem):
    return pltpu.make_async_copy(src_hbm.at[pl.ds(row, 1), :], buf.at[slot, pl.ds(r, 1), :], sem.at[slot])


def _start_row_gather(idx_ref, base, src_hbm, buf, slot, sem, tm):
    def body(r, carry):
        _row_gather_copy(src_hbm, idx_ref[base + r], buf, slot, r, sem).start()
        return carry

    lax.fori_loop(0, tm, body, 0, unroll=8)


def _start_row_gather_inline(idx_ref, base, src_hbm, buf, slot, sem, tm):
    for r in range(tm):
        _row_gather_copy(src_hbm, idx_ref[base + r], buf, slot, r, sem).start()


def _wait_row_gather(src_hbm, buf, slot, sem, tm):
    pltpu.make_async_copy(src_hbm.at[pl.ds(0, tm), :], buf.at[slot], sem.at[slot]).wait()


def _moe_kernel(src_ref, ea_ref, eb_ref, valid_ref, hg_hbm,
                wga_ref, wua_ref, wda_ref, wgb_ref, wub_ref, wdb_ref, y_ref, buf, sem, *, tm, d):
    t = pl.program_id(0)
    slot = t % 2

    @pl.when(t == 0)
    def _():
        _start_row_gather(src_ref, 0, hg_hbm, buf, 0, sem, tm)

    def used_tile(slot):
        _wait_row_gather(hg_hbm, buf, slot, sem, tm)
        _start_row_gather_inline(src_ref, (t + 1) * tm, hg_hbm, buf, 1 - slot, sem, tm)
        h = buf[slot, :, :d].astype(BF16)
        gates = buf[slot, :, d:]
        lane = lax.broadcasted_iota(jnp.int32, gates.shape, 1)

        experts = ((wga_ref, wua_ref, wda_ref, ea_ref[t]), (wgb_ref, wub_ref, wdb_ref, eb_ref[t]))
        pre = [_dot(h, wg_ref[0]) for wg_ref, _, _, _ in experts]
        up = [_dot(h, wu_ref[0]) for _, wu_ref, _, _ in experts]
        he = [(_silu(pre[i]) * up[i]).astype(BF16) for i in range(2)]
        gate = [jnp.sum(jnp.where(lane == e, gates, 0.0), axis=1, keepdims=True) for _, _, _, e in experts]
        out = [gate[i] * _dot(he[i], experts[i][2][0]) for i in range(2)]
        y_ref[...] = out[0] + out[1]

    for parity in range(2):
        pl.when((valid_ref[t] == 1) & (slot == parity))(functools.partial(used_tile, parity))

    @pl.when(valid_ref[t] == 0)
    def _():
        @pl.when(valid_ref[jnp.maximum(t - 1, 0)] == 1)
        def _():
            _wait_row_gather(hg_hbm, buf, slot, sem, tm)

        y_ref[...] = jnp.zeros_like(y_ref)


def _moe(hg, src, ea, eb, valid, wg, wu, wd, n_tiles, tm):
    d = hg.shape[1] - LANES
    _, _, de = wg.shape
    kern = functools.partial(_moe_kernel, tm=tm, d=d)
    wa = lambda t, src, ea, eb, valid: (ea[t], 0, 0)
    wb = lambda t, src, ea, eb, valid: (eb[t], 0, 0)
    return pl.pallas_call(
        kern,
        grid_spec=pltpu.PrefetchScalarGridSpec(
            num_scalar_prefetch=4,
            grid=(n_tiles,),
            in_specs=[pl.BlockSpec(memory_space=pl.ANY),
                      pl.BlockSpec((1, d, de), wa), pl.BlockSpec((1, d, de), wa), pl.BlockSpec((1, de, d), wa),
                      pl.BlockSpec((1, d, de), wb), pl.BlockSpec((1, d, de), wb), pl.BlockSpec((1, de, d), wb)],
            out_specs=pl.BlockSpec((tm, d), lambda t, src, ea, eb, valid: (t, 0)),
            scratch_shapes=[pltpu.VMEM((2, tm, d + LANES), F32), pltpu.SemaphoreType.DMA((2,))]),
        out_shape=jax.ShapeDtypeStruct((n_tiles * tm, d), F32),
        compiler_params=_cparams(("arbitrary",)),
        name="moe",
    )(src, ea, eb, valid, hg, wg, wu, wd, wg, wu, wd)


def _ffn_out_kernel(pos_ref, y_hbm, x1_ref, mod_ref, gpost_ref, o_ref, buf, sem, *, tm):
    i = pl.program_id(0)
    n_i = pl.num_programs(0)
    slot = i % 2

    @pl.when(i == 0)
    def _():
        _start_row_gather(pos_ref, 0, y_hbm, buf, 0, sem, tm)

    _wait_row_gather(y_hbm, buf, slot, sem, tm)

    @pl.when(i + 1 < n_i)
    def _():
        _start_row_gather_inline(pos_ref, (i + 1) * tm, y_hbm, buf, 1 - slot, sem, tm)
        o_ref[...] = x1_ref[...] + _rms(buf[slot], gpost_ref[...] * mod_ref[0, 5:6, :])

    @pl.when(i + 1 == n_i)
    def _():
        o_ref[...] = x1_ref[...] + _rms(buf[slot], gpost_ref[...] * mod_ref[0, 5:6, :])


def _ffn_out(y_sorted, pos, x1, mod, mod_row, g_post, tm):
    n, d = x1.shape
    kern = functools.partial(_ffn_out_kernel, tm=tm)
    return pl.pallas_call(
        kern,
        grid_spec=pltpu.PrefetchScalarGridSpec(
            num_scalar_prefetch=1,
            grid=(n // tm,),
            in_specs=[pl.BlockSpec(memory_space=pl.ANY),
                      pl.BlockSpec((tm, d), lambda i, pos: (i, 0)),
                      pl.BlockSpec((1, 6, d), lambda i, pos: (mod_row(i * tm), 0, 0)),
                      pl.BlockSpec((1, d), lambda i, pos: (0, 0))],
            out_specs=pl.BlockSpec((tm, d), lambda i, pos: (i, 0)),
            scratch_shapes=[pltpu.VMEM((2, tm, d), F32), pltpu.SemaphoreType.DMA((2,))]),
        out_shape=jax.ShapeDtypeStruct((n, d), F32),
        compiler_params=_cparams(("arbitrary",)),
        name="ffn_out",
    )(pos, y_sorted, x1, mod, g_post)


def _route_tables(route, counts, n_tiles, tm):
    n = route.shape[1]
    n_buckets = N_GROUPS * PAIRS_PER_GROUP
    cnt = counts[0, :n_buckets].astype(jnp.int32)
    padded = (cnt + tm - 1) // tm * tm
    ends = jnp.cumsum(padded)
    starts = ends - padded
    bucket = route[0].astype(jnp.int32)
    rank = route[ROUTE_RANK_LANE - ROUTE_BUCKET_LANE].astype(jnp.int32)
    is_bucket = bucket[:, None] == jnp.arange(n_buckets, dtype=jnp.int32)[None, :]
    pos = jnp.sum(jnp.where(is_bucket, starts[None, :], 0), axis=1) + rank
    src = (jnp.arange(n_tiles * tm, dtype=jnp.int32) % n).at[pos].set(jnp.arange(n, dtype=jnp.int32))
    n_valid = ends[-1] // tm
    tile = jnp.arange(n_tiles, dtype=jnp.int32)
    used = jnp.minimum(tile, n_valid - 1)
    tile_bucket = jnp.sum((ends[None, :] <= (used * tm)[:, None]).astype(jnp.int32), axis=1)
    grp = tile_bucket // PAIRS_PER_GROUP
    pair = tile_bucket % PAIRS_PER_GROUP
    pair_lo = jnp.array([a for a in range(EXP_PER_GROUP) for b in range(a + 1, EXP_PER_GROUP)], jnp.int32)
    pair_hi = jnp.array([b for a in range(EXP_PER_GROUP) for b in range(a + 1, EXP_PER_GROUP)], jnp.int32)
    ea = grp * EXP_PER_GROUP + pair_lo[pair]
    eb = grp * EXP_PER_GROUP + pair_hi[pair]
    valid = (tile < n_valid).astype(jnp.int32)
    return pos, src, ea, eb, valid


def _dft_kernel(cs_ref, *, seq, tr):
    k = pl.program_id(0) * tr + lax.broadcasted_iota(jnp.int32, (tr, LANES), 0)
    j = lax.broadcasted_iota(jnp.int32, (tr, LANES), 1)
    period = 2 * seq

    ang = ((k * j) % period).astype(F32) * (math.pi / seq)
    c0 = jnp.cos(ang)
    s0 = jnp.sin(ang)
    ch = c0[:, LANES // 2:LANES // 2 + 1]
    sh = s0[:, LANES // 2:LANES // 2 + 1]
    c_step = 2.0 * ch * ch - 1.0
    s_step = 2.0 * sh * ch
    ca = jnp.ones_like(ch)
    sa = jnp.zeros_like(ch)
    for t1 in range(seq // LANES):
        cols = slice(t1 * LANES, (t1 + 1) * LANES)
        cs_ref[0, :, cols] = (ca * c0 - sa * s0).astype(cs_ref.dtype)
        cs_ref[1, :, cols] = (sa * c0 + ca * s0).astype(cs_ref.dtype)
        ca, sa = ca * c_step - sa * s_step, sa * c_step + ca * s_step


def _dft_cos_sin(seq, n_rows, tr):
    kern = functools.partial(_dft_kernel, seq=seq, tr=tr)
    return pl.pallas_call(
        kern,
        grid=(n_rows // tr,),
        out_specs=pl.BlockSpec((2, tr, seq), lambda r: (0, r, 0)),
        out_shape=jax.ShapeDtypeStruct((2, n_rows, seq), BF16),
        compiler_params=_cparams(("arbitrary",)),
        name="dft_matrix",
    )()


def _positional_features(seq):
    t = jnp.arange(seq, dtype=F32)
    t01 = t / max(seq - 1, 1)
    ang = 2.0 * math.pi * t / seq
    bands = jnp.linspace(1e-4, HY_BANDS - 1, HY_BANDS, dtype=F32)
    pe = jnp.concatenate([t01[:, None], jnp.cos(ang[:, None] * bands), -jnp.sin(ang[:, None] * bands)], axis=-1)
    return jnp.pad(pe, ((0, 0), (0, LANES - pe.shape[1])))


def _mixer_and_router(x3, mod, mod_row, s0f, s0b, n_seg, p, tiles, cnt_in, hg_all, row0, n_total, w_out, casts):
    batch, seq, d = x3.shape
    x = x3.reshape(batch * seq, d)
    dk, dv = p["dk"], p["dv"]
    hy_width = p["hy_width"]
    casted = {}

    proj, lr, *casted["premix"] = _premix_proj(x, mod, mod_row, p["g_pre_mix"], p["w_main"], p["w_lr"],
                                               tiles["tm_proj"], tiles["tn_proj"], casts["premix"])
    if w_out is None:
        w_out = casted["premix"][0]
    y_gla, s_f, s_b, *casted["gla"] = _gla(proj, lr, p["wdf"], p["bdf"], p["wdb"], p["bdb"], p["g_gla"], s0f, s0b,
                                           batch, seq, dk, dv, tiles["gla_heads_per_step"], casts["gla"])

    ad, hl, hm = _hy_filter(_positional_features(seq), p["hy_w1"], p["hy_b1"], p["hy_w2"], p["hy_b2"], p["hy_freq"],
                            p["hy_w3"], p["hy_deltas"], seq, hy_width, tiles["tc_filter"])
    tr, tc_conv = tiles["tr_spec"], tiles["tc_conv"]
    if tiles["hy_split"]:
        half = seq // 2
        hspec = _hy_spectrum(_dft_cos_sin(seq, half, tr), ad, seq, half, 4, hy_width, tr, tiles["tc_spec"])
        cs = _dft_cos_sin(half, half, tr).reshape(seq, half)
        ang = jnp.arange(half, dtype=F32) * (math.pi / seq)
        tw = jnp.broadcast_to(jnp.stack([jnp.cos(ang), jnp.sin(ang)])[:, :, None], (2, half, tc_conv))
    else:
        cs3 = _dft_cos_sin(seq, seq, tr)
        hspec = _hy_spectrum(cs3, ad, seq, seq, 2, hy_width, tr, tiles["tc_spec"])
        cs = cs3.reshape(2 * seq, seq)
        tw = None
    z_hy, *casted["hy_conv"] = _hy_conv(proj, p["hy_conv_w"], p["hy_conv_b"], cs, hspec, hl, hm, tw, p["hy_bias"],
                                        batch, seq, seq // n_seg, p["hy_col0"], hy_width, tc_conv, tiles["hy_split"],
                                        tiles["hy_batch_per_step"], casts["hy_conv"])

    x1, hg_all, route, counts = _mix_out(y_gla.reshape(batch * seq, -1), z_hy.reshape(batch * seq, -1), x, mod,
                                         mod_row, p["g_hy"], p["g_post_mix"], p["g_pre_ffn"], w_out, p["wr_hi"],
                                         p["wr_lo"], p["b_r"], cnt_in, hg_all, row0, n_total, tiles["tm_mix"])
    return x1, hg_all, route, counts, s_f, s_b, casted


def kernel(x_prompt, x_sample, c, state_gla_fwd, state_gla_bwd, c_ctx, w_ada, b_ada, g_pre_mix, g_post_mix, g_pre_ffn, g_post_ffn, w_in, w_dec_f, b_dec_f, w_dec_b, b_dec_b, g_gla, hy_conv_w, hy_conv_b, hy_w1, hy_b1, hy_w2, hy_b2, hy_w3, hy_freq, hy_bias, g_hy, w_out, w_router_grp, b_router_grp, w_router_exp, b_router_exp, w_exp_gate, w_exp_up, w_exp_down):
    depth = w_ada.shape[0]
    assert depth == 1
    l = 0
    d = x_prompt.shape[-1]
    dec_batch = x_sample.shape[0]
    heads = GLA_HEADS
    dk, dv = state_gla_fwd.shape[-2:]
    kdim = heads * dk
    gla_width = heads * dv
    hy_width = g_hy.shape[-1]
    hid = hy_w2.shape[-1]

    cond = jnp.concatenate([c_ctx[None, :], c], axis=0)
    cond = jnp.pad(cond, ((0, -cond.shape[0] % SUBLANES), (0, 0)))
    mod = _ada_mod(cond, w_ada[l], b_ada[l]).reshape(cond.shape[0], 6, d)

    n_main = 2 * kdim + 2 * gla_width
    w_main, w_lr = _repack_w_in(w_in, l, n_main, 2 * GLA_RANK, REPACK_COLS)

    def dec_weight(w, first_row):
        wh = w.reshape(GLA_RANK, heads, dk).transpose(1, 0, 2)
        return jnp.pad(wh, ((0, 0), (first_row, LANES - GLA_RANK - first_row), (0, 0)))

    deltas = jnp.abs(jnp.linspace(math.log(HY_TARGET) / HY_SLOW_PCT, math.log(HY_TARGET) / HY_FAST_PCT, hy_width,
                                  dtype=F32))
    w_r = jnp.pad(jnp.concatenate([w_router_exp[l], w_router_grp[l]], axis=1),
                  ((0, 0), (0, LANES - N_EXPERTS - N_GROUPS)))
    wr_hi, wr_lo = _split2(w_r)
    b_r = jnp.pad(jnp.concatenate([b_router_exp[l], b_router_grp[l]]), (0, LANES - N_EXPERTS - N_GROUPS))

    p = dict(
        dk=dk, dv=dv, hy_width=hy_width, hy_col0=n_main,
        g_pre_mix=g_pre_mix[l][None, :], g_post_mix=g_post_mix[l][None, :],
        g_pre_ffn=g_pre_ffn[l][None, :], g_post_ffn=g_post_ffn[l][None, :],
        w_main=w_main, w_lr=w_lr,
        wdf=dec_weight(w_dec_f[l], 0), bdf=b_dec_f[l][None, :],
        wdb=dec_weight(w_dec_b[l], GLA_RANK), bdb=b_dec_b[l][None, :],
        g_gla=g_gla[l][None, :],
        hy_conv_w=hy_conv_w[l], hy_conv_b=hy_conv_b[l][None, :],
        hy_w1=jnp.pad(hy_w1[l], ((0, LANES - hy_w1.shape[1]), (0, 0))), hy_b1=hy_b1[l][None, :],
        hy_w2=hy_w2[l], hy_b2=hy_b2[l][None, :], hy_freq=hy_freq[l][None, :],
        hy_w3=hy_w3[l].reshape(hid, 2 * HY_ORDER, hy_width).transpose(1, 0, 2), hy_deltas=deltas[None, :],
        hy_bias=hy_bias[l][:, None, :], g_hy=g_hy[l][None, :],
        wr_hi=wr_hi, wr_lo=wr_lo, b_r=b_r[None, :],
    )

    zero_state = jnp.zeros((1, heads, dk, dv), F32)
    dec_seq = x_sample.shape[1]
    tiles_p = _tile_plan(x_prompt.shape[1])
    tiles_s = _tile_plan(dec_seq)
    tm_moe = MXU_DIM
    tm_out = 512

    n_p = x_prompt.shape[0] * x_prompt.shape[1]
    n_s = dec_batch * dec_seq
    n_all = n_p + n_s
    mod_row_p = lambda r: 0
    mod_row_s = lambda r: 1 + r // dec_seq
    n_exp, _, d_exp = w_exp_gate.shape[1:]
    casts_p = dict(premix=[w_out[l]],
                   gla=[w_exp_gate[l].reshape(n_exp * d, d_exp), w_exp_down[l].reshape(n_exp * d_exp, d)],
                   hy_conv=[w_exp_up[l].reshape(n_exp * d, d_exp)])
    casts_s = dict(premix=[], gla=[], hy_conv=[])
    x1_p, hg_all, route_p, counts, s_f, s_b, casted = _mixer_and_router(
        x_prompt, mod, mod_row_p, zero_state, zero_state, 1, p, tiles_p, jnp.zeros((1, LANES), F32), None, 0, n_all,
        None, casts_p)
    (w_out_b,), (w_gate_b, w_down_b), (w_up_b,) = casted["premix"], casted["gla"], casted["hy_conv"]
    x1_s, hg_all, route_s, counts, _, _, _ = _mixer_and_router(
        x_sample, mod, mod_row_s, state_gla_fwd[:, l], state_gla_bwd[:, l], dec_seq // GRID_W, p, tiles_s,
        counts, hg_all, n_p, n_all, w_out_b, casts_s)

    n_tiles = n_all // tm_moe + N_GROUPS * PAIRS_PER_GROUP
    pos, src, ea, eb, valid = _route_tables(jnp.concatenate([route_p, route_s], axis=1), counts, n_tiles, tm_moe)
    y_sorted = _moe(hg_all, src, ea, eb, valid, w_gate_b.reshape(n_exp, d, d_exp), w_up_b.reshape(n_exp, d, d_exp),
                    w_down_b.reshape(n_exp, d_exp, d), n_tiles, tm_moe)
    y_p = _ffn_out(y_sorted, pos[:n_p], x1_p, mod, mod_row_p, p["g_post_ffn"], tm_out)
    y_s = _ffn_out(y_sorted, pos[n_p:], x1_s, mod, mod_row_s, p["g_post_ffn"], tm_out)
    return (y_p.reshape(x_prompt.shape), y_s.reshape(x_sample.shape),
            s_f[:, None].astype(x_prompt.dtype), s_b[:, None].astype(x_prompt.dtype))
```

```python
import functools
import math

import jax
import jax.numpy as jnp
from jax import lax
from jax.experimental import pallas as pl
from jax.experimental.pallas import tpu as pltpu

F32 = jnp.float32
BF16 = jnp.bfloat16

GRID_W = 64
GLA_HEADS = 4
GLA_RANK = 16
GLA_TAU = 16.0
GLA_CHUNK = 64
GLA_UNROLL = 4
MIX_SUBTILES = 1
HY_ORDER = 2
HY_SHORT = 3
HY_BANDS = 16
HY_TARGET = 1e-2
HY_FAST_PCT = 0.3
HY_SLOW_PCT = 1.5
N_GROUPS = 4
EXP_PER_GROUP = 4
N_EXPERTS = N_GROUPS * EXP_PER_GROUP
PAIRS_PER_GROUP = EXP_PER_GROUP * (EXP_PER_GROUP - 1) // 2
ROUTE_BUCKET_LANE = N_EXPERTS
ROUTE_RANK_LANE = N_EXPERTS + 1
EPS = 1e-6

LANES = 128
SUBLANES = 8
MXU_DIM = 256
VMEM_LIMIT = 56 << 20
REPACK_COLS = MXU_DIM
FREQ_BLOCK = 2 * MXU_DIM


def _tile_plan(seq):
    long_seq = seq >= 1024
    return dict(
        tm_proj=1024, tn_proj=2048,
        tc_filter=128 if long_seq else 256,
        tr_spec=512 if long_seq else 256,
        tc_spec=1024 if long_seq else 512,
        hy_split=long_seq,
        tc_conv=256 if long_seq else 512,
        hy_batch_per_step=1 if long_seq else 2,
        gla_heads_per_step=2 if long_seq else 4,
        tm_mix=512,
    )


def _cparams(sem):
    return pltpu.CompilerParams(dimension_semantics=sem, vmem_limit_bytes=VMEM_LIMIT)


def _dot(a, b):
    return jnp.dot(a, b, preferred_element_type=F32)


def _dot_nt(a, b):
    return lax.dot_general(a, b, (((1,), (1,)), ((), ())), preferred_element_type=F32)


def _dot_tn(a, b):
    return lax.dot_general(a, b, (((0,), (0,)), ((), ())), preferred_element_type=F32)


def _split2(x):
    hi = x.astype(BF16)
    lo = (x - hi.astype(F32)).astype(BF16)
    return hi, lo


def _dot_hp(a, b):
    ah, al = a if isinstance(a, tuple) else _split2(a)
    bh, bl = _split2(b)
    return _dot(ah, bh) + (_dot(ah, bl) + _dot(al, bh))


def _dot_exact_lhs(t, x):
    hi, lo = _split2(x)
    return _dot(t, hi) + _dot(t, lo)


def _rms(x, g):
    return x * lax.rsqrt(jnp.mean(x * x, axis=-1, keepdims=True) + EPS) * g


def _silu(x):
    return x / (1.0 + jnp.exp(-x))


def _ada_kernel(c_ref, w_ref, b_ref, o_ref):
    s = _silu(c_ref[...]).astype(BF16)
    o_ref[...] = _dot(s, w_ref[...].astype(BF16)) + b_ref[...]


def _ada_mod(cond, w_ada, b_ada):
    rows, d = cond.shape
    n = w_ada.shape[1]
    tn = 1024
    return pl.pallas_call(
        _ada_kernel,
        grid=(n // tn,),
        in_specs=[pl.BlockSpec((rows, d), lambda j: (0, 0)),
                  pl.BlockSpec((d, tn), lambda j: (0, j)),
                  pl.BlockSpec((1, tn), lambda j: (0, j))],
        out_specs=pl.BlockSpec((rows, tn), lambda j: (0, j)),
        out_shape=jax.ShapeDtypeStruct((rows, n), F32),
        compiler_params=_cparams(("arbitrary",)),
        name="ada_mod",
    )(cond, w_ada, b_ada.reshape(1, n))


def _repack_kernel(wt_hbm, main_ref, lr_ref, buf, lr_buf, sem, lr_sem, *, layer, n_main, n_lr, tb):
    i = pl.program_id(0)
    n_i = pl.num_programs(0)
    slot = i % 2
    d = main_ref.shape[0]

    def block_copy(j, s):
        first = j * tb + jnp.where(j * tb >= n_main, n_lr, 0)
        return pltpu.make_async_copy(wt_hbm.at[layer, pl.ds(first, tb), :], buf.at[s], sem.at[s])

    lr_copy = pltpu.make_async_copy(wt_hbm.at[layer, pl.ds(n_main, n_lr), :], lr_buf, lr_sem)

    @pl.when(i == 0)
    def _():
        block_copy(0, 0).start()
        lr_copy.start()

    @pl.when(i + 1 < n_i)
    def _():
        block_copy(i + 1, 1 - slot).start()

    block_copy(i, slot).wait()
    main_ref[...] = buf[slot].T.astype(BF16)

    @pl.when(i == 0)
    def _():
        lr_copy.wait()
        lr = jnp.concatenate([lr_buf[...], jnp.zeros((LANES - n_lr, d), F32)], axis=0).T
        lr_ref[...] = lr.astype(BF16)


def _repack_w_in(w_in, layer, n_main, n_lr, tb):
    w_t = jnp.swapaxes(w_in, 1, 2)
    _, n_cols, d = w_t.shape
    n_out = n_cols - n_lr
    assert n_main % tb == 0 and n_out % tb == 0
    kern = functools.partial(_repack_kernel, layer=layer, n_main=n_main, n_lr=n_lr, tb=tb)
    return pl.pallas_call(
        kern,
        grid=(n_out // tb,),
        in_specs=[pl.BlockSpec(memory_space=pl.ANY)],
        out_specs=[pl.BlockSpec((d, tb), lambda i: (0, i)),
                   pl.BlockSpec((d, LANES), lambda i: (0, 0))],
        out_shape=[jax.ShapeDtypeStruct((d, n_out), BF16),
                   jax.ShapeDtypeStruct((d, LANES), BF16)],
        scratch_shapes=[pltpu.VMEM((2, tb, d), F32), pltpu.VMEM((n_lr, d), F32),
                        pltpu.SemaphoreType.DMA((2,)), pltpu.SemaphoreType.DMA(())],
        compiler_params=_cparams(("arbitrary",)),
        name="repack_w_in",
    )(w_t)


def _with_cast_riders(kernel_fn, n_in, n_out, n_cast):
    def kernel(*refs):
        ins = refs[:n_in]
        cast_in = refs[n_in:n_in + n_cast]
        outs = refs[n_in + n_cast:n_in + n_cast + n_out]
        cast_out = refs[n_in + n_cast + n_out:n_in + 2 * n_cast + n_out]
        kernel_fn(*ins, *outs, *refs[n_in + 2 * n_cast + n_out:])
        for src, dst in zip(cast_in, cast_out):
            dst[...] = src[...].astype(dst.dtype)

    return kernel


def _cast_rider_specs(cast_srcs, n_steps, step_of):
    n_slabs = 1 << (n_steps.bit_length() - 1)
    specs = []
    for w in cast_srcs:
        rows, cols = w.shape
        assert rows % n_slabs == 0
        specs.append(pl.BlockSpec((rows // n_slabs, cols),
                                  lambda *idx: (jnp.minimum(step_of(*idx), n_slabs - 1), 0)))
    return specs, [jax.ShapeDtypeStruct(w.shape, BF16) for w in cast_srcs]


def _premix_kernel(x_ref, mod_ref, g_ref, w_ref, wlr_ref, o_ref, lr_ref, h_scr):
    @pl.when(pl.program_id(1) == 0)
    def _():
        h = _rms(x_ref[...], g_ref[...] * (1.0 + mod_ref[0, 1:2, :])) + mod_ref[0, 0:1, :]
        h_scr[...] = h.astype(BF16)
        lr_ref[...] = _dot(h_scr[...], wlr_ref[...])

    o_ref[...] = _dot(h_scr[...], w_ref[...]).astype(o_ref.dtype)


def _premix_proj(x, mod, mod_row, g, w_main, w_lr, tm, tn, cast_srcs):
    n, d = x.shape
    nc = w_main.shape[1]
    n_j = nc // tn
    cast_specs, cast_shapes = _cast_rider_specs(cast_srcs, (n // tm) * n_j, lambda i, j: i * n_j + j)
    kern = _with_cast_riders(_premix_kernel, 5, 2, len(cast_srcs))
    return pl.pallas_call(
        kern,
        grid=(n // tm, n_j),
        in_specs=[pl.BlockSpec((tm, d), lambda i, j: (i, 0)),
                  pl.BlockSpec((1, 6, d), lambda i, j: (mod_row(i * tm), 0, 0)),
                  pl.BlockSpec((1, d), lambda i, j: (0, 0)),
                  pl.BlockSpec((d, tn), lambda i, j: (0, j)),
                  pl.BlockSpec((d, LANES), lambda i, j: (0, 0))] + cast_specs,
        out_specs=[pl.BlockSpec((tm, tn), lambda i, j: (i, j)),
                   pl.BlockSpec((tm, LANES), lambda i, j: (i, 0))] + cast_specs,
        out_shape=[jax.ShapeDtypeStruct((n, nc), BF16), jax.ShapeDtypeStruct((n, LANES), F32)] + cast_shapes,
        scratch_shapes=[pltpu.VMEM((tm, d), BF16)],
        compiler_params=_cparams(("arbitrary", "arbitrary")),
        name="premix_proj",
    )(x, mod, g, w_main, w_lr, *cast_srcs)


def _log_sigmoid(x):
    return jnp.minimum(x, 0.0) - jnp.log(1.0 + jnp.exp(-jnp.abs(x)))


def _gla_kernel(q_ref, k_ref, v_ref, g_ref, lr_ref, wdf_ref, bdf_ref, wdb_ref, bdb_ref, gg_ref,
                s0f_ref, s0b_ref, y_ref, sf_ref, sb_ref,
                laf_scr, lab_scr, qf_scr, qb_scr, o_scr, uf_scr, ub_scr, df_scr, db_scr, *, seq, dk, dv, hp):
    c = GLA_CHUNK
    n_chunks = seq // c
    scale = dk ** -0.5
    heads = range(hp)
    kcols = [slice(h * dk, (h + 1) * dk) for h in heads]
    vcols = [slice(h * dv, (h + 1) * dv) for h in heads]

    lr_split = _split2(lr_ref[0])
    for h in heads:
        laf_scr[h] = _log_sigmoid(_dot_hp(lr_split, wdf_ref[h]) + bdf_ref[:, kcols[h]]) / GLA_TAU
        lab_scr[h] = _log_sigmoid(_dot_hp(lr_split, wdb_ref[h]) + bdb_ref[:, kcols[h]]) / GLA_TAU

    per = GLA_UNROLL
    blk = per * c
    row = lax.broadcasted_iota(jnp.int32, (blk, blk), 0)
    col = lax.broadcasted_iota(jnp.int32, (blk, blk), 1)
    same = (row // c) == (col // c)
    lower = same & (row >= col)
    upper = same & (col >= row)
    t_fwd = lower.astype(BF16)
    t_bwd = upper.astype(BF16)

    def chunk_rows(x, r):
        return jnp.concatenate([jnp.broadcast_to(x[j * c + r:j * c + r + 1], (c, dk)) for j in range(per)], axis=0)

    def block_local(m, carry):
        sl = pl.ds(pl.multiple_of(m * blk, blk), blk)
        bf = [_dot_exact_lhs(t_fwd, laf_scr[h, sl, :]) for h in heads]
        bb = [_dot_exact_lhs(t_bwd, lab_scr[h, sl, :]) for h in heads]
        tot_f = [chunk_rows(x, c - 1) for x in bf]
        tot_b = [chunk_rows(x, 0) for x in bb]
        q = [q_ref[0, sl, kcols[h]].astype(F32) * scale for h in heads]
        k = [k_ref[0, sl, kcols[h]].astype(F32) for h in heads]
        v = [v_ref[0, sl, vcols[h]] for h in heads]
        qf = [(q[h] * jnp.exp(bf[h])).astype(BF16) for h in heads]
        kf = [(k[h] * jnp.exp(-bf[h])).astype(BF16) for h in heads]
        qb = [(q[h] * jnp.exp(bb[h])).astype(BF16) for h in heads]
        kb = [(k[h] * jnp.exp(-bb[h])).astype(BF16) for h in heads]
        ksf = [(k[h] * jnp.exp(tot_f[h] - bf[h])).astype(BF16) for h in heads]
        ksb = [(k[h] * jnp.exp(tot_b[h] - bb[h])).astype(BF16) for h in heads]
        sc_f = [_dot_nt(qf[h], kf[h]) for h in heads]
        sc_b = [_dot_nt(qb[h], kb[h]) for h in heads]
        att = [(jnp.where(lower, sc_f[h], 0.0) + jnp.where(upper, sc_b[h], 0.0)).astype(BF16) for h in heads]
        o_loc = [_dot(att[h], v[h]) for h in heads]
        dec_f = [jnp.exp(x) for x in tot_f]
        dec_b = [jnp.exp(x) for x in tot_b]
        for h in heads:
            o_scr[h, sl, :] = o_loc[h]
            qf_scr[h, sl, :] = qf[h]
            qb_scr[h, sl, :] = qb[h]
        for j in range(per):
            n = m * per + j
            rows = slice(j * c, (j + 1) * c)
            for h in heads:
                uf_scr[h, n] = _dot_tn(v[h][rows], ksf[h][rows])
                ub_scr[h, n] = _dot_tn(v[h][rows], ksb[h][rows])
                df_scr[h, n] = dec_f[h][j * c:j * c + SUBLANES]
                db_scr[h, n] = dec_b[h][j * c:j * c + SUBLANES]
        return carry

    lax.fori_loop(0, n_chunks // per, block_local, 0)

    def scan_fwd(n, s):
        upd = [uf_scr[h, n] for h in heads]
        for h in heads:
            uf_scr[h, n] = s[h]
        return tuple(s[h] * df_scr[h, n][0:1, :] + upd[h] for h in heads)

    def scan_bwd(i, s):
        n = n_chunks - 1 - i
        upd = [ub_scr[h, n] for h in heads]
        for h in heads:
            ub_scr[h, n] = s[h]
        return tuple(s[h] * db_scr[h, n][0:1, :] + upd[h] for h in heads)

    s_f = lax.fori_loop(0, n_chunks, scan_fwd, tuple(s0f_ref[0, h].T for h in heads), unroll=GLA_UNROLL)
    s_b = lax.fori_loop(0, n_chunks, scan_bwd, tuple(s0b_ref[0, h].T for h in heads), unroll=GLA_UNROLL)
    for h in heads:
        sf_ref[0, h] = s_f[h].T
        sb_ref[0, h] = s_b[h].T

    def chunk_inter(n, carry):
        sl = pl.ds(pl.multiple_of(n * c, c), c)
        inter = [_dot_nt(qf_scr[h, sl, :], uf_scr[h, n].astype(BF16))
                 + _dot_nt(qb_scr[h, sl, :], ub_scr[h, n].astype(BF16)) for h in heads]
        for h in heads:
            o_scr[h, sl, :] += inter[h]
        return carry

    lax.fori_loop(0, n_chunks, chunk_inter, 0, unroll=GLA_UNROLL)

    for h in heads:
        o = _rms(o_scr[h], gg_ref[...])
        y_ref[0, :, vcols[h]] = (o * _silu(g_ref[0, :, vcols[h]].astype(F32))).astype(y_ref.dtype)


def _gla(proj, lr, wdf, bdf, wdb, bdb, g_gla, s0f, s0b, batch, seq, dk, dv, hp, cast_srcs):
    heads = GLA_HEADS
    proj3 = proj.reshape(batch, seq, proj.shape[-1])
    lr3 = lr.reshape(batch, seq, LANES)
    kdim = heads * dk
    width = heads * dv
    bk, bv = hp * dk, hp * dv
    k_blk = kdim // bk
    v_blk = 2 * kdim // bv
    g_blk = (2 * kdim + width) // bv
    n_chunks = seq // GLA_CHUNK

    def s0_map(s0):
        if s0.shape[0] == batch:
            return lambda b, h: (b, h, 0, 0)
        return lambda b, h: (0, h, 0, 0)

    n_hsteps = heads // hp
    cast_specs, cast_shapes = _cast_rider_specs(cast_srcs, batch * n_hsteps, lambda b, h: b * n_hsteps + h)
    kern = _with_cast_riders(functools.partial(_gla_kernel, seq=seq, dk=dk, dv=dv, hp=hp), 12, 3, len(cast_srcs))
    return pl.pallas_call(
        kern,
        grid=(batch, n_hsteps),
        in_specs=[pl.BlockSpec((1, seq, bk), lambda b, h: (b, 0, h)),
                  pl.BlockSpec((1, seq, bk), lambda b, h: (b, 0, k_blk + h)),
                  pl.BlockSpec((1, seq, bv), lambda b, h: (b, 0, v_blk + h)),
                  pl.BlockSpec((1, seq, bv), lambda b, h: (b, 0, g_blk + h)),
                  pl.BlockSpec((1, seq, LANES), lambda b, h: (b, 0, 0)),
                  pl.BlockSpec((hp, LANES, dk), lambda b, h: (h, 0, 0)),
                  pl.BlockSpec((1, bk), lambda b, h: (0, h)),
                  pl.BlockSpec((hp, LANES, dk), lambda b, h: (h, 0, 0)),
                  pl.BlockSpec((1, bk), lambda b, h: (0, h)),
                  pl.BlockSpec((1, dv), lambda b, h: (0, 0)),
                  pl.BlockSpec((1, hp, dk, dv), s0_map(s0f)),
                  pl.BlockSpec((1, hp, dk, dv), s0_map(s0b))] + cast_specs,
        out_specs=[pl.BlockSpec((1, seq, bv), lambda b, h: (b, 0, h)),
                   pl.BlockSpec((1, hp, dk, dv), lambda b, h: (b, h, 0, 0)),
                   pl.BlockSpec((1, hp, dk, dv), lambda b, h: (b, h, 0, 0))] + cast_specs,
        out_shape=[jax.ShapeDtypeStruct((batch, seq, width), BF16),
                   jax.ShapeDtypeStruct((batch, heads, dk, dv), F32),
                   jax.ShapeDtypeStruct((batch, heads, dk, dv), F32)] + cast_shapes,
        scratch_shapes=[pltpu.VMEM((hp, seq, dk), F32), pltpu.VMEM((hp, seq, dk), F32),
                        pltpu.VMEM((hp, seq, dk), BF16), pltpu.VMEM((hp, seq, dk), BF16),
                        pltpu.VMEM((hp, seq, dv), F32),
                        pltpu.VMEM((hp, n_chunks, dv, dk), F32), pltpu.VMEM((hp, n_chunks, dv, dk), F32),
                        pltpu.VMEM((hp, n_chunks, SUBLANES, dk), F32), pltpu.VMEM((hp, n_chunks, SUBLANES, dk), F32)],
        compiler_params=_cparams(("arbitrary", "arbitrary")),
        name="gla",
    )(proj3, proj3, proj3, proj3, lr3, wdf, bdf, wdb, bdb, g_gla, s0f, s0b, *cast_srcs)


def _filter_kernel(pe_ref, w1_ref, b1_ref, w2_ref, b2_ref, fr_ref, w3_ref, dl_ref, ad_ref, hl_ref, hm_ref, mlp_scr,
                   *, seq):
    @pl.when(pl.program_id(0) == 0)
    def _():
        fr = fr_ref[...]
        h1 = jnp.sin(fr * (_dot_hp(pe_ref[...], w1_ref[...]) + b1_ref[...]))
        mlp_scr[...] = jnp.sin(fr * (_dot_hp(h1, w2_ref[...]) + b2_ref[...]))

    h2 = _split2(mlp_scr[...])
    dec = jnp.exp(-pe_ref[:, 0:1] * dl_ref[...])
    row = lax.broadcasted_iota(jnp.int32, (seq, 1), 0)
    alt = jnp.where(row % 2 == 0, 1.0, -1.0)
    phase = row % 4
    cos_half = jnp.where(phase == 0, 1.0, jnp.where(phase == 2, -1.0, 0.0))
    sin_half = jnp.where(phase == 1, 1.0, jnp.where(phase == 3, -1.0, 0.0))
    for o in range(HY_ORDER):
        ff = _dot_hp(h2, w3_ref[o]) * dec
        fb = _dot_hp(h2, w3_ref[HY_ORDER + o]) * dec
        nrm = jnp.sum(jnp.abs(ff), axis=0, keepdims=True) + jnp.sum(jnp.abs(fb), axis=0, keepdims=True)
        inv = 1.0 / nrm
        ff = ff * inv
        fb = jnp.where(row == 0, 0.0, fb * inv)
        a = ff + fb
        d = fb - ff
        ad_ref[o, 0] = a.astype(ad_ref.dtype)
        ad_ref[o, 1] = d.astype(ad_ref.dtype)
        ad_ref[o, 2] = (alt * a).astype(ad_ref.dtype)
        ad_ref[o, 3] = (-alt * d).astype(ad_ref.dtype)
        hl_ref[o] = jnp.sum(alt * a, axis=0, keepdims=True)
        hm_ref[o] = jnp.concatenate([jnp.sum(cos_half * a, axis=0, keepdims=True),
                                     jnp.sum(sin_half * d, axis=0, keepdims=True)], axis=0) * (1.0 / seq)


def _hy_filter(pe, w1p, b1, w2, b2, freq, w3r, deltas, seq, width, tc):
    hid = w2.shape[0]
    kern = functools.partial(_filter_kernel, seq=seq)
    return pl.pallas_call(
        kern,
        grid=(width // tc,),
        in_specs=[pl.BlockSpec((seq, LANES), lambda c: (0, 0)),
                  pl.BlockSpec((LANES, hid), lambda c: (0, 0)),
                  pl.BlockSpec((1, hid), lambda c: (0, 0)),
                  pl.BlockSpec((hid, hid), lambda c: (0, 0)),
                  pl.BlockSpec((1, hid), lambda c: (0, 0)),
                  pl.BlockSpec((1, hid), lambda c: (0, 0)),
                  pl.BlockSpec((2 * HY_ORDER, hid, tc), lambda c: (0, 0, c)),
                  pl.BlockSpec((1, tc), lambda c: (0, c))],
        out_specs=[pl.BlockSpec((HY_ORDER, 4, seq, tc), lambda c: (0, 0, 0, c)),
                   pl.BlockSpec((HY_ORDER, 1, tc), lambda c: (0, 0, c)),
                   pl.BlockSpec((HY_ORDER, 2, tc), lambda c: (0, 0, c))],
        out_shape=[jax.ShapeDtypeStruct((HY_ORDER, 4, seq, width), BF16),
                   jax.ShapeDtypeStruct((HY_ORDER, 1, width), F32),
                   jax.ShapeDtypeStruct((HY_ORDER, 2, width), F32)],
        scratch_shapes=[pltpu.VMEM((seq, hid), F32)],
        compiler_params=_cparams(("arbitrary",)),
        name="hy_filter",
    )(pe, w1p, b1, w2, b2, freq, w3r, deltas)


def _spectrum_kernel(cs_ref, ad_ref, h_ref, *, seq, per_part, tr):
    r = pl.program_id(1)
    k = (r % per_part) * tr + lax.broadcasted_iota(jnp.int32, (tr, 1), 0)
    wgt = jnp.where(k == 0, 1.0, 2.0) * (0.5 / seq)
    h_ref[0] = wgt * _dot(cs_ref[0], ad_ref[0, 0])


def _hy_spectrum(cs3, ad, seq, part_rows, n_parts, width, tr, tc):
    per_part = part_rows // tr
    kern = functools.partial(_spectrum_kernel, seq=seq, per_part=per_part, tr=tr)
    return pl.pallas_call(
        kern,
        grid=(HY_ORDER, n_parts * per_part, width // tc),
        in_specs=[pl.BlockSpec((1, tr, seq), lambda o, r, c: ((r // per_part) % 2, r % per_part, 0)),
                  pl.BlockSpec((1, 1, seq, tc), lambda o, r, c: (o, r // per_part, 0, c))],
        out_specs=pl.BlockSpec((1, tr, tc), lambda o, r, c: (o, r, c)),
        out_shape=jax.ShapeDtypeStruct((HY_ORDER, n_parts * part_rows, width), F32),
        compiler_params=_cparams(("arbitrary", "arbitrary", "arbitrary")),
        name="hy_spectrum",
    )(cs3, ad)


def _short_conv_fn(seq, seg):
    row = lax.broadcasted_iota(jnp.int32, (seq, 1), 0)
    pos = row % seg

    def short_conv(u_ref, cw_ref, cb_ref, b=0):
        u = u_ref[b].astype(F32)
        prev = jnp.where(pos == 0, 0.0, pltpu.roll(u, 1, 0))
        nxt = jnp.where(pos == seg - 1, 0.0, pltpu.roll(u, seq - 1, 0))
        return prev * cw_ref[0:1, :] + u * cw_ref[1:2, :] + nxt * cw_ref[2:3, :] + cb_ref[...]

    return short_conv


def _hyconv_kernel(u0_ref, u1_ref, u2_ref, cw0_ref, cw1_ref, cw2_ref, cb0_ref, cb1_ref, cb2_ref,
                   cs_ref, h_ref, hl_ref, bias_ref, z_ref, y_scr, *, seq, seg):
    short_conv = _short_conv_fn(seq, seg)
    row = lax.broadcasted_iota(jnp.int32, (seq, 1), 0)
    alt = jnp.where(row % 2 == 0, 1.0, -1.0)
    gates = ((u1_ref, cw1_ref, cb1_ref), (u2_ref, cw2_ref, cb2_ref))
    items = range(z_ref.shape[0])
    z = [short_conv(u0_ref, cw0_ref, cb0_ref, b) for b in items]
    kb = min(seq, FREQ_BLOCK)
    for n in range(HY_ORDER):
        zb = [z[b].astype(BF16) for b in items]
        for r in range(0, seq, kb):
            xc = [_dot(cs_ref[r:r + kb, :], zb[b]) for b in items]
            xs = [_dot(cs_ref[seq + r:seq + r + kb, :], zb[b]) for b in items]
            hre = h_ref[n, r:r + kb, :]
            him = h_ref[n, seq + r:seq + r + kb, :]
            for b in items:
                y_scr[b, r:r + kb, :] = (xc[b] * hre + xs[b] * him).astype(BF16)
                y_scr[b, seq + r:seq + r + kb, :] = (xs[b] * hre - xc[b] * him).astype(BF16)
        nyq = [jnp.sum(alt * z[b], axis=0, keepdims=True) * (hl_ref[n] * (0.5 / seq)) for b in items]
        conv = [_dot(cs_ref[:seq, :], y_scr[b, :seq, :]) + _dot(cs_ref[seq:, :], y_scr[b, seq:, :]) + alt * nyq[b]
                for b in items]
        gate = [short_conv(*gates[n], b) for b in items]
        z = [gate[b] * (conv[b] + bias_ref[n] * z[b]) for b in items]
    for b in items:
        z_ref[b] = z[b].astype(z_ref.dtype)


def _hyconv_split_kernel(u0_ref, u1_ref, u2_ref, cw0_ref, cw1_ref, cw2_ref, cb0_ref, cb1_ref, cb2_ref,
                         cs_ref, h_ref, hm_ref, tw_ref, bias_ref, z_ref, g_scr, z_scr, c_scr, *, seq, seg):
    m = seq // 2
    tc = z_ref.shape[2]
    n_lane_blocks = tc // LANES
    short_conv = _short_conv_fn(seq, seg)
    row = lax.broadcasted_iota(jnp.int32, (m, 1), 0)
    alt = jnp.where(row % 2 == 0, 1.0, -1.0)
    gates = ((u1_ref, cw1_ref, cb1_ref), (u2_ref, cw2_ref, cb2_ref))
    z = short_conv(u0_ref, cw0_ref, cb0_ref)
    kb = min(m, FREQ_BLOCK)
    for n in range(HY_ORDER):
        for j in range(n_lane_blocks):
            z_scr[j] = z[:, j * LANES:(j + 1) * LANES]
        ze = jnp.concatenate([z_scr[j, pl.ds(0, m, stride=2), :] for j in range(n_lane_blocks)], axis=1)
        zo = jnp.concatenate([z_scr[j, pl.ds(1, m, stride=2), :] for j in range(n_lane_blocks)], axis=1)
        e_mid = jnp.sum(alt * ze, axis=0, keepdims=True)
        o_mid = jnp.sum(alt * zo, axis=0, keepdims=True)
        zeb = ze.astype(BF16)
        zob = zo.astype(BF16)
        for r in range(0, m, kb):
            rows = slice(r, r + kb)
            srows = slice(m + r, m + r + kb)
            ec = _dot(cs_ref[rows, :], zeb)
            es = _dot(cs_ref[srows, :], zeb)
            oc = _dot(cs_ref[rows, :], zob)
            os_ = _dot(cs_ref[srows, :], zob)
            c = tw_ref[0, rows, :]
            s = tw_ref[1, rows, :]
            pc = c * oc - s * os_
            ps = c * os_ + s * oc
            xca, xsa = ec + pc, es + ps
            xcb, xsb = ec - pc, ps - es
            har = h_ref[n, rows, :]
            hai = h_ref[n, srows, :]
            hbr = h_ref[n, 2 * m + r:2 * m + r + kb, :]
            hbi = h_ref[n, 3 * m + r:3 * m + r + kb, :]
            yar = xca * har + xsa * hai
            yai = xca * hai - xsa * har
            ybr = xcb * hbr + xsb * hbi
            ybi = xcb * hbi - xsb * hbr
            dr = yar - ybr
            di = yai + ybi
            g_scr[0, rows, :] = (yar + ybr).astype(BF16)
            g_scr[0, srows, :] = (ybi - yai).astype(BF16)
            g_scr[1, rows, :] = (c * dr - s * di).astype(BF16)
            g_scr[1, srows, :] = (-(s * dr + c * di)).astype(BF16)
        hr = hm_ref[n, 0:1, :]
        hi = hm_ref[n, 1:2, :]
        ymr = e_mid * hr + o_mid * hi
        ymi = e_mid * hi - o_mid * hr
        y_even = _dot(cs_ref[:m, :], g_scr[0, :m, :]) + _dot(cs_ref[m:, :], g_scr[0, m:, :]) + alt * ymr
        y_odd = _dot(cs_ref[:m, :], g_scr[1, :m, :]) + _dot(cs_ref[m:, :], g_scr[1, m:, :]) - alt * ymi
        for j in range(n_lane_blocks):
            c_scr[j, pl.ds(0, m, stride=2), :] = y_even[:, j * LANES:(j + 1) * LANES]
            c_scr[j, pl.ds(1, m, stride=2), :] = y_odd[:, j * LANES:(j + 1) * LANES]
        conv = jnp.concatenate([c_scr[j] for j in range(n_lane_blocks)], axis=1)
        z = short_conv(*gates[n]) * (conv + bias_ref[n] * z)
    z_ref[0] = z.astype(z_ref.dtype)


def _hy_conv(proj, conv_w, conv_b, cs, hspec, hl, hm, tw, bias, batch, seq, seg, col0, width, tc, split, bp,
             cast_srcs):
    proj3 = proj.reshape(batch, seq, proj.shape[-1])
    blk0 = col0 // tc
    per = width // tc
    n_b = batch // bp
    assert bp == 1 or not split
    once = dict(pipeline_mode=pl.Buffered(1))

    def u_spec(p):
        return pl.BlockSpec((bp, seq, tc), lambda c, b: (b, 0, blk0 + p * per + c))

    def w_spec(p, rows):
        return pl.BlockSpec((rows, tc), lambda c, b: (0, p * per + c))

    in_specs = [u_spec(0), u_spec(1), u_spec(2),
                w_spec(0, HY_SHORT), w_spec(1, HY_SHORT), w_spec(2, HY_SHORT),
                w_spec(0, 1), w_spec(1, 1), w_spec(2, 1),
                pl.BlockSpec(cs.shape, lambda c, b: (0, 0), **once),
                pl.BlockSpec((HY_ORDER, 2 * seq, tc), lambda c, b: (0, 0, c), **once)]
    bias_spec = pl.BlockSpec((HY_ORDER, 1, tc), lambda c, b: (0, 0, c))
    operands = [proj3, proj3, proj3, conv_w, conv_w, conv_w, conv_b, conv_b, conv_b, cs, hspec]
    if split:
        kern = functools.partial(_hyconv_split_kernel, seq=seq, seg=seg)
        in_specs += [pl.BlockSpec((HY_ORDER, 2, tc), lambda c, b: (0, 0, c)),
                     pl.BlockSpec((2, seq // 2, tc), lambda c, b: (0, 0, 0)), bias_spec]
        operands += [hm, tw, bias]
        scratch = [pltpu.VMEM((2, seq, tc), BF16), pltpu.VMEM((tc // LANES, seq, LANES), F32),
                   pltpu.VMEM((tc // LANES, seq, LANES), F32)]
    else:
        kern = functools.partial(_hyconv_kernel, seq=seq, seg=seg)
        in_specs += [pl.BlockSpec((HY_ORDER, 1, tc), lambda c, b: (0, 0, c)), bias_spec]
        operands += [hl, bias]
        scratch = [pltpu.VMEM((bp, 2 * seq, tc), BF16)]
    cast_specs, cast_shapes = _cast_rider_specs(cast_srcs, per * n_b, lambda c, b: c * n_b + b)
    kern = _with_cast_riders(kern, len(operands), 1, len(cast_srcs))
    return pl.pallas_call(
        kern,
        grid=(per, n_b),
        in_specs=in_specs + cast_specs,
        out_specs=[pl.BlockSpec((bp, seq, tc), lambda c, b: (b, 0, c))] + cast_specs,
        out_shape=[jax.ShapeDtypeStruct((batch, seq, width), BF16)] + cast_shapes,
        scratch_shapes=scratch,
        compiler_params=_cparams(("arbitrary", "arbitrary")),
        name="hy_conv",
    )(*operands, *cast_srcs)


def _mixout_kernel(*refs, gla_width, n_own, aliased):
    (yg_ref, zh_ref, x_ref, mod_ref, ghy_ref, gpost_ref, gpre_ref, wo_ref, wrh_ref, wrl_ref, br_ref,
     cnt_in_ref) = refs[:12]
    x1_ref, hg_ref, route_ref, cnt_ref, cnt_scr = refs[12 + int(aliased):]
    d = x_ref.shape[1]
    tm = x_ref.shape[0] // MIX_SUBTILES
    i = pl.program_id(0)

    @pl.when(i == 0)
    def _():
        cnt_scr[...] = cnt_in_ref[...]

    @pl.when(i < n_own)
    def _():
        for s in range(MIX_SUBTILES):
            _mixout_rows(slice(s * tm, (s + 1) * tm), tm, d, gla_width, yg_ref, zh_ref, x_ref, mod_ref, ghy_ref,
                         gpost_ref, gpre_ref, wo_ref, wrh_ref, wrl_ref, br_ref, x1_ref, hg_ref, route_ref, cnt_scr)

    @pl.when(i >= n_own)
    def _():
        hg_ref[...] = jnp.zeros_like(hg_ref)

    cnt_ref[...] = cnt_scr[...]


def _mixout_rows(rows, tm, d, gla_width, yg_ref, zh_ref, x_ref, mod_ref, ghy_ref, gpost_ref, gpre_ref, wo_ref,
                 wrh_ref, wrl_ref, br_ref, x1_ref, hg_ref, route_ref, cnt_scr):
    yh = _rms(zh_ref[rows, :].astype(F32), ghy_ref[...]).astype(BF16)
    y = _dot(yg_ref[rows, :], wo_ref[:gla_width, :]) + _dot(yh, wo_ref[gla_width:, :])
    x1 = x_ref[rows, :] + _rms(y, gpost_ref[...] * mod_ref[0, 2:3, :])
    x1_ref[rows, :] = x1
    h2 = _rms(x1, gpre_ref[...] * (1.0 + mod_ref[0, 4:5, :])) + mod_ref[0, 3:4, :]
    hh, hl = _split2(h2)
    hg_ref[rows, :d] = h2
    logits = _dot(hh, wrh_ref[...]) + (_dot(hh, wrl_ref[...]) + _dot(hl, wrh_ref[...])) + br_ref[...]

    lane = lax.broadcasted_iota(jnp.int32, logits.shape, 1).astype(F32)
    neg = -jnp.inf
    is_grp = (lane >= N_EXPERTS) & (lane < N_EXPERTS + N_GROUPS)
    m = jnp.max(jnp.where(is_grp, logits, neg), axis=1, keepdims=True)
    p_grp = 1.0 / jnp.sum(jnp.where(is_grp, jnp.exp(logits - m), 0.0), axis=1, keepdims=True)
    grp = jnp.min(jnp.where(is_grp & (logits == m), lane - N_EXPERTS, 1e9), axis=1, keepdims=True)
    sel = (lane >= grp * EXP_PER_GROUP) & (lane < (grp + 1.0) * EXP_PER_GROUP)
    me = jnp.max(jnp.where(sel, logits, neg), axis=1, keepdims=True)
    pe = jnp.where(sel, jnp.exp(logits - me), -1.0)
    v1 = jnp.max(pe, axis=1, keepdims=True)
    i1 = jnp.min(jnp.where(pe == v1, lane, 1e9), axis=1, keepdims=True)
    pe2 = jnp.where(lane == i1, -1.0, pe)
    v2 = jnp.max(pe2, axis=1, keepdims=True)
    i2 = jnp.min(jnp.where(pe2 == v2, lane, 1e9), axis=1, keepdims=True)
    den = v1 + v2
    gates = jnp.where(lane == i1, v1 / den, jnp.where(lane == i2, v2 / den, 0.0)) * p_grp

    lo = jnp.minimum(i1, i2) - grp * EXP_PER_GROUP
    hi = jnp.maximum(i1, i2) - grp * EXP_PER_GROUP
    bucket = grp * PAIRS_PER_GROUP + lo * (2 * EXP_PER_GROUP - 1 - lo) * 0.5 + (hi - lo - 1.0)
    onehot = lane == bucket
    r_i = lax.broadcasted_iota(jnp.int32, (tm, tm), 0)
    c_i = lax.broadcasted_iota(jnp.int32, (tm, tm), 1)
    earlier = _dot((r_i > c_i).astype(BF16), onehot.astype(BF16)) + cnt_scr[...]
    rank = jnp.sum(jnp.where(onehot, earlier, 0.0), axis=1, keepdims=True)
    cnt_scr[...] += jnp.sum(onehot.astype(F32), axis=0, keepdims=True)
    route = jnp.where(lane == ROUTE_BUCKET_LANE, bucket, jnp.where(lane == ROUTE_RANK_LANE, rank, gates))
    hg_ref[rows, d:] = route
    route_ref[:, rows] = route.T[ROUTE_BUCKET_LANE:ROUTE_BUCKET_LANE + SUBLANES, :]


def _mix_out(yg, zh, x, mod, mod_row, g_hy, g_post, g_pre, w_out, wr_hi, wr_lo, b_r, cnt_in, hg_all, row0, n_total, tm):
    n, d = x.shape
    gw = yg.shape[1]
    hw = zh.shape[1]
    aliased = hg_all is not None
    n_own = n // tm
    n_steps = n_own if aliased else n_total // tm
    assert row0 % tm == 0 and n_total % tm == 0 and (row0 + n == n_total if aliased else row0 == 0)
    kern = functools.partial(_mixout_kernel, gla_width=gw, n_own=n_own, aliased=aliased)
    row = lambda i: (jnp.minimum(i, n_own - 1), 0)
    fixed = lambda i: (0, 0)
    in_specs = [pl.BlockSpec((tm, gw), row),
                pl.BlockSpec((tm, hw), row),
                pl.BlockSpec((tm, d), row),
                pl.BlockSpec((1, 6, d), lambda i: (mod_row(jnp.minimum(i, n_own - 1) * tm), 0, 0)),
                pl.BlockSpec((1, hw), fixed),
                pl.BlockSpec((1, d), fixed),
                pl.BlockSpec((1, d), fixed),
                pl.BlockSpec((gw + hw, d), fixed),
                pl.BlockSpec((d, LANES), fixed),
                pl.BlockSpec((d, LANES), fixed),
                pl.BlockSpec((1, LANES), fixed),
                pl.BlockSpec((1, LANES), fixed)]
    operands = [yg, zh, x, mod, g_hy, g_post, g_pre, w_out, wr_hi, wr_lo, b_r, cnt_in]
    if aliased:
        in_specs.append(pl.BlockSpec(memory_space=pl.ANY))
        operands.append(hg_all)
    return pl.pallas_call(
        kern,
        grid=(n_steps,),
        in_specs=in_specs,
        out_specs=[pl.BlockSpec((tm, d), row),
                   pl.BlockSpec((tm, d + LANES), lambda i: (row0 // tm + i, 0)),
                   pl.BlockSpec((SUBLANES, tm), lambda i: (0, jnp.minimum(i, n_own - 1))),
                   pl.BlockSpec((1, LANES), fixed)],
        out_shape=[jax.ShapeDtypeStruct((n, d), F32),
                   jax.ShapeDtypeStruct((n_total, d + LANES), F32),
                   jax.ShapeDtypeStruct((SUBLANES, n), F32),
                   jax.ShapeDtypeStruct((1, LANES), F32)],
        scratch_shapes=[pltpu.VMEM((1, LANES), F32)],
        input_output_aliases={len(operands) - 1: 1} if aliased else {},
        compiler_params=_cparams(("arbitrary",)),
        name="mix_out",
    )(*operands)


def _row_gather_copy(src_hbm, row, buf, slot, r, sem):
    return pltpu.make_async_copy(src_hbm.at[pl.ds(row, 1), :], buf.at[slot, pl.ds(r, 1), :], sem.at[slot])


def _start_row_gather(idx_ref, base, src_hbm, buf, slot, sem, tm):
    def body(r, carry):
        _row_gather_copy(src_hbm, idx_ref[base + r], buf, slot, r, sem).start()
        return carry

    lax.fori_loop(0, tm, body, 0, unroll=8)


def _start_row_gather_inline(idx_ref, base, src_hbm, buf, slot, sem, tm):
    for r in range(tm):
        _row_gather_copy(src_hbm, idx_ref[base + r], buf, slot, r, sem).start()


def _wait_row_gather(src_hbm, buf, slot, sem, tm):
    pltpu.make_async_copy(src_hbm.at[pl.ds(0, tm), :], buf.at[slot], sem.at[slot]).wait()


def _moe_kernel(src_ref, ea_ref, eb_ref, valid_ref, hg_hbm,
                wga_ref, wua_ref, wda_ref, wgb_ref, wub_ref, wdb_ref, y_ref, buf, sem, *, tm, d):
    t = pl.program_id(0)
    slot = t % 2

    @pl.when(t == 0)
    def _():
        _start_row_gather(src_ref, 0, hg_hbm, buf, 0, sem, tm)

    def used_tile(slot):
        _wait_row_gather(hg_hbm, buf, slot, sem, tm)
        _start_row_gather_inline(src_ref, (t + 1) * tm, hg_hbm, buf, 1 - slot, sem, tm)
        h = buf[slot, :, :d].astype(BF16)
        gates = buf[slot, :, d:]
        lane = lax.broadcasted_iota(jnp.int32, gates.shape, 1)

        experts = ((wga_ref, wua_ref, wda_ref, ea_ref[t]), (wgb_ref, wub_ref, wdb_ref, eb_ref[t]))
        pre = [_dot(h, wg_ref[0]) for wg_ref, _, _, _ in experts]
        up = [_dot(h, wu_ref[0]) for _, wu_ref, _, _ in experts]
        he = [(_silu(pre[i]) * up[i]).astype(BF16) for i in range(2)]
        gate = [jnp.sum(jnp.where(lane == e, gates, 0.0), axis=1, keepdims=True) for _, _, _, e in experts]
        out = [gate[i] * _dot(he[i], experts[i][2][0]) for i in range(2)]
        y_ref[...] = out[0] + out[1]

    for parity in range(2):
        pl.when((valid_ref[t] == 1) & (slot == parity))(functools.partial(used_tile, parity))

    @pl.when(valid_ref[t] == 0)
    def _():
        @pl.when(valid_ref[jnp.maximum(t - 1, 0)] == 1)
        def _():
            _wait_row_gather(hg_hbm, buf, slot, sem, tm)

        y_ref[...] = jnp.zeros_like(y_ref)


def _moe(hg, src, ea, eb, valid, wg, wu, wd, n_tiles, tm):
    d = hg.shape[1] - LANES
    _, _, de = wg.shape
    kern = functools.partial(_moe_kernel, tm=tm, d=d)
    wa = lambda t, src, ea, eb, valid: (ea[t], 0, 0)
    wb = lambda t, src, ea, eb, valid: (eb[t], 0, 0)
    return pl.pallas_call(
        kern,
        grid_spec=pltpu.PrefetchScalarGridSpec(
            num_scalar_prefetch=4,
            grid=(n_tiles,),
            in_specs=[pl.BlockSpec(memory_space=pl.ANY),
                      pl.BlockSpec((1, d, de), wa), pl.BlockSpec((1, d, de), wa), pl.BlockSpec((1, de, d), wa),
                      pl.BlockSpec((1, d, de), wb), pl.BlockSpec((1, d, de), wb), pl.BlockSpec((1, de, d), wb)],
            out_specs=pl.BlockSpec((tm, d), lambda t, src, ea, eb, valid: (t, 0)),
            scratch_shapes=[pltpu.VMEM((2, tm, d + LANES), F32), pltpu.SemaphoreType.DMA((2,))]),
        out_shape=jax.ShapeDtypeStruct((n_tiles * tm, d), F32),
        compiler_params=_cparams(("arbitrary",)),
        name="moe",
    )(src, ea, eb, valid, hg, wg, wu, wd, wg, wu, wd)


def _ffn_out_kernel(pos_ref, y_hbm, x1_ref, mod_ref, gpost_ref, o_ref, buf, sem, *, tm):
    i = pl.program_id(0)
    n_i = pl.num_programs(0)
    slot = i % 2

    @pl.when(i == 0)
    def _():
        _start_row_gather(pos_ref, 0, y_hbm, buf, 0, sem, tm)

    _wait_row_gather(y_hbm, buf, slot, sem, tm)

    @pl.when(i + 1 < n_i)
    def _():
        _start_row_gather_inline(pos_ref, (i + 1) * tm, y_hbm, buf, 1 - slot, sem, tm)
        o_ref[...] = x1_ref[...] + _rms(buf[slot], gpost_ref[...] * mod_ref[0, 5:6, :])

    @pl.when(i + 1 == n_i)
    def _():
        o_ref[...] = x1_ref[...] + _rms(buf[slot], gpost_ref[...] * mod_ref[0, 5:6, :])


def _ffn_out(y_sorted, pos, x1, mod, mod_row, g_post, tm):
    n, d = x1.shape
    kern = functools.partial(_ffn_out_kernel, tm=tm)
    return pl.pallas_call(
        kern,
        grid_spec=pltpu.PrefetchScalarGridSpec(
            num_scalar_prefetch=1,
            grid=(n // tm,),
            in_specs=[pl.BlockSpec(memory_space=pl.ANY),
                      pl.BlockSpec((tm, d), lambda i, pos: (i, 0)),
                      pl.BlockSpec((1, 6, d), lambda i, pos: (mod_row(i * tm), 0, 0)),
                      pl.BlockSpec((1, d), lambda i, pos: (0, 0))],
            out_specs=pl.BlockSpec((tm, d), lambda i, pos: (i, 0)),
            scratch_shapes=[pltpu.VMEM((2, tm, d), F32), pltpu.SemaphoreType.DMA((2,))]),
        out_shape=jax.ShapeDtypeStruct((n, d), F32),
        compiler_params=_cparams(("arbitrary",)),
        name="ffn_out",
    )(pos, y_sorted, x1, mod, g_post)


def _inverse_permutation_kernel(pos_ref, src_ref):
    n = pos_ref.shape[0]
    n_pad = src_ref.shape[0]

    def fill(p, carry):
        src_ref[p] = p
        return carry

    def fill_tail(p, carry):
        src_ref[n + p] = p
        return carry

    def claim(t, carry):
        src_ref[pos_ref[t]] = t
        return carry

    lax.fori_loop(0, n, fill, 0, unroll=16)
    lax.fori_loop(0, n_pad - n, fill_tail, 0, unroll=16)
    lax.fori_loop(0, n, claim, 0, unroll=16)


def _inverse_permutation(pos, n_pad):
    assert pos.shape[0] <= n_pad <= 2 * pos.shape[0]
    return pl.pallas_call(
        _inverse_permutation_kernel,
        in_specs=[pl.BlockSpec(memory_space=pltpu.SMEM)],
        out_specs=pl.BlockSpec(memory_space=pltpu.SMEM),
        out_shape=jax.ShapeDtypeStruct((n_pad,), jnp.int32),
        name="inverse_permutation",
    )(pos)


def _route_tables(route, counts, n_tiles, tm):
    n_buckets = N_GROUPS * PAIRS_PER_GROUP
    cnt = counts[0, :n_buckets].astype(jnp.int32)
    padded = (cnt + tm - 1) // tm * tm
    ends = jnp.cumsum(padded)
    starts = ends - padded
    bucket = route[0].astype(jnp.int32)
    rank = route[ROUTE_RANK_LANE - ROUTE_BUCKET_LANE].astype(jnp.int32)
    is_bucket = bucket[:, None] == jnp.arange(n_buckets, dtype=jnp.int32)[None, :]
    pos = jnp.sum(jnp.where(is_bucket, starts[None, :], 0), axis=1) + rank
    src = _inverse_permutation(pos, n_tiles * tm)
    n_valid = ends[-1] // tm
    tile = jnp.arange(n_tiles, dtype=jnp.int32)
    used = jnp.minimum(tile, n_valid - 1)
    tile_bucket = jnp.sum((ends[None, :] <= (used * tm)[:, None]).astype(jnp.int32), axis=1)
    grp = tile_bucket // PAIRS_PER_GROUP
    pair = tile_bucket % PAIRS_PER_GROUP
    pair_lo = jnp.array([a for a in range(EXP_PER_GROUP) for b in range(a + 1, EXP_PER_GROUP)], jnp.int32)
    pair_hi = jnp.array([b for a in range(EXP_PER_GROUP) for b in range(a + 1, EXP_PER_GROUP)], jnp.int32)
    ea = grp * EXP_PER_GROUP + pair_lo[pair]
    eb = grp * EXP_PER_GROUP + pair_hi[pair]
    valid = (tile < n_valid).astype(jnp.int32)
    return pos, src, ea, eb, valid


def _dft_kernel(cs_ref, *, seq, tr):
    k = pl.program_id(0) * tr + lax.broadcasted_iota(jnp.int32, (tr, LANES), 0)
    j = lax.broadcasted_iota(jnp.int32, (tr, LANES), 1)
    period = 2 * seq

    ang = ((k * j) % period).astype(F32) * (math.pi / seq)
    c0 = jnp.cos(ang)
    s0 = jnp.sin(ang)
    ch = c0[:, LANES // 2:LANES // 2 + 1]
    sh = s0[:, LANES // 2:LANES // 2 + 1]
    c_step = 2.0 * ch * ch - 1.0
    s_step = 2.0 * sh * ch
    ca = jnp.ones_like(ch)
    sa = jnp.zeros_like(ch)
    for t1 in range(seq // LANES):
        cols = slice(t1 * LANES, (t1 + 1) * LANES)
        cs_ref[0, :, cols] = (ca * c0 - sa * s0).astype(cs_ref.dtype)
        cs_ref[1, :, cols] = (sa * c0 + ca * s0).astype(cs_ref.dtype)
        ca, sa = ca * c_step - sa * s_step, sa * c_step + ca * s_step


def _dft_cos_sin(seq, n_rows, tr):
    kern = functools.partial(_dft_kernel, seq=seq, tr=tr)
    return pl.pallas_call(
        kern,
        grid=(n_rows // tr,),
        out_specs=pl.BlockSpec((2, tr, seq), lambda r: (0, r, 0)),
        out_shape=jax.ShapeDtypeStruct((2, n_rows, seq), BF16),
        compiler_params=_cparams(("arbitrary",)),
        name="dft_matrix",
    )()


def _positional_features(seq):
    t = jnp.arange(seq, dtype=F32)
    t01 = t / max(seq - 1, 1)
    ang = 2.0 * math.pi * t / seq
    bands = jnp.linspace(1e-4, HY_BANDS - 1, HY_BANDS, dtype=F32)
    pe = jnp.concatenate([t01[:, None], jnp.cos(ang[:, None] * bands), -jnp.sin(ang[:, None] * bands)], axis=-1)
    return jnp.pad(pe, ((0, 0), (0, LANES - pe.shape[1])))


def _mixer_and_router(x3, mod, mod_row, s0f, s0b, n_seg, p, tiles, cnt_in, hg_all, row0, n_total, w_out, casts):
    batch, seq, d = x3.shape
    x = x3.reshape(batch * seq, d)
    dk, dv = p["dk"], p["dv"]
    hy_width = p["hy_width"]
    casted = {}

    proj, lr, *casted["premix"] = _premix_proj(x, mod, mod_row, p["g_pre_mix"], p["w_main"], p["w_lr"],
                                               tiles["tm_proj"], tiles["tn_proj"], casts["premix"])
    if w_out is None:
        w_out = casted["premix"][0]
    y_gla, s_f, s_b, *casted["gla"] = _gla(proj, lr, p["wdf"], p["bdf"], p["wdb"], p["bdb"], p["g_gla"], s0f, s0b,
                                           batch, seq, dk, dv, tiles["gla_heads_per_step"], casts["gla"])

    ad, hl, hm = _hy_filter(_positional_features(seq), p["hy_w1"], p["hy_b1"], p["hy_w2"], p["hy_b2"], p["hy_freq"],
                            p["hy_w3"], p["hy_deltas"], seq, hy_width, tiles["tc_filter"])
    tr, tc_conv = tiles["tr_spec"], tiles["tc_conv"]
    if tiles["hy_split"]:
        half = seq // 2
        hspec = _hy_spectrum(_dft_cos_sin(seq, half, tr), ad, seq, half, 4, hy_width, tr, tiles["tc_spec"])
        cs = _dft_cos_sin(half, half, tr).reshape(seq, half)
        ang = jnp.arange(half, dtype=F32) * (math.pi / seq)
        tw = jnp.broadcast_to(jnp.stack([jnp.cos(ang), jnp.sin(ang)])[:, :, None], (2, half, tc_conv))
    else:
        cs3 = _dft_cos_sin(seq, seq, tr)
        hspec = _hy_spectrum(cs3, ad, seq, seq, 2, hy_width, tr, tiles["tc_spec"])
        cs = cs3.reshape(2 * seq, seq)
        tw = None
    z_hy, *casted["hy_conv"] = _hy_conv(proj, p["hy_conv_w"], p["hy_conv_b"], cs, hspec, hl, hm, tw, p["hy_bias"],
                                        batch, seq, seq // n_seg, p["hy_col0"], hy_width, tc_conv, tiles["hy_split"],
                                        tiles["hy_batch_per_step"], casts["hy_conv"])

    x1, hg_all, route, counts = _mix_out(y_gla.reshape(batch * seq, -1), z_hy.reshape(batch * seq, -1), x, mod,
                                         mod_row, p["g_hy"], p["g_post_mix"], p["g_pre_ffn"], w_out, p["wr_hi"],
                                         p["wr_lo"], p["b_r"], cnt_in, hg_all, row0, n_total, tiles["tm_mix"])
    return x1, hg_all, route, counts, s_f, s_b, casted


def kernel(x_prompt, x_sample, c, state_gla_fwd, state_gla_bwd, c_ctx, w_ada, b_ada, g_pre_mix, g_post_mix, g_pre_ffn, g_post_ffn, w_in, w_dec_f, b_dec_f, w_dec_b, b_dec_b, g_gla, hy_conv_w, hy_conv_b, hy_w1, hy_b1, hy_w2, hy_b2, hy_w3, hy_freq, hy_bias, g_hy, w_out, w_router_grp, b_router_grp, w_router_exp, b_router_exp, w_exp_gate, w_exp_up, w_exp_down):
    depth = w_ada.shape[0]
    assert depth == 1
    l = 0
    d = x_prompt.shape[-1]
    dec_batch = x_sample.shape[0]
    heads = GLA_HEADS
    dk, dv = state_gla_fwd.shape[-2:]
    kdim = heads * dk
    gla_width = heads * dv
    hy_width = g_hy.shape[-1]
    hid = hy_w2.shape[-1]

    cond = jnp.concatenate([c_ctx[None, :], c], axis=0)
    cond = jnp.pad(cond, ((0, -cond.shape[0] % SUBLANES), (0, 0)))
    mod = _ada_mod(cond, w_ada[l], b_ada[l]).reshape(cond.shape[0], 6, d)

    n_main = 2 * kdim + 2 * gla_width
    w_main, w_lr = _repack_w_in(w_in, l, n_main, 2 * GLA_RANK, REPACK_COLS)

    def dec_weight(w, first_row):
        wh = w.reshape(GLA_RANK, heads, dk).transpose(1, 0, 2)
        return jnp.pad(wh, ((0, 0), (first_row, LANES - GLA_RANK - first_row), (0, 0)))

    deltas = jnp.abs(jnp.linspace(math.log(HY_TARGET) / HY_SLOW_PCT, math.log(HY_TARGET) / HY_FAST_PCT, hy_width,
                                  dtype=F32))
    w_r = jnp.pad(jnp.concatenate([w_router_exp[l], w_router_grp[l]], axis=1),
                  ((0, 0), (0, LANES - N_EXPERTS - N_GROUPS)))
    wr_hi, wr_lo = _split2(w_r)
    b_r = jnp.pad(jnp.concatenate([b_router_exp[l], b_router_grp[l]]), (0, LANES - N_EXPERTS - N_GROUPS))

    p = dict(
        dk=dk, dv=dv, hy_width=hy_width, hy_col0=n_main,
        g_pre_mix=g_pre_mix[l][None, :], g_post_mix=g_post_mix[l][None, :],
        g_pre_ffn=g_pre_ffn[l][None, :], g_post_ffn=g_post_ffn[l][None, :],
        w_main=w_main, w_lr=w_lr,
        wdf=dec_weight(w_dec_f[l], 0), bdf=b_dec_f[l][None, :],
        wdb=dec_weight(w_dec_b[l], GLA_RANK), bdb=b_dec_b[l][None, :],
        g_gla=g_gla[l][None, :],
        hy_conv_w=hy_conv_w[l], hy_conv_b=hy_conv_b[l][None, :],
        hy_w1=jnp.pad(hy_w1[l], ((0, LANES - hy_w1.shape[1]), (0, 0))), hy_b1=hy_b1[l][None, :],
        hy_w2=hy_w2[l], hy_b2=hy_b2[l][None, :], hy_freq=hy_freq[l][None, :],
        hy_w3=hy_w3[l].reshape(hid, 2 * HY_ORDER, hy_width).transpose(1, 0, 2), hy_deltas=deltas[None, :],
        hy_bias=hy_bias[l][:, None, :], g_hy=g_hy[l][None, :],
        wr_hi=wr_hi, wr_lo=wr_lo, b_r=b_r[None, :],
    )

    zero_state = jnp.zeros((1, heads, dk, dv), F32)
    dec_seq = x_sample.shape[1]
    tiles_p = _tile_plan(x_prompt.shape[1])
    tiles_s = _tile_plan(dec_seq)
    tm_moe = MXU_DIM
    tm_out = 512

    n_p = x_prompt.shape[0] * x_prompt.shape[1]
    n_s = dec_batch * dec_seq
    n_all = n_p + n_s
    mod_row_p = lambda r: 0
    mod_row_s = lambda r: 1 + r // dec_seq
    n_exp, _, d_exp = w_exp_gate.shape[1:]
    casts_p = dict(premix=[w_out[l]],
                   gla=[w_exp_gate[l].reshape(n_exp * d, d_exp), w_exp_down[l].reshape(n_exp * d_exp, d)],
                   hy_conv=[w_exp_up[l].reshape(n_exp * d, d_exp)])
    casts_s = dict(premix=[], gla=[], hy_conv=[])
    x1_p, hg_all, route_p, counts, s_f, s_b, casted = _mixer_and_router(
        x_prompt, mod, mod_row_p, zero_state, zero_state, 1, p, tiles_p, jnp.zeros((1, LANES), F32), None, 0, n_all,
        None, casts_p)
    (w_out_b,), (w_gate_b, w_down_b), (w_up_b,) = casted["premix"], casted["gla"], casted["hy_conv"]
    x1_s, hg_all, route_s, counts, _, _, _ = _mixer_and_router(
        x_sample, mod, mod_row_s, state_gla_fwd[:, l], state_gla_bwd[:, l], dec_seq // GRID_W, p, tiles_s,
        counts, hg_all, n_p, n_all, w_out_b, casts_s)

    n_tiles = n_all // tm_moe + N_GROUPS * PAIRS_PER_GROUP
    pos, src, ea, eb, valid = _route_tables(jnp.concatenate([route_p, route_s], axis=1), counts, n_tiles, tm_moe)
    y_sorted = _moe(hg_all, src, ea, eb, valid, w_gate_b.reshape(n_exp, d, d_exp), w_up_b.reshape(n_exp, d, d_exp),
                    w_down_b.reshape(n_exp, d_exp, d), n_tiles, tm_moe)
    y_p = _ffn_out(y_sorted, pos[:n_p], x1_p, mod, mod_row_p, p["g_post_ffn"], tm_out)
    y_s = _ffn_out(y_sorted, pos[n_p:], x1_s, mod, mod_row_s, p["g_post_ffn"], tm_out)
    return (y_p.reshape(x_prompt.shape), y_s.reshape(x_sample.shape),
            s_f[:, None].astype(x_prompt.dtype), s_b[:, None].astype(x_prompt.dtype))
```

```python
import functools
import math

import jax
import jax.numpy as jnp
from jax import lax
from jax.experimental import pallas as pl
from jax.experimental.pallas import tpu as pltpu

F32 = jnp.float32
BF16 = jnp.bfloat16

GRID_W = 64
GLA_HEADS = 4
GLA_RANK = 16
GLA_TAU = 16.0
GLA_CHUNK = 64
GLA_UNROLL = 4
MIX_SUBTILES = 1
HY_ORDER = 2
HY_SHORT = 3
HY_BANDS = 16
HY_TARGET = 1e-2
HY_FAST_PCT = 0.3
HY_SLOW_PCT = 1.5
N_GROUPS = 4
EXP_PER_GROUP = 4
N_EXPERTS = N_GROUPS * EXP_PER_GROUP
PAIRS_PER_GROUP = EXP_PER_GROUP * (EXP_PER_GROUP - 1) // 2
ROUTE_BUCKET_LANE = N_EXPERTS
ROUTE_RANK_LANE = N_EXPERTS + 1
EPS = 1e-6

LANES = 128
SUBLANES = 8
MXU_DIM = 256
VMEM_LIMIT = 56 << 20
REPACK_COLS = MXU_DIM
FREQ_BLOCK = 2 * MXU_DIM


def _tile_plan(seq):
    long_seq = seq >= 1024
    return dict(
        tm_proj=1024, tn_proj=2048,
        tc_filter=128 if long_seq else 256,
        tr_spec=512 if long_seq else 256,
        tc_spec=1024 if long_seq else 512,
        hy_split=long_seq,
        tc_conv=256 if long_seq else 512,
        hy_batch_per_step=1 if long_seq else 2,
        gla_heads_per_step=2 if long_seq else 4,
        tm_mix=512,
    )


def _cparams(sem):
    return pltpu.CompilerParams(dimension_semantics=sem, vmem_limit_bytes=VMEM_LIMIT)


def _dot(a, b):
    return jnp.dot(a, b, preferred_element_type=F32)


def _dot_nt(a, b):
    return lax.dot_general(a, b, (((1,), (1,)), ((), ())), preferred_element_type=F32)


def _dot_tn(a, b):
    return lax.dot_general(a, b, (((0,), (0,)), ((), ())), preferred_element_type=F32)


def _split2(x):
    hi = x.astype(BF16)
    lo = (x - hi.astype(F32)).astype(BF16)
    return hi, lo


def _dot_hp(a, b):
    ah, al = a if isinstance(a, tuple) else _split2(a)
    bh, bl = _split2(b)
    return _dot(ah, bh) + (_dot(ah, bl) + _dot(al, bh))


def _dot_exact_lhs(t, x):
    hi, lo = _split2(x)
    return _dot(t, hi) + _dot(t, lo)


def _rms(x, g):
    return x * lax.rsqrt(jnp.mean(x * x, axis=-1, keepdims=True) + EPS) * g


def _silu(x):
    return x / (1.0 + jnp.exp(-x))


def _ada_kernel(c_ref, w_ref, b_ref, o_ref):
    s = _silu(c_ref[...]).astype(BF16)
    o_ref[...] = _dot(s, w_ref[...].astype(BF16)) + b_ref[...]


def _ada_mod(cond, w_ada, b_ada):
    rows, d = cond.shape
    n = w_ada.shape[1]
    tn = 1024
    return pl.pallas_call(
        _ada_kernel,
        grid=(n // tn,),
        in_specs=[pl.BlockSpec((rows, d), lambda j: (0, 0)),
                  pl.BlockSpec((d, tn), lambda j: (0, j)),
                  pl.BlockSpec((1, tn), lambda j: (0, j))],
        out_specs=pl.BlockSpec((rows, tn), lambda j: (0, j)),
        out_shape=jax.ShapeDtypeStruct((rows, n), F32),
        compiler_params=_cparams(("arbitrary",)),
        name="ada_mod",
    )(cond, w_ada, b_ada.reshape(1, n))


def _repack_kernel(wt_hbm, main_ref, lr_ref, buf, lr_buf, sem, lr_sem, *, layer, n_main, n_lr, tb):
    i = pl.program_id(0)
    n_i = pl.num_programs(0)
    slot = i % 2
    d = main_ref.shape[0]

    def block_copy(j, s):
        first = j * tb + jnp.where(j * tb >= n_main, n_lr, 0)
        return pltpu.make_async_copy(wt_hbm.at[layer, pl.ds(first, tb), :], buf.at[s], sem.at[s])

    lr_copy = pltpu.make_async_copy(wt_hbm.at[layer, pl.ds(n_main, n_lr), :], lr_buf, lr_sem)

    @pl.when(i == 0)
    def _():
        block_copy(0, 0).start()
        lr_copy.start()

    @pl.when(i + 1 < n_i)
    def _():
        block_copy(i + 1, 1 - slot).start()

    block_copy(i, slot).wait()
    main_ref[...] = buf[slot].T.astype(BF16)

    @pl.when(i == 0)
    def _():
        lr_copy.wait()
        lr = jnp.concatenate([lr_buf[...], jnp.zeros((LANES - n_lr, d), F32)], axis=0).T
        lr_ref[...] = lr.astype(BF16)


def _repack_w_in(w_in, layer, n_main, n_lr, tb):
    w_t = jnp.swapaxes(w_in, 1, 2)
    _, n_cols, d = w_t.shape
    n_out = n_cols - n_lr
    assert n_main % tb == 0 and n_out % tb == 0
    kern = functools.partial(_repack_kernel, layer=layer, n_main=n_main, n_lr=n_lr, tb=tb)
    return pl.pallas_call(
        kern,
        grid=(n_out // tb,),
        in_specs=[pl.BlockSpec(memory_space=pl.ANY)],
        out_specs=[pl.BlockSpec((d, tb), lambda i: (0, i)),
                   pl.BlockSpec((d, LANES), lambda i: (0, 0))],
        out_shape=[jax.ShapeDtypeStruct((d, n_out), BF16),
                   jax.ShapeDtypeStruct((d, LANES), BF16)],
        scratch_shapes=[pltpu.VMEM((2, tb, d), F32), pltpu.VMEM((n_lr, d), F32),
                        pltpu.SemaphoreType.DMA((2,)), pltpu.SemaphoreType.DMA(())],
        compiler_params=_cparams(("arbitrary",)),
        name="repack_w_in",
    )(w_t)


def _with_cast_riders(kernel_fn, n_in, n_out, n_cast):
    def kernel(*refs):
        ins = refs[:n_in]
        cast_in = refs[n_in:n_in + n_cast]
        outs = refs[n_in + n_cast:n_in + n_cast + n_out]
        cast_out = refs[n_in + n_cast + n_out:n_in + 2 * n_cast + n_out]
        kernel_fn(*ins, *outs, *refs[n_in + 2 * n_cast + n_out:])
        for src, dst in zip(cast_in, cast_out):
            dst[...] = src[...].astype(dst.dtype)

    return kernel


def _cast_rider_specs(cast_srcs, n_steps, step_of):
    n_slabs = 1 << (n_steps.bit_length() - 1)
    specs = []
    for w in cast_srcs:
        rows, cols = w.shape
        assert rows % n_slabs == 0
        specs.append(pl.BlockSpec((rows // n_slabs, cols),
                                  lambda *idx: (jnp.minimum(step_of(*idx), n_slabs - 1), 0)))
    return specs, [jax.ShapeDtypeStruct(w.shape, BF16) for w in cast_srcs]


def _premix_kernel(x_ref, mod_ref, g_ref, w_ref, wlr_ref, o_ref, lr_ref, h_scr):
    @pl.when(pl.program_id(1) == 0)
    def _():
        h = _rms(x_ref[...], g_ref[...] * (1.0 + mod_ref[0, 1:2, :])) + mod_ref[0, 0:1, :]
        h_scr[...] = h.astype(BF16)
        lr_ref[...] = _dot(h_scr[...], wlr_ref[...])

    o_ref[...] = _dot(h_scr[...], w_ref[...]).astype(o_ref.dtype)


def _premix_proj(x, mod, mod_row, g, w_main, w_lr, tm, tn, cast_srcs):
    n, d = x.shape
    nc = w_main.shape[1]
    n_j = nc // tn
    cast_specs, cast_shapes = _cast_rider_specs(cast_srcs, (n // tm) * n_j, lambda i, j: i * n_j + j)
    kern = _with_cast_riders(_premix_kernel, 5, 2, len(cast_srcs))
    return pl.pallas_call(
        kern,
        grid=(n // tm, n_j),
        in_specs=[pl.BlockSpec((tm, d), lambda i, j: (i, 0)),
                  pl.BlockSpec((1, 6, d), lambda i, j: (mod_row(i * tm), 0, 0)),
                  pl.BlockSpec((1, d), lambda i, j: (0, 0)),
                  pl.BlockSpec((d, tn), lambda i, j: (0, j)),
                  pl.BlockSpec((d, LANES), lambda i, j: (0, 0))] + cast_specs,
        out_specs=[pl.BlockSpec((tm, tn), lambda i, j: (i, j)),
                   pl.BlockSpec((tm, LANES), lambda i, j: (i, 0))] + cast_specs,
        out_shape=[jax.ShapeDtypeStruct((n, nc), BF16), jax.ShapeDtypeStruct((n, LANES), F32)] + cast_shapes,
        scratch_shapes=[pltpu.VMEM((tm, d), BF16)],
        compiler_params=_cparams(("arbitrary", "arbitrary")),
        name="premix_proj",
    )(x, mod, g, w_main, w_lr, *cast_srcs)


def _log_sigmoid(x):
    return jnp.minimum(x, 0.0) - jnp.log(1.0 + jnp.exp(-jnp.abs(x)))


def _gla_kernel(q_ref, k_ref, v_ref, g_ref, lr_ref, wdf_ref, bdf_ref, wdb_ref, bdb_ref, gg_ref,
                s0f_ref, s0b_ref, y_ref, sf_ref, sb_ref,
                laf_scr, lab_scr, qf_scr, qb_scr, o_scr, uf_scr, ub_scr, df_scr, db_scr, *, seq, dk, dv, hp):
    c = GLA_CHUNK
    n_chunks = seq // c
    scale = dk ** -0.5
    heads = range(hp)
    kcols = [slice(h * dk, (h + 1) * dk) for h in heads]
    vcols = [slice(h * dv, (h + 1) * dv) for h in heads]

    lr_split = _split2(lr_ref[0])
    for h in heads:
        laf_scr[h] = _log_sigmoid(_dot_hp(lr_split, wdf_ref[h]) + bdf_ref[:, kcols[h]]) / GLA_TAU
        lab_scr[h] = _log_sigmoid(_dot_hp(lr_split, wdb_ref[h]) + bdb_ref[:, kcols[h]]) / GLA_TAU

    per = GLA_UNROLL
    blk = per * c
    row = lax.broadcasted_iota(jnp.int32, (blk, blk), 0)
    col = lax.broadcasted_iota(jnp.int32, (blk, blk), 1)
    same = (row // c) == (col // c)
    lower = same & (row >= col)
    upper = same & (col >= row)
    t_fwd = lower.astype(BF16)
    t_bwd = upper.astype(BF16)

    def chunk_rows(x, r):
        return jnp.concatenate([jnp.broadcast_to(x[j * c + r:j * c + r + 1], (c, dk)) for j in range(per)], axis=0)

    def block_local(m, carry):
        sl = pl.ds(pl.multiple_of(m * blk, blk), blk)
        bf = [_dot_exact_lhs(t_fwd, laf_scr[h, sl, :]) for h in heads]
        bb = [_dot_exact_lhs(t_bwd, lab_scr[h, sl, :]) for h in heads]
        tot_f = [chunk_rows(x, c - 1) for x in bf]
        tot_b = [chunk_rows(x, 0) for x in bb]
        q = [q_ref[0, sl, kcols[h]].astype(F32) * scale for h in heads]
        k = [k_ref[0, sl, kcols[h]].astype(F32) for h in heads]
        v = [v_ref[0, sl, vcols[h]] for h in heads]
        qf = [(q[h] * jnp.exp(bf[h])).astype(BF16) for h in heads]
        kf = [(k[h] * jnp.exp(-bf[h])).astype(BF16) for h in heads]
        qb = [(q[h] * jnp.exp(bb[h])).astype(BF16) for h in heads]
        kb = [(k[h] * jnp.exp(-bb[h])).astype(BF16) for h in heads]
        ksf = [(k[h] * jnp.exp(tot_f[h] - bf[h])).astype(BF16) for h in heads]
        ksb = [(k[h] * jnp.exp(tot_b[h] - bb[h])).astype(BF16) for h in heads]
        sc_f = [_dot_nt(qf[h], kf[h]) for h in heads]
        sc_b = [_dot_nt(qb[h], kb[h]) for h in heads]
        att = [(jnp.where(lower, sc_f[h], 0.0) + jnp.where(upper, sc_b[h], 0.0)).astype(BF16) for h in heads]
        o_loc = [_dot(att[h], v[h]) for h in heads]
        dec_f = [jnp.exp(x) for x in tot_f]
        dec_b = [jnp.exp(x) for x in tot_b]
        for h in heads:
            o_scr[h, sl, :] = o_loc[h]
            qf_scr[h, sl, :] = qf[h]
            qb_scr[h, sl, :] = qb[h]
        for j in range(per):
            n = m * per + j
            rows = slice(j * c, (j + 1) * c)
            for h in heads:
                uf_scr[h, n] = _dot_tn(v[h][rows], ksf[h][rows])
                ub_scr[h, n] = _dot_tn(v[h][rows], ksb[h][rows])
                df_scr[h, n] = dec_f[h][j * c:j * c + SUBLANES]
                db_scr[h, n] = dec_b[h][j * c:j * c + SUBLANES]
        return carry

    lax.fori_loop(0, n_chunks // per, block_local, 0)

    def scan_fwd(n, s):
        upd = [uf_scr[h, n] for h in heads]
        for h in heads:
            uf_scr[h, n] = s[h]
        return tuple(s[h] * df_scr[h, n][0:1, :] + upd[h] for h in heads)

    def scan_bwd(i, s):
        n = n_chunks - 1 - i
        upd = [ub_scr[h, n] for h in heads]
        for h in heads:
            ub_scr[h, n] = s[h]
        return tuple(s[h] * db_scr[h, n][0:1, :] + upd[h] for h in heads)

    s_f = lax.fori_loop(0, n_chunks, scan_fwd, tuple(s0f_ref[0, h].T for h in heads), unroll=GLA_UNROLL)
    s_b = lax.fori_loop(0, n_chunks, scan_bwd, tuple(s0b_ref[0, h].T for h in heads), unroll=GLA_UNROLL)
    for h in heads:
        sf_ref[0, h] = s_f[h].T
        sb_ref[0, h] = s_b[h].T

    def chunk_inter(n, carry):
        sl = pl.ds(pl.multiple_of(n * c, c), c)
        inter = [_dot_nt(qf_scr[h, sl, :], uf_scr[h, n].astype(BF16))
                 + _dot_nt(qb_scr[h, sl, :], ub_scr[h, n].astype(BF16)) for h in heads]
        for h in heads:
            o_scr[h, sl, :] += inter[h]
        return carry

    lax.fori_loop(0, n_chunks, chunk_inter, 0, unroll=GLA_UNROLL)

    for h in heads:
        o = _rms(o_scr[h], gg_ref[...])
        y_ref[0, :, vcols[h]] = (o * _silu(g_ref[0, :, vcols[h]].astype(F32))).astype(y_ref.dtype)


def _gla(proj, lr, wdf, bdf, wdb, bdb, g_gla, s0f, s0b, batch, seq, dk, dv, hp, cast_srcs):
    heads = GLA_HEADS
    proj3 = proj.reshape(batch, seq, proj.shape[-1])
    lr3 = lr.reshape(batch, seq, LANES)
    kdim = heads * dk
    width = heads * dv
    bk, bv = hp * dk, hp * dv
    k_blk = kdim // bk
    v_blk = 2 * kdim // bv
    g_blk = (2 * kdim + width) // bv
    n_chunks = seq // GLA_CHUNK

    def s0_map(s0):
        if s0.shape[0] == batch:
            return lambda b, h: (b, h, 0, 0)
        return lambda b, h: (0, h, 0, 0)

    n_hsteps = heads // hp
    cast_specs, cast_shapes = _cast_rider_specs(cast_srcs, batch * n_hsteps, lambda b, h: b * n_hsteps + h)
    kern = _with_cast_riders(functools.partial(_gla_kernel, seq=seq, dk=dk, dv=dv, hp=hp), 12, 3, len(cast_srcs))
    return pl.pallas_call(
        kern,
        grid=(batch, n_hsteps),
        in_specs=[pl.BlockSpec((1, seq, bk), lambda b, h: (b, 0, h)),
                  pl.BlockSpec((1, seq, bk), lambda b, h: (b, 0, k_blk + h)),
                  pl.BlockSpec((1, seq, bv), lambda b, h: (b, 0, v_blk + h)),
                  pl.BlockSpec((1, seq, bv), lambda b, h: (b, 0, g_blk + h)),
                  pl.BlockSpec((1, seq, LANES), lambda b, h: (b, 0, 0)),
                  pl.BlockSpec((hp, LANES, dk), lambda b, h: (h, 0, 0)),
                  pl.BlockSpec((1, bk), lambda b, h: (0, h)),
                  pl.BlockSpec((hp, LANES, dk), lambda b, h: (h, 0, 0)),
                  pl.BlockSpec((1, bk), lambda b, h: (0, h)),
                  pl.BlockSpec((1, dv), lambda b, h: (0, 0)),
                  pl.BlockSpec((1, hp, dk, dv), s0_map(s0f)),
                  pl.BlockSpec((1, hp, dk, dv), s0_map(s0b))] + cast_specs,
        out_specs=[pl.BlockSpec((1, seq, bv), lambda b, h: (b, 0, h)),
                   pl.BlockSpec((1, hp, dk, dv), lambda b, h: (b, h, 0, 0)),
                   pl.BlockSpec((1, hp, dk, dv), lambda b, h: (b, h, 0, 0))] + cast_specs,
        out_shape=[jax.ShapeDtypeStruct((batch, seq, width), BF16),
                   jax.ShapeDtypeStruct((batch, heads, dk, dv), F32),
                   jax.ShapeDtypeStruct((batch, heads, dk, dv), F32)] + cast_shapes,
        scratch_shapes=[pltpu.VMEM((hp, seq, dk), F32), pltpu.VMEM((hp, seq, dk), F32),
                        pltpu.VMEM((hp, seq, dk), BF16), pltpu.VMEM((hp, seq, dk), BF16),
                        pltpu.VMEM((hp, seq, dv), F32),
                        pltpu.VMEM((hp, n_chunks, dv, dk), F32), pltpu.VMEM((hp, n_chunks, dv, dk), F32),
                        pltpu.VMEM((hp, n_chunks, SUBLANES, dk), F32), pltpu.VMEM((hp, n_chunks, SUBLANES, dk), F32)],
        compiler_params=_cparams(("arbitrary", "arbitrary")),
        name="gla",
    )(proj3, proj3, proj3, proj3, lr3, wdf, bdf, wdb, bdb, g_gla, s0f, s0b, *cast_srcs)


def _filter_kernel(pe_ref, w1_ref, b1_ref, w2_ref, b2_ref, fr_ref, w3_ref, dl_ref, ad_ref, hl_ref, hm_ref, mlp_scr,
                   *, seq):
    @pl.when(pl.program_id(0) == 0)
    def _():
        fr = fr_ref[...]
        h1 = jnp.sin(fr * (_dot_hp(pe_ref[...], w1_ref[...]) + b1_ref[...]))
        mlp_scr[...] = jnp.sin(fr * (_dot_hp(h1, w2_ref[...]) + b2_ref[...]))

    h2 = _split2(mlp_scr[...])
    dec = jnp.exp(-pe_ref[:, 0:1] * dl_ref[...])
    row = lax.broadcasted_iota(jnp.int32, (seq, 1), 0)
    alt = jnp.where(row % 2 == 0, 1.0, -1.0)
    phase = row % 4
    cos_half = jnp.where(phase == 0, 1.0, jnp.where(phase == 2, -1.0, 0.0))
    sin_half = jnp.where(phase == 1, 1.0, jnp.where(phase == 3, -1.0, 0.0))
    for o in range(HY_ORDER):
        ff = _dot_hp(h2, w3_ref[o]) * dec
        fb = _dot_hp(h2, w3_ref[HY_ORDER + o]) * dec
        nrm = jnp.sum(jnp.abs(ff), axis=0, keepdims=True) + jnp.sum(jnp.abs(fb), axis=0, keepdims=True)
        inv = 1.0 / nrm
        ff = ff * inv
        fb = jnp.where(row == 0, 0.0, fb * inv)
        a = ff + fb
        d = fb - ff
        ad_ref[o, 0] = a.astype(ad_ref.dtype)
        ad_ref[o, 1] = d.astype(ad_ref.dtype)
        ad_ref[o, 2] = (alt * a).astype(ad_ref.dtype)
        ad_ref[o, 3] = (-alt * d).astype(ad_ref.dtype)
        hl_ref[o] = jnp.sum(alt * a, axis=0, keepdims=True)
        hm_ref[o] = jnp.concatenate([jnp.sum(cos_half * a, axis=0, keepdims=True),
                                     jnp.sum(sin_half * d, axis=0, keepdims=True)], axis=0) * (1.0 / seq)


def _hy_filter(pe, w1p, b1, w2, b2, freq, w3r, deltas, seq, width, tc):
    hid = w2.shape[0]
    kern = functools.partial(_filter_kernel, seq=seq)
    return pl.pallas_call(
        kern,
        grid=(width // tc,),
        in_specs=[pl.BlockSpec((seq, LANES), lambda c: (0, 0)),
                  pl.BlockSpec((LANES, hid), lambda c: (0, 0)),
                  pl.BlockSpec((1, hid), lambda c: (0, 0)),
                  pl.BlockSpec((hid, hid), lambda c: (0, 0)),
                  pl.BlockSpec((1, hid), lambda c: (0, 0)),
                  pl.BlockSpec((1, hid), lambda c: (0, 0)),
                  pl.BlockSpec((2 * HY_ORDER, hid, tc), lambda c: (0, 0, c)),
                  pl.BlockSpec((1, tc), lambda c: (0, c))],
        out_specs=[pl.BlockSpec((HY_ORDER, 4, seq, tc), lambda c: (0, 0, 0, c)),
                   pl.BlockSpec((HY_ORDER, 1, tc), lambda c: (0, 0, c)),
                   pl.BlockSpec((HY_ORDER, 2, tc), lambda c: (0, 0, c))],
        out_shape=[jax.ShapeDtypeStruct((HY_ORDER, 4, seq, width), BF16),
                   jax.ShapeDtypeStruct((HY_ORDER, 1, width), F32),
                   jax.ShapeDtypeStruct((HY_ORDER, 2, width), F32)],
        scratch_shapes=[pltpu.VMEM((seq, hid), F32)],
        compiler_params=_cparams(("arbitrary",)),
        name="hy_filter",
    )(pe, w1p, b1, w2, b2, freq, w3r, deltas)


def _spectrum_kernel(cs_ref, ad_ref, h_ref, *, seq, per_part, tr):
    r = pl.program_id(1)
    k = (r % per_part) * tr + lax.broadcasted_iota(jnp.int32, (tr, 1), 0)
    wgt = jnp.where(k == 0, 1.0, 2.0) * (0.5 / seq)
    h_ref[0] = wgt * _dot(cs_ref[0], ad_ref[0, 0])


def _hy_spectrum(cs3, ad, seq, part_rows, n_parts, width, tr, tc):
    per_part = part_rows // tr
    kern = functools.partial(_spectrum_kernel, seq=seq, per_part=per_part, tr=tr)
    return pl.pallas_call(
        kern,
        grid=(HY_ORDER, n_parts * per_part, width // tc),
        in_specs=[pl.BlockSpec((1, tr, seq), lambda o, r, c: ((r // per_part) % 2, r % per_part, 0)),
                  pl.BlockSpec((1, 1, seq, tc), lambda o, r, c: (o, r // per_part, 0, c))],
        out_specs=pl.BlockSpec((1, tr, tc), lambda o, r, c: (o, r, c)),
        out_shape=jax.ShapeDtypeStruct((HY_ORDER, n_parts * part_rows, width), F32),
        compiler_params=_cparams(("arbitrary", "arbitrary", "arbitrary")),
        name="hy_spectrum",
    )(cs3, ad)


def _short_conv_fn(seq, seg):
    row = lax.broadcasted_iota(jnp.int32, (seq, 1), 0)
    pos = row % seg

    def short_conv(u_ref, cw_ref, cb_ref, b=0):
        u = u_ref[b].astype(F32)
        prev = jnp.where(pos == 0, 0.0, pltpu.roll(u, 1, 0))
        nxt = jnp.where(pos == seg - 1, 0.0, pltpu.roll(u, seq - 1, 0))
        return prev * cw_ref[0:1, :] + u * cw_ref[1:2, :] + nxt * cw_ref[2:3, :] + cb_ref[...]

    return short_conv


def _hyconv_kernel(u0_ref, u1_ref, u2_ref, cw0_ref, cw1_ref, cw2_ref, cb0_ref, cb1_ref, cb2_ref,
                   cs_ref, h_ref, hl_ref, bias_ref, z_ref, y_scr, *, seq, seg):
    short_conv = _short_conv_fn(seq, seg)
    row = lax.broadcasted_iota(jnp.int32, (seq, 1), 0)
    alt = jnp.where(row % 2 == 0, 1.0, -1.0)
    gates = ((u1_ref, cw1_ref, cb1_ref), (u2_ref, cw2_ref, cb2_ref))
    items = range(z_ref.shape[0])
    z = [short_conv(u0_ref, cw0_ref, cb0_ref, b) for b in items]
    kb = min(seq, FREQ_BLOCK)
    for n in range(HY_ORDER):
        zb = [z[b].astype(BF16) for b in items]
        for r in range(0, seq, kb):
            xc = [_dot(cs_ref[r:r + kb, :], zb[b]) for b in items]
            xs = [_dot(cs_ref[seq + r:seq + r + kb, :], zb[b]) for b in items]
            hre = h_ref[n, r:r + kb, :]
            him = h_ref[n, seq + r:seq + r + kb, :]
            for b in items:
                y_scr[b, r:r + kb, :] = (xc[b] * hre + xs[b] * him).astype(BF16)
                y_scr[b, seq + r:seq + r + kb, :] = (xs[b] * hre - xc[b] * him).astype(BF16)
        nyq = [jnp.sum(alt * z[b], axis=0, keepdims=True) * (hl_ref[n] * (0.5 / seq)) for b in items]
        conv = [_dot(cs_ref[:seq, :], y_scr[b, :seq, :]) + _dot(cs_ref[seq:, :], y_scr[b, seq:, :]) + alt * nyq[b]
                for b in items]
        gate = [short_conv(*gates[n], b) for b in items]
        z = [gate[b] * (conv[b] + bias_ref[n] * z[b]) for b in items]
    for b in items:
        z_ref[b] = z[b].astype(z_ref.dtype)


def _hyconv_split_kernel(u0_ref, u1_ref, u2_ref, cw0_ref, cw1_ref, cw2_ref, cb0_ref, cb1_ref, cb2_ref,
                         cs_ref, h_ref, hm_ref, tw_ref, bias_ref, z_ref, g_scr, z_scr, c_scr, *, seq, seg):
    m = seq // 2
    tc = z_ref.shape[2]
    n_lane_blocks = tc // LANES
    short_conv = _short_conv_fn(seq, seg)
    row = lax.broadcasted_iota(jnp.int32, (m, 1), 0)
    alt = jnp.where(row % 2 == 0, 1.0, -1.0)
    gates = ((u1_ref, cw1_ref, cb1_ref), (u2_ref, cw2_ref, cb2_ref))
    z = short_conv(u0_ref, cw0_ref, cb0_ref)
    kb = min(m, FREQ_BLOCK)
    for n in range(HY_ORDER):
        for j in range(n_lane_blocks):
            z_scr[j] = z[:, j * LANES:(j + 1) * LANES]
        ze = jnp.concatenate([z_scr[j, pl.ds(0, m, stride=2), :] for j in range(n_lane_blocks)], axis=1)
        zo = jnp.concatenate([z_scr[j, pl.ds(1, m, stride=2), :] for j in range(n_lane_blocks)], axis=1)
        e_mid = jnp.sum(alt * ze, axis=0, keepdims=True)
        o_mid = jnp.sum(alt * zo, axis=0, keepdims=True)
        zeb = ze.astype(BF16)
        zob = zo.astype(BF16)
        for r in range(0, m, kb):
            rows = slice(r, r + kb)
            srows = slice(m + r, m + r + kb)
            ec = _dot(cs_ref[rows, :], zeb)
            es = _dot(cs_ref[srows, :], zeb)
            oc = _dot(cs_ref[rows, :], zob)
            os_ = _dot(cs_ref[srows, :], zob)
            c = tw_ref[0, rows, :]
            s = tw_ref[1, rows, :]
            pc = c * oc - s * os_
            ps = c * os_ + s * oc
            xca, xsa = ec + pc, es + ps
            xcb, xsb = ec - pc, ps - es
            har = h_ref[n, rows, :]
            hai = h_ref[n, srows, :]
            hbr = h_ref[n, 2 * m + r:2 * m + r + kb, :]
            hbi = h_ref[n, 3 * m + r:3 * m + r + kb, :]
            yar = xca * har + xsa * hai
            yai = xca * hai - xsa * har
            ybr = xcb * hbr + xsb * hbi
            ybi = xcb * hbi - xsb * hbr
            dr = yar - ybr
            di = yai + ybi
            g_scr[0, rows, :] = (yar + ybr).astype(BF16)
            g_scr[0, srows, :] = (ybi - yai).astype(BF16)
            g_scr[1, rows, :] = (c * dr - s * di).astype(BF16)
            g_scr[1, srows, :] = (-(s * dr + c * di)).astype(BF16)
        hr = hm_ref[n, 0:1, :]
        hi = hm_ref[n, 1:2, :]
        ymr = e_mid * hr + o_mid * hi
        ymi = e_mid * hi - o_mid * hr
        y_even = _dot(cs_ref[:m, :], g_scr[0, :m, :]) + _dot(cs_ref[m:, :], g_scr[0, m:, :]) + alt * ymr
        y_odd = _dot(cs_ref[:m, :], g_scr[1, :m, :]) + _dot(cs_ref[m:, :], g_scr[1, m:, :]) - alt * ymi
        for j in range(n_lane_blocks):
            c_scr[j, pl.ds(0, m, stride=2), :] = y_even[:, j * LANES:(j + 1) * LANES]
            c_scr[j, pl.ds(1, m, stride=2), :] = y_odd[:, j * LANES:(j + 1) * LANES]
        conv = jnp.concatenate([c_scr[j] for j in range(n_lane_blocks)], axis=1)
        z = short_conv(*gates[n]) * (conv + bias_ref[n] * z)
    z_ref[0] = z.astype(z_ref.dtype)


def _hy_conv(proj, conv_w, conv_b, cs, hspec, hl, hm, tw, bias, batch, seq, seg, col0, width, tc, split, bp,
             cast_srcs):
    proj3 = proj.reshape(batch, seq, proj.shape[-1])
    blk0 = col0 // tc
    per = width // tc
    n_b = batch // bp
    assert bp == 1 or not split
    once = dict(pipeline_mode=pl.Buffered(1))

    def u_spec(p):
        return pl.BlockSpec((bp, seq, tc), lambda c, b: (b, 0, blk0 + p * per + c))

    def w_spec(p, rows):
        return pl.BlockSpec((rows, tc), lambda c, b: (0, p * per + c))

    in_specs = [u_spec(0), u_spec(1), u_spec(2),
                w_spec(0, HY_SHORT), w_spec(1, HY_SHORT), w_spec(2, HY_SHORT),
                w_spec(0, 1), w_spec(1, 1), w_spec(2, 1),
                pl.BlockSpec(cs.shape, lambda c, b: (0, 0), **once),
                pl.BlockSpec((HY_ORDER, 2 * seq, tc), lambda c, b: (0, 0, c), **once)]
    bias_spec = pl.BlockSpec((HY_ORDER, 1, tc), lambda c, b: (0, 0, c))
    operands = [proj3, proj3, proj3, conv_w, conv_w, conv_w, conv_b, conv_b, conv_b, cs, hspec]
    if split:
        kern = functools.partial(_hyconv_split_kernel, seq=seq, seg=seg)
        in_specs += [pl.BlockSpec((HY_ORDER, 2, tc), lambda c, b: (0, 0, c)),
                     pl.BlockSpec((2, seq // 2, tc), lambda c, b: (0, 0, 0)), bias_spec]
        operands += [hm, tw, bias]
        scratch = [pltpu.VMEM((2, seq, tc), BF16), pltpu.VMEM((tc // LANES, seq, LANES), F32),
                   pltpu.VMEM((tc // LANES, seq, LANES), F32)]
    else:
        kern = functools.partial(_hyconv_kernel, seq=seq, seg=seg)
        in_specs += [pl.BlockSpec((HY_ORDER, 1, tc), lambda c, b: (0, 0, c)), bias_spec]
        operands += [hl, bias]
        scratch = [pltpu.VMEM((bp, 2 * seq, tc), BF16)]
    cast_specs, cast_shapes = _cast_rider_specs(cast_srcs, per * n_b, lambda c, b: c * n_b + b)
    kern = _with_cast_riders(kern, len(operands), 1, len(cast_srcs))
    return pl.pallas_call(
        kern,
        grid=(per, n_b),
        in_specs=in_specs + cast_specs,
        out_specs=[pl.BlockSpec((bp, seq, tc), lambda c, b: (b, 0, c))] + cast_specs,
        out_shape=[jax.ShapeDtypeStruct((batch, seq, width), BF16)] + cast_shapes,
        scratch_shapes=scratch,
        compiler_params=_cparams(("arbitrary", "arbitrary")),
        name="hy_conv",
    )(*operands, *cast_srcs)


def _mixout_kernel(*refs, gla_width, n_own, aliased):
    (yg_ref, zh_ref, x_ref, mod_ref, ghy_ref, gpost_ref, gpre_ref, wo_ref, wrh_ref, wrl_ref, br_ref,
     cnt_in_ref) = refs[:12]
    x1_ref, hg_ref, route_ref, cnt_ref, cnt_scr = refs[12 + int(aliased):]
    d = x_ref.shape[1]
    tm = x_ref.shape[0] // MIX_SUBTILES
    i = pl.program_id(0)

    @pl.when(i == 0)
    def _():
        cnt_scr[...] = cnt_in_ref[...]

    @pl.when(i < n_own)
    def _():
        for s in range(MIX_SUBTILES):
            _mixout_rows(slice(s * tm, (s + 1) * tm), tm, d, gla_width, yg_ref, zh_ref, x_ref, mod_ref, ghy_ref,
                         gpost_ref, gpre_ref, wo_ref, wrh_ref, wrl_ref, br_ref, x1_ref, hg_ref, route_ref, cnt_scr)

    @pl.when(i >= n_own)
    def _():
        hg_ref[...] = jnp.zeros_like(hg_ref)

    cnt_ref[...] = cnt_scr[...]


def _mixout_rows(rows, tm, d, gla_width, yg_ref, zh_ref, x_ref, mod_ref, ghy_ref, gpost_ref, gpre_ref, wo_ref,
                 wrh_ref, wrl_ref, br_ref, x1_ref, hg_ref, route_ref, cnt_scr):
    yh = _rms(zh_ref[rows, :].astype(F32), ghy_ref[...]).astype(BF16)
    y = _dot(yg_ref[rows, :], wo_ref[:gla_width, :]) + _dot(yh, wo_ref[gla_width:, :])
    x1 = x_ref[rows, :] + _rms(y, gpost_ref[...] * mod_ref[0, 2:3, :])
    x1_ref[rows, :] = x1
    h2 = _rms(x1, gpre_ref[...] * (1.0 + mod_ref[0, 4:5, :])) + mod_ref[0, 3:4, :]
    hh, hl = _split2(h2)
    hg_ref[rows, :d] = h2
    logits = _dot(hh, wrh_ref[...]) + (_dot(hh, wrl_ref[...]) + _dot(hl, wrh_ref[...])) + br_ref[...]

    lane = lax.broadcasted_iota(jnp.int32, logits.shape, 1).astype(F32)
    neg = -jnp.inf
    is_grp = (lane >= N_EXPERTS) & (lane < N_EXPERTS + N_GROUPS)
    m = jnp.max(jnp.where(is_grp, logits, neg), axis=1, keepdims=True)
    p_grp = 1.0 / jnp.sum(jnp.where(is_grp, jnp.exp(logits - m), 0.0), axis=1, keepdims=True)
    grp = jnp.min(jnp.where(is_grp & (logits == m), lane - N_EXPERTS, 1e9), axis=1, keepdims=True)
    sel = (lane >= grp * EXP_PER_GROUP) & (lane < (grp + 1.0) * EXP_PER_GROUP)
    me = jnp.max(jnp.where(sel, logits, neg), axis=1, keepdims=True)
    pe = jnp.where(sel, jnp.exp(logits - me), -1.0)
    v1 = jnp.max(pe, axis=1, keepdims=True)
    i1 = jnp.min(jnp.where(pe == v1, lane, 1e9), axis=1, keepdims=True)
    pe2 = jnp.where(lane == i1, -1.0, pe)
    v2 = jnp.max(pe2, axis=1, keepdims=True)
    i2 = jnp.min(jnp.where(pe2 == v2, lane, 1e9), axis=1, keepdims=True)
    den = v1 + v2
    gates = jnp.where(lane == i1, v1 / den, jnp.where(lane == i2, v2 / den, 0.0)) * p_grp

    lo = jnp.minimum(i1, i2) - grp * EXP_PER_GROUP
    hi = jnp.maximum(i1, i2) - grp * EXP_PER_GROUP
    bucket = grp * PAIRS_PER_GROUP + lo * (2 * EXP_PER_GROUP - 1 - lo) * 0.5 + (hi - lo - 1.0)
    onehot = lane == bucket
    r_i = lax.broadcasted_iota(jnp.int32, (tm, tm), 0)
    c_i = lax.broadcasted_iota(jnp.int32, (tm, tm), 1)
    earlier = _dot((r_i > c_i).astype(BF16), onehot.astype(BF16)) + cnt_scr[...]
    rank = jnp.sum(jnp.where(onehot, earlier, 0.0), axis=1, keepdims=True)
    cnt_scr[...] += jnp.sum(onehot.astype(F32), axis=0, keepdims=True)
    route = jnp.where(lane == ROUTE_BUCKET_LANE, bucket, jnp.where(lane == ROUTE_RANK_LANE, rank, gates))
    hg_ref[rows, d:] = route
    route_ref[:, rows] = route.T[ROUTE_BUCKET_LANE:ROUTE_BUCKET_LANE + SUBLANES, :]


def _mix_out(yg, zh, x, mod, mod_row, g_hy, g_post, g_pre, w_out, wr_hi, wr_lo, b_r, cnt_in, hg_all, row0, n_total, tm):
    n, d = x.shape
    gw = yg.shape[1]
    hw = zh.shape[1]
    aliased = hg_all is not None
    n_own = n // tm
    n_steps = n_own if aliased else n_total // tm
    assert row0 % tm == 0 and n_total % tm == 0 and (row0 + n == n_total if aliased else row0 == 0)
    kern = functools.partial(_mixout_kernel, gla_width=gw, n_own=n_own, aliased=aliased)
    row = lambda i: (jnp.minimum(i, n_own - 1), 0)
    fixed = lambda i: (0, 0)
    in_specs = [pl.BlockSpec((tm, gw), row),
                pl.BlockSpec((tm, hw), row),
                pl.BlockSpec((tm, d), row),
                pl.BlockSpec((1, 6, d), lambda i: (mod_row(jnp.minimum(i, n_own - 1) * tm), 0, 0)),
                pl.BlockSpec((1, hw), fixed),
                pl.BlockSpec((1, d), fixed),
                pl.BlockSpec((1, d), fixed),
                pl.BlockSpec((gw + hw, d), fixed),
                pl.BlockSpec((d, LANES), fixed),
                pl.BlockSpec((d, LANES), fixed),
                pl.BlockSpec((1, LANES), fixed),
                pl.BlockSpec((1, LANES), fixed)]
    operands = [yg, zh, x, mod, g_hy, g_post, g_pre, w_out, wr_hi, wr_lo, b_r, cnt_in]
    if aliased:
        in_specs.append(pl.BlockSpec(memory_space=pl.ANY))
        operands.append(hg_all)
    return pl.pallas_call(
        kern,
        grid=(n_steps,),
        in_specs=in_specs,
        out_specs=[pl.BlockSpec((tm, d), row),
                   pl.BlockSpec((tm, d + LANES), lambda i: (row0 // tm + i, 0)),
                   pl.BlockSpec((SUBLANES, tm), lambda i: (0, jnp.minimum(i, n_own - 1))),
                   pl.BlockSpec((1, LANES), fixed)],
        out_shape=[jax.ShapeDtypeStruct((n, d), F32),
                   jax.ShapeDtypeStruct((n_total, d + LANES), F32),
                   jax.ShapeDtypeStruct((SUBLANES, n), F32),
                   jax.ShapeDtypeStruct((1, LANES), F32)],
        scratch_shapes=[pltpu.VMEM((1, LANES), F32)],
        input_output_aliases={len(operands) - 1: 1} if aliased else {},
        compiler_params=_cparams(("arbitrary",)),
        name="mix_out",
    )(*operands)


def _row_gather_copy(src_hbm, row, buf, slot, r, sem):
    return pltpu.make_async_copy(src_hbm.at[pl.ds(row, 1), :], buf.at[slot, pl.ds(r, 1), :], sem.at[slot])


def _start_row_gather(idx_ref, base, src_hbm, buf, slot, sem, tm):
    def body(r, carry):
        _row_gather_copy(src_hbm, idx_ref[base + r], buf, slot, r, sem).start()
        return carry

    lax.fori_loop(0, tm, body, 0, unroll=8)


def _start_row_gather_inline(idx_ref, base, src_hbm, buf, slot, sem, tm):
    for r in range(tm):
        _row_gather_copy(src_hbm, idx_ref[base + r], buf, slot, r, sem).start()


def _wait_row_gather(src_hbm, buf, slot, sem, tm):
    pltpu.make_async_copy(src_hbm.at[pl.ds(0, tm), :], buf.at[slot], sem.at[slot]).wait()


def _moe_kernel(src_ref, ea_ref, eb_ref, valid_ref, hg_hbm,
                wga_ref, wua_ref, wda_ref, wgb_ref, wub_ref, wdb_ref, y_ref, buf, sem, *, tm, d):
    t = pl.program_id(0)
    slot = t % 2

    @pl.when(t == 0)
    def _():
        _start_row_gather(src_ref, 0, hg_hbm, buf, 0, sem, tm)

    def used_tile(slot):
        _wait_row_gather(hg_hbm, buf, slot, sem, tm)
        _start_row_gather_inline(src_ref, (t + 1) * tm, hg_hbm, buf, 1 - slot, sem, tm)
        h = buf[slot, :, :d].astype(BF16)
        gates = buf[slot, :, d:]
        lane = lax.broadcasted_iota(jnp.int32, gates.shape, 1)

        experts = ((wga_ref, wua_ref, wda_ref, ea_ref[t]), (wgb_ref, wub_ref, wdb_ref, eb_ref[t]))
        pre = [_dot(h, wg_ref[0]) for wg_ref, _, _, _ in experts]
        up = [_dot(h, wu_ref[0]) for _, wu_ref, _, _ in experts]
        he = [(_silu(pre[i]) * up[i]).astype(BF16) for i in range(2)]
        gate = [jnp.sum(jnp.where(lane == e, gates, 0.0), axis=1, keepdims=True) for _, _, _, e in experts]
        out = [gate[i] * _dot(he[i], experts[i][2][0]) for i in range(2)]
        y_ref[...] = out[0] + out[1]

    for parity in range(2):
        pl.when((valid_ref[t] == 1) & (slot == parity))(functools.partial(used_tile, parity))

    @pl.when(valid_ref[t] == 0)
    def _():
        @pl.when(valid_ref[jnp.maximum(t - 1, 0)] == 1)
        def _():
            _wait_row_gather(hg_hbm, buf, slot, sem, tm)

        y_ref[...] = jnp.zeros_like(y_ref)


def _moe(hg, src, ea, eb, valid, wg, wu, wd, n_tiles, tm):
    d = hg.shape[1] - LANES
    _, _, de = wg.shape
    kern = functools.partial(_moe_kernel, tm=tm, d=d)
    wa = lambda t, src, ea, eb, valid: (ea[t], 0, 0)
    wb = lambda t, src, ea, eb, valid: (eb[t], 0, 0)
    return pl.pallas_call(
        kern,
        grid_spec=pltpu.PrefetchScalarGridSpec(
            num_scalar_prefetch=4,
            grid=(n_tiles,),
            in_specs=[pl.BlockSpec(memory_space=pl.ANY),
                      pl.BlockSpec((1, d, de), wa), pl.BlockSpec((1, d, de), wa), pl.BlockSpec((1, de, d), wa),
                      pl.BlockSpec((1, d, de), wb), pl.BlockSpec((1, d, de), wb), pl.BlockSpec((1, de, d), wb)],
            out_specs=pl.BlockSpec((tm, d), lambda t, src, ea, eb, valid: (t, 0)),
            scratch_shapes=[pltpu.VMEM((2, tm, d + LANES), F32), pltpu.SemaphoreType.DMA((2,))]),
        out_shape=jax.ShapeDtypeStruct((n_tiles * tm, d), F32),
        compiler_params=_cparams(("arbitrary",)),
        name="moe",
    )(src, ea, eb, valid, hg, wg, wu, wd, wg, wu, wd)


def _ffn_out_kernel(pos_ref, y_hbm, x1_ref, mod_ref, gpost_ref, o_ref, buf, sem, *, tm):
    i = pl.program_id(0)
    n_i = pl.num_programs(0)
    slot = i % 2

    @pl.when(i == 0)
    def _():
        _start_row_gather(pos_ref, 0, y_hbm, buf, 0, sem, tm)

    _wait_row_gather(y_hbm, buf, slot, sem, tm)

    @pl.when(i + 1 < n_i)
    def _():
        _start_row_gather_inline(pos_ref, (i + 1) * tm, y_hbm, buf, 1 - slot, sem, tm)
        o_ref[...] = x1_ref[...] + _rms(buf[slot], gpost_ref[...] * mod_ref[0, 5:6, :])

    @pl.when(i + 1 == n_i)
    def _():
        o_ref[...] = x1_ref[...] + _rms(buf[slot], gpost_ref[...] * mod_ref[0, 5:6, :])


def _ffn_out(y_sorted, pos, x1, mod, mod_row, g_post, tm):
    n, d = x1.shape
    kern = functools.partial(_ffn_out_kernel, tm=tm)
    return pl.pallas_call(
        kern,
        grid_spec=pltpu.PrefetchScalarGridSpec(
            num_scalar_prefetch=1,
            grid=(n // tm,),
            in_specs=[pl.BlockSpec(memory_space=pl.ANY),
                      pl.BlockSpec((tm, d), lambda i, pos: (i, 0)),
                      pl.BlockSpec((1, 6, d), lambda i, pos: (mod_row(i * tm), 0, 0)),
                      pl.BlockSpec((1, d), lambda i, pos: (0, 0))],
            out_specs=pl.BlockSpec((tm, d), lambda i, pos: (i, 0)),
            scratch_shapes=[pltpu.VMEM((2, tm, d), F32), pltpu.SemaphoreType.DMA((2,))]),
        out_shape=jax.ShapeDtypeStruct((n, d), F32),
        compiler_params=_cparams(("arbitrary",)),
        name="ffn_out",
    )(pos, y_sorted, x1, mod, g_post)


def _inverse_permutation_kernel(pos_ref, init_hbm, src_ref, sem):
    init_copy = pltpu.make_async_copy(init_hbm, src_ref, sem)
    init_copy.start()
    init_copy.wait()

    def claim(t, carry):
        src_ref[pos_ref[t]] = t
        return carry

    lax.fori_loop(0, pos_ref.shape[0], claim, 0, unroll=16)


def _inverse_permutation(pos, n_pad):
    n = pos.shape[0]
    init = jnp.arange(n_pad, dtype=jnp.int32) % n
    return pl.pallas_call(
        _inverse_permutation_kernel,
        in_specs=[pl.BlockSpec(memory_space=pltpu.SMEM), pl.BlockSpec(memory_space=pl.ANY)],
        out_specs=pl.BlockSpec(memory_space=pltpu.SMEM),
        out_shape=jax.ShapeDtypeStruct((n_pad,), jnp.int32),
        scratch_shapes=[pltpu.SemaphoreType.DMA(())],
        name="inverse_permutation",
    )(pos, init)


def _route_tables(route, counts, n_tiles, tm):
    n_buckets = N_GROUPS * PAIRS_PER_GROUP
    cnt = counts[0, :n_buckets].astype(jnp.int32)
    padded = (cnt + tm - 1) // tm * tm
    ends = jnp.cumsum(padded)
    starts = ends - padded
    bucket = route[0].astype(jnp.int32)
    rank = route[ROUTE_RANK_LANE - ROUTE_BUCKET_LANE].astype(jnp.int32)
    is_bucket = bucket[:, None] == jnp.arange(n_buckets, dtype=jnp.int32)[None, :]
    pos = jnp.sum(jnp.where(is_bucket, starts[None, :], 0), axis=1) + rank
    src = _inverse_permutation(pos, n_tiles * tm)
    n_valid = ends[-1] // tm
    tile = jnp.arange(n_tiles, dtype=jnp.int32)
    used = jnp.minimum(tile, n_valid - 1)
    tile_bucket = jnp.sum((ends[None, :] <= (used * tm)[:, None]).astype(jnp.int32), axis=1)
    grp = tile_bucket // PAIRS_PER_GROUP
    pair = tile_bucket % PAIRS_PER_GROUP
    pair_lo = jnp.array([a for a in range(EXP_PER_GROUP) for b in range(a + 1, EXP_PER_GROUP)], jnp.int32)
    pair_hi = jnp.array([b for a in range(EXP_PER_GROUP) for b in range(a + 1, EXP_PER_GROUP)], jnp.int32)
    ea = grp * EXP_PER_GROUP + pair_lo[pair]
    eb = grp * EXP_PER_GROUP + pair_hi[pair]
    valid = (tile < n_valid).astype(jnp.int32)
    return pos, src, ea, eb, valid


def _dft_kernel(cs_ref, *, seq, tr):
    k = pl.program_id(0) * tr + lax.broadcasted_iota(jnp.int32, (tr, LANES), 0)
    j = lax.broadcasted_iota(jnp.int32, (tr, LANES), 1)
    period = 2 * seq

    ang = ((k * j) % period).astype(F32) * (math.pi / seq)
    c0 = jnp.cos(ang)
    s0 = jnp.sin(ang)
    ch = c0[:, LANES // 2:LANES // 2 + 1]
    sh = s0[:, LANES // 2:LANES // 2 + 1]
    c_step = 2.0 * ch * ch - 1.0
    s_step = 2.0 * sh * ch
    ca = jnp.ones_like(ch)
    sa = jnp.zeros_like(ch)
    for t1 in range(seq // LANES):
        cols = slice(t1 * LANES, (t1 + 1) * LANES)
        cs_ref[0, :, cols] = (ca * c0 - sa * s0).astype(cs_ref.dtype)
        cs_ref[1, :, cols] = (sa * c0 + ca * s0).astype(cs_ref.dtype)
        ca, sa = ca * c_step - sa * s_step, sa * c_step + ca * s_step


def _dft_cos_sin(seq, n_rows, tr):
    kern = functools.partial(_dft_kernel, seq=seq, tr=tr)
    return pl.pallas_call(
        kern,
        grid=(n_rows // tr,),
        out_specs=pl.BlockSpec((2, tr, seq), lambda r: (0, r, 0)),
        out_shape=jax.ShapeDtypeStruct((2, n_rows, seq), BF16),
        compiler_params=_cparams(("arbitrary",)),
        name="dft_matrix",
    )()


def _positional_features(seq):
    t = jnp.arange(seq, dtype=F32)
    t01 = t / max(seq - 1, 1)
    ang = 2.0 * math.pi * t / seq
    bands = jnp.linspace(1e-4, HY_BANDS - 1, HY_BANDS, dtype=F32)
    pe = jnp.concatenate([t01[:, None], jnp.cos(ang[:, None] * bands), -jnp.sin(ang[:, None] * bands)], axis=-1)
    return jnp.pad(pe, ((0, 0), (0, LANES - pe.shape[1])))


def _mixer_and_router(x3, mod, mod_row, s0f, s0b, n_seg, p, tiles, cnt_in, hg_all, row0, n_total, w_out, casts):
    batch, seq, d = x3.shape
    x = x3.reshape(batch * seq, d)
    dk, dv = p["dk"], p["dv"]
    hy_width = p["hy_width"]
    casted = {}

    proj, lr, *casted["premix"] = _premix_proj(x, mod, mod_row, p["g_pre_mix"], p["w_main"], p["w_lr"],
                                               tiles["tm_proj"], tiles["tn_proj"], casts["premix"])
    if w_out is None:
        w_out = casted["premix"][0]
    y_gla, s_f, s_b, *casted["gla"] = _gla(proj, lr, p["wdf"], p["bdf"], p["wdb"], p["bdb"], p["g_gla"], s0f, s0b,
                                           batch, seq, dk, dv, tiles["gla_heads_per_step"], casts["gla"])

    ad, hl, hm = _hy_filter(_positional_features(seq), p["hy_w1"], p["hy_b1"], p["hy_w2"], p["hy_b2"], p["hy_freq"],
                            p["hy_w3"], p["hy_deltas"], seq, hy_width, tiles["tc_filter"])
    tr, tc_conv = tiles["tr_spec"], tiles["tc_conv"]
    if tiles["hy_split"]:
        half = seq // 2
        hspec = _hy_spectrum(_dft_cos_sin(seq, half, tr), ad, seq, half, 4, hy_width, tr, tiles["tc_spec"])
        cs = _dft_cos_sin(half, half, tr).reshape(seq, half)
        ang = jnp.arange(half, dtype=F32) * (math.pi / seq)
        tw = jnp.broadcast_to(jnp.stack([jnp.cos(ang), jnp.sin(ang)])[:, :, None], (2, half, tc_conv))
    else:
        cs3 = _dft_cos_sin(seq, seq, tr)
        hspec = _hy_spectrum(cs3, ad, seq, seq, 2, hy_width, tr, tiles["tc_spec"])
        cs = cs3.reshape(2 * seq, seq)
        tw = None
    z_hy, *casted["hy_conv"] = _hy_conv(proj, p["hy_conv_w"], p["hy_conv_b"], cs, hspec, hl, hm, tw, p["hy_bias"],
                                        batch, seq, seq // n_seg, p["hy_col0"], hy_width, tc_conv, tiles["hy_split"],
                                        tiles["hy_batch_per_step"], casts["hy_conv"])

    x1, hg_all, route, counts = _mix_out(y_gla.reshape(batch * seq, -1), z_hy.reshape(batch * seq, -1), x, mod,
                                         mod_row, p["g_hy"], p["g_post_mix"], p["g_pre_ffn"], w_out, p["wr_hi"],
                                         p["wr_lo"], p["b_r"], cnt_in, hg_all, row0, n_total, tiles["tm_mix"])
    return x1, hg_all, route, counts, s_f, s_b, casted


def kernel(x_prompt, x_sample, c, state_gla_fwd, state_gla_bwd, c_ctx, w_ada, b_ada, g_pre_mix, g_post_mix, g_pre_ffn, g_post_ffn, w_in, w_dec_f, b_dec_f, w_dec_b, b_dec_b, g_gla, hy_conv_w, hy_conv_b, hy_w1, hy_b1, hy_w2, hy_b2, hy_w3, hy_freq, hy_bias, g_hy, w_out, w_router_grp, b_router_grp, w_router_exp, b_router_exp, w_exp_gate, w_exp_up, w_exp_down):
    depth = w_ada.shape[0]
    assert depth == 1
    l = 0
    d = x_prompt.shape[-1]
    dec_batch = x_sample.shape[0]
    heads = GLA_HEADS
    dk, dv = state_gla_fwd.shape[-2:]
    kdim = heads * dk
    gla_width = heads * dv
    hy_width = g_hy.shape[-1]
    hid = hy_w2.shape[-1]

    cond = jnp.concatenate([c_ctx[None, :], c], axis=0)
    cond = jnp.pad(cond, ((0, -cond.shape[0] % SUBLANES), (0, 0)))
    mod = _ada_mod(cond, w_ada[l], b_ada[l]).reshape(cond.shape[0], 6, d)

    n_main = 2 * kdim + 2 * gla_width
    w_main, w_lr = _repack_w_in(w_in, l, n_main, 2 * GLA_RANK, REPACK_COLS)

    def dec_weight(w, first_row):
        wh = w.reshape(GLA_RANK, heads, dk).transpose(1, 0, 2)
        return jnp.pad(wh, ((0, 0), (first_row, LANES - GLA_RANK - first_row), (0, 0)))

    deltas = jnp.abs(jnp.linspace(math.log(HY_TARGET) / HY_SLOW_PCT, math.log(HY_TARGET) / HY_FAST_PCT, hy_width,
                                  dtype=F32))
    w_r = jnp.pad(jnp.concatenate([w_router_exp[l], w_router_grp[l]], axis=1),
                  ((0, 0), (0, LANES - N_EXPERTS - N_GROUPS)))
    wr_hi, wr_lo = _split2(w_r)
    b_r = jnp.pad(jnp.concatenate([b_router_exp[l], b_router_grp[l]]), (0, LANES - N_EXPERTS - N_GROUPS))

    p = dict(
        dk=dk, dv=dv, hy_width=hy_width, hy_col0=n_main,
        g_pre_mix=g_pre_mix[l][None, :], g_post_mix=g_post_mix[l][None, :],
        g_pre_ffn=g_pre_ffn[l][None, :], g_post_ffn=g_post_ffn[l][None, :],
        w_main=w_main, w_lr=w_lr,
        wdf=dec_weight(w_dec_f[l], 0), bdf=b_dec_f[l][None, :],
        wdb=dec_weight(w_dec_b[l], GLA_RANK), bdb=b_dec_b[l][None, :],
        g_gla=g_gla[l][None, :],
        hy_conv_w=hy_conv_w[l], hy_conv_b=hy_conv_b[l][None, :],
        hy_w1=jnp.pad(hy_w1[l], ((0, LANES - hy_w1.shape[1]), (0, 0))), hy_b1=hy_b1[l][None, :],
        hy_w2=hy_w2[l], hy_b2=hy_b2[l][None, :], hy_freq=hy_freq[l][None, :],
        hy_w3=hy_w3[l].reshape(hid, 2 * HY_ORDER, hy_width).transpose(1, 0, 2), hy_deltas=deltas[None, :],
        hy_bias=hy_bias[l][:, None, :], g_hy=g_hy[l][None, :],
        wr_hi=wr_hi, wr_lo=wr_lo, b_r=b_r[None, :],
    )

    zero_state = jnp.zeros((1, heads, dk, dv), F32)
    dec_seq = x_sample.shape[1]
    tiles_p = _tile_plan(x_prompt.shape[1])
    tiles_s = _tile_plan(dec_seq)
    tm_moe = MXU_DIM
    tm_out = 512

    n_p = x_prompt.shape[0] * x_prompt.shape[1]
    n_s = dec_batch * dec_seq
    n_all = n_p + n_s
    mod_row_p = lambda r: 0
    mod_row_s = lambda r: 1 + r // dec_seq
    n_exp, _, d_exp = w_exp_gate.shape[1:]
    casts_p = dict(premix=[w_out[l]],
                   gla=[w_exp_gate[l].reshape(n_exp * d, d_exp), w_exp_down[l].reshape(n_exp * d_exp, d)],
                   hy_conv=[w_exp_up[l].reshape(n_exp * d, d_exp)])
    casts_s = dict(premix=[], gla=[], hy_conv=[])
    x1_p, hg_all, route_p, counts, s_f, s_b, casted = _mixer_and_router(
        x_prompt, mod, mod_row_p, zero_state, zero_state, 1, p, tiles_p, jnp.zeros((1, LANES), F32), None, 0, n_all,
        None, casts_p)
    (w_out_b,), (w_gate_b, w_down_b), (w_up_b,) = casted["premix"], casted["gla"], casted["hy_conv"]
    x1_s, hg_all, route_s, counts, _, _, _ = _mixer_and_router(
        x_sample, mod, mod_row_s, state_gla_fwd[:, l], state_gla_bwd[:, l], dec_seq // GRID_W, p, tiles_s,
        counts, hg_all, n_p, n_all, w_out_b, casts_s)

    n_tiles = n_all // tm_moe + N_GROUPS * PAIRS_PER_GROUP
    pos, src, ea, eb, valid = _route_tables(jnp.concatenate([route_p, route_s], axis=1), counts, n_tiles, tm_moe)
    y_sorted = _moe(hg_all, src, ea, eb, valid, w_gate_b.reshape(n_exp, d, d_exp), w_up_b.reshape(n_exp, d, d_exp),
                    w_down_b.reshape(n_exp, d_exp, d), n_tiles, tm_moe)
    y_p = _ffn_out(y_sorted, pos[:n_p], x1_p, mod, mod_row_p, p["g_post_ffn"], tm_out)
    y_s = _ffn_out(y_sorted, pos[n_p:], x1_s, mod, mod_row_s, p["g_post_ffn"], tm_out)
    return (y_p.reshape(x_prompt.shape), y_s.reshape(x_sample.shape),
            s_f[:, None].astype(x_prompt.dtype), s_b[:, None].astype(x_prompt.dtype))
```

```python
import functools
import math

import jax
import jax.numpy as jnp
from jax import lax
from jax.experimental import pallas as pl
from jax.experimental.pallas import tpu as pltpu

F32 = jnp.float32
BF16 = jnp.bfloat16

GRID_W = 64
GLA_HEADS = 4
GLA_RANK = 16
GLA_TAU = 16.0
GLA_CHUNK = 64
GLA_UNROLL = 4
MIX_SUBTILES = 1
HY_ORDER = 2
HY_SHORT = 3
HY_BANDS = 16
HY_TARGET = 1e-2
HY_FAST_PCT = 0.3
HY_SLOW_PCT = 1.5
N_GROUPS = 4
EXP_PER_GROUP = 4
N_EXPERTS = N_GROUPS * EXP_PER_GROUP
PAIRS_PER_GROUP = EXP_PER_GROUP * (EXP_PER_GROUP - 1) // 2
ROUTE_BUCKET_LANE = N_EXPERTS
ROUTE_RANK_LANE = N_EXPERTS + 1
EPS = 1e-6

LANES = 128
SUBLANES = 8
MXU_DIM = 256
VMEM_LIMIT = 56 << 20
REPACK_COLS = MXU_DIM
FREQ_BLOCK = 2 * MXU_DIM


def _tile_plan(seq):
    long_seq = seq >= 1024
    return dict(
        tm_proj=1024, tn_proj=2048,
        tc_filter=128 if long_seq else 256,
        tr_spec=512 if long_seq else 256,
        tc_spec=1024 if long_seq else 512,
        hy_split=long_seq,
        tc_conv=256,
        hy_batch_per_step=1 if long_seq else 4,
        gla_heads_per_step=2 if long_seq else 4,
        tm_mix=512,
    )


def _cparams(sem):
    return pltpu.CompilerParams(dimension_semantics=sem, vmem_limit_bytes=VMEM_LIMIT)


def _dot(a, b):
    return jnp.dot(a, b, preferred_element_type=F32)


def _dot_nt(a, b):
    return lax.dot_general(a, b, (((1,), (1,)), ((), ())), preferred_element_type=F32)


def _dot_tn(a, b):
    return lax.dot_general(a, b, (((0,), (0,)), ((), ())), preferred_element_type=F32)


def _split2(x):
    hi = x.astype(BF16)
    lo = (x - hi.astype(F32)).astype(BF16)
    return hi, lo


def _dot_hp(a, b):
    ah, al = a if isinstance(a, tuple) else _split2(a)
    bh, bl = _split2(b)
    return _dot(ah, bh) + (_dot(ah, bl) + _dot(al, bh))


def _dot_exact_lhs(t, x):
    hi, lo = _split2(x)
    return _dot(t, hi) + _dot(t, lo)


def _rms(x, g):
    return x * lax.rsqrt(jnp.mean(x * x, axis=-1, keepdims=True) + EPS) * g


def _silu(x):
    return x / (1.0 + jnp.exp(-x))


def _ada_kernel(c_ref, w_ref, b_ref, o_ref):
    s = _silu(c_ref[...]).astype(BF16)
    o_ref[...] = _dot(s, w_ref[...].astype(BF16)) + b_ref[...]


def _ada_mod(cond, w_ada, b_ada):
    rows, d = cond.shape
    n = w_ada.shape[1]
    tn = 1024
    return pl.pallas_call(
        _ada_kernel,
        grid=(n // tn,),
        in_specs=[pl.BlockSpec((rows, d), lambda j: (0, 0)),
                  pl.BlockSpec((d, tn), lambda j: (0, j)),
                  pl.BlockSpec((1, tn), lambda j: (0, j))],
        out_specs=pl.BlockSpec((rows, tn), lambda j: (0, j)),
        out_shape=jax.ShapeDtypeStruct((rows, n), F32),
        compiler_params=_cparams(("arbitrary",)),
        name="ada_mod",
    )(cond, w_ada, b_ada.reshape(1, n))


def _repack_kernel(wt_hbm, main_ref, lr_ref, buf, lr_buf, sem, lr_sem, *, layer, n_main, n_lr, tb):
    i = pl.program_id(0)
    n_i = pl.num_programs(0)
    slot = i % 2
    d = main_ref.shape[0]

    def block_copy(j, s):
        first = j * tb + jnp.where(j * tb >= n_main, n_lr, 0)
        return pltpu.make_async_copy(wt_hbm.at[layer, pl.ds(first, tb), :], buf.at[s], sem.at[s])

    lr_copy = pltpu.make_async_copy(wt_hbm.at[layer, pl.ds(n_main, n_lr), :], lr_buf, lr_sem)

    @pl.when(i == 0)
    def _():
        block_copy(0, 0).start()
        lr_copy.start()

    @pl.when(i + 1 < n_i)
    def _():
        block_copy(i + 1, 1 - slot).start()

    block_copy(i, slot).wait()
    main_ref[...] = buf[slot].T.astype(BF16)

    @pl.when(i == 0)
    def _():
        lr_copy.wait()
        lr = jnp.concatenate([lr_buf[...], jnp.zeros((LANES - n_lr, d), F32)], axis=0).T
        lr_ref[...] = lr.astype(BF16)


def _repack_w_in(w_in, layer, n_main, n_lr, tb):
    w_t = jnp.swapaxes(w_in, 1, 2)
    _, n_cols, d = w_t.shape
    n_out = n_cols - n_lr
    assert n_main % tb == 0 and n_out % tb == 0
    kern = functools.partial(_repack_kernel, layer=layer, n_main=n_main, n_lr=n_lr, tb=tb)
    return pl.pallas_call(
        kern,
        grid=(n_out // tb,),
        in_specs=[pl.BlockSpec(memory_space=pl.ANY)],
        out_specs=[pl.BlockSpec((d, tb), lambda i: (0, i)),
                   pl.BlockSpec((d, LANES), lambda i: (0, 0))],
        out_shape=[jax.ShapeDtypeStruct((d, n_out), BF16),
                   jax.ShapeDtypeStruct((d, LANES), BF16)],
        scratch_shapes=[pltpu.VMEM((2, tb, d), F32), pltpu.VMEM((n_lr, d), F32),
                        pltpu.SemaphoreType.DMA((2,)), pltpu.SemaphoreType.DMA(())],
        compiler_params=_cparams(("arbitrary",)),
        name="repack_w_in",
    )(w_t)


def _with_cast_riders(kernel_fn, n_in, n_out, n_cast):
    def kernel(*refs):
        ins = refs[:n_in]
        cast_in = refs[n_in:n_in + n_cast]
        outs = refs[n_in + n_cast:n_in + n_cast + n_out]
        cast_out = refs[n_in + n_cast + n_out:n_in + 2 * n_cast + n_out]
        kernel_fn(*ins, *outs, *refs[n_in + 2 * n_cast + n_out:])
        for src, dst in zip(cast_in, cast_out):
            dst[...] = src[...].astype(dst.dtype)

    return kernel


def _cast_rider_specs(cast_srcs, n_steps, step_of):
    n_slabs = 1 << (n_steps.bit_length() - 1)
    specs = []
    for w in cast_srcs:
        rows, cols = w.shape
        assert rows % n_slabs == 0
        specs.append(pl.BlockSpec((rows // n_slabs, cols),
                                  lambda *idx: (jnp.minimum(step_of(*idx), n_slabs - 1), 0)))
    return specs, [jax.ShapeDtypeStruct(w.shape, BF16) for w in cast_srcs]


def _premix_kernel(x_ref, mod_ref, g_ref, w_ref, wlr_ref, o_ref, lr_ref, h_scr):
    @pl.when(pl.program_id(1) == 0)
    def _():
        h = _rms(x_ref[...], g_ref[...] * (1.0 + mod_ref[0, 1:2, :])) + mod_ref[0, 0:1, :]
        h_scr[...] = h.astype(BF16)
        lr_ref[...] = _dot(h_scr[...], wlr_ref[...])

    o_ref[...] = _dot(h_scr[...], w_ref[...]).astype(o_ref.dtype)


def _premix_proj(x, mod, mod_row, g, w_main, w_lr, tm, tn, cast_srcs):
    n, d = x.shape
    nc = w_main.shape[1]
    n_j = nc // tn
    cast_specs, cast_shapes = _cast_rider_specs(cast_srcs, (n // tm) * n_j, lambda i, j: i * n_j + j)
    kern = _with_cast_riders(_premix_kernel, 5, 2, len(cast_srcs))
    return pl.pallas_call(
        kern,
        grid=(n // tm, n_j),
        in_specs=[pl.BlockSpec((tm, d), lambda i, j: (i, 0)),
                  pl.BlockSpec((1, 6, d), lambda i, j: (mod_row(i * tm), 0, 0)),
                  pl.BlockSpec((1, d), lambda i, j: (0, 0)),
                  pl.BlockSpec((d, tn), lambda i, j: (0, j)),
                  pl.BlockSpec((d, LANES), lambda i, j: (0, 0))] + cast_specs,
        out_specs=[pl.BlockSpec((tm, tn), lambda i, j: (i, j)),
                   pl.BlockSpec((tm, LANES), lambda i, j: (i, 0))] + cast_specs,
        out_shape=[jax.ShapeDtypeStruct((n, nc), BF16), jax.ShapeDtypeStruct((n, LANES), F32)] + cast_shapes,
        scratch_shapes=[pltpu.VMEM((tm, d), BF16)],
        compiler_params=_cparams(("arbitrary", "arbitrary")),
        name="premix_proj",
    )(x, mod, g, w_main, w_lr, *cast_srcs)


def _log_sigmoid(x):
    return jnp.minimum(x, 0.0) - jnp.log(1.0 + jnp.exp(-jnp.abs(x)))


def _gla_kernel(q_ref, k_ref, v_ref, g_ref, lr_ref, wdf_ref, bdf_ref, wdb_ref, bdb_ref, gg_ref,
                s0f_ref, s0b_ref, y_ref, sf_ref, sb_ref,
                laf_scr, lab_scr, qf_scr, qb_scr, o_scr, uf_scr, ub_scr, df_scr, db_scr, *, seq, dk, dv, hp):
    c = GLA_CHUNK
    n_chunks = seq // c
    scale = dk ** -0.5
    heads = range(hp)
    kcols = [slice(h * dk, (h + 1) * dk) for h in heads]
    vcols = [slice(h * dv, (h + 1) * dv) for h in heads]

    lr_split = _split2(lr_ref[0])
    for h in heads:
        laf_scr[h] = _log_sigmoid(_dot_hp(lr_split, wdf_ref[h]) + bdf_ref[:, kcols[h]]) / GLA_TAU
        lab_scr[h] = _log_sigmoid(_dot_hp(lr_split, wdb_ref[h]) + bdb_ref[:, kcols[h]]) / GLA_TAU

    per = GLA_UNROLL
    blk = per * c
    row = lax.broadcasted_iota(jnp.int32, (blk, blk), 0)
    col = lax.broadcasted_iota(jnp.int32, (blk, blk), 1)
    same = (row // c) == (col // c)
    lower = same & (row >= col)
    upper = same & (col >= row)
    t_fwd = lower.astype(BF16)
    t_bwd = upper.astype(BF16)

    def chunk_rows(x, r):
        return jnp.concatenate([jnp.broadcast_to(x[j * c + r:j * c + r + 1], (c, dk)) for j in range(per)], axis=0)

    def block_local(m, carry):
        sl = pl.ds(pl.multiple_of(m * blk, blk), blk)
        bf = [_dot_exact_lhs(t_fwd, laf_scr[h, sl, :]) for h in heads]
        bb = [_dot_exact_lhs(t_bwd, lab_scr[h, sl, :]) for h in heads]
        tot_f = [chunk_rows(x, c - 1) for x in bf]
        tot_b = [chunk_rows(x, 0) for x in bb]
        q = [q_ref[0, sl, kcols[h]].astype(F32) * scale for h in heads]
        k = [k_ref[0, sl, kcols[h]].astype(F32) for h in heads]
        v = [v_ref[0, sl, vcols[h]] for h in heads]
        qf = [(q[h] * jnp.exp(bf[h])).astype(BF16) for h in heads]
        kf = [(k[h] * jnp.exp(-bf[h])).astype(BF16) for h in heads]
        qb = [(q[h] * jnp.exp(bb[h])).astype(BF16) for h in heads]
        kb = [(k[h] * jnp.exp(-bb[h])).astype(BF16) for h in heads]
        ksf = [(k[h] * jnp.exp(tot_f[h] - bf[h])).astype(BF16) for h in heads]
        ksb = [(k[h] * jnp.exp(tot_b[h] - bb[h])).astype(BF16) for h in heads]
        sc_f = [_dot_nt(qf[h], kf[h]) for h in heads]
        sc_b = [_dot_nt(qb[h], kb[h]) for h in heads]
        att = [(jnp.where(lower, sc_f[h], 0.0) + jnp.where(upper, sc_b[h], 0.0)).astype(BF16) for h in heads]
        o_loc = [_dot(att[h], v[h]) for h in heads]
        dec_f = [jnp.exp(x) for x in tot_f]
        dec_b = [jnp.exp(x) for x in tot_b]
        for h in heads:
            o_scr[h, sl, :] = o_loc[h]
            qf_scr[h, sl, :] = qf[h]
            qb_scr[h, sl, :] = qb[h]
        for j in range(per):
            n = m * per + j
            rows = slice(j * c, (j + 1) * c)
            for h in heads:
                uf_scr[h, n] = _dot_tn(v[h][rows], ksf[h][rows])
                ub_scr[h, n] = _dot_tn(v[h][rows], ksb[h][rows])
                df_scr[h, n] = dec_f[h][j * c:j * c + SUBLANES]
                db_scr[h, n] = dec_b[h][j * c:j * c + SUBLANES]
        return carry

    lax.fori_loop(0, n_chunks // per, block_local, 0)

    def scan_fwd(n, s):
        upd = [uf_scr[h, n] for h in heads]
        for h in heads:
            uf_scr[h, n] = s[h]
        return tuple(s[h] * df_scr[h, n][0:1, :] + upd[h] for h in heads)

    def scan_bwd(i, s):
        n = n_chunks - 1 - i
        upd = [ub_scr[h, n] for h in heads]
        for h in heads:
            ub_scr[h, n] = s[h]
        return tuple(s[h] * db_scr[h, n][0:1, :] + upd[h] for h in heads)

    s_f = lax.fori_loop(0, n_chunks, scan_fwd, tuple(s0f_ref[0, h].T for h in heads), unroll=GLA_UNROLL)
    s_b = lax.fori_loop(0, n_chunks, scan_bwd, tuple(s0b_ref[0, h].T for h in heads), unroll=GLA_UNROLL)
    for h in heads:
        sf_ref[0, h] = s_f[h].T
        sb_ref[0, h] = s_b[h].T

    def chunk_inter(n, carry):
        sl = pl.ds(pl.multiple_of(n * c, c), c)
        inter = [_dot_nt(qf_scr[h, sl, :], uf_scr[h, n].astype(BF16))
                 + _dot_nt(qb_scr[h, sl, :], ub_scr[h, n].astype(BF16)) for h in heads]
        for h in heads:
            o_scr[h, sl, :] += inter[h]
        return carry

    lax.fori_loop(0, n_chunks, chunk_inter, 0, unroll=GLA_UNROLL)

    for h in heads:
        o = _rms(o_scr[h], gg_ref[...])
        y_ref[0, :, vcols[h]] = (o * _silu(g_ref[0, :, vcols[h]].astype(F32))).astype(y_ref.dtype)


def _gla(proj, lr, wdf, bdf, wdb, bdb, g_gla, s0f, s0b, batch, seq, dk, dv, hp, cast_srcs):
    heads = GLA_HEADS
    proj3 = proj.reshape(batch, seq, proj.shape[-1])
    lr3 = lr.reshape(batch, seq, LANES)
    kdim = heads * dk
    width = heads * dv
    bk, bv = hp * dk, hp * dv
    k_blk = kdim // bk
    v_blk = 2 * kdim // bv
    g_blk = (2 * kdim + width) // bv
    n_chunks = seq // GLA_CHUNK

    def s0_map(s0):
        if s0.shape[0] == batch:
            return lambda b, h: (b, h, 0, 0)
        return lambda b, h: (0, h, 0, 0)

    n_hsteps = heads // hp
    cast_specs, cast_shapes = _cast_rider_specs(cast_srcs, batch * n_hsteps, lambda b, h: b * n_hsteps + h)
    kern = _with_cast_riders(functools.partial(_gla_kernel, seq=seq, dk=dk, dv=dv, hp=hp), 12, 3, len(cast_srcs))
    return pl.pallas_call(
        kern,
        grid=(batch, n_hsteps),
        in_specs=[pl.BlockSpec((1, seq, bk), lambda b, h: (b, 0, h)),
                  pl.BlockSpec((1, seq, bk), lambda b, h: (b, 0, k_blk + h)),
                  pl.BlockSpec((1, seq, bv), lambda b, h: (b, 0, v_blk + h)),
                  pl.BlockSpec((1, seq, bv), lambda b, h: (b, 0, g_blk + h)),
                  pl.BlockSpec((1, seq, LANES), lambda b, h: (b, 0, 0)),
                  pl.BlockSpec((hp, LANES, dk), lambda b, h: (h, 0, 0)),
                  pl.BlockSpec((1, bk), lambda b, h: (0, h)),
                  pl.BlockSpec((hp, LANES, dk), lambda b, h: (h, 0, 0)),
                  pl.BlockSpec((1, bk), lambda b, h: (0, h)),
                  pl.BlockSpec((1, dv), lambda b, h: (0, 0)),
                  pl.BlockSpec((1, hp, dk, dv), s0_map(s0f)),
                  pl.BlockSpec((1, hp, dk, dv), s0_map(s0b))] + cast_specs,
        out_specs=[pl.BlockSpec((1, seq, bv), lambda b, h: (b, 0, h)),
                   pl.BlockSpec((1, hp, dk, dv), lambda b, h: (b, h, 0, 0)),
                   pl.BlockSpec((1, hp, dk, dv), lambda b, h: (b, h, 0, 0))] + cast_specs,
        out_shape=[jax.ShapeDtypeStruct((batch, seq, width), BF16),
                   jax.ShapeDtypeStruct((batch, heads, dk, dv), F32),
                   jax.ShapeDtypeStruct((batch, heads, dk, dv), F32)] + cast_shapes,
        scratch_shapes=[pltpu.VMEM((hp, seq, dk), F32), pltpu.VMEM((hp, seq, dk), F32),
                        pltpu.VMEM((hp, seq, dk), BF16), pltpu.VMEM((hp, seq, dk), BF16),
                        pltpu.VMEM((hp, seq, dv), F32),
                        pltpu.VMEM((hp, n_chunks, dv, dk), F32), pltpu.VMEM((hp, n_chunks, dv, dk), F32),
                        pltpu.VMEM((hp, n_chunks, SUBLANES, dk), F32), pltpu.VMEM((hp, n_chunks, SUBLANES, dk), F32)],
        compiler_params=_cparams(("arbitrary", "arbitrary")),
        name="gla",
    )(proj3, proj3, proj3, proj3, lr3, wdf, bdf, wdb, bdb, g_gla, s0f, s0b, *cast_srcs)


def _filter_kernel(pe_ref, w1_ref, b1_ref, w2_ref, b2_ref, fr_ref, w3_ref, dl_ref, ad_ref, hl_ref, hm_ref, mlp_scr,
                   *, seq):
    @pl.when(pl.program_id(0) == 0)
    def _():
        fr = fr_ref[...]
        h1 = jnp.sin(fr * (_dot_hp(pe_ref[...], w1_ref[...]) + b1_ref[...]))
        mlp_scr[...] = jnp.sin(fr * (_dot_hp(h1, w2_ref[...]) + b2_ref[...]))

    h2 = _split2(mlp_scr[...])
    dec = jnp.exp(-pe_ref[:, 0:1] * dl_ref[...])
    row = lax.broadcasted_iota(jnp.int32, (seq, 1), 0)
    alt = jnp.where(row % 2 == 0, 1.0, -1.0)
    phase = row % 4
    cos_half = jnp.where(phase == 0, 1.0, jnp.where(phase == 2, -1.0, 0.0))
    sin_half = jnp.where(phase == 1, 1.0, jnp.where(phase == 3, -1.0, 0.0))
    for o in range(HY_ORDER):
        ff = _dot_hp(h2, w3_ref[o]) * dec
        fb = _dot_hp(h2, w3_ref[HY_ORDER + o]) * dec
        nrm = jnp.sum(jnp.abs(ff), axis=0, keepdims=True) + jnp.sum(jnp.abs(fb), axis=0, keepdims=True)
        inv = 1.0 / nrm
        ff = ff * inv
        fb = jnp.where(row == 0, 0.0, fb * inv)
        a = ff + fb
        d = fb - ff
        ad_ref[o, 0] = a.astype(ad_ref.dtype)
        ad_ref[o, 1] = d.astype(ad_ref.dtype)
        ad_ref[o, 2] = (alt * a).astype(ad_ref.dtype)
        ad_ref[o, 3] = (-alt * d).astype(ad_ref.dtype)
        hl_ref[o] = jnp.sum(alt * a, axis=0, keepdims=True)
        hm_ref[o] = jnp.concatenate([jnp.sum(cos_half * a, axis=0, keepdims=True),
                                     jnp.sum(sin_half * d, axis=0, keepdims=True)], axis=0) * (1.0 / seq)


def _hy_filter(pe, w1p, b1, w2, b2, freq, w3r, deltas, seq, width, tc):
    hid = w2.shape[0]
    kern = functools.partial(_filter_kernel, seq=seq)
    return pl.pallas_call(
        kern,
        grid=(width // tc,),
        in_specs=[pl.BlockSpec((seq, LANES), lambda c: (0, 0)),
                  pl.BlockSpec((LANES, hid), lambda c: (0, 0)),
                  pl.BlockSpec((1, hid), lambda c: (0, 0)),
                  pl.BlockSpec((hid, hid), lambda c: (0, 0)),
                  pl.BlockSpec((1, hid), lambda c: (0, 0)),
                  pl.BlockSpec((1, hid), lambda c: (0, 0)),
                  pl.BlockSpec((2 * HY_ORDER, hid, tc), lambda c: (0, 0, c)),
                  pl.BlockSpec((1, tc), lambda c: (0, c))],
        out_specs=[pl.BlockSpec((HY_ORDER, 4, seq, tc), lambda c: (0, 0, 0, c)),
                   pl.BlockSpec((HY_ORDER, 1, tc), lambda c: (0, 0, c)),
                   pl.BlockSpec((HY_ORDER, 2, tc), lambda c: (0, 0, c))],
        out_shape=[jax.ShapeDtypeStruct((HY_ORDER, 4, seq, width), BF16),
                   jax.ShapeDtypeStruct((HY_ORDER, 1, width), F32),
                   jax.ShapeDtypeStruct((HY_ORDER, 2, width), F32)],
        scratch_shapes=[pltpu.VMEM((seq, hid), F32)],
        compiler_params=_cparams(("arbitrary",)),
        name="hy_filter",
    )(pe, w1p, b1, w2, b2, freq, w3r, deltas)


def _spectrum_kernel(cs_ref, ad_ref, h_ref, *, seq, per_part, tr):
    r = pl.program_id(1)
    k = (r % per_part) * tr + lax.broadcasted_iota(jnp.int32, (tr, 1), 0)
    wgt = jnp.where(k == 0, 1.0, 2.0) * (0.5 / seq)
    h_ref[0] = wgt * _dot(cs_ref[0], ad_ref[0, 0])


def _hy_spectrum(cs3, ad, seq, part_rows, n_parts, width, tr, tc):
    per_part = part_rows // tr
    kern = functools.partial(_spectrum_kernel, seq=seq, per_part=per_part, tr=tr)
    return pl.pallas_call(
        kern,
        grid=(HY_ORDER, n_parts * per_part, width // tc),
        in_specs=[pl.BlockSpec((1, tr, seq), lambda o, r, c: ((r // per_part) % 2, r % per_part, 0)),
                  pl.BlockSpec((1, 1, seq, tc), lambda o, r, c: (o, r // per_part, 0, c))],
        out_specs=pl.BlockSpec((1, tr, tc), lambda o, r, c: (o, r, c)),
        out_shape=jax.ShapeDtypeStruct((HY_ORDER, n_parts * part_rows, width), F32),
        compiler_params=_cparams(("arbitrary", "arbitrary", "arbitrary")),
        name="hy_spectrum",
    )(cs3, ad)


def _short_conv_fn(seq, seg):
    row = lax.broadcasted_iota(jnp.int32, (seq, 1), 0)
    pos = row % seg

    def short_conv(u_ref, cw_ref, cb_ref, b=0):
        u = u_ref[b].astype(F32)
        prev = jnp.where(pos == 0, 0.0, pltpu.roll(u, 1, 0))
        nxt = jnp.where(pos == seg - 1, 0.0, pltpu.roll(u, seq - 1, 0))
        return prev * cw_ref[0:1, :] + u * cw_ref[1:2, :] + nxt * cw_ref[2:3, :] + cb_ref[...]

    return short_conv


def _hyconv_kernel(u0_ref, u1_ref, u2_ref, cw0_ref, cw1_ref, cw2_ref, cb0_ref, cb1_ref, cb2_ref,
                   cs_ref, h_ref, hl_ref, bias_ref, z_ref, y_scr, *, seq, seg):
    short_conv = _short_conv_fn(seq, seg)
    row = lax.broadcasted_iota(jnp.int32, (seq, 1), 0)
    alt = jnp.where(row % 2 == 0, 1.0, -1.0)
    gates = ((u1_ref, cw1_ref, cb1_ref), (u2_ref, cw2_ref, cb2_ref))
    items = range(z_ref.shape[0])
    z = [short_conv(u0_ref, cw0_ref, cb0_ref, b) for b in items]
    kb = min(seq, FREQ_BLOCK)
    for n in range(HY_ORDER):
        zb = [z[b].astype(BF16) for b in items]
        for r in range(0, seq, kb):
            xc = [_dot(cs_ref[r:r + kb, :], zb[b]) for b in items]
            xs = [_dot(cs_ref[seq + r:seq + r + kb, :], zb[b]) for b in items]
            hre = h_ref[n, r:r + kb, :]
            him = h_ref[n, seq + r:seq + r + kb, :]
            for b in items:
                y_scr[b, r:r + kb, :] = (xc[b] * hre + xs[b] * him).astype(BF16)
                y_scr[b, seq + r:seq + r + kb, :] = (xs[b] * hre - xc[b] * him).astype(BF16)
        nyq = [jnp.sum(alt * z[b], axis=0, keepdims=True) * (hl_ref[n] * (0.5 / seq)) for b in items]
        conv = [_dot(cs_ref[:seq, :], y_scr[b, :seq, :]) + _dot(cs_ref[seq:, :], y_scr[b, seq:, :]) + alt * nyq[b]
                for b in items]
        gate = [short_conv(*gates[n], b) for b in items]
        z = [gate[b] * (conv[b] + bias_ref[n] * z[b]) for b in items]
    for b in items:
        z_ref[b] = z[b].astype(z_ref.dtype)


def _hyconv_split_kernel(u0_ref, u1_ref, u2_ref, cw0_ref, cw1_ref, cw2_ref, cb0_ref, cb1_ref, cb2_ref,
                         cs_ref, h_ref, hm_ref, tw_ref, bias_ref, z_ref, g_scr, z_scr, c_scr, *, seq, seg):
    m = seq // 2
    tc = z_ref.shape[2]
    n_lane_blocks = tc // LANES
    short_conv = _short_conv_fn(seq, seg)
    row = lax.broadcasted_iota(jnp.int32, (m, 1), 0)
    alt = jnp.where(row % 2 == 0, 1.0, -1.0)
    gates = ((u1_ref, cw1_ref, cb1_ref), (u2_ref, cw2_ref, cb2_ref))
    z = short_conv(u0_ref, cw0_ref, cb0_ref)
    kb = min(m, FREQ_BLOCK)
    for n in range(HY_ORDER):
        for j in range(n_lane_blocks):
            z_scr[j] = z[:, j * LANES:(j + 1) * LANES]
        ze = jnp.concatenate([z_scr[j, pl.ds(0, m, stride=2), :] for j in range(n_lane_blocks)], axis=1)
        zo = jnp.concatenate([z_scr[j, pl.ds(1, m, stride=2), :] for j in range(n_lane_blocks)], axis=1)
        e_mid = jnp.sum(alt * ze, axis=0, keepdims=True)
        o_mid = jnp.sum(alt * zo, axis=0, keepdims=True)
        zeb = ze.astype(BF16)
        zob = zo.astype(BF16)
        for r in range(0, m, kb):
            rows = slice(r, r + kb)
            srows = slice(m + r, m + r + kb)
            ec = _dot(cs_ref[rows, :], zeb)
            es = _dot(cs_ref[srows, :], zeb)
            oc = _dot(cs_ref[rows, :], zob)
            os_ = _dot(cs_ref[srows, :], zob)
            c = tw_ref[0, rows, :]
            s = tw_ref[1, rows, :]
            pc = c * oc - s * os_
            ps = c * os_ + s * oc
            xca, xsa = ec + pc, es + ps
            xcb, xsb = ec - pc, ps - es
            har = h_ref[n, rows, :]
            hai = h_ref[n, srows, :]
            hbr = h_ref[n, 2 * m + r:2 * m + r + kb, :]
            hbi = h_ref[n, 3 * m + r:3 * m + r + kb, :]
            yar = xca * har + xsa * hai
            yai = xca * hai - xsa * har
            ybr = xcb * hbr + xsb * hbi
            ybi = xcb * hbi - xsb * hbr
            dr = yar - ybr
            di = yai + ybi
            g_scr[0, rows, :] = (yar + ybr).astype(BF16)
            g_scr[0, srows, :] = (ybi - yai).astype(BF16)
            g_scr[1, rows, :] = (c * dr - s * di).astype(BF16)
            g_scr[1, srows, :] = (-(s * dr + c * di)).astype(BF16)
        hr = hm_ref[n, 0:1, :]
        hi = hm_ref[n, 1:2, :]
        ymr = e_mid * hr + o_mid * hi
        ymi = e_mid * hi - o_mid * hr
        y_even = _dot(cs_ref[:m, :], g_scr[0, :m, :]) + _dot(cs_ref[m:, :], g_scr[0, m:, :]) + alt * ymr
        y_odd = _dot(cs_ref[:m, :], g_scr[1, :m, :]) + _dot(cs_ref[m:, :], g_scr[1, m:, :]) - alt * ymi
        for j in range(n_lane_blocks):
            c_scr[j, pl.ds(0, m, stride=2), :] = y_even[:, j * LANES:(j + 1) * LANES]
            c_scr[j, pl.ds(1, m, stride=2), :] = y_odd[:, j * LANES:(j + 1) * LANES]
        conv = jnp.concatenate([c_scr[j] for j in range(n_lane_blocks)], axis=1)
        z = short_conv(*gates[n]) * (conv + bias_ref[n] * z)
    z_ref[0] = z.astype(z_ref.dtype)


def _hy_conv(proj, conv_w, conv_b, cs, hspec, hl, hm, tw, bias, batch, seq, seg, col0, width, tc, split, bp,
             cast_srcs):
    proj3 = proj.reshape(batch, seq, proj.shape[-1])
    blk0 = col0 // tc
    per = width // tc
    n_b = batch // bp
    assert bp == 1 or not split
    once = dict(pipeline_mode=pl.Buffered(1))

    def u_spec(p):
        return pl.BlockSpec((bp, seq, tc), lambda c, b: (b, 0, blk0 + p * per + c))

    def w_spec(p, rows):
        return pl.BlockSpec((rows, tc), lambda c, b: (0, p * per + c))

    in_specs = [u_spec(0), u_spec(1), u_spec(2),
                w_spec(0, HY_SHORT), w_spec(1, HY_SHORT), w_spec(2, HY_SHORT),
                w_spec(0, 1), w_spec(1, 1), w_spec(2, 1),
                pl.BlockSpec(cs.shape, lambda c, b: (0, 0), **once),
                pl.BlockSpec((HY_ORDER, 2 * seq, tc), lambda c, b: (0, 0, c), **once)]
    bias_spec = pl.BlockSpec((HY_ORDER, 1, tc), lambda c, b: (0, 0, c))
    operands = [proj3, proj3, proj3, conv_w, conv_w, conv_w, conv_b, conv_b, conv_b, cs, hspec]
    if split:
        kern = functools.partial(_hyconv_split_kernel, seq=seq, seg=seg)
        in_specs += [pl.BlockSpec((HY_ORDER, 2, tc), lambda c, b: (0, 0, c)),
                     pl.BlockSpec((2, seq // 2, tc), lambda c, b: (0, 0, 0)), bias_spec]
        operands += [hm, tw, bias]
        scratch = [pltpu.VMEM((2, seq, tc), BF16), pltpu.VMEM((tc // LANES, seq, LANES), F32),
                   pltpu.VMEM((tc // LANES, seq, LANES), F32)]
    else:
        kern = functools.partial(_hyconv_kernel, seq=seq, seg=seg)
        in_specs += [pl.BlockSpec((HY_ORDER, 1, tc), lambda c, b: (0, 0, c)), bias_spec]
        operands += [hl, bias]
        scratch = [pltpu.VMEM((bp, 2 * seq, tc), BF16)]
    cast_specs, cast_shapes = _cast_rider_specs(cast_srcs, per * n_b, lambda c, b: c * n_b + b)
    kern = _with_cast_riders(kern, len(operands), 1, len(cast_srcs))
    return pl.pallas_call(
        kern,
        grid=(per, n_b),
        in_specs=in_specs + cast_specs,
        out_specs=[pl.BlockSpec((bp, seq, tc), lambda c, b: (b, 0, c))] + cast_specs,
        out_shape=[jax.ShapeDtypeStruct((batch, seq, width), BF16)] + cast_shapes,
        scratch_shapes=scratch,
        compiler_params=_cparams(("arbitrary", "arbitrary")),
        name="hy_conv",
    )(*operands, *cast_srcs)


def _mixout_kernel(*refs, gla_width, n_own, aliased):
    (yg_ref, zh_ref, x_ref, mod_ref, ghy_ref, gpost_ref, gpre_ref, wo_ref, wrh_ref, wrl_ref, br_ref,
     cnt_in_ref) = refs[:12]
    x1_ref, hg_ref, route_ref, cnt_ref, cnt_scr = refs[12 + int(aliased):]
    d = x_ref.shape[1]
    tm = x_ref.shape[0] // MIX_SUBTILES
    i = pl.program_id(0)

    @pl.when(i == 0)
    def _():
        cnt_scr[...] = cnt_in_ref[...]

    @pl.when(i < n_own)
    def _():
        for s in range(MIX_SUBTILES):
            _mixout_rows(slice(s * tm, (s + 1) * tm), tm, d, gla_width, yg_ref, zh_ref, x_ref, mod_ref, ghy_ref,
                         gpost_ref, gpre_ref, wo_ref, wrh_ref, wrl_ref, br_ref, x1_ref, hg_ref, route_ref, cnt_scr)

    @pl.when(i >= n_own)
    def _():
        hg_ref[...] = jnp.zeros_like(hg_ref)

    cnt_ref[...] = cnt_scr[...]


def _mixout_rows(rows, tm, d, gla_width, yg_ref, zh_ref, x_ref, mod_ref, ghy_ref, gpost_ref, gpre_ref, wo_ref,
                 wrh_ref, wrl_ref, br_ref, x1_ref, hg_ref, route_ref, cnt_scr):
    yh = _rms(zh_ref[rows, :].astype(F32), ghy_ref[...]).astype(BF16)
    y = _dot(yg_ref[rows, :], wo_ref[:gla_width, :]) + _dot(yh, wo_ref[gla_width:, :])
    x1 = x_ref[rows, :] + _rms(y, gpost_ref[...] * mod_ref[0, 2:3, :])
    x1_ref[rows, :] = x1
    h2 = _rms(x1, gpre_ref[...] * (1.0 + mod_ref[0, 4:5, :])) + mod_ref[0, 3:4, :]
    hh, hl = _split2(h2)
    hg_ref[rows, :d] = h2
    logits = _dot(hh, wrh_ref[...]) + (_dot(hh, wrl_ref[...]) + _dot(hl, wrh_ref[...])) + br_ref[...]

    lane = lax.broadcasted_iota(jnp.int32, logits.shape, 1).astype(F32)
    neg = -jnp.inf
    is_grp = (lane >= N_EXPERTS) & (lane < N_EXPERTS + N_GROUPS)
    m = jnp.max(jnp.where(is_grp, logits, neg), axis=1, keepdims=True)
    p_grp = 1.0 / jnp.sum(jnp.where(is_grp, jnp.exp(logits - m), 0.0), axis=1, keepdims=True)
    grp = jnp.min(jnp.where(is_grp & (logits == m), lane - N_EXPERTS, 1e9), axis=1, keepdims=True)
    sel = (lane >= grp * EXP_PER_GROUP) & (lane < (grp + 1.0) * EXP_PER_GROUP)
    me = jnp.max(jnp.where(sel, logits, neg), axis=1, keepdims=True)
    pe = jnp.where(sel, jnp.exp(logits - me), -1.0)
    v1 = jnp.max(pe, axis=1, keepdims=True)
    i1 = jnp.min(jnp.where(pe == v1, lane, 1e9), axis=1, keepdims=True)
    pe2 = jnp.where(lane == i1, -1.0, pe)
    v2 = jnp.max(pe2, axis=1, keepdims=True)
    i2 = jnp.min(jnp.where(pe2 == v2, lane, 1e9), axis=1, keepdims=True)
    den = v1 + v2
    gates = jnp.where(lane == i1, v1 / den, jnp.where(lane == i2, v2 / den, 0.0)) * p_grp

    lo = jnp.minimum(i1, i2) - grp * EXP_PER_GROUP
    hi = jnp.maximum(i1, i2) - grp * EXP_PER_GROUP
    bucket = grp * PAIRS_PER_GROUP + lo * (2 * EXP_PER_GROUP - 1 - lo) * 0.5 + (hi - lo - 1.0)
    onehot = lane == bucket
    r_i = lax.broadcasted_iota(jnp.int32, (tm, tm), 0)
    c_i = lax.broadcasted_iota(jnp.int32, (tm, tm), 1)
    earlier = _dot((r_i > c_i).astype(BF16), onehot.astype(BF16)) + cnt_scr[...]
    rank = jnp.sum(jnp.where(onehot, earlier, 0.0), axis=1, keepdims=True)
    cnt_scr[...] += jnp.sum(onehot.astype(F32), axis=0, keepdims=True)
    route = jnp.where(lane == ROUTE_BUCKET_LANE, bucket, jnp.where(lane == ROUTE_RANK_LANE, rank, gates))
    hg_ref[rows, d:] = route
    route_ref[:, rows] = route.T[ROUTE_BUCKET_LANE:ROUTE_BUCKET_LANE + SUBLANES, :]


def _mix_out(yg, zh, x, mod, mod_row, g_hy, g_post, g_pre, w_out, wr_hi, wr_lo, b_r, cnt_in, hg_all, row0, n_total, tm):
    n, d = x.shape
    gw = yg.shape[1]
    hw = zh.shape[1]
    aliased = hg_all is not None
    n_own = n // tm
    n_steps = n_own if aliased else n_total // tm
    assert row0 % tm == 0 and n_total % tm == 0 and (row0 + n == n_total if aliased else row0 == 0)
    kern = functools.partial(_mixout_kernel, gla_width=gw, n_own=n_own, aliased=aliased)
    row = lambda i: (jnp.minimum(i, n_own - 1), 0)
    fixed = lambda i: (0, 0)
    in_specs = [pl.BlockSpec((tm, gw), row),
                pl.BlockSpec((tm, hw), row),
                pl.BlockSpec((tm, d), row),
                pl.BlockSpec((1, 6, d), lambda i: (mod_row(jnp.minimum(i, n_own - 1) * tm), 0, 0)),
                pl.BlockSpec((1, hw), fixed),
                pl.BlockSpec((1, d), fixed),
                pl.BlockSpec((1, d), fixed),
                pl.BlockSpec((gw + hw, d), fixed),
                pl.BlockSpec((d, LANES), fixed),
                pl.BlockSpec((d, LANES), fixed),
                pl.BlockSpec((1, LANES), fixed),
                pl.BlockSpec((1, LANES), fixed)]
    operands = [yg, zh, x, mod, g_hy, g_post, g_pre, w_out, wr_hi, wr_lo, b_r, cnt_in]
    if aliased:
        in_specs.append(pl.BlockSpec(memory_space=pl.ANY))
        operands.append(hg_all)
    return pl.pallas_call(
        kern,
        grid=(n_steps,),
        in_specs=in_specs,
        out_specs=[pl.BlockSpec((tm, d), row),
                   pl.BlockSpec((tm, d + LANES), lambda i: (row0 // tm + i, 0)),
                   pl.BlockSpec((SUBLANES, tm), lambda i: (0, jnp.minimum(i, n_own - 1))),
                   pl.BlockSpec((1, LANES), fixed)],
        out_shape=[jax.ShapeDtypeStruct((n, d), F32),
                   jax.ShapeDtypeStruct((n_total, d + LANES), F32),
                   jax.ShapeDtypeStruct((SUBLANES, n), F32),
                   jax.ShapeDtypeStruct((1, LANES), F32)],
        scratch_shapes=[pltpu.VMEM((1, LANES), F32)],
        input_output_aliases={len(operands) - 1: 1} if aliased else {},
        compiler_params=_cparams(("arbitrary",)),
        name="mix_out",
    )(*operands)


def _row_gather_copy(src_hbm, row, buf, slot, r, sem):
    return pltpu.make_async_copy(src_hbm.at[pl.ds(row, 1), :], buf.at[slot, pl.ds(r, 1), :], sem.at[slot])


def _start_row_gather(idx_ref, base, src_hbm, buf, slot, sem, tm):
    def body(r, carry):
        _row_gather_copy(src_hbm, idx_ref[base + r], buf, slot, r, sem).start()
        return carry

    lax.fori_loop(0, tm, body, 0, unroll=8)


def _start_row_gather_inline(idx_ref, base, src_hbm, buf, slot, sem, tm):
    for r in range(tm):
        _row_gather_copy(src_hbm, idx_ref[base + r], buf, slot, r, sem).start()


def _wait_row_gather(src_hbm, buf, slot, sem, tm):
    pltpu.make_async_copy(src_hbm.at[pl.ds(0, tm), :], buf.at[slot], sem.at[slot]).wait()


def _moe_kernel(src_ref, ea_ref, eb_ref, valid_ref, hg_hbm,
                wga_ref, wua_ref, wda_ref, wgb_ref, wub_ref, wdb_ref, y_ref, buf, sem, *, tm, d):
    t = pl.program_id(0)
    slot = t % 2

    @pl.when(t == 0)
    def _():
        _start_row_gather(src_ref, 0, hg_hbm, buf, 0, sem, tm)

    def used_tile(slot):
        _wait_row_gather(hg_hbm, buf, slot, sem, tm)
        _start_row_gather_inline(src_ref, (t + 1) * tm, hg_hbm, buf, 1 - slot, sem, tm)
        h = buf[slot, :, :d].astype(BF16)
        gates = buf[slot, :, d:]
        lane = lax.broadcasted_iota(jnp.int32, gates.shape, 1)

        experts = ((wga_ref, wua_ref, wda_ref, ea_ref[t]), (wgb_ref, wub_ref, wdb_ref, eb_ref[t]))
        pre = [_dot(h, wg_ref[0]) for wg_ref, _, _, _ in experts]
        up = [_dot(h, wu_ref[0]) for _, wu_ref, _, _ in experts]
        he = [(_silu(pre[i]) * up[i]).astype(BF16) for i in range(2)]
        gate = [jnp.sum(jnp.where(lane == e, gates, 0.0), axis=1, keepdims=True) for _, _, _, e in experts]
        out = [gate[i] * _dot(he[i], experts[i][2][0]) for i in range(2)]
        y_ref[...] = out[0] + out[1]

    for parity in range(2):
        pl.when((valid_ref[t] == 1) & (slot == parity))(functools.partial(used_tile, parity))

    @pl.when(valid_ref[t] == 0)
    def _():
        @pl.when(valid_ref[jnp.maximum(t - 1, 0)] == 1)
        def _():
            _wait_row_gather(hg_hbm, buf, slot, sem, tm)

        y_ref[...] = jnp.zeros_like(y_ref)


def _moe(hg, src, ea, eb, valid, wg, wu, wd, n_tiles, tm):
    d = hg.shape[1] - LANES
    _, _, de = wg.shape
    kern = functools.partial(_moe_kernel, tm=tm, d=d)
    wa = lambda t, src, ea, eb, valid: (ea[t], 0, 0)
    wb = lambda t, src, ea, eb, valid: (eb[t], 0, 0)
    return pl.pallas_call(
        kern,
        grid_spec=pltpu.PrefetchScalarGridSpec(
            num_scalar_prefetch=4,
            grid=(n_tiles,),
            in_specs=[pl.BlockSpec(memory_space=pl.ANY),
                      pl.BlockSpec((1, d, de), wa), pl.BlockSpec((1, d, de), wa), pl.BlockSpec((1, de, d), wa),
                      pl.BlockSpec((1, d, de), wb), pl.BlockSpec((1, d, de), wb), pl.BlockSpec((1, de, d), wb)],
            out_specs=pl.BlockSpec((tm, d), lambda t, src, ea, eb, valid: (t, 0)),
            scratch_shapes=[pltpu.VMEM((2, tm, d + LANES), F32), pltpu.SemaphoreType.DMA((2,))]),
        out_shape=jax.ShapeDtypeStruct((n_tiles * tm, d), F32),
        compiler_params=_cparams(("arbitrary",)),
        name="moe",
    )(src, ea, eb, valid, hg, wg, wu, wd, wg, wu, wd)


def _ffn_out_kernel(pos_ref, y_hbm, x1_ref, mod_ref, gpost_ref, o_ref, buf, sem, *, tm):
    i = pl.program_id(0)
    n_i = pl.num_programs(0)
    slot = i % 2

    @pl.when(i == 0)
    def _():
        _start_row_gather(pos_ref, 0, y_hbm, buf, 0, sem, tm)

    _wait_row_gather(y_hbm, buf, slot, sem, tm)

    @pl.when(i + 1 < n_i)
    def _():
        _start_row_gather_inline(pos_ref, (i + 1) * tm, y_hbm, buf, 1 - slot, sem, tm)
        o_ref[...] = x1_ref[...] + _rms(buf[slot], gpost_ref[...] * mod_ref[0, 5:6, :])

    @pl.when(i + 1 == n_i)
    def _():
        o_ref[...] = x1_ref[...] + _rms(buf[slot], gpost_ref[...] * mod_ref[0, 5:6, :])


def _ffn_out(y_sorted, pos, x1, mod, mod_row, g_post, tm):
    n, d = x1.shape
    kern = functools.partial(_ffn_out_kernel, tm=tm)
    return pl.pallas_call(
        kern,
        grid_spec=pltpu.PrefetchScalarGridSpec(
            num_scalar_prefetch=1,
            grid=(n // tm,),
            in_specs=[pl.BlockSpec(memory_space=pl.ANY),
                      pl.BlockSpec((tm, d), lambda i, pos: (i, 0)),
                      pl.BlockSpec((1, 6, d), lambda i, pos: (mod_row(i * tm), 0, 0)),
                      pl.BlockSpec((1, d), lambda i, pos: (0, 0))],
            out_specs=pl.BlockSpec((tm, d), lambda i, pos: (i, 0)),
            scratch_shapes=[pltpu.VMEM((2, tm, d), F32), pltpu.SemaphoreType.DMA((2,))]),
        out_shape=jax.ShapeDtypeStruct((n, d), F32),
        compiler_params=_cparams(("arbitrary",)),
        name="ffn_out",
    )(pos, y_sorted, x1, mod, g_post)


def _inverse_permutation_kernel(pos_ref, init_hbm, src_ref, sem):
    init_copy = pltpu.make_async_copy(init_hbm, src_ref, sem)
    init_copy.start()
    init_copy.wait()

    def claim(t, carry):
        src_ref[pos_ref[t]] = t
        return carry

    lax.fori_loop(0, pos_ref.shape[0], claim, 0, unroll=16)


def _inverse_permutation(pos, n_pad):
    n = pos.shape[0]
    init = jnp.arange(n_pad, dtype=jnp.int32) % n
    return pl.pallas_call(
        _inverse_permutation_kernel,
        in_specs=[pl.BlockSpec(memory_space=pltpu.SMEM), pl.BlockSpec(memory_space=pl.ANY)],
        out_specs=pl.BlockSpec(memory_space=pltpu.SMEM),
        out_shape=jax.ShapeDtypeStruct((n_pad,), jnp.int32),
        scratch_shapes=[pltpu.SemaphoreType.DMA(())],
        name="inverse_permutation",
    )(pos, init)


def _route_tables(route, counts, n_tiles, tm):
    n_buckets = N_GROUPS * PAIRS_PER_GROUP
    cnt = counts[0, :n_buckets].astype(jnp.int32)
    padded = (cnt + tm - 1) // tm * tm
    ends = jnp.cumsum(padded)
    starts = ends - padded
    bucket = route[0].astype(jnp.int32)
    rank = route[ROUTE_RANK_LANE - ROUTE_BUCKET_LANE].astype(jnp.int32)
    is_bucket = bucket[:, None] == jnp.arange(n_buckets, dtype=jnp.int32)[None, :]
    pos = jnp.sum(jnp.where(is_bucket, starts[None, :], 0), axis=1) + rank
    src = _inverse_permutation(pos, n_tiles * tm)
    n_valid = ends[-1] // tm
    tile = jnp.arange(n_tiles, dtype=jnp.int32)
    used = jnp.minimum(tile, n_valid - 1)
    tile_bucket = jnp.sum((ends[None, :] <= (used * tm)[:, None]).astype(jnp.int32), axis=1)
    grp = tile_bucket // PAIRS_PER_GROUP
    pair = tile_bucket % PAIRS_PER_GROUP
    pair_lo = jnp.array([a for a in range(EXP_PER_GROUP) for b in range(a + 1, EXP_PER_GROUP)], jnp.int32)
    pair_hi = jnp.array([b for a in range(EXP_PER_GROUP) for b in range(a + 1, EXP_PER_GROUP)], jnp.int32)
    ea = grp * EXP_PER_GROUP + pair_lo[pair]
    eb = grp * EXP_PER_GROUP + pair_hi[pair]
    valid = (tile < n_valid).astype(jnp.int32)
    return pos, src, ea, eb, valid


def _dft_kernel(cs_ref, *, seq, tr):
    k = pl.program_id(0) * tr + lax.broadcasted_iota(jnp.int32, (tr, LANES), 0)
    j = lax.broadcasted_iota(jnp.int32, (tr, LANES), 1)
    period = 2 * seq

    ang = ((k * j) % period).astype(F32) * (math.pi / seq)
    c0 = jnp.cos(ang)
    s0 = jnp.sin(ang)
    ch = c0[:, LANES // 2:LANES // 2 + 1]
    sh = s0[:, LANES // 2:LANES // 2 + 1]
    c_step = 2.0 * ch * ch - 1.0
    s_step = 2.0 * sh * ch
    ca = jnp.ones_like(ch)
    sa = jnp.zeros_like(ch)
    for t1 in range(seq // LANES):
        cols = slice(t1 * LANES, (t1 + 1) * LANES)
        cs_ref[0, :, cols] = (ca * c0 - sa * s0).astype(cs_ref.dtype)
        cs_ref[1, :, cols] = (sa * c0 + ca * s0).astype(cs_ref.dtype)
        ca, sa = ca * c_step - sa * s_step, sa * c_step + ca * s_step


def _dft_cos_sin(seq, n_rows, tr):
    kern = functools.partial(_dft_kernel, seq=seq, tr=tr)
    return pl.pallas_call(
        kern,
        grid=(n_rows // tr,),
        out_specs=pl.BlockSpec((2, tr, seq), lambda r: (0, r, 0)),
        out_shape=jax.ShapeDtypeStruct((2, n_rows, seq), BF16),
        compiler_params=_cparams(("arbitrary",)),
        name="dft_matrix",
    )()


def _positional_features(seq):
    t = jnp.arange(seq, dtype=F32)
    t01 = t / max(seq - 1, 1)
    ang = 2.0 * math.pi * t / seq
    bands = jnp.linspace(1e-4, HY_BANDS - 1, HY_BANDS, dtype=F32)
    pe = jnp.concatenate([t01[:, None], jnp.cos(ang[:, None] * bands), -jnp.sin(ang[:, None] * bands)], axis=-1)
    return jnp.pad(pe, ((0, 0), (0, LANES - pe.shape[1])))


def _mixer_and_router(x3, mod, mod_row, s0f, s0b, n_seg, p, tiles, cnt_in, hg_all, row0, n_total, w_out, casts):
    batch, seq, d = x3.shape
    x = x3.reshape(batch * seq, d)
    dk, dv = p["dk"], p["dv"]
    hy_width = p["hy_width"]
    casted = {}

    proj, lr, *casted["premix"] = _premix_proj(x, mod, mod_row, p["g_pre_mix"], p["w_main"], p["w_lr"],
                                               tiles["tm_proj"], tiles["tn_proj"], casts["premix"])
    if w_out is None:
        w_out = casted["premix"][0]
    y_gla, s_f, s_b, *casted["gla"] = _gla(proj, lr, p["wdf"], p["bdf"], p["wdb"], p["bdb"], p["g_gla"], s0f, s0b,
                                           batch, seq, dk, dv, tiles["gla_heads_per_step"], casts["gla"])

    ad, hl, hm = _hy_filter(_positional_features(seq), p["hy_w1"], p["hy_b1"], p["hy_w2"], p["hy_b2"], p["hy_freq"],
                            p["hy_w3"], p["hy_deltas"], seq, hy_width, tiles["tc_filter"])
    tr, tc_conv = tiles["tr_spec"], tiles["tc_conv"]
    if tiles["hy_split"]:
        half = seq // 2
        hspec = _hy_spectrum(_dft_cos_sin(seq, half, tr), ad, seq, half, 4, hy_width, tr, tiles["tc_spec"])
        cs = _dft_cos_sin(half, half, tr).reshape(seq, half)
        ang = jnp.arange(half, dtype=F32) * (math.pi / seq)
        tw = jnp.broadcast_to(jnp.stack([jnp.cos(ang), jnp.sin(ang)])[:, :, None], (2, half, tc_conv))
    else:
        cs3 = _dft_cos_sin(seq, seq, tr)
        hspec = _hy_spectrum(cs3, ad, seq, seq, 2, hy_width, tr, tiles["tc_spec"])
        cs = cs3.reshape(2 * seq, seq)
        tw = None
    z_hy, *casted["hy_conv"] = _hy_conv(proj, p["hy_conv_w"], p["hy_conv_b"], cs, hspec, hl, hm, tw, p["hy_bias"],
                                        batch, seq, seq // n_seg, p["hy_col0"], hy_width, tc_conv, tiles["hy_split"],
                                        tiles["hy_batch_per_step"], casts["hy_conv"])

    x1, hg_all, route, counts = _mix_out(y_gla.reshape(batch * seq, -1), z_hy.reshape(batch * seq, -1), x, mod,
                                         mod_row, p["g_hy"], p["g_post_mix"], p["g_pre_ffn"], w_out, p["wr_hi"],
                                         p["wr_lo"], p["b_r"], cnt_in, hg_all, row0, n_total, tiles["tm_mix"])
    return x1, hg_all, route, counts, s_f, s_b, casted


def kernel(x_prompt, x_sample, c, state_gla_fwd, state_gla_bwd, c_ctx, w_ada, b_ada, g_pre_mix, g_post_mix, g_pre_ffn, g_post_ffn, w_in, w_dec_f, b_dec_f, w_dec_b, b_dec_b, g_gla, hy_conv_w, hy_conv_b, hy_w1, hy_b1, hy_w2, hy_b2, hy_w3, hy_freq, hy_bias, g_hy, w_out, w_router_grp, b_router_grp, w_router_exp, b_router_exp, w_exp_gate, w_exp_up, w_exp_down):
    depth = w_ada.shape[0]
    assert depth == 1
    l = 0
    d = x_prompt.shape[-1]
    dec_batch = x_sample.shape[0]
    heads = GLA_HEADS
    dk, dv = state_gla_fwd.shape[-2:]
    kdim = heads * dk
    gla_width = heads * dv
    hy_width = g_hy.shape[-1]
    hid = hy_w2.shape[-1]

    cond = jnp.concatenate([c_ctx[None, :], c], axis=0)
    cond = jnp.pad(cond, ((0, -cond.shape[0] % SUBLANES), (0, 0)))
    mod = _ada_mod(cond, w_ada[l], b_ada[l]).reshape(cond.shape[0], 6, d)

    n_main = 2 * kdim + 2 * gla_width
    w_main, w_lr = _repack_w_in(w_in, l, n_main, 2 * GLA_RANK, REPACK_COLS)

    def dec_weight(w, first_row):
        wh = w.reshape(GLA_RANK, heads, dk).transpose(1, 0, 2)
        return jnp.pad(wh, ((0, 0), (first_row, LANES - GLA_RANK - first_row), (0, 0)))

    deltas = jnp.abs(jnp.linspace(math.log(HY_TARGET) / HY_SLOW_PCT, math.log(HY_TARGET) / HY_FAST_PCT, hy_width,
                                  dtype=F32))
    w_r = jnp.pad(jnp.concatenate([w_router_exp[l], w_router_grp[l]], axis=1),
                  ((0, 0), (0, LANES - N_EXPERTS - N_GROUPS)))
    wr_hi, wr_lo = _split2(w_r)
    b_r = jnp.pad(jnp.concatenate([b_router_exp[l], b_router_grp[l]]), (0, LANES - N_EXPERTS - N_GROUPS))

    p = dict(
        dk=dk, dv=dv, hy_width=hy_width, hy_col0=n_main,
        g_pre_mix=g_pre_mix[l][None, :], g_post_mix=g_post_mix[l][None, :],
        g_pre_ffn=g_pre_ffn[l][None, :], g_post_ffn=g_post_ffn[l][None, :],
        w_main=w_main, w_lr=w_lr,
        wdf=dec_weight(w_dec_f[l], 0), bdf=b_dec_f[l][None, :],
        wdb=dec_weight(w_dec_b[l], GLA_RANK), bdb=b_dec_b[l][None, :],
        g_gla=g_gla[l][None, :],
        hy_conv_w=hy_conv_w[l], hy_conv_b=hy_conv_b[l][None, :],
        hy_w1=jnp.pad(hy_w1[l], ((0, LANES - hy_w1.shape[1]), (0, 0))), hy_b1=hy_b1[l][None, :],
        hy_w2=hy_w2[l], hy_b2=hy_b2[l][None, :], hy_freq=hy_freq[l][None, :],
        hy_w3=hy_w3[l].reshape(hid, 2 * HY_ORDER, hy_width).transpose(1, 0, 2), hy_deltas=deltas[None, :],
        hy_bias=hy_bias[l][:, None, :], g_hy=g_hy[l][None, :],
        wr_hi=wr_hi, wr_lo=wr_lo, b_r=b_r[None, :],
    )

    zero_state = jnp.zeros((1, heads, dk, dv), F32)
    dec_seq = x_sample.shape[1]
    tiles_p = _tile_plan(x_prompt.shape[1])
    tiles_s = _tile_plan(dec_seq)
    tm_moe = MXU_DIM
    tm_out = 512

    n_p = x_prompt.shape[0] * x_prompt.shape[1]
    n_s = dec_batch * dec_seq
    n_all = n_p + n_s
    mod_row_p = lambda r: 0
    mod_row_s = lambda r: 1 + r // dec_seq
    n_exp, _, d_exp = w_exp_gate.shape[1:]
    casts_p = dict(premix=[w_out[l]],
                   gla=[w_exp_gate[l].reshape(n_exp * d, d_exp), w_exp_down[l].reshape(n_exp * d_exp, d)],
                   hy_conv=[w_exp_up[l].reshape(n_exp * d, d_exp)])
    casts_s = dict(premix=[], gla=[], hy_conv=[])
    x1_p, hg_all, route_p, counts, s_f, s_b, casted = _mixer_and_router(
        x_prompt, mod, mod_row_p, zero_state, zero_state, 1, p, tiles_p, jnp.zeros((1, LANES), F32), None, 0, n_all,
        None, casts_p)
    (w_out_b,), (w_gate_b, w_down_b), (w_up_b,) = casted["premix"], casted["gla"], casted["hy_conv"]
    x1_s, hg_all, route_s, counts, _, _, _ = _mixer_and_router(
        x_sample, mod, mod_row_s, state_gla_fwd[:, l], state_gla_bwd[:, l], dec_seq // GRID_W, p, tiles_s,
        counts, hg_all, n_p, n_all, w_out_b, casts_s)

    n_tiles = n_all // tm_moe + N_GROUPS * PAIRS_PER_GROUP
    pos, src, ea, eb, valid = _route_tables(jnp.concatenate([route_p, route_s], axis=1), counts, n_tiles, tm_moe)
    y_sorted = _moe(hg_all, src, ea, eb, valid, w_gate_b.reshape(n_exp, d, d_exp), w_up_b.reshape(n_exp, d, d_exp),
                    w_down_b.reshape(n_exp, d_exp, d), n_tiles, tm_moe)
    y_p = _ffn_out(y_sorted, pos[:n_p], x1_p, mod, mod_row_p, p["g_post_ffn"], tm_out)
    y_s = _ffn_out(y_sorted, pos[n_p:], x1_s, mod, mod_row_s, p["g_post_ffn"], tm_out)
    return (y_p.reshape(x_prompt.shape), y_s.reshape(x_sample.shape),
            s_f[:, None].astype(x_prompt.dtype), s_b[:, None].astype(x_prompt.dtype))
```
